```python
import math
import jax, jax.numpy as jnp
from jax import lax
import numpy as np

D_MODEL = 1024
BATCH = 4
SEQ = 8192
DEPTH = 2

HEAD_DIM = 64
FOX_HEADS = 8
RET_HEADS = 8
FOX_WIDTH = FOX_HEADS * HEAD_DIM
RET_WIDTH = RET_HEADS * HEAD_DIM
EVEN_MIX_WIDTH = FOX_WIDTH + RET_WIDTH
FOX_Q_BLOCK = 128
RET_CHUNK = 128
EVEN_SIZES = (FOX_WIDTH, FOX_WIDTH, FOX_WIDTH, FOX_HEADS, RET_WIDTH, RET_WIDTH, RET_WIDTH, EVEN_MIX_WIDTH)
EVEN_IN = sum(EVEN_SIZES)
EVEN_SPLITS = tuple(int(v) for v in np.cumsum(EVEN_SIZES)[:-1])
NSA_HEADS = 16
NSA_KV_GROUPS = 4
NSA_HEADS_PER_GROUP = NSA_HEADS // NSA_KV_GROUPS
NSA_WIDTH = NSA_HEADS * HEAD_DIM
NSA_KV_WIDTH = NSA_KV_GROUPS * HEAD_DIM
N_BRANCH = 3
CMP_BLOCK = 32
CMP_STRIDE = 16
CMP_HIDDEN = 256
SLC_BLOCK = 64
SLC_TOPK = 16
WINDOW = 512
NSA_Q_BLOCK = 64
ODD_SIZES = (NSA_WIDTH,) + (NSA_KV_WIDTH,) * 6 + (NSA_HEADS * N_BRANCH, NSA_WIDTH)
ODD_IN = sum(ODD_SIZES)
ODD_SPLITS = tuple(int(v) for v in np.cumsum(ODD_SIZES)[:-1])

RMS_EPS = 1e-6
GN_EPS = 1e-5
NEG = -1e30
FORCE_BONUS = 1e6

kernel_name = "fox_retnet_nsa_hybrid"


def rms_norm(x, g):
    xf = x.astype(jnp.float32)
    y = xf * lax.rsqrt(jnp.mean(xf * xf, axis=-1, keepdims=True) + RMS_EPS)
    return (y * g.astype(jnp.float32)).astype(x.dtype)


def masked_softmax(s, mask):
    s = jnp.where(mask, s, NEG)
    p = jax.nn.softmax(s, axis=-1)
    return jnp.where(mask, p, 0.0)


def alibi_slopes(n):
    return jnp.asarray(2.0 ** (-8.0 * (np.arange(n) + 1) / n), jnp.float32)


def fox_attention(q, k, v, f_logit):
    B, S, H, d = q.shape
    nb = S // FOX_Q_BLOCK
    scale = d ** -0.5
    c = jnp.cumsum(jax.nn.log_sigmoid(f_logit.astype(jnp.float32)), axis=1)
    c_h = c.transpose(0, 2, 1)
    kh = k.transpose(0, 2, 1, 3)
    vh = v.transpose(0, 2, 1, 3)
    qb = q.reshape(B, nb, FOX_Q_BLOCK, H, d).transpose(1, 0, 3, 2, 4)
    cb = c.reshape(B, nb, FOX_Q_BLOCK, H).transpose(1, 0, 3, 2)
    key_pos = jnp.arange(S)

    def block(args):
        qi, ci, bi = args
        t = bi * FOX_Q_BLOCK + jnp.arange(FOX_Q_BLOCK)
        s = jnp.einsum('bhqd,bhkd->bhqk', qi, kh).astype(jnp.float32) * scale
        s = s + ci[..., None] - c_h[:, :, None, :]
        p = masked_softmax(s, key_pos[None, :] <= t[:, None])
        return jnp.einsum('bhqk,bhkd->bhqd', p.astype(vh.dtype), vh)

    o = lax.map(block, (qb, cb, jnp.arange(nb)))
    return o.transpose(1, 0, 3, 2, 4).reshape(B, S, H, d)


def retention_decays(n_heads, chunk):
    lg = np.log(1.0 - 2.0 ** (-5.0 - np.arange(n_heads)))
    i = np.arange(chunk)
    diff = i[:, None] - i[None, :]
    inner = np.where(diff[None] >= 0, np.exp(lg[:, None, None] * np.maximum(diff, 0)[None]), 0.0)
    cross = np.exp(lg[:, None] * (i[None, :] + 1))
    kdec = np.exp(lg[:, None] * (chunk - 1 - i)[None, :])
    cdec = np.exp(lg * chunk)
    return (jnp.asarray(inner, jnp.float32), jnp.asarray(cross, jnp.float32),
            jnp.asarray(kdec, jnp.float32), jnp.asarray(cdec, jnp.float32))


def retention(q, k, v):
    B, S, H, d = q.shape
    n = S // RET_CHUNK
    inner, cross, kdec, cdec = retention_decays(H, RET_CHUNK)

    def chunks(a):
        return a.astype(jnp.float32).reshape(B, n, RET_CHUNK, H, d).transpose(1, 0, 3, 2, 4)

    qc, kc, vc = chunks(q), chunks(k) * (d ** -0.5), chunks(v)

    def step(state, xs):
        qi, ki, vi = xs
        a = jnp.einsum('bhid,bhjd->bhij', qi, ki) * inner
        o = jnp.einsum('bhij,bhje->bhie', a, vi) + jnp.einsum('bhid,bhde->bhie', qi, state) * cross[None, :, :, None]
        state = state * cdec[None, :, None, None] + jnp.einsum('bhjd,bhje->bhde', ki * kdec[None, :, :, None], vi)
        return state, o

    _, o = lax.scan(step, jnp.zeros((B, H, d, d), jnp.float32), (qc, kc, vc))
    o = o.transpose(1, 0, 3, 2, 4).reshape(B, S, H, d)
    mu = jnp.mean(o, axis=-1, keepdims=True)
    var = jnp.mean(jnp.square(o - mu), axis=-1, keepdims=True)
    return ((o - mu) * lax.rsqrt(var + GN_EPS)).reshape(B, S, H * d)


def even_layer(x, norm_g, w_in, b_f, gn_g, w_out):
    B, S, _ = x.shape
    h = rms_norm(x, norm_g)
    u = h @ w_in
    q_f, k_f, v_f, f_l, q_r, k_r, v_r, z = jnp.split(u, EVEN_SPLITS, axis=-1)
    hd = lambda a, n: a.reshape(B, S, n, HEAD_DIM)
    o_f = fox_attention(hd(q_f, FOX_HEADS), hd(k_f, FOX_HEADS), hd(v_f, FOX_HEADS), f_l + b_f)
    o_f = o_f.reshape(B, S, FOX_WIDTH)
    o_r = (retention(hd(q_r, RET_HEADS), hd(k_r, RET_HEADS), hd(v_r, RET_HEADS)) * gn_g).astype(x.dtype)
    y = jnp.concatenate([o_f, o_r], axis=-1) * jax.nn.silu(z)
    return x + y @ w_out


def compress_blocks(a, idx, pe, w1, w2):
    B = a.shape[0]
    nc, l = idx.shape
    ab = a[:, idx] + pe[None, None, :, None, :]
    ab = ab.transpose(0, 1, 3, 2, 4).reshape(B, nc, NSA_KV_GROUPS, l * HEAD_DIM)
    return jax.nn.silu(ab @ w1) @ w2


def nsa_attention(q, kc, vc, ks, vs, kw, vw, gate_logit, pe_k, pe_v, wk1, wk2, wv1, wv2):
    B, S, H, d = q.shape
    G, Hg, QB = NSA_KV_GROUPS, NSA_HEADS_PER_GROUP, NSA_Q_BLOCK
    scale = d ** -0.5
    dtype = q.dtype
    nc = (S - CMP_BLOCK) // CMP_STRIDE + 1
    idx = np.arange(nc)[:, None] * CMP_STRIDE + np.arange(CMP_BLOCK)[None, :]
    cmp_end = jnp.asarray(idx[:, -1], jnp.int32)
    k_cmp = compress_blocks(kc, idx, pe_k, wk1, wk2)
    v_cmp = compress_blocks(vc, idx, pe_v, wv1, wv2)
    ns = S // SLC_BLOCK
    topk = min(SLC_TOPK, ns)
    c0 = np.arange(nc)[:, None] * CMP_STRIDE
    s0 = np.arange(ns)[None, :] * SLC_BLOCK
    overlap = np.clip(np.minimum(c0 + CMP_BLOCK, s0 + SLC_BLOCK) - np.maximum(c0, s0), 0, None)
    cmp_to_slc = jnp.asarray(overlap / CMP_STRIDE, jnp.float32)
    slopes = alibi_slopes(H).reshape(G, Hg)[None, :, :, None, None]
    ks_g = ks.transpose(0, 2, 1, 3)
    vs_g = vs.transpose(0, 2, 1, 3)
    kw_pad = jnp.pad(kw, ((0, 0), (WINDOW, 0), (0, 0), (0, 0)))
    vw_pad = jnp.pad(vw, ((0, 0), (WINDOW, 0), (0, 0), (0, 0)))
    nqb = S // QB
    qb = q.reshape(B, nqb, QB, G, Hg, d).transpose(1, 0, 2, 3, 4, 5)
    gb = gate_logit.reshape(B, nqb, QB, G, Hg, N_BRANCH).transpose(1, 0, 2, 3, 4, 5)
    b_idx = jnp.arange(B)[:, None, None, None]
    g_idx = jnp.arange(G)[None, :, None, None]
    blk = jnp.arange(ns)

    def block(args):
        qi, gi, bi = args
        t = bi * QB + jnp.arange(QB)
        s = jnp.einsum('bqghd,bcgd->bghqc', qi, k_cmp).astype(jnp.float32) * scale
        s = s - slopes * (t[:, None] - cmp_end[None, :]).astype(jnp.float32)
        p_cmp = masked_softmax(s, cmp_end[None, :] <= t[:, None])
        o_cmp = jnp.einsum('bghqc,bcgd->bqghd', p_cmp.astype(dtype), v_cmp)
        imp = p_cmp.sum(axis=2) @ cmp_to_slc
        cur = t // SLC_BLOCK
        valid = blk[None, :] * SLC_BLOCK <= t[:, None]
        forced = (blk[None, :] == 0) | (blk[None, :] == cur[:, None]) | (blk[None, :] == cur[:, None] - 1)
        score = jnp.where(valid, imp + jnp.where(forced, FORCE_BONUS, 0.0), NEG)
        _, sel = lax.top_k(score, topk)
        tok = (sel[..., None] * SLC_BLOCK + jnp.arange(SLC_BLOCK)).reshape(B, G, QB, topk * SLC_BLOCK)
        k_sel = ks_g[b_idx, g_idx, tok]
        v_sel = vs_g[b_idx, g_idx, tok]
        s = jnp.einsum('bqghd,bgqld->bghql', qi, k_sel).astype(jnp.float32) * scale
        s = s - slopes * (t[None, None, :, None] - tok).astype(jnp.float32)[:, :, None]
        p = masked_softmax(s, (tok <= t[None, None, :, None])[:, :, None])
        o_slc = jnp.einsum('bghql,bgqld->bqghd', p.astype(dtype), v_sel)
        start = bi * QB
        k_win = lax.dynamic_slice_in_dim(kw_pad, start, QB + WINDOW, axis=1)
        v_win = lax.dynamic_slice_in_dim(vw_pad, start, QB + WINDOW, axis=1)
        sp = start - WINDOW + jnp.arange(QB + WINDOW)
        wmask = (sp[None, :] <= t[:, None]) & (sp[None, :] > t[:, None] - WINDOW) & (sp[None, :] >= 0)
        s = jnp.einsum('bqghd,bkgd->bghqk', qi, k_win).astype(jnp.float32) * scale
        s = s - slopes * (t[:, None] - sp[None, :]).astype(jnp.float32)
        p = masked_softmax(s, wmask)
        o_win = jnp.einsum('bghqk,bkgd->bqghd', p.astype(dtype), v_win)
        g = jax.nn.sigmoid(gi.astype(jnp.float32)).astype(dtype)
        return g[..., 0:1] * o_cmp + g[..., 1:2] * o_slc + g[..., 2:3] * o_win

    o = lax.map(block, (qb, gb, jnp.arange(nqb)))
    return o.transpose(1, 0, 2, 3, 4, 5).reshape(B, S, H * d)


def odd_layer(x, norm_g, w_in, b_gate, pe_k, pe_v, wk1, wk2, wv1, wv2, w_out):
    B, S, _ = x.shape
    h = rms_norm(x, norm_g)
    u = h @ w_in
    q, kc, vc, ks, vs, kw, vw, gl, z = jnp.split(u, ODD_SPLITS, axis=-1)
    kv = lambda a: a.reshape(B, S, NSA_KV_GROUPS, HEAD_DIM)
    gl = (gl + b_gate).reshape(B, S, NSA_HEADS, N_BRANCH)
    o = nsa_attention(q.reshape(B, S, NSA_HEADS, HEAD_DIM), kv(kc), kv(vc), kv(ks), kv(vs), kv(kw), kv(vw),
                      gl, pe_k, pe_v, wk1, wk2, wv1, wv2)
    return x + (o * jax.nn.silu(z)) @ w_out


def setup_inputs(seed: int = 0) -> dict:
    key = jax.random.key(seed)
    ks = jax.random.split(key, 24)
    ne = (DEPTH + 1) // 2
    no = DEPTH // 2
    nrm = lambda k, shape, sc: jax.random.normal(k, shape, jnp.float32) * sc
    fan_cmp = CMP_BLOCK * HEAD_DIM
    return {
        "x": nrm(ks[0], (BATCH, SEQ, D_MODEL), 1.0),
        "even_norm_g": 1.0 + nrm(ks[1], (ne, D_MODEL), 0.05),
        "even_w_in": nrm(ks[2], (ne, D_MODEL, EVEN_IN), D_MODEL ** -0.5),
        "even_b_f": 1.0 + nrm(ks[3], (ne, FOX_HEADS), 0.1),
        "even_gn_g": 1.0 + nrm(ks[4], (ne, RET_WIDTH), 0.05),
        "even_w_out": nrm(ks[5], (ne, EVEN_MIX_WIDTH, D_MODEL), EVEN_MIX_WIDTH ** -0.5),
        "odd_norm_g": 1.0 + nrm(ks[6], (no, D_MODEL), 0.05),
        "odd_w_in": nrm(ks[7], (no, D_MODEL, ODD_IN), D_MODEL ** -0.5),
        "odd_b_gate": nrm(ks[8], (no, NSA_HEADS * N_BRANCH), 0.1),
        "odd_pe_k": nrm(ks[9], (no, CMP_BLOCK, HEAD_DIM), 0.1),
        "odd_pe_v": nrm(ks[10], (no, CMP_BLOCK, HEAD_DIM), 0.1),
        "odd_wk1": nrm(ks[11], (no, fan_cmp, CMP_HIDDEN), fan_cmp ** -0.5),
        "odd_wk2": nrm(ks[12], (no, CMP_HIDDEN, HEAD_DIM), CMP_HIDDEN ** -0.5),
        "odd_wv1": nrm(ks[13], (no, fan_cmp, CMP_HIDDEN), fan_cmp ** -0.5),
        "odd_wv2": nrm(ks[14], (no, CMP_HIDDEN, HEAD_DIM), CMP_HIDDEN ** -0.5),
        "odd_w_out": nrm(ks[15], (no, NSA_WIDTH, D_MODEL), NSA_WIDTH ** -0.5),
        "final_g": 1.0 + nrm(ks[16], (D_MODEL,), 0.05),
    }


def reference(x, even_norm_g, even_w_in, even_b_f, even_gn_g, even_w_out,
              odd_norm_g, odd_w_in, odd_b_gate, odd_pe_k, odd_pe_v, odd_wk1, odd_wk2, odd_wv1, odd_wv2, odd_w_out,
              final_g):
    for layer in range(DEPTH):
        i = layer // 2
        if layer % 2 == 0:
            x = even_layer(x, even_norm_g[i], even_w_in[i], even_b_f[i], even_gn_g[i], even_w_out[i])
        else:
            x = odd_layer(x, odd_norm_g[i], odd_w_in[i], odd_b_gate[i], odd_pe_k[i], odd_pe_v[i],
                          odd_wk1[i], odd_wk2[i], odd_wv1[i], odd_wv2[i], odd_w_out[i])
    return rms_norm(x, final_g)
```

```python
import functools

import jax
import jax.numpy as jnp
import numpy as np
from jax import lax
from jax.experimental import pallas as pl
from jax.experimental.pallas import tpu as pltpu

D_MODEL = 1024
HEAD_DIM = 64
LANES = 128
FOX_HEADS = 8
RET_HEADS = 8
FOX_WIDTH = FOX_HEADS * HEAD_DIM
RET_WIDTH = RET_HEADS * HEAD_DIM
RET_CHUNK = 128
NSA_HEADS = 16
NSA_GROUPS = 4
NSA_HPG = NSA_HEADS // NSA_GROUPS
NSA_WIDTH = NSA_HEADS * HEAD_DIM
NSA_KV_WIDTH = NSA_GROUPS * HEAD_DIM
N_BRANCH = 3
CMP_BLOCK = 32
CMP_STRIDE = 16
CMP_HIDDEN = 256
SLC_BLOCK = 64
SLC_TOPK = 16
WINDOW = 512
RMS_EPS = 1e-6
GN_EPS = 1e-5
NEG = -1e30
FORCE_BONUS = 1e6
MASK_BIG = 2.0 ** 100

PROJ_TM = 512
PROJ_TN = 512
FOX_TQ = 256
FOX_TK = 256
NSA_TQ = 128
SLC_TK = 256
WIN_TK = 128
GATE_TM = 512
OUT_TM = 512
VMEM_LIMIT = 48 * 1024 * 1024

F32 = jnp.float32
BF16 = jnp.bfloat16


def _cp(sem, vmem=VMEM_LIMIT):
    return pltpu.CompilerParams(dimension_semantics=sem, vmem_limit_bytes=vmem)


def _dot(a, b):
    return jnp.dot(a, b, preferred_element_type=F32)


def _dot_nt(a, b):
    return lax.dot_general(a, b, (((1,), (1,)), ((), ())), preferred_element_type=F32)


def _dot_tn(a, b):
    return lax.dot_general(a, b, (((0,), (0,)), ((), ())), preferred_element_type=F32)


def _rms(x, g):
    return x * lax.rsqrt(jnp.mean(x * x, axis=-1, keepdims=True) + RMS_EPS) * g


def _silu(x):
    return x * (1.0 / (1.0 + jnp.exp(-x)))


def _sigmoid(x):
    return 1.0 / (1.0 + jnp.exp(-x))


def _low_half(shape, axis):
    return lax.broadcasted_iota(jnp.int32, shape, axis) < HEAD_DIM


def _fgate_kernel(x_ref, g_ref, wf_ref, wft_ref, bcol_ref, brow_ref, ccol_ref, crow_ref,
                  carry_col, carry_row):
    si = pl.program_id(1)

    @pl.when(si == 0)
    def _():
        carry_col[...] = jnp.zeros_like(carry_col)
        carry_row[...] = jnp.zeros_like(carry_row)

    h = _rms(x_ref[0], g_ref[...])
    t = h.shape[0]
    hp = lax.Precision.HIGHEST

    def logsig(f):
        return jnp.minimum(f, 0.0) - jnp.log(1.0 + jnp.exp(-jnp.abs(f)))

    r = lax.broadcasted_iota(jnp.int32, (t, t), 0)
    c = lax.broadcasted_iota(jnp.int32, (t, t), 1)
    f_col = jnp.dot(h, wf_ref[...], precision=hp, preferred_element_type=F32) + brow_ref[...]
    lower = (c <= r).astype(F32)
    c_col = jnp.dot(lower, logsig(f_col), precision=hp, preferred_element_type=F32) + carry_row[...]
    carry_row[...] = c_col[t - 1:t, :]
    for j in range(FOX_HEADS // 2):
        ccol_ref[0, j] = c_col[:, 2 * j:2 * j + 2]
    f_row = lax.dot_general(wft_ref[...], h, (((1,), (1,)), ((), ())), precision=hp,
                            preferred_element_type=F32) + bcol_ref[...]
    upper = (r <= c).astype(F32)
    c_row = jnp.dot(logsig(f_row), upper, precision=hp, preferred_element_type=F32) + carry_col[...]
    carry_col[...] = c_row[:, t - 1:t]
    for j in range(FOX_HEADS // 2):
        crow_ref[0, j] = c_row[2 * j:2 * j + 2, :]


def _fgate(x, g, wf, b_f, *, tile):
    B, S, D = x.shape
    npair = FOX_HEADS // 2
    return pl.pallas_call(
        _fgate_kernel,
        grid=(B, S // tile),
        in_specs=[
            pl.BlockSpec((1, tile, D), lambda b, s: (b, s, 0)),
            pl.BlockSpec((1, D), lambda b, s: (0, 0)),
            pl.BlockSpec((D, FOX_HEADS), lambda b, s: (0, 0)),
            pl.BlockSpec((FOX_HEADS, D), lambda b, s: (0, 0)),
            pl.BlockSpec((FOX_HEADS, 1), lambda b, s: (0, 0)),
            pl.BlockSpec((1, FOX_HEADS), lambda b, s: (0, 0)),
        ],
        out_specs=[
            pl.BlockSpec((1, npair, tile, 2), lambda b, s: (b, 0, s, 0)),
            pl.BlockSpec((1, npair, 2, tile), lambda b, s: (b, 0, 0, s)),
        ],
        out_shape=[
            jax.ShapeDtypeStruct((B, npair, S, 2), F32),
            jax.ShapeDtypeStruct((B, npair, 2, S), F32),
        ],
        scratch_shapes=[pltpu.VMEM((FOX_HEADS, 1), F32), pltpu.VMEM((1, FOX_HEADS), F32)],
        compiler_params=_cp(("parallel", "arbitrary")),
        name="fgate",
    )(x, g.reshape(1, D), wf, wf.T, b_f.reshape(FOX_HEADS, 1), b_f.reshape(1, FOX_HEADS))


def _proj_kernel(x_ref, g_ref, w_ref, *rest, split_tile):
    if split_tile is None:
        u_ref, h_sc = rest
    else:
        u_ref, e0_ref, e1_ref, h_sc = rest
    j = pl.program_id(1)

    @pl.when(j == 0)
    def _():
        h_sc[...] = _rms(x_ref[...], g_ref[...]).astype(BF16)

    acc = _dot(h_sc[...], w_ref[...])
    u_ref[...] = acc.astype(u_ref.dtype)
    if split_tile is not None:
        @pl.when(j == split_tile)
        def _():
            half = acc.shape[1] // 2
            e0_ref[...] = acc[:, :half].astype(e0_ref.dtype)
            e1_ref[...] = acc[:, half:].astype(e1_ref.dtype)


def _proj(x2, g, w, *, out_dtype=BF16, split_tile=None, tm=PROJ_TM, tn=PROJ_TN):
    N, D = x2.shape
    W = w.shape[1]
    out_shape = [jax.ShapeDtypeStruct((N, W), out_dtype)]
    out_specs = [pl.BlockSpec((tm, tn), lambda i, j: (i, j))]
    if split_tile is not None:
        out_shape += [jax.ShapeDtypeStruct((N, tn // 2), out_dtype)] * 2
        out_specs += [pl.BlockSpec((tm, tn // 2), lambda i, j: (i, 0))] * 2
    res = pl.pallas_call(
        functools.partial(_proj_kernel, split_tile=split_tile),
        grid=(N // tm, W // tn),
        in_specs=[
            pl.BlockSpec((tm, D), lambda i, j: (i, 0)),
            pl.BlockSpec((1, D), lambda i, j: (0, 0)),
            pl.BlockSpec((D, tn), lambda i, j: (0, j)),
        ],
        out_specs=out_specs,
        out_shape=out_shape,
        scratch_shapes=[pltpu.VMEM((tm, D), BF16)],
        compiler_params=_cp(("parallel", "arbitrary")),
        name="proj",
    )(x2, g.reshape(1, D), w)
    return res[0] if split_tile is None else res


def _fox_kernel(q_ref, k_ref, v_ref, ccol_ref, crow_ref, o_ref, acc_sc):
    qi = pl.program_id(2)
    tq, tk = FOX_TQ, FOX_TK
    q = q_ref[0]
    low = _low_half(q.shape, 1)
    zero = jnp.zeros_like(q)
    qh = (jnp.where(low, q, zero), jnp.where(low, zero, q))
    ct = ccol_ref[0, 0]
    cth = (ct[:, 0:1], ct[:, 1:2])
    acc_sc[...] = jnp.zeros_like(acc_sc)
    row = qi * tq + lax.broadcasted_iota(jnp.int32, (tq, tk), 0)
    coli = lax.broadcasted_iota(jnp.int32, (tq, tk), 1)

    def step(kb, carry, masked):
        k0 = pl.multiple_of(kb * tk, tk)
        k = k_ref[0, pl.ds(k0, tk), :]
        v = v_ref[0, pl.ds(k0, tk), :]
        cs = crow_ref[0, 0, :, pl.ds(k0, tk)]
        out = []
        for h in range(2):
            m, l = carry[2 * h], carry[2 * h + 1]
            s = _dot_nt(qh[h], k) + (cth[h] - cs[h:h + 1, :])
            if masked:
                s = jnp.where(k0 + coli <= row, s, NEG)
            m_new = jnp.maximum(m, jnp.max(s, axis=-1, keepdims=True))
            p = jnp.exp(s - m_new)
            alpha = jnp.exp(m - m_new)
            l_new = alpha * l + jnp.sum(p, axis=-1, keepdims=True)
            acc_sc[h] = alpha * acc_sc[h] + _dot(p.astype(BF16), v)
            out += [m_new, l_new]
        return tuple(out)

    init = (jnp.full((tq, 1), NEG, F32), jnp.zeros((tq, 1), F32)) * 2
    nfull = (qi * tq) // tk
    carry = lax.fori_loop(0, nfull, lambda kb, c: step(kb, c, False), init)
    ndiag = tq // tk
    for d in range(ndiag):
        carry = step(nfull + d, carry, True)
    o_a = acc_sc[0] / carry[1]
    o_b = acc_sc[1] / carry[3]
    o_ref[0] = jnp.where(low, o_a, o_b)


def _fox(u3, ccol, crow, *, q_col, k_col, v_col):
    B, S, _ = u3.shape
    npair = FOX_HEADS // 2
    return pl.pallas_call(
        _fox_kernel,
        grid=(B, npair, S // FOX_TQ),
        in_specs=[
            pl.BlockSpec((1, FOX_TQ, LANES), lambda b, j, i: (b, i, q_col + j)),
            pl.BlockSpec((1, S, LANES), lambda b, j, i: (b, 0, k_col + j)),
            pl.BlockSpec((1, S, LANES), lambda b, j, i: (b, 0, v_col + j)),
            pl.BlockSpec((1, 1, FOX_TQ, 2), lambda b, j, i: (b, j, i, 0)),
            pl.BlockSpec((1, 1, 2, S), lambda b, j, i: (b, j, 0, 0)),
        ],
        out_specs=pl.BlockSpec((1, FOX_TQ, LANES), lambda b, j, i: (b, i, j)),
        out_shape=jax.ShapeDtypeStruct((B, S, FOX_WIDTH), F32),
        scratch_shapes=[pltpu.VMEM((2, FOX_TQ, LANES), F32)],
        compiler_params=_cp(("parallel", "parallel", "arbitrary")),
        name="fox",
    )(u3, u3, u3, ccol, crow)


def _ret_kernel(q_ref, k_ref, v_ref, inner_ref, cross_ref, kdec_ref, cd_ref, bd_ref, gn_ref,
                o_ref, state_sc):
    @pl.when(pl.program_id(2) == 0)
    def _():
        state_sc[...] = jnp.zeros_like(state_sc)

    q, k, v = q_ref[0], k_ref[0], v_ref[0]
    low = _low_half(q.shape, 1)
    zero = jnp.zeros_like(q)
    qa, qb = jnp.where(low, q, zero), jnp.where(low, zero, q)
    pa = (_dot_nt(qa, k) * inner_ref[0, 0]).astype(BF16)
    pb = (_dot_nt(qb, k) * inner_ref[0, 1]).astype(BF16)
    o_in = jnp.where(low, _dot(pa, v), _dot(pb, v))
    state = state_sc[...]
    o = o_in + _dot(q, state.astype(BF16)) * cross_ref[0]
    kd = (k.astype(F32) * kdec_ref[0]).astype(BF16)
    state_sc[...] = state * cd_ref[0] + _dot_tn(kd, v) * bd_ref[...]
    inv = 1.0 / HEAD_DIM
    sa = jnp.sum(jnp.where(low, o, 0.0), axis=-1, keepdims=True)
    st = jnp.sum(o, axis=-1, keepdims=True)
    mu = jnp.where(low, sa, st - sa) * inv
    d = o - mu
    d2 = d * d
    va = jnp.sum(jnp.where(low, d2, 0.0), axis=-1, keepdims=True)
    vt = jnp.sum(d2, axis=-1, keepdims=True)
    var = jnp.where(low, va, vt - va) * inv
    o_ref[0] = d * lax.rsqrt(var + GN_EPS) * gn_ref[...]


def _ret_constants():
    lg = np.log(1.0 - 2.0 ** (-5.0 - np.arange(RET_HEADS)))
    i = np.arange(RET_CHUNK)
    diff = i[:, None] - i[None, :]
    inner = np.where(diff[None] >= 0, np.exp(lg[:, None, None] * np.maximum(diff, 0)[None]), 0.0)
    cross = np.exp(lg[:, None] * (i[None, :] + 1))
    kdec = np.exp(lg[:, None] * (RET_CHUNK - 1 - i)[None, :])
    cdec = np.exp(lg * RET_CHUNK)
    npair = RET_HEADS // 2
    inner = inner.reshape(npair, 2, RET_CHUNK, RET_CHUNK)

    def lanes(a):
        a = a.reshape(npair, 2, RET_CHUNK)
        return np.repeat(a.transpose(0, 2, 1), HEAD_DIM, axis=2)

    bd = np.kron(np.eye(2), np.ones((HEAD_DIM, HEAD_DIM)))
    cd = np.repeat(cdec.reshape(npair, 2), HEAD_DIM, axis=1)[:, :, None] * bd[None]
    f = lambda a: jnp.asarray(a, F32)
    return f(inner), f(lanes(cross)), f(lanes(kdec)), f(cd), f(bd)


def _retention(u3, gn_g, *, q_col, k_col, v_col):
    B, S, _ = u3.shape
    C = RET_CHUNK
    npair = RET_HEADS // 2
    inner, cross, kdec, cd, bd = _ret_constants()
    return pl.pallas_call(
        _ret_kernel,
        grid=(B, npair, S // C),
        in_specs=[
            pl.BlockSpec((1, C, LANES), lambda b, j, i: (b, i, q_col + j)),
            pl.BlockSpec((1, C, LANES), lambda b, j, i: (b, i, k_col + j)),
            pl.BlockSpec((1, C, LANES), lambda b, j, i: (b, i, v_col + j)),
            pl.BlockSpec((1, 2, C, C), lambda b, j, i: (j, 0, 0, 0)),
            pl.BlockSpec((1, C, LANES), lambda b, j, i: (j, 0, 0)),
            pl.BlockSpec((1, C, LANES), lambda b, j, i: (j, 0, 0)),
            pl.BlockSpec((1, LANES, LANES), lambda b, j, i: (j, 0, 0)),
            pl.BlockSpec((LANES, LANES), lambda b, j, i: (0, 0)),
            pl.BlockSpec((1, LANES), lambda b, j, i: (0, j)),
        ],
        out_specs=pl.BlockSpec((1, C, LANES), lambda b, j, i: (b, i, j)),
        out_shape=jax.ShapeDtypeStruct((B, S, RET_WIDTH), F32),
        scratch_shapes=[pltpu.VMEM((LANES, LANES), F32)],
        compiler_params=_cp(("parallel", "parallel", "arbitrary")),
        name="retention",
    )(u3, u3, u3, inner, cross, kdec, cd, bd, gn_g.reshape(1, RET_WIDTH))


def _out0_kernel(of_ref, or_ref, z_ref, x_ref, w_ref, o_ref):
    z = _silu(z_ref[...].astype(F32))
    ya = (of_ref[...] * z[:, :FOX_WIDTH]).astype(BF16)
    yb = (or_ref[...] * z[:, FOX_WIDTH:]).astype(BF16)
    o_ref[...] = x_ref[...] + _dot(ya, w_ref[:FOX_WIDTH, :]) + _dot(yb, w_ref[FOX_WIDTH:, :])


def _out0(o_f, o_r, u, x2, w_out, *, tm=OUT_TM):
    N, D = x2.shape
    return pl.pallas_call(
        _out0_kernel,
        grid=(N // tm,),
        in_specs=[
            pl.BlockSpec((tm, FOX_WIDTH), lambda i: (i, 0)),
            pl.BlockSpec((tm, RET_WIDTH), lambda i: (i, 0)),
            pl.BlockSpec((tm, D), lambda i: (i, 0)),
            pl.BlockSpec((tm, D), lambda i: (i, 0)),
            pl.BlockSpec((D, D), lambda i: (0, 0)),
        ],
        out_specs=pl.BlockSpec((tm, D), lambda i: (i, 0)),
        out_shape=jax.ShapeDtypeStruct((N, D), F32),
        compiler_params=_cp(("parallel",)),
        name="out0",
    )(o_f, o_r, u, x2, w_out)


def _out1_kernel(oc_ref, os_ref, ow_ref, z_ref, x_ref, w_ref, g_ref, o_ref):
    z = _silu(z_ref[...].astype(F32))
    y = ((oc_ref[...] + os_ref[...] + ow_ref[...]) * z).astype(BF16)
    o_ref[...] = _rms(x_ref[...] + _dot(y, w_ref[...]), g_ref[...])


def _out1(o_c, o_s, o_w, u, x2, w_out, final_g, *, tm=OUT_TM):
    N, D = x2.shape
    row = pl.BlockSpec((tm, D), lambda i: (i, 0))
    return pl.pallas_call(
        _out1_kernel,
        grid=(N // tm,),
        in_specs=[row, row, row, row, row,
                  pl.BlockSpec((D, D), lambda i: (0, 0)),
                  pl.BlockSpec((1, D), lambda i: (0, 0))],
        out_specs=row,
        out_shape=jax.ShapeDtypeStruct((N, D), F32),
        compiler_params=_cp(("parallel",)),
        name="out1",
    )(o_c, o_s, o_w, u, x2, w_out, final_g.reshape(1, D))


def _compress_kernel(x_ref, pea_ref, peb_ref, wa_ref, wb_ref, w2_ref, o_ref, *, transposed):
    x = x_ref[0].astype(F32)
    a = _dot((x + pea_ref[...]).astype(BF16), wa_ref[0])
    b = _dot((x + peb_ref[...]).astype(BF16), wb_ref[0])
    nseg = x.shape[0]
    pre = a + pltpu.roll(b, nseg - 1, 0)
    hid = _silu(pre).astype(BF16)
    if transposed:
        o_ref[0, 0] = _dot_nt(w2_ref[...], hid).astype(o_ref.dtype)
    else:
        o_ref[0, 0] = _dot(hid, w2_ref[...]).astype(o_ref.dtype)


def _compress(a3, pe, w1, w2, *, transposed):
    B, nseg, wid = a3.shape
    half = CMP_STRIDE * HEAD_DIM
    eye = jnp.eye(NSA_GROUPS, dtype=w1.dtype)

    def big(wh):
        w4 = wh.reshape(CMP_STRIDE, 1, HEAD_DIM, CMP_HIDDEN)
        sel = eye[:, None, :, None, None]
        return (sel * w4[None]).reshape(NSA_GROUPS, wid, CMP_HIDDEN).astype(BF16)

    def pe_big(p):
        return jnp.broadcast_to(p[:, None, :], (CMP_STRIDE, NSA_GROUPS, HEAD_DIM)).reshape(1, wid)

    w2d = jnp.concatenate([w2, w2], axis=1).astype(BF16)
    if transposed:
        w2d = w2d.T
        oshape, oblock = (B, NSA_GROUPS, LANES, nseg), (1, 1, LANES, nseg)
    else:
        oshape, oblock = (B, NSA_GROUPS, nseg, LANES), (1, 1, nseg, LANES)
    return pl.pallas_call(
        functools.partial(_compress_kernel, transposed=transposed),
        grid=(B, NSA_GROUPS),
        in_specs=[
            pl.BlockSpec((1, nseg, wid), lambda b, g: (b, 0, 0)),
            pl.BlockSpec((1, wid), lambda b, g: (0, 0)),
            pl.BlockSpec((1, wid), lambda b, g: (0, 0)),
            pl.BlockSpec((1, wid, CMP_HIDDEN), lambda b, g: (g, 0, 0)),
            pl.BlockSpec((1, wid, CMP_HIDDEN), lambda b, g: (g, 0, 0)),
            pl.BlockSpec(w2d.shape, lambda b, g: (0, 0)),
        ],
        out_specs=pl.BlockSpec(oblock, lambda b, g: (b, g, 0, 0)),
        out_shape=jax.ShapeDtypeStruct(oshape, BF16),
        compiler_params=_cp(("parallel", "parallel")),
        name="compress",
    )(a3, pe_big(pe[:CMP_STRIDE]), pe_big(pe[CMP_STRIDE:]), big(w1[:half]), big(w1[half:]), w2d)


def _head_queries(q):
    out = []
    for p in range(NSA_HPG // 2):
        qp = q[:, p * LANES:(p + 1) * LANES]
        low = _low_half(qp.shape, 1)
        zero = jnp.zeros_like(qp)
        out += [jnp.where(low, qp, zero), jnp.where(low, zero, qp)]
    return out


def _gates(gl_ref, bg_ref, branch):
    gl = gl_ref[0] + bg_ref[...]
    return [_sigmoid(gl[:, N_BRANCH * i + branch:N_BRANCH * i + branch + 1]) for i in range(NSA_HPG)]


def _cmp_kernel(q_ref, kc_ref, vct_ref, mt_ref, sl_ref, gl_ref, bg_ref, o_ref, sel_ref):
    qi = pl.program_id(2)
    tq = NSA_TQ
    nseg = kc_ref.shape[2]
    qh = _head_queries(q_ref[0])
    kc = kc_ref[0, 0]
    vct = vct_ref[0, 0]
    t = qi * tq + lax.broadcasted_iota(jnp.int32, (1, tq), 1)
    cidx = lax.broadcasted_iota(jnp.int32, (nseg, 1), 0)
    cend = cidx * CMP_STRIDE + (CMP_BLOCK - 1)
    valid = (cend <= t) & (cidx < nseg - 1)
    dist = (t - cend).astype(F32)
    gates = _gates(gl_ref, bg_ref, 0)
    psum = jnp.zeros((nseg, tq), F32)
    outs = []
    for i in range(NSA_HPG):
        s = _dot_nt(kc, qh[i]) - sl_ref[0, i:i + 1, :] * dist
        s = jnp.where(valid, s, NEG)
        m = jnp.max(s, axis=0, keepdims=True)
        e = jnp.where(valid, jnp.exp(s - m), 0.0)
        l = jnp.sum(e, axis=0, keepdims=True)
        p = e * jnp.where(l > 0.0, 1.0 / l, 0.0)
        psum = psum + p
        o_t = _dot(vct, p.astype(BF16))
        outs.append(o_t.T * gates[i])
    low = _low_half((tq, LANES), 1)
    o_ref[0, :, :LANES] = jnp.where(low, outs[0], outs[1])
    o_ref[0, :, LANES:] = jnp.where(low, outs[2], outs[3])
    p_hi = psum.astype(BF16)
    r1 = psum - p_hi.astype(F32)
    p_mid = r1.astype(BF16)
    p_lo = (r1 - p_mid.astype(F32)).astype(BF16)
    mt = mt_ref[...]
    imp = _dot(mt, p_hi) + _dot(mt, p_mid) + _dot(mt, p_lo)
    ns = imp.shape[0]
    blk = lax.broadcasted_iota(jnp.int32, (ns, 1), 0)
    cur = t // SLC_BLOCK
    bvalid = blk * SLC_BLOCK <= t
    forced = (blk == 0) | (blk == cur) | (blk == cur - 1)
    score = jnp.where(bvalid, imp + jnp.where(forced, FORCE_BONUS, 0.0), NEG)
    blk_f = blk.astype(F32)
    sel = jnp.zeros((ns, tq), F32)
    for _ in range(SLC_TOPK):
        mx = jnp.max(score, axis=0, keepdims=True)
        first = jnp.min(jnp.where(score == mx, blk_f, float(ns)), axis=0, keepdims=True)
        hit = blk_f == first
        sel = jnp.where(hit, 1.0, sel)
        score = jnp.where(hit, -jnp.inf, score)
    sel = jnp.where(bvalid, sel, 0.0)
    sel_ref[0, 0] = sel.T.astype(sel_ref.dtype)


def _cmp_to_slc_t(nseg, ns):
    c0 = np.arange(nseg)[:, None] * CMP_STRIDE
    s0 = np.arange(ns)[None, :] * SLC_BLOCK
    overlap = np.clip(np.minimum(c0 + CMP_BLOCK, s0 + SLC_BLOCK) - np.maximum(c0, s0), 0, None)
    m = overlap / CMP_STRIDE
    m[nseg - 1] = 0.0
    return jnp.asarray(m.T, BF16)


def _slopes():
    s = 2.0 ** (-8.0 * (np.arange(NSA_HEADS) + 1) / NSA_HEADS)
    s = np.asarray(s, np.float32).reshape(NSA_GROUPS, NSA_HPG, 1)
    s = np.broadcast_to(s, (NSA_GROUPS, NSA_HPG, LANES))
    pad = np.zeros((NSA_GROUPS, 8 - NSA_HPG, LANES), np.float32)
    return jnp.asarray(np.concatenate([s, pad], axis=1))


def _nsa_common_specs(q_col):
    return dict(
        q=pl.BlockSpec((1, NSA_TQ, NSA_HPG * HEAD_DIM), lambda b, g, i: (b, i, q_col + g)),
        slopes=pl.BlockSpec((1, 8, LANES), lambda b, g, i: (g, 0, 0)),
        gl=pl.BlockSpec((1, NSA_TQ, LANES), lambda b, g, i: (b, i, g)),
        bg=pl.BlockSpec((1, LANES), lambda b, g, i: (0, g)),
        out=pl.BlockSpec((1, NSA_TQ, NSA_HPG * HEAD_DIM), lambda b, g, i: (b, i, g)),
    )


def _cmp_attention(u3, kcmp, vcmp_t, gl3, bg, *, q_col):
    B, S, _ = u3.shape
    nseg = kcmp.shape[2]
    ns = S // SLC_BLOCK
    sp = _nsa_common_specs(q_col)
    return pl.pallas_call(
        _cmp_kernel,
        grid=(B, NSA_GROUPS, S // NSA_TQ),
        in_specs=[
            sp["q"],
            pl.BlockSpec((1, 1, nseg, LANES), lambda b, g, i: (b, g, 0, 0)),
            pl.BlockSpec((1, 1, LANES, nseg), lambda b, g, i: (b, g, 0, 0)),
            pl.BlockSpec((ns, nseg), lambda b, g, i: (0, 0)),
            sp["slopes"], sp["gl"], sp["bg"],
        ],
        out_specs=[sp["out"], pl.BlockSpec((1, 1, NSA_TQ, ns), lambda b, g, i: (b, g, i, 0))],
        out_shape=[jax.ShapeDtypeStruct((B, S, NSA_WIDTH), F32),
                   jax.ShapeDtypeStruct((B, NSA_GROUPS, S, ns), BF16)],
        compiler_params=_cp(("parallel", "parallel", "arbitrary")),
        name="cmp_attention",
    )(u3, kcmp, vcmp_t, _cmp_to_slc_t(nseg, ns), _slopes(), gl3, bg)


def _tile_attention(qi, qh, k_ref, v_ref, sl_ref, gates, o_ref, acc_sc, *, tk, lo, hi, mask_fn,
                    bias_fn):
    tq = NSA_TQ
    nh = NSA_HPG
    qq = jnp.concatenate(qh, axis=0)
    t = qi * tq + lax.broadcasted_iota(jnp.int32, (tq, tk), 0)
    coli = lax.broadcasted_iota(jnp.int32, (tq, tk), 1)
    slope = [sl_ref[0, i:i + 1, 0:1] for i in range(nh)]
    acc_sc[...] = jnp.zeros_like(acc_sc)

    def step(kb, carry):
        k0 = pl.multiple_of(kb * tk, tk)
        k = k_ref[0, pl.ds(k0, tk), :]
        v = v_ref[0, pl.ds(k0, tk), :]
        s_all = _dot_nt(qq, k)
        col = k0 + coli
        valid = mask_fn(t, col)
        dist = (t - col).astype(F32)
        bias = bias_fn(k0)
        out = []
        for i in range(nh):
            m, l = carry[2 * i], carry[2 * i + 1]
            s = s_all[i * tq:(i + 1) * tq] - slope[i] * dist
            if bias is not None:
                s = s + bias
            s = jnp.where(valid, s, NEG)
            m_new = jnp.maximum(m, jnp.max(s, axis=-1, keepdims=True))
            p = jnp.exp(s - m_new)
            alpha = jnp.exp(m - m_new)
            l_new = alpha * l + jnp.sum(p, axis=-1, keepdims=True)
            acc_sc[i] = alpha * acc_sc[i] + _dot(p.astype(BF16), v)
            out += [m_new, l_new]
        return tuple(out)

    init = (jnp.full((tq, 1), NEG, F32), jnp.zeros((tq, 1), F32)) * nh
    carry = lax.fori_loop(lo, hi, step, init)
    outs = [acc_sc[i] / carry[2 * i + 1] * gates[i] for i in range(nh)]
    low = _low_half((tq, LANES), 1)
    o_ref[0, :, :LANES] = jnp.where(low, outs[0], outs[1])
    o_ref[0, :, LANES:] = jnp.where(low, outs[2], outs[3])


def _slc_kernel(q_ref, k_ref, v_ref, sel_ref, e_ref, sl_ref, gl_ref, bg_ref, o_ref, acc_sc):
    qi = pl.program_id(2)
    tk = SLC_TK
    selneg = ((sel_ref[0, 0].astype(F32) - 1.0) * MASK_BIG).astype(BF16)

    def bias_fn(k0):
        return _dot(selneg, e_ref[:, pl.ds(k0, tk)])

    hi = (qi * NSA_TQ + NSA_TQ + tk - 1) // tk
    _tile_attention(qi, _head_queries(q_ref[0]), k_ref, v_ref, sl_ref, _gates(gl_ref, bg_ref, 1),
                    o_ref, acc_sc, tk=tk, lo=0, hi=hi, mask_fn=lambda t, col: col <= t,
                    bias_fn=bias_fn)


def _win_kernel(q_ref, k_ref, v_ref, sl_ref, gl_ref, bg_ref, o_ref, acc_sc):
    qi = pl.program_id(2)
    tk = WIN_TK
    first = qi * NSA_TQ - WINDOW
    lo = jnp.maximum(first, 0) // tk
    hi = (qi * NSA_TQ + NSA_TQ + tk - 1) // tk
    _tile_attention(qi, _head_queries(q_ref[0]), k_ref, v_ref, sl_ref, _gates(gl_ref, bg_ref, 2),
                    o_ref, acc_sc, tk=tk, lo=lo, hi=hi,
                    mask_fn=lambda t, col: (col <= t) & (col > t - WINDOW),
                    bias_fn=lambda k0: None)


def _block_expand(ns, S):
    e = (np.arange(S)[None, :] // SLC_BLOCK) == np.arange(ns)[:, None]
    return jnp.asarray(e, BF16)


def _slc_attention(u3, sel, gl3, bg, *, q_col, k_col, v_col):
    B, S, _ = u3.shape
    ns = S // SLC_BLOCK
    sp = _nsa_common_specs(q_col)
    return pl.pallas_call(
        _slc_kernel,
        grid=(B, NSA_GROUPS, S // NSA_TQ),
        in_specs=[
            sp["q"],
            pl.BlockSpec((1, S, LANES), lambda b, g, i: (b, 0, k_col + g)),
            pl.BlockSpec((1, S, LANES), lambda b, g, i: (b, 0, v_col + g)),
            pl.BlockSpec((1, 1, NSA_TQ, ns), lambda b, g, i: (b, g, i, 0)),
            pl.BlockSpec((ns, S), lambda b, g, i: (0, 0)),
            sp["slopes"], sp["gl"], sp["bg"],
        ],
        out_specs=sp["out"],
        out_shape=jax.ShapeDtypeStruct((B, S, NSA_WIDTH), F32),
        scratch_shapes=[pltpu.VMEM((NSA_HPG, NSA_TQ, LANES), F32)],
        compiler_params=_cp(("parallel", "parallel", "arbitrary")),
        name="slc_attention",
    )(u3, u3, u3, sel, _block_expand(ns, S), _slopes(), gl3, bg)


def _win_attention(u3, gl3, bg, *, q_col, k_col, v_col):
    B, S, _ = u3.shape
    sp = _nsa_common_specs(q_col)
    return pl.pallas_call(
        _win_kernel,
        grid=(B, NSA_GROUPS, S // NSA_TQ),
        in_specs=[
            sp["q"],
            pl.BlockSpec((1, S, LANES), lambda b, g, i: (b, 0, k_col + g)),
            pl.BlockSpec((1, S, LANES), lambda b, g, i: (b, 0, v_col + g)),
            sp["slopes"], sp["gl"], sp["bg"],
        ],
        out_specs=sp["out"],
        out_shape=jax.ShapeDtypeStruct((B, S, NSA_WIDTH), F32),
        scratch_shapes=[pltpu.VMEM((NSA_HPG, NSA_TQ, LANES), F32)],
        compiler_params=_cp(("parallel", "parallel", "arbitrary")),
        name="win_attention",
    )(u3, u3, u3, _slopes(), gl3, bg)


def _dup_groups(w):
    d = w.shape[0]
    w = w.reshape(d, NSA_GROUPS, 1, HEAD_DIM)
    return jnp.broadcast_to(w, (d, NSA_GROUPS, 2, HEAD_DIM)).reshape(d, NSA_GROUPS * LANES)


def _even_layer(x, norm_g, w_in, b_f, gn_g, w_out):
    B, S, D = x.shape
    scale = HEAD_DIM ** -0.5
    q_f, k_f, v_f, w_fl, q_r, k_r, v_r, z = jnp.split(
        w_in, np.cumsum([FOX_WIDTH] * 3 + [FOX_HEADS] + [RET_WIDTH] * 3)[:].tolist(), axis=1)
    w = jnp.concatenate([z, q_f * scale, k_f, v_f, q_r, k_r * scale, v_r], axis=1).astype(BF16)
    x2 = x.reshape(B * S, D)
    u = _proj(x2, norm_g, w)
    u3 = u.reshape(B, S, -1)
    ccol, crow = _fgate(x, norm_g, w_fl, b_f, tile=min(512, S))
    cb = D // LANES
    nb = FOX_WIDTH // LANES
    o_f = _fox(u3, ccol, crow, q_col=cb, k_col=cb + nb, v_col=cb + 2 * nb)
    o_r = _retention(u3, gn_g, q_col=cb + 3 * nb, k_col=cb + 4 * nb, v_col=cb + 5 * nb)
    out = _out0(o_f.reshape(B * S, -1), o_r.reshape(B * S, -1), u, x2, w_out.astype(BF16))
    return out.reshape(B, S, D)


def _odd_layer(x, norm_g, w_in, b_gate, pe_k, pe_v, wk1, wk2, wv1, wv2, w_out, final_g):
    B, S, D = x.shape
    scale = HEAD_DIM ** -0.5
    sizes = [NSA_WIDTH] + [NSA_KV_WIDTH] * 6 + [NSA_HEADS * N_BRANCH]
    q, kc, vc, ks, vs, kw, vw, gl, z = jnp.split(w_in, np.cumsum(sizes).tolist(), axis=1)
    w = jnp.concatenate([z, q * scale, kc, vc, _dup_groups(ks), _dup_groups(vs), _dup_groups(kw),
                         _dup_groups(vw)], axis=1).astype(BF16)
    x2 = x.reshape(B * S, D)
    split_tile = (2 * D) // PROJ_TN
    u, kc_a, vc_a = _proj(x2, norm_g, w, split_tile=split_tile)
    u3 = u.reshape(B, S, -1)
    per_group = NSA_HPG * N_BRANCH
    glw = jnp.pad(gl.reshape(D, NSA_GROUPS, per_group), ((0, 0), (0, 0), (0, LANES - per_group)))
    glw = glw.reshape(D, NSA_GROUPS * LANES).astype(BF16)
    bg = jnp.pad(b_gate.reshape(NSA_GROUPS, per_group), ((0, 0), (0, LANES - per_group)))
    bg = bg.reshape(1, NSA_GROUPS * LANES)
    gl3 = _proj(x2, norm_g, glw, out_dtype=F32, tn=NSA_GROUPS * LANES).reshape(B, S, -1)
    nseg = S // CMP_STRIDE
    kcmp = _compress(kc_a.reshape(B, nseg, -1), pe_k, wk1, wk2, transposed=False)
    vcmp_t = _compress(vc_a.reshape(B, nseg, -1), pe_v, wv1, wv2, transposed=True)
    q_col = D // (NSA_HPG * HEAD_DIM)
    o_c, sel = _cmp_attention(u3, kcmp, vcmp_t, gl3, bg, q_col=q_col)
    kb = (2 * D + 2 * NSA_KV_WIDTH) // LANES
    o_s = _slc_attention(u3, sel, gl3, bg, q_col=q_col, k_col=kb, v_col=kb + NSA_GROUPS)
    o_w = _win_attention(u3, gl3, bg, q_col=q_col, k_col=kb + 2 * NSA_GROUPS,
                         v_col=kb + 3 * NSA_GROUPS)
    r = lambda a: a.reshape(B * S, -1)
    out = _out1(r(o_c), r(o_s), r(o_w), u, x2, w_out.astype(BF16), final_g)
    return out.reshape(B, S, D)


def kernel(x, even_norm_g, even_w_in, even_b_f, even_gn_g, even_w_out, odd_norm_g, odd_w_in,
           odd_b_gate, odd_pe_k, odd_pe_v, odd_wk1, odd_wk2, odd_wv1, odd_wv2, odd_w_out, final_g):
    x = _even_layer(x, even_norm_g[0], even_w_in[0], even_b_f[0], even_gn_g[0], even_w_out[0])
    return _odd_layer(x, odd_norm_g[0], odd_w_in[0], odd_b_gate[0], odd_pe_k[0], odd_pe_v[0],
                      odd_wk1[0], odd_wk2[0], odd_wv1[0], odd_wv2[0], odd_w_out[0], final_g)
```

```python
import functools
import math

import jax
import jax.numpy as jnp
import numpy as np
from jax import lax
from jax.experimental import pallas as pl
from jax.experimental.pallas import tpu as pltpu

D_MODEL = 1024
HEAD_DIM = 64
LANES = 128
FOX_HEADS = 8
RET_HEADS = 8
FOX_WIDTH = FOX_HEADS * HEAD_DIM
RET_WIDTH = RET_HEADS * HEAD_DIM
RET_CHUNK = 128
NSA_HEADS = 16
NSA_GROUPS = 4
NSA_HPG = NSA_HEADS // NSA_GROUPS
NSA_WIDTH = NSA_HEADS * HEAD_DIM
NSA_KV_WIDTH = NSA_GROUPS * HEAD_DIM
N_BRANCH = 3
GATE_ROWS = 16
CMP_BLOCK = 32
CMP_STRIDE = 16
CMP_HIDDEN = 256
SLC_BLOCK = 64
SLC_TOPK = 16
NS_PAD = LANES
WINDOW = 512
RMS_EPS = 1e-6
GN_EPS = 1e-5
NEG = -1e30
FORCE_BONUS = 1e6
MASK_BIG = 2.0 ** 100
LOG2E = math.log2(math.e)
FEAT0 = HEAD_DIM
ACC_ROWS = HEAD_DIM + 16

PROJ_TM = 512
PROJ_TN = 512
FOX_TQ = 256
FOX_TK = 512
NSA_TQ = 128
SLC_TK = 512
WIN_TK = 128
OUT_TM = 512
VMEM_LIMIT = 48 * 1024 * 1024

F32 = jnp.float32
BF16 = jnp.bfloat16


def _cp(sem, vmem=VMEM_LIMIT):
    return pltpu.CompilerParams(dimension_semantics=sem, vmem_limit_bytes=vmem)


def _dot(a, b):
    return jnp.dot(a, b, preferred_element_type=F32)


def _dot_nt(a, b):
    return lax.dot_general(a, b, (((1,), (1,)), ((), ())), preferred_element_type=F32)


def _dot_tn(a, b):
    return lax.dot_general(a, b, (((0,), (0,)), ((), ())), preferred_element_type=F32)


def _rms(x, g):
    return x * lax.rsqrt(jnp.mean(x * x, axis=-1, keepdims=True) + RMS_EPS) * g


def _silu(x):
    return x * (1.0 / (1.0 + jnp.exp(-x)))


def _sigmoid(x):
    return 1.0 / (1.0 + jnp.exp(-x))


def _low_half(shape, axis):
    return lax.broadcasted_iota(jnp.int32, shape, axis) < HEAD_DIM


def _pieces(v):
    p1 = v.astype(BF16).astype(F32)
    r = v - p1
    p2 = r.astype(BF16).astype(F32)
    p3 = (r - p2).astype(BF16).astype(F32)
    return p1, p2, p3


def _np_pieces(v):
    v = np.asarray(v, np.float64)
    bf = lambda a: np.asarray(a, np.float32).astype(BF16).astype(np.float64)
    p1 = bf(v)
    p2 = bf(v - p1)
    p3 = bf(v - p1 - p2)
    return p1, p2, p3


def _place(lane, cols):
    out = jnp.zeros(lane.shape, F32)
    for i, c in cols.items():
        out = jnp.where(lane == i, c, out)
    return out


def _fgate_kernel(x_ref, g_ref, wf_ref, b_ref, kf_ref, qf_ref, carry):
    @pl.when(pl.program_id(1) == 0)
    def _():
        carry[...] = jnp.zeros_like(carry)

    h = _rms(x_ref[0], g_ref[...])
    t = h.shape[0]
    hp = lax.Precision.HIGHEST
    f = jnp.dot(h, wf_ref[...], precision=hp, preferred_element_type=F32) + b_ref[...]
    ls = jnp.minimum(f, 0.0) - jnp.log(1.0 + jnp.exp(-jnp.abs(f)))
    r = lax.broadcasted_iota(jnp.int32, (t, t), 0)
    c = lax.broadcasted_iota(jnp.int32, (t, t), 1)
    lower = (c <= r).astype(F32)
    cs = jnp.dot(lower, ls, precision=hp, preferred_element_type=F32) + carry[...]
    carry[...] = cs[t - 1:t, :]
    cl = cs * LOG2E
    lane = lax.broadcasted_iota(jnp.int32, (t, LANES), 1)
    one = jnp.ones((t, 1), F32)
    for j in range(FOX_HEADS // 2):
        a1, a2, a3 = _pieces(cl[:, 2 * j:2 * j + 1])
        b1, b2, b3 = _pieces(cl[:, 2 * j + 1:2 * j + 2])
        kf = _place(lane, {0: a1, 1: a2, 2: a3, 3: one, 4: one, 5: one, 6: b1, 7: b2, 8: b3})
        kf_ref[0, j] = kf.astype(BF16)
        qf_ref[0, j, 0] = _place(lane, {0: -one, 1: -one, 2: -one, 3: a1, 4: a2, 5: a3}).astype(BF16)
        qf_ref[0, j, 1] = _place(lane, {3: b1, 4: b2, 5: b3, 6: -one, 7: -one, 8: -one}).astype(BF16)


def _fgate(x, g, wf, b_f, *, tile):
    B, S, D = x.shape
    npair = FOX_HEADS // 2
    return pl.pallas_call(
        _fgate_kernel,
        grid=(B, S // tile),
        in_specs=[
            pl.BlockSpec((1, tile, D), lambda b, s: (b, s, 0)),
            pl.BlockSpec((1, D), lambda b, s: (0, 0)),
            pl.BlockSpec((D, FOX_HEADS), lambda b, s: (0, 0)),
            pl.BlockSpec((1, FOX_HEADS), lambda b, s: (0, 0)),
        ],
        out_specs=[
            pl.BlockSpec((1, npair, tile, LANES), lambda b, s: (b, 0, s, 0)),
            pl.BlockSpec((1, npair, 2, tile, LANES), lambda b, s: (b, 0, 0, s, 0)),
        ],
        out_shape=[
            jax.ShapeDtypeStruct((B, npair, S, LANES), BF16),
            jax.ShapeDtypeStruct((B, npair, 2, S, LANES), BF16),
        ],
        scratch_shapes=[pltpu.VMEM((1, FOX_HEADS), F32)],
        compiler_params=_cp(("parallel", "arbitrary")),
        name="fgate",
    )(x, g.reshape(1, D), wf, b_f.reshape(1, FOX_HEADS))


def _proj_kernel(*refs, split_tile, add_tiles, n_t):
    it = iter(refs)
    x_ref, g_ref, w_ref = next(it), next(it), next(it)
    add_ref = next(it) if add_tiles else None
    wt_refs = [next(it) for _ in range(n_t)]
    u_ref = next(it)
    e_refs = [next(it), next(it)] if split_tile is not None else None
    ut_refs = [next(it) for _ in range(n_t)]
    h_sc = next(it)
    j = pl.program_id(1)

    @pl.when(j == 0)
    def _():
        h = _rms(x_ref[...], g_ref[...]).astype(BF16)
        h_sc[...] = h
        for wt_ref, ut_ref in zip(wt_refs, ut_refs):
            ut_ref[0] = _dot_nt(wt_ref[...], h).astype(ut_ref.dtype)

    acc = _dot(h_sc[...], w_ref[...])
    if add_tiles:
        is_add = functools.reduce(jnp.logical_or, [j == t for t in add_tiles])

        @pl.when(is_add)
        def _():
            u_ref[...] = (acc + add_ref[...].astype(F32)).astype(u_ref.dtype)

        @pl.when(jnp.logical_not(is_add))
        def _():
            u_ref[...] = acc.astype(u_ref.dtype)
    else:
        u_ref[...] = acc.astype(u_ref.dtype)
    if split_tile is not None:
        @pl.when(j == split_tile)
        def _():
            half = acc.shape[1] // 2
            e_refs[0][...] = acc[:, :half].astype(e_refs[0].dtype)
            e_refs[1][...] = acc[:, half:].astype(e_refs[1].dtype)


def _proj(x2, g, w, *, seq, split_tile=None, addend=None, add_tiles=(), w_t=(), t_dtypes=(),
          tm=PROJ_TM, tn=PROJ_TN):
    N, D = x2.shape
    W = w.shape[1]
    nbs = seq // tm
    in_specs = [
        pl.BlockSpec((tm, D), lambda i, j: (i, 0)),
        pl.BlockSpec((1, D), lambda i, j: (0, 0)),
        pl.BlockSpec((D, tn), lambda i, j: (0, j)),
    ]
    args = [x2, g.reshape(1, D), w]
    if add_tiles:
        in_specs.append(pl.BlockSpec((tm, tn), lambda i, j: (i % nbs, 0)))
        args.append(addend)
    out_shape = [jax.ShapeDtypeStruct((N, W), BF16)]
    out_specs = [pl.BlockSpec((tm, tn), lambda i, j: (i, j))]
    if split_tile is not None:
        out_shape += [jax.ShapeDtypeStruct((N, tn // 2), BF16)] * 2
        out_specs += [pl.BlockSpec((tm, tn // 2), lambda i, j: (i, 0))] * 2
    for wt, dt in zip(w_t, t_dtypes):
        rows = wt.shape[0]
        in_specs.append(pl.BlockSpec((rows, D), lambda i, j: (0, 0)))
        args.append(wt)
        out_shape.append(jax.ShapeDtypeStruct((N // seq, rows, seq), dt))
        out_specs.append(pl.BlockSpec((1, rows, tm), lambda i, j: (i // nbs, 0, i % nbs)))
    return pl.pallas_call(
        functools.partial(_proj_kernel, split_tile=split_tile, add_tiles=tuple(add_tiles),
                          n_t=len(w_t)),
        grid=(N // tm, W // tn),
        in_specs=in_specs,
        out_specs=out_specs,
        out_shape=out_shape,
        scratch_shapes=[pltpu.VMEM((tm, D), BF16)],
        compiler_params=_cp(("parallel", "arbitrary")),
        name="proj",
    )(*args)


def _flash_t(bm, nh, tq, tk, k_tile, v_rows, acc_sc, lo, nfull, n_masked, valid_fn):
    acc_sc[...] = jnp.zeros_like(acc_sc)
    ones = jnp.ones((ACC_ROWS - HEAD_DIM, tk), BF16)
    key_iota = lax.broadcasted_iota(jnp.int32, (tk, tq), 0)

    def step(kb, ms, masked):
        k0 = pl.multiple_of(kb * tk, tk)
        s_all = _dot_nt(k_tile(k0), bm)
        if masked:
            valid = valid_fn(k0 + key_iota, k0)
        out = []
        for h in range(nh):
            s = s_all[:, h * tq:(h + 1) * tq]
            if masked:
                s = jnp.where(valid, s, NEG)
            m_new = jnp.maximum(ms[h], jnp.max(s, axis=0, keepdims=True))
            p = jnp.exp2(s - m_new).astype(BF16)
            alpha = jnp.exp2(ms[h] - m_new)
            lhs = jnp.concatenate([v_rows(h, k0), ones], axis=0)
            acc_sc[h] = alpha * acc_sc[h] + _dot(lhs, p)
            out.append(m_new)
        return tuple(out)

    ms = (jnp.full((1, tq), NEG, F32),) * nh
    ms = lax.fori_loop(lo, lo + nfull, lambda kb, c: step(kb, c, False), ms)
    if isinstance(n_masked, int):
        for d in range(n_masked):
            ms = step(lo + nfull + d, ms, True)
    else:
        ms = lax.fori_loop(lo + nfull, lo + nfull + n_masked, lambda kb, c: step(kb, c, True), ms)
    outs = []
    for h in range(nh):
        a = acc_sc[h]
        outs.append(a[:HEAD_DIM] * (1.0 / a[HEAD_DIM:HEAD_DIM + 1]))
    return outs


def _fox_kernel(q_ref, qf_ref, k_ref, kf_ref, vt_ref, o_ref, acc_sc):
    qi = pl.program_id(2)
    tq, tk = FOX_TQ, FOX_TK
    q = q_ref[0]
    low = _low_half(q.shape, 1)
    zero = jnp.zeros_like(q)
    bm = jnp.concatenate([
        jnp.concatenate([jnp.where(low, q, zero), qf_ref[0, 0, 0]], axis=1),
        jnp.concatenate([jnp.where(low, zero, q), qf_ref[0, 0, 1]], axis=1)], axis=0)
    qpos = qi * tq + lax.broadcasted_iota(jnp.int32, (tk, tq), 1)

    def k_tile(k0):
        return jnp.concatenate([k_ref[0, pl.ds(k0, tk), :], kf_ref[0, 0, pl.ds(k0, tk), :]], axis=1)

    def v_rows(h, k0):
        return vt_ref[0, h * HEAD_DIM:(h + 1) * HEAD_DIM, pl.ds(k0, tk)]

    outs = _flash_t(bm, 2, tq, tk, k_tile, v_rows, acc_sc, 0, (qi * tq) // tk, 1,
                    lambda key, k0: key <= qpos)
    o_ref[0] = jnp.concatenate(outs, axis=0).T


def _fox(u3, vt, kfeat, qfeat, *, q_col, k_col):
    B, S, _ = u3.shape
    npair = FOX_HEADS // 2
    return pl.pallas_call(
        _fox_kernel,
        grid=(B, npair, S // FOX_TQ),
        in_specs=[
            pl.BlockSpec((1, FOX_TQ, LANES), lambda b, j, i: (b, i, q_col + j)),
            pl.BlockSpec((1, 1, 2, FOX_TQ, LANES), lambda b, j, i: (b, j, 0, i, 0)),
            pl.BlockSpec((1, S, LANES), lambda b, j, i: (b, 0, k_col + j)),
            pl.BlockSpec((1, 1, S, LANES), lambda b, j, i: (b, j, 0, 0)),
            pl.BlockSpec((1, LANES, S), lambda b, j, i: (b, j, 0)),
        ],
        out_specs=pl.BlockSpec((1, FOX_TQ, LANES), lambda b, j, i: (b, i, j)),
        out_shape=jax.ShapeDtypeStruct((B, S, FOX_WIDTH), F32),
        scratch_shapes=[pltpu.VMEM((2, ACC_ROWS, FOX_TQ), F32)],
        compiler_params=_cp(("parallel", "parallel", "arbitrary")),
        name="fox",
    )(u3, qfeat, u3, kfeat, vt)


def _ret_kernel(q_ref, k_ref, v_ref, inner_ref, cross_ref, kdec_ref, cd_ref, bd_ref, gn_ref,
                o_ref, state_sc):
    @pl.when(pl.program_id(2) == 0)
    def _():
        state_sc[...] = jnp.zeros_like(state_sc)

    q, k, v = q_ref[0], k_ref[0], v_ref[0]
    low = _low_half(q.shape, 1)
    zero = jnp.zeros_like(q)
    qa, qb = jnp.where(low, q, zero), jnp.where(low, zero, q)
    pa = (_dot_nt(qa, k) * inner_ref[0, 0]).astype(BF16)
    pb = (_dot_nt(qb, k) * inner_ref[0, 1]).astype(BF16)
    o_in = jnp.where(low, _dot(pa, v), _dot(pb, v))
    state = state_sc[...]
    o = o_in + _dot(q, state.astype(BF16)) * cross_ref[0]
    kd = (k.astype(F32) * kdec_ref[0]).astype(BF16)
    state_sc[...] = state * cd_ref[0] + _dot_tn(kd, v) * bd_ref[...]
    inv = 1.0 / HEAD_DIM
    sa = jnp.sum(jnp.where(low, o, 0.0), axis=-1, keepdims=True)
    st = jnp.sum(o, axis=-1, keepdims=True)
    mu = jnp.where(low, sa, st - sa) * inv
    d = o - mu
    d2 = d * d
    va = jnp.sum(jnp.where(low, d2, 0.0), axis=-1, keepdims=True)
    vt = jnp.sum(d2, axis=-1, keepdims=True)
    var = jnp.where(low, va, vt - va) * inv
    o_ref[0] = d * lax.rsqrt(var + GN_EPS) * gn_ref[...]


def _ret_constants():
    lg = np.log(1.0 - 2.0 ** (-5.0 - np.arange(RET_HEADS)))
    i = np.arange(RET_CHUNK)
    diff = i[:, None] - i[None, :]
    inner = np.where(diff[None] >= 0, np.exp(lg[:, None, None] * np.maximum(diff, 0)[None]), 0.0)
    cross = np.exp(lg[:, None] * (i[None, :] + 1))
    kdec = np.exp(lg[:, None] * (RET_CHUNK - 1 - i)[None, :])
    cdec = np.exp(lg * RET_CHUNK)
    npair = RET_HEADS // 2
    inner = inner.reshape(npair, 2, RET_CHUNK, RET_CHUNK)

    def lanes(a):
        a = a.reshape(npair, 2, RET_CHUNK)
        return np.repeat(a.transpose(0, 2, 1), HEAD_DIM, axis=2)

    bd = np.kron(np.eye(2), np.ones((HEAD_DIM, HEAD_DIM)))
    cd = np.repeat(cdec.reshape(npair, 2), HEAD_DIM, axis=1)[:, :, None] * bd[None]
    f = lambda a: jnp.asarray(a, F32)
    return f(inner), f(lanes(cross)), f(lanes(kdec)), f(cd), f(bd)


def _retention(u3, gn_g, *, q_col, k_col, v_col):
    B, S, _ = u3.shape
    C = RET_CHUNK
    npair = RET_HEADS // 2
    inner, cross, kdec, cd, bd = _ret_constants()
    return pl.pallas_call(
        _ret_kernel,
        grid=(B, npair, S // C),
        in_specs=[
            pl.BlockSpec((1, C, LANES), lambda b, j, i: (b, i, q_col + j)),
            pl.BlockSpec((1, C, LANES), lambda b, j, i: (b, i, k_col + j)),
            pl.BlockSpec((1, C, LANES), lambda b, j, i: (b, i, v_col + j)),
            pl.BlockSpec((1, 2, C, C), lambda b, j, i: (j, 0, 0, 0)),
            pl.BlockSpec((1, C, LANES), lambda b, j, i: (j, 0, 0)),
            pl.BlockSpec((1, C, LANES), lambda b, j, i: (j, 0, 0)),
            pl.BlockSpec((1, LANES, LANES), lambda b, j, i: (j, 0, 0)),
            pl.BlockSpec((LANES, LANES), lambda b, j, i: (0, 0)),
            pl.BlockSpec((1, LANES), lambda b, j, i: (0, j)),
        ],
        out_specs=pl.BlockSpec((1, C, LANES), lambda b, j, i: (b, i, j)),
        out_shape=jax.ShapeDtypeStruct((B, S, RET_WIDTH), F32),
        scratch_shapes=[pltpu.VMEM((LANES, LANES), F32)],
        compiler_params=_cp(("parallel", "parallel", "arbitrary")),
        name="retention",
    )(u3, u3, u3, inner, cross, kdec, cd, bd, gn_g.reshape(1, RET_WIDTH))


def _out0_kernel(of_ref, or_ref, z_ref, x_ref, w_ref, o_ref):
    z = _silu(z_ref[...].astype(F32))
    ya = (of_ref[...] * z[:, :FOX_WIDTH]).astype(BF16)
    yb = (or_ref[...] * z[:, FOX_WIDTH:]).astype(BF16)
    o_ref[...] = x_ref[...] + _dot(ya, w_ref[:FOX_WIDTH, :]) + _dot(yb, w_ref[FOX_WIDTH:, :])


def _out0(o_f, o_r, u, x2, w_out, *, tm=OUT_TM):
    N, D = x2.shape
    return pl.pallas_call(
        _out0_kernel,
        grid=(N // tm,),
        in_specs=[
            pl.BlockSpec((tm, FOX_WIDTH), lambda i: (i, 0)),
            pl.BlockSpec((tm, RET_WIDTH), lambda i: (i, 0)),
            pl.BlockSpec((tm, D), lambda i: (i, 0)),
            pl.BlockSpec((tm, D), lambda i: (i, 0)),
            pl.BlockSpec((D, D), lambda i: (0, 0)),
        ],
        out_specs=pl.BlockSpec((tm, D), lambda i: (i, 0)),
        out_shape=jax.ShapeDtypeStruct((N, D), F32),
        compiler_params=_cp(("parallel",)),
        name="out0",
    )(o_f, o_r, u, x2, w_out)


def _out1_kernel(oc_ref, os_ref, ow_ref, z_ref, x_ref, w_ref, g_ref, o_ref):
    z = _silu(z_ref[...].astype(F32))
    y = ((oc_ref[...] + os_ref[...] + ow_ref[...]) * z).astype(BF16)
    o_ref[...] = _rms(x_ref[...] + _dot(y, w_ref[...]), g_ref[...])


def _out1(o_c, o_s, o_w, u, x2, w_out, final_g, *, tm=OUT_TM):
    N, D = x2.shape
    row = pl.BlockSpec((tm, D), lambda i: (i, 0))
    return pl.pallas_call(
        _out1_kernel,
        grid=(N // tm,),
        in_specs=[row, row, row, row, row,
                  pl.BlockSpec((D, D), lambda i: (0, 0)),
                  pl.BlockSpec((1, D), lambda i: (0, 0))],
        out_specs=row,
        out_shape=jax.ShapeDtypeStruct((N, D), F32),
        compiler_params=_cp(("parallel",)),
        name="out1",
    )(o_c, o_s, o_w, u, x2, w_out, final_g.reshape(1, D))


def _compress_kernel(x_ref, pea_ref, peb_ref, wa_ref, wb_ref, w2_ref, o_ref, *, transposed):
    x = x_ref[0].astype(F32)
    a = _dot((x + pea_ref[...]).astype(BF16), wa_ref[0])
    b = _dot((x + peb_ref[...]).astype(BF16), wb_ref[0])
    nseg = x.shape[0]
    pre = a + pltpu.roll(b, nseg - 1, 0)
    hid = _silu(pre).astype(BF16)
    if transposed:
        o_ref[0, 0] = _dot_nt(w2_ref[...], hid).astype(o_ref.dtype)
    else:
        o_ref[0, 0] = _dot(hid, w2_ref[...]).astype(o_ref.dtype)


def _compress(a3, pe, w1, w2, *, transposed):
    B, nseg, wid = a3.shape
    half = CMP_STRIDE * HEAD_DIM
    eye = jnp.eye(NSA_GROUPS, dtype=w1.dtype)

    def big(wh):
        w4 = wh.reshape(CMP_STRIDE, 1, HEAD_DIM, CMP_HIDDEN)
        sel = eye[:, None, :, None, None]
        return (sel * w4[None]).reshape(NSA_GROUPS, wid, CMP_HIDDEN).astype(BF16)

    def pe_big(p):
        return jnp.broadcast_to(p[:, None, :], (CMP_STRIDE, NSA_GROUPS, HEAD_DIM)).reshape(1, wid)

    w2d = jnp.concatenate([w2, w2], axis=1).astype(BF16)
    if transposed:
        w2d = w2d.T
        oshape, oblock = (B, NSA_GROUPS, LANES, nseg), (1, 1, LANES, nseg)
    else:
        oshape, oblock = (B, NSA_GROUPS, nseg, LANES), (1, 1, nseg, LANES)
    return pl.pallas_call(
        functools.partial(_compress_kernel, transposed=transposed),
        grid=(B, NSA_GROUPS),
        in_specs=[
            pl.BlockSpec((1, nseg, wid), lambda b, g: (b, 0, 0)),
            pl.BlockSpec((1, wid), lambda b, g: (0, 0)),
            pl.BlockSpec((1, wid), lambda b, g: (0, 0)),
            pl.BlockSpec((1, wid, CMP_HIDDEN), lambda b, g: (g, 0, 0)),
            pl.BlockSpec((1, wid, CMP_HIDDEN), lambda b, g: (g, 0, 0)),
            pl.BlockSpec(w2d.shape, lambda b, g: (0, 0)),
        ],
        out_specs=pl.BlockSpec(oblock, lambda b, g: (b, g, 0, 0)),
        out_shape=jax.ShapeDtypeStruct(oshape, BF16),
        compiler_params=_cp(("parallel", "parallel")),
        name="compress",
    )(a3, pe_big(pe[:CMP_STRIDE]), pe_big(pe[CMP_STRIDE:]), big(w1[:half]), big(w1[half:]), w2d)


def _slope_table():
    s = np.asarray(2.0 ** (-8.0 * (np.arange(NSA_HEADS) + 1) / NSA_HEADS), np.float32)
    sl = np.asarray(s.astype(np.float64) * LOG2E, np.float32)
    p1, p2, p3 = _np_pieces(sl)
    tab = np.zeros((NSA_HEADS, LANES), np.float32)
    tab[:, 0] = sl
    for k, p in enumerate((p1, p1, p2, p2, p3, p3)):
        tab[:, FEAT0 + k] = p
    tab = tab.reshape(NSA_GROUPS, NSA_HPG, LANES)
    pad = np.zeros((NSA_GROUPS, 8 - NSA_HPG, LANES), np.float32)
    return jnp.asarray(np.concatenate([tab, pad], axis=1))


def _key_pos_features(S, width):
    pos = jnp.arange(S, dtype=jnp.int32)[:, None]
    lane = jnp.arange(width, dtype=jnp.int32)[None, :] % LANES
    hi = ((pos // SLC_BLOCK) * SLC_BLOCK).astype(F32)
    lo = (pos % SLC_BLOCK).astype(F32)
    k = lane - FEAT0
    f = jnp.where((k >= 0) & (k < 6), jnp.where(k % 2 == 0, hi, lo), 0.0)
    f = jnp.where((k >= 6) & (k < 9), 1.0, f)
    return f.astype(BF16)


def _nsa_queries(q, tab_ref, t0):
    tq = q.shape[0]
    lane = lax.broadcasted_iota(jnp.int32, (tq, LANES), 1)
    low = lane < HEAD_DIM
    t = (t0 + lax.broadcasted_iota(jnp.int32, (tq, 1), 0)).astype(F32)
    out = []
    for i in range(NSA_HPG):
        p, hf = divmod(i, 2)
        qp = q[:, p * LANES:(p + 1) * LANES].astype(F32)
        if hf:
            qp = pltpu.roll(qp, HEAD_DIM, 1)
        row = tab_ref[0, i:i + 1, :]
        a1, a2, a3 = _pieces(-(row[:, 0:1] * t))
        feat = jnp.where(lane == FEAT0 + 6, a1,
                         jnp.where(lane == FEAT0 + 7, a2, jnp.where(lane == FEAT0 + 8, a3, row)))
        out.append(jnp.where(low, qp, feat).astype(BF16))
    return out


def _gates_t(gt_ref, bg_ref, branch):
    gl = gt_ref[0] + bg_ref[...]
    return [_sigmoid(gl[N_BRANCH * i + branch:N_BRANCH * i + branch + 1, :]) for i in range(NSA_HPG)]


def _store_heads(o_ref, outs_t, gates):
    g = [o * gt for o, gt in zip(outs_t, gates)]
    o_ref[0, :, :LANES] = jnp.concatenate(g[:2], axis=0).T
    o_ref[0, :, LANES:] = jnp.concatenate(g[2:], axis=0).T


def _nsa_specs(q_col):
    return dict(
        q=pl.BlockSpec((1, NSA_TQ, NSA_HPG * HEAD_DIM), lambda b, g, i: (b, i, q_col + g)),
        tab=pl.BlockSpec((1, 8, LANES), lambda b, g, i: (g, 0, 0)),
        gt=pl.BlockSpec((1, GATE_ROWS, NSA_TQ), lambda b, g, i: (b, g, i)),
        bg=pl.BlockSpec((GATE_ROWS, 1), lambda b, g, i: (g, 0)),
        out=pl.BlockSpec((1, NSA_TQ, NSA_HPG * HEAD_DIM), lambda b, g, i: (b, i, g)),
    )


def _cmp_kernel(q_ref, kc_ref, vct_ref, mt_ref, tab_ref, gt_ref, bg_ref, o_ref, sel_ref):
    qi = pl.program_id(2)
    tq = NSA_TQ
    nseg = kc_ref.shape[2]
    q = q_ref[0]
    kc = kc_ref[0, 0]
    vct = vct_ref[0, 0]
    t = qi * tq + lax.broadcasted_iota(jnp.int32, (1, tq), 1)
    cidx = lax.broadcasted_iota(jnp.int32, (nseg, 1), 0)
    cend = cidx * CMP_STRIDE + (CMP_BLOCK - 1)
    valid = (cend <= t) & (cidx < nseg - 1)
    dist = (t - cend).astype(F32)
    gates = _gates_t(gt_ref, bg_ref, 0)
    psum = jnp.zeros((nseg, tq), F32)
    outs = []
    for i in range(NSA_HPG):
        p_, hf = divmod(i, 2)
        qp = q[:, p_ * LANES:(p_ + 1) * LANES]
        low = _low_half(qp.shape, 1)
        zero = jnp.zeros_like(qp)
        qh = jnp.where(low, zero, qp) if hf else jnp.where(low, qp, zero)
        s = _dot_nt(kc, qh) - tab_ref[0, i:i + 1, 0:1] * dist
        s = jnp.where(valid, s, NEG)
        m = jnp.max(s, axis=0, keepdims=True)
        e = jnp.where(valid, jnp.exp2(s - m), 0.0)
        l = jnp.sum(e, axis=0, keepdims=True)
        p = e * jnp.where(l > 0.0, 1.0 / l, 0.0)
        psum = psum + p
        outs.append(_dot(vct[:HEAD_DIM], p.astype(BF16)))
    _store_heads(o_ref, outs, gates)
    p_hi = psum.astype(BF16)
    r1 = psum - p_hi.astype(F32)
    p_mid = r1.astype(BF16)
    p_lo = (r1 - p_mid.astype(F32)).astype(BF16)
    mt = mt_ref[...]
    imp = _dot(mt, p_hi) + _dot(mt, p_mid) + _dot(mt, p_lo)
    ns = imp.shape[0]
    blk = lax.broadcasted_iota(jnp.int32, (ns, 1), 0)
    cur = t // SLC_BLOCK
    bvalid = blk * SLC_BLOCK <= t
    forced = (blk == 0) | (blk == cur) | (blk == cur - 1)
    score = jnp.where(bvalid, imp + jnp.where(forced, FORCE_BONUS, 0.0), NEG)
    blk_f = blk.astype(F32)
    sel = jnp.zeros((ns, tq), F32)
    for _ in range(SLC_TOPK):
        mx = jnp.max(score, axis=0, keepdims=True)
        first = jnp.min(jnp.where(score == mx, blk_f, float(ns)), axis=0, keepdims=True)
        hit = blk_f == first
        sel = jnp.where(hit, 1.0, sel)
        score = jnp.where(hit, -jnp.inf, score)
    selneg = jnp.where(bvalid & (sel > 0.0), 0.0, -MASK_BIG)
    sel_ref[0, 0] = selneg.T.astype(sel_ref.dtype)


def _cmp_to_slc_t(nseg, ns):
    c0 = np.arange(nseg)[:, None] * CMP_STRIDE
    s0 = np.arange(ns)[None, :] * SLC_BLOCK
    overlap = np.clip(np.minimum(c0 + CMP_BLOCK, s0 + SLC_BLOCK) - np.maximum(c0, s0), 0, None)
    m = overlap / CMP_STRIDE
    m[nseg - 1] = 0.0
    mt = np.zeros((NS_PAD, nseg))
    mt[:ns] = m.T
    return jnp.asarray(mt, BF16)


def _cmp_attention(u3, kcmp, vcmp_t, gt, bg, *, q_col):
    B, S, _ = u3.shape
    nseg = kcmp.shape[2]
    sp = _nsa_specs(q_col)
    return pl.pallas_call(
        _cmp_kernel,
        grid=(B, NSA_GROUPS, S // NSA_TQ),
        in_specs=[
            sp["q"],
            pl.BlockSpec((1, 1, nseg, LANES), lambda b, g, i: (b, g, 0, 0)),
            pl.BlockSpec((1, 1, LANES, nseg), lambda b, g, i: (b, g, 0, 0)),
            pl.BlockSpec((NS_PAD, nseg), lambda b, g, i: (0, 0)),
            sp["tab"], sp["gt"], sp["bg"],
        ],
        out_specs=[sp["out"], pl.BlockSpec((1, 1, NSA_TQ, NS_PAD), lambda b, g, i: (b, g, i, 0))],
        out_shape=[jax.ShapeDtypeStruct((B, S, NSA_WIDTH), F32),
                   jax.ShapeDtypeStruct((B, NSA_GROUPS, S, NS_PAD), BF16)],
        compiler_params=_cp(("parallel", "parallel", "arbitrary")),
        name="cmp_attention",
    )(u3, kcmp, vcmp_t, _cmp_to_slc_t(nseg, S // SLC_BLOCK), _slope_table(), gt, bg)


def _slc_kernel(q_ref, k_ref, vt_ref, sel_ref, e_ref, tab_ref, gt_ref, bg_ref, o_ref, acc_sc):
    qi = pl.program_id(2)
    tq, tk = NSA_TQ, SLC_TK
    t0 = qi * tq
    selneg = sel_ref[0, 0]
    bm = jnp.concatenate([jnp.concatenate([qh, selneg], axis=1)
                          for qh in _nsa_queries(q_ref[0], tab_ref, t0)], axis=0)
    qpos = t0 + lax.broadcasted_iota(jnp.int32, (tk, tq), 1)

    def k_tile(k0):
        return jnp.concatenate([k_ref[0, pl.ds(k0, tk), :], e_ref[pl.ds(k0, tk), :]], axis=1)

    def v_rows(h, k0):
        return vt_ref[0, :, pl.ds(k0, tk)]

    outs = _flash_t(bm, NSA_HPG, tq, tk, k_tile, v_rows, acc_sc, 0, t0 // tk, 1,
                    lambda key, k0: key <= qpos)
    _store_heads(o_ref, outs, _gates_t(gt_ref, bg_ref, 1))


def _win_kernel(q_ref, k_ref, vt_ref, tab_ref, gt_ref, bg_ref, o_ref, acc_sc):
    qi = pl.program_id(2)
    tq, tk = NSA_TQ, WIN_TK
    t0 = qi * tq
    bm = jnp.concatenate(_nsa_queries(q_ref[0], tab_ref, t0), axis=0)
    qpos = t0 + lax.broadcasted_iota(jnp.int32, (tk, tq), 1)
    lo = jnp.maximum(t0 - WINDOW, 0) // tk
    hi = (t0 + tq + tk - 1) // tk
    outs = _flash_t(bm, NSA_HPG, tq, tk, lambda k0: k_ref[0, pl.ds(k0, tk), :],
                    lambda h, k0: vt_ref[0, :, pl.ds(k0, tk)], acc_sc, lo, 0, hi - lo,
                    lambda key, k0: (key <= qpos) & (key > qpos - WINDOW))
    _store_heads(o_ref, outs, _gates_t(gt_ref, bg_ref, 2))


def _block_onehot(S):
    e = (np.arange(S)[:, None] // SLC_BLOCK) == np.arange(NS_PAD)[None, :]
    return jnp.asarray(e, BF16)


def _slc_attention(u3, vt, selneg, gt, bg, *, q_col, k_col, v_row):
    B, S, _ = u3.shape
    sp = _nsa_specs(q_col)
    return pl.pallas_call(
        _slc_kernel,
        grid=(B, NSA_GROUPS, S // NSA_TQ),
        in_specs=[
            sp["q"],
            pl.BlockSpec((1, S, LANES), lambda b, g, i: (b, 0, k_col + g)),
            pl.BlockSpec((1, HEAD_DIM, S), lambda b, g, i: (b, v_row + g, 0)),
            pl.BlockSpec((1, 1, NSA_TQ, NS_PAD), lambda b, g, i: (b, g, i, 0)),
            pl.BlockSpec((S, NS_PAD), lambda b, g, i: (0, 0)),
            sp["tab"], sp["gt"], sp["bg"],
        ],
        out_specs=sp["out"],
        out_shape=jax.ShapeDtypeStruct((B, S, NSA_WIDTH), F32),
        scratch_shapes=[pltpu.VMEM((NSA_HPG, ACC_ROWS, NSA_TQ), F32)],
        compiler_params=_cp(("parallel", "parallel", "arbitrary")),
        name="slc_attention",
    )(u3, u3, vt, selneg, _block_onehot(S), _slope_table(), gt, bg)


def _win_attention(u3, vt, gt, bg, *, q_col, k_col, v_row):
    B, S, _ = u3.shape
    sp = _nsa_specs(q_col)
    return pl.pallas_call(
        _win_kernel,
        grid=(B, NSA_GROUPS, S // NSA_TQ),
        in_specs=[
            sp["q"],
            pl.BlockSpec((1, S, LANES), lambda b, g, i: (b, 0, k_col + g)),
            pl.BlockSpec((1, HEAD_DIM, S), lambda b, g, i: (b, v_row + g, 0)),
            sp["tab"], sp["gt"], sp["bg"],
        ],
        out_specs=sp["out"],
        out_shape=jax.ShapeDtypeStruct((B, S, NSA_WIDTH), F32),
        scratch_shapes=[pltpu.VMEM((NSA_HPG, ACC_ROWS, NSA_TQ), F32)],
        compiler_params=_cp(("parallel", "parallel", "arbitrary")),
        name="win_attention",
    )(u3, u3, vt, _slope_table(), gt, bg)


def _aug_groups(w):
    d = w.shape[0]
    w = w.reshape(d, NSA_GROUPS, HEAD_DIM)
    return jnp.pad(w, ((0, 0), (0, 0), (0, LANES - HEAD_DIM))).reshape(d, NSA_GROUPS * LANES)


def _even_layer(x, norm_g, w_in, b_f, gn_g, w_out):
    B, S, D = x.shape
    qscale = HEAD_DIM ** -0.5 * LOG2E
    q_f, k_f, v_f, w_fl, q_r, k_r, v_r, z = jnp.split(
        w_in, np.cumsum([FOX_WIDTH] * 3 + [FOX_HEADS] + [RET_WIDTH] * 3).tolist(), axis=1)
    w = jnp.concatenate([z, q_f * qscale, k_f, q_r, k_r * HEAD_DIM ** -0.5, v_r], axis=1).astype(BF16)
    x2 = x.reshape(B * S, D)
    u, vt = _proj(x2, norm_g, w, seq=S, w_t=[v_f.T.astype(BF16)], t_dtypes=[BF16])
    u3 = u.reshape(B, S, -1)
    kfeat, qfeat = _fgate(x, norm_g, w_fl, b_f, tile=min(512, S))
    cb = D // LANES
    nb = FOX_WIDTH // LANES
    o_f = _fox(u3, vt, kfeat, qfeat, q_col=cb, k_col=cb + nb)
    o_r = _retention(u3, gn_g, q_col=cb + 2 * nb, k_col=cb + 3 * nb, v_col=cb + 4 * nb)
    out = _out0(o_f.reshape(B * S, -1), o_r.reshape(B * S, -1), u, x2, w_out.astype(BF16))
    return out.reshape(B, S, D)


def _odd_layer(x, norm_g, w_in, b_gate, pe_k, pe_v, wk1, wk2, wv1, wv2, w_out, final_g):
    B, S, D = x.shape
    assert S // SLC_BLOCK <= NS_PAD
    qscale = HEAD_DIM ** -0.5 * LOG2E
    sizes = [NSA_WIDTH] + [NSA_KV_WIDTH] * 6 + [NSA_HEADS * N_BRANCH]
    q, kc, vc, ks, vs, kw, vw, gl, z = jnp.split(w_in, np.cumsum(sizes).tolist(), axis=1)
    w = jnp.concatenate([z, q * qscale, kc, vc, _aug_groups(ks), _aug_groups(kw)], axis=1).astype(BF16)
    per_group = NSA_HPG * N_BRANCH
    glt = jnp.pad(gl.T.reshape(NSA_GROUPS, per_group, D), ((0, 0), (0, GATE_ROWS - per_group), (0, 0)))
    glt = glt.reshape(NSA_GROUPS * GATE_ROWS, D).astype(BF16)
    bg = jnp.pad(b_gate.reshape(NSA_GROUPS, per_group), ((0, 0), (0, GATE_ROWS - per_group)))
    bg = bg.reshape(NSA_GROUPS * GATE_ROWS, 1)
    w_vt = jnp.concatenate([vs, vw], axis=1).T.astype(BF16)
    x2 = x.reshape(B * S, D)
    first_k = (2 * D + 2 * NSA_KV_WIDTH) // PROJ_TN
    u, kc_a, vc_a, vt, gt = _proj(
        x2, norm_g, w, seq=S, split_tile=(2 * D) // PROJ_TN,
        addend=_key_pos_features(S, PROJ_TN), add_tiles=(first_k, first_k + 1),
        w_t=[w_vt, glt], t_dtypes=[BF16, F32])
    u3 = u.reshape(B, S, -1)
    nseg = S // CMP_STRIDE
    kcmp = _compress(kc_a.reshape(B, nseg, -1), pe_k, wk1, wk2, transposed=False)
    vcmp_t = _compress(vc_a.reshape(B, nseg, -1), pe_v, wv1, wv2, transposed=True)
    q_col = D // (NSA_HPG * HEAD_DIM)
    o_c, selneg = _cmp_attention(u3, kcmp, vcmp_t, gt, bg, q_col=q_col)
    kb = (2 * D + 2 * NSA_KV_WIDTH) // LANES
    o_s = _slc_attention(u3, vt, selneg, gt, bg, q_col=q_col, k_col=kb, v_row=0)
    o_w = _win_attention(u3, vt, gt, bg, q_col=q_col, k_col=kb + NSA_GROUPS, v_row=NSA_GROUPS)
    r = lambda a: a.reshape(B * S, -1)
    out = _out1(r(o_c), r(o_s), r(o_w), u, x2, w_out.astype(BF16), final_g)
    return out.reshape(B, S, D)


def kernel(x, even_norm_g, even_w_in, even_b_f, even_gn_g, even_w_out, odd_norm_g, odd_w_in,
           odd_b_gate, odd_pe_k, odd_pe_v, odd_wk1, odd_wk2, odd_wv1, odd_wv2, odd_w_out, final_g):
    x = _even_layer(x, even_norm_g[0], even_w_in[0], even_b_f[0], even_gn_g[0], even_w_out[0])
    return _odd_layer(x, odd_norm_g[0], odd_w_in[0], odd_b_gate[0], odd_pe_k[0], odd_pe_v[0],
                      odd_wk1[0], odd_wk2[0], odd_wv1[0], odd_wv2[0], odd_w_out[0], final_g)
```

```python
import functools
import math

import jax
import jax.numpy as jnp
import numpy as np
from jax import lax
from jax.experimental import pallas as pl
from jax.experimental.pallas import tpu as pltpu

D_MODEL = 1024
HEAD_DIM = 64
LANES = 128
FOX_HEADS = 8
RET_HEADS = 8
FOX_WIDTH = FOX_HEADS * HEAD_DIM
RET_WIDTH = RET_HEADS * HEAD_DIM
RET_CHUNK = 128
NSA_HEADS = 16
NSA_GROUPS = 4
NSA_HPG = NSA_HEADS // NSA_GROUPS
NSA_WIDTH = NSA_HEADS * HEAD_DIM
NSA_KV_WIDTH = NSA_GROUPS * HEAD_DIM
N_BRANCH = 3
GATE_ROWS = 16
CMP_BLOCK = 32
CMP_STRIDE = 16
CMP_HIDDEN = 256
SLC_BLOCK = 64
SLC_TOPK = 16
NS_PAD = LANES
WINDOW = 512
RMS_EPS = 1e-6
GN_EPS = 1e-5
NEG = -1e30
FORCE_BONUS = 1e6
MASK_BIG = 2.0 ** 100
LOG2E = math.log2(math.e)
FEAT0 = HEAD_DIM
ACC_ROWS = HEAD_DIM + 16

PROJ_TM = 512
PROJ_TN = 512
FOX_TQ = 256
FOX_TK = 512
NSA_TQ = 256
SLC_TK = 512
OUT_TM = 512
VMEM_LIMIT = 48 * 1024 * 1024

F32 = jnp.float32
BF16 = jnp.bfloat16


def _cp(sem, vmem=VMEM_LIMIT):
    return pltpu.CompilerParams(dimension_semantics=sem, vmem_limit_bytes=vmem)


def _dot(a, b):
    return jnp.dot(a, b, preferred_element_type=F32)


def _dot_nt(a, b):
    return lax.dot_general(a, b, (((1,), (1,)), ((), ())), preferred_element_type=F32)


def _dot_tn(a, b):
    return lax.dot_general(a, b, (((0,), (0,)), ((), ())), preferred_element_type=F32)


def _rms(x, g):
    return x * lax.rsqrt(jnp.mean(x * x, axis=-1, keepdims=True) + RMS_EPS) * g


def _silu(x):
    return x * (1.0 / (1.0 + jnp.exp(-x)))


def _sigmoid(x):
    return 1.0 / (1.0 + jnp.exp(-x))


def _low_half(shape, axis):
    return lax.broadcasted_iota(jnp.int32, shape, axis) < HEAD_DIM


def _pieces(v):
    p1 = v.astype(BF16).astype(F32)
    r = v - p1
    p2 = r.astype(BF16).astype(F32)
    p3 = (r - p2).astype(BF16).astype(F32)
    return p1, p2, p3


def _np_pieces(v):
    v = np.asarray(v, np.float64)
    bf = lambda a: np.asarray(a, np.float32).astype(BF16).astype(np.float64)
    p1 = bf(v)
    p2 = bf(v - p1)
    p3 = bf(v - p1 - p2)
    return p1, p2, p3


def _place(lane, cols):
    out = jnp.zeros(lane.shape, F32)
    for i, c in cols.items():
        out = jnp.where(lane == i, c, out)
    return out


def _fgate_kernel(x_ref, g_ref, wf_ref, b_ref, kf_ref, qf_ref, carry):
    @pl.when(pl.program_id(1) == 0)
    def _():
        carry[...] = jnp.zeros_like(carry)

    h = _rms(x_ref[0], g_ref[...])
    t = h.shape[0]
    hp = lax.Precision.HIGHEST
    f = jnp.dot(h, wf_ref[...], precision=hp, preferred_element_type=F32) + b_ref[...]
    ls = jnp.minimum(f, 0.0) - jnp.log(1.0 + jnp.exp(-jnp.abs(f)))
    r = lax.broadcasted_iota(jnp.int32, (t, t), 0)
    c = lax.broadcasted_iota(jnp.int32, (t, t), 1)
    lower = (c <= r).astype(F32)
    cs = jnp.dot(lower, ls, precision=hp, preferred_element_type=F32) + carry[...]
    carry[...] = cs[t - 1:t, :]
    cl = cs * LOG2E
    lane = lax.broadcasted_iota(jnp.int32, (t, LANES), 1)
    one = jnp.ones((t, 1), F32)
    for j in range(FOX_HEADS // 2):
        a1, a2, a3 = _pieces(cl[:, 2 * j:2 * j + 1])
        b1, b2, b3 = _pieces(cl[:, 2 * j + 1:2 * j + 2])
        kf = _place(lane, {0: a1, 1: a2, 2: a3, 3: one, 4: one, 5: one, 6: b1, 7: b2, 8: b3})
        kf_ref[0, j] = kf.astype(BF16)
        qf_ref[0, j, 0] = _place(lane, {0: -one, 1: -one, 2: -one, 3: a1, 4: a2, 5: a3}).astype(BF16)
        qf_ref[0, j, 1] = _place(lane, {3: b1, 4: b2, 5: b3, 6: -one, 7: -one, 8: -one}).astype(BF16)


def _fgate(x, g, wf, b_f, *, tile):
    B, S, D = x.shape
    npair = FOX_HEADS // 2
    return pl.pallas_call(
        _fgate_kernel,
        grid=(B, S // tile),
        in_specs=[
            pl.BlockSpec((1, tile, D), lambda b, s: (b, s, 0)),
            pl.BlockSpec((1, D), lambda b, s: (0, 0)),
            pl.BlockSpec((D, FOX_HEADS), lambda b, s: (0, 0)),
            pl.BlockSpec((1, FOX_HEADS), lambda b, s: (0, 0)),
        ],
        out_specs=[
            pl.BlockSpec((1, npair, tile, LANES), lambda b, s: (b, 0, s, 0)),
            pl.BlockSpec((1, npair, 2, tile, LANES), lambda b, s: (b, 0, 0, s, 0)),
        ],
        out_shape=[
            jax.ShapeDtypeStruct((B, npair, S, LANES), BF16),
            jax.ShapeDtypeStruct((B, npair, 2, S, LANES), BF16),
        ],
        scratch_shapes=[pltpu.VMEM((1, FOX_HEADS), F32)],
        compiler_params=_cp(("parallel", "arbitrary")),
        name="fgate",
    )(x, g.reshape(1, D), wf, b_f.reshape(1, FOX_HEADS))


def _proj_kernel(*refs, split_tile, add_tiles, n_t):
    it = iter(refs)
    x_ref, g_ref, w_ref = next(it), next(it), next(it)
    add_ref = next(it) if add_tiles else None
    wt_refs = [next(it) for _ in range(n_t)]
    u_ref = next(it)
    e_refs = [next(it), next(it)] if split_tile is not None else None
    ut_refs = [next(it) for _ in range(n_t)]
    h_sc = next(it)
    j = pl.program_id(1)

    @pl.when(j == 0)
    def _():
        h = _rms(x_ref[...], g_ref[...]).astype(BF16)
        h_sc[...] = h
        for wt_ref, ut_ref in zip(wt_refs, ut_refs):
            ut_ref[0] = _dot_nt(wt_ref[...], h).astype(ut_ref.dtype)

    acc = _dot(h_sc[...], w_ref[...])
    if add_tiles:
        is_add = functools.reduce(jnp.logical_or, [j == t for t in add_tiles])

        @pl.when(is_add)
        def _():
            u_ref[...] = (acc + add_ref[...].astype(F32)).astype(u_ref.dtype)

        @pl.when(jnp.logical_not(is_add))
        def _():
            u_ref[...] = acc.astype(u_ref.dtype)
    else:
        u_ref[...] = acc.astype(u_ref.dtype)
    if split_tile is not None:
        @pl.when(j == split_tile)
        def _():
            half = acc.shape[1] // 2
            e_refs[0][...] = acc[:, :half].astype(e_refs[0].dtype)
            e_refs[1][...] = acc[:, half:].astype(e_refs[1].dtype)


def _proj(x2, g, w, *, seq, split_tile=None, addend=None, add_tiles=(), w_t=(), t_dtypes=(),
          tm=PROJ_TM, tn=PROJ_TN):
    N, D = x2.shape
    W = w.shape[1]
    nbs = seq // tm
    in_specs = [
        pl.BlockSpec((tm, D), lambda i, j: (i, 0)),
        pl.BlockSpec((1, D), lambda i, j: (0, 0)),
        pl.BlockSpec((D, tn), lambda i, j: (0, j)),
    ]
    args = [x2, g.reshape(1, D), w]
    if add_tiles:
        in_specs.append(pl.BlockSpec((tm, tn), lambda i, j: (i % nbs, 0)))
        args.append(addend)
    out_shape = [jax.ShapeDtypeStruct((N, W), BF16)]
    out_specs = [pl.BlockSpec((tm, tn), lambda i, j: (i, j))]
    if split_tile is not None:
        out_shape += [jax.ShapeDtypeStruct((N, tn // 2), BF16)] * 2
        out_specs += [pl.BlockSpec((tm, tn // 2), lambda i, j: (i, 0))] * 2
    for wt, dt in zip(w_t, t_dtypes):
        rows = wt.shape[0]
        in_specs.append(pl.BlockSpec((rows, D), lambda i, j: (0, 0)))
        args.append(wt)
        out_shape.append(jax.ShapeDtypeStruct((N // seq, rows, seq), dt))
        out_specs.append(pl.BlockSpec((1, rows, tm), lambda i, j: (i // nbs, 0, i % nbs)))
    return pl.pallas_call(
        functools.partial(_proj_kernel, split_tile=split_tile, add_tiles=tuple(add_tiles),
                          n_t=len(w_t)),
        grid=(N // tm, W // tn),
        in_specs=in_specs,
        out_specs=out_specs,
        out_shape=out_shape,
        scratch_shapes=[pltpu.VMEM((tm, D), BF16)],
        compiler_params=_cp(("parallel", "arbitrary")),
        name="proj",
    )(*args)


def _flash_t(bm_sc, s_bufs, acc_sc, nh, tq, tk, k_tile, v_rows, n, last, valid_fn):
    acc_sc[...] = jnp.zeros_like(acc_sc)
    ones = jnp.ones((ACC_ROWS - HEAD_DIM, tk), BF16)
    key_iota = lax.broadcasted_iota(jnp.int32, (tk, tq), 0)

    def load_pos(kb):
        return pl.multiple_of(jnp.minimum(kb, last) * tk, tk)

    def qk(kb, buf):
        buf[...] = _dot_nt(k_tile(load_pos(kb)), bm_sc[...])

    def soft(kb, buf, ms, masked):
        if masked:
            valid = valid_fn(kb * tk + key_iota)
        k0 = load_pos(kb)
        out = []
        for h in range(nh):
            cols = slice(h * tq, (h + 1) * tq)
            if masked:
                s = jnp.where(valid, buf[:, cols], NEG)
                m_new = jnp.maximum(ms[h], jnp.max(s, axis=0, keepdims=True))
                p = jnp.exp2(s - m_new).astype(BF16)
            else:
                m_new = jnp.maximum(ms[h], jnp.max(buf[:, cols], axis=0, keepdims=True))
                p = jnp.exp2(buf[:, cols] - m_new).astype(BF16)
            alpha = jnp.exp2(ms[h] - m_new)
            lhs = jnp.concatenate([v_rows(h, k0), ones], axis=0)
            acc_sc[h] = alpha * acc_sc[h] + _dot(lhs, p)
            out.append(m_new)
        return tuple(out)

    def pair(i, ms, masked):
        kb = 2 * i
        qk(kb + 1, s_bufs[1])
        ms = soft(kb, s_bufs[0], ms, masked)
        if not masked:
            qk(kb + 2, s_bufs[0])
        return soft(kb + 1, s_bufs[1], ms, masked)

    qk(0, s_bufs[0])
    ms = (jnp.full((1, tq), NEG, F32),) * nh
    npairs = (n - 1) // 2
    ms = lax.fori_loop(0, npairs, lambda i, c: pair(i, c, False), ms)
    ms = pair(npairs, ms, True)
    outs = []
    for h in range(nh):
        a = acc_sc[h]
        outs.append(a[:HEAD_DIM] * (1.0 / a[HEAD_DIM:HEAD_DIM + 1]))
    return outs


def _flash_scratch(nh, tq, tk, kdim):
    return [pltpu.VMEM((nh * tq, kdim), BF16), pltpu.VMEM((tk, nh * tq), F32),
            pltpu.VMEM((tk, nh * tq), F32), pltpu.VMEM((nh, ACC_ROWS, tq), F32)]


def _fox_kernel(q_ref, qf_ref, k_ref, kf_ref, vt_ref, o_ref, bm_sc, s0_sc, s1_sc, acc_sc):
    qi = pl.program_id(2)
    tq, tk = FOX_TQ, FOX_TK
    q = q_ref[0]
    low = _low_half(q.shape, 1)
    zero = jnp.zeros_like(q)
    bm_sc[...] = jnp.concatenate([
        jnp.concatenate([jnp.where(low, q, zero), qf_ref[0, 0, 0]], axis=1),
        jnp.concatenate([jnp.where(low, zero, q), qf_ref[0, 0, 1]], axis=1)], axis=0)
    qpos = qi * tq + lax.broadcasted_iota(jnp.int32, (tk, tq), 1)

    def k_tile(k0):
        return jnp.concatenate([k_ref[0, pl.ds(k0, tk), :], kf_ref[0, 0, pl.ds(k0, tk), :]], axis=1)

    def v_rows(h, k0):
        return vt_ref[0, h * HEAD_DIM:(h + 1) * HEAD_DIM, pl.ds(k0, tk)]

    outs = _flash_t(bm_sc, (s0_sc, s1_sc), acc_sc, 2, tq, tk, k_tile, v_rows,
                    (qi * tq) // tk + 1, k_ref.shape[1] // tk - 1, lambda key: key <= qpos)
    o_ref[0] = jnp.concatenate(outs, axis=0).T


def _fox(u3, vt, kfeat, qfeat, *, q_col, k_col):
    B, S, _ = u3.shape
    npair = FOX_HEADS // 2
    return pl.pallas_call(
        _fox_kernel,
        grid=(B, npair, S // FOX_TQ),
        in_specs=[
            pl.BlockSpec((1, FOX_TQ, LANES), lambda b, j, i: (b, i, q_col + j)),
            pl.BlockSpec((1, 1, 2, FOX_TQ, LANES), lambda b, j, i: (b, j, 0, i, 0)),
            pl.BlockSpec((1, S, LANES), lambda b, j, i: (b, 0, k_col + j)),
            pl.BlockSpec((1, 1, S, LANES), lambda b, j, i: (b, j, 0, 0)),
            pl.BlockSpec((1, LANES, S), lambda b, j, i: (b, j, 0)),
        ],
        out_specs=pl.BlockSpec((1, FOX_TQ, LANES), lambda b, j, i: (b, i, j)),
        out_shape=jax.ShapeDtypeStruct((B, S, FOX_WIDTH), F32),
        scratch_shapes=_flash_scratch(2, FOX_TQ, FOX_TK, 2 * LANES),
        compiler_params=_cp(("parallel", "parallel", "arbitrary")),
        name="fox",
    )(u3, qfeat, u3, kfeat, vt)


def _ret_kernel(q_ref, k_ref, v_ref, inner_ref, cross_ref, kdec_ref, cd_ref, bd_ref, gn_ref,
                o_ref, state_sc):
    @pl.when(pl.program_id(2) == 0)
    def _():
        state_sc[...] = jnp.zeros_like(state_sc)

    q, k, v = q_ref[0], k_ref[0], v_ref[0]
    low = _low_half(q.shape, 1)
    zero = jnp.zeros_like(q)
    qa, qb = jnp.where(low, q, zero), jnp.where(low, zero, q)
    pa = (_dot_nt(qa, k) * inner_ref[0, 0]).astype(BF16)
    pb = (_dot_nt(qb, k) * inner_ref[0, 1]).astype(BF16)
    o_in = jnp.where(low, _dot(pa, v), _dot(pb, v))
    state = state_sc[...]
    o = o_in + _dot(q, state.astype(BF16)) * cross_ref[0]
    kd = (k.astype(F32) * kdec_ref[0]).astype(BF16)
    state_sc[...] = state * cd_ref[0] + _dot_tn(kd, v) * bd_ref[...]
    inv = 1.0 / HEAD_DIM
    sa = jnp.sum(jnp.where(low, o, 0.0), axis=-1, keepdims=True)
    st = jnp.sum(o, axis=-1, keepdims=True)
    mu = jnp.where(low, sa, st - sa) * inv
    d = o - mu
    d2 = d * d
    va = jnp.sum(jnp.where(low, d2, 0.0), axis=-1, keepdims=True)
    vt = jnp.sum(d2, axis=-1, keepdims=True)
    var = jnp.where(low, va, vt - va) * inv
    o_ref[0] = d * lax.rsqrt(var + GN_EPS) * gn_ref[...]


def _ret_constants():
    lg = np.log(1.0 - 2.0 ** (-5.0 - np.arange(RET_HEADS)))
    i = np.arange(RET_CHUNK)
    diff = i[:, None] - i[None, :]
    inner = np.where(diff[None] >= 0, np.exp(lg[:, None, None] * np.maximum(diff, 0)[None]), 0.0)
    cross = np.exp(lg[:, None] * (i[None, :] + 1))
    kdec = np.exp(lg[:, None] * (RET_CHUNK - 1 - i)[None, :])
    cdec = np.exp(lg * RET_CHUNK)
    npair = RET_HEADS // 2
    inner = inner.reshape(npair, 2, RET_CHUNK, RET_CHUNK)

    def lanes(a):
        a = a.reshape(npair, 2, RET_CHUNK)
        return np.repeat(a.transpose(0, 2, 1), HEAD_DIM, axis=2)

    bd = np.kron(np.eye(2), np.ones((HEAD_DIM, HEAD_DIM)))
    cd = np.repeat(cdec.reshape(npair, 2), HEAD_DIM, axis=1)[:, :, None] * bd[None]
    f = lambda a: jnp.asarray(a, F32)
    return f(inner), f(lanes(cross)), f(lanes(kdec)), f(cd), f(bd)


def _retention(u3, gn_g, *, q_col, k_col, v_col):
    B, S, _ = u3.shape
    C = RET_CHUNK
    npair = RET_HEADS // 2
    inner, cross, kdec, cd, bd = _ret_constants()
    return pl.pallas_call(
        _ret_kernel,
        grid=(B, npair, S // C),
        in_specs=[
            pl.BlockSpec((1, C, LANES), lambda b, j, i: (b, i, q_col + j)),
            pl.BlockSpec((1, C, LANES), lambda b, j, i: (b, i, k_col + j)),
            pl.BlockSpec((1, C, LANES), lambda b, j, i: (b, i, v_col + j)),
            pl.BlockSpec((1, 2, C, C), lambda b, j, i: (j, 0, 0, 0)),
            pl.BlockSpec((1, C, LANES), lambda b, j, i: (j, 0, 0)),
            pl.BlockSpec((1, C, LANES), lambda b, j, i: (j, 0, 0)),
            pl.BlockSpec((1, LANES, LANES), lambda b, j, i: (j, 0, 0)),
            pl.BlockSpec((LANES, LANES), lambda b, j, i: (0, 0)),
            pl.BlockSpec((1, LANES), lambda b, j, i: (0, j)),
        ],
        out_specs=pl.BlockSpec((1, C, LANES), lambda b, j, i: (b, i, j)),
        out_shape=jax.ShapeDtypeStruct((B, S, RET_WIDTH), F32),
        scratch_shapes=[pltpu.VMEM((LANES, LANES), F32)],
        compiler_params=_cp(("parallel", "parallel", "arbitrary")),
        name="retention",
    )(u3, u3, u3, inner, cross, kdec, cd, bd, gn_g.reshape(1, RET_WIDTH))


def _out0_kernel(of_ref, or_ref, z_ref, x_ref, w_ref, o_ref):
    z = _silu(z_ref[...].astype(F32))
    ya = (of_ref[...] * z[:, :FOX_WIDTH]).astype(BF16)
    yb = (or_ref[...] * z[:, FOX_WIDTH:]).astype(BF16)
    o_ref[...] = x_ref[...] + _dot(ya, w_ref[:FOX_WIDTH, :]) + _dot(yb, w_ref[FOX_WIDTH:, :])


def _out0(o_f, o_r, u, x2, w_out, *, tm=OUT_TM):
    N, D = x2.shape
    return pl.pallas_call(
        _out0_kernel,
        grid=(N // tm,),
        in_specs=[
            pl.BlockSpec((tm, FOX_WIDTH), lambda i: (i, 0)),
            pl.BlockSpec((tm, RET_WIDTH), lambda i: (i, 0)),
            pl.BlockSpec((tm, D), lambda i: (i, 0)),
            pl.BlockSpec((tm, D), lambda i: (i, 0)),
            pl.BlockSpec((D, D), lambda i: (0, 0)),
        ],
        out_specs=pl.BlockSpec((tm, D), lambda i: (i, 0)),
        out_shape=jax.ShapeDtypeStruct((N, D), F32),
        compiler_params=_cp(("parallel",)),
        name="out0",
    )(o_f, o_r, u, x2, w_out)


def _out1_kernel(oc_ref, os_ref, ow_ref, z_ref, x_ref, w_ref, g_ref, o_ref):
    z = _silu(z_ref[...].astype(F32))
    y = ((oc_ref[...] + os_ref[...] + ow_ref[...]) * z).astype(BF16)
    o_ref[...] = _rms(x_ref[...] + _dot(y, w_ref[...]), g_ref[...])


def _out1(o_c, o_s, o_w, u, x2, w_out, final_g, *, tm=OUT_TM):
    N, D = x2.shape
    row = pl.BlockSpec((tm, D), lambda i: (i, 0))
    return pl.pallas_call(
        _out1_kernel,
        grid=(N // tm,),
        in_specs=[row, row, row, row, row,
                  pl.BlockSpec((D, D), lambda i: (0, 0)),
                  pl.BlockSpec((1, D), lambda i: (0, 0))],
        out_specs=row,
        out_shape=jax.ShapeDtypeStruct((N, D), F32),
        compiler_params=_cp(("parallel",)),
        name="out1",
    )(o_c, o_s, o_w, u, x2, w_out, final_g.reshape(1, D))


def _compress_kernel(x_ref, pea_ref, peb_ref, wa_ref, wb_ref, w2_ref, o_ref, *, transposed):
    x = x_ref[0].astype(F32)
    a = _dot((x + pea_ref[...]).astype(BF16), wa_ref[0])
    b = _dot((x + peb_ref[...]).astype(BF16), wb_ref[0])
    nseg = x.shape[0]
    pre = a + pltpu.roll(b, nseg - 1, 0)
    hid = _silu(pre).astype(BF16)
    if transposed:
        o_ref[0, 0] = _dot_nt(w2_ref[...], hid).astype(o_ref.dtype)
    else:
        o_ref[0, 0] = _dot(hid, w2_ref[...]).astype(o_ref.dtype)


def _compress(a3, pe, w1, w2, *, transposed):
    B, nseg, wid = a3.shape
    half = CMP_STRIDE * HEAD_DIM
    eye = jnp.eye(NSA_GROUPS, dtype=w1.dtype)

    def big(wh):
        w4 = wh.reshape(CMP_STRIDE, 1, HEAD_DIM, CMP_HIDDEN)
        sel = eye[:, None, :, None, None]
        return (sel * w4[None]).reshape(NSA_GROUPS, wid, CMP_HIDDEN).astype(BF16)

    def pe_big(p):
        return jnp.broadcast_to(p[:, None, :], (CMP_STRIDE, NSA_GROUPS, HEAD_DIM)).reshape(1, wid)

    w2d = jnp.concatenate([w2, w2], axis=1).astype(BF16)
    if transposed:
        w2d = w2d.T
        oshape, oblock = (B, NSA_GROUPS, LANES, nseg), (1, 1, LANES, nseg)
    else:
        oshape, oblock = (B, NSA_GROUPS, nseg, LANES), (1, 1, nseg, LANES)
    return pl.pallas_call(
        functools.partial(_compress_kernel, transposed=transposed),
        grid=(B, NSA_GROUPS),
        in_specs=[
            pl.BlockSpec((1, nseg, wid), lambda b, g: (b, 0, 0)),
            pl.BlockSpec((1, wid), lambda b, g: (0, 0)),
            pl.BlockSpec((1, wid), lambda b, g: (0, 0)),
            pl.BlockSpec((1, wid, CMP_HIDDEN), lambda b, g: (g, 0, 0)),
            pl.BlockSpec((1, wid, CMP_HIDDEN), lambda b, g: (g, 0, 0)),
            pl.BlockSpec(w2d.shape, lambda b, g: (0, 0)),
        ],
        out_specs=pl.BlockSpec(oblock, lambda b, g: (b, g, 0, 0)),
        out_shape=jax.ShapeDtypeStruct(oshape, BF16),
        compiler_params=_cp(("parallel", "parallel")),
        name="compress",
    )(a3, pe_big(pe[:CMP_STRIDE]), pe_big(pe[CMP_STRIDE:]), big(w1[:half]), big(w1[half:]), w2d)


def _slope_table():
    s = np.asarray(2.0 ** (-8.0 * (np.arange(NSA_HEADS) + 1) / NSA_HEADS), np.float32)
    sl = np.asarray(s.astype(np.float64) * LOG2E, np.float32)
    p1, p2, p3 = _np_pieces(sl)
    tab = np.zeros((NSA_HEADS, LANES), np.float32)
    tab[:, 0] = sl
    for k, p in enumerate((p1, p1, p2, p2, p3, p3)):
        tab[:, FEAT0 + k] = p
    tab = tab.reshape(NSA_GROUPS, NSA_HPG, LANES)
    pad = np.zeros((NSA_GROUPS, 8 - NSA_HPG, LANES), np.float32)
    return jnp.asarray(np.concatenate([tab, pad], axis=1))


def _key_pos_features(S, width):
    pos = jnp.arange(S, dtype=jnp.int32)[:, None]
    lane = jnp.arange(width, dtype=jnp.int32)[None, :] % LANES
    hi = ((pos // SLC_BLOCK) * SLC_BLOCK).astype(F32)
    lo = (pos % SLC_BLOCK).astype(F32)
    k = lane - FEAT0
    f = jnp.where((k >= 0) & (k < 6), jnp.where(k % 2 == 0, hi, lo), 0.0)
    f = jnp.where((k >= 6) & (k < 9), 1.0, f)
    return f.astype(BF16)


def _nsa_queries(q, tab_ref, t0):
    tq = q.shape[0]
    lane = lax.broadcasted_iota(jnp.int32, (tq, LANES), 1)
    low = lane < HEAD_DIM
    t = (t0 + lax.broadcasted_iota(jnp.int32, (tq, 1), 0)).astype(F32)
    out = []
    for i in range(NSA_HPG):
        p, hf = divmod(i, 2)
        qp = q[:, p * LANES:(p + 1) * LANES].astype(F32)
        if hf:
            qp = pltpu.roll(qp, HEAD_DIM, 1)
        row = tab_ref[0, i:i + 1, :]
        a1, a2, a3 = _pieces(-(row[:, 0:1] * t))
        feat = jnp.where(lane == FEAT0 + 6, a1,
                         jnp.where(lane == FEAT0 + 7, a2, jnp.where(lane == FEAT0 + 8, a3, row)))
        out.append(jnp.where(low, qp, feat).astype(BF16))
    return out


def _gates_t(gt_ref, bg_ref, branch):
    gl = gt_ref[0] + bg_ref[...]
    return [_sigmoid(gl[N_BRANCH * i + branch:N_BRANCH * i + branch + 1, :]) for i in range(NSA_HPG)]


def _store_heads(o_ref, outs_t, gates):
    g = [o * gt for o, gt in zip(outs_t, gates)]
    o_ref[0, :, :LANES] = jnp.concatenate(g[:2], axis=0).T
    o_ref[0, :, LANES:] = jnp.concatenate(g[2:], axis=0).T


def _nsa_specs(q_col):
    return dict(
        q=pl.BlockSpec((1, NSA_TQ, NSA_HPG * HEAD_DIM), lambda b, g, i: (b, i, q_col + g)),
        tab=pl.BlockSpec((1, 8, LANES), lambda b, g, i: (g, 0, 0)),
        gt=pl.BlockSpec((1, GATE_ROWS, NSA_TQ), lambda b, g, i: (b, g, i)),
        bg=pl.BlockSpec((GATE_ROWS, 1), lambda b, g, i: (g, 0)),
        out=pl.BlockSpec((1, NSA_TQ, NSA_HPG * HEAD_DIM), lambda b, g, i: (b, i, g)),
    )


def _cmp_kernel(q_ref, kc_ref, vct_ref, mt_ref, tab_ref, gt_ref, bg_ref, o_ref, sel_ref):
    qi = pl.program_id(2)
    tq = NSA_TQ
    nseg = kc_ref.shape[2]
    q = q_ref[0]
    kc = kc_ref[0, 0]
    vct = vct_ref[0, 0]
    t = qi * tq + lax.broadcasted_iota(jnp.int32, (1, tq), 1)
    cidx = lax.broadcasted_iota(jnp.int32, (nseg, 1), 0)
    cend = cidx * CMP_STRIDE + (CMP_BLOCK - 1)
    valid = (cend <= t) & (cidx < nseg - 1)
    dist = (t - cend).astype(F32)
    gates = _gates_t(gt_ref, bg_ref, 0)
    psum = jnp.zeros((nseg, tq), F32)
    outs = []
    for i in range(NSA_HPG):
        p_, hf = divmod(i, 2)
        qp = q[:, p_ * LANES:(p_ + 1) * LANES]
        low = _low_half(qp.shape, 1)
        zero = jnp.zeros_like(qp)
        qh = jnp.where(low, zero, qp) if hf else jnp.where(low, qp, zero)
        s = _dot_nt(kc, qh) - tab_ref[0, i:i + 1, 0:1] * dist
        s = jnp.where(valid, s, NEG)
        m = jnp.max(s, axis=0, keepdims=True)
        e = jnp.where(valid, jnp.exp2(s - m), 0.0)
        l = jnp.sum(e, axis=0, keepdims=True)
        p = e * jnp.where(l > 0.0, 1.0 / l, 0.0)
        psum = psum + p
        outs.append(_dot(vct[:HEAD_DIM], p.astype(BF16)))
    _store_heads(o_ref, outs, gates)
    p_hi = psum.astype(BF16)
    r1 = psum - p_hi.astype(F32)
    p_mid = r1.astype(BF16)
    p_lo = (r1 - p_mid.astype(F32)).astype(BF16)
    mt = mt_ref[...]
    imp = _dot(mt, p_hi) + _dot(mt, p_mid) + _dot(mt, p_lo)
    ns = imp.shape[0]
    blk = lax.broadcasted_iota(jnp.int32, (ns, 1), 0)
    cur = t // SLC_BLOCK
    bvalid = blk * SLC_BLOCK <= t
    forced = (blk == 0) | (blk == cur) | (blk == cur - 1)
    score = jnp.where(bvalid, imp + jnp.where(forced, FORCE_BONUS, 0.0), NEG)
    blk_f = blk.astype(F32)
    sel = jnp.zeros((ns, tq), F32)
    for _ in range(SLC_TOPK):
        mx = jnp.max(score, axis=0, keepdims=True)
        first = jnp.min(jnp.where(score == mx, blk_f, float(ns)), axis=0, keepdims=True)
        hit = blk_f == first
        sel = jnp.where(hit, 1.0, sel)
        score = jnp.where(hit, -jnp.inf, score)
    selneg = jnp.where(bvalid & (sel > 0.0), 0.0, -MASK_BIG)
    sel_ref[0, 0] = selneg.T.astype(sel_ref.dtype)


def _cmp_to_slc_t(nseg, ns):
    c0 = np.arange(nseg)[:, None] * CMP_STRIDE
    s0 = np.arange(ns)[None, :] * SLC_BLOCK
    overlap = np.clip(np.minimum(c0 + CMP_BLOCK, s0 + SLC_BLOCK) - np.maximum(c0, s0), 0, None)
    m = overlap / CMP_STRIDE
    m[nseg - 1] = 0.0
    mt = np.zeros((NS_PAD, nseg))
    mt[:ns] = m.T
    return jnp.asarray(mt, BF16)


def _cmp_attention(u3, kcmp, vcmp_t, gt, bg, *, q_col):
    B, S, _ = u3.shape
    nseg = kcmp.shape[2]
    sp = _nsa_specs(q_col)
    return pl.pallas_call(
        _cmp_kernel,
        grid=(B, NSA_GROUPS, S // NSA_TQ),
        in_specs=[
            sp["q"],
            pl.BlockSpec((1, 1, nseg, LANES), lambda b, g, i: (b, g, 0, 0)),
            pl.BlockSpec((1, 1, LANES, nseg), lambda b, g, i: (b, g, 0, 0)),
            pl.BlockSpec((NS_PAD, nseg), lambda b, g, i: (0, 0)),
            sp["tab"], sp["gt"], sp["bg"],
        ],
        out_specs=[sp["out"], pl.BlockSpec((1, 1, NSA_TQ, NS_PAD), lambda b, g, i: (b, g, i, 0))],
        out_shape=[jax.ShapeDtypeStruct((B, S, NSA_WIDTH), F32),
                   jax.ShapeDtypeStruct((B, NSA_GROUPS, S, NS_PAD), BF16)],
        compiler_params=_cp(("parallel", "parallel", "arbitrary")),
        name="cmp_attention",
    )(u3, kcmp, vcmp_t, _cmp_to_slc_t(nseg, S // SLC_BLOCK), _slope_table(), gt, bg)


def _slc_kernel(q_ref, k_ref, vt_ref, sel_ref, e_ref, tab_ref, gt_ref, bg_ref, o_ref,
                bm_sc, s0_sc, s1_sc, acc_sc):
    qi = pl.program_id(2)
    tq, tk = NSA_TQ, SLC_TK
    t0 = qi * tq
    selneg = sel_ref[0, 0]
    bm_sc[...] = jnp.concatenate([jnp.concatenate([qh, selneg], axis=1)
                                  for qh in _nsa_queries(q_ref[0], tab_ref, t0)], axis=0)
    qpos = t0 + lax.broadcasted_iota(jnp.int32, (tk, tq), 1)

    def k_tile(k0):
        return jnp.concatenate([k_ref[0, pl.ds(k0, tk), :], e_ref[pl.ds(k0, tk), :]], axis=1)

    def v_rows(h, k0):
        return vt_ref[0, :, pl.ds(k0, tk)]

    outs = _flash_t(bm_sc, (s0_sc, s1_sc), acc_sc, NSA_HPG, tq, tk, k_tile, v_rows,
                    t0 // tk + 1, k_ref.shape[1] // tk - 1, lambda key: key <= qpos)
    _store_heads(o_ref, outs, _gates_t(gt_ref, bg_ref, 1))


def _win_kernel(q_ref, k_ref, vt_ref, tab_ref, gt_ref, bg_ref, o_ref):
    qi = pl.program_id(2)
    tq = NSA_TQ
    nk = WINDOW + tq
    t0 = qi * tq
    start = pl.multiple_of(jnp.maximum(t0 - WINDOW, 0), LANES)
    bm = jnp.concatenate(_nsa_queries(q_ref[0], tab_ref, t0), axis=0)
    s_all = _dot_nt(k_ref[0, pl.ds(start, nk), :], bm)
    key = start + lax.broadcasted_iota(jnp.int32, (nk, tq), 0)
    qpos = t0 + lax.broadcasted_iota(jnp.int32, (nk, tq), 1)
    valid = (key <= qpos) & (key > qpos - WINDOW)
    lhs = jnp.concatenate([vt_ref[0, :, pl.ds(start, nk)],
                           jnp.ones((ACC_ROWS - HEAD_DIM, nk), BF16)], axis=0)
    outs = []
    for h in range(NSA_HPG):
        s = jnp.where(valid, s_all[:, h * tq:(h + 1) * tq], NEG)
        p = jnp.exp2(s - jnp.max(s, axis=0, keepdims=True)).astype(BF16)
        a = _dot(lhs, p)
        outs.append(a[:HEAD_DIM] * (1.0 / a[HEAD_DIM:HEAD_DIM + 1]))
    _store_heads(o_ref, outs, _gates_t(gt_ref, bg_ref, 2))


def _block_onehot(S):
    e = (np.arange(S)[:, None] // SLC_BLOCK) == np.arange(NS_PAD)[None, :]
    return jnp.asarray(e, BF16)


def _slc_attention(u3, vt, selneg, gt, bg, *, q_col, k_col, v_row):
    B, S, _ = u3.shape
    sp = _nsa_specs(q_col)
    return pl.pallas_call(
        _slc_kernel,
        grid=(B, NSA_GROUPS, S // NSA_TQ),
        in_specs=[
            sp["q"],
            pl.BlockSpec((1, S, LANES), lambda b, g, i: (b, 0, k_col + g)),
            pl.BlockSpec((1, HEAD_DIM, S), lambda b, g, i: (b, v_row + g, 0)),
            pl.BlockSpec((1, 1, NSA_TQ, NS_PAD), lambda b, g, i: (b, g, i, 0)),
            pl.BlockSpec((S, NS_PAD), lambda b, g, i: (0, 0)),
            sp["tab"], sp["gt"], sp["bg"],
        ],
        out_specs=sp["out"],
        out_shape=jax.ShapeDtypeStruct((B, S, NSA_WIDTH), F32),
        scratch_shapes=_flash_scratch(NSA_HPG, NSA_TQ, SLC_TK, 2 * LANES),
        compiler_params=_cp(("parallel", "parallel", "arbitrary")),
        name="slc_attention",
    )(u3, u3, vt, selneg, _block_onehot(S), _slope_table(), gt, bg)


def _win_attention(u3, vt, gt, bg, *, q_col, k_col, v_row):
    B, S, _ = u3.shape
    sp = _nsa_specs(q_col)
    return pl.pallas_call(
        _win_kernel,
        grid=(B, NSA_GROUPS, S // NSA_TQ),
        in_specs=[
            sp["q"],
            pl.BlockSpec((1, S, LANES), lambda b, g, i: (b, 0, k_col + g)),
            pl.BlockSpec((1, HEAD_DIM, S), lambda b, g, i: (b, v_row + g, 0)),
            sp["tab"], sp["gt"], sp["bg"],
        ],
        out_specs=sp["out"],
        out_shape=jax.ShapeDtypeStruct((B, S, NSA_WIDTH), F32),
        compiler_params=_cp(("parallel", "parallel", "arbitrary")),
        name="win_attention",
    )(u3, u3, vt, _slope_table(), gt, bg)


def _aug_groups(w):
    d = w.shape[0]
    w = w.reshape(d, NSA_GROUPS, HEAD_DIM)
    return jnp.pad(w, ((0, 0), (0, 0), (0, LANES - HEAD_DIM))).reshape(d, NSA_GROUPS * LANES)


def _even_layer(x, norm_g, w_in, b_f, gn_g, w_out):
    B, S, D = x.shape
    qscale = HEAD_DIM ** -0.5 * LOG2E
    q_f, k_f, v_f, w_fl, q_r, k_r, v_r, z = jnp.split(
        w_in, np.cumsum([FOX_WIDTH] * 3 + [FOX_HEADS] + [RET_WIDTH] * 3).tolist(), axis=1)
    w = jnp.concatenate([z, q_f * qscale, k_f, q_r, k_r * HEAD_DIM ** -0.5, v_r], axis=1).astype(BF16)
    x2 = x.reshape(B * S, D)
    u, vt = _proj(x2, norm_g, w, seq=S, w_t=[v_f.T.astype(BF16)], t_dtypes=[BF16])
    u3 = u.reshape(B, S, -1)
    kfeat, qfeat = _fgate(x, norm_g, w_fl, b_f, tile=min(512, S))
    cb = D // LANES
    nb = FOX_WIDTH // LANES
    o_f = _fox(u3, vt, kfeat, qfeat, q_col=cb, k_col=cb + nb)
    o_r = _retention(u3, gn_g, q_col=cb + 2 * nb, k_col=cb + 3 * nb, v_col=cb + 4 * nb)
    out = _out0(o_f.reshape(B * S, -1), o_r.reshape(B * S, -1), u, x2, w_out.astype(BF16))
    return out.reshape(B, S, D)


def _odd_layer(x, norm_g, w_in, b_gate, pe_k, pe_v, wk1, wk2, wv1, wv2, w_out, final_g):
    B, S, D = x.shape
    assert S // SLC_BLOCK <= NS_PAD
    qscale = HEAD_DIM ** -0.5 * LOG2E
    sizes = [NSA_WIDTH] + [NSA_KV_WIDTH] * 6 + [NSA_HEADS * N_BRANCH]
    q, kc, vc, ks, vs, kw, vw, gl, z = jnp.split(w_in, np.cumsum(sizes).tolist(), axis=1)
    w = jnp.concatenate([z, q * qscale, kc, vc, _aug_groups(ks), _aug_groups(kw)], axis=1).astype(BF16)
    per_group = NSA_HPG * N_BRANCH
    glt = jnp.pad(gl.T.reshape(NSA_GROUPS, per_group, D), ((0, 0), (0, GATE_ROWS - per_group), (0, 0)))
    glt = glt.reshape(NSA_GROUPS * GATE_ROWS, D).astype(BF16)
    bg = jnp.pad(b_gate.reshape(NSA_GROUPS, per_group), ((0, 0), (0, GATE_ROWS - per_group)))
    bg = bg.reshape(NSA_GROUPS * GATE_ROWS, 1)
    w_vt = jnp.concatenate([vs, vw], axis=1).T.astype(BF16)
    x2 = x.reshape(B * S, D)
    first_k = (2 * D + 2 * NSA_KV_WIDTH) // PROJ_TN
    u, kc_a, vc_a, vt, gt = _proj(
        x2, norm_g, w, seq=S, split_tile=(2 * D) // PROJ_TN,
        addend=_key_pos_features(S, PROJ_TN), add_tiles=(first_k, first_k + 1),
        w_t=[w_vt, glt], t_dtypes=[BF16, F32])
    u3 = u.reshape(B, S, -1)
    nseg = S // CMP_STRIDE
    kcmp = _compress(kc_a.reshape(B, nseg, -1), pe_k, wk1, wk2, transposed=False)
    vcmp_t = _compress(vc_a.reshape(B, nseg, -1), pe_v, wv1, wv2, transposed=True)
    q_col = D // (NSA_HPG * HEAD_DIM)
    o_c, selneg = _cmp_attention(u3, kcmp, vcmp_t, gt, bg, q_col=q_col)
    kb = (2 * D + 2 * NSA_KV_WIDTH) // LANES
    o_s = _slc_attention(u3, vt, selneg, gt, bg, q_col=q_col, k_col=kb, v_row=0)
    o_w = _win_attention(u3, vt, gt, bg, q_col=q_col, k_col=kb + NSA_GROUPS, v_row=NSA_GROUPS)
    r = lambda a: a.reshape(B * S, -1)
    out = _out1(r(o_c), r(o_s), r(o_w), u, x2, w_out.astype(BF16), final_g)
    return out.reshape(B, S, D)


def kernel(x, even_norm_g, even_w_in, even_b_f, even_gn_g, even_w_out, odd_norm_g, odd_w_in,
           odd_b_gate, odd_pe_k, odd_pe_v, odd_wk1, odd_wk2, odd_wv1, odd_wv2, odd_w_out, final_g):
    x = _even_layer(x, even_norm_g[0], even_w_in[0], even_b_f[0], even_gn_g[0], even_w_out[0])
    return _odd_layer(x, odd_norm_g[0], odd_w_in[0], odd_b_gate[0], odd_pe_k[0], odd_pe_v[0],
                      odd_wk1[0], odd_wk2[0], odd_wv1[0], odd_wv2[0], odd_w_out[0], final_g)
```

```python
import functools
import math

import jax
import jax.numpy as jnp
import numpy as np
from jax import lax
from jax.experimental import pallas as pl
from jax.experimental.pallas import tpu as pltpu

D_MODEL = 1024
HEAD_DIM = 64
LANES = 128
FOX_HEADS = 8
RET_HEADS = 8
FOX_WIDTH = FOX_HEADS * HEAD_DIM
RET_WIDTH = RET_HEADS * HEAD_DIM
RET_CHUNK = 128
NSA_HEADS = 16
NSA_GROUPS = 4
NSA_HPG = NSA_HEADS // NSA_GROUPS
NSA_WIDTH = NSA_HEADS * HEAD_DIM
NSA_KV_WIDTH = NSA_GROUPS * HEAD_DIM
N_BRANCH = 3
GATE_ROWS = 16
CMP_BLOCK = 32
CMP_STRIDE = 16
CMP_HIDDEN = 256
SLC_BLOCK = 64
SLC_TOPK = 16
N_FORCED = 3
NS_PAD = LANES
WINDOW = 512
RMS_EPS = 1e-6
GN_EPS = 1e-5
NEG = -1e30
FORCE_BONUS = 1e6
MASK_BIG = 2.0 ** 100
LOG2E = math.log2(math.e)
FEAT0 = HEAD_DIM
ACC_ROWS = HEAD_DIM + 16

PROJ_TM = 1024
PROJ_TN = 512
FOX_TQ = 512
FOX_TK = 512
NSA_TQ = 256
SLC_TK = 512
OUT_TM = 512
VMEM_LIMIT = 48 * 1024 * 1024

F32 = jnp.float32
BF16 = jnp.bfloat16


def _cp(sem, vmem=VMEM_LIMIT):
    return pltpu.CompilerParams(dimension_semantics=sem, vmem_limit_bytes=vmem)


def _dot(a, b):
    return jnp.dot(a, b, preferred_element_type=F32)


def _dot_nt(a, b):
    return lax.dot_general(a, b, (((1,), (1,)), ((), ())), preferred_element_type=F32)


def _dot_tn(a, b):
    return lax.dot_general(a, b, (((0,), (0,)), ((), ())), preferred_element_type=F32)


def _rms(x, g):
    return x * lax.rsqrt(jnp.mean(x * x, axis=-1, keepdims=True) + RMS_EPS) * g


def _silu(x):
    return x * (1.0 / (1.0 + jnp.exp(-x)))


def _sigmoid(x):
    return 1.0 / (1.0 + jnp.exp(-x))


def _low_half(shape, axis):
    return lax.broadcasted_iota(jnp.int32, shape, axis) < HEAD_DIM


def _pieces(v):
    p1 = v.astype(BF16).astype(F32)
    r = v - p1
    p2 = r.astype(BF16).astype(F32)
    p3 = (r - p2).astype(BF16).astype(F32)
    return p1, p2, p3


def _np_pieces(v):
    v = np.asarray(v, np.float64)
    bf = lambda a: np.asarray(a, np.float32).astype(BF16).astype(np.float64)
    p1 = bf(v)
    p2 = bf(v - p1)
    p3 = bf(v - p1 - p2)
    return p1, p2, p3


def _place(lane, cols):
    out = jnp.zeros(lane.shape, F32)
    for i, c in cols.items():
        out = jnp.where(lane == i, c, out)
    return out


def _fgate_kernel(x_ref, g_ref, wf_ref, b_ref, kf_ref, qf_ref, carry):
    @pl.when(pl.program_id(1) == 0)
    def _():
        carry[...] = jnp.zeros_like(carry)

    h = _rms(x_ref[0], g_ref[...])
    t = h.shape[0]
    h1 = h.astype(BF16)
    h2 = (h - h1.astype(F32)).astype(BF16)
    w = wf_ref[...]
    w1 = w.astype(BF16)
    w2 = (w - w1.astype(F32)).astype(BF16)
    f = _dot(h1, w1) + _dot(h1, w2) + _dot(h2, w1) + b_ref[...]
    ls = jnp.minimum(f, 0.0) - jnp.log(1.0 + jnp.exp(-jnp.abs(f)))
    r = lax.broadcasted_iota(jnp.int32, (t, t), 0)
    c = lax.broadcasted_iota(jnp.int32, (t, t), 1)
    lower = jnp.where(c <= r, 1.0, 0.0).astype(BF16)
    cs = sum(_dot(lower, p.astype(BF16)) for p in _pieces(ls)) + carry[...]
    carry[...] = cs[t - 1:t, :]
    cl = cs * LOG2E
    lane = lax.broadcasted_iota(jnp.int32, (t, LANES), 1)
    one = jnp.ones((t, 1), F32)
    for j in range(FOX_HEADS // 2):
        a1, a2, a3 = _pieces(cl[:, 2 * j:2 * j + 1])
        b1, b2, b3 = _pieces(cl[:, 2 * j + 1:2 * j + 2])
        kf = _place(lane, {0: a1, 1: a2, 2: a3, 3: one, 4: one, 5: one, 6: b1, 7: b2, 8: b3})
        kf_ref[0, j] = kf.astype(BF16)
        qf_ref[0, j, 0] = _place(lane, {0: -one, 1: -one, 2: -one, 3: a1, 4: a2, 5: a3}).astype(BF16)
        qf_ref[0, j, 1] = _place(lane, {3: b1, 4: b2, 5: b3, 6: -one, 7: -one, 8: -one}).astype(BF16)


def _fgate(x, g, wf, b_f, *, tile):
    B, S, D = x.shape
    npair = FOX_HEADS // 2
    return pl.pallas_call(
        _fgate_kernel,
        grid=(B, S // tile),
        in_specs=[
            pl.BlockSpec((1, tile, D), lambda b, s: (b, s, 0)),
            pl.BlockSpec((1, D), lambda b, s: (0, 0)),
            pl.BlockSpec((D, FOX_HEADS), lambda b, s: (0, 0)),
            pl.BlockSpec((1, FOX_HEADS), lambda b, s: (0, 0)),
        ],
        out_specs=[
            pl.BlockSpec((1, npair, tile, LANES), lambda b, s: (b, 0, s, 0)),
            pl.BlockSpec((1, npair, 2, tile, LANES), lambda b, s: (b, 0, 0, s, 0)),
        ],
        out_shape=[
            jax.ShapeDtypeStruct((B, npair, S, LANES), BF16),
            jax.ShapeDtypeStruct((B, npair, 2, S, LANES), BF16),
        ],
        scratch_shapes=[pltpu.VMEM((1, FOX_HEADS), F32)],
        compiler_params=_cp(("parallel", "arbitrary")),
        name="fgate",
    )(x, g.reshape(1, D), wf, b_f.reshape(1, FOX_HEADS))


def _proj_kernel(*refs, split_tile, add_tiles, n_t):
    it = iter(refs)
    x_ref, g_ref, w_ref = next(it), next(it), next(it)
    add_ref = next(it) if add_tiles else None
    wt_refs = [next(it) for _ in range(n_t)]
    u_ref = next(it)
    e_refs = [next(it), next(it)] if split_tile is not None else None
    ut_refs = [next(it) for _ in range(n_t)]
    h_sc = next(it)
    j = pl.program_id(1)

    @pl.when(j == 0)
    def _():
        h = _rms(x_ref[...], g_ref[...]).astype(BF16)
        h_sc[...] = h
        for wt_ref, ut_ref in zip(wt_refs, ut_refs):
            ut_ref[0] = _dot_nt(wt_ref[...], h).astype(ut_ref.dtype)

    acc = _dot(h_sc[...], w_ref[...])
    if add_tiles:
        is_add = functools.reduce(jnp.logical_or, [j == t for t in add_tiles])

        @pl.when(is_add)
        def _():
            u_ref[...] = (acc + add_ref[...].astype(F32)).astype(u_ref.dtype)

        @pl.when(jnp.logical_not(is_add))
        def _():
            u_ref[...] = acc.astype(u_ref.dtype)
    else:
        u_ref[...] = acc.astype(u_ref.dtype)
    if split_tile is not None:
        @pl.when(j == split_tile)
        def _():
            half = acc.shape[1] // 2
            e_refs[0][...] = acc[:, :half].astype(e_refs[0].dtype)
            e_refs[1][...] = acc[:, half:].astype(e_refs[1].dtype)


def _proj(x2, g, w, *, seq, split_tile=None, addend=None, add_tiles=(), w_t=(), t_dtypes=(),
          tm=PROJ_TM, tn=PROJ_TN):
    N, D = x2.shape
    W = w.shape[1]
    nbs = seq // tm
    in_specs = [
        pl.BlockSpec((tm, D), lambda i, j: (i, 0)),
        pl.BlockSpec((1, D), lambda i, j: (0, 0)),
        pl.BlockSpec((D, tn), lambda i, j: (0, j)),
    ]
    args = [x2, g.reshape(1, D), w]
    if add_tiles:
        in_specs.append(pl.BlockSpec((tm, tn), lambda i, j: (i % nbs, 0)))
        args.append(addend)
    out_shape = [jax.ShapeDtypeStruct((N, W), BF16)]
    out_specs = [pl.BlockSpec((tm, tn), lambda i, j: (i, j))]
    if split_tile is not None:
        out_shape += [jax.ShapeDtypeStruct((N, tn // 2), BF16)] * 2
        out_specs += [pl.BlockSpec((tm, tn // 2), lambda i, j: (i, 0))] * 2
    for wt, dt in zip(w_t, t_dtypes):
        rows = wt.shape[0]
        in_specs.append(pl.BlockSpec((rows, D), lambda i, j: (0, 0)))
        args.append(wt)
        out_shape.append(jax.ShapeDtypeStruct((N // seq, rows, seq), dt))
        out_specs.append(pl.BlockSpec((1, rows, tm), lambda i, j: (i // nbs, 0, i % nbs)))
    return pl.pallas_call(
        functools.partial(_proj_kernel, split_tile=split_tile, add_tiles=tuple(add_tiles),
                          n_t=len(w_t)),
        grid=(N // tm, W // tn),
        in_specs=in_specs,
        out_specs=out_specs,
        out_shape=out_shape,
        scratch_shapes=[pltpu.VMEM((tm, D), BF16)],
        compiler_params=_cp(("parallel", "arbitrary")),
        name="proj",
    )(*args)


def _flash_t(bm_sc, s_bufs, acc_sc, nh, tq, tk, k_tile, v_rows, n, last, valid_fn):
    acc_sc[...] = jnp.zeros_like(acc_sc)
    ones = jnp.ones((ACC_ROWS - HEAD_DIM, tk), BF16)
    key_iota = lax.broadcasted_iota(jnp.int32, (tk, tq), 0)

    def load_pos(kb):
        return pl.multiple_of(jnp.minimum(kb, last) * tk, tk)

    def qk(kb, buf):
        buf[...] = _dot_nt(k_tile(load_pos(kb)), bm_sc[...])

    def soft(kb, buf, ms, masked):
        if masked:
            valid = valid_fn(kb * tk + key_iota)
        k0 = load_pos(kb)
        out = []
        for h in range(nh):
            cols = slice(h * tq, (h + 1) * tq)
            if masked:
                s = jnp.where(valid, buf[:, cols], NEG)
                m_new = jnp.maximum(ms[h], jnp.max(s, axis=0, keepdims=True))
                p = jnp.exp2(s - m_new).astype(BF16)
            else:
                m_new = jnp.maximum(ms[h], jnp.max(buf[:, cols], axis=0, keepdims=True))
                p = jnp.exp2(buf[:, cols] - m_new).astype(BF16)
            alpha = jnp.exp2(ms[h] - m_new)
            lhs = jnp.concatenate([v_rows(h, k0), ones], axis=0)
            acc_sc[h] = alpha * acc_sc[h] + _dot(lhs, p)
            out.append(m_new)
        return tuple(out)

    def pair(i, ms, masked):
        kb = 2 * i
        qk(kb + 1, s_bufs[1])
        ms = soft(kb, s_bufs[0], ms, masked)
        if not masked:
            qk(kb + 2, s_bufs[0])
        return soft(kb + 1, s_bufs[1], ms, masked)

    qk(0, s_bufs[0])
    ms = (jnp.full((1, tq), NEG, F32),) * nh
    npairs = (n - 1) // 2
    ms = lax.fori_loop(0, npairs, lambda i, c: pair(i, c, False), ms)
    ms = pair(npairs, ms, True)
    outs = []
    for h in range(nh):
        a = acc_sc[h]
        outs.append(a[:HEAD_DIM] * (1.0 / a[HEAD_DIM:HEAD_DIM + 1]))
    return outs


def _flash_scratch(nh, tq, tk, kdim):
    return [pltpu.VMEM((nh * tq, kdim), BF16), pltpu.VMEM((tk, nh * tq), F32),
            pltpu.VMEM((tk, nh * tq), F32), pltpu.VMEM((nh, ACC_ROWS, tq), F32)]


def _fox_kernel(q_ref, qf_ref, k_ref, kf_ref, vt_ref, o_ref, bm_sc, s0_sc, s1_sc, acc_sc):
    qi = pl.program_id(2)
    tq, tk = FOX_TQ, FOX_TK
    q = q_ref[0]
    low = _low_half(q.shape, 1)
    zero = jnp.zeros_like(q)
    bm_sc[...] = jnp.concatenate([
        jnp.concatenate([jnp.where(low, q, zero), qf_ref[0, 0, 0]], axis=1),
        jnp.concatenate([jnp.where(low, zero, q), qf_ref[0, 0, 1]], axis=1)], axis=0)
    qpos = qi * tq + lax.broadcasted_iota(jnp.int32, (tk, tq), 1)

    def k_tile(k0):
        return jnp.concatenate([k_ref[0, pl.ds(k0, tk), :], kf_ref[0, 0, pl.ds(k0, tk), :]], axis=1)

    def v_rows(h, k0):
        return vt_ref[0, h * HEAD_DIM:(h + 1) * HEAD_DIM, pl.ds(k0, tk)]

    outs = _flash_t(bm_sc, (s0_sc, s1_sc), acc_sc, 2, tq, tk, k_tile, v_rows,
                    (qi * tq) // tk + 1, k_ref.shape[1] // tk - 1, lambda key: key <= qpos)
    o_ref[0] = jnp.concatenate(outs, axis=0).T


def _fox(u3, vt, kfeat, qfeat, *, q_col, k_col):
    B, S, _ = u3.shape
    npair = FOX_HEADS // 2
    return pl.pallas_call(
        _fox_kernel,
        grid=(B, npair, S // FOX_TQ),
        in_specs=[
            pl.BlockSpec((1, FOX_TQ, LANES), lambda b, j, i: (b, i, q_col + j)),
            pl.BlockSpec((1, 1, 2, FOX_TQ, LANES), lambda b, j, i: (b, j, 0, i, 0)),
            pl.BlockSpec((1, S, LANES), lambda b, j, i: (b, 0, k_col + j)),
            pl.BlockSpec((1, 1, S, LANES), lambda b, j, i: (b, j, 0, 0)),
            pl.BlockSpec((1, LANES, S), lambda b, j, i: (b, j, 0)),
        ],
        out_specs=pl.BlockSpec((1, FOX_TQ, LANES), lambda b, j, i: (b, i, j)),
        out_shape=jax.ShapeDtypeStruct((B, S, FOX_WIDTH), F32),
        scratch_shapes=_flash_scratch(2, FOX_TQ, FOX_TK, 2 * LANES),
        compiler_params=_cp(("parallel", "parallel", "arbitrary")),
        name="fox",
    )(u3, qfeat, u3, kfeat, vt)


def _ret_kernel(q_ref, k_ref, v_ref, inner_ref, cross_ref, kdec_ref, cd_ref, bd_ref, gn_ref,
                o_ref, state_sc):
    @pl.when(pl.program_id(1) == 0)
    def _():
        state_sc[...] = jnp.zeros_like(state_sc)

    low = _low_half((RET_CHUNK, LANES), 1)
    inv = 1.0 / HEAD_DIM
    for j in range(RET_HEADS // 2):
        cols = slice(j * LANES, (j + 1) * LANES)
        q, k, v = q_ref[0, :, cols], k_ref[0, :, cols], v_ref[0, :, cols]
        zero = jnp.zeros_like(q)
        qa, qb = jnp.where(low, q, zero), jnp.where(low, zero, q)
        pa = (_dot_nt(qa, k) * inner_ref[j, 0]).astype(BF16)
        pb = (_dot_nt(qb, k) * inner_ref[j, 1]).astype(BF16)
        o_in = jnp.where(low, _dot(pa, v), _dot(pb, v))
        state = state_sc[j]
        o = o_in + _dot(q, state.astype(BF16)) * cross_ref[j]
        kd = (k.astype(F32) * kdec_ref[j]).astype(BF16)
        state_sc[j] = state * cd_ref[j] + _dot_tn(kd, v) * bd_ref[...]
        sa = jnp.sum(jnp.where(low, o, 0.0), axis=-1, keepdims=True)
        st = jnp.sum(o, axis=-1, keepdims=True)
        mu = jnp.where(low, sa, st - sa) * inv
        d = o - mu
        d2 = d * d
        va = jnp.sum(jnp.where(low, d2, 0.0), axis=-1, keepdims=True)
        vt = jnp.sum(d2, axis=-1, keepdims=True)
        var = jnp.where(low, va, vt - va) * inv
        o_ref[0, :, cols] = d * lax.rsqrt(var + GN_EPS) * gn_ref[:, cols]


def _ret_constants():
    lg = np.log(1.0 - 2.0 ** (-5.0 - np.arange(RET_HEADS)))
    i = np.arange(RET_CHUNK)
    diff = i[:, None] - i[None, :]
    inner = np.where(diff[None] >= 0, np.exp(lg[:, None, None] * np.maximum(diff, 0)[None]), 0.0)
    cross = np.exp(lg[:, None] * (i[None, :] + 1))
    kdec = np.exp(lg[:, None] * (RET_CHUNK - 1 - i)[None, :])
    cdec = np.exp(lg * RET_CHUNK)
    npair = RET_HEADS // 2
    inner = inner.reshape(npair, 2, RET_CHUNK, RET_CHUNK)

    def lanes(a):
        a = a.reshape(npair, 2, RET_CHUNK)
        return np.repeat(a.transpose(0, 2, 1), HEAD_DIM, axis=2)

    bd = np.kron(np.eye(2), np.ones((HEAD_DIM, HEAD_DIM)))
    cd = np.repeat(cdec.reshape(npair, 2), HEAD_DIM, axis=1)[:, :, None] * bd[None]
    f = lambda a: jnp.asarray(a, F32)
    return f(inner), f(lanes(cross)), f(lanes(kdec)), f(cd), f(bd)


def _retention(u3, gn_g, *, q_col, k_col, v_col):
    B, S, _ = u3.shape
    C = RET_CHUNK
    npair = RET_HEADS // 2
    inner, cross, kdec, cd, bd = _ret_constants()
    full = lambda shape: pl.BlockSpec(shape, lambda b, i: (0,) * len(shape))
    return pl.pallas_call(
        _ret_kernel,
        grid=(B, S // C),
        in_specs=[
            pl.BlockSpec((1, C, RET_WIDTH), lambda b, i: (b, i, q_col)),
            pl.BlockSpec((1, C, RET_WIDTH), lambda b, i: (b, i, k_col)),
            pl.BlockSpec((1, C, RET_WIDTH), lambda b, i: (b, i, v_col)),
            full((npair, 2, C, C)), full((npair, C, LANES)), full((npair, C, LANES)),
            full((npair, LANES, LANES)), full((LANES, LANES)), full((1, RET_WIDTH)),
        ],
        out_specs=pl.BlockSpec((1, C, RET_WIDTH), lambda b, i: (b, i, 0)),
        out_shape=jax.ShapeDtypeStruct((B, S, RET_WIDTH), F32),
        scratch_shapes=[pltpu.VMEM((npair, LANES, LANES), F32)],
        compiler_params=_cp(("parallel", "arbitrary")),
        name="retention",
    )(u3, u3, u3, inner, cross, kdec, cd, bd, gn_g.reshape(1, RET_WIDTH))


def _out0_kernel(of_ref, or_ref, z_ref, x_ref, w_ref, o_ref):
    z = _silu(z_ref[...].astype(F32))
    ya = (of_ref[...] * z[:, :FOX_WIDTH]).astype(BF16)
    yb = (or_ref[...] * z[:, FOX_WIDTH:]).astype(BF16)
    o_ref[...] = x_ref[...] + _dot(ya, w_ref[:FOX_WIDTH, :]) + _dot(yb, w_ref[FOX_WIDTH:, :])


def _out0(o_f, o_r, u, x2, w_out, *, tm=OUT_TM):
    N, D = x2.shape
    return pl.pallas_call(
        _out0_kernel,
        grid=(N // tm,),
        in_specs=[
            pl.BlockSpec((tm, FOX_WIDTH), lambda i: (i, 0)),
            pl.BlockSpec((tm, RET_WIDTH), lambda i: (i, 0)),
            pl.BlockSpec((tm, D), lambda i: (i, 0)),
            pl.BlockSpec((tm, D), lambda i: (i, 0)),
            pl.BlockSpec((D, D), lambda i: (0, 0)),
        ],
        out_specs=pl.BlockSpec((tm, D), lambda i: (i, 0)),
        out_shape=jax.ShapeDtypeStruct((N, D), F32),
        compiler_params=_cp(("parallel",)),
        name="out0",
    )(o_f, o_r, u, x2, w_out)


def _out1_kernel(oc_ref, os_ref, ow_ref, z_ref, x_ref, w_ref, g_ref, o_ref):
    z = _silu(z_ref[...].astype(F32))
    y = ((oc_ref[...] + os_ref[...] + ow_ref[...]) * z).astype(BF16)
    o_ref[...] = _rms(x_ref[...] + _dot(y, w_ref[...]), g_ref[...])


def _out1(o_c, o_s, o_w, u, x2, w_out, final_g, *, tm=OUT_TM):
    N, D = x2.shape
    row = pl.BlockSpec((tm, D), lambda i: (i, 0))
    return pl.pallas_call(
        _out1_kernel,
        grid=(N // tm,),
        in_specs=[row, row, row, row, row,
                  pl.BlockSpec((D, D), lambda i: (0, 0)),
                  pl.BlockSpec((1, D), lambda i: (0, 0))],
        out_specs=row,
        out_shape=jax.ShapeDtypeStruct((N, D), F32),
        compiler_params=_cp(("parallel",)),
        name="out1",
    )(o_c, o_s, o_w, u, x2, w_out, final_g.reshape(1, D))


def _compress_kernel(x_ref, pea_ref, peb_ref, wa_ref, wb_ref, w2_ref, *rest, transposed):
    x = x_ref[0].astype(F32)
    a = _dot((x + pea_ref[...]).astype(BF16), wa_ref[0])
    b = _dot((x + peb_ref[...]).astype(BF16), wb_ref[0])
    nseg = x.shape[0]
    pre = a + pltpu.roll(b, nseg - 1, 0)
    hid = _silu(pre).astype(BF16)
    if transposed:
        o_ref, = rest
        o_ref[0, 0] = _dot_nt(w2_ref[...], hid).astype(o_ref.dtype)
    else:
        feat_ref, o_ref = rest
        o_ref[0, 0] = (_dot(hid, w2_ref[...]) + feat_ref[...].astype(F32)).astype(o_ref.dtype)


def _compress(a3, pe, w1, w2, *, transposed):
    B, nseg, wid = a3.shape
    half = CMP_STRIDE * HEAD_DIM
    eye = jnp.eye(NSA_GROUPS, dtype=w1.dtype)

    def big(wh):
        w4 = wh.reshape(CMP_STRIDE, 1, HEAD_DIM, CMP_HIDDEN)
        sel = eye[:, None, :, None, None]
        return (sel * w4[None]).reshape(NSA_GROUPS, wid, CMP_HIDDEN).astype(BF16)

    def pe_big(p):
        return jnp.broadcast_to(p[:, None, :], (CMP_STRIDE, NSA_GROUPS, HEAD_DIM)).reshape(1, wid)

    args = [a3, pe_big(pe[:CMP_STRIDE]), pe_big(pe[CMP_STRIDE:]), big(w1[:half]), big(w1[half:])]
    in_specs = [
        pl.BlockSpec((1, nseg, wid), lambda b, g: (b, 0, 0)),
        pl.BlockSpec((1, wid), lambda b, g: (0, 0)),
        pl.BlockSpec((1, wid), lambda b, g: (0, 0)),
        pl.BlockSpec((1, wid, CMP_HIDDEN), lambda b, g: (g, 0, 0)),
        pl.BlockSpec((1, wid, CMP_HIDDEN), lambda b, g: (g, 0, 0)),
    ]
    if transposed:
        w2d = w2.T.astype(BF16)
        oshape, oblock = (B, NSA_GROUPS, HEAD_DIM, nseg), (1, 1, HEAD_DIM, nseg)
        args.append(w2d)
        in_specs.append(pl.BlockSpec(w2d.shape, lambda b, g: (0, 0)))
    else:
        w2d = jnp.pad(w2, ((0, 0), (0, LANES - HEAD_DIM))).astype(BF16)
        oshape, oblock = (B, NSA_GROUPS, nseg, LANES), (1, 1, nseg, LANES)
        feat = _pos_features(jnp.arange(nseg, dtype=jnp.int32) * CMP_STRIDE + (CMP_BLOCK - 1), LANES)
        args += [w2d, feat]
        in_specs += [pl.BlockSpec(w2d.shape, lambda b, g: (0, 0)),
                     pl.BlockSpec((nseg, LANES), lambda b, g: (0, 0))]
    return pl.pallas_call(
        functools.partial(_compress_kernel, transposed=transposed),
        grid=(B, NSA_GROUPS),
        in_specs=in_specs,
        out_specs=pl.BlockSpec(oblock, lambda b, g: (b, g, 0, 0)),
        out_shape=jax.ShapeDtypeStruct(oshape, BF16),
        compiler_params=_cp(("parallel", "parallel")),
        name="compress",
    )(*args)


def _slope_table():
    s = np.asarray(2.0 ** (-8.0 * (np.arange(NSA_HEADS) + 1) / NSA_HEADS), np.float32)
    sl = np.asarray(s.astype(np.float64) * LOG2E, np.float32)
    p1, p2, p3 = _np_pieces(sl)
    tab = np.zeros((NSA_HEADS, LANES), np.float32)
    tab[:, 0] = sl
    for k, p in enumerate((p1, p1, p2, p2, p3, p3)):
        tab[:, FEAT0 + k] = p
    tab = tab.reshape(NSA_GROUPS, NSA_HPG, LANES)
    pad = np.zeros((NSA_GROUPS, 8 - NSA_HPG, LANES), np.float32)
    return jnp.asarray(np.concatenate([tab, pad], axis=1))


def _pos_features(pos, width):
    pos = pos[:, None]
    lane = jnp.arange(width, dtype=jnp.int32)[None, :] % LANES
    hi = ((pos // SLC_BLOCK) * SLC_BLOCK).astype(F32)
    lo = (pos % SLC_BLOCK).astype(F32)
    k = lane - FEAT0
    f = jnp.where((k >= 0) & (k < 6), jnp.where(k % 2 == 0, hi, lo), 0.0)
    f = jnp.where((k >= 6) & (k < 9), 1.0, f)
    return f.astype(BF16)


def _nsa_queries(q, tab_ref, t0):
    tq = q.shape[0]
    lane = lax.broadcasted_iota(jnp.int32, (tq, LANES), 1)
    low = lane < HEAD_DIM
    t = (t0 + lax.broadcasted_iota(jnp.int32, (tq, 1), 0)).astype(F32)
    out = []
    for i in range(NSA_HPG):
        p, hf = divmod(i, 2)
        qp = q[:, p * LANES:(p + 1) * LANES].astype(F32)
        if hf:
            qp = pltpu.roll(qp, HEAD_DIM, 1)
        row = tab_ref[0, i:i + 1, :]
        a1, a2, a3 = _pieces(-(row[:, 0:1] * t))
        feat = jnp.where(lane == FEAT0 + 6, a1,
                         jnp.where(lane == FEAT0 + 7, a2, jnp.where(lane == FEAT0 + 8, a3, row)))
        out.append(jnp.where(low, qp, feat).astype(BF16))
    return out


def _gates_t(gt_ref, bg_ref, branch):
    gl = gt_ref[0] + bg_ref[...]
    return [_sigmoid(gl[N_BRANCH * i + branch:N_BRANCH * i + branch + 1, :]) for i in range(NSA_HPG)]


def _store_heads(o_ref, outs_t, gates):
    g = [o * gt for o, gt in zip(outs_t, gates)]
    o_ref[0, :, :LANES] = jnp.concatenate(g[:2], axis=0).T
    o_ref[0, :, LANES:] = jnp.concatenate(g[2:], axis=0).T


def _nsa_specs(q_col):
    return dict(
        q=pl.BlockSpec((1, NSA_TQ, NSA_HPG * HEAD_DIM), lambda b, g, i: (b, i, q_col + g)),
        tab=pl.BlockSpec((1, 8, LANES), lambda b, g, i: (g, 0, 0)),
        gt=pl.BlockSpec((1, GATE_ROWS, NSA_TQ), lambda b, g, i: (b, g, i)),
        bg=pl.BlockSpec((GATE_ROWS, 1), lambda b, g, i: (g, 0)),
        out=pl.BlockSpec((1, NSA_TQ, NSA_HPG * HEAD_DIM), lambda b, g, i: (b, i, g)),
    )


def _cmp_kernel(q_ref, kc_ref, vct_ref, mt_ref, tab_ref, gt_ref, bg_ref, o_ref, sel_ref):
    qi = pl.program_id(2)
    tq = NSA_TQ
    nseg = kc_ref.shape[2]
    t0 = qi * tq
    qh = _nsa_queries(q_ref[0], tab_ref, t0)
    kc = kc_ref[0, 0]
    vct = vct_ref[0, 0]
    t = t0 + lax.broadcasted_iota(jnp.int32, (1, tq), 1)
    cidx = lax.broadcasted_iota(jnp.int32, (nseg, 1), 0)
    cend = cidx * CMP_STRIDE + (CMP_BLOCK - 1)
    valid = (cend <= t) & (cidx < nseg - 1)
    gates = _gates_t(gt_ref, bg_ref, 0)
    psum = jnp.zeros((nseg, tq), F32)
    outs = []
    for i in range(NSA_HPG):
        s = jnp.where(valid, _dot_nt(kc, qh[i]), NEG)
        m = jnp.max(s, axis=0, keepdims=True)
        e = jnp.where(valid, jnp.exp2(s - m), 0.0)
        l = jnp.sum(e, axis=0, keepdims=True)
        p = e * jnp.where(l > 0.0, 1.0 / l, 0.0)
        psum = psum + p
        outs.append(_dot(vct, p.astype(BF16)))
    _store_heads(o_ref, outs, gates)
    mt = mt_ref[...]
    imp = sum(_dot(mt, p.astype(BF16)) for p in _pieces(psum))
    ns = imp.shape[0]
    blk = lax.broadcasted_iota(jnp.int32, (ns, 1), 0)
    cur = t // SLC_BLOCK
    bvalid = blk * SLC_BLOCK <= t
    forced = (blk == 0) | (blk == cur) | (blk == cur - 1)
    score = jnp.where(forced, -jnp.inf, jnp.where(bvalid, imp, NEG))
    blk_f = blk.astype(F32)
    sel = jnp.where(forced, 1.0, 0.0)
    for _ in range(SLC_TOPK - N_FORCED):
        mx = jnp.max(score, axis=0, keepdims=True)
        first = jnp.min(jnp.where(score == mx, blk_f, float(ns)), axis=0, keepdims=True)
        hit = blk_f == first
        sel = jnp.where(hit, 1.0, sel)
        score = jnp.where(hit, -jnp.inf, score)
    selneg = jnp.where(bvalid & (sel > 0.0), 0.0, -MASK_BIG)
    sel_ref[0, 0] = selneg.T.astype(sel_ref.dtype)


def _cmp_to_slc_t(nseg, ns):
    c0 = np.arange(nseg)[:, None] * CMP_STRIDE
    s0 = np.arange(ns)[None, :] * SLC_BLOCK
    overlap = np.clip(np.minimum(c0 + CMP_BLOCK, s0 + SLC_BLOCK) - np.maximum(c0, s0), 0, None)
    m = overlap / CMP_STRIDE
    m[nseg - 1] = 0.0
    mt = np.zeros((NS_PAD, nseg))
    mt[:ns] = m.T
    return jnp.asarray(mt, BF16)


def _cmp_attention(u3, kcmp, vcmp_t, gt, bg, *, q_col):
    B, S, _ = u3.shape
    nseg = kcmp.shape[2]
    sp = _nsa_specs(q_col)
    return pl.pallas_call(
        _cmp_kernel,
        grid=(B, NSA_GROUPS, S // NSA_TQ),
        in_specs=[
            sp["q"],
            pl.BlockSpec((1, 1, nseg, LANES), lambda b, g, i: (b, g, 0, 0)),
            pl.BlockSpec((1, 1, HEAD_DIM, nseg), lambda b, g, i: (b, g, 0, 0)),
            pl.BlockSpec((NS_PAD, nseg), lambda b, g, i: (0, 0)),
            sp["tab"], sp["gt"], sp["bg"],
        ],
        out_specs=[sp["out"], pl.BlockSpec((1, 1, NSA_TQ, NS_PAD), lambda b, g, i: (b, g, i, 0))],
        out_shape=[jax.ShapeDtypeStruct((B, S, NSA_WIDTH), F32),
                   jax.ShapeDtypeStruct((B, NSA_GROUPS, S, NS_PAD), BF16)],
        compiler_params=_cp(("parallel", "parallel", "arbitrary")),
        name="cmp_attention",
    )(u3, kcmp, vcmp_t, _cmp_to_slc_t(nseg, S // SLC_BLOCK), _slope_table(), gt, bg)


def _slc_kernel(q_ref, k_ref, vt_ref, sel_ref, e_ref, tab_ref, gt_ref, bg_ref, o_ref,
                bm_sc, s0_sc, s1_sc, acc_sc):
    qi = pl.program_id(2)
    tq, tk = NSA_TQ, SLC_TK
    t0 = qi * tq
    selneg = sel_ref[0, 0]
    bm_sc[...] = jnp.concatenate([jnp.concatenate([qh, selneg], axis=1)
                                  for qh in _nsa_queries(q_ref[0], tab_ref, t0)], axis=0)
    qpos = t0 + lax.broadcasted_iota(jnp.int32, (tk, tq), 1)

    def k_tile(k0):
        return jnp.concatenate([k_ref[0, pl.ds(k0, tk), :], e_ref[pl.ds(k0, tk), :]], axis=1)

    def v_rows(h, k0):
        return vt_ref[0, :, pl.ds(k0, tk)]

    outs = _flash_t(bm_sc, (s0_sc, s1_sc), acc_sc, NSA_HPG, tq, tk, k_tile, v_rows,
                    t0 // tk + 1, k_ref.shape[1] // tk - 1, lambda key: key <= qpos)
    _store_heads(o_ref, outs, _gates_t(gt_ref, bg_ref, 1))


def _win_kernel(q_ref, k_ref, vt_ref, tab_ref, gt_ref, bg_ref, o_ref):
    qi = pl.program_id(2)
    tq = NSA_TQ
    nk = WINDOW + tq
    t0 = qi * tq
    start = pl.multiple_of(jnp.maximum(t0 - WINDOW, 0), LANES)
    bm = jnp.concatenate(_nsa_queries(q_ref[0], tab_ref, t0), axis=0)
    s_all = _dot_nt(k_ref[0, pl.ds(start, nk), :], bm)
    key = start + lax.broadcasted_iota(jnp.int32, (nk, tq), 0)
    qpos = t0 + lax.broadcasted_iota(jnp.int32, (nk, tq), 1)
    valid = (key <= qpos) & (key > qpos - WINDOW)
    lhs = jnp.concatenate([vt_ref[0, :, pl.ds(start, nk)],
                           jnp.ones((ACC_ROWS - HEAD_DIM, nk), BF16)], axis=0)
    outs = []
    for h in range(NSA_HPG):
        s = jnp.where(valid, s_all[:, h * tq:(h + 1) * tq], NEG)
        p = jnp.exp2(s - jnp.max(s, axis=0, keepdims=True)).astype(BF16)
        a = _dot(lhs, p)
        outs.append(a[:HEAD_DIM] * (1.0 / a[HEAD_DIM:HEAD_DIM + 1]))
    _store_heads(o_ref, outs, _gates_t(gt_ref, bg_ref, 2))


def _block_onehot(S):
    e = (np.arange(S)[:, None] // SLC_BLOCK) == np.arange(NS_PAD)[None, :]
    return jnp.asarray(e, BF16)


def _slc_attention(u3, vt, selneg, gt, bg, *, q_col, k_col, v_row):
    B, S, _ = u3.shape
    sp = _nsa_specs(q_col)
    return pl.pallas_call(
        _slc_kernel,
        grid=(B, NSA_GROUPS, S // NSA_TQ),
        in_specs=[
            sp["q"],
            pl.BlockSpec((1, S, LANES), lambda b, g, i: (b, 0, k_col + g)),
            pl.BlockSpec((1, HEAD_DIM, S), lambda b, g, i: (b, v_row + g, 0)),
            pl.BlockSpec((1, 1, NSA_TQ, NS_PAD), lambda b, g, i: (b, g, i, 0)),
            pl.BlockSpec((S, NS_PAD), lambda b, g, i: (0, 0)),
            sp["tab"], sp["gt"], sp["bg"],
        ],
        out_specs=sp["out"],
        out_shape=jax.ShapeDtypeStruct((B, S, NSA_WIDTH), F32),
        scratch_shapes=_flash_scratch(NSA_HPG, NSA_TQ, SLC_TK, 2 * LANES),
        compiler_params=_cp(("parallel", "parallel", "arbitrary")),
        name="slc_attention",
    )(u3, u3, vt, selneg, _block_onehot(S), _slope_table(), gt, bg)


def _win_attention(u3, vt, gt, bg, *, q_col, k_col, v_row):
    B, S, _ = u3.shape
    sp = _nsa_specs(q_col)
    return pl.pallas_call(
        _win_kernel,
        grid=(B, NSA_GROUPS, S // NSA_TQ),
        in_specs=[
            sp["q"],
            pl.BlockSpec((1, S, LANES), lambda b, g, i: (b, 0, k_col + g)),
            pl.BlockSpec((1, HEAD_DIM, S), lambda b, g, i: (b, v_row + g, 0)),
            sp["tab"], sp["gt"], sp["bg"],
        ],
        out_specs=sp["out"],
        out_shape=jax.ShapeDtypeStruct((B, S, NSA_WIDTH), F32),
        compiler_params=_cp(("parallel", "parallel", "arbitrary")),
        name="win_attention",
    )(u3, u3, vt, _slope_table(), gt, bg)


def _aug_groups(w):
    d = w.shape[0]
    w = w.reshape(d, NSA_GROUPS, HEAD_DIM)
    return jnp.pad(w, ((0, 0), (0, 0), (0, LANES - HEAD_DIM))).reshape(d, NSA_GROUPS * LANES)


def _even_layer(x, norm_g, w_in, b_f, gn_g, w_out):
    B, S, D = x.shape
    qscale = HEAD_DIM ** -0.5 * LOG2E
    q_f, k_f, v_f, w_fl, q_r, k_r, v_r, z = jnp.split(
        w_in, np.cumsum([FOX_WIDTH] * 3 + [FOX_HEADS] + [RET_WIDTH] * 3).tolist(), axis=1)
    w = jnp.concatenate([z, q_f * qscale, k_f, q_r, k_r * HEAD_DIM ** -0.5, v_r], axis=1).astype(BF16)
    x2 = x.reshape(B * S, D)
    u, vt = _proj(x2, norm_g, w, seq=S, w_t=[v_f.T.astype(BF16)], t_dtypes=[BF16])
    u3 = u.reshape(B, S, -1)
    kfeat, qfeat = _fgate(x, norm_g, w_fl, b_f, tile=min(512, S))
    cb = D // LANES
    nb = FOX_WIDTH // LANES
    o_f = _fox(u3, vt, kfeat, qfeat, q_col=cb, k_col=cb + nb)
    rb = (D + 2 * FOX_WIDTH) // RET_WIDTH
    o_r = _retention(u3, gn_g, q_col=rb, k_col=rb + 1, v_col=rb + 2)
    out = _out0(o_f.reshape(B * S, -1), o_r.reshape(B * S, -1), u, x2, w_out.astype(BF16))
    return out.reshape(B, S, D)


def _odd_layer(x, norm_g, w_in, b_gate, pe_k, pe_v, wk1, wk2, wv1, wv2, w_out, final_g):
    B, S, D = x.shape
    assert S // SLC_BLOCK <= NS_PAD
    qscale = HEAD_DIM ** -0.5 * LOG2E
    sizes = [NSA_WIDTH] + [NSA_KV_WIDTH] * 6 + [NSA_HEADS * N_BRANCH]
    q, kc, vc, ks, vs, kw, vw, gl, z = jnp.split(w_in, np.cumsum(sizes).tolist(), axis=1)
    w = jnp.concatenate([z, q * qscale, kc, vc, _aug_groups(ks), _aug_groups(kw)], axis=1).astype(BF16)
    per_group = NSA_HPG * N_BRANCH
    glt = jnp.pad(gl.T.reshape(NSA_GROUPS, per_group, D), ((0, 0), (0, GATE_ROWS - per_group), (0, 0)))
    glt = glt.reshape(NSA_GROUPS * GATE_ROWS, D).astype(BF16)
    bg = jnp.pad(b_gate.reshape(NSA_GROUPS, per_group), ((0, 0), (0, GATE_ROWS - per_group)))
    bg = bg.reshape(NSA_GROUPS * GATE_ROWS, 1)
    w_vt = jnp.concatenate([vs, vw], axis=1).T.astype(BF16)
    x2 = x.reshape(B * S, D)
    first_k = (2 * D + 2 * NSA_KV_WIDTH) // PROJ_TN
    u, kc_a, vc_a, vt, gt = _proj(
        x2, norm_g, w, seq=S, split_tile=(2 * D) // PROJ_TN,
        addend=_pos_features(jnp.arange(S, dtype=jnp.int32), PROJ_TN), add_tiles=(first_k, first_k + 1),
        w_t=[w_vt, glt], t_dtypes=[BF16, F32])
    u3 = u.reshape(B, S, -1)
    nseg = S // CMP_STRIDE
    kcmp = _compress(kc_a.reshape(B, nseg, -1), pe_k, wk1, wk2, transposed=False)
    vcmp_t = _compress(vc_a.reshape(B, nseg, -1), pe_v, wv1, wv2, transposed=True)
    q_col = D // (NSA_HPG * HEAD_DIM)
    o_c, selneg = _cmp_attention(u3, kcmp, vcmp_t, gt, bg, q_col=q_col)
    kb = (2 * D + 2 * NSA_KV_WIDTH) // LANES
    o_s = _slc_attention(u3, vt, selneg, gt, bg, q_col=q_col, k_col=kb, v_row=0)
    o_w = _win_attention(u3, vt, gt, bg, q_col=q_col, k_col=kb + NSA_GROUPS, v_row=NSA_GROUPS)
    r = lambda a: a.reshape(B * S, -1)
    out = _out1(r(o_c), r(o_s), r(o_w), u, x2, w_out.astype(BF16), final_g)
    return out.reshape(B, S, D)


def kernel(x, even_norm_g, even_w_in, even_b_f, even_gn_g, even_w_out, odd_norm_g, odd_w_in,
           odd_b_gate, odd_pe_k, odd_pe_v, odd_wk1, odd_wk2, odd_wv1, odd_wv2, odd_w_out, final_g):
    x = _even_layer(x, even_norm_g[0], even_w_in[0], even_b_f[0], even_gn_g[0], even_w_out[0])
    return _odd_layer(x, odd_norm_g[0], odd_w_in[0], odd_b_gate[0], odd_pe_k[0], odd_pe_v[0],
                      odd_wk1[0], odd_wk2[0], odd_wv1[0], odd_wv2[0], odd_w_out[0], final_g)
```

```python
import functools
import math

import jax
import jax.numpy as jnp
import numpy as np
from jax import lax
from jax.experimental import pallas as pl
from jax.experimental.pallas import tpu as pltpu

D_MODEL = 1024
HEAD_DIM = 64
LANES = 128
FOX_HEADS = 8
RET_HEADS = 8
FOX_WIDTH = FOX_HEADS * HEAD_DIM
RET_WIDTH = RET_HEADS * HEAD_DIM
RET_CHUNK = 128
NSA_HEADS = 16
NSA_GROUPS = 4
NSA_HPG = NSA_HEADS // NSA_GROUPS
NSA_WIDTH = NSA_HEADS * HEAD_DIM
NSA_KV_WIDTH = NSA_GROUPS * HEAD_DIM
N_BRANCH = 3
GATE_ROWS = 16
CMP_BLOCK = 32
CMP_STRIDE = 16
CMP_HIDDEN = 256
CMP_CHUNK = 128
SLC_BLOCK = 64
SLC_TOPK = 16
N_FORCED = 3
NS_PAD = LANES
WINDOW = 512
RMS_EPS = 1e-6
GN_EPS = 1e-5
NEG = -1e30
FORCE_BONUS = 1e6
MASK_BIG = 2.0 ** 100
LOG2E = math.log2(math.e)
FEAT0 = HEAD_DIM
ACC_ROWS = HEAD_DIM + 16

PROJ_TM = 1024
PROJ_TN = 512
FOX_TQ = 512
FOX_TK = 512
NSA_TQ = 256
SLC_TK = 512
OUT_TM = 512
VMEM_LIMIT = 48 * 1024 * 1024

F32 = jnp.float32
BF16 = jnp.bfloat16


def _cp(sem, vmem=VMEM_LIMIT):
    return pltpu.CompilerParams(dimension_semantics=sem, vmem_limit_bytes=vmem)


def _dot(a, b):
    return jnp.dot(a, b, preferred_element_type=F32)


def _dot_nt(a, b):
    return lax.dot_general(a, b, (((1,), (1,)), ((), ())), preferred_element_type=F32)


def _dot_tn(a, b):
    return lax.dot_general(a, b, (((0,), (0,)), ((), ())), preferred_element_type=F32)


def _rms(x, g):
    return x * lax.rsqrt(jnp.mean(x * x, axis=-1, keepdims=True) + RMS_EPS) * g


def _silu(x):
    return x * (1.0 / (1.0 + jnp.exp(-x)))


def _sigmoid(x):
    return 1.0 / (1.0 + jnp.exp(-x))


def _low_half(shape, axis):
    return lax.broadcasted_iota(jnp.int32, shape, axis) < HEAD_DIM


def _pieces(v):
    p1 = v.astype(BF16).astype(F32)
    r = v - p1
    p2 = r.astype(BF16).astype(F32)
    p3 = (r - p2).astype(BF16).astype(F32)
    return p1, p2, p3


def _np_pieces(v):
    v = np.asarray(v, np.float64)
    bf = lambda a: np.asarray(a, np.float32).astype(BF16).astype(np.float64)
    p1 = bf(v)
    p2 = bf(v - p1)
    p3 = bf(v - p1 - p2)
    return p1, p2, p3


def _place(lane, cols):
    out = jnp.zeros(lane.shape, F32)
    for i, c in cols.items():
        out = jnp.where(lane == i, c, out)
    return out


def _fgate_kernel(x_ref, g_ref, wf_ref, b_ref, kf_ref, qf_ref, carry):
    @pl.when(pl.program_id(1) == 0)
    def _():
        carry[...] = jnp.zeros_like(carry)

    h = _rms(x_ref[0], g_ref[...])
    t = h.shape[0]
    h1 = h.astype(BF16)
    h2 = (h - h1.astype(F32)).astype(BF16)
    w = wf_ref[...]
    w1 = w.astype(BF16)
    w2 = (w - w1.astype(F32)).astype(BF16)
    f = _dot(h1, w1) + _dot(h1, w2) + _dot(h2, w1) + b_ref[...]
    ls = jnp.minimum(f, 0.0) - jnp.log(1.0 + jnp.exp(-jnp.abs(f)))
    r = lax.broadcasted_iota(jnp.int32, (t, t), 0)
    c = lax.broadcasted_iota(jnp.int32, (t, t), 1)
    lower = jnp.where(c <= r, 1.0, 0.0).astype(BF16)
    cs = sum(_dot(lower, p.astype(BF16)) for p in _pieces(ls)) + carry[...]
    carry[...] = cs[t - 1:t, :]
    cl = cs * LOG2E
    lane = lax.broadcasted_iota(jnp.int32, (t, LANES), 1)
    one = jnp.ones((t, 1), F32)
    for j in range(FOX_HEADS // 2):
        a1, a2, a3 = _pieces(cl[:, 2 * j:2 * j + 1])
        b1, b2, b3 = _pieces(cl[:, 2 * j + 1:2 * j + 2])
        kf = _place(lane, {0: a1, 1: a2, 2: a3, 3: one, 4: one, 5: one, 6: b1, 7: b2, 8: b3})
        kf_ref[0, j] = kf.astype(BF16)
        qf_ref[0, j, 0] = _place(lane, {0: -one, 1: -one, 2: -one, 3: a1, 4: a2, 5: a3}).astype(BF16)
        qf_ref[0, j, 1] = _place(lane, {3: b1, 4: b2, 5: b3, 6: -one, 7: -one, 8: -one}).astype(BF16)


def _fgate(x, g, wf, b_f, *, tile):
    B, S, D = x.shape
    npair = FOX_HEADS // 2
    return pl.pallas_call(
        _fgate_kernel,
        grid=(B, S // tile),
        in_specs=[
            pl.BlockSpec((1, tile, D), lambda b, s: (b, s, 0)),
            pl.BlockSpec((1, D), lambda b, s: (0, 0)),
            pl.BlockSpec((D, FOX_HEADS), lambda b, s: (0, 0)),
            pl.BlockSpec((1, FOX_HEADS), lambda b, s: (0, 0)),
        ],
        out_specs=[
            pl.BlockSpec((1, npair, tile, LANES), lambda b, s: (b, 0, s, 0)),
            pl.BlockSpec((1, npair, 2, tile, LANES), lambda b, s: (b, 0, 0, s, 0)),
        ],
        out_shape=[
            jax.ShapeDtypeStruct((B, npair, S, LANES), BF16),
            jax.ShapeDtypeStruct((B, npair, 2, S, LANES), BF16),
        ],
        scratch_shapes=[pltpu.VMEM((1, FOX_HEADS), F32)],
        compiler_params=_cp(("parallel", "arbitrary")),
        name="fgate",
    )(x, g.reshape(1, D), wf, b_f.reshape(1, FOX_HEADS))


def _proj_kernel(*refs, split_tile, add_tiles, n_t):
    it = iter(refs)
    x_ref, g_ref, w_ref = next(it), next(it), next(it)
    add_ref = next(it) if add_tiles else None
    wt_refs = [next(it) for _ in range(n_t)]
    u_ref = next(it)
    e_refs = [next(it), next(it)] if split_tile is not None else None
    ut_refs = [next(it) for _ in range(n_t)]
    h_sc = next(it)
    j = pl.program_id(1)

    @pl.when(j == 0)
    def _():
        h = _rms(x_ref[...], g_ref[...]).astype(BF16)
        h_sc[...] = h
        for wt_ref, ut_ref in zip(wt_refs, ut_refs):
            ut_ref[0] = _dot_nt(wt_ref[...], h).astype(ut_ref.dtype)

    acc = _dot(h_sc[...], w_ref[...])
    if add_tiles:
        is_add = functools.reduce(jnp.logical_or, [j == t for t in add_tiles])

        @pl.when(is_add)
        def _():
            u_ref[...] = (acc + add_ref[...].astype(F32)).astype(u_ref.dtype)

        @pl.when(jnp.logical_not(is_add))
        def _():
            u_ref[...] = acc.astype(u_ref.dtype)
    else:
        u_ref[...] = acc.astype(u_ref.dtype)
    if split_tile is not None:
        @pl.when(j == split_tile)
        def _():
            half = acc.shape[1] // 2
            e_refs[0][...] = acc[:, :half].astype(e_refs[0].dtype)
            e_refs[1][...] = acc[:, half:].astype(e_refs[1].dtype)


def _proj(x2, g, w, *, seq, split_tile=None, addend=None, add_tiles=(), w_t=(), t_dtypes=(),
          tm=PROJ_TM, tn=PROJ_TN):
    N, D = x2.shape
    W = w.shape[1]
    nbs = seq // tm
    in_specs = [
        pl.BlockSpec((tm, D), lambda i, j: (i, 0)),
        pl.BlockSpec((1, D), lambda i, j: (0, 0)),
        pl.BlockSpec((D, tn), lambda i, j: (0, j)),
    ]
    args = [x2, g.reshape(1, D), w]
    if add_tiles:
        in_specs.append(pl.BlockSpec((tm, tn), lambda i, j: (i % nbs, 0)))
        args.append(addend)
    out_shape = [jax.ShapeDtypeStruct((N, W), BF16)]
    out_specs = [pl.BlockSpec((tm, tn), lambda i, j: (i, j))]
    if split_tile is not None:
        out_shape += [jax.ShapeDtypeStruct((N, tn // 2), BF16)] * 2
        out_specs += [pl.BlockSpec((tm, tn // 2), lambda i, j: (i, 0))] * 2
    for wt, dt in zip(w_t, t_dtypes):
        rows = wt.shape[0]
        in_specs.append(pl.BlockSpec((rows, D), lambda i, j: (0, 0)))
        args.append(wt)
        out_shape.append(jax.ShapeDtypeStruct((N // seq, rows, seq), dt))
        out_specs.append(pl.BlockSpec((1, rows, tm), lambda i, j: (i // nbs, 0, i % nbs)))
    return pl.pallas_call(
        functools.partial(_proj_kernel, split_tile=split_tile, add_tiles=tuple(add_tiles),
                          n_t=len(w_t)),
        grid=(N // tm, W // tn),
        in_specs=in_specs,
        out_specs=out_specs,
        out_shape=out_shape,
        scratch_shapes=[pltpu.VMEM((tm, D), BF16)],
        compiler_params=_cp(("parallel", "arbitrary")),
        name="proj",
    )(*args)


def _flash_t(bm_sc, s_bufs, acc_sc, nh, tq, tk, k_tile, v_rows, n_un, tile_of, diag, valid_fn):
    acc_sc[...] = jnp.zeros_like(acc_sc)
    ones = jnp.ones((ACC_ROWS - HEAD_DIM, tk), BF16)
    key_iota = lax.broadcasted_iota(jnp.int32, (tk, tq), 0)

    def qk(tile, buf):
        buf[...] = _dot_nt(k_tile(pl.multiple_of(tile * tk, tk)), bm_sc[...])

    def soft(tile, buf, ms, masked):
        k0 = pl.multiple_of(tile * tk, tk)
        if masked:
            valid = valid_fn(k0 + key_iota)
        out = []
        for h in range(nh):
            cols = slice(h * tq, (h + 1) * tq)
            if masked:
                s = jnp.where(valid, buf[:, cols], NEG)
                m_new = jnp.maximum(ms[h], jnp.max(s, axis=0, keepdims=True))
                p = jnp.exp2(s - m_new).astype(BF16)
            else:
                m_new = jnp.maximum(ms[h], jnp.max(buf[:, cols], axis=0, keepdims=True))
                p = jnp.exp2(buf[:, cols] - m_new).astype(BF16)
            alpha = jnp.exp2(ms[h] - m_new)
            lhs = jnp.concatenate([v_rows(h, k0), ones], axis=0)
            acc_sc[h] = alpha * acc_sc[h] + _dot(lhs, p)
            out.append(m_new)
        return tuple(out)

    def pair(i, ms):
        t_a, t_b, t_c = tile_of(2 * i), tile_of(2 * i + 1), tile_of(2 * i + 2)
        qk(t_b, s_bufs[1])
        ms = soft(t_a, s_bufs[0], ms, False)
        qk(t_c, s_bufs[0])
        return soft(t_b, s_bufs[1], ms, False)

    def odd_tail(ms):
        qk(diag, s_bufs[1])
        ms = soft(tile_of(n_un - 1), s_bufs[0], ms, False)
        return soft(diag, s_bufs[1], ms, True)

    def even_tail(ms):
        return soft(diag, s_bufs[0], ms, True)

    qk(tile_of(0), s_bufs[0])
    ms = (jnp.full((1, tq), NEG, F32),) * nh
    ms = lax.fori_loop(0, n_un // 2, pair, ms)
    ms = lax.cond(n_un % 2 == 1, odd_tail, even_tail, ms)
    outs = []
    for h in range(nh):
        a = acc_sc[h]
        outs.append(a[:HEAD_DIM] * (1.0 / a[HEAD_DIM:HEAD_DIM + 1]))
    return outs


def _flash_scratch(nh, tq, tk, kdim):
    return [pltpu.VMEM((nh * tq, kdim), BF16), pltpu.VMEM((tk, nh * tq), F32),
            pltpu.VMEM((tk, nh * tq), F32), pltpu.VMEM((nh, ACC_ROWS, tq), F32)]


def _fox_kernel(q_ref, qf_ref, k_ref, kf_ref, vt_ref, o_ref, bm_sc, s0_sc, s1_sc, acc_sc):
    qi = pl.program_id(2)
    tq, tk = FOX_TQ, FOX_TK
    q = q_ref[0]
    low = _low_half(q.shape, 1)
    zero = jnp.zeros_like(q)
    bm_sc[...] = jnp.concatenate([
        jnp.concatenate([jnp.where(low, q, zero), qf_ref[0, 0, 0]], axis=1),
        jnp.concatenate([jnp.where(low, zero, q), qf_ref[0, 0, 1]], axis=1)], axis=0)
    qpos = qi * tq + lax.broadcasted_iota(jnp.int32, (tk, tq), 1)

    def k_tile(k0):
        return jnp.concatenate([k_ref[0, pl.ds(k0, tk), :], kf_ref[0, 0, pl.ds(k0, tk), :]], axis=1)

    def v_rows(h, k0):
        return vt_ref[0, h * HEAD_DIM:(h + 1) * HEAD_DIM, pl.ds(k0, tk)]

    diag = (qi * tq) // tk
    outs = _flash_t(bm_sc, (s0_sc, s1_sc), acc_sc, 2, tq, tk, k_tile, v_rows,
                    diag, lambda i: i, diag, lambda key: key <= qpos)
    o_ref[0] = jnp.concatenate(outs, axis=0).T


def _fox(u3, vt, kfeat, qfeat, *, q_col, k_col):
    B, S, _ = u3.shape
    npair = FOX_HEADS // 2
    return pl.pallas_call(
        _fox_kernel,
        grid=(B, npair, S // FOX_TQ),
        in_specs=[
            pl.BlockSpec((1, FOX_TQ, LANES), lambda b, j, i: (b, i, q_col + j)),
            pl.BlockSpec((1, 1, 2, FOX_TQ, LANES), lambda b, j, i: (b, j, 0, i, 0)),
            pl.BlockSpec((1, S, LANES), lambda b, j, i: (b, 0, k_col + j)),
            pl.BlockSpec((1, 1, S, LANES), lambda b, j, i: (b, j, 0, 0)),
            pl.BlockSpec((1, LANES, S), lambda b, j, i: (b, j, 0)),
        ],
        out_specs=pl.BlockSpec((1, FOX_TQ, LANES), lambda b, j, i: (b, i, j)),
        out_shape=jax.ShapeDtypeStruct((B, S, FOX_WIDTH), F32),
        scratch_shapes=_flash_scratch(2, FOX_TQ, FOX_TK, 2 * LANES),
        compiler_params=_cp(("parallel", "parallel", "arbitrary")),
        name="fox",
    )(u3, qfeat, u3, kfeat, vt)


def _ret_kernel(q_ref, k_ref, v_ref, inner_ref, cross_ref, kdec_ref, cd_ref, bd_ref, gn_ref,
                o_ref, state_sc):
    @pl.when(pl.program_id(1) == 0)
    def _():
        state_sc[...] = jnp.zeros_like(state_sc)

    low = _low_half((RET_CHUNK, LANES), 1)
    inv = 1.0 / HEAD_DIM
    for j in range(RET_HEADS // 2):
        cols = slice(j * LANES, (j + 1) * LANES)
        q, k, v = q_ref[0, :, cols], k_ref[0, :, cols], v_ref[0, :, cols]
        zero = jnp.zeros_like(q)
        qa, qb = jnp.where(low, q, zero), jnp.where(low, zero, q)
        pa = (_dot_nt(qa, k) * inner_ref[j, 0]).astype(BF16)
        pb = (_dot_nt(qb, k) * inner_ref[j, 1]).astype(BF16)
        o_in = jnp.where(low, _dot(pa, v), _dot(pb, v))
        state = state_sc[j]
        o = o_in + _dot(q, state.astype(BF16)) * cross_ref[j]
        kd = (k.astype(F32) * kdec_ref[j]).astype(BF16)
        state_sc[j] = state * cd_ref[j] + _dot_tn(kd, v) * bd_ref[...]
        sa = jnp.sum(jnp.where(low, o, 0.0), axis=-1, keepdims=True)
        st = jnp.sum(o, axis=-1, keepdims=True)
        mu = jnp.where(low, sa, st - sa) * inv
        d = o - mu
        d2 = d * d
        va = jnp.sum(jnp.where(low, d2, 0.0), axis=-1, keepdims=True)
        vt = jnp.sum(d2, axis=-1, keepdims=True)
        var = jnp.where(low, va, vt - va) * inv
        o_ref[0, :, cols] = d * lax.rsqrt(var + GN_EPS) * gn_ref[:, cols]


def _ret_constants():
    lg = np.log(1.0 - 2.0 ** (-5.0 - np.arange(RET_HEADS)))
    i = np.arange(RET_CHUNK)
    diff = i[:, None] - i[None, :]
    inner = np.where(diff[None] >= 0, np.exp(lg[:, None, None] * np.maximum(diff, 0)[None]), 0.0)
    cross = np.exp(lg[:, None] * (i[None, :] + 1))
    kdec = np.exp(lg[:, None] * (RET_CHUNK - 1 - i)[None, :])
    cdec = np.exp(lg * RET_CHUNK)
    npair = RET_HEADS // 2
    inner = inner.reshape(npair, 2, RET_CHUNK, RET_CHUNK)

    def lanes(a):
        a = a.reshape(npair, 2, RET_CHUNK)
        return np.repeat(a.transpose(0, 2, 1), HEAD_DIM, axis=2)

    bd = np.kron(np.eye(2), np.ones((HEAD_DIM, HEAD_DIM)))
    cd = np.repeat(cdec.reshape(npair, 2), HEAD_DIM, axis=1)[:, :, None] * bd[None]
    f = lambda a: jnp.asarray(a, F32)
    return f(inner), f(lanes(cross)), f(lanes(kdec)), f(cd), f(bd)


def _retention(u3, gn_g, *, q_col, k_col, v_col):
    B, S, _ = u3.shape
    C = RET_CHUNK
    npair = RET_HEADS // 2
    inner, cross, kdec, cd, bd = _ret_constants()
    full = lambda shape: pl.BlockSpec(shape, lambda b, i: (0,) * len(shape))
    return pl.pallas_call(
        _ret_kernel,
        grid=(B, S // C),
        in_specs=[
            pl.BlockSpec((1, C, RET_WIDTH), lambda b, i: (b, i, q_col)),
            pl.BlockSpec((1, C, RET_WIDTH), lambda b, i: (b, i, k_col)),
            pl.BlockSpec((1, C, RET_WIDTH), lambda b, i: (b, i, v_col)),
            full((npair, 2, C, C)), full((npair, C, LANES)), full((npair, C, LANES)),
            full((npair, LANES, LANES)), full((LANES, LANES)), full((1, RET_WIDTH)),
        ],
        out_specs=pl.BlockSpec((1, C, RET_WIDTH), lambda b, i: (b, i, 0)),
        out_shape=jax.ShapeDtypeStruct((B, S, RET_WIDTH), F32),
        scratch_shapes=[pltpu.VMEM((npair, LANES, LANES), F32)],
        compiler_params=_cp(("parallel", "arbitrary")),
        name="retention",
    )(u3, u3, u3, inner, cross, kdec, cd, bd, gn_g.reshape(1, RET_WIDTH))


def _out0_kernel(of_ref, or_ref, z_ref, x_ref, w_ref, o_ref):
    z = _silu(z_ref[...].astype(F32))
    ya = (of_ref[...] * z[:, :FOX_WIDTH]).astype(BF16)
    yb = (or_ref[...] * z[:, FOX_WIDTH:]).astype(BF16)
    o_ref[...] = x_ref[...] + _dot(ya, w_ref[:FOX_WIDTH, :]) + _dot(yb, w_ref[FOX_WIDTH:, :])


def _out0(o_f, o_r, u, x2, w_out, *, tm=OUT_TM):
    N, D = x2.shape
    return pl.pallas_call(
        _out0_kernel,
        grid=(N // tm,),
        in_specs=[
            pl.BlockSpec((tm, FOX_WIDTH), lambda i: (i, 0)),
            pl.BlockSpec((tm, RET_WIDTH), lambda i: (i, 0)),
            pl.BlockSpec((tm, D), lambda i: (i, 0)),
            pl.BlockSpec((tm, D), lambda i: (i, 0)),
            pl.BlockSpec((D, D), lambda i: (0, 0)),
        ],
        out_specs=pl.BlockSpec((tm, D), lambda i: (i, 0)),
        out_shape=jax.ShapeDtypeStruct((N, D), F32),
        compiler_params=_cp(("parallel",)),
        name="out0",
    )(o_f, o_r, u, x2, w_out)


def _out1_kernel(oc_ref, os_ref, ow_ref, z_ref, x_ref, w_ref, g_ref, o_ref):
    z = _silu(z_ref[...].astype(F32))
    y = ((oc_ref[...] + os_ref[...] + ow_ref[...]) * z).astype(BF16)
    o_ref[...] = _rms(x_ref[...] + _dot(y, w_ref[...]), g_ref[...])


def _out1(o_c, o_s, o_w, u, x2, w_out, final_g, *, tm=OUT_TM):
    N, D = x2.shape
    row = pl.BlockSpec((tm, D), lambda i: (i, 0))
    return pl.pallas_call(
        _out1_kernel,
        grid=(N // tm,),
        in_specs=[row, row, row, row, row,
                  pl.BlockSpec((D, D), lambda i: (0, 0)),
                  pl.BlockSpec((1, D), lambda i: (0, 0))],
        out_specs=row,
        out_shape=jax.ShapeDtypeStruct((N, D), F32),
        compiler_params=_cp(("parallel",)),
        name="out1",
    )(o_c, o_s, o_w, u, x2, w_out, final_g.reshape(1, D))


def _compress_kernel(x_ref, pea_ref, peb_ref, wa_ref, wb_ref, w2_ref, *rest, transposed):
    x = x_ref[0].astype(F32)
    a = _dot((x + pea_ref[...]).astype(BF16), wa_ref[0])
    b = _dot((x + peb_ref[...]).astype(BF16), wb_ref[0])
    nseg = x.shape[0]
    pre = a + pltpu.roll(b, nseg - 1, 0)
    hid = _silu(pre).astype(BF16)
    if transposed:
        o_ref, = rest
        o_ref[0, 0] = _dot_nt(w2_ref[...], hid).astype(o_ref.dtype)
    else:
        feat_ref, o_ref = rest
        o_ref[0, 0] = (_dot(hid, w2_ref[...]) + feat_ref[...].astype(F32)).astype(o_ref.dtype)


def _compress(a3, pe, w1, w2, *, transposed):
    B, nseg, wid = a3.shape
    half = CMP_STRIDE * HEAD_DIM
    eye = jnp.eye(NSA_GROUPS, dtype=w1.dtype)

    def big(wh):
        w4 = wh.reshape(CMP_STRIDE, 1, HEAD_DIM, CMP_HIDDEN)
        sel = eye[:, None, :, None, None]
        return (sel * w4[None]).reshape(NSA_GROUPS, wid, CMP_HIDDEN).astype(BF16)

    def pe_big(p):
        return jnp.broadcast_to(p[:, None, :], (CMP_STRIDE, NSA_GROUPS, HEAD_DIM)).reshape(1, wid)

    args = [a3, pe_big(pe[:CMP_STRIDE]), pe_big(pe[CMP_STRIDE:]), big(w1[:half]), big(w1[half:])]
    in_specs = [
        pl.BlockSpec((1, nseg, wid), lambda b, g: (b, 0, 0)),
        pl.BlockSpec((1, wid), lambda b, g: (0, 0)),
        pl.BlockSpec((1, wid), lambda b, g: (0, 0)),
        pl.BlockSpec((1, wid, CMP_HIDDEN), lambda b, g: (g, 0, 0)),
        pl.BlockSpec((1, wid, CMP_HIDDEN), lambda b, g: (g, 0, 0)),
    ]
    if transposed:
        w2d = w2.T.astype(BF16)
        oshape, oblock = (B, NSA_GROUPS, HEAD_DIM, nseg), (1, 1, HEAD_DIM, nseg)
        args.append(w2d)
        in_specs.append(pl.BlockSpec(w2d.shape, lambda b, g: (0, 0)))
    else:
        w2d = jnp.pad(w2, ((0, 0), (0, LANES - HEAD_DIM))).astype(BF16)
        oshape, oblock = (B, NSA_GROUPS, nseg, LANES), (1, 1, nseg, LANES)
        feat = _pos_features(jnp.arange(nseg, dtype=jnp.int32) * CMP_STRIDE + (CMP_BLOCK - 1), LANES)
        args += [w2d, feat]
        in_specs += [pl.BlockSpec(w2d.shape, lambda b, g: (0, 0)),
                     pl.BlockSpec((nseg, LANES), lambda b, g: (0, 0))]
    return pl.pallas_call(
        functools.partial(_compress_kernel, transposed=transposed),
        grid=(B, NSA_GROUPS),
        in_specs=in_specs,
        out_specs=pl.BlockSpec(oblock, lambda b, g: (b, g, 0, 0)),
        out_shape=jax.ShapeDtypeStruct(oshape, BF16),
        compiler_params=_cp(("parallel", "parallel")),
        name="compress",
    )(*args)


def _slope_table():
    s = np.asarray(2.0 ** (-8.0 * (np.arange(NSA_HEADS) + 1) / NSA_HEADS), np.float32)
    sl = np.asarray(s.astype(np.float64) * LOG2E, np.float32)
    p1, p2, p3 = _np_pieces(sl)
    tab = np.zeros((NSA_HEADS, LANES), np.float32)
    tab[:, 0] = sl
    for k, p in enumerate((p1, p1, p2, p2, p3, p3)):
        tab[:, FEAT0 + k] = p
    tab = tab.reshape(NSA_GROUPS, NSA_HPG, LANES)
    pad = np.zeros((NSA_GROUPS, 8 - NSA_HPG, LANES), np.float32)
    return jnp.asarray(np.concatenate([tab, pad], axis=1))


def _pos_features(pos, width):
    pos = pos[:, None]
    lane = jnp.arange(width, dtype=jnp.int32)[None, :] % LANES
    hi = ((pos // SLC_BLOCK) * SLC_BLOCK).astype(F32)
    lo = (pos % SLC_BLOCK).astype(F32)
    k = lane - FEAT0
    f = jnp.where((k >= 0) & (k < 6), jnp.where(k % 2 == 0, hi, lo), 0.0)
    f = jnp.where((k >= 6) & (k < 9), 1.0, f)
    return f.astype(BF16)


def _nsa_queries(q, tab_ref, t0):
    tq = q.shape[0]
    lane = lax.broadcasted_iota(jnp.int32, (tq, LANES), 1)
    low = lane < HEAD_DIM
    t = (t0 + lax.broadcasted_iota(jnp.int32, (tq, 1), 0)).astype(F32)
    out = []
    for i in range(NSA_HPG):
        p, hf = divmod(i, 2)
        qp = q[:, p * LANES:(p + 1) * LANES].astype(F32)
        if hf:
            qp = pltpu.roll(qp, HEAD_DIM, 1)
        row = tab_ref[0, i:i + 1, :]
        a1, a2, a3 = _pieces(-(row[:, 0:1] * t))
        feat = jnp.where(lane == FEAT0 + 6, a1,
                         jnp.where(lane == FEAT0 + 7, a2, jnp.where(lane == FEAT0 + 8, a3, row)))
        out.append(jnp.where(low, qp, feat).astype(BF16))
    return out


def _gates_t(gt_ref, bg_ref, branch):
    gl = gt_ref[0] + bg_ref[...]
    return [_sigmoid(gl[N_BRANCH * i + branch:N_BRANCH * i + branch + 1, :]) for i in range(NSA_HPG)]


def _store_heads(o_ref, outs_t, gates):
    g = [o * gt for o, gt in zip(outs_t, gates)]
    o_ref[0, :, :LANES] = jnp.concatenate(g[:2], axis=0).T
    o_ref[0, :, LANES:] = jnp.concatenate(g[2:], axis=0).T


def _nsa_specs(q_col):
    return dict(
        q=pl.BlockSpec((1, NSA_TQ, NSA_HPG * HEAD_DIM), lambda b, g, i: (b, i, q_col + g)),
        tab=pl.BlockSpec((1, 8, LANES), lambda b, g, i: (g, 0, 0)),
        gt=pl.BlockSpec((1, GATE_ROWS, NSA_TQ), lambda b, g, i: (b, g, i)),
        bg=pl.BlockSpec((GATE_ROWS, 1), lambda b, g, i: (g, 0)),
        out=pl.BlockSpec((1, NSA_TQ, NSA_HPG * HEAD_DIM), lambda b, g, i: (b, i, g)),
    )


def _cmp_body(nc, t0, qh, kc_ref, vct_ref, mt_ref, grp_ref, gates, o_ref, sel_ref, flag_ref):
    tq = NSA_TQ
    rows = nc * CMP_CHUNK
    full = max(rows - CMP_CHUNK - 8, 0)
    nseg = kc_ref.shape[2]
    kc = kc_ref[0, 0, :rows, :]
    vct = vct_ref[0, 0, :, :rows]
    t = t0 + lax.broadcasted_iota(jnp.int32, (1, tq), 1)
    cidx = full + lax.broadcasted_iota(jnp.int32, (rows - full, 1), 0)
    valid = (cidx * CMP_STRIDE + (CMP_BLOCK - 1) <= t) & (cidx < nseg - 1)
    psum = jnp.zeros((rows, tq), F32)
    outs = []
    for i in range(NSA_HPG):
        s = _dot_nt(kc, qh[i])
        s_last = jnp.where(valid, s[full:], NEG)
        m = jnp.max(s_last, axis=0, keepdims=True)
        if nc > 1:
            m = jnp.maximum(m, jnp.max(s[:full], axis=0, keepdims=True))
        e = jnp.where(valid, jnp.exp2(s_last - m), 0.0)
        if nc > 1:
            e = jnp.concatenate([jnp.exp2(s[:full] - m), e], axis=0)
        l = jnp.sum(e, axis=0, keepdims=True)
        p = e * jnp.where(l > 0.0, 1.0 / l, 0.0)
        psum = psum + p
        outs.append(_dot(vct, p.astype(BF16)))
    _store_heads(o_ref, outs, gates)
    ns = rows * CMP_STRIDE // SLC_BLOCK
    mt = mt_ref[:ns, :rows]
    imp = sum(_dot(mt, p.astype(BF16)) for p in _pieces(psum))
    blk = lax.broadcasted_iota(jnp.int32, (ns, 1), 0)
    cur = t // SLC_BLOCK
    bvalid = blk * SLC_BLOCK <= t
    forced = (blk == 0) | (blk == cur) | (blk == cur - 1)
    score = jnp.where(forced, -jnp.inf, jnp.where(bvalid, imp, NEG))
    blk_f = blk.astype(F32)
    sel = jnp.where(forced, 1.0, 0.0)
    for _ in range(SLC_TOPK - N_FORCED):
        mx = jnp.max(score, axis=0, keepdims=True)
        first = jnp.min(jnp.where(score == mx, blk_f, float(ns)), axis=0, keepdims=True)
        hit = blk_f == first
        sel = jnp.where(hit, 1.0, sel)
        score = jnp.where(hit, -jnp.inf, score)
    selneg = jnp.where(bvalid & (sel > 0.0), 0.0, -MASK_BIG)
    if ns < NS_PAD:
        selneg = jnp.concatenate([selneg, jnp.full((NS_PAD - ns, tq), -MASK_BIG, F32)], axis=0)
    selneg_t = selneg.T
    sel_ref[0, 0] = selneg_t.astype(sel_ref.dtype)
    used = jnp.max(jnp.where(selneg_t == 0.0, 1.0, 0.0), axis=0, keepdims=True)
    used = jnp.broadcast_to(used, (8, NS_PAD)).astype(BF16)
    flag_ref[0] = (_dot(used, grp_ref[...])[0:1] > 0.0).astype(jnp.int32)


def _cmp_kernel(q_ref, kc_ref, vct_ref, mt_ref, grp_ref, tab_ref, gt_ref, bg_ref,
                o_ref, sel_ref, flag_ref):
    t0 = pl.program_id(2) * NSA_TQ
    qh = _nsa_queries(q_ref[0], tab_ref, t0)
    gates = _gates_t(gt_ref, bg_ref, 0)
    nchunk = kc_ref.shape[2] // CMP_CHUNK
    last = t0 // (CMP_CHUNK * CMP_STRIDE)
    for nc in range(1, nchunk + 1):
        pl.when(last == nc - 1)(functools.partial(
            _cmp_body, nc, t0, qh, kc_ref, vct_ref, mt_ref, grp_ref, gates, o_ref, sel_ref, flag_ref))


def _cmp_to_slc_t(nseg, ns):
    c0 = np.arange(nseg)[:, None] * CMP_STRIDE
    s0 = np.arange(ns)[None, :] * SLC_BLOCK
    overlap = np.clip(np.minimum(c0 + CMP_BLOCK, s0 + SLC_BLOCK) - np.maximum(c0, s0), 0, None)
    m = overlap / CMP_STRIDE
    m[nseg - 1] = 0.0
    mt = np.zeros((NS_PAD, nseg))
    mt[:ns] = m.T
    return jnp.asarray(mt, BF16)


def _tile_groups():
    per = SLC_TK // SLC_BLOCK
    g = (np.arange(NS_PAD)[:, None] // per) == np.arange(NS_PAD)[None, :]
    return jnp.asarray(g, BF16)


def _cmp_attention(u3, kcmp, vcmp_t, gt, bg, *, q_col):
    B, S, _ = u3.shape
    nseg = kcmp.shape[2]
    nq = S // NSA_TQ
    sp = _nsa_specs(q_col)
    return pl.pallas_call(
        _cmp_kernel,
        grid=(B, NSA_GROUPS, S // NSA_TQ),
        in_specs=[
            sp["q"],
            pl.BlockSpec((1, 1, nseg, LANES), lambda b, g, i: (b, g, 0, 0)),
            pl.BlockSpec((1, 1, HEAD_DIM, nseg), lambda b, g, i: (b, g, 0, 0)),
            pl.BlockSpec((NS_PAD, nseg), lambda b, g, i: (0, 0)),
            pl.BlockSpec((NS_PAD, NS_PAD), lambda b, g, i: (0, 0)),
            sp["tab"], sp["gt"], sp["bg"],
        ],
        out_specs=[sp["out"], pl.BlockSpec((1, 1, NSA_TQ, NS_PAD), lambda b, g, i: (b, g, i, 0)),
                   pl.BlockSpec((1, 1, NS_PAD), lambda b, g, i: ((b * NSA_GROUPS + g) * nq + i, 0, 0))],
        out_shape=[jax.ShapeDtypeStruct((B, S, NSA_WIDTH), F32),
                   jax.ShapeDtypeStruct((B, NSA_GROUPS, S, NS_PAD), BF16),
                   jax.ShapeDtypeStruct((B * NSA_GROUPS * nq, 1, NS_PAD), jnp.int32)],
        compiler_params=_cp(("parallel", "parallel", "arbitrary")),
        name="cmp_attention",
    )(u3, kcmp, vcmp_t, _cmp_to_slc_t(nseg, S // SLC_BLOCK), _tile_groups(), _slope_table(), gt, bg)


def _slc_kernel(fl_ref, q_ref, k_ref, vt_ref, sel_ref, e_ref, tab_ref, gt_ref, bg_ref, o_ref,
                bm_sc, s0_sc, s1_sc, acc_sc, tiles_sm):
    qi = pl.program_id(2)
    tq, tk = NSA_TQ, SLC_TK
    t0 = qi * tq
    diag = t0 // tk
    row = (pl.program_id(0) * NSA_GROUPS + pl.program_id(1)) * pl.num_programs(2) + qi
    n_un = jnp.int32(0)
    for j in range(k_ref.shape[1] // tk):
        tiles_sm[n_un] = jnp.int32(j)
        n_un = n_un + ((fl_ref[row, j] > 0) & (j < diag)).astype(jnp.int32)
    tiles_sm[n_un] = diag
    selneg = sel_ref[0, 0]
    bm_sc[...] = jnp.concatenate([jnp.concatenate([qh, selneg], axis=1)
                                  for qh in _nsa_queries(q_ref[0], tab_ref, t0)], axis=0)
    qpos = t0 + lax.broadcasted_iota(jnp.int32, (tk, tq), 1)

    def k_tile(k0):
        return jnp.concatenate([k_ref[0, pl.ds(k0, tk), :], e_ref[pl.ds(k0, tk), :]], axis=1)

    def v_rows(h, k0):
        return vt_ref[0, :, pl.ds(k0, tk)]

    outs = _flash_t(bm_sc, (s0_sc, s1_sc), acc_sc, NSA_HPG, tq, tk, k_tile, v_rows,
                    n_un, lambda i: tiles_sm[i], diag, lambda key: key <= qpos)
    _store_heads(o_ref, outs, _gates_t(gt_ref, bg_ref, 1))


def _win_kernel(q_ref, k_ref, vt_ref, tab_ref, gt_ref, bg_ref, o_ref):
    qi = pl.program_id(2)
    tq = NSA_TQ
    nk = WINDOW + tq
    t0 = qi * tq
    start = pl.multiple_of(jnp.maximum(t0 - WINDOW, 0), LANES)
    bm = jnp.concatenate(_nsa_queries(q_ref[0], tab_ref, t0), axis=0)
    s_all = _dot_nt(k_ref[0, pl.ds(start, nk), :], bm)
    key = start + lax.broadcasted_iota(jnp.int32, (nk, tq), 0)
    qpos = t0 + lax.broadcasted_iota(jnp.int32, (nk, tq), 1)
    valid = (key <= qpos) & (key > qpos - WINDOW)
    lhs = jnp.concatenate([vt_ref[0, :, pl.ds(start, nk)],
                           jnp.ones((ACC_ROWS - HEAD_DIM, nk), BF16)], axis=0)
    outs = []
    for h in range(NSA_HPG):
        s = jnp.where(valid, s_all[:, h * tq:(h + 1) * tq], NEG)
        p = jnp.exp2(s - jnp.max(s, axis=0, keepdims=True)).astype(BF16)
        a = _dot(lhs, p)
        outs.append(a[:HEAD_DIM] * (1.0 / a[HEAD_DIM:HEAD_DIM + 1]))
    _store_heads(o_ref, outs, _gates_t(gt_ref, bg_ref, 2))


def _block_onehot(S):
    e = (np.arange(S)[:, None] // SLC_BLOCK) == np.arange(NS_PAD)[None, :]
    return jnp.asarray(e, BF16)


def _slc_attention(u3, vt, selneg, flags, gt, bg, *, q_col, k_col, v_row):
    B, S, _ = u3.shape
    sp = {k: pl.BlockSpec(v.block_shape, lambda b, g, i, fl, f=v.index_map: f(b, g, i))
          for k, v in _nsa_specs(q_col).items()}
    grid_spec = pltpu.PrefetchScalarGridSpec(
        num_scalar_prefetch=1,
        grid=(B, NSA_GROUPS, S // NSA_TQ),
        in_specs=[
            sp["q"],
            pl.BlockSpec((1, S, LANES), lambda b, g, i, fl: (b, 0, k_col + g)),
            pl.BlockSpec((1, HEAD_DIM, S), lambda b, g, i, fl: (b, v_row + g, 0)),
            pl.BlockSpec((1, 1, NSA_TQ, NS_PAD), lambda b, g, i, fl: (b, g, i, 0)),
            pl.BlockSpec((S, NS_PAD), lambda b, g, i, fl: (0, 0)),
            sp["tab"], sp["gt"], sp["bg"],
        ],
        out_specs=sp["out"],
        scratch_shapes=_flash_scratch(NSA_HPG, NSA_TQ, SLC_TK, 2 * LANES)
        + [pltpu.SMEM((S // SLC_TK + 1,), jnp.int32)],
    )
    return pl.pallas_call(
        _slc_kernel,
        grid_spec=grid_spec,
        out_shape=jax.ShapeDtypeStruct((B, S, NSA_WIDTH), F32),
        compiler_params=_cp(("parallel", "parallel", "arbitrary")),
        name="slc_attention",
    )(flags, u3, u3, vt, selneg, _block_onehot(S), _slope_table(), gt, bg)


def _win_attention(u3, vt, gt, bg, *, q_col, k_col, v_row):
    B, S, _ = u3.shape
    sp = _nsa_specs(q_col)
    return pl.pallas_call(
        _win_kernel,
        grid=(B, NSA_GROUPS, S // NSA_TQ),
        in_specs=[
            sp["q"],
            pl.BlockSpec((1, S, LANES), lambda b, g, i: (b, 0, k_col + g)),
            pl.BlockSpec((1, HEAD_DIM, S), lambda b, g, i: (b, v_row + g, 0)),
            sp["tab"], sp["gt"], sp["bg"],
        ],
        out_specs=sp["out"],
        out_shape=jax.ShapeDtypeStruct((B, S, NSA_WIDTH), F32),
        compiler_params=_cp(("parallel", "parallel", "arbitrary")),
        name="win_attention",
    )(u3, u3, vt, _slope_table(), gt, bg)


def _aug_groups(w):
    d = w.shape[0]
    w = w.reshape(d, NSA_GROUPS, HEAD_DIM)
    return jnp.pad(w, ((0, 0), (0, 0), (0, LANES - HEAD_DIM))).reshape(d, NSA_GROUPS * LANES)


def _even_layer(x, norm_g, w_in, b_f, gn_g, w_out):
    B, S, D = x.shape
    qscale = HEAD_DIM ** -0.5 * LOG2E
    q_f, k_f, v_f, w_fl, q_r, k_r, v_r, z = jnp.split(
        w_in, np.cumsum([FOX_WIDTH] * 3 + [FOX_HEADS] + [RET_WIDTH] * 3).tolist(), axis=1)
    w = jnp.concatenate([z, q_f * qscale, k_f, q_r, k_r * HEAD_DIM ** -0.5, v_r], axis=1).astype(BF16)
    x2 = x.reshape(B * S, D)
    u, vt = _proj(x2, norm_g, w, seq=S, w_t=[v_f.T.astype(BF16)], t_dtypes=[BF16])
    u3 = u.reshape(B, S, -1)
    kfeat, qfeat = _fgate(x, norm_g, w_fl, b_f, tile=min(512, S))
    cb = D // LANES
    nb = FOX_WIDTH // LANES
    o_f = _fox(u3, vt, kfeat, qfeat, q_col=cb, k_col=cb + nb)
    rb = (D + 2 * FOX_WIDTH) // RET_WIDTH
    o_r = _retention(u3, gn_g, q_col=rb, k_col=rb + 1, v_col=rb + 2)
    out = _out0(o_f.reshape(B * S, -1), o_r.reshape(B * S, -1), u, x2, w_out.astype(BF16))
    return out.reshape(B, S, D)


def _odd_layer(x, norm_g, w_in, b_gate, pe_k, pe_v, wk1, wk2, wv1, wv2, w_out, final_g):
    B, S, D = x.shape
    assert S // SLC_BLOCK <= NS_PAD
    qscale = HEAD_DIM ** -0.5 * LOG2E
    sizes = [NSA_WIDTH] + [NSA_KV_WIDTH] * 6 + [NSA_HEADS * N_BRANCH]
    q, kc, vc, ks, vs, kw, vw, gl, z = jnp.split(w_in, np.cumsum(sizes).tolist(), axis=1)
    w = jnp.concatenate([z, q * qscale, kc, vc, _aug_groups(ks), _aug_groups(kw)], axis=1).astype(BF16)
    per_group = NSA_HPG * N_BRANCH
    glt = jnp.pad(gl.T.reshape(NSA_GROUPS, per_group, D), ((0, 0), (0, GATE_ROWS - per_group), (0, 0)))
    glt = glt.reshape(NSA_GROUPS * GATE_ROWS, D).astype(BF16)
    bg = jnp.pad(b_gate.reshape(NSA_GROUPS, per_group), ((0, 0), (0, GATE_ROWS - per_group)))
    bg = bg.reshape(NSA_GROUPS * GATE_ROWS, 1)
    w_vt = jnp.concatenate([vs, vw], axis=1).T.astype(BF16)
    x2 = x.reshape(B * S, D)
    first_k = (2 * D + 2 * NSA_KV_WIDTH) // PROJ_TN
    u, kc_a, vc_a, vt, gt = _proj(
        x2, norm_g, w, seq=S, split_tile=(2 * D) // PROJ_TN,
        addend=_pos_features(jnp.arange(S, dtype=jnp.int32), PROJ_TN), add_tiles=(first_k, first_k + 1),
        w_t=[w_vt, glt], t_dtypes=[BF16, F32])
    u3 = u.reshape(B, S, -1)
    nseg = S // CMP_STRIDE
    kcmp = _compress(kc_a.reshape(B, nseg, -1), pe_k, wk1, wk2, transposed=False)
    vcmp_t = _compress(vc_a.reshape(B, nseg, -1), pe_v, wv1, wv2, transposed=True)
    q_col = D // (NSA_HPG * HEAD_DIM)
    o_c, selneg, flags = _cmp_attention(u3, kcmp, vcmp_t, gt, bg, q_col=q_col)
    kb = (2 * D + 2 * NSA_KV_WIDTH) // LANES
    o_s = _slc_attention(u3, vt, selneg, flags[:, 0, :S // SLC_TK], gt, bg, q_col=q_col, k_col=kb,
                         v_row=0)
    o_w = _win_attention(u3, vt, gt, bg, q_col=q_col, k_col=kb + NSA_GROUPS, v_row=NSA_GROUPS)
    r = lambda a: a.reshape(B * S, -1)
    out = _out1(r(o_c), r(o_s), r(o_w), u, x2, w_out.astype(BF16), final_g)
    return out.reshape(B, S, D)


def kernel(x, even_norm_g, even_w_in, even_b_f, even_gn_g, even_w_out, odd_norm_g, odd_w_in,
           odd_b_gate, odd_pe_k, odd_pe_v, odd_wk1, odd_wk2, odd_wv1, odd_wv2, odd_w_out, final_g):
    x = _even_layer(x, even_norm_g[0], even_w_in[0], even_b_f[0], even_gn_g[0], even_w_out[0])
    return _odd_layer(x, odd_norm_g[0], odd_w_in[0], odd_b_gate[0], odd_pe_k[0], odd_pe_v[0],
                      odd_wk1[0], odd_wk2[0], odd_wv1[0], odd_wv2[0], odd_w_out[0], final_g)
```

```python
import functools
import math

import jax
import jax.numpy as jnp
import numpy as np
from jax import lax
from jax.experimental import pallas as pl
from jax.experimental.pallas import tpu as pltpu

D_MODEL = 1024
HEAD_DIM = 64
LANES = 128
FOX_HEADS = 8
RET_HEADS = 8
FOX_WIDTH = FOX_HEADS * HEAD_DIM
RET_WIDTH = RET_HEADS * HEAD_DIM
RET_CHUNK = 128
NSA_HEADS = 16
NSA_GROUPS = 4
NSA_HPG = NSA_HEADS // NSA_GROUPS
NSA_WIDTH = NSA_HEADS * HEAD_DIM
NSA_KV_WIDTH = NSA_GROUPS * HEAD_DIM
N_BRANCH = 3
GATE_ROWS = 16
CMP_BLOCK = 32
CMP_STRIDE = 16
CMP_HIDDEN = 256
CMP_CHUNK = 128
SLC_BLOCK = 64
SLC_TOPK = 16
N_FORCED = 3
NS_PAD = LANES
WINDOW = 512
RMS_EPS = 1e-6
GN_EPS = 1e-5
NEG = -1e30
FORCE_BONUS = 1e6
MASK_BIG = 2.0 ** 100
LOG2E = math.log2(math.e)
FEAT0 = HEAD_DIM
QF_ROWS = 16
ACC_ROWS = HEAD_DIM + 16

PROJ_TM = 1024
PROJ_TN = 512
FOX_TQ = 512
FOX_TK = 512
NSA_TQ = 256
SLC_TK = 512
OUT_TM = 512
VMEM_LIMIT = 48 * 1024 * 1024

F32 = jnp.float32
BF16 = jnp.bfloat16


def _cp(sem, vmem=VMEM_LIMIT):
    return pltpu.CompilerParams(dimension_semantics=sem, vmem_limit_bytes=vmem)


def _dot(a, b):
    return jnp.dot(a, b, preferred_element_type=F32)


def _dot_nt(a, b):
    return lax.dot_general(a, b, (((1,), (1,)), ((), ())), preferred_element_type=F32)


def _dot_tn(a, b):
    return lax.dot_general(a, b, (((0,), (0,)), ((), ())), preferred_element_type=F32)


def _rms(x, g):
    return x * lax.rsqrt(jnp.mean(x * x, axis=-1, keepdims=True) + RMS_EPS) * g


def _silu(x):
    return x * (1.0 / (1.0 + jnp.exp(-x)))


def _sigmoid(x):
    return 1.0 / (1.0 + jnp.exp(-x))


def _low_half(shape, axis):
    return lax.broadcasted_iota(jnp.int32, shape, axis) < HEAD_DIM


def _pieces(v):
    p1 = v.astype(BF16).astype(F32)
    r = v - p1
    p2 = r.astype(BF16).astype(F32)
    p3 = (r - p2).astype(BF16).astype(F32)
    return p1, p2, p3


def _np_pieces(v):
    v = np.asarray(v, np.float64)
    bf = lambda a: np.asarray(a, np.float32).astype(BF16).astype(np.float64)
    p1 = bf(v)
    p2 = bf(v - p1)
    p3 = bf(v - p1 - p2)
    return p1, p2, p3


def _place(lane, cols):
    out = jnp.zeros(lane.shape, F32)
    for i, c in cols.items():
        out = jnp.where(lane == i, c, out)
    return out


def _fgate_kernel(x_ref, g_ref, wf_ref, b_ref, kf_ref, qf_ref, carry):
    @pl.when(pl.program_id(1) == 0)
    def _():
        carry[...] = jnp.zeros_like(carry)

    h = _rms(x_ref[0], g_ref[...])
    t = h.shape[0]
    h1 = h.astype(BF16)
    h2 = (h - h1.astype(F32)).astype(BF16)
    w = wf_ref[...]
    w1 = w.astype(BF16)
    w2 = (w - w1.astype(F32)).astype(BF16)
    f = _dot(h1, w1) + _dot(h1, w2) + _dot(h2, w1) + b_ref[...]
    ls = jnp.minimum(f, 0.0) - jnp.log(1.0 + jnp.exp(-jnp.abs(f)))
    r = lax.broadcasted_iota(jnp.int32, (t, t), 0)
    c = lax.broadcasted_iota(jnp.int32, (t, t), 1)
    lower = jnp.where(c <= r, 1.0, 0.0).astype(BF16)
    cs = sum(_dot(lower, p.astype(BF16)) for p in _pieces(ls)) + carry[...]
    carry[...] = cs[t - 1:t, :]
    cl = cs * LOG2E
    lane = lax.broadcasted_iota(jnp.int32, (t, LANES), 1)
    one = jnp.ones((t, 1), F32)
    for j in range(FOX_HEADS // 2):
        a1, a2, a3 = _pieces(cl[:, 2 * j:2 * j + 1])
        b1, b2, b3 = _pieces(cl[:, 2 * j + 1:2 * j + 2])
        kf = _place(lane, {0: a1, 1: a2, 2: a3, 3: one, 4: one, 5: one, 6: b1, 7: b2, 8: b3})
        kf_ref[0, j] = kf.astype(BF16)
        qa = _place(lane, {0: -one, 1: -one, 2: -one, 3: a1, 4: a2, 5: a3})
        qb = _place(lane, {3: b1, 4: b2, 5: b3, 6: -one, 7: -one, 8: -one})
        qf_ref[0, j, 0] = qa.T[:QF_ROWS].astype(BF16)
        qf_ref[0, j, 1] = qb.T[:QF_ROWS].astype(BF16)


def _fgate(x, g, wf, b_f, *, tile):
    B, S, D = x.shape
    npair = FOX_HEADS // 2
    return pl.pallas_call(
        _fgate_kernel,
        grid=(B, S // tile),
        in_specs=[
            pl.BlockSpec((1, tile, D), lambda b, s: (b, s, 0)),
            pl.BlockSpec((1, D), lambda b, s: (0, 0)),
            pl.BlockSpec((D, FOX_HEADS), lambda b, s: (0, 0)),
            pl.BlockSpec((1, FOX_HEADS), lambda b, s: (0, 0)),
        ],
        out_specs=[
            pl.BlockSpec((1, npair, tile, LANES), lambda b, s: (b, 0, s, 0)),
            pl.BlockSpec((1, npair, 2, QF_ROWS, tile), lambda b, s: (b, 0, 0, 0, s)),
        ],
        out_shape=[
            jax.ShapeDtypeStruct((B, npair, S, LANES), BF16),
            jax.ShapeDtypeStruct((B, npair, 2, QF_ROWS, S), BF16),
        ],
        scratch_shapes=[pltpu.VMEM((1, FOX_HEADS), F32)],
        compiler_params=_cp(("parallel", "arbitrary")),
        name="fgate",
    )(x, g.reshape(1, D), wf, b_f.reshape(1, FOX_HEADS))


def _proj_kernel(*refs, split_tile, add_tiles, n_t):
    it = iter(refs)
    x_ref, g_ref, w_ref = next(it), next(it), next(it)
    add_ref = next(it) if add_tiles else None
    wt_refs = [next(it) for _ in range(n_t)]
    u_ref = next(it)
    e_refs = [next(it), next(it)] if split_tile is not None else None
    ut_refs = [next(it) for _ in range(n_t)]
    h_sc = next(it)
    j = pl.program_id(1)

    @pl.when(j == 0)
    def _():
        h = _rms(x_ref[...], g_ref[...]).astype(BF16)
        h_sc[...] = h
        for wt_ref, ut_ref in zip(wt_refs, ut_refs):
            ut_ref[0] = _dot_nt(wt_ref[...], h).astype(ut_ref.dtype)

    acc = _dot(h_sc[...], w_ref[...])
    if add_tiles:
        is_add = functools.reduce(jnp.logical_or, [j == t for t in add_tiles])

        @pl.when(is_add)
        def _():
            u_ref[...] = (acc + add_ref[...].astype(F32)).astype(u_ref.dtype)

        @pl.when(jnp.logical_not(is_add))
        def _():
            u_ref[...] = acc.astype(u_ref.dtype)
    else:
        u_ref[...] = acc.astype(u_ref.dtype)
    if split_tile is not None:
        @pl.when(j == split_tile)
        def _():
            half = acc.shape[1] // 2
            e_refs[0][...] = acc[:, :half].astype(e_refs[0].dtype)
            e_refs[1][...] = acc[:, half:].astype(e_refs[1].dtype)


def _proj(x2, g, w, *, seq, split_tile=None, addend=None, add_tiles=(), w_t=(), t_dtypes=(),
          tm=PROJ_TM, tn=PROJ_TN):
    N, D = x2.shape
    W = w.shape[1]
    nbs = seq // tm
    in_specs = [
        pl.BlockSpec((tm, D), lambda i, j: (i, 0)),
        pl.BlockSpec((1, D), lambda i, j: (0, 0)),
        pl.BlockSpec((D, tn), lambda i, j: (0, j)),
    ]
    args = [x2, g.reshape(1, D), w]
    if add_tiles:
        in_specs.append(pl.BlockSpec((tm, tn), lambda i, j: (i % nbs, 0)))
        args.append(addend)
    out_shape = [jax.ShapeDtypeStruct((N, W), BF16)]
    out_specs = [pl.BlockSpec((tm, tn), lambda i, j: (i, j))]
    if split_tile is not None:
        out_shape += [jax.ShapeDtypeStruct((N, tn // 2), BF16)] * 2
        out_specs += [pl.BlockSpec((tm, tn // 2), lambda i, j: (i, 0))] * 2
    for wt, dt in zip(w_t, t_dtypes):
        rows = wt.shape[0]
        in_specs.append(pl.BlockSpec((rows, D), lambda i, j: (0, 0)))
        args.append(wt)
        out_shape.append(jax.ShapeDtypeStruct((N // seq, rows, seq), dt))
        out_specs.append(pl.BlockSpec((1, rows, tm), lambda i, j: (i // nbs, 0, i % nbs)))
    return pl.pallas_call(
        functools.partial(_proj_kernel, split_tile=split_tile, add_tiles=tuple(add_tiles),
                          n_t=len(w_t)),
        grid=(N // tm, W // tn),
        in_specs=in_specs,
        out_specs=out_specs,
        out_shape=out_shape,
        scratch_shapes=[pltpu.VMEM((tm, D), BF16)],
        compiler_params=_cp(("parallel", "arbitrary")),
        name="proj",
    )(*args)


def _flash_t(bm_sc, s_bufs, mx_sc, acc_sc, nh, tq, tk, k_tile, v_rows, n_un, tile_of, diag,
             valid_fn):
    acc_sc[...] = jnp.zeros_like(acc_sc)
    ones = jnp.ones((ACC_ROWS - HEAD_DIM, tk), BF16)
    key_iota = lax.broadcasted_iota(jnp.int32, (tk, tq), 0)

    def qk_head(tile, slot, h):
        cols = slice(h * tq, (h + 1) * tq)
        s = _dot(k_tile(pl.multiple_of(tile * tk, tk)), bm_sc[:, cols])
        s_bufs[slot][:, cols] = s
        mx_sc[slot, :, cols] = jnp.max(s, axis=0, keepdims=True)

    def soft_head(tile, slot, m_old, h, valid):
        cols = slice(h * tq, (h + 1) * tq)
        buf = s_bufs[slot]
        k0 = pl.multiple_of(tile * tk, tk)
        if valid is not None:
            s = jnp.where(valid, buf[:, cols], NEG)
            m_new = jnp.maximum(m_old, jnp.max(s, axis=0, keepdims=True))
            p = jnp.exp2(s - m_new).astype(BF16)
        else:
            m_new = jnp.maximum(m_old, mx_sc[slot, :, cols])
            p = jnp.exp2(buf[:, cols] - m_new).astype(BF16)
        alpha = jnp.exp2(m_old - m_new)
        lhs = jnp.concatenate([v_rows(h, k0), ones], axis=0)
        acc_sc[h] = alpha * acc_sc[h] + _dot(lhs, p)
        return m_new

    def step(cur, slot, ms, nxt=None, masked=False):
        valid = valid_fn(pl.multiple_of(cur * tk, tk) + key_iota) if masked else None
        out = []
        for h in range(nh):
            if nxt is not None:
                qk_head(nxt, 1 - slot, h)
            out.append(soft_head(cur, slot, ms[h], h, valid))
        return tuple(out)

    def pair(i, ms):
        t_a, t_b, t_c = tile_of(2 * i), tile_of(2 * i + 1), tile_of(2 * i + 2)
        return step(t_b, 1, step(t_a, 0, ms, nxt=t_b), nxt=t_c)

    def odd_tail(ms):
        return step(diag, 1, step(tile_of(n_un - 1), 0, ms, nxt=diag), masked=True)

    def even_tail(ms):
        return step(diag, 0, ms, masked=True)

    for h in range(nh):
        qk_head(tile_of(0), 0, h)
    ms = (jnp.full((1, tq), NEG, F32),) * nh
    ms = lax.fori_loop(0, n_un // 2, pair, ms)
    ms = lax.cond(n_un % 2 == 1, odd_tail, even_tail, ms)
    outs = []
    for h in range(nh):
        a = acc_sc[h]
        outs.append(a[:HEAD_DIM] * (1.0 / a[HEAD_DIM:HEAD_DIM + 1]))
    return outs


def _flash_scratch(nh, tq, tk, kdim):
    return [pltpu.VMEM((kdim, nh * tq), BF16), pltpu.VMEM((tk, nh * tq), F32),
            pltpu.VMEM((tk, nh * tq), F32), pltpu.VMEM((2, 1, nh * tq), F32),
            pltpu.VMEM((nh, ACC_ROWS, tq), F32)]


def _fox_kernel(qt_ref, qf_ref, k_ref, kf_ref, vt_ref, o_ref, bm_sc, s0_sc, s1_sc, mx_sc, acc_sc):
    qi = pl.program_id(2)
    tq, tk = FOX_TQ, FOX_TK
    bm_sc[...] = jnp.zeros_like(bm_sc)
    for h in range(2):
        rows = slice(h * HEAD_DIM, (h + 1) * HEAD_DIM)
        bm_sc[rows, h * tq:(h + 1) * tq] = qt_ref[0, rows, :]
        bm_sc[LANES:LANES + QF_ROWS, h * tq:(h + 1) * tq] = qf_ref[0, 0, h]
    qpos = qi * tq + lax.broadcasted_iota(jnp.int32, (tk, tq), 1)

    def k_tile(k0):
        return jnp.concatenate([k_ref[0, pl.ds(k0, tk), :], kf_ref[0, 0, pl.ds(k0, tk), :]], axis=1)

    def v_rows(h, k0):
        return vt_ref[0, h * HEAD_DIM:(h + 1) * HEAD_DIM, pl.ds(k0, tk)]

    diag = (qi * tq) // tk
    outs = _flash_t(bm_sc, (s0_sc, s1_sc), mx_sc, acc_sc, 2, tq, tk, k_tile, v_rows,
                    diag, lambda i: i, diag, lambda key: key <= qpos)
    o_ref[0] = jnp.concatenate(outs, axis=0).T


def _fox(u3, ut, kfeat, qfeat, *, q_row, k_col, v_row):
    B, S, _ = u3.shape
    npair = FOX_HEADS // 2
    return pl.pallas_call(
        _fox_kernel,
        grid=(B, npair, S // FOX_TQ),
        in_specs=[
            pl.BlockSpec((1, LANES, FOX_TQ), lambda b, j, i: (b, q_row + j, i)),
            pl.BlockSpec((1, 1, 2, QF_ROWS, FOX_TQ), lambda b, j, i: (b, j, 0, 0, i)),
            pl.BlockSpec((1, S, LANES), lambda b, j, i: (b, 0, k_col + j)),
            pl.BlockSpec((1, 1, S, LANES), lambda b, j, i: (b, j, 0, 0)),
            pl.BlockSpec((1, LANES, S), lambda b, j, i: (b, v_row + j, 0)),
        ],
        out_specs=pl.BlockSpec((1, FOX_TQ, LANES), lambda b, j, i: (b, i, j)),
        out_shape=jax.ShapeDtypeStruct((B, S, FOX_WIDTH), F32),
        scratch_shapes=_flash_scratch(2, FOX_TQ, FOX_TK, 2 * LANES),
        compiler_params=_cp(("parallel", "parallel", "arbitrary")),
        name="fox",
    )(ut, qfeat, u3, kfeat, ut)


def _ret_kernel(q_ref, k_ref, v_ref, inner_ref, cross_ref, kdec_ref, cd_ref, bd_ref, gn_ref,
                o_ref, state_sc):
    @pl.when(pl.program_id(1) == 0)
    def _():
        state_sc[...] = jnp.zeros_like(state_sc)

    low = _low_half((RET_CHUNK, LANES), 1)
    inv = 1.0 / HEAD_DIM
    for j in range(RET_HEADS // 2):
        cols = slice(j * LANES, (j + 1) * LANES)
        q, k, v = q_ref[0, :, cols], k_ref[0, :, cols], v_ref[0, :, cols]
        zero = jnp.zeros_like(q)
        qa, qb = jnp.where(low, q, zero), jnp.where(low, zero, q)
        pa = (_dot_nt(qa, k) * inner_ref[j, 0]).astype(BF16)
        pb = (_dot_nt(qb, k) * inner_ref[j, 1]).astype(BF16)
        o_in = jnp.where(low, _dot(pa, v), _dot(pb, v))
        state = state_sc[j]
        o = o_in + _dot(q, state.astype(BF16)) * cross_ref[j]
        kd = (k.astype(F32) * kdec_ref[j]).astype(BF16)
        state_sc[j] = state * cd_ref[j] + _dot_tn(kd, v) * bd_ref[...]
        sa = jnp.sum(jnp.where(low, o, 0.0), axis=-1, keepdims=True)
        st = jnp.sum(o, axis=-1, keepdims=True)
        mu = jnp.where(low, sa, st - sa) * inv
        d = o - mu
        d2 = d * d
        va = jnp.sum(jnp.where(low, d2, 0.0), axis=-1, keepdims=True)
        vt = jnp.sum(d2, axis=-1, keepdims=True)
        var = jnp.where(low, va, vt - va) * inv
        o_ref[0, :, cols] = d * lax.rsqrt(var + GN_EPS) * gn_ref[:, cols]


def _ret_constants():
    lg = np.log(1.0 - 2.0 ** (-5.0 - np.arange(RET_HEADS)))
    i = np.arange(RET_CHUNK)
    diff = i[:, None] - i[None, :]
    inner = np.where(diff[None] >= 0, np.exp(lg[:, None, None] * np.maximum(diff, 0)[None]), 0.0)
    cross = np.exp(lg[:, None] * (i[None, :] + 1))
    kdec = np.exp(lg[:, None] * (RET_CHUNK - 1 - i)[None, :])
    cdec = np.exp(lg * RET_CHUNK)
    npair = RET_HEADS // 2
    inner = inner.reshape(npair, 2, RET_CHUNK, RET_CHUNK)

    def lanes(a):
        a = a.reshape(npair, 2, RET_CHUNK)
        return np.repeat(a.transpose(0, 2, 1), HEAD_DIM, axis=2)

    bd = np.kron(np.eye(2), np.ones((HEAD_DIM, HEAD_DIM)))
    cd = np.repeat(cdec.reshape(npair, 2), HEAD_DIM, axis=1)[:, :, None] * bd[None]
    f = lambda a: jnp.asarray(a, F32)
    return f(inner), f(lanes(cross)), f(lanes(kdec)), f(cd), f(bd)


def _retention(u3, gn_g, *, q_col, k_col, v_col):
    B, S, _ = u3.shape
    C = RET_CHUNK
    npair = RET_HEADS // 2
    inner, cross, kdec, cd, bd = _ret_constants()
    full = lambda shape: pl.BlockSpec(shape, lambda b, i: (0,) * len(shape))
    return pl.pallas_call(
        _ret_kernel,
        grid=(B, S // C),
        in_specs=[
            pl.BlockSpec((1, C, RET_WIDTH), lambda b, i: (b, i, q_col)),
            pl.BlockSpec((1, C, RET_WIDTH), lambda b, i: (b, i, k_col)),
            pl.BlockSpec((1, C, RET_WIDTH), lambda b, i: (b, i, v_col)),
            full((npair, 2, C, C)), full((npair, C, LANES)), full((npair, C, LANES)),
            full((npair, LANES, LANES)), full((LANES, LANES)), full((1, RET_WIDTH)),
        ],
        out_specs=pl.BlockSpec((1, C, RET_WIDTH), lambda b, i: (b, i, 0)),
        out_shape=jax.ShapeDtypeStruct((B, S, RET_WIDTH), F32),
        scratch_shapes=[pltpu.VMEM((npair, LANES, LANES), F32)],
        compiler_params=_cp(("parallel", "arbitrary")),
        name="retention",
    )(u3, u3, u3, inner, cross, kdec, cd, bd, gn_g.reshape(1, RET_WIDTH))


def _out0_kernel(of_ref, or_ref, z_ref, x_ref, w_ref, o_ref):
    z = _silu(z_ref[...].astype(F32))
    ya = (of_ref[...] * z[:, :FOX_WIDTH]).astype(BF16)
    yb = (or_ref[...] * z[:, FOX_WIDTH:]).astype(BF16)
    o_ref[...] = x_ref[...] + _dot(ya, w_ref[:FOX_WIDTH, :]) + _dot(yb, w_ref[FOX_WIDTH:, :])


def _out0(o_f, o_r, u, x2, w_out, *, tm=OUT_TM):
    N, D = x2.shape
    return pl.pallas_call(
        _out0_kernel,
        grid=(N // tm,),
        in_specs=[
            pl.BlockSpec((tm, FOX_WIDTH), lambda i: (i, 0)),
            pl.BlockSpec((tm, RET_WIDTH), lambda i: (i, 0)),
            pl.BlockSpec((tm, D), lambda i: (i, 0)),
            pl.BlockSpec((tm, D), lambda i: (i, 0)),
            pl.BlockSpec((D, D), lambda i: (0, 0)),
        ],
        out_specs=pl.BlockSpec((tm, D), lambda i: (i, 0)),
        out_shape=jax.ShapeDtypeStruct((N, D), F32),
        compiler_params=_cp(("parallel",)),
        name="out0",
    )(o_f, o_r, u, x2, w_out)


def _out1_kernel(oc_ref, os_ref, ow_ref, z_ref, x_ref, w_ref, g_ref, o_ref):
    z = _silu(z_ref[...].astype(F32))
    y = ((oc_ref[...] + os_ref[...] + ow_ref[...]) * z).astype(BF16)
    o_ref[...] = _rms(x_ref[...] + _dot(y, w_ref[...]), g_ref[...])


def _out1(o_c, o_s, o_w, u, x2, w_out, final_g, *, tm=OUT_TM):
    N, D = x2.shape
    row = pl.BlockSpec((tm, D), lambda i: (i, 0))
    return pl.pallas_call(
        _out1_kernel,
        grid=(N // tm,),
        in_specs=[row, row, row, row, row,
                  pl.BlockSpec((D, D), lambda i: (0, 0)),
                  pl.BlockSpec((1, D), lambda i: (0, 0))],
        out_specs=row,
        out_shape=jax.ShapeDtypeStruct((N, D), F32),
        compiler_params=_cp(("parallel",)),
        name="out1",
    )(o_c, o_s, o_w, u, x2, w_out, final_g.reshape(1, D))


def _compress_kernel(x_ref, pea_ref, peb_ref, wa_ref, wb_ref, w2_ref, *rest, transposed):
    x = x_ref[0].astype(F32)
    a = _dot((x + pea_ref[...]).astype(BF16), wa_ref[0])
    b = _dot((x + peb_ref[...]).astype(BF16), wb_ref[0])
    nseg = x.shape[0]
    pre = a + pltpu.roll(b, nseg - 1, 0)
    hid = _silu(pre).astype(BF16)
    if transposed:
        o_ref, = rest
        o_ref[0, 0] = _dot_nt(w2_ref[...], hid).astype(o_ref.dtype)
    else:
        feat_ref, o_ref = rest
        o_ref[0, 0] = (_dot(hid, w2_ref[...]) + feat_ref[...].astype(F32)).astype(o_ref.dtype)


def _compress(a3, pe, w1, w2, *, transposed):
    B, nseg, wid = a3.shape
    half = CMP_STRIDE * HEAD_DIM
    eye = jnp.eye(NSA_GROUPS, dtype=w1.dtype)

    def big(wh):
        w4 = wh.reshape(CMP_STRIDE, 1, HEAD_DIM, CMP_HIDDEN)
        sel = eye[:, None, :, None, None]
        return (sel * w4[None]).reshape(NSA_GROUPS, wid, CMP_HIDDEN).astype(BF16)

    def pe_big(p):
        return jnp.broadcast_to(p[:, None, :], (CMP_STRIDE, NSA_GROUPS, HEAD_DIM)).reshape(1, wid)

    args = [a3, pe_big(pe[:CMP_STRIDE]), pe_big(pe[CMP_STRIDE:]), big(w1[:half]), big(w1[half:])]
    in_specs = [
        pl.BlockSpec((1, nseg, wid), lambda b, g: (b, 0, 0)),
        pl.BlockSpec((1, wid), lambda b, g: (0, 0)),
        pl.BlockSpec((1, wid), lambda b, g: (0, 0)),
        pl.BlockSpec((1, wid, CMP_HIDDEN), lambda b, g: (g, 0, 0)),
        pl.BlockSpec((1, wid, CMP_HIDDEN), lambda b, g: (g, 0, 0)),
    ]
    if transposed:
        w2d = w2.T.astype(BF16)
        oshape, oblock = (B, NSA_GROUPS, HEAD_DIM, nseg), (1, 1, HEAD_DIM, nseg)
        args.append(w2d)
        in_specs.append(pl.BlockSpec(w2d.shape, lambda b, g: (0, 0)))
    else:
        w2d = jnp.pad(w2, ((0, 0), (0, LANES - HEAD_DIM))).astype(BF16)
        oshape, oblock = (B, NSA_GROUPS, nseg, LANES), (1, 1, nseg, LANES)
        feat = _pos_features(jnp.arange(nseg, dtype=jnp.int32) * CMP_STRIDE + (CMP_BLOCK - 1), LANES)
        args += [w2d, feat]
        in_specs += [pl.BlockSpec(w2d.shape, lambda b, g: (0, 0)),
                     pl.BlockSpec((nseg, LANES), lambda b, g: (0, 0))]
    return pl.pallas_call(
        functools.partial(_compress_kernel, transposed=transposed),
        grid=(B, NSA_GROUPS),
        in_specs=in_specs,
        out_specs=pl.BlockSpec(oblock, lambda b, g: (b, g, 0, 0)),
        out_shape=jax.ShapeDtypeStruct(oshape, BF16),
        compiler_params=_cp(("parallel", "parallel")),
        name="compress",
    )(*args)


def _slope_table():
    s = np.asarray(2.0 ** (-8.0 * (np.arange(NSA_HEADS) + 1) / NSA_HEADS), np.float32)
    sl = np.asarray(s.astype(np.float64) * LOG2E, np.float32)
    p1, p2, p3 = _np_pieces(sl)
    tab = np.zeros((NSA_HEADS, QF_ROWS), np.float32)
    for k, p in enumerate((p1, p1, p2, p2, p3, p3)):
        tab[:, k] = p
    tab[:, 6] = sl
    tab = np.broadcast_to(tab.reshape(NSA_GROUPS, NSA_HPG * QF_ROWS, 1),
                          (NSA_GROUPS, NSA_HPG * QF_ROWS, NSA_TQ))
    return jnp.asarray(tab)


def _pos_features(pos, width):
    pos = pos[:, None]
    lane = jnp.arange(width, dtype=jnp.int32)[None, :] % LANES
    hi = ((pos // SLC_BLOCK) * SLC_BLOCK).astype(F32)
    lo = (pos % SLC_BLOCK).astype(F32)
    k = lane - FEAT0
    f = jnp.where((k >= 0) & (k < 6), jnp.where(k % 2 == 0, hi, lo), 0.0)
    f = jnp.where((k >= 6) & (k < 9), 1.0, f)
    return f.astype(BF16)


def _nsa_queries(qt_ref, tab_ref, t0):
    tq = qt_ref.shape[2]
    r = lax.broadcasted_iota(jnp.int32, (QF_ROWS, tq), 0)
    t = (t0 + lax.broadcasted_iota(jnp.int32, (1, tq), 1)).astype(F32)
    zeros = jnp.zeros((LANES - HEAD_DIM - QF_ROWS, tq), BF16)
    out = []
    for i in range(NSA_HPG):
        tile = tab_ref[0, i * QF_ROWS:(i + 1) * QF_ROWS, :]
        a1, a2, a3 = _pieces(-(tile[6:7, :] * t))
        feat = jnp.where(r == 6, a1, jnp.where(r == 7, a2, jnp.where(r == 8, a3,
                                                                     jnp.where(r < 6, tile, 0.0))))
        out.append(jnp.concatenate([qt_ref[0, i * HEAD_DIM:(i + 1) * HEAD_DIM, :],
                                    feat.astype(BF16), zeros], axis=0))
    return out


def _gates_t(gt_ref, bg_ref, branch):
    gl = gt_ref[0] + bg_ref[...]
    return [_sigmoid(gl[N_BRANCH * i + branch:N_BRANCH * i + branch + 1, :]) for i in range(NSA_HPG)]


def _store_heads(o_ref, outs_t, gates):
    g = [o * gt for o, gt in zip(outs_t, gates)]
    o_ref[0, :, :LANES] = jnp.concatenate(g[:2], axis=0).T
    o_ref[0, :, LANES:] = jnp.concatenate(g[2:], axis=0).T


def _nsa_specs(q_row):
    return dict(
        q=pl.BlockSpec((1, NSA_HPG * HEAD_DIM, NSA_TQ), lambda b, g, i: (b, q_row + g, i)),
        tab=pl.BlockSpec((1, NSA_HPG * QF_ROWS, NSA_TQ), lambda b, g, i: (g, 0, 0)),
        gt=pl.BlockSpec((1, GATE_ROWS, NSA_TQ), lambda b, g, i: (b, g, i)),
        bg=pl.BlockSpec((GATE_ROWS, 1), lambda b, g, i: (g, 0)),
        out=pl.BlockSpec((1, NSA_TQ, NSA_HPG * HEAD_DIM), lambda b, g, i: (b, i, g)),
    )


def _cmp_body(nc, t0, qh, kc_ref, vct_ref, mt_ref, grp_ref, gates, o_ref, sel_ref, flag_ref):
    tq = NSA_TQ
    rows = nc * CMP_CHUNK
    full = max(rows - CMP_CHUNK - 8, 0)
    nseg = kc_ref.shape[2]
    kc = kc_ref[0, 0, :rows, :]
    vct = vct_ref[0, 0, :, :rows]
    t = t0 + lax.broadcasted_iota(jnp.int32, (1, tq), 1)
    cidx = full + lax.broadcasted_iota(jnp.int32, (rows - full, 1), 0)
    valid = (cidx * CMP_STRIDE + (CMP_BLOCK - 1) <= t) & (cidx < nseg - 1)
    psum = jnp.zeros((rows, tq), F32)
    outs = []
    for i in range(NSA_HPG):
        s = _dot(kc, qh[i])
        s_last = jnp.where(valid, s[full:], NEG)
        m = jnp.max(s_last, axis=0, keepdims=True)
        if nc > 1:
            m = jnp.maximum(m, jnp.max(s[:full], axis=0, keepdims=True))
        e = jnp.where(valid, jnp.exp2(s_last - m), 0.0)
        if nc > 1:
            e = jnp.concatenate([jnp.exp2(s[:full] - m), e], axis=0)
        l = jnp.sum(e, axis=0, keepdims=True)
        p = e * jnp.where(l > 0.0, 1.0 / l, 0.0)
        psum = psum + p
        outs.append(_dot(vct, p.astype(BF16)))
    _store_heads(o_ref, outs, gates)
    ns = rows * CMP_STRIDE // SLC_BLOCK
    mt = mt_ref[:ns, :rows]
    imp = sum(_dot(mt, p.astype(BF16)) for p in _pieces(psum))
    blk = lax.broadcasted_iota(jnp.int32, (ns, 1), 0)
    cur = t // SLC_BLOCK
    bvalid = blk * SLC_BLOCK <= t
    forced = (blk == 0) | (blk == cur) | (blk == cur - 1)
    score = jnp.where(forced, -jnp.inf, jnp.where(bvalid, imp, NEG))
    blk_f = blk.astype(F32)
    sel = jnp.where(forced, 1.0, 0.0)
    for _ in range(SLC_TOPK - N_FORCED):
        mx = jnp.max(score, axis=0, keepdims=True)
        first = jnp.min(jnp.where(score == mx, blk_f, float(ns)), axis=0, keepdims=True)
        hit = blk_f == first
        sel = jnp.where(hit, 1.0, sel)
        score = jnp.where(hit, -jnp.inf, score)
    selneg = jnp.where(bvalid & (sel > 0.0), 0.0, -MASK_BIG)
    if ns < NS_PAD:
        selneg = jnp.concatenate([selneg, jnp.full((NS_PAD - ns, tq), -MASK_BIG, F32)], axis=0)
    sel_ref[0, 0] = selneg.astype(sel_ref.dtype)
    picked = jnp.where(selneg == 0.0, 1.0, 0.0).astype(BF16)
    used = _dot_nt(jnp.ones((8, tq), BF16), picked)
    used = jnp.where(used > 0.0, 1.0, 0.0).astype(BF16)
    flag_ref[0] = (_dot(used, grp_ref[...])[0:1] > 0.0).astype(jnp.int32)


def _cmp_kernel(q_ref, kc_ref, vct_ref, mt_ref, grp_ref, tab_ref, gt_ref, bg_ref,
                o_ref, sel_ref, flag_ref):
    t0 = pl.program_id(2) * NSA_TQ
    qh = _nsa_queries(q_ref, tab_ref, t0)
    gates = _gates_t(gt_ref, bg_ref, 0)
    nchunk = kc_ref.shape[2] // CMP_CHUNK
    last = t0 // (CMP_CHUNK * CMP_STRIDE)
    for nc in range(1, nchunk + 1):
        pl.when(last == nc - 1)(functools.partial(
            _cmp_body, nc, t0, qh, kc_ref, vct_ref, mt_ref, grp_ref, gates, o_ref, sel_ref, flag_ref))


def _cmp_to_slc_t(nseg, ns):
    c0 = np.arange(nseg)[:, None] * CMP_STRIDE
    s0 = np.arange(ns)[None, :] * SLC_BLOCK
    overlap = np.clip(np.minimum(c0 + CMP_BLOCK, s0 + SLC_BLOCK) - np.maximum(c0, s0), 0, None)
    m = overlap / CMP_STRIDE
    m[nseg - 1] = 0.0
    mt = np.zeros((NS_PAD, nseg))
    mt[:ns] = m.T
    return jnp.asarray(mt, BF16)


def _tile_groups():
    per = SLC_TK // SLC_BLOCK
    g = (np.arange(NS_PAD)[:, None] // per) == np.arange(NS_PAD)[None, :]
    return jnp.asarray(g, BF16)


def _cmp_attention(ut, kcmp, vcmp_t, gt, bg, *, q_row):
    B, _, S = ut.shape
    nseg = kcmp.shape[2]
    nq = S // NSA_TQ
    sp = _nsa_specs(q_row)
    return pl.pallas_call(
        _cmp_kernel,
        grid=(B, NSA_GROUPS, S // NSA_TQ),
        in_specs=[
            sp["q"],
            pl.BlockSpec((1, 1, nseg, LANES), lambda b, g, i: (b, g, 0, 0)),
            pl.BlockSpec((1, 1, HEAD_DIM, nseg), lambda b, g, i: (b, g, 0, 0)),
            pl.BlockSpec((NS_PAD, nseg), lambda b, g, i: (0, 0)),
            pl.BlockSpec((NS_PAD, NS_PAD), lambda b, g, i: (0, 0)),
            sp["tab"], sp["gt"], sp["bg"],
        ],
        out_specs=[sp["out"], pl.BlockSpec((1, 1, NS_PAD, NSA_TQ), lambda b, g, i: (b, g, 0, i)),
                   pl.BlockSpec((1, 1, NS_PAD), lambda b, g, i: ((b * NSA_GROUPS + g) * nq + i, 0, 0))],
        out_shape=[jax.ShapeDtypeStruct((B, S, NSA_WIDTH), F32),
                   jax.ShapeDtypeStruct((B, NSA_GROUPS, NS_PAD, S), BF16),
                   jax.ShapeDtypeStruct((B * NSA_GROUPS * nq, 1, NS_PAD), jnp.int32)],
        compiler_params=_cp(("parallel", "parallel", "arbitrary")),
        name="cmp_attention",
    )(ut, kcmp, vcmp_t, _cmp_to_slc_t(nseg, S // SLC_BLOCK), _tile_groups(), _slope_table(), gt, bg)


def _slc_kernel(fl_ref, q_ref, k_ref, vt_ref, sel_ref, e_ref, tab_ref, gt_ref, bg_ref, o_ref,
                bm_sc, s0_sc, s1_sc, mx_sc, acc_sc, tiles_sm):
    qi = pl.program_id(2)
    tq, tk = NSA_TQ, SLC_TK
    t0 = qi * tq
    diag = t0 // tk
    row = (pl.program_id(0) * NSA_GROUPS + pl.program_id(1)) * pl.num_programs(2) + qi
    n_un = jnp.int32(0)
    for j in range(k_ref.shape[1] // tk):
        tiles_sm[n_un] = jnp.int32(j)
        n_un = n_un + ((fl_ref[row, j] > 0) & (j < diag)).astype(jnp.int32)
    tiles_sm[n_un] = diag
    selneg = sel_ref[0, 0]
    bm_sc[...] = jnp.concatenate([jnp.concatenate([qh, selneg], axis=0)
                                  for qh in _nsa_queries(q_ref, tab_ref, t0)], axis=1)
    qpos = t0 + lax.broadcasted_iota(jnp.int32, (tk, tq), 1)

    def k_tile(k0):
        return jnp.concatenate([k_ref[0, pl.ds(k0, tk), :], e_ref[pl.ds(k0, tk), :]], axis=1)

    def v_rows(h, k0):
        return vt_ref[0, :, pl.ds(k0, tk)]

    outs = _flash_t(bm_sc, (s0_sc, s1_sc), mx_sc, acc_sc, NSA_HPG, tq, tk, k_tile, v_rows,
                    n_un, lambda i: tiles_sm[i], diag, lambda key: key <= qpos)
    _store_heads(o_ref, outs, _gates_t(gt_ref, bg_ref, 1))


def _win_kernel(q_ref, k_ref, vt_ref, tab_ref, gt_ref, bg_ref, o_ref):
    qi = pl.program_id(2)
    tq = NSA_TQ
    nk = WINDOW + tq
    t0 = qi * tq
    start = pl.multiple_of(jnp.maximum(t0 - WINDOW, 0), LANES)
    bm = jnp.concatenate(_nsa_queries(q_ref, tab_ref, t0), axis=1)
    s_all = _dot(k_ref[0, pl.ds(start, nk), :], bm)
    key = start + lax.broadcasted_iota(jnp.int32, (nk, tq), 0)
    qpos = t0 + lax.broadcasted_iota(jnp.int32, (nk, tq), 1)
    valid = (key <= qpos) & (key > qpos - WINDOW)
    lhs = jnp.concatenate([vt_ref[0, :, pl.ds(start, nk)],
                           jnp.ones((ACC_ROWS - HEAD_DIM, nk), BF16)], axis=0)
    outs = []
    for h in range(NSA_HPG):
        s = jnp.where(valid, s_all[:, h * tq:(h + 1) * tq], NEG)
        p = jnp.exp2(s - jnp.max(s, axis=0, keepdims=True)).astype(BF16)
        a = _dot(lhs, p)
        outs.append(a[:HEAD_DIM] * (1.0 / a[HEAD_DIM:HEAD_DIM + 1]))
    _store_heads(o_ref, outs, _gates_t(gt_ref, bg_ref, 2))


def _block_onehot(S):
    e = (np.arange(S)[:, None] // SLC_BLOCK) == np.arange(NS_PAD)[None, :]
    return jnp.asarray(e, BF16)


def _slc_attention(u3, ut, selneg, flags, gt, bg, *, q_row, k_col, v_row):
    B, S, _ = u3.shape
    sp = {k: pl.BlockSpec(v.block_shape, lambda b, g, i, fl, f=v.index_map: f(b, g, i))
          for k, v in _nsa_specs(q_row).items()}
    grid_spec = pltpu.PrefetchScalarGridSpec(
        num_scalar_prefetch=1,
        grid=(B, NSA_GROUPS, S // NSA_TQ),
        in_specs=[
            sp["q"],
            pl.BlockSpec((1, S, LANES), lambda b, g, i, fl: (b, 0, k_col + g)),
            pl.BlockSpec((1, HEAD_DIM, S), lambda b, g, i, fl: (b, v_row + g, 0)),
            pl.BlockSpec((1, 1, NS_PAD, NSA_TQ), lambda b, g, i, fl: (b, g, 0, i)),
            pl.BlockSpec((S, NS_PAD), lambda b, g, i, fl: (0, 0)),
            sp["tab"], sp["gt"], sp["bg"],
        ],
        out_specs=sp["out"],
        scratch_shapes=_flash_scratch(NSA_HPG, NSA_TQ, SLC_TK, 2 * LANES)
        + [pltpu.SMEM((S // SLC_TK + 1,), jnp.int32)],
    )
    return pl.pallas_call(
        _slc_kernel,
        grid_spec=grid_spec,
        out_shape=jax.ShapeDtypeStruct((B, S, NSA_WIDTH), F32),
        compiler_params=_cp(("parallel", "parallel", "arbitrary")),
        name="slc_attention",
    )(flags, ut, u3, ut, selneg, _block_onehot(S), _slope_table(), gt, bg)


def _win_attention(u3, ut, gt, bg, *, q_row, k_col, v_row):
    B, S, _ = u3.shape
    sp = _nsa_specs(q_row)
    return pl.pallas_call(
        _win_kernel,
        grid=(B, NSA_GROUPS, S // NSA_TQ),
        in_specs=[
            sp["q"],
            pl.BlockSpec((1, S, LANES), lambda b, g, i: (b, 0, k_col + g)),
            pl.BlockSpec((1, HEAD_DIM, S), lambda b, g, i: (b, v_row + g, 0)),
            sp["tab"], sp["gt"], sp["bg"],
        ],
        out_specs=sp["out"],
        out_shape=jax.ShapeDtypeStruct((B, S, NSA_WIDTH), F32),
        compiler_params=_cp(("parallel", "parallel", "arbitrary")),
        name="win_attention",
    )(ut, u3, ut, _slope_table(), gt, bg)


def _aug_groups(w):
    d = w.shape[0]
    w = w.reshape(d, NSA_GROUPS, HEAD_DIM)
    return jnp.pad(w, ((0, 0), (0, 0), (0, LANES - HEAD_DIM))).reshape(d, NSA_GROUPS * LANES)


def _even_layer(x, norm_g, w_in, b_f, gn_g, w_out):
    B, S, D = x.shape
    qscale = HEAD_DIM ** -0.5 * LOG2E
    q_f, k_f, v_f, w_fl, q_r, k_r, v_r, z = jnp.split(
        w_in, np.cumsum([FOX_WIDTH] * 3 + [FOX_HEADS] + [RET_WIDTH] * 3).tolist(), axis=1)
    w = jnp.concatenate([z, k_f, q_r, k_r * HEAD_DIM ** -0.5, v_r], axis=1).astype(BF16)
    w_t = jnp.concatenate([q_f * qscale, v_f], axis=1).T.astype(BF16)
    x2 = x.reshape(B * S, D)
    u, ut = _proj(x2, norm_g, w, seq=S, w_t=[w_t], t_dtypes=[BF16])
    u3 = u.reshape(B, S, -1)
    kfeat, qfeat = _fgate(x, norm_g, w_fl, b_f, tile=min(512, S))
    o_f = _fox(u3, ut, kfeat, qfeat, q_row=0, k_col=D // LANES, v_row=FOX_WIDTH // LANES)
    rb = (D + FOX_WIDTH) // RET_WIDTH
    o_r = _retention(u3, gn_g, q_col=rb, k_col=rb + 1, v_col=rb + 2)
    out = _out0(o_f.reshape(B * S, -1), o_r.reshape(B * S, -1), u, x2, w_out.astype(BF16))
    return out.reshape(B, S, D)


def _odd_layer(x, norm_g, w_in, b_gate, pe_k, pe_v, wk1, wk2, wv1, wv2, w_out, final_g):
    B, S, D = x.shape
    assert S // SLC_BLOCK <= NS_PAD
    qscale = HEAD_DIM ** -0.5 * LOG2E
    sizes = [NSA_WIDTH] + [NSA_KV_WIDTH] * 6 + [NSA_HEADS * N_BRANCH]
    q, kc, vc, ks, vs, kw, vw, gl, z = jnp.split(w_in, np.cumsum(sizes).tolist(), axis=1)
    w = jnp.concatenate([z, kc, vc, _aug_groups(ks), _aug_groups(kw)], axis=1).astype(BF16)
    per_group = NSA_HPG * N_BRANCH
    glt = jnp.pad(gl.T.reshape(NSA_GROUPS, per_group, D), ((0, 0), (0, GATE_ROWS - per_group), (0, 0)))
    glt = glt.reshape(NSA_GROUPS * GATE_ROWS, D).astype(BF16)
    bg = jnp.pad(b_gate.reshape(NSA_GROUPS, per_group), ((0, 0), (0, GATE_ROWS - per_group)))
    bg = bg.reshape(NSA_GROUPS * GATE_ROWS, 1)
    w_vt = jnp.concatenate([q * qscale, vs, vw], axis=1).T.astype(BF16)
    x2 = x.reshape(B * S, D)
    first_k = (D + 2 * NSA_KV_WIDTH) // PROJ_TN
    u, kc_a, vc_a, ut, gt = _proj(
        x2, norm_g, w, seq=S, split_tile=D // PROJ_TN,
        addend=_pos_features(jnp.arange(S, dtype=jnp.int32), PROJ_TN), add_tiles=(first_k, first_k + 1),
        w_t=[w_vt, glt], t_dtypes=[BF16, F32])
    u3 = u.reshape(B, S, -1)
    nseg = S // CMP_STRIDE
    kcmp = _compress(kc_a.reshape(B, nseg, -1), pe_k, wk1, wk2, transposed=False)
    vcmp_t = _compress(vc_a.reshape(B, nseg, -1), pe_v, wv1, wv2, transposed=True)
    o_c, selneg, flags = _cmp_attention(ut, kcmp, vcmp_t, gt, bg, q_row=0)
    kb = (D + 2 * NSA_KV_WIDTH) // LANES
    vb = NSA_WIDTH // HEAD_DIM
    o_s = _slc_attention(u3, ut, selneg, flags[:, 0, :S // SLC_TK], gt, bg, q_row=0, k_col=kb,
                         v_row=vb)
    o_w = _win_attention(u3, ut, gt, bg, q_row=0, k_col=kb + NSA_GROUPS, v_row=vb + NSA_GROUPS)
    r = lambda a: a.reshape(B * S, -1)
    out = _out1(r(o_c), r(o_s), r(o_w), u, x2, w_out.astype(BF16), final_g)
    return out.reshape(B, S, D)


def kernel(x, even_norm_g, even_w_in, even_b_f, even_gn_g, even_w_out, odd_norm_g, odd_w_in,
           odd_b_gate, odd_pe_k, odd_pe_v, odd_wk1, odd_wk2, odd_wv1, odd_wv2, odd_w_out, final_g):
    x = _even_layer(x, even_norm_g[0], even_w_in[0], even_b_f[0], even_gn_g[0], even_w_out[0])
    return _odd_layer(x, odd_norm_g[0], odd_w_in[0], odd_b_gate[0], odd_pe_k[0], odd_pe_v[0],
                      odd_wk1[0], odd_wk2[0], odd_wv1[0], odd_wv2[0], odd_w_out[0], final_g)
```

```python
import functools
import math

import jax
import jax.numpy as jnp
import numpy as np
from jax import lax
from jax.experimental import pallas as pl
from jax.experimental.pallas import tpu as pltpu

D_MODEL = 1024
HEAD_DIM = 64
LANES = 128
FOX_HEADS = 8
RET_HEADS = 8
FOX_WIDTH = FOX_HEADS * HEAD_DIM
RET_WIDTH = RET_HEADS * HEAD_DIM
RET_CHUNK = 128
NSA_HEADS = 16
NSA_GROUPS = 4
NSA_HPG = NSA_HEADS // NSA_GROUPS
NSA_WIDTH = NSA_HEADS * HEAD_DIM
NSA_KV_WIDTH = NSA_GROUPS * HEAD_DIM
N_BRANCH = 3
GATE_ROWS = 16
CMP_BLOCK = 32
CMP_STRIDE = 16
CMP_HIDDEN = 256
CMP_CHUNK = 128
SLC_BLOCK = 64
SLC_TOPK = 16
N_FORCED = 3
NS_PAD = LANES
WINDOW = 512
RMS_EPS = 1e-6
GN_EPS = 1e-5
NEG = -1e30
FORCE_BONUS = 1e6
MASK_BIG = 2.0 ** 100
LOG2E = math.log2(math.e)
FEAT0 = HEAD_DIM
QF_ROWS = 16
ACC_ROWS = HEAD_DIM + 16

PROJ_TM = 1024
PROJ_TN = 512
FOX_TQ = 512
FOX_TK = 512
NSA_TQ = 256
SLC_TK = 512
OUT_TM = 512
VMEM_LIMIT = 48 * 1024 * 1024

F32 = jnp.float32
BF16 = jnp.bfloat16


def _cp(sem, vmem=VMEM_LIMIT):
    return pltpu.CompilerParams(dimension_semantics=sem, vmem_limit_bytes=vmem)


def _dot(a, b):
    return jnp.dot(a, b, preferred_element_type=F32)


def _dot_nt(a, b):
    return lax.dot_general(a, b, (((1,), (1,)), ((), ())), preferred_element_type=F32)


def _dot_tn(a, b):
    return lax.dot_general(a, b, (((0,), (0,)), ((), ())), preferred_element_type=F32)


def _rms(x, g):
    return x * lax.rsqrt(jnp.mean(x * x, axis=-1, keepdims=True) + RMS_EPS) * g


def _silu(x):
    return x * (1.0 / (1.0 + jnp.exp(-x)))


def _sigmoid(x):
    return 1.0 / (1.0 + jnp.exp(-x))


def _low_half(shape, axis):
    return lax.broadcasted_iota(jnp.int32, shape, axis) < HEAD_DIM


def _pieces(v):
    p1 = v.astype(BF16).astype(F32)
    r = v - p1
    p2 = r.astype(BF16).astype(F32)
    p3 = (r - p2).astype(BF16).astype(F32)
    return p1, p2, p3


def _np_pieces(v):
    v = np.asarray(v, np.float64)
    bf = lambda a: np.asarray(a, np.float32).astype(BF16).astype(np.float64)
    p1 = bf(v)
    p2 = bf(v - p1)
    p3 = bf(v - p1 - p2)
    return p1, p2, p3


def _place(lane, cols):
    out = jnp.zeros(lane.shape, F32)
    for i, c in cols.items():
        out = jnp.where(lane == i, c, out)
    return out


def _fgate_kernel(x_ref, g_ref, wf_ref, b_ref, kf_ref, qf_ref, carry):
    @pl.when(pl.program_id(1) == 0)
    def _():
        carry[...] = jnp.zeros_like(carry)

    h = _rms(x_ref[0], g_ref[...])
    t = h.shape[0]
    h1 = h.astype(BF16)
    h2 = (h - h1.astype(F32)).astype(BF16)
    w = wf_ref[...]
    w1 = w.astype(BF16)
    w2 = (w - w1.astype(F32)).astype(BF16)
    f = _dot(h1, w1) + _dot(h1, w2) + _dot(h2, w1) + b_ref[...]
    ls = jnp.minimum(f, 0.0) - jnp.log(1.0 + jnp.exp(-jnp.abs(f)))
    r = lax.broadcasted_iota(jnp.int32, (t, t), 0)
    c = lax.broadcasted_iota(jnp.int32, (t, t), 1)
    lower = jnp.where(c <= r, 1.0, 0.0).astype(BF16)
    cs = sum(_dot(lower, p.astype(BF16)) for p in _pieces(ls)) + carry[...]
    carry[...] = cs[t - 1:t, :]
    cl = cs * LOG2E
    lane = lax.broadcasted_iota(jnp.int32, (t, LANES), 1)
    one = jnp.ones((t, 1), F32)
    for j in range(FOX_HEADS // 2):
        a1, a2, a3 = _pieces(cl[:, 2 * j:2 * j + 1])
        b1, b2, b3 = _pieces(cl[:, 2 * j + 1:2 * j + 2])
        kf = _place(lane, {0: a1, 1: a2, 2: a3, 3: one, 4: one, 5: one, 6: b1, 7: b2, 8: b3})
        kf_ref[0, j] = kf.astype(BF16)
        qa = _place(lane, {0: -one, 1: -one, 2: -one, 3: a1, 4: a2, 5: a3})
        qb = _place(lane, {3: b1, 4: b2, 5: b3, 6: -one, 7: -one, 8: -one})
        qf_ref[0, j, 0] = qa.T[:QF_ROWS].astype(BF16)
        qf_ref[0, j, 1] = qb.T[:QF_ROWS].astype(BF16)


def _fgate(x, g, wf, b_f, *, tile):
    B, S, D = x.shape
    npair = FOX_HEADS // 2
    return pl.pallas_call(
        _fgate_kernel,
        grid=(B, S // tile),
        in_specs=[
            pl.BlockSpec((1, tile, D), lambda b, s: (b, s, 0)),
            pl.BlockSpec((1, D), lambda b, s: (0, 0)),
            pl.BlockSpec((D, FOX_HEADS), lambda b, s: (0, 0)),
            pl.BlockSpec((1, FOX_HEADS), lambda b, s: (0, 0)),
        ],
        out_specs=[
            pl.BlockSpec((1, npair, tile, LANES), lambda b, s: (b, 0, s, 0)),
            pl.BlockSpec((1, npair, 2, QF_ROWS, tile), lambda b, s: (b, 0, 0, 0, s)),
        ],
        out_shape=[
            jax.ShapeDtypeStruct((B, npair, S, LANES), BF16),
            jax.ShapeDtypeStruct((B, npair, 2, QF_ROWS, S), BF16),
        ],
        scratch_shapes=[pltpu.VMEM((1, FOX_HEADS), F32)],
        compiler_params=_cp(("parallel", "arbitrary")),
        name="fgate",
    )(x, g.reshape(1, D), wf, b_f.reshape(1, FOX_HEADS))


def _proj_kernel(*refs, split_tile, add_tiles, n_t):
    it = iter(refs)
    x_ref, g_ref, w_ref = next(it), next(it), next(it)
    add_ref = next(it) if add_tiles else None
    wt_refs = [next(it) for _ in range(n_t)]
    u_ref = next(it)
    e_refs = [next(it), next(it)] if split_tile is not None else None
    ut_refs = [next(it) for _ in range(n_t)]
    h_sc = next(it)
    j = pl.program_id(1)

    @pl.when(j == 0)
    def _():
        h = _rms(x_ref[...], g_ref[...]).astype(BF16)
        h_sc[...] = h
        for wt_ref, ut_ref in zip(wt_refs, ut_refs):
            ut_ref[0] = _dot_nt(wt_ref[...], h).astype(ut_ref.dtype)

    acc = _dot(h_sc[...], w_ref[...])
    if add_tiles:
        is_add = functools.reduce(jnp.logical_or, [j == t for t in add_tiles])

        @pl.when(is_add)
        def _():
            u_ref[...] = (acc + add_ref[...].astype(F32)).astype(u_ref.dtype)

        @pl.when(jnp.logical_not(is_add))
        def _():
            u_ref[...] = acc.astype(u_ref.dtype)
    else:
        u_ref[...] = acc.astype(u_ref.dtype)
    if split_tile is not None:
        @pl.when(j == split_tile)
        def _():
            half = acc.shape[1] // 2
            e_refs[0][...] = acc[:, :half].astype(e_refs[0].dtype)
            e_refs[1][...] = acc[:, half:].astype(e_refs[1].dtype)


def _proj(x2, g, w, *, seq, split_tile=None, addend=None, add_tiles=(), w_t=(), t_dtypes=(),
          tm=PROJ_TM, tn=PROJ_TN):
    N, D = x2.shape
    W = w.shape[1]
    nbs = seq // tm
    in_specs = [
        pl.BlockSpec((tm, D), lambda i, j: (i, 0)),
        pl.BlockSpec((1, D), lambda i, j: (0, 0)),
        pl.BlockSpec((D, tn), lambda i, j: (0, j)),
    ]
    args = [x2, g.reshape(1, D), w]
    if add_tiles:
        in_specs.append(pl.BlockSpec((tm, tn), lambda i, j: (i % nbs, 0)))
        args.append(addend)
    out_shape = [jax.ShapeDtypeStruct((N, W), BF16)]
    out_specs = [pl.BlockSpec((tm, tn), lambda i, j: (i, j))]
    if split_tile is not None:
        out_shape += [jax.ShapeDtypeStruct((N, tn // 2), BF16)] * 2
        out_specs += [pl.BlockSpec((tm, tn // 2), lambda i, j: (i, 0))] * 2
    for wt, dt in zip(w_t, t_dtypes):
        rows = wt.shape[0]
        in_specs.append(pl.BlockSpec((rows, D), lambda i, j: (0, 0)))
        args.append(wt)
        out_shape.append(jax.ShapeDtypeStruct((N // seq, rows, seq), dt))
        out_specs.append(pl.BlockSpec((1, rows, tm), lambda i, j: (i // nbs, 0, i % nbs)))
    return pl.pallas_call(
        functools.partial(_proj_kernel, split_tile=split_tile, add_tiles=tuple(add_tiles),
                          n_t=len(w_t)),
        grid=(N // tm, W // tn),
        in_specs=in_specs,
        out_specs=out_specs,
        out_shape=out_shape,
        scratch_shapes=[pltpu.VMEM((tm, D), BF16)],
        compiler_params=_cp(("parallel", "arbitrary")),
        name="proj",
    )(*args)


def _flash_t(bm_sc, s_bufs, mx_sc, acc_sc, nh, tq, tk, k_tile, v_rows, n_un, tile_of, diag,
             valid_fn):
    acc_sc[...] = jnp.zeros_like(acc_sc)
    ones = jnp.ones((ACC_ROWS - HEAD_DIM, tk), BF16)
    key_iota = lax.broadcasted_iota(jnp.int32, (tk, tq), 0)

    def qk_head(tile, slot, h):
        cols = slice(h * tq, (h + 1) * tq)
        s = _dot(k_tile(pl.multiple_of(tile * tk, tk)), bm_sc[:, cols])
        s_bufs[slot][:, cols] = s
        mx_sc[slot, :, cols] = jnp.max(s, axis=0, keepdims=True)

    def soft_head(tile, slot, m_old, h, valid):
        cols = slice(h * tq, (h + 1) * tq)
        buf = s_bufs[slot]
        k0 = pl.multiple_of(tile * tk, tk)
        if valid is not None:
            s = jnp.where(valid, buf[:, cols], NEG)
            m_new = jnp.maximum(m_old, jnp.max(s, axis=0, keepdims=True))
            p = jnp.exp2(s - m_new).astype(BF16)
        else:
            m_new = jnp.maximum(m_old, mx_sc[slot, :, cols])
            p = jnp.exp2(buf[:, cols] - m_new).astype(BF16)
        alpha = jnp.exp2(m_old - m_new)
        lhs = jnp.concatenate([v_rows(h, k0), ones], axis=0)
        acc_sc[h] = alpha * acc_sc[h] + _dot(lhs, p)
        return m_new

    def step(cur, slot, ms, nxt=None, masked=False):
        valid = valid_fn(pl.multiple_of(cur * tk, tk) + key_iota) if masked else None
        out = []
        for h in range(nh):
            if nxt is not None:
                qk_head(nxt, 1 - slot, h)
            out.append(soft_head(cur, slot, ms[h], h, valid))
        return tuple(out)

    def pair(i, ms):
        t_a, t_b, t_c = tile_of(2 * i), tile_of(2 * i + 1), tile_of(2 * i + 2)
        return step(t_b, 1, step(t_a, 0, ms, nxt=t_b), nxt=t_c)

    def odd_tail(ms):
        return step(diag, 1, step(tile_of(n_un - 1), 0, ms, nxt=diag), masked=True)

    def even_tail(ms):
        return step(diag, 0, ms, masked=True)

    for h in range(nh):
        qk_head(tile_of(0), 0, h)
    ms = (jnp.full((1, tq), NEG, F32),) * nh
    ms = lax.fori_loop(0, n_un // 2, pair, ms)
    ms = lax.cond(n_un % 2 == 1, odd_tail, even_tail, ms)
    outs = []
    for h in range(nh):
        a = acc_sc[h]
        outs.append(a[:HEAD_DIM] * (1.0 / a[HEAD_DIM:HEAD_DIM + 1]))
    return outs


def _flash_scratch(nh, tq, tk, kdim):
    return [pltpu.VMEM((kdim, nh * tq), BF16), pltpu.VMEM((tk, nh * tq), F32),
            pltpu.VMEM((tk, nh * tq), F32), pltpu.VMEM((2, 1, nh * tq), F32),
            pltpu.VMEM((nh, ACC_ROWS, tq), F32)]


def _fox_kernel(qt_ref, qf_ref, k_ref, kf_ref, vt_ref, o_ref, bm_sc, s0_sc, s1_sc, mx_sc, acc_sc):
    qi = pl.program_id(2)
    tq, tk = FOX_TQ, FOX_TK
    bm_sc[...] = jnp.zeros_like(bm_sc)
    for h in range(2):
        rows = slice(h * HEAD_DIM, (h + 1) * HEAD_DIM)
        bm_sc[rows, h * tq:(h + 1) * tq] = qt_ref[0, rows, :]
        bm_sc[LANES:LANES + QF_ROWS, h * tq:(h + 1) * tq] = qf_ref[0, 0, h]
    qpos = qi * tq + lax.broadcasted_iota(jnp.int32, (tk, tq), 1)

    def k_tile(k0):
        return jnp.concatenate([k_ref[0, pl.ds(k0, tk), :], kf_ref[0, 0, pl.ds(k0, tk), :]], axis=1)

    def v_rows(h, k0):
        return vt_ref[0, h * HEAD_DIM:(h + 1) * HEAD_DIM, pl.ds(k0, tk)]

    diag = (qi * tq) // tk
    outs = _flash_t(bm_sc, (s0_sc, s1_sc), mx_sc, acc_sc, 2, tq, tk, k_tile, v_rows,
                    diag, lambda i: i, diag, lambda key: key <= qpos)
    o_ref[0] = jnp.concatenate(outs, axis=0).T


def _fox(u3, ut, kfeat, qfeat, *, q_row, k_col, v_row):
    B, S, _ = u3.shape
    npair = FOX_HEADS // 2
    return pl.pallas_call(
        _fox_kernel,
        grid=(B, npair, S // FOX_TQ),
        in_specs=[
            pl.BlockSpec((1, LANES, FOX_TQ), lambda b, j, i: (b, q_row + j, i)),
            pl.BlockSpec((1, 1, 2, QF_ROWS, FOX_TQ), lambda b, j, i: (b, j, 0, 0, i)),
            pl.BlockSpec((1, S, LANES), lambda b, j, i: (b, 0, k_col + j)),
            pl.BlockSpec((1, 1, S, LANES), lambda b, j, i: (b, j, 0, 0)),
            pl.BlockSpec((1, LANES, S), lambda b, j, i: (b, v_row + j, 0)),
        ],
        out_specs=pl.BlockSpec((1, FOX_TQ, LANES), lambda b, j, i: (b, i, j)),
        out_shape=jax.ShapeDtypeStruct((B, S, FOX_WIDTH), F32),
        scratch_shapes=_flash_scratch(2, FOX_TQ, FOX_TK, 2 * LANES),
        compiler_params=_cp(("parallel", "parallel", "arbitrary")),
        name="fox",
    )(ut, qfeat, u3, kfeat, ut)


def _ret_kernel(q_ref, k_ref, v_ref, inner_ref, cross_ref, kdec_ref, cd_ref, bd_ref, gn_ref,
                o_ref, state_sc):
    @pl.when(pl.program_id(1) == 0)
    def _():
        state_sc[...] = jnp.zeros_like(state_sc)

    low = _low_half((RET_CHUNK, LANES), 1)
    inv = 1.0 / HEAD_DIM
    for j in range(RET_HEADS // 2):
        cols = slice(j * LANES, (j + 1) * LANES)
        q, k, v = q_ref[0, :, cols], k_ref[0, :, cols], v_ref[0, :, cols]
        zero = jnp.zeros_like(q)
        qa, qb = jnp.where(low, q, zero), jnp.where(low, zero, q)
        pa = (_dot_nt(qa, k) * inner_ref[j, 0]).astype(BF16)
        pb = (_dot_nt(qb, k) * inner_ref[j, 1]).astype(BF16)
        o_in = jnp.where(low, _dot(pa, v), _dot(pb, v))
        state = state_sc[j]
        o = o_in + _dot(q, state.astype(BF16)) * cross_ref[j]
        kd = (k.astype(F32) * kdec_ref[j]).astype(BF16)
        state_sc[j] = state * cd_ref[j] + _dot_tn(kd, v) * bd_ref[...]
        sa = jnp.sum(jnp.where(low, o, 0.0), axis=-1, keepdims=True)
        st = jnp.sum(o, axis=-1, keepdims=True)
        mu = jnp.where(low, sa, st - sa) * inv
        d = o - mu
        d2 = d * d
        va = jnp.sum(jnp.where(low, d2, 0.0), axis=-1, keepdims=True)
        vt = jnp.sum(d2, axis=-1, keepdims=True)
        var = jnp.where(low, va, vt - va) * inv
        o_ref[0, :, cols] = d * lax.rsqrt(var + GN_EPS) * gn_ref[:, cols]


def _ret_constants():
    lg = np.log(1.0 - 2.0 ** (-5.0 - np.arange(RET_HEADS)))
    i = np.arange(RET_CHUNK)
    diff = i[:, None] - i[None, :]
    inner = np.where(diff[None] >= 0, np.exp(lg[:, None, None] * np.maximum(diff, 0)[None]), 0.0)
    cross = np.exp(lg[:, None] * (i[None, :] + 1))
    kdec = np.exp(lg[:, None] * (RET_CHUNK - 1 - i)[None, :])
    cdec = np.exp(lg * RET_CHUNK)
    npair = RET_HEADS // 2
    inner = inner.reshape(npair, 2, RET_CHUNK, RET_CHUNK)

    def lanes(a):
        a = a.reshape(npair, 2, RET_CHUNK)
        return np.repeat(a.transpose(0, 2, 1), HEAD_DIM, axis=2)

    bd = np.kron(np.eye(2), np.ones((HEAD_DIM, HEAD_DIM)))
    cd = np.repeat(cdec.reshape(npair, 2), HEAD_DIM, axis=1)[:, :, None] * bd[None]
    f = lambda a: jnp.asarray(a, F32)
    return f(inner), f(lanes(cross)), f(lanes(kdec)), f(cd), f(bd)


def _retention(u3, gn_g, *, q_col, k_col, v_col):
    B, S, _ = u3.shape
    C = RET_CHUNK
    npair = RET_HEADS // 2
    inner, cross, kdec, cd, bd = _ret_constants()
    full = lambda shape: pl.BlockSpec(shape, lambda b, i: (0,) * len(shape))
    return pl.pallas_call(
        _ret_kernel,
        grid=(B, S // C),
        in_specs=[
            pl.BlockSpec((1, C, RET_WIDTH), lambda b, i: (b, i, q_col)),
            pl.BlockSpec((1, C, RET_WIDTH), lambda b, i: (b, i, k_col)),
            pl.BlockSpec((1, C, RET_WIDTH), lambda b, i: (b, i, v_col)),
            full((npair, 2, C, C)), full((npair, C, LANES)), full((npair, C, LANES)),
            full((npair, LANES, LANES)), full((LANES, LANES)), full((1, RET_WIDTH)),
        ],
        out_specs=pl.BlockSpec((1, C, RET_WIDTH), lambda b, i: (b, i, 0)),
        out_shape=jax.ShapeDtypeStruct((B, S, RET_WIDTH), F32),
        scratch_shapes=[pltpu.VMEM((npair, LANES, LANES), F32)],
        compiler_params=_cp(("parallel", "arbitrary")),
        name="retention",
    )(u3, u3, u3, inner, cross, kdec, cd, bd, gn_g.reshape(1, RET_WIDTH))


def _out0_kernel(of_ref, or_ref, z_ref, x_ref, w_ref, o_ref):
    z = _silu(z_ref[...].astype(F32))
    ya = (of_ref[...] * z[:, :FOX_WIDTH]).astype(BF16)
    yb = (or_ref[...] * z[:, FOX_WIDTH:]).astype(BF16)
    o_ref[...] = x_ref[...] + _dot(ya, w_ref[:FOX_WIDTH, :]) + _dot(yb, w_ref[FOX_WIDTH:, :])


def _out0(o_f, o_r, u, x2, w_out, *, tm=OUT_TM):
    N, D = x2.shape
    return pl.pallas_call(
        _out0_kernel,
        grid=(N // tm,),
        in_specs=[
            pl.BlockSpec((tm, FOX_WIDTH), lambda i: (i, 0)),
            pl.BlockSpec((tm, RET_WIDTH), lambda i: (i, 0)),
            pl.BlockSpec((tm, D), lambda i: (i, 0)),
            pl.BlockSpec((tm, D), lambda i: (i, 0)),
            pl.BlockSpec((D, D), lambda i: (0, 0)),
        ],
        out_specs=pl.BlockSpec((tm, D), lambda i: (i, 0)),
        out_shape=jax.ShapeDtypeStruct((N, D), F32),
        compiler_params=_cp(("parallel",)),
        name="out0",
    )(o_f, o_r, u, x2, w_out)


def _out1_kernel(oc_ref, os_ref, ow_ref, z_ref, x_ref, w_ref, g_ref, o_ref):
    z = _silu(z_ref[...].astype(F32))
    y = ((oc_ref[...] + os_ref[...] + ow_ref[...]) * z).astype(BF16)
    o_ref[...] = _rms(x_ref[...] + _dot(y, w_ref[...]), g_ref[...])


def _out1(o_c, o_s, o_w, u, x2, w_out, final_g, *, tm=OUT_TM):
    N, D = x2.shape
    row = pl.BlockSpec((tm, D), lambda i: (i, 0))
    return pl.pallas_call(
        _out1_kernel,
        grid=(N // tm,),
        in_specs=[row, row, row, row, row,
                  pl.BlockSpec((D, D), lambda i: (0, 0)),
                  pl.BlockSpec((1, D), lambda i: (0, 0))],
        out_specs=row,
        out_shape=jax.ShapeDtypeStruct((N, D), F32),
        compiler_params=_cp(("parallel",)),
        name="out1",
    )(o_c, o_s, o_w, u, x2, w_out, final_g.reshape(1, D))


def _compress_kernel(x_ref, pea_ref, peb_ref, wa_ref, wb_ref, w2_ref, *rest, transposed):
    x = x_ref[0].astype(F32)
    a = _dot((x + pea_ref[...]).astype(BF16), wa_ref[0])
    b = _dot((x + peb_ref[...]).astype(BF16), wb_ref[0])
    nseg = x.shape[0]
    pre = a + pltpu.roll(b, nseg - 1, 0)
    hid = _silu(pre).astype(BF16)
    if transposed:
        o_ref, = rest
        o_ref[0, 0] = _dot_nt(w2_ref[...], hid).astype(o_ref.dtype)
    else:
        feat_ref, o_ref = rest
        o_ref[0, 0] = (_dot(hid, w2_ref[...]) + feat_ref[...].astype(F32)).astype(o_ref.dtype)


def _compress(a3, pe, w1, w2, *, transposed):
    B, nseg, wid = a3.shape
    half = CMP_STRIDE * HEAD_DIM
    eye = jnp.eye(NSA_GROUPS, dtype=w1.dtype)

    def big(wh):
        w4 = wh.reshape(CMP_STRIDE, 1, HEAD_DIM, CMP_HIDDEN)
        sel = eye[:, None, :, None, None]
        return (sel * w4[None]).reshape(NSA_GROUPS, wid, CMP_HIDDEN).astype(BF16)

    def pe_big(p):
        return jnp.broadcast_to(p[:, None, :], (CMP_STRIDE, NSA_GROUPS, HEAD_DIM)).reshape(1, wid)

    args = [a3, pe_big(pe[:CMP_STRIDE]), pe_big(pe[CMP_STRIDE:]), big(w1[:half]), big(w1[half:])]
    in_specs = [
        pl.BlockSpec((1, nseg, wid), lambda b, g: (b, 0, 0)),
        pl.BlockSpec((1, wid), lambda b, g: (0, 0)),
        pl.BlockSpec((1, wid), lambda b, g: (0, 0)),
        pl.BlockSpec((1, wid, CMP_HIDDEN), lambda b, g: (g, 0, 0)),
        pl.BlockSpec((1, wid, CMP_HIDDEN), lambda b, g: (g, 0, 0)),
    ]
    if transposed:
        w2d = w2.T.astype(BF16)
        oshape, oblock = (B, NSA_GROUPS, HEAD_DIM, nseg), (1, 1, HEAD_DIM, nseg)
        args.append(w2d)
        in_specs.append(pl.BlockSpec(w2d.shape, lambda b, g: (0, 0)))
    else:
        w2d = jnp.pad(w2, ((0, 0), (0, LANES - HEAD_DIM))).astype(BF16)
        oshape, oblock = (B, NSA_GROUPS, nseg, LANES), (1, 1, nseg, LANES)
        feat = _pos_features(jnp.arange(nseg, dtype=jnp.int32) * CMP_STRIDE + (CMP_BLOCK - 1), LANES)
        args += [w2d, feat]
        in_specs += [pl.BlockSpec(w2d.shape, lambda b, g: (0, 0)),
                     pl.BlockSpec((nseg, LANES), lambda b, g: (0, 0))]
    return pl.pallas_call(
        functools.partial(_compress_kernel, transposed=transposed),
        grid=(B, NSA_GROUPS),
        in_specs=in_specs,
        out_specs=pl.BlockSpec(oblock, lambda b, g: (b, g, 0, 0)),
        out_shape=jax.ShapeDtypeStruct(oshape, BF16),
        compiler_params=_cp(("parallel", "parallel")),
        name="compress",
    )(*args)


def _slope_table():
    s = np.asarray(2.0 ** (-8.0 * (np.arange(NSA_HEADS) + 1) / NSA_HEADS), np.float32)
    sl = np.asarray(s.astype(np.float64) * LOG2E, np.float32)
    p1, p2, p3 = _np_pieces(sl)
    tab = np.zeros((NSA_HEADS, QF_ROWS), np.float32)
    for k, p in enumerate((p1, p1, p2, p2, p3, p3)):
        tab[:, k] = p
    tab[:, 6] = sl
    tab = np.broadcast_to(tab.reshape(NSA_GROUPS, NSA_HPG * QF_ROWS, 1),
                          (NSA_GROUPS, NSA_HPG * QF_ROWS, NSA_TQ))
    return jnp.asarray(tab)


def _pos_features(pos, width):
    pos = pos[:, None]
    lane = jnp.arange(width, dtype=jnp.int32)[None, :] % LANES
    hi = ((pos // SLC_BLOCK) * SLC_BLOCK).astype(F32)
    lo = (pos % SLC_BLOCK).astype(F32)
    k = lane - FEAT0
    f = jnp.where((k >= 0) & (k < 6), jnp.where(k % 2 == 0, hi, lo), 0.0)
    f = jnp.where((k >= 6) & (k < 9), 1.0, f)
    return f.astype(BF16)


def _nsa_queries(qt_ref, tab_ref, t0):
    tq = qt_ref.shape[2]
    r = lax.broadcasted_iota(jnp.int32, (QF_ROWS, tq), 0)
    t = (t0 + lax.broadcasted_iota(jnp.int32, (1, tq), 1)).astype(F32)
    zeros = jnp.zeros((LANES - HEAD_DIM - QF_ROWS, tq), BF16)
    out = []
    for i in range(NSA_HPG):
        tile = tab_ref[0, i * QF_ROWS:(i + 1) * QF_ROWS, :]
        a1, a2, a3 = _pieces(-(tile[6:7, :] * t))
        feat = jnp.where(r == 6, a1, jnp.where(r == 7, a2, jnp.where(r == 8, a3,
                                                                     jnp.where(r < 6, tile, 0.0))))
        out.append(jnp.concatenate([qt_ref[0, i * HEAD_DIM:(i + 1) * HEAD_DIM, :],
                                    feat.astype(BF16), zeros], axis=0))
    return out


def _gates_t(gt_ref, bg_ref, branch):
    gl = gt_ref[0] + bg_ref[...]
    return [_sigmoid(gl[N_BRANCH * i + branch:N_BRANCH * i + branch + 1, :]) for i in range(NSA_HPG)]


def _store_heads(o_ref, outs_t, gates):
    g = [o * gt for o, gt in zip(outs_t, gates)]
    o_ref[0, :, :LANES] = jnp.concatenate(g[:2], axis=0).T
    o_ref[0, :, LANES:] = jnp.concatenate(g[2:], axis=0).T


def _nsa_specs(q_row):
    return dict(
        q=pl.BlockSpec((1, NSA_HPG * HEAD_DIM, NSA_TQ), lambda b, g, i: (b, q_row + g, i)),
        tab=pl.BlockSpec((1, NSA_HPG * QF_ROWS, NSA_TQ), lambda b, g, i: (g, 0, 0)),
        gt=pl.BlockSpec((1, GATE_ROWS, NSA_TQ), lambda b, g, i: (b, g, i)),
        bg=pl.BlockSpec((GATE_ROWS, 1), lambda b, g, i: (g, 0)),
        out=pl.BlockSpec((1, NSA_TQ, NSA_HPG * HEAD_DIM), lambda b, g, i: (b, i, g)),
    )


def _cmp_body(nc, t0, qh, kc_ref, vct_ref, mt_ref, grp_ref, gates, o_ref, sel_ref, flag_ref):
    tq = NSA_TQ
    rows = nc * CMP_CHUNK
    full = max(rows - CMP_CHUNK - 8, 0)
    nseg = kc_ref.shape[2]
    kc = kc_ref[0, 0, :rows, :]
    vct = vct_ref[0, 0, :, :rows]
    t = t0 + lax.broadcasted_iota(jnp.int32, (1, tq), 1)
    cidx = full + lax.broadcasted_iota(jnp.int32, (rows - full, 1), 0)
    valid = (cidx * CMP_STRIDE + (CMP_BLOCK - 1) <= t) & (cidx < nseg - 1)
    psum = jnp.zeros((rows, tq), F32)
    ps = []
    s_all = _dot(kc, jnp.concatenate(qh, axis=1))
    for i in range(NSA_HPG):
        s = s_all[:, i * tq:(i + 1) * tq]
        s_last = jnp.where(valid, s[full:], NEG)
        m = jnp.max(s_last, axis=0, keepdims=True)
        if nc > 1:
            m = jnp.maximum(m, jnp.max(s[:full], axis=0, keepdims=True))
        e = jnp.where(valid, jnp.exp2(s_last - m), 0.0)
        if nc > 1:
            e = jnp.concatenate([jnp.exp2(s[:full] - m), e], axis=0)
        l = jnp.sum(e, axis=0, keepdims=True)
        p = e * jnp.where(l > 0.0, 1.0 / l, 0.0)
        psum = psum + p
        ps.append(p.astype(BF16))
    o_all = _dot(vct, jnp.concatenate(ps, axis=1))
    _store_heads(o_ref, [o_all[:, i * tq:(i + 1) * tq] for i in range(NSA_HPG)], gates)
    ns = rows * CMP_STRIDE // SLC_BLOCK
    mt = mt_ref[:ns, :rows]
    imp = sum(_dot(mt, p.astype(BF16)) for p in _pieces(psum))
    blk = lax.broadcasted_iota(jnp.int32, (ns, 1), 0)
    cur = t // SLC_BLOCK
    bvalid = blk * SLC_BLOCK <= t
    forced = (blk == 0) | (blk == cur) | (blk == cur - 1)
    score = jnp.where(forced, -jnp.inf, jnp.where(bvalid, imp, NEG))
    blk_f = blk.astype(F32)
    sel = jnp.where(forced, 1.0, 0.0)
    for _ in range(SLC_TOPK - N_FORCED):
        mx = jnp.max(score, axis=0, keepdims=True)
        first = jnp.min(jnp.where(score == mx, blk_f, float(ns)), axis=0, keepdims=True)
        hit = blk_f == first
        sel = jnp.where(hit, 1.0, sel)
        score = jnp.where(hit, -jnp.inf, score)
    selneg = jnp.where(bvalid & (sel > 0.0), 0.0, -MASK_BIG)
    if ns < NS_PAD:
        selneg = jnp.concatenate([selneg, jnp.full((NS_PAD - ns, tq), -MASK_BIG, F32)], axis=0)
    sel_ref[0, 0] = selneg.astype(sel_ref.dtype)
    picked = jnp.where(selneg == 0.0, 1.0, 0.0).astype(BF16)
    used = _dot_nt(jnp.ones((8, tq), BF16), picked)
    used = jnp.where(used > 0.0, 1.0, 0.0).astype(BF16)
    flag_ref[0] = (_dot(used, grp_ref[...])[0:1] > 0.0).astype(jnp.int32)


def _cmp_kernel(q_ref, kc_ref, vct_ref, mt_ref, grp_ref, tab_ref, gt_ref, bg_ref,
                o_ref, sel_ref, flag_ref):
    t0 = pl.program_id(2) * NSA_TQ
    qh = _nsa_queries(q_ref, tab_ref, t0)
    gates = _gates_t(gt_ref, bg_ref, 0)
    nchunk = kc_ref.shape[2] // CMP_CHUNK
    last = t0 // (CMP_CHUNK * CMP_STRIDE)
    for nc in range(1, nchunk + 1):
        pl.when(last == nc - 1)(functools.partial(
            _cmp_body, nc, t0, qh, kc_ref, vct_ref, mt_ref, grp_ref, gates, o_ref, sel_ref, flag_ref))


def _cmp_to_slc_t(nseg, ns):
    c0 = np.arange(nseg)[:, None] * CMP_STRIDE
    s0 = np.arange(ns)[None, :] * SLC_BLOCK
    overlap = np.clip(np.minimum(c0 + CMP_BLOCK, s0 + SLC_BLOCK) - np.maximum(c0, s0), 0, None)
    m = overlap / CMP_STRIDE
    m[nseg - 1] = 0.0
    mt = np.zeros((NS_PAD, nseg))
    mt[:ns] = m.T
    return jnp.asarray(mt, BF16)


def _tile_groups():
    per = SLC_TK // SLC_BLOCK
    g = (np.arange(NS_PAD)[:, None] // per) == np.arange(NS_PAD)[None, :]
    return jnp.asarray(g, BF16)


def _cmp_attention(ut, kcmp, vcmp_t, gt, bg, *, q_row):
    B, _, S = ut.shape
    nseg = kcmp.shape[2]
    nq = S // NSA_TQ
    sp = _nsa_specs(q_row)
    return pl.pallas_call(
        _cmp_kernel,
        grid=(B, NSA_GROUPS, S // NSA_TQ),
        in_specs=[
            sp["q"],
            pl.BlockSpec((1, 1, nseg, LANES), lambda b, g, i: (b, g, 0, 0)),
            pl.BlockSpec((1, 1, HEAD_DIM, nseg), lambda b, g, i: (b, g, 0, 0)),
            pl.BlockSpec((NS_PAD, nseg), lambda b, g, i: (0, 0)),
            pl.BlockSpec((NS_PAD, NS_PAD), lambda b, g, i: (0, 0)),
            sp["tab"], sp["gt"], sp["bg"],
        ],
        out_specs=[sp["out"], pl.BlockSpec((1, 1, NS_PAD, NSA_TQ), lambda b, g, i: (b, g, 0, i)),
                   pl.BlockSpec((1, 1, NS_PAD), lambda b, g, i: ((b * NSA_GROUPS + g) * nq + i, 0, 0))],
        out_shape=[jax.ShapeDtypeStruct((B, S, NSA_WIDTH), F32),
                   jax.ShapeDtypeStruct((B, NSA_GROUPS, NS_PAD, S), BF16),
                   jax.ShapeDtypeStruct((B * NSA_GROUPS * nq, 1, NS_PAD), jnp.int32)],
        compiler_params=_cp(("parallel", "parallel", "arbitrary")),
        name="cmp_attention",
    )(ut, kcmp, vcmp_t, _cmp_to_slc_t(nseg, S // SLC_BLOCK), _tile_groups(), _slope_table(), gt, bg)


def _slc_kernel(fl_ref, q_ref, k_ref, vt_ref, sel_ref, e_ref, tab_ref, gt_ref, bg_ref, o_ref,
                bm_sc, s0_sc, s1_sc, mx_sc, acc_sc, tiles_sm):
    qi = pl.program_id(2)
    tq, tk = NSA_TQ, SLC_TK
    t0 = qi * tq
    diag = t0 // tk
    row = (pl.program_id(0) * NSA_GROUPS + pl.program_id(1)) * pl.num_programs(2) + qi
    n_un = jnp.int32(0)
    for j in range(k_ref.shape[1] // tk):
        tiles_sm[n_un] = jnp.int32(j)
        n_un = n_un + ((fl_ref[row, j] > 0) & (j < diag)).astype(jnp.int32)
    tiles_sm[n_un] = diag
    selneg = sel_ref[0, 0]
    bm_sc[...] = jnp.concatenate([jnp.concatenate([qh, selneg], axis=0)
                                  for qh in _nsa_queries(q_ref, tab_ref, t0)], axis=1)
    qpos = t0 + lax.broadcasted_iota(jnp.int32, (tk, tq), 1)

    def k_tile(k0):
        return jnp.concatenate([k_ref[0, pl.ds(k0, tk), :], e_ref[pl.ds(k0, tk), :]], axis=1)

    def v_rows(h, k0):
        return vt_ref[0, :, pl.ds(k0, tk)]

    outs = _flash_t(bm_sc, (s0_sc, s1_sc), mx_sc, acc_sc, NSA_HPG, tq, tk, k_tile, v_rows,
                    n_un, lambda i: tiles_sm[i], diag, lambda key: key <= qpos)
    _store_heads(o_ref, outs, _gates_t(gt_ref, bg_ref, 1))


def _win_kernel(q_ref, k_ref, vt_ref, bias_ref, tab_ref, gt_ref, bg_ref, o_ref):
    qi = pl.program_id(2)
    tq = NSA_TQ
    nk = WINDOW + tq
    t0 = qi * tq
    start = pl.multiple_of(jnp.maximum(t0 - WINDOW, 0), LANES)
    k = k_ref[0, pl.ds(start, nk), :]
    lhs = jnp.concatenate([vt_ref[0, :, pl.ds(start, nk)],
                           jnp.ones((ACC_ROWS - HEAD_DIM, nk), BF16)], axis=0)
    s_all = _dot(k, jnp.concatenate(_nsa_queries(q_ref, tab_ref, t0), axis=1))
    outs = []
    for h in range(NSA_HPG):
        s = s_all[:, h * tq:(h + 1) * tq] + bias_ref[0]
        p = jnp.exp2(s - jnp.max(s, axis=0, keepdims=True)).astype(BF16)
        a = _dot(lhs, p)
        outs.append(a[:HEAD_DIM] * (1.0 / a[HEAD_DIM:HEAD_DIM + 1]))
    _store_heads(o_ref, outs, _gates_t(gt_ref, bg_ref, 2))


def _win_bias():
    tq, nk = NSA_TQ, WINDOW + NSA_TQ
    out = []
    for p in range(WINDOW // tq + 1):
        t0 = p * tq
        key = max(t0 - WINDOW, 0) + np.arange(nk)[:, None]
        qpos = t0 + np.arange(tq)[None, :]
        out.append(np.where((key <= qpos) & (key > qpos - WINDOW), 0.0, NEG))
    return jnp.asarray(np.stack(out), F32)


def _block_onehot(S):
    e = (np.arange(S)[:, None] // SLC_BLOCK) == np.arange(NS_PAD)[None, :]
    return jnp.asarray(e, BF16)


def _slc_attention(u3, ut, selneg, flags, gt, bg, *, q_row, k_col, v_row):
    B, S, _ = u3.shape
    sp = {k: pl.BlockSpec(v.block_shape, lambda b, g, i, fl, f=v.index_map: f(b, g, i))
          for k, v in _nsa_specs(q_row).items()}
    grid_spec = pltpu.PrefetchScalarGridSpec(
        num_scalar_prefetch=1,
        grid=(B, NSA_GROUPS, S // NSA_TQ),
        in_specs=[
            sp["q"],
            pl.BlockSpec((1, S, LANES), lambda b, g, i, fl: (b, 0, k_col + g)),
            pl.BlockSpec((1, HEAD_DIM, S), lambda b, g, i, fl: (b, v_row + g, 0)),
            pl.BlockSpec((1, 1, NS_PAD, NSA_TQ), lambda b, g, i, fl: (b, g, 0, i)),
            pl.BlockSpec((S, NS_PAD), lambda b, g, i, fl: (0, 0)),
            sp["tab"], sp["gt"], sp["bg"],
        ],
        out_specs=sp["out"],
        scratch_shapes=_flash_scratch(NSA_HPG, NSA_TQ, SLC_TK, 2 * LANES)
        + [pltpu.SMEM((S // SLC_TK + 1,), jnp.int32)],
    )
    return pl.pallas_call(
        _slc_kernel,
        grid_spec=grid_spec,
        out_shape=jax.ShapeDtypeStruct((B, S, NSA_WIDTH), F32),
        compiler_params=_cp(("parallel", "parallel", "arbitrary")),
        name="slc_attention",
    )(flags, ut, u3, ut, selneg, _block_onehot(S), _slope_table(), gt, bg)


def _win_attention(u3, ut, gt, bg, *, q_row, k_col, v_row):
    B, S, _ = u3.shape
    sp = _nsa_specs(q_row)
    bias = _win_bias()
    npat = bias.shape[0]
    return pl.pallas_call(
        _win_kernel,
        grid=(B, NSA_GROUPS, S // NSA_TQ),
        in_specs=[
            sp["q"],
            pl.BlockSpec((1, S, LANES), lambda b, g, i: (b, 0, k_col + g)),
            pl.BlockSpec((1, HEAD_DIM, S), lambda b, g, i: (b, v_row + g, 0)),
            pl.BlockSpec((1,) + bias.shape[1:], lambda b, g, i: (jnp.minimum(i, npat - 1), 0, 0)),
            sp["tab"], sp["gt"], sp["bg"],
        ],
        out_specs=sp["out"],
        out_shape=jax.ShapeDtypeStruct((B, S, NSA_WIDTH), F32),
        compiler_params=_cp(("parallel", "parallel", "arbitrary")),
        name="win_attention",
    )(ut, u3, ut, bias, _slope_table(), gt, bg)


def _aug_groups(w):
    d = w.shape[0]
    w = w.reshape(d, NSA_GROUPS, HEAD_DIM)
    return jnp.pad(w, ((0, 0), (0, 0), (0, LANES - HEAD_DIM))).reshape(d, NSA_GROUPS * LANES)


def _even_layer(x, norm_g, w_in, b_f, gn_g, w_out):
    B, S, D = x.shape
    qscale = HEAD_DIM ** -0.5 * LOG2E
    q_f, k_f, v_f, w_fl, q_r, k_r, v_r, z = jnp.split(
        w_in, np.cumsum([FOX_WIDTH] * 3 + [FOX_HEADS] + [RET_WIDTH] * 3).tolist(), axis=1)
    w = jnp.concatenate([z, k_f, q_r, k_r * HEAD_DIM ** -0.5, v_r], axis=1).astype(BF16)
    w_t = jnp.concatenate([q_f * qscale, v_f], axis=1).T.astype(BF16)
    x2 = x.reshape(B * S, D)
    u, ut = _proj(x2, norm_g, w, seq=S, w_t=[w_t], t_dtypes=[BF16])
    u3 = u.reshape(B, S, -1)
    kfeat, qfeat = _fgate(x, norm_g, w_fl, b_f, tile=min(512, S))
    o_f = _fox(u3, ut, kfeat, qfeat, q_row=0, k_col=D // LANES, v_row=FOX_WIDTH // LANES)
    rb = (D + FOX_WIDTH) // RET_WIDTH
    o_r = _retention(u3, gn_g, q_col=rb, k_col=rb + 1, v_col=rb + 2)
    out = _out0(o_f.reshape(B * S, -1), o_r.reshape(B * S, -1), u, x2, w_out.astype(BF16))
    return out.reshape(B, S, D)


def _odd_layer(x, norm_g, w_in, b_gate, pe_k, pe_v, wk1, wk2, wv1, wv2, w_out, final_g):
    B, S, D = x.shape
    assert S // SLC_BLOCK <= NS_PAD
    qscale = HEAD_DIM ** -0.5 * LOG2E
    sizes = [NSA_WIDTH] + [NSA_KV_WIDTH] * 6 + [NSA_HEADS * N_BRANCH]
    q, kc, vc, ks, vs, kw, vw, gl, z = jnp.split(w_in, np.cumsum(sizes).tolist(), axis=1)
    w = jnp.concatenate([z, kc, vc, _aug_groups(ks), _aug_groups(kw)], axis=1).astype(BF16)
    per_group = NSA_HPG * N_BRANCH
    glt = jnp.pad(gl.T.reshape(NSA_GROUPS, per_group, D), ((0, 0), (0, GATE_ROWS - per_group), (0, 0)))
    glt = glt.reshape(NSA_GROUPS * GATE_ROWS, D).astype(BF16)
    bg = jnp.pad(b_gate.reshape(NSA_GROUPS, per_group), ((0, 0), (0, GATE_ROWS - per_group)))
    bg = bg.reshape(NSA_GROUPS * GATE_ROWS, 1)
    w_vt = jnp.concatenate([q * qscale, vs, vw], axis=1).T.astype(BF16)
    x2 = x.reshape(B * S, D)
    first_k = (D + 2 * NSA_KV_WIDTH) // PROJ_TN
    u, kc_a, vc_a, ut, gt = _proj(
        x2, norm_g, w, seq=S, split_tile=D // PROJ_TN,
        addend=_pos_features(jnp.arange(S, dtype=jnp.int32), PROJ_TN), add_tiles=(first_k, first_k + 1),
        w_t=[w_vt, glt], t_dtypes=[BF16, F32])
    u3 = u.reshape(B, S, -1)
    nseg = S // CMP_STRIDE
    kcmp = _compress(kc_a.reshape(B, nseg, -1), pe_k, wk1, wk2, transposed=False)
    vcmp_t = _compress(vc_a.reshape(B, nseg, -1), pe_v, wv1, wv2, transposed=True)
    o_c, selneg, flags = _cmp_attention(ut, kcmp, vcmp_t, gt, bg, q_row=0)
    kb = (D + 2 * NSA_KV_WIDTH) // LANES
    vb = NSA_WIDTH // HEAD_DIM
    o_s = _slc_attention(u3, ut, selneg, flags[:, 0, :S // SLC_TK], gt, bg, q_row=0, k_col=kb,
                         v_row=vb)
    o_w = _win_attention(u3, ut, gt, bg, q_row=0, k_col=kb + NSA_GROUPS, v_row=vb + NSA_GROUPS)
    r = lambda a: a.reshape(B * S, -1)
    out = _out1(r(o_c), r(o_s), r(o_w), u, x2, w_out.astype(BF16), final_g)
    return out.reshape(B, S, D)


def kernel(x, even_norm_g, even_w_in, even_b_f, even_gn_g, even_w_out, odd_norm_g, odd_w_in,
           odd_b_gate, odd_pe_k, odd_pe_v, odd_wk1, odd_wk2, odd_wv1, odd_wv2, odd_w_out, final_g):
    x = _even_layer(x, even_norm_g[0], even_w_in[0], even_b_f[0], even_gn_g[0], even_w_out[0])
    return _odd_layer(x, odd_norm_g[0], odd_w_in[0], odd_b_gate[0], odd_pe_k[0], odd_pe_v[0],
                      odd_wk1[0], odd_wk2[0], odd_wv1[0], odd_wv2[0], odd_w_out[0], final_g)
```

```python
import functools
import math

import jax
import jax.numpy as jnp
import numpy as np
from jax import lax
from jax.experimental import pallas as pl
from jax.experimental.pallas import tpu as pltpu

D_MODEL = 1024
HEAD_DIM = 64
LANES = 128
FOX_HEADS = 8
RET_HEADS = 8
FOX_WIDTH = FOX_HEADS * HEAD_DIM
RET_WIDTH = RET_HEADS * HEAD_DIM
RET_CHUNK = 128
NSA_HEADS = 16
NSA_GROUPS = 4
NSA_HPG = NSA_HEADS // NSA_GROUPS
NSA_WIDTH = NSA_HEADS * HEAD_DIM
NSA_KV_WIDTH = NSA_GROUPS * HEAD_DIM
N_BRANCH = 3
GATE_ROWS = 16
CMP_BLOCK = 32
CMP_STRIDE = 16
CMP_HIDDEN = 256
CMP_CHUNK = 128
SLC_BLOCK = 64
SLC_TOPK = 16
N_FORCED = 3
NS_PAD = LANES
WINDOW = 512
RMS_EPS = 1e-6
GN_EPS = 1e-5
NEG = -1e30
FORCE_BONUS = 1e6
MASK_BIG = 2.0 ** 100
LOG2E = math.log2(math.e)
FEAT0 = HEAD_DIM
QF_ROWS = 16
ACC_ROWS = HEAD_DIM + 16

PROJ_TM = 1024
PROJ_TN = 512
FOX_TQ = 512
FOX_TK = 512
NSA_TQ = 256
SLC_TK = 512
OUT_TM = 512
VMEM_LIMIT = 48 * 1024 * 1024

F32 = jnp.float32
BF16 = jnp.bfloat16


def _cp(sem, vmem=VMEM_LIMIT):
    return pltpu.CompilerParams(dimension_semantics=sem, vmem_limit_bytes=vmem)


def _dot(a, b):
    return jnp.dot(a, b, preferred_element_type=F32)


def _dot_nt(a, b):
    return lax.dot_general(a, b, (((1,), (1,)), ((), ())), preferred_element_type=F32)


def _dot_tn(a, b):
    return lax.dot_general(a, b, (((0,), (0,)), ((), ())), preferred_element_type=F32)


def _rms(x, g):
    return x * lax.rsqrt(jnp.mean(x * x, axis=-1, keepdims=True) + RMS_EPS) * g


def _silu(x):
    return x * (1.0 / (1.0 + jnp.exp(-x)))


def _sigmoid(x):
    return 1.0 / (1.0 + jnp.exp(-x))


def _low_half(shape, axis):
    return lax.broadcasted_iota(jnp.int32, shape, axis) < HEAD_DIM


def _pieces(v):
    p1 = v.astype(BF16).astype(F32)
    r = v - p1
    p2 = r.astype(BF16).astype(F32)
    p3 = (r - p2).astype(BF16).astype(F32)
    return p1, p2, p3


def _np_pieces(v):
    v = np.asarray(v, np.float64)
    bf = lambda a: np.asarray(a, np.float32).astype(BF16).astype(np.float64)
    p1 = bf(v)
    p2 = bf(v - p1)
    p3 = bf(v - p1 - p2)
    return p1, p2, p3


def _place(lane, cols):
    out = jnp.zeros(lane.shape, F32)
    for i, c in cols.items():
        out = jnp.where(lane == i, c, out)
    return out


def _fgate_kernel(x_ref, g_ref, wf_ref, b_ref, kf_ref, qf_ref, carry):
    @pl.when(pl.program_id(1) == 0)
    def _():
        carry[...] = jnp.zeros_like(carry)

    h = _rms(x_ref[0], g_ref[...])
    t = h.shape[0]
    h1 = h.astype(BF16)
    h2 = (h - h1.astype(F32)).astype(BF16)
    w = wf_ref[...]
    w1 = w.astype(BF16)
    w2 = (w - w1.astype(F32)).astype(BF16)
    f = _dot(h1, w1) + _dot(h1, w2) + _dot(h2, w1) + b_ref[...]
    ls = jnp.minimum(f, 0.0) - jnp.log(1.0 + jnp.exp(-jnp.abs(f)))
    r = lax.broadcasted_iota(jnp.int32, (t, t), 0)
    c = lax.broadcasted_iota(jnp.int32, (t, t), 1)
    lower = jnp.where(c <= r, 1.0, 0.0).astype(BF16)
    cs = sum(_dot(lower, p.astype(BF16)) for p in _pieces(ls)) + carry[...]
    carry[...] = cs[t - 1:t, :]
    cl = cs * LOG2E
    lane = lax.broadcasted_iota(jnp.int32, (t, LANES), 1)
    one = jnp.ones((t, 1), F32)
    for j in range(FOX_HEADS // 2):
        a1, a2, a3 = _pieces(cl[:, 2 * j:2 * j + 1])
        b1, b2, b3 = _pieces(cl[:, 2 * j + 1:2 * j + 2])
        kf = _place(lane, {0: a1, 1: a2, 2: a3, 3: one, 4: one, 5: one, 6: b1, 7: b2, 8: b3})
        kf_ref[0, j] = kf.astype(BF16)
        qa = _place(lane, {0: -one, 1: -one, 2: -one, 3: a1, 4: a2, 5: a3})
        qb = _place(lane, {3: b1, 4: b2, 5: b3, 6: -one, 7: -one, 8: -one})
        qf_ref[0, j, 0] = qa.T[:QF_ROWS].astype(BF16)
        qf_ref[0, j, 1] = qb.T[:QF_ROWS].astype(BF16)


def _fgate(x, g, wf, b_f, *, tile):
    B, S, D = x.shape
    npair = FOX_HEADS // 2
    return pl.pallas_call(
        _fgate_kernel,
        grid=(B, S // tile),
        in_specs=[
            pl.BlockSpec((1, tile, D), lambda b, s: (b, s, 0)),
            pl.BlockSpec((1, D), lambda b, s: (0, 0)),
            pl.BlockSpec((D, FOX_HEADS), lambda b, s: (0, 0)),
            pl.BlockSpec((1, FOX_HEADS), lambda b, s: (0, 0)),
        ],
        out_specs=[
            pl.BlockSpec((1, npair, tile, LANES), lambda b, s: (b, 0, s, 0)),
            pl.BlockSpec((1, npair, 2, QF_ROWS, tile), lambda b, s: (b, 0, 0, 0, s)),
        ],
        out_shape=[
            jax.ShapeDtypeStruct((B, npair, S, LANES), BF16),
            jax.ShapeDtypeStruct((B, npair, 2, QF_ROWS, S), BF16),
        ],
        scratch_shapes=[pltpu.VMEM((1, FOX_HEADS), F32)],
        compiler_params=_cp(("parallel", "arbitrary")),
        name="fgate",
    )(x, g.reshape(1, D), wf, b_f.reshape(1, FOX_HEADS))


def _proj_kernel(*refs, split_tile, add_tiles, n_t):
    it = iter(refs)
    x_ref, g_ref, w_ref = next(it), next(it), next(it)
    add_ref = next(it) if add_tiles else None
    wt_refs = [next(it) for _ in range(n_t)]
    u_ref = next(it)
    e_refs = [next(it), next(it)] if split_tile is not None else None
    ut_refs = [next(it) for _ in range(n_t)]
    h_sc = next(it)
    j = pl.program_id(1)

    @pl.when(j == 0)
    def _():
        h = _rms(x_ref[...], g_ref[...]).astype(BF16)
        h_sc[...] = h
        for wt_ref, ut_ref in zip(wt_refs, ut_refs):
            ut_ref[0] = _dot_nt(wt_ref[...], h).astype(ut_ref.dtype)

    acc = _dot(h_sc[...], w_ref[...])
    if add_tiles:
        is_add = functools.reduce(jnp.logical_or, [j == t for t in add_tiles])

        @pl.when(is_add)
        def _():
            u_ref[...] = (acc + add_ref[...].astype(F32)).astype(u_ref.dtype)

        @pl.when(jnp.logical_not(is_add))
        def _():
            u_ref[...] = acc.astype(u_ref.dtype)
    else:
        u_ref[...] = acc.astype(u_ref.dtype)
    if split_tile is not None:
        @pl.when(j == split_tile)
        def _():
            half = acc.shape[1] // 2
            e_refs[0][...] = acc[:, :half].astype(e_refs[0].dtype)
            e_refs[1][...] = acc[:, half:].astype(e_refs[1].dtype)


def _proj(x2, g, w, *, seq, split_tile=None, addend=None, add_tiles=(), w_t=(), t_dtypes=(),
          tm=PROJ_TM, tn=PROJ_TN):
    N, D = x2.shape
    W = w.shape[1]
    nbs = seq // tm
    in_specs = [
        pl.BlockSpec((tm, D), lambda i, j: (i, 0)),
        pl.BlockSpec((1, D), lambda i, j: (0, 0)),
        pl.BlockSpec((D, tn), lambda i, j: (0, j)),
    ]
    args = [x2, g.reshape(1, D), w]
    if add_tiles:
        in_specs.append(pl.BlockSpec((tm, tn), lambda i, j: (i % nbs, 0)))
        args.append(addend)
    out_shape = [jax.ShapeDtypeStruct((N, W), BF16)]
    out_specs = [pl.BlockSpec((tm, tn), lambda i, j: (i, j))]
    if split_tile is not None:
        out_shape += [jax.ShapeDtypeStruct((N, tn // 2), BF16)] * 2
        out_specs += [pl.BlockSpec((tm, tn // 2), lambda i, j: (i, 0))] * 2
    for wt, dt in zip(w_t, t_dtypes):
        rows = wt.shape[0]
        in_specs.append(pl.BlockSpec((rows, D), lambda i, j: (0, 0)))
        args.append(wt)
        out_shape.append(jax.ShapeDtypeStruct((N // seq, rows, seq), dt))
        out_specs.append(pl.BlockSpec((1, rows, tm), lambda i, j: (i // nbs, 0, i % nbs)))
    return pl.pallas_call(
        functools.partial(_proj_kernel, split_tile=split_tile, add_tiles=tuple(add_tiles),
                          n_t=len(w_t)),
        grid=(N // tm, W // tn),
        in_specs=in_specs,
        out_specs=out_specs,
        out_shape=out_shape,
        scratch_shapes=[pltpu.VMEM((tm, D), BF16)],
        compiler_params=_cp(("parallel", "arbitrary")),
        name="proj",
    )(*args)


def _flash_t(bm_sc, s_bufs, mx_sc, acc_sc, nh, tq, tk, k_tile, v_rows, n_un, tile_of, diag,
             valid_fn):
    acc_sc[...] = jnp.zeros_like(acc_sc)
    ones = jnp.ones((ACC_ROWS - HEAD_DIM, tk), BF16)
    key_iota = lax.broadcasted_iota(jnp.int32, (tk, tq), 0)

    def qk_head(tile, slot, h):
        cols = slice(h * tq, (h + 1) * tq)
        s = _dot(k_tile(pl.multiple_of(tile * tk, tk)), bm_sc[:, cols])
        s_bufs[slot][:, cols] = s
        mx_sc[slot, :, cols] = jnp.max(s, axis=0, keepdims=True)

    def soft_head(tile, slot, m_old, h, valid):
        cols = slice(h * tq, (h + 1) * tq)
        buf = s_bufs[slot]
        k0 = pl.multiple_of(tile * tk, tk)
        if valid is not None:
            s = jnp.where(valid, buf[:, cols], NEG)
            m_new = jnp.maximum(m_old, jnp.max(s, axis=0, keepdims=True))
            p = jnp.exp2(s - m_new).astype(BF16)
        else:
            m_new = jnp.maximum(m_old, mx_sc[slot, :, cols])
            p = jnp.exp2(buf[:, cols] - m_new).astype(BF16)
        alpha = jnp.exp2(m_old - m_new)
        lhs = jnp.concatenate([v_rows(h, k0), ones], axis=0)
        acc_sc[h] = alpha * acc_sc[h] + _dot(lhs, p)
        return m_new

    def step(cur, slot, ms, nxt=None, masked=False):
        valid = valid_fn(pl.multiple_of(cur * tk, tk) + key_iota) if masked else None
        out = []
        for h in range(nh):
            if nxt is not None:
                qk_head(nxt, 1 - slot, h)
            out.append(soft_head(cur, slot, ms[h], h, valid))
        return tuple(out)

    def pair(i, ms):
        t_a, t_b, t_c = tile_of(2 * i), tile_of(2 * i + 1), tile_of(2 * i + 2)
        return step(t_b, 1, step(t_a, 0, ms, nxt=t_b), nxt=t_c)

    def odd_tail(ms):
        return step(diag, 1, step(tile_of(n_un - 1), 0, ms, nxt=diag), masked=True)

    def even_tail(ms):
        return step(diag, 0, ms, masked=True)

    for h in range(nh):
        qk_head(tile_of(0), 0, h)
    ms = (jnp.full((1, tq), NEG, F32),) * nh
    ms = lax.fori_loop(0, n_un // 2, pair, ms)
    ms = lax.cond(n_un % 2 == 1, odd_tail, even_tail, ms)
    outs = []
    for h in range(nh):
        a = acc_sc[h]
        outs.append(a[:HEAD_DIM] * (1.0 / a[HEAD_DIM:HEAD_DIM + 1]))
    return outs


def _flash_scratch(nh, tq, tk, kdim):
    return [pltpu.VMEM((kdim, nh * tq), BF16), pltpu.VMEM((tk, nh * tq), F32),
            pltpu.VMEM((tk, nh * tq), F32), pltpu.VMEM((2, 1, nh * tq), F32),
            pltpu.VMEM((nh, ACC_ROWS, tq), F32)]


def _fox_kernel(qt_ref, qf_ref, k_ref, kf_ref, vt_ref, o_ref, bm_sc, s0_sc, s1_sc, mx_sc, acc_sc):
    qi = pl.program_id(2)
    tq, tk = FOX_TQ, FOX_TK
    bm_sc[...] = jnp.zeros_like(bm_sc)
    for h in range(2):
        rows = slice(h * HEAD_DIM, (h + 1) * HEAD_DIM)
        bm_sc[rows, h * tq:(h + 1) * tq] = qt_ref[0, rows, :]
        bm_sc[LANES:LANES + QF_ROWS, h * tq:(h + 1) * tq] = qf_ref[0, 0, h]
    qpos = qi * tq + lax.broadcasted_iota(jnp.int32, (tk, tq), 1)

    def k_tile(k0):
        return jnp.concatenate([k_ref[0, pl.ds(k0, tk), :], kf_ref[0, 0, pl.ds(k0, tk), :]], axis=1)

    def v_rows(h, k0):
        return vt_ref[0, h * HEAD_DIM:(h + 1) * HEAD_DIM, pl.ds(k0, tk)]

    diag = (qi * tq) // tk
    outs = _flash_t(bm_sc, (s0_sc, s1_sc), mx_sc, acc_sc, 2, tq, tk, k_tile, v_rows,
                    diag, lambda i: i, diag, lambda key: key <= qpos)
    o_ref[0] = jnp.concatenate(outs, axis=0).T.astype(o_ref.dtype)


def _fox(u3, ut, kfeat, qfeat, *, q_row, k_col, v_row):
    B, S, _ = u3.shape
    npair = FOX_HEADS // 2
    return pl.pallas_call(
        _fox_kernel,
        grid=(B, npair, S // FOX_TQ),
        in_specs=[
            pl.BlockSpec((1, LANES, FOX_TQ), lambda b, j, i: (b, q_row + j, i)),
            pl.BlockSpec((1, 1, 2, QF_ROWS, FOX_TQ), lambda b, j, i: (b, j, 0, 0, i)),
            pl.BlockSpec((1, S, LANES), lambda b, j, i: (b, 0, k_col + j)),
            pl.BlockSpec((1, 1, S, LANES), lambda b, j, i: (b, j, 0, 0)),
            pl.BlockSpec((1, LANES, S), lambda b, j, i: (b, v_row + j, 0)),
        ],
        out_specs=pl.BlockSpec((1, FOX_TQ, LANES), lambda b, j, i: (b, i, j)),
        out_shape=jax.ShapeDtypeStruct((B, S, FOX_WIDTH), BF16),
        scratch_shapes=_flash_scratch(2, FOX_TQ, FOX_TK, 2 * LANES),
        compiler_params=_cp(("parallel", "parallel", "arbitrary")),
        name="fox",
    )(ut, qfeat, u3, kfeat, ut)


def _ret_kernel(q_ref, k_ref, v_ref, inner_ref, cross_ref, kdec_ref, cd_ref, bd_ref, gn_ref,
                o_ref, state_sc):
    @pl.when(pl.program_id(1) == 0)
    def _():
        state_sc[...] = jnp.zeros_like(state_sc)

    low = _low_half((RET_CHUNK, LANES), 1)
    inv = 1.0 / HEAD_DIM
    for j in range(RET_HEADS // 2):
        cols = slice(j * LANES, (j + 1) * LANES)
        q, k, v = q_ref[0, :, cols], k_ref[0, :, cols], v_ref[0, :, cols]
        zero = jnp.zeros_like(q)
        qa, qb = jnp.where(low, q, zero), jnp.where(low, zero, q)
        pa = (_dot_nt(qa, k) * inner_ref[j, 0]).astype(BF16)
        pb = (_dot_nt(qb, k) * inner_ref[j, 1]).astype(BF16)
        o_in = jnp.where(low, _dot(pa, v), _dot(pb, v))
        state = state_sc[j]
        o = o_in + _dot(q, state.astype(BF16)) * cross_ref[j]
        kd = (k.astype(F32) * kdec_ref[j]).astype(BF16)
        state_sc[j] = state * cd_ref[j] + _dot_tn(kd, v) * bd_ref[...]
        sa = jnp.sum(jnp.where(low, o, 0.0), axis=-1, keepdims=True)
        st = jnp.sum(o, axis=-1, keepdims=True)
        mu = jnp.where(low, sa, st - sa) * inv
        d = o - mu
        d2 = d * d
        va = jnp.sum(jnp.where(low, d2, 0.0), axis=-1, keepdims=True)
        vt = jnp.sum(d2, axis=-1, keepdims=True)
        var = jnp.where(low, va, vt - va) * inv
        o_ref[0, :, cols] = (d * lax.rsqrt(var + GN_EPS) * gn_ref[:, cols]).astype(o_ref.dtype)


def _ret_constants():
    lg = np.log(1.0 - 2.0 ** (-5.0 - np.arange(RET_HEADS)))
    i = np.arange(RET_CHUNK)
    diff = i[:, None] - i[None, :]
    inner = np.where(diff[None] >= 0, np.exp(lg[:, None, None] * np.maximum(diff, 0)[None]), 0.0)
    cross = np.exp(lg[:, None] * (i[None, :] + 1))
    kdec = np.exp(lg[:, None] * (RET_CHUNK - 1 - i)[None, :])
    cdec = np.exp(lg * RET_CHUNK)
    npair = RET_HEADS // 2
    inner = inner.reshape(npair, 2, RET_CHUNK, RET_CHUNK)

    def lanes(a):
        a = a.reshape(npair, 2, RET_CHUNK)
        return np.repeat(a.transpose(0, 2, 1), HEAD_DIM, axis=2)

    bd = np.kron(np.eye(2), np.ones((HEAD_DIM, HEAD_DIM)))
    cd = np.repeat(cdec.reshape(npair, 2), HEAD_DIM, axis=1)[:, :, None] * bd[None]
    f = lambda a: jnp.asarray(a, F32)
    return f(inner), f(lanes(cross)), f(lanes(kdec)), f(cd), f(bd)


def _retention(u3, gn_g, *, q_col, k_col, v_col):
    B, S, _ = u3.shape
    C = RET_CHUNK
    npair = RET_HEADS // 2
    inner, cross, kdec, cd, bd = _ret_constants()
    full = lambda shape: pl.BlockSpec(shape, lambda b, i: (0,) * len(shape))
    return pl.pallas_call(
        _ret_kernel,
        grid=(B, S // C),
        in_specs=[
            pl.BlockSpec((1, C, RET_WIDTH), lambda b, i: (b, i, q_col)),
            pl.BlockSpec((1, C, RET_WIDTH), lambda b, i: (b, i, k_col)),
            pl.BlockSpec((1, C, RET_WIDTH), lambda b, i: (b, i, v_col)),
            full((npair, 2, C, C)), full((npair, C, LANES)), full((npair, C, LANES)),
            full((npair, LANES, LANES)), full((LANES, LANES)), full((1, RET_WIDTH)),
        ],
        out_specs=pl.BlockSpec((1, C, RET_WIDTH), lambda b, i: (b, i, 0)),
        out_shape=jax.ShapeDtypeStruct((B, S, RET_WIDTH), BF16),
        scratch_shapes=[pltpu.VMEM((npair, LANES, LANES), F32)],
        compiler_params=_cp(("parallel", "arbitrary")),
        name="retention",
    )(u3, u3, u3, inner, cross, kdec, cd, bd, gn_g.reshape(1, RET_WIDTH))


def _out0_kernel(of_ref, or_ref, z_ref, x_ref, w_ref, o_ref):
    z = _silu(z_ref[...].astype(F32))
    ya = (of_ref[...].astype(F32) * z[:, :FOX_WIDTH]).astype(BF16)
    yb = (or_ref[...].astype(F32) * z[:, FOX_WIDTH:]).astype(BF16)
    o_ref[...] = x_ref[...] + _dot(ya, w_ref[:FOX_WIDTH, :]) + _dot(yb, w_ref[FOX_WIDTH:, :])


def _out0(o_f, o_r, u, x2, w_out, *, tm=OUT_TM):
    N, D = x2.shape
    return pl.pallas_call(
        _out0_kernel,
        grid=(N // tm,),
        in_specs=[
            pl.BlockSpec((tm, FOX_WIDTH), lambda i: (i, 0)),
            pl.BlockSpec((tm, RET_WIDTH), lambda i: (i, 0)),
            pl.BlockSpec((tm, D), lambda i: (i, 0)),
            pl.BlockSpec((tm, D), lambda i: (i, 0)),
            pl.BlockSpec((D, D), lambda i: (0, 0)),
        ],
        out_specs=pl.BlockSpec((tm, D), lambda i: (i, 0)),
        out_shape=jax.ShapeDtypeStruct((N, D), F32),
        compiler_params=_cp(("parallel",)),
        name="out0",
    )(o_f, o_r, u, x2, w_out)


def _out1_kernel(oc_ref, os_ref, z_ref, x_ref, w_ref, g_ref, o_ref):
    z = _silu(z_ref[...].astype(F32))
    y = ((oc_ref[...].astype(F32) + os_ref[...].astype(F32)) * z).astype(BF16)
    o_ref[...] = _rms(x_ref[...] + _dot(y, w_ref[...]), g_ref[...])


def _out1(o_c, o_s, u, x2, w_out, final_g, *, tm=OUT_TM):
    N, D = x2.shape
    row = pl.BlockSpec((tm, D), lambda i: (i, 0))
    return pl.pallas_call(
        _out1_kernel,
        grid=(N // tm,),
        in_specs=[row, row, row, row,
                  pl.BlockSpec((D, D), lambda i: (0, 0)),
                  pl.BlockSpec((1, D), lambda i: (0, 0))],
        out_specs=row,
        out_shape=jax.ShapeDtypeStruct((N, D), F32),
        compiler_params=_cp(("parallel",)),
        name="out1",
    )(o_c, o_s, u, x2, w_out, final_g.reshape(1, D))


def _compress_kernel(x_ref, pea_ref, peb_ref, wa_ref, wb_ref, w2_ref, *rest, transposed):
    x = x_ref[0].astype(F32)
    a = _dot((x + pea_ref[...]).astype(BF16), wa_ref[0])
    b = _dot((x + peb_ref[...]).astype(BF16), wb_ref[0])
    nseg = x.shape[0]
    pre = a + pltpu.roll(b, nseg - 1, 0)
    hid = _silu(pre).astype(BF16)
    if transposed:
        o_ref, = rest
        o_ref[0, 0] = _dot_nt(w2_ref[...], hid).astype(o_ref.dtype)
    else:
        feat_ref, o_ref = rest
        o_ref[0, 0] = (_dot(hid, w2_ref[...]) + feat_ref[...].astype(F32)).astype(o_ref.dtype)


def _compress(a3, pe, w1, w2, *, transposed):
    B, nseg, wid = a3.shape
    half = CMP_STRIDE * HEAD_DIM
    eye = jnp.eye(NSA_GROUPS, dtype=w1.dtype)

    def big(wh):
        w4 = wh.reshape(CMP_STRIDE, 1, HEAD_DIM, CMP_HIDDEN)
        sel = eye[:, None, :, None, None]
        return (sel * w4[None]).reshape(NSA_GROUPS, wid, CMP_HIDDEN).astype(BF16)

    def pe_big(p):
        return jnp.broadcast_to(p[:, None, :], (CMP_STRIDE, NSA_GROUPS, HEAD_DIM)).reshape(1, wid)

    args = [a3, pe_big(pe[:CMP_STRIDE]), pe_big(pe[CMP_STRIDE:]), big(w1[:half]), big(w1[half:])]
    in_specs = [
        pl.BlockSpec((1, nseg, wid), lambda b, g: (b, 0, 0)),
        pl.BlockSpec((1, wid), lambda b, g: (0, 0)),
        pl.BlockSpec((1, wid), lambda b, g: (0, 0)),
        pl.BlockSpec((1, wid, CMP_HIDDEN), lambda b, g: (g, 0, 0)),
        pl.BlockSpec((1, wid, CMP_HIDDEN), lambda b, g: (g, 0, 0)),
    ]
    if transposed:
        w2d = w2.T.astype(BF16)
        oshape, oblock = (B, NSA_GROUPS, HEAD_DIM, nseg), (1, 1, HEAD_DIM, nseg)
        args.append(w2d)
        in_specs.append(pl.BlockSpec(w2d.shape, lambda b, g: (0, 0)))
    else:
        w2d = jnp.pad(w2, ((0, 0), (0, LANES - HEAD_DIM))).astype(BF16)
        oshape, oblock = (B, NSA_GROUPS, nseg, LANES), (1, 1, nseg, LANES)
        feat = _pos_features(jnp.arange(nseg, dtype=jnp.int32) * CMP_STRIDE + (CMP_BLOCK - 1), LANES)
        args += [w2d, feat]
        in_specs += [pl.BlockSpec(w2d.shape, lambda b, g: (0, 0)),
                     pl.BlockSpec((nseg, LANES), lambda b, g: (0, 0))]
    return pl.pallas_call(
        functools.partial(_compress_kernel, transposed=transposed),
        grid=(B, NSA_GROUPS),
        in_specs=in_specs,
        out_specs=pl.BlockSpec(oblock, lambda b, g: (b, g, 0, 0)),
        out_shape=jax.ShapeDtypeStruct(oshape, BF16),
        compiler_params=_cp(("parallel", "parallel")),
        name="compress",
    )(*args)


def _slope_table():
    s = np.asarray(2.0 ** (-8.0 * (np.arange(NSA_HEADS) + 1) / NSA_HEADS), np.float32)
    sl = np.asarray(s.astype(np.float64) * LOG2E, np.float32)
    p1, p2, p3 = _np_pieces(sl)
    tab = np.zeros((NSA_HEADS, QF_ROWS), np.float32)
    for k, p in enumerate((p1, p1, p2, p2, p3, p3)):
        tab[:, k] = p
    tab[:, 6] = sl
    tab = np.broadcast_to(tab.reshape(NSA_GROUPS, NSA_HPG * QF_ROWS, 1),
                          (NSA_GROUPS, NSA_HPG * QF_ROWS, NSA_TQ))
    return jnp.asarray(tab)


def _pos_features(pos, width):
    pos = pos[:, None]
    lane = jnp.arange(width, dtype=jnp.int32)[None, :] % LANES
    hi = ((pos // SLC_BLOCK) * SLC_BLOCK).astype(F32)
    lo = (pos % SLC_BLOCK).astype(F32)
    k = lane - FEAT0
    f = jnp.where((k >= 0) & (k < 6), jnp.where(k % 2 == 0, hi, lo), 0.0)
    f = jnp.where((k >= 6) & (k < 9), 1.0, f)
    return f.astype(BF16)


def _nsa_queries(qt_ref, tab_ref, t0):
    tq = qt_ref.shape[2]
    r = lax.broadcasted_iota(jnp.int32, (QF_ROWS, tq), 0)
    t = (t0 + lax.broadcasted_iota(jnp.int32, (1, tq), 1)).astype(F32)
    zeros = jnp.zeros((LANES - HEAD_DIM - QF_ROWS, tq), BF16)
    out = []
    for i in range(NSA_HPG):
        tile = tab_ref[0, i * QF_ROWS:(i + 1) * QF_ROWS, :]
        a1, a2, a3 = _pieces(-(tile[6:7, :] * t))
        feat = jnp.where(r == 6, a1, jnp.where(r == 7, a2, jnp.where(r == 8, a3,
                                                                     jnp.where(r < 6, tile, 0.0))))
        out.append(jnp.concatenate([qt_ref[0, i * HEAD_DIM:(i + 1) * HEAD_DIM, :],
                                    feat.astype(BF16), zeros], axis=0))
    return out


def _gates_t(gt_ref, bg_ref, branch):
    gl = gt_ref[0] + bg_ref[...]
    return [_sigmoid(gl[N_BRANCH * i + branch:N_BRANCH * i + branch + 1, :]) for i in range(NSA_HPG)]


def _gated(outs_t, gates):
    return [o * gt for o, gt in zip(outs_t, gates)]


def _store_heads(o_ref, g):
    o_ref[0, :, :LANES] = jnp.concatenate(g[:2], axis=0).T.astype(o_ref.dtype)
    o_ref[0, :, LANES:] = jnp.concatenate(g[2:], axis=0).T.astype(o_ref.dtype)


def _nsa_specs(q_row):
    return dict(
        q=pl.BlockSpec((1, NSA_HPG * HEAD_DIM, NSA_TQ), lambda b, g, i: (b, q_row + g, i)),
        tab=pl.BlockSpec((1, NSA_HPG * QF_ROWS, NSA_TQ), lambda b, g, i: (g, 0, 0)),
        gt=pl.BlockSpec((1, GATE_ROWS, NSA_TQ), lambda b, g, i: (b, g, i)),
        bg=pl.BlockSpec((GATE_ROWS, 1), lambda b, g, i: (g, 0)),
        out=pl.BlockSpec((1, NSA_TQ, NSA_HPG * HEAD_DIM), lambda b, g, i: (b, i, g)),
    )


def _cmp_body(nc, t0, qh, kc_ref, vct_ref, mt_ref, grp_ref, gates, o_ref, sel_ref, flag_ref):
    tq = NSA_TQ
    rows = nc * CMP_CHUNK
    full = max(rows - CMP_CHUNK - 8, 0)
    nseg = kc_ref.shape[2]
    kc = kc_ref[0, 0, :rows, :]
    vct = vct_ref[0, 0, :, :rows]
    t = t0 + lax.broadcasted_iota(jnp.int32, (1, tq), 1)
    cidx = full + lax.broadcasted_iota(jnp.int32, (rows - full, 1), 0)
    valid = (cidx * CMP_STRIDE + (CMP_BLOCK - 1) <= t) & (cidx < nseg - 1)
    psum = jnp.zeros((rows, tq), F32)
    ps = []
    s_all = _dot(kc, jnp.concatenate(qh, axis=1))
    for i in range(NSA_HPG):
        s = s_all[:, i * tq:(i + 1) * tq]
        s_last = jnp.where(valid, s[full:], NEG)
        m = jnp.max(s_last, axis=0, keepdims=True)
        if nc > 1:
            m = jnp.maximum(m, jnp.max(s[:full], axis=0, keepdims=True))
        e = jnp.where(valid, jnp.exp2(s_last - m), 0.0)
        if nc > 1:
            e = jnp.concatenate([jnp.exp2(s[:full] - m), e], axis=0)
        l = jnp.sum(e, axis=0, keepdims=True)
        p = e * jnp.where(l > 0.0, 1.0 / l, 0.0)
        psum = psum + p
        ps.append(p.astype(BF16))
    o_all = _dot(vct, jnp.concatenate(ps, axis=1))
    _store_heads(o_ref, _gated([o_all[:, i * tq:(i + 1) * tq] for i in range(NSA_HPG)], gates))
    ns = rows * CMP_STRIDE // SLC_BLOCK
    mt = mt_ref[:ns, :rows]
    imp = sum(_dot(mt, p.astype(BF16)) for p in _pieces(psum))
    blk = lax.broadcasted_iota(jnp.int32, (ns, 1), 0)
    cur = t // SLC_BLOCK
    bvalid = blk * SLC_BLOCK <= t
    forced = (blk == 0) | (blk == cur) | (blk == cur - 1)
    score = jnp.where(forced, -jnp.inf, jnp.where(bvalid, imp, NEG))
    blk_f = blk.astype(F32)
    sel = jnp.where(forced, 1.0, 0.0)
    for _ in range(SLC_TOPK - N_FORCED):
        mx = jnp.max(score, axis=0, keepdims=True)
        first = jnp.min(jnp.where(score == mx, blk_f, float(ns)), axis=0, keepdims=True)
        hit = blk_f == first
        sel = jnp.where(hit, 1.0, sel)
        score = jnp.where(hit, -jnp.inf, score)
    selneg = jnp.where(bvalid & (sel > 0.0), 0.0, -MASK_BIG)
    if ns < NS_PAD:
        selneg = jnp.concatenate([selneg, jnp.full((NS_PAD - ns, tq), -MASK_BIG, F32)], axis=0)
    sel_ref[0, 0] = selneg.astype(sel_ref.dtype)
    picked = jnp.where(selneg == 0.0, 1.0, 0.0).astype(BF16)
    used = _dot_nt(jnp.ones((8, tq), BF16), picked)
    used = jnp.where(used > 0.0, 1.0, 0.0).astype(BF16)
    flag_ref[0] = (_dot(used, grp_ref[...])[0:1] > 0.0).astype(jnp.int32)


def _cmp_kernel(q_ref, kc_ref, vct_ref, mt_ref, grp_ref, tab_ref, gt_ref, bg_ref,
                o_ref, sel_ref, flag_ref):
    t0 = pl.program_id(2) * NSA_TQ
    qh = _nsa_queries(q_ref, tab_ref, t0)
    gates = _gates_t(gt_ref, bg_ref, 0)
    nchunk = kc_ref.shape[2] // CMP_CHUNK
    last = t0 // (CMP_CHUNK * CMP_STRIDE)
    for nc in range(1, nchunk + 1):
        pl.when(last == nc - 1)(functools.partial(
            _cmp_body, nc, t0, qh, kc_ref, vct_ref, mt_ref, grp_ref, gates, o_ref, sel_ref, flag_ref))


def _cmp_to_slc_t(nseg, ns):
    c0 = np.arange(nseg)[:, None] * CMP_STRIDE
    s0 = np.arange(ns)[None, :] * SLC_BLOCK
    overlap = np.clip(np.minimum(c0 + CMP_BLOCK, s0 + SLC_BLOCK) - np.maximum(c0, s0), 0, None)
    m = overlap / CMP_STRIDE
    m[nseg - 1] = 0.0
    mt = np.zeros((NS_PAD, nseg))
    mt[:ns] = m.T
    return jnp.asarray(mt, BF16)


def _tile_groups():
    per = SLC_TK // SLC_BLOCK
    g = (np.arange(NS_PAD)[:, None] // per) == np.arange(NS_PAD)[None, :]
    return jnp.asarray(g, BF16)


def _cmp_attention(ut, kcmp, vcmp_t, gt, bg, *, q_row):
    B, _, S = ut.shape
    nseg = kcmp.shape[2]
    nq = S // NSA_TQ
    sp = _nsa_specs(q_row)
    return pl.pallas_call(
        _cmp_kernel,
        grid=(B, NSA_GROUPS, S // NSA_TQ),
        in_specs=[
            sp["q"],
            pl.BlockSpec((1, 1, nseg, LANES), lambda b, g, i: (b, g, 0, 0)),
            pl.BlockSpec((1, 1, HEAD_DIM, nseg), lambda b, g, i: (b, g, 0, 0)),
            pl.BlockSpec((NS_PAD, nseg), lambda b, g, i: (0, 0)),
            pl.BlockSpec((NS_PAD, NS_PAD), lambda b, g, i: (0, 0)),
            sp["tab"], sp["gt"], sp["bg"],
        ],
        out_specs=[sp["out"], pl.BlockSpec((1, 1, NS_PAD, NSA_TQ), lambda b, g, i: (b, g, 0, i)),
                   pl.BlockSpec((1, 1, NS_PAD), lambda b, g, i: ((b * NSA_GROUPS + g) * nq + i, 0, 0))],
        out_shape=[jax.ShapeDtypeStruct((B, S, NSA_WIDTH), BF16),
                   jax.ShapeDtypeStruct((B, NSA_GROUPS, NS_PAD, S), BF16),
                   jax.ShapeDtypeStruct((B * NSA_GROUPS * nq, 1, NS_PAD), jnp.int32)],
        compiler_params=_cp(("parallel", "parallel", "arbitrary")),
        name="cmp_attention",
    )(ut, kcmp, vcmp_t, _cmp_to_slc_t(nseg, S // SLC_BLOCK), _tile_groups(), _slope_table(), gt, bg)


def _window_branch(qh, t0, k_ref, vt_ref, bias_ref):
    tq = NSA_TQ
    nk = WINDOW + tq
    start = pl.multiple_of(jnp.maximum(t0 - WINDOW, 0), LANES)
    k = k_ref[0, pl.ds(start, nk), :]
    lhs = jnp.concatenate([vt_ref[0, :, pl.ds(start, nk)],
                           jnp.ones((ACC_ROWS - HEAD_DIM, nk), BF16)], axis=0)
    s_all = _dot(k, jnp.concatenate(qh, axis=1))
    outs = []
    for h in range(NSA_HPG):
        s = s_all[:, h * tq:(h + 1) * tq] + bias_ref[0]
        p = jnp.exp2(s - jnp.max(s, axis=0, keepdims=True)).astype(BF16)
        a = _dot(lhs, p)
        outs.append(a[:HEAD_DIM] * (1.0 / a[HEAD_DIM:HEAD_DIM + 1]))
    return outs


def _slc_win_kernel(fl_ref, q_ref, k_ref, vt_ref, sel_ref, e_ref, kw_ref, vwt_ref, bias_ref,
                    tab_ref, gt_ref, bg_ref, o_ref, bm_sc, s0_sc, s1_sc, mx_sc, acc_sc, tiles_sm):
    qi = pl.program_id(2)
    tq, tk = NSA_TQ, SLC_TK
    t0 = qi * tq
    diag = t0 // tk
    row = (pl.program_id(0) * NSA_GROUPS + pl.program_id(1)) * pl.num_programs(2) + qi
    n_un = jnp.int32(0)
    for j in range(k_ref.shape[1] // tk):
        tiles_sm[n_un] = jnp.int32(j)
        n_un = n_un + ((fl_ref[row, j] > 0) & (j < diag)).astype(jnp.int32)
    tiles_sm[n_un] = diag
    qh = _nsa_queries(q_ref, tab_ref, t0)
    o_win = _gated(_window_branch(qh, t0, kw_ref, vwt_ref, bias_ref), _gates_t(gt_ref, bg_ref, 2))
    selneg = sel_ref[0, 0]
    bm_sc[...] = jnp.concatenate([jnp.concatenate([q, selneg], axis=0) for q in qh], axis=1)
    qpos = t0 + lax.broadcasted_iota(jnp.int32, (tk, tq), 1)

    def k_tile(k0):
        return jnp.concatenate([k_ref[0, pl.ds(k0, tk), :], e_ref[pl.ds(k0, tk), :]], axis=1)

    def v_rows(h, k0):
        return vt_ref[0, :, pl.ds(k0, tk)]

    outs = _flash_t(bm_sc, (s0_sc, s1_sc), mx_sc, acc_sc, NSA_HPG, tq, tk, k_tile, v_rows,
                    n_un, lambda i: tiles_sm[i], diag, lambda key: key <= qpos)
    o_slc = _gated(outs, _gates_t(gt_ref, bg_ref, 1))
    _store_heads(o_ref, [a + b for a, b in zip(o_slc, o_win)])


def _win_bias():
    tq, nk = NSA_TQ, WINDOW + NSA_TQ
    out = []
    for p in range(WINDOW // tq + 1):
        t0 = p * tq
        key = max(t0 - WINDOW, 0) + np.arange(nk)[:, None]
        qpos = t0 + np.arange(tq)[None, :]
        out.append(np.where((key <= qpos) & (key > qpos - WINDOW), 0.0, NEG))
    return jnp.asarray(np.stack(out), F32)


def _block_onehot(S):
    e = (np.arange(S)[:, None] // SLC_BLOCK) == np.arange(NS_PAD)[None, :]
    return jnp.asarray(e, BF16)


def _slc_win_attention(u3, ut, selneg, flags, gt, bg, *, q_row, ks_col, vs_row, kw_col, vw_row):
    B, S, _ = u3.shape
    sp = {k: pl.BlockSpec(v.block_shape, lambda b, g, i, fl, f=v.index_map: f(b, g, i))
          for k, v in _nsa_specs(q_row).items()}
    bias = _win_bias()
    npat = bias.shape[0]
    kspec = lambda col: pl.BlockSpec((1, S, LANES), lambda b, g, i, fl: (b, 0, col + g))
    vspec = lambda row: pl.BlockSpec((1, HEAD_DIM, S), lambda b, g, i, fl: (b, row + g, 0))
    grid_spec = pltpu.PrefetchScalarGridSpec(
        num_scalar_prefetch=1,
        grid=(B, NSA_GROUPS, S // NSA_TQ),
        in_specs=[
            sp["q"], kspec(ks_col), vspec(vs_row),
            pl.BlockSpec((1, 1, NS_PAD, NSA_TQ), lambda b, g, i, fl: (b, g, 0, i)),
            pl.BlockSpec((S, NS_PAD), lambda b, g, i, fl: (0, 0)),
            kspec(kw_col), vspec(vw_row),
            pl.BlockSpec((1,) + bias.shape[1:],
                         lambda b, g, i, fl: (jnp.minimum(i, npat - 1), 0, 0)),
            sp["tab"], sp["gt"], sp["bg"],
        ],
        out_specs=sp["out"],
        scratch_shapes=_flash_scratch(NSA_HPG, NSA_TQ, SLC_TK, 2 * LANES)
        + [pltpu.SMEM((S // SLC_TK + 1,), jnp.int32)],
    )
    return pl.pallas_call(
        _slc_win_kernel,
        grid_spec=grid_spec,
        out_shape=jax.ShapeDtypeStruct((B, S, NSA_WIDTH), BF16),
        compiler_params=_cp(("parallel", "parallel", "arbitrary")),
        name="slc_win_attention",
    )(flags, ut, u3, ut, selneg, _block_onehot(S), u3, ut, bias, _slope_table(), gt, bg)


def _aug_groups(w):
    d = w.shape[0]
    w = w.reshape(d, NSA_GROUPS, HEAD_DIM)
    return jnp.pad(w, ((0, 0), (0, 0), (0, LANES - HEAD_DIM))).reshape(d, NSA_GROUPS * LANES)


def _even_layer(x, norm_g, w_in, b_f, gn_g, w_out):
    B, S, D = x.shape
    qscale = HEAD_DIM ** -0.5 * LOG2E
    q_f, k_f, v_f, w_fl, q_r, k_r, v_r, z = jnp.split(
        w_in, np.cumsum([FOX_WIDTH] * 3 + [FOX_HEADS] + [RET_WIDTH] * 3).tolist(), axis=1)
    w = jnp.concatenate([z, k_f, q_r, k_r * HEAD_DIM ** -0.5, v_r], axis=1).astype(BF16)
    w_t = jnp.concatenate([q_f * qscale, v_f], axis=1).T.astype(BF16)
    x2 = x.reshape(B * S, D)
    u, ut = _proj(x2, norm_g, w, seq=S, w_t=[w_t], t_dtypes=[BF16])
    u3 = u.reshape(B, S, -1)
    kfeat, qfeat = _fgate(x, norm_g, w_fl, b_f, tile=min(512, S))
    o_f = _fox(u3, ut, kfeat, qfeat, q_row=0, k_col=D // LANES, v_row=FOX_WIDTH // LANES)
    rb = (D + FOX_WIDTH) // RET_WIDTH
    o_r = _retention(u3, gn_g, q_col=rb, k_col=rb + 1, v_col=rb + 2)
    out = _out0(o_f.reshape(B * S, -1), o_r.reshape(B * S, -1), u, x2, w_out.astype(BF16))
    return out.reshape(B, S, D)


def _odd_layer(x, norm_g, w_in, b_gate, pe_k, pe_v, wk1, wk2, wv1, wv2, w_out, final_g):
    B, S, D = x.shape
    assert S // SLC_BLOCK <= NS_PAD
    qscale = HEAD_DIM ** -0.5 * LOG2E
    sizes = [NSA_WIDTH] + [NSA_KV_WIDTH] * 6 + [NSA_HEADS * N_BRANCH]
    q, kc, vc, ks, vs, kw, vw, gl, z = jnp.split(w_in, np.cumsum(sizes).tolist(), axis=1)
    w = jnp.concatenate([z, kc, vc, _aug_groups(ks), _aug_groups(kw)], axis=1).astype(BF16)
    per_group = NSA_HPG * N_BRANCH
    glt = jnp.pad(gl.T.reshape(NSA_GROUPS, per_group, D), ((0, 0), (0, GATE_ROWS - per_group), (0, 0)))
    glt = glt.reshape(NSA_GROUPS * GATE_ROWS, D).astype(BF16)
    bg = jnp.pad(b_gate.reshape(NSA_GROUPS, per_group), ((0, 0), (0, GATE_ROWS - per_group)))
    bg = bg.reshape(NSA_GROUPS * GATE_ROWS, 1)
    w_vt = jnp.concatenate([q * qscale, vs, vw], axis=1).T.astype(BF16)
    x2 = x.reshape(B * S, D)
    first_k = (D + 2 * NSA_KV_WIDTH) // PROJ_TN
    u, kc_a, vc_a, ut, gt = _proj(
        x2, norm_g, w, seq=S, split_tile=D // PROJ_TN,
        addend=_pos_features(jnp.arange(S, dtype=jnp.int32), PROJ_TN), add_tiles=(first_k, first_k + 1),
        w_t=[w_vt, glt], t_dtypes=[BF16, F32])
    u3 = u.reshape(B, S, -1)
    nseg = S // CMP_STRIDE
    kcmp = _compress(kc_a.reshape(B, nseg, -1), pe_k, wk1, wk2, transposed=False)
    vcmp_t = _compress(vc_a.reshape(B, nseg, -1), pe_v, wv1, wv2, transposed=True)
    o_c, selneg, flags = _cmp_attention(ut, kcmp, vcmp_t, gt, bg, q_row=0)
    kb = (D + 2 * NSA_KV_WIDTH) // LANES
    vb = NSA_WIDTH // HEAD_DIM
    o_s = _slc_win_attention(u3, ut, selneg, flags[:, 0, :S // SLC_TK], gt, bg, q_row=0, ks_col=kb,
                             vs_row=vb, kw_col=kb + NSA_GROUPS, vw_row=vb + NSA_GROUPS)
    r = lambda a: a.reshape(B * S, -1)
    out = _out1(r(o_c), r(o_s), u, x2, w_out.astype(BF16), final_g)
    return out.reshape(B, S, D)


def kernel(x, even_norm_g, even_w_in, even_b_f, even_gn_g, even_w_out, odd_norm_g, odd_w_in,
           odd_b_gate, odd_pe_k, odd_pe_v, odd_wk1, odd_wk2, odd_wv1, odd_wv2, odd_w_out, final_g):
    x = _even_layer(x, even_norm_g[0], even_w_in[0], even_b_f[0], even_gn_g[0], even_w_out[0])
    return _odd_layer(x, odd_norm_g[0], odd_w_in[0], odd_b_gate[0], odd_pe_k[0], odd_pe_v[0],
                      odd_wk1[0], odd_wk2[0], odd_wv1[0], odd_wv2[0], odd_w_out[0], final_g)
```

```python
import functools
import math

import jax
import jax.numpy as jnp
import numpy as np
from jax import lax
from jax.experimental import pallas as pl
from jax.experimental.pallas import tpu as pltpu

D_MODEL = 1024
HEAD_DIM = 64
LANES = 128
FOX_HEADS = 8
RET_HEADS = 8
FOX_WIDTH = FOX_HEADS * HEAD_DIM
RET_WIDTH = RET_HEADS * HEAD_DIM
RET_CHUNK = 128
NSA_HEADS = 16
NSA_GROUPS = 4
NSA_HPG = NSA_HEADS // NSA_GROUPS
NSA_WIDTH = NSA_HEADS * HEAD_DIM
NSA_KV_WIDTH = NSA_GROUPS * HEAD_DIM
N_BRANCH = 3
GATE_ROWS = 16
CMP_BLOCK = 32
CMP_STRIDE = 16
CMP_HIDDEN = 256
CMP_CHUNK = 128
SLC_BLOCK = 64
SLC_TOPK = 16
N_FORCED = 3
NS_PAD = LANES
WINDOW = 512
RMS_EPS = 1e-6
GN_EPS = 1e-5
NEG = -1e30
FORCE_BONUS = 1e6
MASK_BIG = 2.0 ** 100
LOG2E = math.log2(math.e)
FEAT0 = HEAD_DIM
QF_ROWS = 16
ACC_ROWS = HEAD_DIM + 16

PROJ_TM = 1024
PROJ_TN = 512
FOX_TQ = 512
FOX_TK = 512
NSA_TQ = 256
SLC_TK = 512
OUT_TM = 1024
VMEM_LIMIT = 48 * 1024 * 1024

F32 = jnp.float32
BF16 = jnp.bfloat16


def _cp(sem, vmem=VMEM_LIMIT):
    return pltpu.CompilerParams(dimension_semantics=sem, vmem_limit_bytes=vmem)


def _dot(a, b):
    return jnp.dot(a, b, preferred_element_type=F32)


def _dot_nt(a, b):
    return lax.dot_general(a, b, (((1,), (1,)), ((), ())), preferred_element_type=F32)


def _dot_tn(a, b):
    return lax.dot_general(a, b, (((0,), (0,)), ((), ())), preferred_element_type=F32)


def _rms(x, g):
    return x * lax.rsqrt(jnp.mean(x * x, axis=-1, keepdims=True) + RMS_EPS) * g


def _silu(x):
    return x * (1.0 / (1.0 + jnp.exp(-x)))


def _sigmoid(x):
    return 1.0 / (1.0 + jnp.exp(-x))


def _low_half(shape, axis):
    return lax.broadcasted_iota(jnp.int32, shape, axis) < HEAD_DIM


def _pieces(v):
    p1 = v.astype(BF16).astype(F32)
    r = v - p1
    p2 = r.astype(BF16).astype(F32)
    p3 = (r - p2).astype(BF16).astype(F32)
    return p1, p2, p3


def _np_pieces(v):
    v = np.asarray(v, np.float64)
    bf = lambda a: np.asarray(a, np.float32).astype(BF16).astype(np.float64)
    p1 = bf(v)
    p2 = bf(v - p1)
    p3 = bf(v - p1 - p2)
    return p1, p2, p3


def _place(lane, cols):
    out = jnp.zeros(lane.shape, F32)
    for i, c in cols.items():
        out = jnp.where(lane == i, c, out)
    return out


def _fgate_kernel(x_ref, g_ref, wf_ref, b_ref, kf_ref, qf_ref, carry):
    @pl.when(pl.program_id(1) == 0)
    def _():
        carry[...] = jnp.zeros_like(carry)

    h = _rms(x_ref[0], g_ref[...])
    t = h.shape[0]
    h1 = h.astype(BF16)
    h2 = (h - h1.astype(F32)).astype(BF16)
    w = wf_ref[...]
    w1 = w.astype(BF16)
    w2 = (w - w1.astype(F32)).astype(BF16)
    nf = w.shape[1]
    r = _dot(jnp.concatenate([h1, h2], axis=0), jnp.concatenate([w1, w2], axis=1))
    f = r[:t, :nf] + r[:t, nf:] + r[t:, :nf] + b_ref[...]
    ls = jnp.minimum(f, 0.0) - jnp.log(1.0 + jnp.exp(-jnp.abs(f)))
    r = lax.broadcasted_iota(jnp.int32, (t, t), 0)
    c = lax.broadcasted_iota(jnp.int32, (t, t), 1)
    lower = jnp.where(c <= r, 1.0, 0.0).astype(BF16)
    r = _dot(lower, jnp.concatenate([p.astype(BF16) for p in _pieces(ls)], axis=1))
    cs = r[:, :nf] + r[:, nf:2 * nf] + r[:, 2 * nf:] + carry[...]
    carry[...] = cs[t - 1:t, :]
    cl = cs * LOG2E
    lane = lax.broadcasted_iota(jnp.int32, (t, LANES), 1)
    one = jnp.ones((t, 1), F32)
    for j in range(FOX_HEADS // 2):
        a1, a2, a3 = _pieces(cl[:, 2 * j:2 * j + 1])
        b1, b2, b3 = _pieces(cl[:, 2 * j + 1:2 * j + 2])
        kf = _place(lane, {0: a1, 1: a2, 2: a3, 3: one, 4: one, 5: one, 6: b1, 7: b2, 8: b3})
        kf_ref[0, j] = kf.astype(BF16)
        qa = _place(lane, {0: -one, 1: -one, 2: -one, 3: a1, 4: a2, 5: a3})
        qb = _place(lane, {3: b1, 4: b2, 5: b3, 6: -one, 7: -one, 8: -one})
        qf_ref[0, j, 0] = qa.T[:QF_ROWS].astype(BF16)
        qf_ref[0, j, 1] = qb.T[:QF_ROWS].astype(BF16)


def _fgate(x, g, wf, b_f, *, tile):
    B, S, D = x.shape
    npair = FOX_HEADS // 2
    return pl.pallas_call(
        _fgate_kernel,
        grid=(B, S // tile),
        in_specs=[
            pl.BlockSpec((1, tile, D), lambda b, s: (b, s, 0)),
            pl.BlockSpec((1, D), lambda b, s: (0, 0)),
            pl.BlockSpec((D, FOX_HEADS), lambda b, s: (0, 0)),
            pl.BlockSpec((1, FOX_HEADS), lambda b, s: (0, 0)),
        ],
        out_specs=[
            pl.BlockSpec((1, npair, tile, LANES), lambda b, s: (b, 0, s, 0)),
            pl.BlockSpec((1, npair, 2, QF_ROWS, tile), lambda b, s: (b, 0, 0, 0, s)),
        ],
        out_shape=[
            jax.ShapeDtypeStruct((B, npair, S, LANES), BF16),
            jax.ShapeDtypeStruct((B, npair, 2, QF_ROWS, S), BF16),
        ],
        scratch_shapes=[pltpu.VMEM((1, FOX_HEADS), F32)],
        compiler_params=_cp(("parallel", "arbitrary")),
        name="fgate",
    )(x, g.reshape(1, D), wf, b_f.reshape(1, FOX_HEADS))


def _proj_kernel(*refs, split_tile, add_tiles, n_t):
    it = iter(refs)
    x_ref, g_ref, w_ref = next(it), next(it), next(it)
    add_ref = next(it) if add_tiles else None
    wt_refs = [next(it) for _ in range(n_t)]
    u_ref = next(it)
    e_ref = next(it) if split_tile is not None else None
    ut_refs = [next(it) for _ in range(n_t)]
    h_sc = next(it)
    j = pl.program_id(1)

    @pl.when(j == 0)
    def _():
        h = _rms(x_ref[...], g_ref[...]).astype(BF16)
        h_sc[...] = h
        for wt_ref, ut_ref in zip(wt_refs, ut_refs):
            ut_ref[0] = _dot_nt(wt_ref[...], h).astype(ut_ref.dtype)

    acc = _dot(h_sc[...], w_ref[...])
    if add_tiles:
        is_add = functools.reduce(jnp.logical_or, [j == t for t in add_tiles])

        @pl.when(is_add)
        def _():
            u_ref[...] = (acc + add_ref[...].astype(F32)).astype(u_ref.dtype)

        @pl.when(jnp.logical_not(is_add))
        def _():
            u_ref[...] = acc.astype(u_ref.dtype)
    else:
        u_ref[...] = acc.astype(u_ref.dtype)
    if split_tile is not None:
        @pl.when(j == split_tile)
        def _():
            for a in range(e_ref.shape[0]):
                e_ref[a] = acc[:, a * HEAD_DIM:(a + 1) * HEAD_DIM].astype(e_ref.dtype)


def _proj(x2, g, w, *, seq, split_tile=None, addend=None, add_tiles=(), w_t=(), t_dtypes=(),
          tm=PROJ_TM, tn=PROJ_TN):
    N, D = x2.shape
    W = w.shape[1]
    nbs = seq // tm
    in_specs = [
        pl.BlockSpec((tm, D), lambda i, j: (i, 0)),
        pl.BlockSpec((1, D), lambda i, j: (0, 0)),
        pl.BlockSpec((D, tn), lambda i, j: (0, j)),
    ]
    args = [x2, g.reshape(1, D), w]
    if add_tiles:
        in_specs.append(pl.BlockSpec((tm, tn), lambda i, j: (i % nbs, 0)))
        args.append(addend)
    out_shape = [jax.ShapeDtypeStruct((N, W), BF16)]
    out_specs = [pl.BlockSpec((tm, tn), lambda i, j: (i, j))]
    if split_tile is not None:
        out_shape.append(jax.ShapeDtypeStruct((tn // HEAD_DIM, N, HEAD_DIM), BF16))
        out_specs.append(pl.BlockSpec((tn // HEAD_DIM, tm, HEAD_DIM), lambda i, j: (0, i, 0)))
    for wt, dt in zip(w_t, t_dtypes):
        rows = wt.shape[0]
        in_specs.append(pl.BlockSpec((rows, D), lambda i, j: (0, 0)))
        args.append(wt)
        out_shape.append(jax.ShapeDtypeStruct((N // seq, rows, seq), dt))
        out_specs.append(pl.BlockSpec((1, rows, tm), lambda i, j: (i // nbs, 0, i % nbs)))
    return pl.pallas_call(
        functools.partial(_proj_kernel, split_tile=split_tile, add_tiles=tuple(add_tiles),
                          n_t=len(w_t)),
        grid=(N // tm, W // tn),
        in_specs=in_specs,
        out_specs=out_specs,
        out_shape=out_shape,
        scratch_shapes=[pltpu.VMEM((tm, D), BF16)],
        compiler_params=_cp(("parallel", "arbitrary")),
        name="proj",
    )(*args)


def _flash_t(bm_sc, s_bufs, mx_sc, acc_sc, nh, tq, tk, k_tile, v_rows, n_un, tile_of, diag,
             valid_fn):
    acc_sc[...] = jnp.zeros_like(acc_sc)
    ones = jnp.ones((ACC_ROWS - HEAD_DIM, tk), BF16)
    key_iota = lax.broadcasted_iota(jnp.int32, (tk, tq), 0)

    def qk_head(tile, slot, h):
        cols = slice(h * tq, (h + 1) * tq)
        s = _dot(k_tile(pl.multiple_of(tile * tk, tk)), bm_sc[:, cols])
        s_bufs[slot][:, cols] = s
        mx_sc[slot, :, cols] = jnp.max(s, axis=0, keepdims=True)

    def soft_head(tile, slot, m_old, h, valid):
        cols = slice(h * tq, (h + 1) * tq)
        buf = s_bufs[slot]
        k0 = pl.multiple_of(tile * tk, tk)
        if valid is not None:
            s = jnp.where(valid, buf[:, cols], NEG)
            m_new = jnp.maximum(m_old, jnp.max(s, axis=0, keepdims=True))
            p = jnp.exp2(s - m_new).astype(BF16)
        else:
            m_new = jnp.maximum(m_old, mx_sc[slot, :, cols])
            p = jnp.exp2(buf[:, cols] - m_new).astype(BF16)
        alpha = jnp.exp2(m_old - m_new)
        lhs = jnp.concatenate([v_rows(h, k0), ones], axis=0)
        acc_sc[h] = alpha * acc_sc[h] + _dot(lhs, p)
        return m_new

    def step(cur, slot, ms, nxt=None, masked=False):
        valid = valid_fn(pl.multiple_of(cur * tk, tk) + key_iota) if masked else None
        out = []
        for h in range(nh):
            if nxt is not None:
                qk_head(nxt, 1 - slot, h)
            out.append(soft_head(cur, slot, ms[h], h, valid))
        return tuple(out)

    def pair(i, ms):
        t_a, t_b, t_c = tile_of(2 * i), tile_of(2 * i + 1), tile_of(2 * i + 2)
        return step(t_b, 1, step(t_a, 0, ms, nxt=t_b), nxt=t_c)

    def odd_tail(ms):
        return step(diag, 1, step(tile_of(n_un - 1), 0, ms, nxt=diag), masked=True)

    def even_tail(ms):
        return step(diag, 0, ms, masked=True)

    for h in range(nh):
        qk_head(tile_of(0), 0, h)
    ms = (jnp.full((1, tq), NEG, F32),) * nh
    ms = lax.fori_loop(0, n_un // 2, pair, ms)
    ms = lax.cond(n_un % 2 == 1, odd_tail, even_tail, ms)
    outs = []
    for h in range(nh):
        a = acc_sc[h]
        outs.append(a[:HEAD_DIM] * (1.0 / a[HEAD_DIM:HEAD_DIM + 1]))
    return outs


def _flash_scratch(nh, tq, tk, kdim):
    return [pltpu.VMEM((kdim, nh * tq), BF16), pltpu.VMEM((tk, nh * tq), F32),
            pltpu.VMEM((tk, nh * tq), F32), pltpu.VMEM((2, 1, nh * tq), F32),
            pltpu.VMEM((nh, ACC_ROWS, tq), F32)]


def _fox_kernel(qt_ref, qf_ref, k_ref, kf_ref, vt_ref, o_ref, bm_sc, s0_sc, s1_sc, mx_sc, acc_sc):
    qi = pl.program_id(2)
    tq, tk = FOX_TQ, FOX_TK
    bm_sc[...] = jnp.zeros_like(bm_sc)
    for h in range(2):
        rows = slice(h * HEAD_DIM, (h + 1) * HEAD_DIM)
        bm_sc[rows, h * tq:(h + 1) * tq] = qt_ref[0, rows, :]
        bm_sc[LANES:LANES + QF_ROWS, h * tq:(h + 1) * tq] = qf_ref[0, 0, h]
    qpos = qi * tq + lax.broadcasted_iota(jnp.int32, (tk, tq), 1)

    def k_tile(k0):
        return jnp.concatenate([k_ref[0, pl.ds(k0, tk), :], kf_ref[0, 0, pl.ds(k0, tk), :]], axis=1)

    def v_rows(h, k0):
        return vt_ref[0, h * HEAD_DIM:(h + 1) * HEAD_DIM, pl.ds(k0, tk)]

    diag = (qi * tq) // tk
    outs = _flash_t(bm_sc, (s0_sc, s1_sc), mx_sc, acc_sc, 2, tq, tk, k_tile, v_rows,
                    diag, lambda i: i, diag, lambda key: key <= qpos)
    o_ref[0] = jnp.concatenate(outs, axis=0).T.astype(o_ref.dtype)


def _fox(u3, ut, kfeat, qfeat, *, q_row, k_col, v_row):
    B, S, _ = u3.shape
    npair = FOX_HEADS // 2
    return pl.pallas_call(
        _fox_kernel,
        grid=(B, npair, S // FOX_TQ),
        in_specs=[
            pl.BlockSpec((1, LANES, FOX_TQ), lambda b, j, i: (b, q_row + j, i)),
            pl.BlockSpec((1, 1, 2, QF_ROWS, FOX_TQ), lambda b, j, i: (b, j, 0, 0, i)),
            pl.BlockSpec((1, S, LANES), lambda b, j, i: (b, 0, k_col + j)),
            pl.BlockSpec((1, 1, S, LANES), lambda b, j, i: (b, j, 0, 0)),
            pl.BlockSpec((1, LANES, S), lambda b, j, i: (b, v_row + j, 0)),
        ],
        out_specs=pl.BlockSpec((1, FOX_TQ, LANES), lambda b, j, i: (b, i, j)),
        out_shape=jax.ShapeDtypeStruct((B, S, FOX_WIDTH), BF16),
        scratch_shapes=_flash_scratch(2, FOX_TQ, FOX_TK, 2 * LANES),
        compiler_params=_cp(("parallel", "parallel", "arbitrary")),
        name="fox",
    )(ut, qfeat, u3, kfeat, ut)


def _ret_kernel(q_ref, k_ref, v_ref, inner_ref, cross_ref, kdec_ref, cd_ref, bd_ref, gn_ref,
                o_ref, state_sc):
    @pl.when(pl.program_id(1) == 0)
    def _():
        state_sc[...] = jnp.zeros_like(state_sc)

    low = _low_half((RET_CHUNK, LANES), 1)
    inv = 1.0 / HEAD_DIM
    for j in range(RET_HEADS // 2):
        cols = slice(j * LANES, (j + 1) * LANES)
        q, k, v = q_ref[0, :, cols], k_ref[0, :, cols], v_ref[0, :, cols]
        zero = jnp.zeros_like(q)
        qa, qb = jnp.where(low, q, zero), jnp.where(low, zero, q)
        pa = (_dot_nt(qa, k) * inner_ref[j, 0]).astype(BF16)
        pb = (_dot_nt(qb, k) * inner_ref[j, 1]).astype(BF16)
        o_in = jnp.where(low, _dot(pa, v), _dot(pb, v))
        state = state_sc[j]
        o = o_in + _dot(q, state.astype(BF16)) * cross_ref[j]
        kd = (k.astype(F32) * kdec_ref[j]).astype(BF16)
        state_sc[j] = state * cd_ref[j] + _dot_tn(kd, v) * bd_ref[...]
        sa = jnp.sum(jnp.where(low, o, 0.0), axis=-1, keepdims=True)
        st = jnp.sum(o, axis=-1, keepdims=True)
        mu = jnp.where(low, sa, st - sa) * inv
        d = o - mu
        d2 = d * d
        va = jnp.sum(jnp.where(low, d2, 0.0), axis=-1, keepdims=True)
        vt = jnp.sum(d2, axis=-1, keepdims=True)
        var = jnp.where(low, va, vt - va) * inv
        o_ref[0, :, cols] = (d * lax.rsqrt(var + GN_EPS) * gn_ref[:, cols]).astype(o_ref.dtype)


def _ret_constants():
    lg = np.log(1.0 - 2.0 ** (-5.0 - np.arange(RET_HEADS)))
    i = np.arange(RET_CHUNK)
    diff = i[:, None] - i[None, :]
    inner = np.where(diff[None] >= 0, np.exp(lg[:, None, None] * np.maximum(diff, 0)[None]), 0.0)
    cross = np.exp(lg[:, None] * (i[None, :] + 1))
    kdec = np.exp(lg[:, None] * (RET_CHUNK - 1 - i)[None, :])
    cdec = np.exp(lg * RET_CHUNK)
    npair = RET_HEADS // 2
    inner = inner.reshape(npair, 2, RET_CHUNK, RET_CHUNK)

    def lanes(a):
        a = a.reshape(npair, 2, RET_CHUNK)
        return np.repeat(a.transpose(0, 2, 1), HEAD_DIM, axis=2)

    bd = np.kron(np.eye(2), np.ones((HEAD_DIM, HEAD_DIM)))
    cd = np.repeat(cdec.reshape(npair, 2), HEAD_DIM, axis=1)[:, :, None] * bd[None]
    f = lambda a: jnp.asarray(a, F32)
    return f(inner), f(lanes(cross)), f(lanes(kdec)), f(cd), f(bd)


def _retention(u3, gn_g, *, q_col, k_col, v_col):
    B, S, _ = u3.shape
    C = RET_CHUNK
    npair = RET_HEADS // 2
    inner, cross, kdec, cd, bd = _ret_constants()
    full = lambda shape: pl.BlockSpec(shape, lambda b, i: (0,) * len(shape))
    return pl.pallas_call(
        _ret_kernel,
        grid=(B, S // C),
        in_specs=[
            pl.BlockSpec((1, C, RET_WIDTH), lambda b, i: (b, i, q_col)),
            pl.BlockSpec((1, C, RET_WIDTH), lambda b, i: (b, i, k_col)),
            pl.BlockSpec((1, C, RET_WIDTH), lambda b, i: (b, i, v_col)),
            full((npair, 2, C, C)), full((npair, C, LANES)), full((npair, C, LANES)),
            full((npair, LANES, LANES)), full((LANES, LANES)), full((1, RET_WIDTH)),
        ],
        out_specs=pl.BlockSpec((1, C, RET_WIDTH), lambda b, i: (b, i, 0)),
        out_shape=jax.ShapeDtypeStruct((B, S, RET_WIDTH), BF16),
        scratch_shapes=[pltpu.VMEM((npair, LANES, LANES), F32)],
        compiler_params=_cp(("parallel", "arbitrary")),
        name="retention",
    )(u3, u3, u3, inner, cross, kdec, cd, bd, gn_g.reshape(1, RET_WIDTH))


def _out0_kernel(of_ref, or_ref, z_ref, x_ref, w_ref, o_ref):
    z = _silu(z_ref[...].astype(F32))
    ya = (of_ref[...].astype(F32) * z[:, :FOX_WIDTH]).astype(BF16)
    yb = (or_ref[...].astype(F32) * z[:, FOX_WIDTH:]).astype(BF16)
    o_ref[...] = x_ref[...] + _dot(ya, w_ref[:FOX_WIDTH, :]) + _dot(yb, w_ref[FOX_WIDTH:, :])


def _out0(o_f, o_r, u, x2, w_out, *, tm=OUT_TM):
    N, D = x2.shape
    return pl.pallas_call(
        _out0_kernel,
        grid=(N // tm,),
        in_specs=[
            pl.BlockSpec((tm, FOX_WIDTH), lambda i: (i, 0)),
            pl.BlockSpec((tm, RET_WIDTH), lambda i: (i, 0)),
            pl.BlockSpec((tm, D), lambda i: (i, 0)),
            pl.BlockSpec((tm, D), lambda i: (i, 0)),
            pl.BlockSpec((D, D), lambda i: (0, 0)),
        ],
        out_specs=pl.BlockSpec((tm, D), lambda i: (i, 0)),
        out_shape=jax.ShapeDtypeStruct((N, D), F32),
        compiler_params=_cp(("parallel",)),
        name="out0",
    )(o_f, o_r, u, x2, w_out)


def _out1_kernel(oc_ref, os_ref, z_ref, x_ref, w_ref, g_ref, o_ref):
    z = _silu(z_ref[...].astype(F32))
    y = ((oc_ref[...].astype(F32) + os_ref[...].astype(F32)) * z).astype(BF16)
    o_ref[...] = _rms(x_ref[...] + _dot(y, w_ref[...]), g_ref[...])


def _out1(o_c, o_s, u, x2, w_out, final_g, *, tm=OUT_TM):
    N, D = x2.shape
    row = pl.BlockSpec((tm, D), lambda i: (i, 0))
    return pl.pallas_call(
        _out1_kernel,
        grid=(N // tm,),
        in_specs=[row, row, row, row,
                  pl.BlockSpec((D, D), lambda i: (0, 0)),
                  pl.BlockSpec((1, D), lambda i: (0, 0))],
        out_specs=row,
        out_shape=jax.ShapeDtypeStruct((N, D), F32),
        compiler_params=_cp(("parallel",)),
        name="out1",
    )(o_c, o_s, u, x2, w_out, final_g.reshape(1, D))


def _compress_kernel(x_ref, pea_ref, peb_ref, wa_ref, wb_ref, w2_ref, *rest, transposed):
    x = x_ref[0, 0].astype(F32)
    a = _dot((x + pea_ref[...]).astype(BF16), wa_ref[...])
    b = _dot((x + peb_ref[...]).astype(BF16), wb_ref[...])
    nseg = x.shape[0]
    pre = a + pltpu.roll(b, nseg - 1, 0)
    hid = _silu(pre).astype(BF16)
    if transposed:
        o_ref, = rest
        o_ref[0, 0] = _dot_nt(w2_ref[...], hid).astype(o_ref.dtype)
    else:
        feat_ref, o_ref = rest
        o_ref[0, 0] = (_dot(hid, w2_ref[...]) + feat_ref[...].astype(F32)).astype(o_ref.dtype)


def _compress(a4, pe, w1, w2, *, transposed):
    G, B, nseg, wid = a4.shape
    half = CMP_STRIDE * HEAD_DIM
    args = [a4, pe[:CMP_STRIDE].reshape(1, wid), pe[CMP_STRIDE:].reshape(1, wid),
            w1[:half].astype(BF16), w1[half:].astype(BF16)]
    in_specs = [
        pl.BlockSpec((1, 1, nseg, wid), lambda b, g: (g, b, 0, 0)),
        pl.BlockSpec((1, wid), lambda b, g: (0, 0)),
        pl.BlockSpec((1, wid), lambda b, g: (0, 0)),
        pl.BlockSpec((wid, CMP_HIDDEN), lambda b, g: (0, 0)),
        pl.BlockSpec((wid, CMP_HIDDEN), lambda b, g: (0, 0)),
    ]
    if transposed:
        w2d = w2.T.astype(BF16)
        oshape, oblock = (B, G, HEAD_DIM, nseg), (1, 1, HEAD_DIM, nseg)
        args.append(w2d)
        in_specs.append(pl.BlockSpec(w2d.shape, lambda b, g: (0, 0)))
    else:
        w2d = jnp.pad(w2, ((0, 0), (0, LANES - HEAD_DIM))).astype(BF16)
        oshape, oblock = (B, G, nseg, LANES), (1, 1, nseg, LANES)
        feat = _pos_features(jnp.arange(nseg, dtype=jnp.int32) * CMP_STRIDE + (CMP_BLOCK - 1), LANES)
        args += [w2d, feat]
        in_specs += [pl.BlockSpec(w2d.shape, lambda b, g: (0, 0)),
                     pl.BlockSpec((nseg, LANES), lambda b, g: (0, 0))]
    return pl.pallas_call(
        functools.partial(_compress_kernel, transposed=transposed),
        grid=(B, G),
        in_specs=in_specs,
        out_specs=pl.BlockSpec(oblock, lambda b, g: (b, g, 0, 0)),
        out_shape=jax.ShapeDtypeStruct(oshape, BF16),
        compiler_params=_cp(("parallel", "parallel")),
        name="compress",
    )(*args)


def _slope_table():
    s = np.asarray(2.0 ** (-8.0 * (np.arange(NSA_HEADS) + 1) / NSA_HEADS), np.float32)
    sl = np.asarray(s.astype(np.float64) * LOG2E, np.float32)
    p1, p2, p3 = _np_pieces(sl)
    tab = np.zeros((NSA_HEADS, QF_ROWS), np.float32)
    for k, p in enumerate((p1, p1, p2, p2, p3, p3)):
        tab[:, k] = p
    tab[:, 6] = sl
    tab = np.broadcast_to(tab.reshape(NSA_GROUPS, NSA_HPG * QF_ROWS, 1),
                          (NSA_GROUPS, NSA_HPG * QF_ROWS, NSA_TQ))
    return jnp.asarray(tab)


def _pos_features(pos, width):
    pos = pos[:, None]
    lane = jnp.arange(width, dtype=jnp.int32)[None, :] % LANES
    hi = ((pos // SLC_BLOCK) * SLC_BLOCK).astype(F32)
    lo = (pos % SLC_BLOCK).astype(F32)
    k = lane - FEAT0
    f = jnp.where((k >= 0) & (k < 6), jnp.where(k % 2 == 0, hi, lo), 0.0)
    f = jnp.where((k >= 6) & (k < 9), 1.0, f)
    return f.astype(BF16)


def _nsa_queries(qt_ref, tab_ref, t0):
    tq = qt_ref.shape[2]
    r = lax.broadcasted_iota(jnp.int32, (QF_ROWS, tq), 0)
    t = (t0 + lax.broadcasted_iota(jnp.int32, (1, tq), 1)).astype(F32)
    zeros = jnp.zeros((LANES - HEAD_DIM - QF_ROWS, tq), BF16)
    out = []
    for i in range(NSA_HPG):
        tile = tab_ref[0, i * QF_ROWS:(i + 1) * QF_ROWS, :]
        a1, a2, a3 = _pieces(-(tile[6:7, :] * t))
        feat = jnp.where(r == 6, a1, jnp.where(r == 7, a2, jnp.where(r == 8, a3,
                                                                     jnp.where(r < 6, tile, 0.0))))
        out.append(jnp.concatenate([qt_ref[0, i * HEAD_DIM:(i + 1) * HEAD_DIM, :],
                                    feat.astype(BF16), zeros], axis=0))
    return out


def _gates_t(gt_ref, bg_ref, branch):
    gl = gt_ref[0] + bg_ref[...]
    return [_sigmoid(gl[N_BRANCH * i + branch:N_BRANCH * i + branch + 1, :]) for i in range(NSA_HPG)]


def _gated(outs_t, gates):
    return [o * gt for o, gt in zip(outs_t, gates)]


def _store_heads(o_ref, g):
    o_ref[0, :, :LANES] = jnp.concatenate(g[:2], axis=0).T.astype(o_ref.dtype)
    o_ref[0, :, LANES:] = jnp.concatenate(g[2:], axis=0).T.astype(o_ref.dtype)


def _nsa_specs(q_row):
    return dict(
        q=pl.BlockSpec((1, NSA_HPG * HEAD_DIM, NSA_TQ), lambda b, g, i: (b, q_row + g, i)),
        tab=pl.BlockSpec((1, NSA_HPG * QF_ROWS, NSA_TQ), lambda b, g, i: (g, 0, 0)),
        gt=pl.BlockSpec((1, GATE_ROWS, NSA_TQ), lambda b, g, i: (b, g, i)),
        bg=pl.BlockSpec((GATE_ROWS, 1), lambda b, g, i: (g, 0)),
        out=pl.BlockSpec((1, NSA_TQ, NSA_HPG * HEAD_DIM), lambda b, g, i: (b, i, g)),
    )


def _cmp_body(nc, t0, qh, kc_ref, vct_ref, mt_ref, grp_ref, gates, o_ref, sel_ref, flag_ref):
    tq = NSA_TQ
    rows = nc * CMP_CHUNK
    full = max(rows - CMP_CHUNK - 8, 0)
    nseg = kc_ref.shape[2]
    kc = kc_ref[0, 0, :rows, :]
    vct = vct_ref[0, 0, :, :rows]
    t = t0 + lax.broadcasted_iota(jnp.int32, (1, tq), 1)
    cidx = full + lax.broadcasted_iota(jnp.int32, (rows - full, 1), 0)
    valid = (cidx * CMP_STRIDE + (CMP_BLOCK - 1) <= t) & (cidx < nseg - 1)
    psum = jnp.zeros((rows, tq), F32)
    ps = []
    s_all = _dot(kc, jnp.concatenate(qh, axis=1))
    for i in range(NSA_HPG):
        s = s_all[:, i * tq:(i + 1) * tq]
        s_last = jnp.where(valid, s[full:], NEG)
        m = jnp.max(s_last, axis=0, keepdims=True)
        if nc > 1:
            m = jnp.maximum(m, jnp.max(s[:full], axis=0, keepdims=True))
        e = jnp.where(valid, jnp.exp2(s_last - m), 0.0)
        if nc > 1:
            e = jnp.concatenate([jnp.exp2(s[:full] - m), e], axis=0)
        l = jnp.sum(e, axis=0, keepdims=True)
        p = e * jnp.where(l > 0.0, 1.0 / l, 0.0)
        psum = psum + p
        ps.append(p.astype(BF16))
    o_all = _dot(vct, jnp.concatenate(ps, axis=1))
    _store_heads(o_ref, _gated([o_all[:, i * tq:(i + 1) * tq] for i in range(NSA_HPG)], gates))
    ns = rows * CMP_STRIDE // SLC_BLOCK
    mt = mt_ref[:ns, :rows]
    imp = sum(_dot(mt, p.astype(BF16)) for p in _pieces(psum))
    blk = lax.broadcasted_iota(jnp.int32, (ns, 1), 0)
    cur = t // SLC_BLOCK
    bvalid = blk * SLC_BLOCK <= t
    forced = (blk == 0) | (blk == cur) | (blk == cur - 1)
    score = jnp.where(forced, -jnp.inf, jnp.where(bvalid, imp, NEG))
    blk_f = blk.astype(F32)
    sel = jnp.where(forced, 1.0, 0.0)
    for _ in range(SLC_TOPK - N_FORCED):
        mx = jnp.max(score, axis=0, keepdims=True)
        first = jnp.min(jnp.where(score == mx, blk_f, float(ns)), axis=0, keepdims=True)
        hit = blk_f == first
        sel = jnp.where(hit, 1.0, sel)
        score = jnp.where(hit, -jnp.inf, score)
    selneg = jnp.where(bvalid & (sel > 0.0), 0.0, -MASK_BIG)
    if ns < NS_PAD:
        selneg = jnp.concatenate([selneg, jnp.full((NS_PAD - ns, tq), -MASK_BIG, F32)], axis=0)
    sel_ref[0, 0] = selneg.astype(sel_ref.dtype)
    picked = jnp.where(selneg == 0.0, 1.0, 0.0).astype(BF16)
    used = _dot_nt(jnp.ones((8, tq), BF16), picked)
    used = jnp.where(used > 0.0, 1.0, 0.0).astype(BF16)
    flag_ref[0] = (_dot(used, grp_ref[...])[0:1] > 0.0).astype(jnp.int32)


def _cmp_kernel(q_ref, kc_ref, vct_ref, mt_ref, grp_ref, tab_ref, gt_ref, bg_ref,
                o_ref, sel_ref, flag_ref):
    t0 = pl.program_id(2) * NSA_TQ
    qh = _nsa_queries(q_ref, tab_ref, t0)
    gates = _gates_t(gt_ref, bg_ref, 0)
    nchunk = kc_ref.shape[2] // CMP_CHUNK
    last = t0 // (CMP_CHUNK * CMP_STRIDE)
    for nc in range(1, nchunk + 1):
        pl.when(last == nc - 1)(functools.partial(
            _cmp_body, nc, t0, qh, kc_ref, vct_ref, mt_ref, grp_ref, gates, o_ref, sel_ref, flag_ref))


def _cmp_to_slc_t(nseg, ns):
    c0 = np.arange(nseg)[:, None] * CMP_STRIDE
    s0 = np.arange(ns)[None, :] * SLC_BLOCK
    overlap = np.clip(np.minimum(c0 + CMP_BLOCK, s0 + SLC_BLOCK) - np.maximum(c0, s0), 0, None)
    m = overlap / CMP_STRIDE
    m[nseg - 1] = 0.0
    mt = np.zeros((NS_PAD, nseg))
    mt[:ns] = m.T
    return jnp.asarray(mt, BF16)


def _tile_groups():
    per = SLC_TK // SLC_BLOCK
    g = (np.arange(NS_PAD)[:, None] // per) == np.arange(NS_PAD)[None, :]
    return jnp.asarray(g, BF16)


def _cmp_attention(ut, kcmp, vcmp_t, gt, bg, *, q_row):
    B, _, S = ut.shape
    nseg = kcmp.shape[2]
    nq = S // NSA_TQ
    sp = _nsa_specs(q_row)
    return pl.pallas_call(
        _cmp_kernel,
        grid=(B, NSA_GROUPS, S // NSA_TQ),
        in_specs=[
            sp["q"],
            pl.BlockSpec((1, 1, nseg, LANES), lambda b, g, i: (b, g, 0, 0)),
            pl.BlockSpec((1, 1, HEAD_DIM, nseg), lambda b, g, i: (b, g, 0, 0)),
            pl.BlockSpec((NS_PAD, nseg), lambda b, g, i: (0, 0)),
            pl.BlockSpec((NS_PAD, NS_PAD), lambda b, g, i: (0, 0)),
            sp["tab"], sp["gt"], sp["bg"],
        ],
        out_specs=[sp["out"], pl.BlockSpec((1, 1, NS_PAD, NSA_TQ), lambda b, g, i: (b, g, 0, i)),
                   pl.BlockSpec((1, 1, NS_PAD), lambda b, g, i: ((b * NSA_GROUPS + g) * nq + i, 0, 0))],
        out_shape=[jax.ShapeDtypeStruct((B, S, NSA_WIDTH), BF16),
                   jax.ShapeDtypeStruct((B, NSA_GROUPS, NS_PAD, S), BF16),
                   jax.ShapeDtypeStruct((B * NSA_GROUPS * nq, 1, NS_PAD), jnp.int32)],
        compiler_params=_cp(("parallel", "parallel", "arbitrary")),
        name="cmp_attention",
    )(ut, kcmp, vcmp_t, _cmp_to_slc_t(nseg, S // SLC_BLOCK), _tile_groups(), _slope_table(), gt, bg)


def _window_branch(qh, t0, k_ref, vt_ref, bias_ref):
    tq = NSA_TQ
    nk = WINDOW + tq
    start = pl.multiple_of(jnp.maximum(t0 - WINDOW, 0), LANES)
    k = k_ref[0, pl.ds(start, nk), :]
    lhs = jnp.concatenate([vt_ref[0, :, pl.ds(start, nk)],
                           jnp.ones((ACC_ROWS - HEAD_DIM, nk), BF16)], axis=0)
    s_all = _dot(k, jnp.concatenate(qh, axis=1))
    outs = []
    for h in range(NSA_HPG):
        s = s_all[:, h * tq:(h + 1) * tq] + bias_ref[0]
        p = jnp.exp2(s - jnp.max(s, axis=0, keepdims=True)).astype(BF16)
        a = _dot(lhs, p)
        outs.append(a[:HEAD_DIM] * (1.0 / a[HEAD_DIM:HEAD_DIM + 1]))
    return outs


def _slc_win_kernel(fl_ref, q_ref, k_ref, vt_ref, sel_ref, e_ref, kw_ref, vwt_ref, bias_ref,
                    tab_ref, gt_ref, bg_ref, o_ref, bm_sc, s0_sc, s1_sc, mx_sc, acc_sc, tiles_sm):
    qi = pl.program_id(2)
    tq, tk = NSA_TQ, SLC_TK
    t0 = qi * tq
    diag = t0 // tk
    row = (pl.program_id(0) * NSA_GROUPS + pl.program_id(1)) * pl.num_programs(2) + qi
    n_un = jnp.int32(0)
    for j in range(k_ref.shape[1] // tk):
        tiles_sm[n_un] = jnp.int32(j)
        n_un = n_un + ((fl_ref[row, j] > 0) & (j < diag)).astype(jnp.int32)
    tiles_sm[n_un] = diag
    qh = _nsa_queries(q_ref, tab_ref, t0)
    o_win = _gated(_window_branch(qh, t0, kw_ref, vwt_ref, bias_ref), _gates_t(gt_ref, bg_ref, 2))
    selneg = sel_ref[0, 0]
    bm_sc[...] = jnp.concatenate([jnp.concatenate([q, selneg], axis=0) for q in qh], axis=1)
    qpos = t0 + lax.broadcasted_iota(jnp.int32, (tk, tq), 1)

    def k_tile(k0):
        return jnp.concatenate([k_ref[0, pl.ds(k0, tk), :], e_ref[pl.ds(k0, tk), :]], axis=1)

    def v_rows(h, k0):
        return vt_ref[0, :, pl.ds(k0, tk)]

    outs = _flash_t(bm_sc, (s0_sc, s1_sc), mx_sc, acc_sc, NSA_HPG, tq, tk, k_tile, v_rows,
                    n_un, lambda i: tiles_sm[i], diag, lambda key: key <= qpos)
    o_slc = _gated(outs, _gates_t(gt_ref, bg_ref, 1))
    _store_heads(o_ref, [a + b for a, b in zip(o_slc, o_win)])


def _win_bias():
    tq, nk = NSA_TQ, WINDOW + NSA_TQ
    out = []
    for p in range(WINDOW // tq + 1):
        t0 = p * tq
        key = max(t0 - WINDOW, 0) + np.arange(nk)[:, None]
        qpos = t0 + np.arange(tq)[None, :]
        out.append(np.where((key <= qpos) & (key > qpos - WINDOW), 0.0, NEG))
    return jnp.asarray(np.stack(out), F32)


def _block_onehot(S):
    e = (np.arange(S)[:, None] // SLC_BLOCK) == np.arange(NS_PAD)[None, :]
    return jnp.asarray(e, BF16)


def _slc_win_attention(u3, ut, selneg, flags, gt, bg, *, q_row, ks_col, vs_row, kw_col, vw_row):
    B, S, _ = u3.shape
    sp = {k: pl.BlockSpec(v.block_shape, lambda b, g, i, fl, f=v.index_map: f(b, g, i))
          for k, v in _nsa_specs(q_row).items()}
    bias = _win_bias()
    npat = bias.shape[0]
    kspec = lambda col: pl.BlockSpec((1, S, LANES), lambda b, g, i, fl: (b, 0, col + g))
    vspec = lambda row: pl.BlockSpec((1, HEAD_DIM, S), lambda b, g, i, fl: (b, row + g, 0))
    grid_spec = pltpu.PrefetchScalarGridSpec(
        num_scalar_prefetch=1,
        grid=(B, NSA_GROUPS, S // NSA_TQ),
        in_specs=[
            sp["q"], kspec(ks_col), vspec(vs_row),
            pl.BlockSpec((1, 1, NS_PAD, NSA_TQ), lambda b, g, i, fl: (b, g, 0, i)),
            pl.BlockSpec((S, NS_PAD), lambda b, g, i, fl: (0, 0)),
            kspec(kw_col), vspec(vw_row),
            pl.BlockSpec((1,) + bias.shape[1:],
                         lambda b, g, i, fl: (jnp.minimum(i, npat - 1), 0, 0)),
            sp["tab"], sp["gt"], sp["bg"],
        ],
        out_specs=sp["out"],
        scratch_shapes=_flash_scratch(NSA_HPG, NSA_TQ, SLC_TK, 2 * LANES)
        + [pltpu.SMEM((S // SLC_TK + 1,), jnp.int32)],
    )
    return pl.pallas_call(
        _slc_win_kernel,
        grid_spec=grid_spec,
        out_shape=jax.ShapeDtypeStruct((B, S, NSA_WIDTH), BF16),
        compiler_params=_cp(("parallel", "parallel", "arbitrary")),
        name="slc_win_attention",
    )(flags, ut, u3, ut, selneg, _block_onehot(S), u3, ut, bias, _slope_table(), gt, bg)


def _aug_groups(w):
    d = w.shape[0]
    w = w.reshape(d, NSA_GROUPS, HEAD_DIM)
    return jnp.pad(w, ((0, 0), (0, 0), (0, LANES - HEAD_DIM))).reshape(d, NSA_GROUPS * LANES)


def _even_layer(x, norm_g, w_in, b_f, gn_g, w_out):
    B, S, D = x.shape
    qscale = HEAD_DIM ** -0.5 * LOG2E
    q_f, k_f, v_f, w_fl, q_r, k_r, v_r, z = jnp.split(
        w_in, np.cumsum([FOX_WIDTH] * 3 + [FOX_HEADS] + [RET_WIDTH] * 3).tolist(), axis=1)
    w = jnp.concatenate([z, k_f, q_r, k_r * HEAD_DIM ** -0.5, v_r], axis=1).astype(BF16)
    w_t = jnp.concatenate([q_f * qscale, v_f], axis=1).T.astype(BF16)
    x2 = x.reshape(B * S, D)
    u, ut = _proj(x2, norm_g, w, seq=S, w_t=[w_t], t_dtypes=[BF16])
    u3 = u.reshape(B, S, -1)
    kfeat, qfeat = _fgate(x, norm_g, w_fl, b_f, tile=min(512, S))
    o_f = _fox(u3, ut, kfeat, qfeat, q_row=0, k_col=D // LANES, v_row=FOX_WIDTH // LANES)
    rb = (D + FOX_WIDTH) // RET_WIDTH
    o_r = _retention(u3, gn_g, q_col=rb, k_col=rb + 1, v_col=rb + 2)
    out = _out0(o_f.reshape(B * S, -1), o_r.reshape(B * S, -1), u, x2, w_out.astype(BF16))
    return out.reshape(B, S, D)


def _odd_layer(x, norm_g, w_in, b_gate, pe_k, pe_v, wk1, wk2, wv1, wv2, w_out, final_g):
    B, S, D = x.shape
    assert S // SLC_BLOCK <= NS_PAD
    qscale = HEAD_DIM ** -0.5 * LOG2E
    sizes = [NSA_WIDTH] + [NSA_KV_WIDTH] * 6 + [NSA_HEADS * N_BRANCH]
    q, kc, vc, ks, vs, kw, vw, gl, z = jnp.split(w_in, np.cumsum(sizes).tolist(), axis=1)
    w = jnp.concatenate([z, kc, vc, _aug_groups(ks), _aug_groups(kw)], axis=1).astype(BF16)
    per_group = NSA_HPG * N_BRANCH
    glt = jnp.pad(gl.T.reshape(NSA_GROUPS, per_group, D), ((0, 0), (0, GATE_ROWS - per_group), (0, 0)))
    glt = glt.reshape(NSA_GROUPS * GATE_ROWS, D).astype(BF16)
    bg = jnp.pad(b_gate.reshape(NSA_GROUPS, per_group), ((0, 0), (0, GATE_ROWS - per_group)))
    bg = bg.reshape(NSA_GROUPS * GATE_ROWS, 1)
    w_vt = jnp.concatenate([q * qscale, vs, vw], axis=1).T.astype(BF16)
    x2 = x.reshape(B * S, D)
    first_k = (D + 2 * NSA_KV_WIDTH) // PROJ_TN
    u, kvc, ut, gt = _proj(
        x2, norm_g, w, seq=S, split_tile=D // PROJ_TN,
        addend=_pos_features(jnp.arange(S, dtype=jnp.int32), PROJ_TN), add_tiles=(first_k, first_k + 1),
        w_t=[w_vt, glt], t_dtypes=[BF16, F32])
    u3 = u.reshape(B, S, -1)
    nseg = S // CMP_STRIDE
    kvc = kvc.reshape(2, NSA_GROUPS, B, nseg, CMP_STRIDE * HEAD_DIM)
    kcmp = _compress(kvc[0], pe_k, wk1, wk2, transposed=False)
    vcmp_t = _compress(kvc[1], pe_v, wv1, wv2, transposed=True)
    o_c, selneg, flags = _cmp_attention(ut, kcmp, vcmp_t, gt, bg, q_row=0)
    kb = (D + 2 * NSA_KV_WIDTH) // LANES
    vb = NSA_WIDTH // HEAD_DIM
    o_s = _slc_win_attention(u3, ut, selneg, flags[:, 0, :S // SLC_TK], gt, bg, q_row=0, ks_col=kb,
                             vs_row=vb, kw_col=kb + NSA_GROUPS, vw_row=vb + NSA_GROUPS)
    r = lambda a: a.reshape(B * S, -1)
    out = _out1(r(o_c), r(o_s), u, x2, w_out.astype(BF16), final_g)
    return out.reshape(B, S, D)


def kernel(x, even_norm_g, even_w_in, even_b_f, even_gn_g, even_w_out, odd_norm_g, odd_w_in,
           odd_b_gate, odd_pe_k, odd_pe_v, odd_wk1, odd_wk2, odd_wv1, odd_wv2, odd_w_out, final_g):
    x = _even_layer(x, even_norm_g[0], even_w_in[0], even_b_f[0], even_gn_g[0], even_w_out[0])
    return _odd_layer(x, odd_norm_g[0], odd_w_in[0], odd_b_gate[0], odd_pe_k[0], odd_pe_v[0],
                      odd_wk1[0], odd_wk2[0], odd_wv1[0], odd_wv2[0], odd_w_out[0], final_g)
```

```python
import functools
import math

import jax
import jax.numpy as jnp
import numpy as np
from jax import lax
from jax.experimental import pallas as pl
from jax.experimental.pallas import tpu as pltpu

D_MODEL = 1024
HEAD_DIM = 64
LANES = 128
FOX_HEADS = 8
RET_HEADS = 8
FOX_WIDTH = FOX_HEADS * HEAD_DIM
RET_WIDTH = RET_HEADS * HEAD_DIM
RET_CHUNK = 128
NSA_HEADS = 16
NSA_GROUPS = 4
NSA_HPG = NSA_HEADS // NSA_GROUPS
NSA_WIDTH = NSA_HEADS * HEAD_DIM
NSA_KV_WIDTH = NSA_GROUPS * HEAD_DIM
N_BRANCH = 3
GATE_ROWS = 16
CMP_BLOCK = 32
CMP_STRIDE = 16
CMP_HIDDEN = 256
CMP_CHUNK = 128
SLC_BLOCK = 64
SLC_TOPK = 16
N_FORCED = 3
NS_PAD = LANES
WINDOW = 512
RMS_EPS = 1e-6
GN_EPS = 1e-5
NEG = -1e30
FORCE_BONUS = 1e6
MASK_BIG = 2.0 ** 100
LOG2E = math.log2(math.e)
FEAT0 = HEAD_DIM
QF_ROWS = 16
ACC_ROWS = HEAD_DIM + 16

PROJ_TM = 1024
PROJ_TN = 512
FOX_TQ = 512
FOX_TK = 512
NSA_TQ = 256
SLC_TK = 512
WIN_TK = 256
OUT_TM = 1024
VMEM_LIMIT = 48 * 1024 * 1024

F32 = jnp.float32
BF16 = jnp.bfloat16


def _cp(sem, vmem=VMEM_LIMIT):
    return pltpu.CompilerParams(dimension_semantics=sem, vmem_limit_bytes=vmem)


def _dot(a, b):
    return jnp.dot(a, b, preferred_element_type=F32)


def _dot_nt(a, b):
    return lax.dot_general(a, b, (((1,), (1,)), ((), ())), preferred_element_type=F32)


def _dot_tn(a, b):
    return lax.dot_general(a, b, (((0,), (0,)), ((), ())), preferred_element_type=F32)


def _rms(x, g):
    return x * lax.rsqrt(jnp.mean(x * x, axis=-1, keepdims=True) + RMS_EPS) * g


def _silu(x):
    return x * (1.0 / (1.0 + jnp.exp(-x)))


def _sigmoid(x):
    return 1.0 / (1.0 + jnp.exp(-x))


def _low_half(shape, axis):
    return lax.broadcasted_iota(jnp.int32, shape, axis) < HEAD_DIM


def _pieces(v):
    p1 = v.astype(BF16).astype(F32)
    r = v - p1
    p2 = r.astype(BF16).astype(F32)
    p3 = (r - p2).astype(BF16).astype(F32)
    return p1, p2, p3


def _np_pieces(v):
    v = np.asarray(v, np.float64)
    bf = lambda a: np.asarray(a, np.float32).astype(BF16).astype(np.float64)
    p1 = bf(v)
    p2 = bf(v - p1)
    p3 = bf(v - p1 - p2)
    return p1, p2, p3


def _place(lane, cols):
    out = jnp.zeros(lane.shape, F32)
    for i, c in cols.items():
        out = jnp.where(lane == i, c, out)
    return out


def _fgate_kernel(x_ref, g_ref, wf_ref, b_ref, kf_ref, qf_ref, carry):
    @pl.when(pl.program_id(1) == 0)
    def _():
        carry[...] = jnp.zeros_like(carry)

    h = _rms(x_ref[0], g_ref[...])
    t = h.shape[0]
    h1 = h.astype(BF16)
    h2 = (h - h1.astype(F32)).astype(BF16)
    w = wf_ref[...]
    w1 = w.astype(BF16)
    w2 = (w - w1.astype(F32)).astype(BF16)
    nf = w.shape[1]
    r = _dot(jnp.concatenate([h1, h2], axis=0), jnp.concatenate([w1, w2], axis=1))
    f = r[:t, :nf] + r[:t, nf:] + r[t:, :nf] + b_ref[...]
    ls = jnp.minimum(f, 0.0) - jnp.log(1.0 + jnp.exp(-jnp.abs(f)))
    r = lax.broadcasted_iota(jnp.int32, (t, t), 0)
    c = lax.broadcasted_iota(jnp.int32, (t, t), 1)
    lower = jnp.where(c <= r, 1.0, 0.0).astype(BF16)
    r = _dot(lower, jnp.concatenate([p.astype(BF16) for p in _pieces(ls)], axis=1))
    cs = r[:, :nf] + r[:, nf:2 * nf] + r[:, 2 * nf:] + carry[...]
    carry[...] = cs[t - 1:t, :]
    cl = cs * LOG2E
    lane = lax.broadcasted_iota(jnp.int32, (t, LANES), 1)
    one = jnp.ones((t, 1), F32)
    for j in range(FOX_HEADS // 2):
        a1, a2, a3 = _pieces(cl[:, 2 * j:2 * j + 1])
        b1, b2, b3 = _pieces(cl[:, 2 * j + 1:2 * j + 2])
        kf = _place(lane, {0: a1, 1: a2, 2: a3, 3: one, 4: one, 5: one, 6: b1, 7: b2, 8: b3})
        kf_ref[0, j] = kf.astype(BF16)
        qa = _place(lane, {0: -one, 1: -one, 2: -one, 3: a1, 4: a2, 5: a3})
        qb = _place(lane, {3: b1, 4: b2, 5: b3, 6: -one, 7: -one, 8: -one})
        qf_ref[0, j, 0] = qa.T[:QF_ROWS].astype(BF16)
        qf_ref[0, j, 1] = qb.T[:QF_ROWS].astype(BF16)


def _fgate(x, g, wf, b_f, *, tile):
    B, S, D = x.shape
    npair = FOX_HEADS // 2
    return pl.pallas_call(
        _fgate_kernel,
        grid=(B, S // tile),
        in_specs=[
            pl.BlockSpec((1, tile, D), lambda b, s: (b, s, 0)),
            pl.BlockSpec((1, D), lambda b, s: (0, 0)),
            pl.BlockSpec((D, FOX_HEADS), lambda b, s: (0, 0)),
            pl.BlockSpec((1, FOX_HEADS), lambda b, s: (0, 0)),
        ],
        out_specs=[
            pl.BlockSpec((1, npair, tile, LANES), lambda b, s: (b, 0, s, 0)),
            pl.BlockSpec((1, npair, 2, QF_ROWS, tile), lambda b, s: (b, 0, 0, 0, s)),
        ],
        out_shape=[
            jax.ShapeDtypeStruct((B, npair, S, LANES), BF16),
            jax.ShapeDtypeStruct((B, npair, 2, QF_ROWS, S), BF16),
        ],
        scratch_shapes=[pltpu.VMEM((1, FOX_HEADS), F32)],
        compiler_params=_cp(("parallel", "arbitrary")),
        name="fgate",
    )(x, g.reshape(1, D), wf, b_f.reshape(1, FOX_HEADS))


def _proj_kernel(*refs, split_tile, add_tiles, n_t):
    it = iter(refs)
    x_ref, g_ref, w_ref = next(it), next(it), next(it)
    add_ref = next(it) if add_tiles else None
    wt_refs = [next(it) for _ in range(n_t)]
    u_ref = next(it)
    e_refs = [next(it), next(it)] if split_tile is not None else None
    ut_refs = [next(it) for _ in range(n_t)]
    h_sc = next(it)
    j = pl.program_id(1)

    @pl.when(j == 0)
    def _():
        h = _rms(x_ref[...], g_ref[...]).astype(BF16)
        h_sc[...] = h
        for wt_ref, ut_ref in zip(wt_refs, ut_refs):
            ut_ref[0] = _dot_nt(wt_ref[...], h).astype(ut_ref.dtype)

    acc = _dot(h_sc[...], w_ref[...])
    if add_tiles:
        is_add = functools.reduce(jnp.logical_or, [j == t for t in add_tiles])

        @pl.when(is_add)
        def _():
            u_ref[...] = (acc + add_ref[...].astype(F32)).astype(u_ref.dtype)

        @pl.when(jnp.logical_not(is_add))
        def _():
            u_ref[...] = acc.astype(u_ref.dtype)
    else:
        u_ref[...] = acc.astype(u_ref.dtype)
    if split_tile is not None:
        @pl.when(j == split_tile)
        def _():
            half = acc.shape[1] // 2
            e_refs[0][...] = acc[:, :half].astype(e_refs[0].dtype)
            e_refs[1][...] = acc[:, half:].astype(e_refs[1].dtype)


def _proj(x2, g, w, *, seq, split_tile=None, addend=None, add_tiles=(), w_t=(), t_dtypes=(),
          tm=PROJ_TM, tn=PROJ_TN):
    N, D = x2.shape
    W = w.shape[1]
    nbs = seq // tm
    in_specs = [
        pl.BlockSpec((tm, D), lambda i, j: (i, 0)),
        pl.BlockSpec((1, D), lambda i, j: (0, 0)),
        pl.BlockSpec((D, tn), lambda i, j: (0, j)),
    ]
    args = [x2, g.reshape(1, D), w]
    if add_tiles:
        in_specs.append(pl.BlockSpec((tm, tn), lambda i, j: (i % nbs, 0)))
        args.append(addend)
    out_shape = [jax.ShapeDtypeStruct((N, W), BF16)]
    out_specs = [pl.BlockSpec((tm, tn), lambda i, j: (i, j))]
    if split_tile is not None:
        out_shape += [jax.ShapeDtypeStruct((N, tn // 2), BF16)] * 2
        out_specs += [pl.BlockSpec((tm, tn // 2), lambda i, j: (i, 0))] * 2
    for wt, dt in zip(w_t, t_dtypes):
        rows = wt.shape[0]
        in_specs.append(pl.BlockSpec((rows, D), lambda i, j: (0, 0)))
        args.append(wt)
        out_shape.append(jax.ShapeDtypeStruct((N // seq, rows, seq), dt))
        out_specs.append(pl.BlockSpec((1, rows, tm), lambda i, j: (i // nbs, 0, i % nbs)))
    return pl.pallas_call(
        functools.partial(_proj_kernel, split_tile=split_tile, add_tiles=tuple(add_tiles),
                          n_t=len(w_t)),
        grid=(N // tm, W // tn),
        in_specs=in_specs,
        out_specs=out_specs,
        out_shape=out_shape,
        scratch_shapes=[pltpu.VMEM((tm, D), BF16)],
        compiler_params=_cp(("parallel", "arbitrary")),
        name="proj",
    )(*args)


def _flash_ops(bm_sc, s_bufs, mx_sc, acc_sc, nh, tq, tk, k_tile, v_rows):
    acc_sc[...] = jnp.zeros_like(acc_sc)
    ones = jnp.ones((ACC_ROWS - HEAD_DIM, tk), BF16)

    def qk_head(tile, slot, h):
        cols = slice(h * tq, (h + 1) * tq)
        s = _dot(k_tile(pl.multiple_of(tile * tk, tk)), bm_sc[:, cols])
        s_bufs[slot][:, cols] = s
        if mx_sc is not None:
            mx_sc[slot, :, cols] = jnp.max(s, axis=0, keepdims=True)

    def soft_head(tile, slot, m_old, h, valid, bias):
        cols = slice(h * tq, (h + 1) * tq)
        buf = s_bufs[slot]
        k0 = pl.multiple_of(tile * tk, tk)
        if valid is not None or bias is not None:
            s = buf[:, cols] + bias if valid is None else jnp.where(valid, buf[:, cols], NEG)
            m_new = jnp.maximum(m_old, jnp.max(s, axis=0, keepdims=True))
            p = jnp.exp2(s - m_new).astype(BF16)
        else:
            m_new = jnp.maximum(m_old, mx_sc[slot, :, cols])
            p = jnp.exp2(buf[:, cols] - m_new).astype(BF16)
        alpha = jnp.exp2(m_old - m_new)
        lhs = jnp.concatenate([v_rows(h, k0), ones], axis=0)
        acc_sc[h] = alpha * acc_sc[h] + _dot(lhs, p)
        return m_new

    def qk_all(tile, slot):
        for h in range(nh):
            qk_head(tile, slot, h)

    def step(cur, slot, ms, nxt=None, valid=None, bias=None):
        out = []
        for h in range(nh):
            if nxt is not None:
                qk_head(nxt, 1 - slot, h)
            out.append(soft_head(cur, slot, ms[h], h, valid, bias))
        return tuple(out)

    def finish():
        outs = []
        for h in range(nh):
            a = acc_sc[h]
            outs.append(a[:HEAD_DIM] * (1.0 / a[HEAD_DIM:HEAD_DIM + 1]))
        return outs

    return qk_all, step, finish


def _flash_t(bm_sc, s_bufs, mx_sc, acc_sc, nh, tq, tk, k_tile, v_rows, n_un, tile_of, diag,
             valid_fn):
    qk_all, step, finish = _flash_ops(bm_sc, s_bufs, mx_sc, acc_sc, nh, tq, tk, k_tile, v_rows)
    key_iota = lax.broadcasted_iota(jnp.int32, (tk, tq), 0)
    diag_valid = lambda: valid_fn(pl.multiple_of(diag * tk, tk) + key_iota)

    def pair(i, ms):
        t_a, t_b, t_c = tile_of(2 * i), tile_of(2 * i + 1), tile_of(2 * i + 2)
        return step(t_b, 1, step(t_a, 0, ms, nxt=t_b), nxt=t_c)

    def odd_tail(ms):
        return step(diag, 1, step(tile_of(n_un - 1), 0, ms, nxt=diag), valid=diag_valid())

    def even_tail(ms):
        return step(diag, 0, ms, valid=diag_valid())

    qk_all(tile_of(0), 0)
    ms = (jnp.full((1, tq), NEG, F32),) * nh
    ms = lax.fori_loop(0, n_un // 2, pair, ms)
    lax.cond(n_un % 2 == 1, odd_tail, even_tail, ms)
    return finish()


def _flash_scratch(nh, tq, tk, kdim, col_max=True):
    bufs = [pltpu.VMEM((kdim, nh * tq), BF16), pltpu.VMEM((tk, nh * tq), F32),
            pltpu.VMEM((tk, nh * tq), F32)]
    if col_max:
        bufs.append(pltpu.VMEM((2, 1, nh * tq), F32))
    return bufs + [pltpu.VMEM((nh, ACC_ROWS, tq), F32)]


def _fox_kernel(qt_ref, qf_ref, k_ref, kf_ref, vt_ref, o_ref, bm_sc, s0_sc, s1_sc, mx_sc, acc_sc):
    qi = pl.program_id(2)
    tq, tk = FOX_TQ, FOX_TK
    bm_sc[...] = jnp.zeros_like(bm_sc)
    for h in range(2):
        rows = slice(h * HEAD_DIM, (h + 1) * HEAD_DIM)
        bm_sc[rows, h * tq:(h + 1) * tq] = qt_ref[0, rows, :]
        bm_sc[LANES:LANES + QF_ROWS, h * tq:(h + 1) * tq] = qf_ref[0, 0, h]
    qpos = qi * tq + lax.broadcasted_iota(jnp.int32, (tk, tq), 1)

    def k_tile(k0):
        return jnp.concatenate([k_ref[0, pl.ds(k0, tk), :], kf_ref[0, 0, pl.ds(k0, tk), :]], axis=1)

    def v_rows(h, k0):
        return vt_ref[0, h * HEAD_DIM:(h + 1) * HEAD_DIM, pl.ds(k0, tk)]

    diag = (qi * tq) // tk
    outs = _flash_t(bm_sc, (s0_sc, s1_sc), mx_sc, acc_sc, 2, tq, tk, k_tile, v_rows,
                    diag, lambda i: i, diag, lambda key: key <= qpos)
    o_ref[0] = jnp.concatenate(outs, axis=0).T.astype(o_ref.dtype)


def _fox(u3, ut, kfeat, qfeat, *, q_row, k_col, v_row):
    B, S, _ = u3.shape
    npair = FOX_HEADS // 2
    return pl.pallas_call(
        _fox_kernel,
        grid=(B, npair, S // FOX_TQ),
        in_specs=[
            pl.BlockSpec((1, LANES, FOX_TQ), lambda b, j, i: (b, q_row + j, i)),
            pl.BlockSpec((1, 1, 2, QF_ROWS, FOX_TQ), lambda b, j, i: (b, j, 0, 0, i)),
            pl.BlockSpec((1, S, LANES), lambda b, j, i: (b, 0, k_col + j)),
            pl.BlockSpec((1, 1, S, LANES), lambda b, j, i: (b, j, 0, 0)),
            pl.BlockSpec((1, LANES, S), lambda b, j, i: (b, v_row + j, 0)),
        ],
        out_specs=pl.BlockSpec((1, FOX_TQ, LANES), lambda b, j, i: (b, i, j)),
        out_shape=jax.ShapeDtypeStruct((B, S, FOX_WIDTH), BF16),
        scratch_shapes=_flash_scratch(2, FOX_TQ, FOX_TK, 2 * LANES),
        compiler_params=_cp(("parallel", "parallel", "arbitrary")),
        name="fox",
    )(ut, qfeat, u3, kfeat, ut)


def _ret_kernel(q_ref, k_ref, v_ref, inner_ref, cross_ref, kdec_ref, cd_ref, bd_ref, gn_ref,
                o_ref, state_sc):
    @pl.when(pl.program_id(1) == 0)
    def _():
        state_sc[...] = jnp.zeros_like(state_sc)

    low = _low_half((RET_CHUNK, LANES), 1)
    inv = 1.0 / HEAD_DIM
    for j in range(RET_HEADS // 2):
        cols = slice(j * LANES, (j + 1) * LANES)
        q, k, v = q_ref[0, :, cols], k_ref[0, :, cols], v_ref[0, :, cols]
        zero = jnp.zeros_like(q)
        qa, qb = jnp.where(low, q, zero), jnp.where(low, zero, q)
        pa = (_dot_nt(qa, k) * inner_ref[j, 0]).astype(BF16)
        pb = (_dot_nt(qb, k) * inner_ref[j, 1]).astype(BF16)
        o_in = jnp.where(low, _dot(pa, v), _dot(pb, v))
        state = state_sc[j]
        o = o_in + _dot(q, state.astype(BF16)) * cross_ref[j]
        kd = (k.astype(F32) * kdec_ref[j]).astype(BF16)
        state_sc[j] = state * cd_ref[j] + _dot_tn(kd, v) * bd_ref[...]
        sa = jnp.sum(jnp.where(low, o, 0.0), axis=-1, keepdims=True)
        st = jnp.sum(o, axis=-1, keepdims=True)
        mu = jnp.where(low, sa, st - sa) * inv
        d = o - mu
        d2 = d * d
        va = jnp.sum(jnp.where(low, d2, 0.0), axis=-1, keepdims=True)
        vt = jnp.sum(d2, axis=-1, keepdims=True)
        var = jnp.where(low, va, vt - va) * inv
        o_ref[0, :, cols] = (d * lax.rsqrt(var + GN_EPS) * gn_ref[:, cols]).astype(o_ref.dtype)


def _ret_constants():
    lg = np.log(1.0 - 2.0 ** (-5.0 - np.arange(RET_HEADS)))
    i = np.arange(RET_CHUNK)
    diff = i[:, None] - i[None, :]
    inner = np.where(diff[None] >= 0, np.exp(lg[:, None, None] * np.maximum(diff, 0)[None]), 0.0)
    cross = np.exp(lg[:, None] * (i[None, :] + 1))
    kdec = np.exp(lg[:, None] * (RET_CHUNK - 1 - i)[None, :])
    cdec = np.exp(lg * RET_CHUNK)
    npair = RET_HEADS // 2
    inner = inner.reshape(npair, 2, RET_CHUNK, RET_CHUNK)

    def lanes(a):
        a = a.reshape(npair, 2, RET_CHUNK)
        return np.repeat(a.transpose(0, 2, 1), HEAD_DIM, axis=2)

    bd = np.kron(np.eye(2), np.ones((HEAD_DIM, HEAD_DIM)))
    cd = np.repeat(cdec.reshape(npair, 2), HEAD_DIM, axis=1)[:, :, None] * bd[None]
    f = lambda a: jnp.asarray(a, F32)
    return f(inner), f(lanes(cross)), f(lanes(kdec)), f(cd), f(bd)


def _retention(u3, gn_g, *, q_col, k_col, v_col):
    B, S, _ = u3.shape
    C = RET_CHUNK
    npair = RET_HEADS // 2
    inner, cross, kdec, cd, bd = _ret_constants()
    full = lambda shape: pl.BlockSpec(shape, lambda b, i: (0,) * len(shape))
    return pl.pallas_call(
        _ret_kernel,
        grid=(B, S // C),
        in_specs=[
            pl.BlockSpec((1, C, RET_WIDTH), lambda b, i: (b, i, q_col)),
            pl.BlockSpec((1, C, RET_WIDTH), lambda b, i: (b, i, k_col)),
            pl.BlockSpec((1, C, RET_WIDTH), lambda b, i: (b, i, v_col)),
            full((npair, 2, C, C)), full((npair, C, LANES)), full((npair, C, LANES)),
            full((npair, LANES, LANES)), full((LANES, LANES)), full((1, RET_WIDTH)),
        ],
        out_specs=pl.BlockSpec((1, C, RET_WIDTH), lambda b, i: (b, i, 0)),
        out_shape=jax.ShapeDtypeStruct((B, S, RET_WIDTH), BF16),
        scratch_shapes=[pltpu.VMEM((npair, LANES, LANES), F32)],
        compiler_params=_cp(("parallel", "arbitrary")),
        name="retention",
    )(u3, u3, u3, inner, cross, kdec, cd, bd, gn_g.reshape(1, RET_WIDTH))


def _out0_kernel(of_ref, or_ref, z_ref, x_ref, w_ref, o_ref):
    z = _silu(z_ref[...].astype(F32))
    ya = (of_ref[...].astype(F32) * z[:, :FOX_WIDTH]).astype(BF16)
    yb = (or_ref[...].astype(F32) * z[:, FOX_WIDTH:]).astype(BF16)
    o_ref[...] = x_ref[...] + _dot(ya, w_ref[:FOX_WIDTH, :]) + _dot(yb, w_ref[FOX_WIDTH:, :])


def _out0(o_f, o_r, u, x2, w_out, *, tm=OUT_TM):
    N, D = x2.shape
    return pl.pallas_call(
        _out0_kernel,
        grid=(N // tm,),
        in_specs=[
            pl.BlockSpec((tm, FOX_WIDTH), lambda i: (i, 0)),
            pl.BlockSpec((tm, RET_WIDTH), lambda i: (i, 0)),
            pl.BlockSpec((tm, D), lambda i: (i, 0)),
            pl.BlockSpec((tm, D), lambda i: (i, 0)),
            pl.BlockSpec((D, D), lambda i: (0, 0)),
        ],
        out_specs=pl.BlockSpec((tm, D), lambda i: (i, 0)),
        out_shape=jax.ShapeDtypeStruct((N, D), F32),
        compiler_params=_cp(("parallel",)),
        name="out0",
    )(o_f, o_r, u, x2, w_out)


def _out1_kernel(oc_ref, os_ref, z_ref, x_ref, w_ref, g_ref, o_ref):
    z = _silu(z_ref[...].astype(F32))
    y = ((oc_ref[...].astype(F32) + os_ref[...].astype(F32)) * z).astype(BF16)
    o_ref[...] = _rms(x_ref[...] + _dot(y, w_ref[...]), g_ref[...])


def _out1(o_c, o_s, u, x2, w_out, final_g, *, tm=OUT_TM):
    N, D = x2.shape
    row = pl.BlockSpec((tm, D), lambda i: (i, 0))
    return pl.pallas_call(
        _out1_kernel,
        grid=(N // tm,),
        in_specs=[row, row, row, row,
                  pl.BlockSpec((D, D), lambda i: (0, 0)),
                  pl.BlockSpec((1, D), lambda i: (0, 0))],
        out_specs=row,
        out_shape=jax.ShapeDtypeStruct((N, D), F32),
        compiler_params=_cp(("parallel",)),
        name="out1",
    )(o_c, o_s, u, x2, w_out, final_g.reshape(1, D))


def _compress_kernel(x_ref, pea_ref, peb_ref, wa_ref, wb_ref, w2_ref, *rest, transposed):
    x = x_ref[0].astype(F32)
    a = _dot((x + pea_ref[...]).astype(BF16), wa_ref[0])
    b = _dot((x + peb_ref[...]).astype(BF16), wb_ref[0])
    nseg = x.shape[0]
    pre = a + pltpu.roll(b, nseg - 1, 0)
    hid = _silu(pre).astype(BF16)
    if transposed:
        o_ref, = rest
        o_ref[0, 0] = _dot_nt(w2_ref[...], hid).astype(o_ref.dtype)
    else:
        feat_ref, o_ref = rest
        o_ref[0, 0] = (_dot(hid, w2_ref[...]) + feat_ref[...].astype(F32)).astype(o_ref.dtype)


def _compress(a3, pe, w1, w2, *, transposed):
    B, nseg, wid = a3.shape
    half = CMP_STRIDE * HEAD_DIM
    eye = jnp.eye(NSA_GROUPS, dtype=w1.dtype)

    def big(wh):
        w4 = wh.reshape(CMP_STRIDE, 1, HEAD_DIM, CMP_HIDDEN)
        sel = eye[:, None, :, None, None]
        return (sel * w4[None]).reshape(NSA_GROUPS, wid, CMP_HIDDEN).astype(BF16)

    def pe_big(p):
        return jnp.broadcast_to(p[:, None, :], (CMP_STRIDE, NSA_GROUPS, HEAD_DIM)).reshape(1, wid)

    args = [a3, pe_big(pe[:CMP_STRIDE]), pe_big(pe[CMP_STRIDE:]), big(w1[:half]), big(w1[half:])]
    in_specs = [
        pl.BlockSpec((1, nseg, wid), lambda b, g: (b, 0, 0)),
        pl.BlockSpec((1, wid), lambda b, g: (0, 0)),
        pl.BlockSpec((1, wid), lambda b, g: (0, 0)),
        pl.BlockSpec((1, wid, CMP_HIDDEN), lambda b, g: (g, 0, 0)),
        pl.BlockSpec((1, wid, CMP_HIDDEN), lambda b, g: (g, 0, 0)),
    ]
    if transposed:
        w2d = w2.T.astype(BF16)
        oshape, oblock = (B, NSA_GROUPS, HEAD_DIM, nseg), (1, 1, HEAD_DIM, nseg)
        args.append(w2d)
        in_specs.append(pl.BlockSpec(w2d.shape, lambda b, g: (0, 0)))
    else:
        w2d = jnp.pad(w2, ((0, 0), (0, LANES - HEAD_DIM))).astype(BF16)
        oshape, oblock = (B, NSA_GROUPS, nseg, LANES), (1, 1, nseg, LANES)
        feat = _pos_features(jnp.arange(nseg, dtype=jnp.int32) * CMP_STRIDE + (CMP_BLOCK - 1), LANES)
        args += [w2d, feat]
        in_specs += [pl.BlockSpec(w2d.shape, lambda b, g: (0, 0)),
                     pl.BlockSpec((nseg, LANES), lambda b, g: (0, 0))]
    return pl.pallas_call(
        functools.partial(_compress_kernel, transposed=transposed),
        grid=(B, NSA_GROUPS),
        in_specs=in_specs,
        out_specs=pl.BlockSpec(oblock, lambda b, g: (b, g, 0, 0)),
        out_shape=jax.ShapeDtypeStruct(oshape, BF16),
        compiler_params=_cp(("parallel", "parallel")),
        name="compress",
    )(*args)


def _slope_table():
    s = np.asarray(2.0 ** (-8.0 * (np.arange(NSA_HEADS) + 1) / NSA_HEADS), np.float32)
    sl = np.asarray(s.astype(np.float64) * LOG2E, np.float32)
    p1, p2, p3 = _np_pieces(sl)
    tab = np.zeros((NSA_HEADS, QF_ROWS), np.float32)
    for k, p in enumerate((p1, p1, p2, p2, p3, p3)):
        tab[:, k] = p
    tab[:, 6] = sl
    tab = np.broadcast_to(tab.reshape(NSA_GROUPS, NSA_HPG * QF_ROWS, 1),
                          (NSA_GROUPS, NSA_HPG * QF_ROWS, NSA_TQ))
    return jnp.asarray(tab)


def _pos_features(pos, width):
    pos = pos[:, None]
    lane = jnp.arange(width, dtype=jnp.int32)[None, :] % LANES
    hi = ((pos // SLC_BLOCK) * SLC_BLOCK).astype(F32)
    lo = (pos % SLC_BLOCK).astype(F32)
    k = lane - FEAT0
    f = jnp.where((k >= 0) & (k < 6), jnp.where(k % 2 == 0, hi, lo), 0.0)
    f = jnp.where((k >= 6) & (k < 9), 1.0, f)
    return f.astype(BF16)


def _nsa_queries(qt_ref, tab_ref, t0):
    tq = qt_ref.shape[2]
    r = lax.broadcasted_iota(jnp.int32, (QF_ROWS, tq), 0)
    t = (t0 + lax.broadcasted_iota(jnp.int32, (1, tq), 1)).astype(F32)
    zeros = jnp.zeros((LANES - HEAD_DIM - QF_ROWS, tq), BF16)
    out = []
    for i in range(NSA_HPG):
        tile = tab_ref[0, i * QF_ROWS:(i + 1) * QF_ROWS, :]
        a1, a2, a3 = _pieces(-(tile[6:7, :] * t))
        feat = jnp.where(r == 6, a1, jnp.where(r == 7, a2, jnp.where(r == 8, a3,
                                                                     jnp.where(r < 6, tile, 0.0))))
        out.append(jnp.concatenate([qt_ref[0, i * HEAD_DIM:(i + 1) * HEAD_DIM, :],
                                    feat.astype(BF16), zeros], axis=0))
    return out


def _gates_t(gt_ref, bg_ref, branch):
    gl = gt_ref[0] + bg_ref[...]
    return [_sigmoid(gl[N_BRANCH * i + branch:N_BRANCH * i + branch + 1, :]) for i in range(NSA_HPG)]


def _gated(outs_t, gates):
    return [o * gt for o, gt in zip(outs_t, gates)]


def _store_heads(o_ref, g):
    o_ref[0, :, :LANES] = jnp.concatenate(g[:2], axis=0).T.astype(o_ref.dtype)
    o_ref[0, :, LANES:] = jnp.concatenate(g[2:], axis=0).T.astype(o_ref.dtype)


def _nsa_specs(q_row):
    return dict(
        q=pl.BlockSpec((1, NSA_HPG * HEAD_DIM, NSA_TQ), lambda b, g, i: (b, q_row + g, i)),
        tab=pl.BlockSpec((1, NSA_HPG * QF_ROWS, NSA_TQ), lambda b, g, i: (g, 0, 0)),
        gt=pl.BlockSpec((1, GATE_ROWS, NSA_TQ), lambda b, g, i: (b, g, i)),
        bg=pl.BlockSpec((GATE_ROWS, 1), lambda b, g, i: (g, 0)),
        out=pl.BlockSpec((1, NSA_TQ, NSA_HPG * HEAD_DIM), lambda b, g, i: (b, i, g)),
    )


def _cmp_body(nc, t0, qh, kc_ref, vct_ref, mt_ref, grp_ref, gates, o_ref, sel_ref, flag_ref):
    tq = NSA_TQ
    rows = nc * CMP_CHUNK
    full = max(rows - CMP_CHUNK - 8, 0)
    nseg = kc_ref.shape[2]
    kc = kc_ref[0, 0, :rows, :]
    vct = vct_ref[0, 0, :, :rows]
    t = t0 + lax.broadcasted_iota(jnp.int32, (1, tq), 1)
    cidx = full + lax.broadcasted_iota(jnp.int32, (rows - full, 1), 0)
    valid = (cidx * CMP_STRIDE + (CMP_BLOCK - 1) <= t) & (cidx < nseg - 1)
    psum = jnp.zeros((rows, tq), F32)
    ps = []
    s_all = _dot(kc, jnp.concatenate(qh, axis=1))
    for i in range(NSA_HPG):
        s = s_all[:, i * tq:(i + 1) * tq]
        s_last = jnp.where(valid, s[full:], NEG)
        m = jnp.max(s_last, axis=0, keepdims=True)
        if nc > 1:
            m = jnp.maximum(m, jnp.max(s[:full], axis=0, keepdims=True))
        e = jnp.where(valid, jnp.exp2(s_last - m), 0.0)
        if nc > 1:
            e = jnp.concatenate([jnp.exp2(s[:full] - m), e], axis=0)
        l = jnp.sum(e, axis=0, keepdims=True)
        p = e * jnp.where(l > 0.0, 1.0 / l, 0.0)
        psum = psum + p
        ps.append(p.astype(BF16))
    o_all = _dot(vct, jnp.concatenate(ps, axis=1))
    _store_heads(o_ref, _gated([o_all[:, i * tq:(i + 1) * tq] for i in range(NSA_HPG)], gates))
    ns = rows * CMP_STRIDE // SLC_BLOCK
    mt = mt_ref[:ns, :rows]
    imp = sum(_dot(mt, p.astype(BF16)) for p in _pieces(psum))
    blk = lax.broadcasted_iota(jnp.int32, (ns, 1), 0)
    cur = t // SLC_BLOCK
    bvalid = blk * SLC_BLOCK <= t
    forced = (blk == 0) | (blk == cur) | (blk == cur - 1)
    score = jnp.where(forced, -jnp.inf, jnp.where(bvalid, imp, NEG))
    blk_f = blk.astype(F32)
    sel = jnp.where(forced, 1.0, 0.0)
    for _ in range(SLC_TOPK - N_FORCED):
        mx = jnp.max(score, axis=0, keepdims=True)
        first = jnp.min(jnp.where(score == mx, blk_f, float(ns)), axis=0, keepdims=True)
        hit = blk_f == first
        sel = jnp.where(hit, 1.0, sel)
        score = jnp.where(hit, -jnp.inf, score)
    selneg = jnp.where(bvalid & (sel > 0.0), 0.0, -MASK_BIG)
    if ns < NS_PAD:
        selneg = jnp.concatenate([selneg, jnp.full((NS_PAD - ns, tq), -MASK_BIG, F32)], axis=0)
    sel_ref[0, 0] = selneg.astype(sel_ref.dtype)
    picked = jnp.where(selneg == 0.0, 1.0, 0.0).astype(BF16)
    used = _dot_nt(jnp.ones((8, tq), BF16), picked)
    used = jnp.where(used > 0.0, 1.0, 0.0).astype(BF16)
    flag_ref[0] = (_dot(used, grp_ref[...])[0:1] > 0.0).astype(jnp.int32)


def _cmp_kernel(q_ref, kc_ref, vct_ref, mt_ref, grp_ref, tab_ref, gt_ref, bg_ref,
                o_ref, sel_ref, flag_ref):
    t0 = pl.program_id(2) * NSA_TQ
    qh = _nsa_queries(q_ref, tab_ref, t0)
    gates = _gates_t(gt_ref, bg_ref, 0)
    nchunk = kc_ref.shape[2] // CMP_CHUNK
    last = t0 // (CMP_CHUNK * CMP_STRIDE)
    for nc in range(1, nchunk + 1):
        pl.when(last == nc - 1)(functools.partial(
            _cmp_body, nc, t0, qh, kc_ref, vct_ref, mt_ref, grp_ref, gates, o_ref, sel_ref, flag_ref))


def _cmp_to_slc_t(nseg, ns):
    c0 = np.arange(nseg)[:, None] * CMP_STRIDE
    s0 = np.arange(ns)[None, :] * SLC_BLOCK
    overlap = np.clip(np.minimum(c0 + CMP_BLOCK, s0 + SLC_BLOCK) - np.maximum(c0, s0), 0, None)
    m = overlap / CMP_STRIDE
    m[nseg - 1] = 0.0
    mt = np.zeros((NS_PAD, nseg))
    mt[:ns] = m.T
    return jnp.asarray(mt, BF16)


def _tile_groups():
    per = SLC_TK // SLC_BLOCK
    g = (np.arange(NS_PAD)[:, None] // per) == np.arange(NS_PAD)[None, :]
    return jnp.asarray(g, BF16)


def _cmp_attention(ut, kcmp, vcmp_t, gt, bg, *, q_row):
    B, _, S = ut.shape
    nseg = kcmp.shape[2]
    nq = S // NSA_TQ
    sp = _nsa_specs(q_row)
    return pl.pallas_call(
        _cmp_kernel,
        grid=(B, NSA_GROUPS, S // NSA_TQ),
        in_specs=[
            sp["q"],
            pl.BlockSpec((1, 1, nseg, LANES), lambda b, g, i: (b, g, 0, 0)),
            pl.BlockSpec((1, 1, HEAD_DIM, nseg), lambda b, g, i: (b, g, 0, 0)),
            pl.BlockSpec((NS_PAD, nseg), lambda b, g, i: (0, 0)),
            pl.BlockSpec((NS_PAD, NS_PAD), lambda b, g, i: (0, 0)),
            sp["tab"], sp["gt"], sp["bg"],
        ],
        out_specs=[sp["out"], pl.BlockSpec((1, 1, NS_PAD, NSA_TQ), lambda b, g, i: (b, g, 0, i)),
                   pl.BlockSpec((1, 1, NS_PAD), lambda b, g, i: ((b * NSA_GROUPS + g) * nq + i, 0, 0))],
        out_shape=[jax.ShapeDtypeStruct((B, S, NSA_WIDTH), BF16),
                   jax.ShapeDtypeStruct((B, NSA_GROUPS, NS_PAD, S), BF16),
                   jax.ShapeDtypeStruct((B * NSA_GROUPS * nq, 1, NS_PAD), jnp.int32)],
        compiler_params=_cp(("parallel", "parallel", "arbitrary")),
        name="cmp_attention",
    )(ut, kcmp, vcmp_t, _cmp_to_slc_t(nseg, S // SLC_BLOCK), _tile_groups(), _slope_table(), gt, bg)


def _window_branch(qh, t0, k_ref, vt_ref, bias_ref, bm_sc, s_bufs, acc_sc):
    tq, tk = NSA_TQ, WIN_TK
    bm_sc[...] = jnp.concatenate(qh, axis=1)
    qk_all, step, finish = _flash_ops(
        bm_sc, s_bufs, None, acc_sc, NSA_HPG, tq, tk,
        lambda k0: k_ref[0, pl.ds(k0, tk), :], lambda h, k0: vt_ref[0, :, pl.ds(k0, tk)])
    first = jnp.maximum(t0 - WINDOW, 0) // tk
    ntile = (WINDOW + tq) // tk
    qk_all(first, 0)
    ms = (jnp.full((1, tq), NEG, F32),) * NSA_HPG
    for j in range(ntile):
        ms = step(first + j, j % 2, ms, nxt=first + j + 1 if j + 1 < ntile else None,
                  bias=bias_ref[0, j * tk:(j + 1) * tk, :])
    return finish()


def _slc_win_kernel(fl_ref, q_ref, k_ref, vt_ref, sel_ref, e_ref, kw_ref, vwt_ref, bias_ref,
                    tab_ref, gt_ref, bg_ref, o_ref, bm_sc, s0_sc, s1_sc, mx_sc, acc_sc,
                    wbm_sc, ws0_sc, ws1_sc, wacc_sc, tiles_sm):
    qi = pl.program_id(2)
    tq, tk = NSA_TQ, SLC_TK
    t0 = qi * tq
    diag = t0 // tk
    row = (pl.program_id(0) * NSA_GROUPS + pl.program_id(1)) * pl.num_programs(2) + qi
    n_un = jnp.int32(0)
    for j in range(k_ref.shape[1] // tk):
        tiles_sm[n_un] = jnp.int32(j)
        n_un = n_un + ((fl_ref[row, j] > 0) & (j < diag)).astype(jnp.int32)
    tiles_sm[n_un] = diag
    qh = _nsa_queries(q_ref, tab_ref, t0)
    o_win = _gated(_window_branch(qh, t0, kw_ref, vwt_ref, bias_ref, wbm_sc, (ws0_sc, ws1_sc),
                                  wacc_sc), _gates_t(gt_ref, bg_ref, 2))
    selneg = sel_ref[0, 0]
    bm_sc[...] = jnp.concatenate([jnp.concatenate([q, selneg], axis=0) for q in qh], axis=1)
    qpos = t0 + lax.broadcasted_iota(jnp.int32, (tk, tq), 1)

    def k_tile(k0):
        return jnp.concatenate([k_ref[0, pl.ds(k0, tk), :], e_ref[pl.ds(k0, tk), :]], axis=1)

    def v_rows(h, k0):
        return vt_ref[0, :, pl.ds(k0, tk)]

    outs = _flash_t(bm_sc, (s0_sc, s1_sc), mx_sc, acc_sc, NSA_HPG, tq, tk, k_tile, v_rows,
                    n_un, lambda i: tiles_sm[i], diag, lambda key: key <= qpos)
    o_slc = _gated(outs, _gates_t(gt_ref, bg_ref, 1))
    _store_heads(o_ref, [a + b for a, b in zip(o_slc, o_win)])


def _win_bias():
    tq, nk = NSA_TQ, WINDOW + NSA_TQ
    out = []
    for p in range(WINDOW // tq + 1):
        t0 = p * tq
        key = max(t0 - WINDOW, 0) + np.arange(nk)[:, None]
        qpos = t0 + np.arange(tq)[None, :]
        out.append(np.where((key <= qpos) & (key > qpos - WINDOW), 0.0, NEG))
    return jnp.asarray(np.stack(out), F32)


def _block_onehot(S):
    e = (np.arange(S)[:, None] // SLC_BLOCK) == np.arange(NS_PAD)[None, :]
    return jnp.asarray(e, BF16)


def _slc_win_attention(u3, ut, selneg, flags, gt, bg, *, q_row, ks_col, vs_row, kw_col, vw_row):
    B, S, _ = u3.shape
    sp = {k: pl.BlockSpec(v.block_shape, lambda b, g, i, fl, f=v.index_map: f(b, g, i))
          for k, v in _nsa_specs(q_row).items()}
    bias = _win_bias()
    npat = bias.shape[0]
    kspec = lambda col: pl.BlockSpec((1, S, LANES), lambda b, g, i, fl: (b, 0, col + g))
    vspec = lambda row: pl.BlockSpec((1, HEAD_DIM, S), lambda b, g, i, fl: (b, row + g, 0))
    grid_spec = pltpu.PrefetchScalarGridSpec(
        num_scalar_prefetch=1,
        grid=(B, NSA_GROUPS, S // NSA_TQ),
        in_specs=[
            sp["q"], kspec(ks_col), vspec(vs_row),
            pl.BlockSpec((1, 1, NS_PAD, NSA_TQ), lambda b, g, i, fl: (b, g, 0, i)),
            pl.BlockSpec((S, NS_PAD), lambda b, g, i, fl: (0, 0)),
            kspec(kw_col), vspec(vw_row),
            pl.BlockSpec((1,) + bias.shape[1:],
                         lambda b, g, i, fl: (jnp.minimum(i, npat - 1), 0, 0)),
            sp["tab"], sp["gt"], sp["bg"],
        ],
        out_specs=sp["out"],
        scratch_shapes=_flash_scratch(NSA_HPG, NSA_TQ, SLC_TK, 2 * LANES)
        + _flash_scratch(NSA_HPG, NSA_TQ, WIN_TK, LANES, col_max=False)
        + [pltpu.SMEM((S // SLC_TK + 1,), jnp.int32)],
    )
    return pl.pallas_call(
        _slc_win_kernel,
        grid_spec=grid_spec,
        out_shape=jax.ShapeDtypeStruct((B, S, NSA_WIDTH), BF16),
        compiler_params=_cp(("parallel", "parallel", "arbitrary")),
        name="slc_win_attention",
    )(flags, ut, u3, ut, selneg, _block_onehot(S), u3, ut, bias, _slope_table(), gt, bg)


def _aug_groups(w):
    d = w.shape[0]
    w = w.reshape(d, NSA_GROUPS, HEAD_DIM)
    return jnp.pad(w, ((0, 0), (0, 0), (0, LANES - HEAD_DIM))).reshape(d, NSA_GROUPS * LANES)


def _even_layer(x, norm_g, w_in, b_f, gn_g, w_out):
    B, S, D = x.shape
    qscale = HEAD_DIM ** -0.5 * LOG2E
    q_f, k_f, v_f, w_fl, q_r, k_r, v_r, z = jnp.split(
        w_in, np.cumsum([FOX_WIDTH] * 3 + [FOX_HEADS] + [RET_WIDTH] * 3).tolist(), axis=1)
    w = jnp.concatenate([z, k_f, q_r, k_r * HEAD_DIM ** -0.5, v_r], axis=1).astype(BF16)
    w_t = jnp.concatenate([q_f * qscale, v_f], axis=1).T.astype(BF16)
    x2 = x.reshape(B * S, D)
    u, ut = _proj(x2, norm_g, w, seq=S, w_t=[w_t], t_dtypes=[BF16])
    u3 = u.reshape(B, S, -1)
    kfeat, qfeat = _fgate(x, norm_g, w_fl, b_f, tile=min(512, S))
    o_f = _fox(u3, ut, kfeat, qfeat, q_row=0, k_col=D // LANES, v_row=FOX_WIDTH // LANES)
    rb = (D + FOX_WIDTH) // RET_WIDTH
    o_r = _retention(u3, gn_g, q_col=rb, k_col=rb + 1, v_col=rb + 2)
    out = _out0(o_f.reshape(B * S, -1), o_r.reshape(B * S, -1), u, x2, w_out.astype(BF16))
    return out.reshape(B, S, D)


def _odd_layer(x, norm_g, w_in, b_gate, pe_k, pe_v, wk1, wk2, wv1, wv2, w_out, final_g):
    B, S, D = x.shape
    assert S // SLC_BLOCK <= NS_PAD
    qscale = HEAD_DIM ** -0.5 * LOG2E
    sizes = [NSA_WIDTH] + [NSA_KV_WIDTH] * 6 + [NSA_HEADS * N_BRANCH]
    q, kc, vc, ks, vs, kw, vw, gl, z = jnp.split(w_in, np.cumsum(sizes).tolist(), axis=1)
    w = jnp.concatenate([z, kc, vc, _aug_groups(ks), _aug_groups(kw)], axis=1).astype(BF16)
    per_group = NSA_HPG * N_BRANCH
    glt = jnp.pad(gl.T.reshape(NSA_GROUPS, per_group, D), ((0, 0), (0, GATE_ROWS - per_group), (0, 0)))
    glt = glt.reshape(NSA_GROUPS * GATE_ROWS, D).astype(BF16)
    bg = jnp.pad(b_gate.reshape(NSA_GROUPS, per_group), ((0, 0), (0, GATE_ROWS - per_group)))
    bg = bg.reshape(NSA_GROUPS * GATE_ROWS, 1)
    w_vt = jnp.concatenate([q * qscale, vs, vw], axis=1).T.astype(BF16)
    x2 = x.reshape(B * S, D)
    first_k = (D + 2 * NSA_KV_WIDTH) // PROJ_TN
    u, kc_a, vc_a, ut, gt = _proj(
        x2, norm_g, w, seq=S, split_tile=D // PROJ_TN,
        addend=_pos_features(jnp.arange(S, dtype=jnp.int32), PROJ_TN), add_tiles=(first_k, first_k + 1),
        w_t=[w_vt, glt], t_dtypes=[BF16, F32])
    u3 = u.reshape(B, S, -1)
    nseg = S // CMP_STRIDE
    kcmp = _compress(kc_a.reshape(B, nseg, -1), pe_k, wk1, wk2, transposed=False)
    vcmp_t = _compress(vc_a.reshape(B, nseg, -1), pe_v, wv1, wv2, transposed=True)
    o_c, selneg, flags = _cmp_attention(ut, kcmp, vcmp_t, gt, bg, q_row=0)
    kb = (D + 2 * NSA_KV_WIDTH) // LANES
    vb = NSA_WIDTH // HEAD_DIM
    o_s = _slc_win_attention(u3, ut, selneg, flags[:, 0, :S // SLC_TK], gt, bg, q_row=0, ks_col=kb,
                             vs_row=vb, kw_col=kb + NSA_GROUPS, vw_row=vb + NSA_GROUPS)
    r = lambda a: a.reshape(B * S, -1)
    out = _out1(r(o_c), r(o_s), u, x2, w_out.astype(BF16), final_g)
    return out.reshape(B, S, D)


def kernel(x, even_norm_g, even_w_in, even_b_f, even_gn_g, even_w_out, odd_norm_g, odd_w_in,
           odd_b_gate, odd_pe_k, odd_pe_v, odd_wk1, odd_wk2, odd_wv1, odd_wv2, odd_w_out, final_g):
    x = _even_layer(x, even_norm_g[0], even_w_in[0], even_b_f[0], even_gn_g[0], even_w_out[0])
    return _odd_layer(x, odd_norm_g[0], odd_w_in[0], odd_b_gate[0], odd_pe_k[0], odd_pe_v[0],
                      odd_wk1[0], odd_wk2[0], odd_wv1[0], odd_wv2[0], odd_w_out[0], final_g)
```

```python
import functools
import math

import jax
import jax.numpy as jnp
import numpy as np
from jax import lax
from jax.experimental import pallas as pl
from jax.experimental.pallas import tpu as pltpu

D_MODEL = 1024
HEAD_DIM = 64
LANES = 128
FOX_HEADS = 8
RET_HEADS = 8
FOX_WIDTH = FOX_HEADS * HEAD_DIM
RET_WIDTH = RET_HEADS * HEAD_DIM
RET_CHUNK = 128
NSA_HEADS = 16
NSA_GROUPS = 4
NSA_HPG = NSA_HEADS // NSA_GROUPS
NSA_WIDTH = NSA_HEADS * HEAD_DIM
NSA_KV_WIDTH = NSA_GROUPS * HEAD_DIM
N_BRANCH = 3
GATE_ROWS = 16
CMP_BLOCK = 32
CMP_STRIDE = 16
CMP_HIDDEN = 256
CMP_CHUNK = 128
SLC_BLOCK = 64
SLC_TOPK = 16
N_FORCED = 3
NS_PAD = LANES
WINDOW = 512
RMS_EPS = 1e-6
GN_EPS = 1e-5
NEG = -1e30
FORCE_BONUS = 1e6
MASK_BIG = 2.0 ** 100
LOG2E = math.log2(math.e)
FEAT0 = HEAD_DIM
QF_ROWS = 16
ACC_ROWS = HEAD_DIM + 16

PROJ_TM = 1024
PROJ_TN = 512
FOX_TQ = 512
FOX_TK = 512
NSA_TQ = 256
SLC_TK = 512
WIN_TK = 256
OUT_TM = 1024
VMEM_LIMIT = 48 * 1024 * 1024

F32 = jnp.float32
BF16 = jnp.bfloat16


def _cp(sem, vmem=VMEM_LIMIT):
    return pltpu.CompilerParams(dimension_semantics=sem, vmem_limit_bytes=vmem)


def _dot(a, b):
    return jnp.dot(a, b, preferred_element_type=F32)


def _dot_nt(a, b):
    return lax.dot_general(a, b, (((1,), (1,)), ((), ())), preferred_element_type=F32)


def _dot_tn(a, b):
    return lax.dot_general(a, b, (((0,), (0,)), ((), ())), preferred_element_type=F32)


def _rms(x, g):
    return x * lax.rsqrt(jnp.mean(x * x, axis=-1, keepdims=True) + RMS_EPS) * g


def _silu(x):
    return x * (1.0 / (1.0 + jnp.exp(-x)))


def _sigmoid(x):
    return 1.0 / (1.0 + jnp.exp(-x))


def _low_half(shape, axis):
    return lax.broadcasted_iota(jnp.int32, shape, axis) < HEAD_DIM


def _pieces(v):
    p1 = v.astype(BF16).astype(F32)
    r = v - p1
    p2 = r.astype(BF16).astype(F32)
    p3 = (r - p2).astype(BF16).astype(F32)
    return p1, p2, p3


def _np_pieces(v):
    v = np.asarray(v, np.float64)
    bf = lambda a: np.asarray(a, np.float32).astype(BF16).astype(np.float64)
    p1 = bf(v)
    p2 = bf(v - p1)
    p3 = bf(v - p1 - p2)
    return p1, p2, p3


def _place(lane, cols):
    out = jnp.zeros(lane.shape, F32)
    for i, c in cols.items():
        out = jnp.where(lane == i, c, out)
    return out


def _fgate_kernel(x_ref, g_ref, wf_ref, b_ref, kf_ref, qf_ref, carry):
    @pl.when(pl.program_id(1) == 0)
    def _():
        carry[...] = jnp.zeros_like(carry)

    h = _rms(x_ref[0], g_ref[...])
    t = h.shape[0]
    h1 = h.astype(BF16)
    h2 = (h - h1.astype(F32)).astype(BF16)
    w = wf_ref[...]
    w1 = w.astype(BF16)
    w2 = (w - w1.astype(F32)).astype(BF16)
    nf = w.shape[1]
    r = _dot(jnp.concatenate([h1, h2], axis=0), jnp.concatenate([w1, w2], axis=1))
    f = r[:t, :nf] + r[:t, nf:] + r[t:, :nf] + b_ref[...]
    ls = jnp.minimum(f, 0.0) - jnp.log(1.0 + jnp.exp(-jnp.abs(f)))
    r = lax.broadcasted_iota(jnp.int32, (t, t), 0)
    c = lax.broadcasted_iota(jnp.int32, (t, t), 1)
    lower = jnp.where(c <= r, 1.0, 0.0).astype(BF16)
    r = _dot(lower, jnp.concatenate([p.astype(BF16) for p in _pieces(ls)], axis=1))
    cs = r[:, :nf] + r[:, nf:2 * nf] + r[:, 2 * nf:] + carry[...]
    carry[...] = cs[t - 1:t, :]
    cl = cs * LOG2E
    lane = lax.broadcasted_iota(jnp.int32, (t, LANES), 1)
    one = jnp.ones((t, 1), F32)
    for j in range(FOX_HEADS // 2):
        a1, a2, a3 = _pieces(cl[:, 2 * j:2 * j + 1])
        b1, b2, b3 = _pieces(cl[:, 2 * j + 1:2 * j + 2])
        kf = _place(lane, {0: a1, 1: a2, 2: a3, 3: one, 4: one, 5: one, 6: b1, 7: b2, 8: b3})
        kf_ref[0, j] = kf.astype(BF16)
        qa = _place(lane, {0: -one, 1: -one, 2: -one, 3: a1, 4: a2, 5: a3})
        qb = _place(lane, {3: b1, 4: b2, 5: b3, 6: -one, 7: -one, 8: -one})
        qf_ref[0, j, 0] = qa.T[:QF_ROWS].astype(BF16)
        qf_ref[0, j, 1] = qb.T[:QF_ROWS].astype(BF16)


def _fgate(x, g, wf, b_f, *, tile):
    B, S, D = x.shape
    npair = FOX_HEADS // 2
    return pl.pallas_call(
        _fgate_kernel,
        grid=(B, S // tile),
        in_specs=[
            pl.BlockSpec((1, tile, D), lambda b, s: (b, s, 0)),
            pl.BlockSpec((1, D), lambda b, s: (0, 0)),
            pl.BlockSpec((D, FOX_HEADS), lambda b, s: (0, 0)),
            pl.BlockSpec((1, FOX_HEADS), lambda b, s: (0, 0)),
        ],
        out_specs=[
            pl.BlockSpec((1, npair, tile, LANES), lambda b, s: (b, 0, s, 0)),
            pl.BlockSpec((1, npair, 2, QF_ROWS, tile), lambda b, s: (b, 0, 0, 0, s)),
        ],
        out_shape=[
            jax.ShapeDtypeStruct((B, npair, S, LANES), BF16),
            jax.ShapeDtypeStruct((B, npair, 2, QF_ROWS, S), BF16),
        ],
        scratch_shapes=[pltpu.VMEM((1, FOX_HEADS), F32)],
        compiler_params=_cp(("parallel", "arbitrary")),
        name="fgate",
    )(x, g.reshape(1, D), wf, b_f.reshape(1, FOX_HEADS))


def _proj_kernel(*refs, split_tile, add_tiles, n_t):
    it = iter(refs)
    x_ref, g_ref, w_ref = next(it), next(it), next(it)
    add_ref = next(it) if add_tiles else None
    wt_refs = [next(it) for _ in range(n_t)]
    u_ref = next(it)
    e_refs = [next(it), next(it)] if split_tile is not None else None
    ut_refs = [next(it) for _ in range(n_t)]
    h_sc = next(it)
    j = pl.program_id(1)

    @pl.when(j == 0)
    def _():
        h = _rms(x_ref[...], g_ref[...]).astype(BF16)
        h_sc[...] = h
        for wt_ref, ut_ref in zip(wt_refs, ut_refs):
            ut_ref[0] = _dot_nt(wt_ref[...], h).astype(ut_ref.dtype)

    acc = _dot(h_sc[...], w_ref[...])
    if add_tiles:
        is_add = functools.reduce(jnp.logical_or, [j == t for t in add_tiles])

        @pl.when(is_add)
        def _():
            u_ref[...] = (acc + add_ref[...].astype(F32)).astype(u_ref.dtype)

        @pl.when(jnp.logical_not(is_add))
        def _():
            u_ref[...] = acc.astype(u_ref.dtype)
    else:
        u_ref[...] = acc.astype(u_ref.dtype)
    if split_tile is not None:
        @pl.when(j == split_tile)
        def _():
            half = acc.shape[1] // 2
            e_refs[0][...] = acc[:, :half].astype(e_refs[0].dtype)
            e_refs[1][...] = acc[:, half:].astype(e_refs[1].dtype)


def _proj(x2, g, w, *, seq, split_tile=None, addend=None, add_tiles=(), w_t=(), t_dtypes=(),
          tm=PROJ_TM, tn=PROJ_TN):
    N, D = x2.shape
    W = w.shape[1]
    nbs = seq // tm
    in_specs = [
        pl.BlockSpec((tm, D), lambda i, j: (i, 0)),
        pl.BlockSpec((1, D), lambda i, j: (0, 0)),
        pl.BlockSpec((D, tn), lambda i, j: (0, j)),
    ]
    args = [x2, g.reshape(1, D), w]
    if add_tiles:
        in_specs.append(pl.BlockSpec((tm, tn), lambda i, j: (i % nbs, 0)))
        args.append(addend)
    out_shape = [jax.ShapeDtypeStruct((N, W), BF16)]
    out_specs = [pl.BlockSpec((tm, tn), lambda i, j: (i, j))]
    if split_tile is not None:
        out_shape += [jax.ShapeDtypeStruct((N, tn // 2), BF16)] * 2
        out_specs += [pl.BlockSpec((tm, tn // 2), lambda i, j: (i, 0))] * 2
    for wt, dt in zip(w_t, t_dtypes):
        rows = wt.shape[0]
        in_specs.append(pl.BlockSpec((rows, D), lambda i, j: (0, 0)))
        args.append(wt)
        out_shape.append(jax.ShapeDtypeStruct((N // seq, rows, seq), dt))
        out_specs.append(pl.BlockSpec((1, rows, tm), lambda i, j: (i // nbs, 0, i % nbs)))
    return pl.pallas_call(
        functools.partial(_proj_kernel, split_tile=split_tile, add_tiles=tuple(add_tiles),
                          n_t=len(w_t)),
        grid=(N // tm, W // tn),
        in_specs=in_specs,
        out_specs=out_specs,
        out_shape=out_shape,
        scratch_shapes=[pltpu.VMEM((tm, D), BF16)],
        compiler_params=_cp(("parallel", "arbitrary")),
        name="proj",
    )(*args)


def _flash_ops(bm_sc, s_bufs, mx_sc, acc_sc, nh, tq, tk, k_tile, v_rows):
    acc_sc[...] = jnp.zeros_like(acc_sc)
    ones = jnp.ones((ACC_ROWS - HEAD_DIM, tk), BF16)

    def qk_head(tile, slot, h):
        cols = slice(h * tq, (h + 1) * tq)
        s = _dot(k_tile(pl.multiple_of(tile * tk, tk)), bm_sc[:, cols])
        s_bufs[slot][:, cols] = s
        if mx_sc is not None:
            mx_sc[slot, :, cols] = jnp.max(s, axis=0, keepdims=True)

    def soft_head(tile, slot, m_old, h, valid, bias):
        cols = slice(h * tq, (h + 1) * tq)
        buf = s_bufs[slot]
        k0 = pl.multiple_of(tile * tk, tk)
        if valid is not None or bias is not None:
            s = buf[:, cols] + bias if valid is None else jnp.where(valid, buf[:, cols], NEG)
            m_new = jnp.maximum(m_old, jnp.max(s, axis=0, keepdims=True))
            p = jnp.exp2(s - m_new).astype(BF16)
        else:
            m_new = jnp.maximum(m_old, mx_sc[slot, :, cols])
            p = jnp.exp2(buf[:, cols] - m_new).astype(BF16)
        alpha = jnp.exp2(m_old - m_new)
        lhs = jnp.concatenate([v_rows(h, k0), ones], axis=0)
        acc_sc[h] = alpha * acc_sc[h] + _dot(lhs, p)
        return m_new

    def qk_all(tile, slot):
        for h in range(nh):
            qk_head(tile, slot, h)

    def step(cur, slot, ms, nxt=None, valid=None, bias=None, before=None):
        out = []
        for h in range(nh):
            if nxt is not None:
                qk_head(nxt, 1 - slot, h)
            if before is not None:
                before(h)
            out.append(soft_head(cur, slot, ms[h], h, valid, bias))
        return tuple(out)

    def finish():
        outs = []
        for h in range(nh):
            a = acc_sc[h]
            outs.append(a[:HEAD_DIM] * (1.0 / a[HEAD_DIM:HEAD_DIM + 1]))
        return outs

    return qk_head, qk_all, step, finish


def _flash_t(ops, nh, tq, tk, n_un, tile_of, diag, valid_fn, first_done=False):
    _, qk_all, step, finish = ops
    key_iota = lax.broadcasted_iota(jnp.int32, (tk, tq), 0)
    diag_valid = lambda: valid_fn(pl.multiple_of(diag * tk, tk) + key_iota)

    def pair(i, ms):
        t_a, t_b, t_c = tile_of(2 * i), tile_of(2 * i + 1), tile_of(2 * i + 2)
        return step(t_b, 1, step(t_a, 0, ms, nxt=t_b), nxt=t_c)

    def odd_tail(ms):
        return step(diag, 1, step(tile_of(n_un - 1), 0, ms, nxt=diag), valid=diag_valid())

    def even_tail(ms):
        return step(diag, 0, ms, valid=diag_valid())

    if not first_done:
        qk_all(tile_of(0), 0)
    ms = (jnp.full((1, tq), NEG, F32),) * nh
    ms = lax.fori_loop(0, n_un // 2, pair, ms)
    lax.cond(n_un % 2 == 1, odd_tail, even_tail, ms)
    return finish()


def _flash_scratch(nh, tq, tk, kdim, col_max=True):
    bufs = [pltpu.VMEM((kdim, nh * tq), BF16), pltpu.VMEM((tk, nh * tq), F32),
            pltpu.VMEM((tk, nh * tq), F32)]
    if col_max:
        bufs.append(pltpu.VMEM((2, 1, nh * tq), F32))
    return bufs + [pltpu.VMEM((nh, ACC_ROWS, tq), F32)]


def _fox_kernel(qt_ref, qf_ref, k_ref, kf_ref, vt_ref, o_ref, bm_sc, s0_sc, s1_sc, mx_sc, acc_sc):
    qi = pl.program_id(2)
    tq, tk = FOX_TQ, FOX_TK
    bm_sc[...] = jnp.zeros_like(bm_sc)
    for h in range(2):
        rows = slice(h * HEAD_DIM, (h + 1) * HEAD_DIM)
        bm_sc[rows, h * tq:(h + 1) * tq] = qt_ref[0, rows, :]
        bm_sc[LANES:LANES + QF_ROWS, h * tq:(h + 1) * tq] = qf_ref[0, 0, h]
    qpos = qi * tq + lax.broadcasted_iota(jnp.int32, (tk, tq), 1)

    def k_tile(k0):
        return jnp.concatenate([k_ref[0, pl.ds(k0, tk), :], kf_ref[0, 0, pl.ds(k0, tk), :]], axis=1)

    def v_rows(h, k0):
        return vt_ref[0, h * HEAD_DIM:(h + 1) * HEAD_DIM, pl.ds(k0, tk)]

    diag = (qi * tq) // tk
    ops = _flash_ops(bm_sc, (s0_sc, s1_sc), mx_sc, acc_sc, 2, tq, tk, k_tile, v_rows)
    outs = _flash_t(ops, 2, tq, tk, diag, lambda i: i, diag, lambda key: key <= qpos)
    o_ref[0] = jnp.concatenate(outs, axis=0).T.astype(o_ref.dtype)


def _fox(u3, ut, kfeat, qfeat, *, q_row, k_col, v_row):
    B, S, _ = u3.shape
    npair = FOX_HEADS // 2
    return pl.pallas_call(
        _fox_kernel,
        grid=(B, npair, S // FOX_TQ),
        in_specs=[
            pl.BlockSpec((1, LANES, FOX_TQ), lambda b, j, i: (b, q_row + j, i)),
            pl.BlockSpec((1, 1, 2, QF_ROWS, FOX_TQ), lambda b, j, i: (b, j, 0, 0, i)),
            pl.BlockSpec((1, S, LANES), lambda b, j, i: (b, 0, k_col + j)),
            pl.BlockSpec((1, 1, S, LANES), lambda b, j, i: (b, j, 0, 0)),
            pl.BlockSpec((1, LANES, S), lambda b, j, i: (b, v_row + j, 0)),
        ],
        out_specs=pl.BlockSpec((1, FOX_TQ, LANES), lambda b, j, i: (b, i, j)),
        out_shape=jax.ShapeDtypeStruct((B, S, FOX_WIDTH), BF16),
        scratch_shapes=_flash_scratch(2, FOX_TQ, FOX_TK, 2 * LANES),
        compiler_params=_cp(("parallel", "parallel", "arbitrary")),
        name="fox",
    )(ut, qfeat, u3, kfeat, ut)


def _ret_kernel(q_ref, k_ref, v_ref, inner_ref, cross_ref, kdec_ref, cd_ref, bd_ref, gn_ref,
                o_ref, state_sc):
    @pl.when(pl.program_id(1) == 0)
    def _():
        state_sc[...] = jnp.zeros_like(state_sc)

    low = _low_half((RET_CHUNK, LANES), 1)
    inv = 1.0 / HEAD_DIM
    for j in range(RET_HEADS // 2):
        cols = slice(j * LANES, (j + 1) * LANES)
        q, k, v = q_ref[0, :, cols], k_ref[0, :, cols], v_ref[0, :, cols]
        zero = jnp.zeros_like(q)
        qa, qb = jnp.where(low, q, zero), jnp.where(low, zero, q)
        pa = (_dot_nt(qa, k) * inner_ref[j, 0]).astype(BF16)
        pb = (_dot_nt(qb, k) * inner_ref[j, 1]).astype(BF16)
        o_in = jnp.where(low, _dot(pa, v), _dot(pb, v))
        state = state_sc[j]
        o = o_in + _dot(q, state.astype(BF16)) * cross_ref[j]
        kd = (k.astype(F32) * kdec_ref[j]).astype(BF16)
        state_sc[j] = state * cd_ref[j] + _dot_tn(kd, v) * bd_ref[...]
        sa = jnp.sum(jnp.where(low, o, 0.0), axis=-1, keepdims=True)
        st = jnp.sum(o, axis=-1, keepdims=True)
        mu = jnp.where(low, sa, st - sa) * inv
        d = o - mu
        d2 = d * d
        va = jnp.sum(jnp.where(low, d2, 0.0), axis=-1, keepdims=True)
        vt = jnp.sum(d2, axis=-1, keepdims=True)
        var = jnp.where(low, va, vt - va) * inv
        o_ref[0, :, cols] = (d * lax.rsqrt(var + GN_EPS) * gn_ref[:, cols]).astype(o_ref.dtype)


def _ret_constants():
    lg = np.log(1.0 - 2.0 ** (-5.0 - np.arange(RET_HEADS)))
    i = np.arange(RET_CHUNK)
    diff = i[:, None] - i[None, :]
    inner = np.where(diff[None] >= 0, np.exp(lg[:, None, None] * np.maximum(diff, 0)[None]), 0.0)
    cross = np.exp(lg[:, None] * (i[None, :] + 1))
    kdec = np.exp(lg[:, None] * (RET_CHUNK - 1 - i)[None, :])
    cdec = np.exp(lg * RET_CHUNK)
    npair = RET_HEADS // 2
    inner = inner.reshape(npair, 2, RET_CHUNK, RET_CHUNK)

    def lanes(a):
        a = a.reshape(npair, 2, RET_CHUNK)
        return np.repeat(a.transpose(0, 2, 1), HEAD_DIM, axis=2)

    bd = np.kron(np.eye(2), np.ones((HEAD_DIM, HEAD_DIM)))
    cd = np.repeat(cdec.reshape(npair, 2), HEAD_DIM, axis=1)[:, :, None] * bd[None]
    f = lambda a: jnp.asarray(a, F32)
    return f(inner), f(lanes(cross)), f(lanes(kdec)), f(cd), f(bd)


def _retention(u3, gn_g, *, q_col, k_col, v_col):
    B, S, _ = u3.shape
    C = RET_CHUNK
    npair = RET_HEADS // 2
    inner, cross, kdec, cd, bd = _ret_constants()
    full = lambda shape: pl.BlockSpec(shape, lambda b, i: (0,) * len(shape))
    return pl.pallas_call(
        _ret_kernel,
        grid=(B, S // C),
        in_specs=[
            pl.BlockSpec((1, C, RET_WIDTH), lambda b, i: (b, i, q_col)),
            pl.BlockSpec((1, C, RET_WIDTH), lambda b, i: (b, i, k_col)),
            pl.BlockSpec((1, C, RET_WIDTH), lambda b, i: (b, i, v_col)),
            full((npair, 2, C, C)), full((npair, C, LANES)), full((npair, C, LANES)),
            full((npair, LANES, LANES)), full((LANES, LANES)), full((1, RET_WIDTH)),
        ],
        out_specs=pl.BlockSpec((1, C, RET_WIDTH), lambda b, i: (b, i, 0)),
        out_shape=jax.ShapeDtypeStruct((B, S, RET_WIDTH), BF16),
        scratch_shapes=[pltpu.VMEM((npair, LANES, LANES), F32)],
        compiler_params=_cp(("parallel", "arbitrary")),
        name="retention",
    )(u3, u3, u3, inner, cross, kdec, cd, bd, gn_g.reshape(1, RET_WIDTH))


def _out0_kernel(of_ref, or_ref, z_ref, x_ref, w_ref, o_ref):
    z = _silu(z_ref[...].astype(F32))
    ya = (of_ref[...].astype(F32) * z[:, :FOX_WIDTH]).astype(BF16)
    yb = (or_ref[...].astype(F32) * z[:, FOX_WIDTH:]).astype(BF16)
    o_ref[...] = x_ref[...] + _dot(ya, w_ref[:FOX_WIDTH, :]) + _dot(yb, w_ref[FOX_WIDTH:, :])


def _out0(o_f, o_r, u, x2, w_out, *, tm=OUT_TM):
    N, D = x2.shape
    return pl.pallas_call(
        _out0_kernel,
        grid=(N // tm,),
        in_specs=[
            pl.BlockSpec((tm, FOX_WIDTH), lambda i: (i, 0)),
            pl.BlockSpec((tm, RET_WIDTH), lambda i: (i, 0)),
            pl.BlockSpec((tm, D), lambda i: (i, 0)),
            pl.BlockSpec((tm, D), lambda i: (i, 0)),
            pl.BlockSpec((D, D), lambda i: (0, 0)),
        ],
        out_specs=pl.BlockSpec((tm, D), lambda i: (i, 0)),
        out_shape=jax.ShapeDtypeStruct((N, D), F32),
        compiler_params=_cp(("parallel",)),
        name="out0",
    )(o_f, o_r, u, x2, w_out)


def _out1_kernel(oc_ref, os_ref, z_ref, x_ref, w_ref, g_ref, o_ref):
    z = _silu(z_ref[...].astype(F32))
    y = ((oc_ref[...].astype(F32) + os_ref[...].astype(F32)) * z).astype(BF16)
    o_ref[...] = _rms(x_ref[...] + _dot(y, w_ref[...]), g_ref[...])


def _out1(o_c, o_s, u, x2, w_out, final_g, *, tm=OUT_TM):
    N, D = x2.shape
    row = pl.BlockSpec((tm, D), lambda i: (i, 0))
    return pl.pallas_call(
        _out1_kernel,
        grid=(N // tm,),
        in_specs=[row, row, row, row,
                  pl.BlockSpec((D, D), lambda i: (0, 0)),
                  pl.BlockSpec((1, D), lambda i: (0, 0))],
        out_specs=row,
        out_shape=jax.ShapeDtypeStruct((N, D), F32),
        compiler_params=_cp(("parallel",)),
        name="out1",
    )(o_c, o_s, u, x2, w_out, final_g.reshape(1, D))


def _compress_kernel(x_ref, pea_ref, peb_ref, wa_ref, wb_ref, w2_ref, *rest, transposed):
    x = x_ref[0].astype(F32)
    a = _dot((x + pea_ref[...]).astype(BF16), wa_ref[0])
    b = _dot((x + peb_ref[...]).astype(BF16), wb_ref[0])
    nseg = x.shape[0]
    pre = a + pltpu.roll(b, nseg - 1, 0)
    hid = _silu(pre).astype(BF16)
    if transposed:
        o_ref, = rest
        o_ref[0, 0] = _dot_nt(w2_ref[...], hid).astype(o_ref.dtype)
    else:
        feat_ref, o_ref = rest
        o_ref[0, 0] = (_dot(hid, w2_ref[...]) + feat_ref[...].astype(F32)).astype(o_ref.dtype)


def _compress(a3, pe, w1, w2, *, transposed):
    B, nseg, wid = a3.shape
    half = CMP_STRIDE * HEAD_DIM
    eye = jnp.eye(NSA_GROUPS, dtype=w1.dtype)

    def big(wh):
        w4 = wh.reshape(CMP_STRIDE, 1, HEAD_DIM, CMP_HIDDEN)
        sel = eye[:, None, :, None, None]
        return (sel * w4[None]).reshape(NSA_GROUPS, wid, CMP_HIDDEN).astype(BF16)

    def pe_big(p):
        return jnp.broadcast_to(p[:, None, :], (CMP_STRIDE, NSA_GROUPS, HEAD_DIM)).reshape(1, wid)

    args = [a3, pe_big(pe[:CMP_STRIDE]), pe_big(pe[CMP_STRIDE:]), big(w1[:half]), big(w1[half:])]
    in_specs = [
        pl.BlockSpec((1, nseg, wid), lambda b, g: (b, 0, 0)),
        pl.BlockSpec((1, wid), lambda b, g: (0, 0)),
        pl.BlockSpec((1, wid), lambda b, g: (0, 0)),
        pl.BlockSpec((1, wid, CMP_HIDDEN), lambda b, g: (g, 0, 0)),
        pl.BlockSpec((1, wid, CMP_HIDDEN), lambda b, g: (g, 0, 0)),
    ]
    if transposed:
        w2d = w2.T.astype(BF16)
        oshape, oblock = (B, NSA_GROUPS, HEAD_DIM, nseg), (1, 1, HEAD_DIM, nseg)
        args.append(w2d)
        in_specs.append(pl.BlockSpec(w2d.shape, lambda b, g: (0, 0)))
    else:
        w2d = jnp.pad(w2, ((0, 0), (0, LANES - HEAD_DIM))).astype(BF16)
        oshape, oblock = (B, NSA_GROUPS, nseg, LANES), (1, 1, nseg, LANES)
        feat = _pos_features(jnp.arange(nseg, dtype=jnp.int32) * CMP_STRIDE + (CMP_BLOCK - 1), LANES)
        args += [w2d, feat]
        in_specs += [pl.BlockSpec(w2d.shape, lambda b, g: (0, 0)),
                     pl.BlockSpec((nseg, LANES), lambda b, g: (0, 0))]
    return pl.pallas_call(
        functools.partial(_compress_kernel, transposed=transposed),
        grid=(B, NSA_GROUPS),
        in_specs=in_specs,
        out_specs=pl.BlockSpec(oblock, lambda b, g: (b, g, 0, 0)),
        out_shape=jax.ShapeDtypeStruct(oshape, BF16),
        compiler_params=_cp(("parallel", "parallel")),
        name="compress",
    )(*args)


def _slope_table():
    s = np.asarray(2.0 ** (-8.0 * (np.arange(NSA_HEADS) + 1) / NSA_HEADS), np.float32)
    sl = np.asarray(s.astype(np.float64) * LOG2E, np.float32)
    p1, p2, p3 = _np_pieces(sl)
    tab = np.zeros((NSA_HEADS, QF_ROWS), np.float32)
    for k, p in enumerate((p1, p1, p2, p2, p3, p3)):
        tab[:, k] = p
    tab[:, 6] = sl
    tab = np.broadcast_to(tab.reshape(NSA_GROUPS, NSA_HPG * QF_ROWS, 1),
                          (NSA_GROUPS, NSA_HPG * QF_ROWS, NSA_TQ))
    return jnp.asarray(tab)


def _pos_features(pos, width):
    pos = pos[:, None]
    lane = jnp.arange(width, dtype=jnp.int32)[None, :] % LANES
    hi = ((pos // SLC_BLOCK) * SLC_BLOCK).astype(F32)
    lo = (pos % SLC_BLOCK).astype(F32)
    k = lane - FEAT0
    f = jnp.where((k >= 0) & (k < 6), jnp.where(k % 2 == 0, hi, lo), 0.0)
    f = jnp.where((k >= 6) & (k < 9), 1.0, f)
    return f.astype(BF16)


def _nsa_queries(qt_ref, tab_ref, t0):
    tq = qt_ref.shape[2]
    r = lax.broadcasted_iota(jnp.int32, (QF_ROWS, tq), 0)
    t = (t0 + lax.broadcasted_iota(jnp.int32, (1, tq), 1)).astype(F32)
    zeros = jnp.zeros((LANES - HEAD_DIM - QF_ROWS, tq), BF16)
    out = []
    for i in range(NSA_HPG):
        tile = tab_ref[0, i * QF_ROWS:(i + 1) * QF_ROWS, :]
        a1, a2, a3 = _pieces(-(tile[6:7, :] * t))
        feat = jnp.where(r == 6, a1, jnp.where(r == 7, a2, jnp.where(r == 8, a3,
                                                                     jnp.where(r < 6, tile, 0.0))))
        out.append(jnp.concatenate([qt_ref[0, i * HEAD_DIM:(i + 1) * HEAD_DIM, :],
                                    feat.astype(BF16), zeros], axis=0))
    return out


def _gates_t(gt_ref, bg_ref, branch):
    gl = gt_ref[0] + bg_ref[...]
    return [_sigmoid(gl[N_BRANCH * i + branch:N_BRANCH * i + branch + 1, :]) for i in range(NSA_HPG)]


def _gated(outs_t, gates):
    return [o * gt for o, gt in zip(outs_t, gates)]


def _store_heads(o_ref, g):
    o_ref[0, :, :LANES] = jnp.concatenate(g[:2], axis=0).T.astype(o_ref.dtype)
    o_ref[0, :, LANES:] = jnp.concatenate(g[2:], axis=0).T.astype(o_ref.dtype)


def _nsa_specs(q_row):
    return dict(
        q=pl.BlockSpec((1, NSA_HPG * HEAD_DIM, NSA_TQ), lambda b, g, i: (b, q_row + g, i)),
        tab=pl.BlockSpec((1, NSA_HPG * QF_ROWS, NSA_TQ), lambda b, g, i: (g, 0, 0)),
        gt=pl.BlockSpec((1, GATE_ROWS, NSA_TQ), lambda b, g, i: (b, g, i)),
        bg=pl.BlockSpec((GATE_ROWS, 1), lambda b, g, i: (g, 0)),
        out=pl.BlockSpec((1, NSA_TQ, NSA_HPG * HEAD_DIM), lambda b, g, i: (b, i, g)),
    )


def _argmax_first(v, idx):
    n = v.shape[0]
    slabs = [(v[r:r + 8], idx[r:r + 8]) for r in range(0, n, 8)]
    while len(slabs) > 1:
        nxt = []
        for (va, ia), (vb, ib) in zip(slabs[0::2], slabs[1::2]):
            right = vb > va
            nxt.append((jnp.where(right, vb, va), jnp.where(right, ib, ia)))
        if len(slabs) % 2:
            nxt.append(slabs[-1])
        slabs = nxt
    v, idx = slabs[0]
    mx = jnp.max(v, axis=0, keepdims=True)
    first = jnp.min(jnp.where(v == mx, idx, float(n)), axis=0, keepdims=True)
    return mx, first


def _cmp_body(nc, t0, qh, kc_ref, vct_ref, mt_ref, grp_ref, gates, o_ref, sel_ref, flag_ref):
    tq = NSA_TQ
    rows = nc * CMP_CHUNK
    full = max(rows - CMP_CHUNK - 8, 0)
    nseg = kc_ref.shape[2]
    kc = kc_ref[0, 0, :rows, :]
    vct = vct_ref[0, 0, :, :rows]
    t = t0 + lax.broadcasted_iota(jnp.int32, (1, tq), 1)
    cidx = full + lax.broadcasted_iota(jnp.int32, (rows - full, 1), 0)
    valid = (cidx * CMP_STRIDE + (CMP_BLOCK - 1) <= t) & (cidx < nseg - 1)
    psum = jnp.zeros((rows, tq), F32)
    ps = []
    s_all = _dot(kc, jnp.concatenate(qh, axis=1))
    for i in range(NSA_HPG):
        s = s_all[:, i * tq:(i + 1) * tq]
        s_last = jnp.where(valid, s[full:], NEG)
        m = jnp.max(s_last, axis=0, keepdims=True)
        if nc > 1:
            m = jnp.maximum(m, jnp.max(s[:full], axis=0, keepdims=True))
        e = jnp.where(valid, jnp.exp2(s_last - m), 0.0)
        if nc > 1:
            e = jnp.concatenate([jnp.exp2(s[:full] - m), e], axis=0)
        l = jnp.sum(e, axis=0, keepdims=True)
        p = e * jnp.where(l > 0.0, 1.0 / l, 0.0)
        psum = psum + p
        ps.append(p.astype(BF16))
    o_all = _dot(vct, jnp.concatenate(ps, axis=1))
    _store_heads(o_ref, _gated([o_all[:, i * tq:(i + 1) * tq] for i in range(NSA_HPG)], gates))
    ns = rows * CMP_STRIDE // SLC_BLOCK
    mt = mt_ref[:ns, :rows]
    imp = sum(_dot(mt, p.astype(BF16)) for p in _pieces(psum))
    blk = lax.broadcasted_iota(jnp.int32, (ns, 1), 0)
    cur = t // SLC_BLOCK
    bvalid = blk * SLC_BLOCK <= t
    forced = (blk == 0) | (blk == cur) | (blk == cur - 1)
    score = jnp.where(forced, -jnp.inf, jnp.where(bvalid, imp, NEG))
    blk_f = jnp.broadcast_to(blk.astype(F32), (ns, tq))
    work = score
    for _ in range(SLC_TOPK - N_FORCED):
        mx, first = _argmax_first(work, blk_f)
        work = jnp.where(blk_f == first, -jnp.inf, work)
    picked = (score > mx) | ((score == mx) & (blk_f <= first))
    selneg = jnp.where(bvalid & (forced | picked), 0.0, -MASK_BIG)
    if ns < NS_PAD:
        selneg = jnp.concatenate([selneg, jnp.full((NS_PAD - ns, tq), -MASK_BIG, F32)], axis=0)
    sel_ref[0, 0] = selneg.astype(sel_ref.dtype)
    picked = jnp.where(selneg == 0.0, 1.0, 0.0).astype(BF16)
    used = _dot_nt(jnp.ones((8, tq), BF16), picked)
    used = jnp.where(used > 0.0, 1.0, 0.0).astype(BF16)
    flag_ref[0] = (_dot(used, grp_ref[...])[0:1] > 0.0).astype(jnp.int32)


def _cmp_kernel(q_ref, kc_ref, vct_ref, mt_ref, grp_ref, tab_ref, gt_ref, bg_ref,
                o_ref, sel_ref, flag_ref):
    t0 = pl.program_id(2) * NSA_TQ
    qh = _nsa_queries(q_ref, tab_ref, t0)
    gates = _gates_t(gt_ref, bg_ref, 0)
    nchunk = kc_ref.shape[2] // CMP_CHUNK
    last = t0 // (CMP_CHUNK * CMP_STRIDE)
    for nc in range(1, nchunk + 1):
        pl.when(last == nc - 1)(functools.partial(
            _cmp_body, nc, t0, qh, kc_ref, vct_ref, mt_ref, grp_ref, gates, o_ref, sel_ref, flag_ref))


def _cmp_to_slc_t(nseg, ns):
    c0 = np.arange(nseg)[:, None] * CMP_STRIDE
    s0 = np.arange(ns)[None, :] * SLC_BLOCK
    overlap = np.clip(np.minimum(c0 + CMP_BLOCK, s0 + SLC_BLOCK) - np.maximum(c0, s0), 0, None)
    m = overlap / CMP_STRIDE
    m[nseg - 1] = 0.0
    mt = np.zeros((NS_PAD, nseg))
    mt[:ns] = m.T
    return jnp.asarray(mt, BF16)


def _tile_groups():
    per = SLC_TK // SLC_BLOCK
    g = (np.arange(NS_PAD)[:, None] // per) == np.arange(NS_PAD)[None, :]
    return jnp.asarray(g, BF16)


def _cmp_attention(ut, kcmp, vcmp_t, gt, bg, *, q_row):
    B, _, S = ut.shape
    nseg = kcmp.shape[2]
    nq = S // NSA_TQ
    sp = _nsa_specs(q_row)
    return pl.pallas_call(
        _cmp_kernel,
        grid=(B, NSA_GROUPS, S // NSA_TQ),
        in_specs=[
            sp["q"],
            pl.BlockSpec((1, 1, nseg, LANES), lambda b, g, i: (b, g, 0, 0)),
            pl.BlockSpec((1, 1, HEAD_DIM, nseg), lambda b, g, i: (b, g, 0, 0)),
            pl.BlockSpec((NS_PAD, nseg), lambda b, g, i: (0, 0)),
            pl.BlockSpec((NS_PAD, NS_PAD), lambda b, g, i: (0, 0)),
            sp["tab"], sp["gt"], sp["bg"],
        ],
        out_specs=[sp["out"], pl.BlockSpec((1, 1, NS_PAD, NSA_TQ), lambda b, g, i: (b, g, 0, i)),
                   pl.BlockSpec((1, 1, NS_PAD), lambda b, g, i: ((b * NSA_GROUPS + g) * nq + i, 0, 0))],
        out_shape=[jax.ShapeDtypeStruct((B, S, NSA_WIDTH), BF16),
                   jax.ShapeDtypeStruct((B, NSA_GROUPS, NS_PAD, S), BF16),
                   jax.ShapeDtypeStruct((B * NSA_GROUPS * nq, 1, NS_PAD), jnp.int32)],
        compiler_params=_cp(("parallel", "parallel", "arbitrary")),
        name="cmp_attention",
    )(ut, kcmp, vcmp_t, _cmp_to_slc_t(nseg, S // SLC_BLOCK), _tile_groups(), _slope_table(), gt, bg)


def _window_branch(qh, t0, k_ref, vt_ref, bias_ref, bm_sc, s_bufs, acc_sc, last_before):
    tq, tk = NSA_TQ, WIN_TK
    bm_sc[...] = jnp.concatenate(qh, axis=1)
    _, qk_all, step, finish = _flash_ops(
        bm_sc, s_bufs, None, acc_sc, NSA_HPG, tq, tk,
        lambda k0: k_ref[0, pl.ds(k0, tk), :], lambda h, k0: vt_ref[0, :, pl.ds(k0, tk)])
    first = jnp.maximum(t0 - WINDOW, 0) // tk
    ntile = (WINDOW + tq) // tk
    qk_all(first, 0)
    ms = (jnp.full((1, tq), NEG, F32),) * NSA_HPG
    for j in range(ntile):
        last = j + 1 == ntile
        ms = step(first + j, j % 2, ms, nxt=None if last else first + j + 1,
                  bias=bias_ref[0, j * tk:(j + 1) * tk, :], before=last_before if last else None)
    return finish()


def _slc_win_kernel(fl_ref, q_ref, k_ref, vt_ref, sel_ref, e_ref, kw_ref, vwt_ref, bias_ref,
                    tab_ref, gt_ref, bg_ref, o_ref, bm_sc, s0_sc, s1_sc, mx_sc, acc_sc,
                    wbm_sc, ws0_sc, ws1_sc, wacc_sc, tiles_sm):
    qi = pl.program_id(2)
    tq, tk = NSA_TQ, SLC_TK
    t0 = qi * tq
    diag = t0 // tk
    row = (pl.program_id(0) * NSA_GROUPS + pl.program_id(1)) * pl.num_programs(2) + qi
    n_un = jnp.int32(0)
    for j in range(k_ref.shape[1] // tk):
        tiles_sm[n_un] = jnp.int32(j)
        n_un = n_un + ((fl_ref[row, j] > 0) & (j < diag)).astype(jnp.int32)
    tiles_sm[n_un] = diag
    qh = _nsa_queries(q_ref, tab_ref, t0)
    selneg = sel_ref[0, 0]
    bm_sc[...] = jnp.concatenate([jnp.concatenate([q, selneg], axis=0) for q in qh], axis=1)
    qpos = t0 + lax.broadcasted_iota(jnp.int32, (tk, tq), 1)

    def k_tile(k0):
        return jnp.concatenate([k_ref[0, pl.ds(k0, tk), :], e_ref[pl.ds(k0, tk), :]], axis=1)

    def v_rows(h, k0):
        return vt_ref[0, :, pl.ds(k0, tk)]

    ops = _flash_ops(bm_sc, (s0_sc, s1_sc), mx_sc, acc_sc, NSA_HPG, tq, tk, k_tile, v_rows)
    o_win = _gated(_window_branch(qh, t0, kw_ref, vwt_ref, bias_ref, wbm_sc, (ws0_sc, ws1_sc),
                                  wacc_sc, lambda h: ops[0](tiles_sm[0], 0, h)),
                   _gates_t(gt_ref, bg_ref, 2))
    outs = _flash_t(ops, NSA_HPG, tq, tk, n_un, lambda i: tiles_sm[i], diag,
                    lambda key: key <= qpos, first_done=True)
    o_slc = _gated(outs, _gates_t(gt_ref, bg_ref, 1))
    _store_heads(o_ref, [a + b for a, b in zip(o_slc, o_win)])


def _win_bias():
    tq, nk = NSA_TQ, WINDOW + NSA_TQ
    out = []
    for p in range(WINDOW // tq + 1):
        t0 = p * tq
        key = max(t0 - WINDOW, 0) + np.arange(nk)[:, None]
        qpos = t0 + np.arange(tq)[None, :]
        out.append(np.where((key <= qpos) & (key > qpos - WINDOW), 0.0, NEG))
    return jnp.asarray(np.stack(out), F32)


def _block_onehot(S):
    e = (np.arange(S)[:, None] // SLC_BLOCK) == np.arange(NS_PAD)[None, :]
    return jnp.asarray(e, BF16)


def _slc_win_attention(u3, ut, selneg, flags, gt, bg, *, q_row, ks_col, vs_row, kw_col, vw_row):
    B, S, _ = u3.shape
    sp = {k: pl.BlockSpec(v.block_shape, lambda b, g, i, fl, f=v.index_map: f(b, g, i))
          for k, v in _nsa_specs(q_row).items()}
    bias = _win_bias()
    npat = bias.shape[0]
    kspec = lambda col: pl.BlockSpec((1, S, LANES), lambda b, g, i, fl: (b, 0, col + g))
    vspec = lambda row: pl.BlockSpec((1, HEAD_DIM, S), lambda b, g, i, fl: (b, row + g, 0))
    grid_spec = pltpu.PrefetchScalarGridSpec(
        num_scalar_prefetch=1,
        grid=(B, NSA_GROUPS, S // NSA_TQ),
        in_specs=[
            sp["q"], kspec(ks_col), vspec(vs_row),
            pl.BlockSpec((1, 1, NS_PAD, NSA_TQ), lambda b, g, i, fl: (b, g, 0, i)),
            pl.BlockSpec((S, NS_PAD), lambda b, g, i, fl: (0, 0)),
            kspec(kw_col), vspec(vw_row),
            pl.BlockSpec((1,) + bias.shape[1:],
                         lambda b, g, i, fl: (jnp.minimum(i, npat - 1), 0, 0)),
            sp["tab"], sp["gt"], sp["bg"],
        ],
        out_specs=sp["out"],
        scratch_shapes=_flash_scratch(NSA_HPG, NSA_TQ, SLC_TK, 2 * LANES)
        + _flash_scratch(NSA_HPG, NSA_TQ, WIN_TK, LANES, col_max=False)
        + [pltpu.SMEM((S // SLC_TK + 1,), jnp.int32)],
    )
    return pl.pallas_call(
        _slc_win_kernel,
        grid_spec=grid_spec,
        out_shape=jax.ShapeDtypeStruct((B, S, NSA_WIDTH), BF16),
        compiler_params=_cp(("parallel", "parallel", "arbitrary")),
        name="slc_win_attention",
    )(flags, ut, u3, ut, selneg, _block_onehot(S), u3, ut, bias, _slope_table(), gt, bg)


def _aug_groups(w):
    d = w.shape[0]
    w = w.reshape(d, NSA_GROUPS, HEAD_DIM)
    return jnp.pad(w, ((0, 0), (0, 0), (0, LANES - HEAD_DIM))).reshape(d, NSA_GROUPS * LANES)


def _even_layer(x, norm_g, w_in, b_f, gn_g, w_out):
    B, S, D = x.shape
    qscale = HEAD_DIM ** -0.5 * LOG2E
    q_f, k_f, v_f, w_fl, q_r, k_r, v_r, z = jnp.split(
        w_in, np.cumsum([FOX_WIDTH] * 3 + [FOX_HEADS] + [RET_WIDTH] * 3).tolist(), axis=1)
    w = jnp.concatenate([z, k_f, q_r, k_r * HEAD_DIM ** -0.5, v_r], axis=1).astype(BF16)
    w_t = jnp.concatenate([q_f * qscale, v_f], axis=1).T.astype(BF16)
    x2 = x.reshape(B * S, D)
    u, ut = _proj(x2, norm_g, w, seq=S, w_t=[w_t], t_dtypes=[BF16])
    u3 = u.reshape(B, S, -1)
    kfeat, qfeat = _fgate(x, norm_g, w_fl, b_f, tile=min(512, S))
    o_f = _fox(u3, ut, kfeat, qfeat, q_row=0, k_col=D // LANES, v_row=FOX_WIDTH // LANES)
    rb = (D + FOX_WIDTH) // RET_WIDTH
    o_r = _retention(u3, gn_g, q_col=rb, k_col=rb + 1, v_col=rb + 2)
    out = _out0(o_f.reshape(B * S, -1), o_r.reshape(B * S, -1), u, x2, w_out.astype(BF16))
    return out.reshape(B, S, D)


def _odd_layer(x, norm_g, w_in, b_gate, pe_k, pe_v, wk1, wk2, wv1, wv2, w_out, final_g):
    B, S, D = x.shape
    assert S // SLC_BLOCK <= NS_PAD
    qscale = HEAD_DIM ** -0.5 * LOG2E
    sizes = [NSA_WIDTH] + [NSA_KV_WIDTH] * 6 + [NSA_HEADS * N_BRANCH]
    q, kc, vc, ks, vs, kw, vw, gl, z = jnp.split(w_in, np.cumsum(sizes).tolist(), axis=1)
    w = jnp.concatenate([z, kc, vc, _aug_groups(ks), _aug_groups(kw)], axis=1).astype(BF16)
    per_group = NSA_HPG * N_BRANCH
    glt = jnp.pad(gl.T.reshape(NSA_GROUPS, per_group, D), ((0, 0), (0, GATE_ROWS - per_group), (0, 0)))
    glt = glt.reshape(NSA_GROUPS * GATE_ROWS, D).astype(BF16)
    bg = jnp.pad(b_gate.reshape(NSA_GROUPS, per_group), ((0, 0), (0, GATE_ROWS - per_group)))
    bg = bg.reshape(NSA_GROUPS * GATE_ROWS, 1)
    w_vt = jnp.concatenate([q * qscale, vs, vw], axis=1).T.astype(BF16)
    x2 = x.reshape(B * S, D)
    first_k = (D + 2 * NSA_KV_WIDTH) // PROJ_TN
    u, kc_a, vc_a, ut, gt = _proj(
        x2, norm_g, w, seq=S, split_tile=D // PROJ_TN,
        addend=_pos_features(jnp.arange(S, dtype=jnp.int32), PROJ_TN), add_tiles=(first_k, first_k + 1),
        w_t=[w_vt, glt], t_dtypes=[BF16, F32])
    u3 = u.reshape(B, S, -1)
    nseg = S // CMP_STRIDE
    kcmp = _compress(kc_a.reshape(B, nseg, -1), pe_k, wk1, wk2, transposed=False)
    vcmp_t = _compress(vc_a.reshape(B, nseg, -1), pe_v, wv1, wv2, transposed=True)
    o_c, selneg, flags = _cmp_attention(ut, kcmp, vcmp_t, gt, bg, q_row=0)
    kb = (D + 2 * NSA_KV_WIDTH) // LANES
    vb = NSA_WIDTH // HEAD_DIM
    o_s = _slc_win_attention(u3, ut, selneg, flags[:, 0, :S // SLC_TK], gt, bg, q_row=0, ks_col=kb,
                             vs_row=vb, kw_col=kb + NSA_GROUPS, vw_row=vb + NSA_GROUPS)
    r = lambda a: a.reshape(B * S, -1)
    out = _out1(r(o_c), r(o_s), u, x2, w_out.astype(BF16), final_g)
    return out.reshape(B, S, D)


def kernel(x, even_norm_g, even_w_in, even_b_f, even_gn_g, even_w_out, odd_norm_g, odd_w_in,
           odd_b_gate, odd_pe_k, odd_pe_v, odd_wk1, odd_wk2, odd_wv1, odd_wv2, odd_w_out, final_g):
    x = _even_layer(x, even_norm_g[0], even_w_in[0], even_b_f[0], even_gn_g[0], even_w_out[0])
    return _odd_layer(x, odd_norm_g[0], odd_w_in[0], odd_b_gate[0], odd_pe_k[0], odd_pe_v[0],
                      odd_wk1[0], odd_wk2[0], odd_wv1[0], odd_wv2[0], odd_w_out[0], final_g)
```

```python
import functools
import math

import jax
import jax.numpy as jnp
import numpy as np
from jax import lax
from jax.experimental import pallas as pl
from jax.experimental.pallas import tpu as pltpu

D_MODEL = 1024
HEAD_DIM = 64
LANES = 128
FOX_HEADS = 8
RET_HEADS = 8
FOX_WIDTH = FOX_HEADS * HEAD_DIM
RET_WIDTH = RET_HEADS * HEAD_DIM
RET_CHUNK = 128
RET_STEP = 4
NSA_HEADS = 16
NSA_GROUPS = 4
NSA_HPG = NSA_HEADS // NSA_GROUPS
NSA_WIDTH = NSA_HEADS * HEAD_DIM
NSA_KV_WIDTH = NSA_GROUPS * HEAD_DIM
N_BRANCH = 3
GATE_ROWS = 16
CMP_BLOCK = 32
CMP_STRIDE = 16
CMP_HIDDEN = 256
CMP_CHUNK = 128
SLC_BLOCK = 64
SLC_TOPK = 16
N_FORCED = 3
NS_PAD = LANES
WINDOW = 512
RMS_EPS = 1e-6
GN_EPS = 1e-5
NEG = -1e30
FORCE_BONUS = 1e6
MASK_BIG = 2.0 ** 100
LOG2E = math.log2(math.e)
FEAT0 = HEAD_DIM
QF_ROWS = 16
ACC_ROWS = HEAD_DIM + 16

PROJ_TM = 1024
PROJ_TN = 512
FOX_TQ = 512
FOX_TK = 512
NSA_TQ = 256
SLC_TK = 512
WIN_TK = 256
OUT_TM = 1024
VMEM_LIMIT = 48 * 1024 * 1024

F32 = jnp.float32
BF16 = jnp.bfloat16


def _cp(sem, vmem=VMEM_LIMIT):
    return pltpu.CompilerParams(dimension_semantics=sem, vmem_limit_bytes=vmem)


def _dot(a, b):
    return jnp.dot(a, b, preferred_element_type=F32)


def _dot_nt(a, b):
    return lax.dot_general(a, b, (((1,), (1,)), ((), ())), preferred_element_type=F32)


def _dot_tn(a, b):
    return lax.dot_general(a, b, (((0,), (0,)), ((), ())), preferred_element_type=F32)


def _rms(x, g):
    return x * lax.rsqrt(jnp.mean(x * x, axis=-1, keepdims=True) + RMS_EPS) * g


def _silu(x):
    return x * (1.0 / (1.0 + jnp.exp(-x)))


def _sigmoid(x):
    return 1.0 / (1.0 + jnp.exp(-x))


def _low_half(shape, axis):
    return lax.broadcasted_iota(jnp.int32, shape, axis) < HEAD_DIM


def _pieces(v):
    p1 = v.astype(BF16).astype(F32)
    r = v - p1
    p2 = r.astype(BF16).astype(F32)
    p3 = (r - p2).astype(BF16).astype(F32)
    return p1, p2, p3


def _np_pieces(v):
    v = np.asarray(v, np.float64)
    bf = lambda a: np.asarray(a, np.float32).astype(BF16).astype(np.float64)
    p1 = bf(v)
    p2 = bf(v - p1)
    p3 = bf(v - p1 - p2)
    return p1, p2, p3


def _place(lane, cols):
    out = jnp.zeros(lane.shape, F32)
    for i, c in cols.items():
        out = jnp.where(lane == i, c, out)
    return out


def _fgate_kernel(x_ref, g_ref, wf_ref, b_ref, kf_ref, qf_ref, carry):
    @pl.when(pl.program_id(1) == 0)
    def _():
        carry[...] = jnp.zeros_like(carry)

    h = _rms(x_ref[0], g_ref[...])
    t = h.shape[0]
    h1 = h.astype(BF16)
    h2 = (h - h1.astype(F32)).astype(BF16)
    w = wf_ref[...]
    w1 = w.astype(BF16)
    w2 = (w - w1.astype(F32)).astype(BF16)
    nf = w.shape[1]
    r = _dot(jnp.concatenate([h1, h2], axis=0), jnp.concatenate([w1, w2], axis=1))
    f = r[:t, :nf] + r[:t, nf:] + r[t:, :nf] + b_ref[...]
    ls = jnp.minimum(f, 0.0) - jnp.log(1.0 + jnp.exp(-jnp.abs(f)))
    r = lax.broadcasted_iota(jnp.int32, (t, t), 0)
    c = lax.broadcasted_iota(jnp.int32, (t, t), 1)
    lower = jnp.where(c <= r, 1.0, 0.0).astype(BF16)
    r = _dot(lower, jnp.concatenate([p.astype(BF16) for p in _pieces(ls)], axis=1))
    cs = r[:, :nf] + r[:, nf:2 * nf] + r[:, 2 * nf:] + carry[...]
    carry[...] = cs[t - 1:t, :]
    cl = cs * LOG2E
    lane = lax.broadcasted_iota(jnp.int32, (t, LANES), 1)
    one = jnp.ones((t, 1), F32)
    for j in range(FOX_HEADS // 2):
        a1, a2, a3 = _pieces(cl[:, 2 * j:2 * j + 1])
        b1, b2, b3 = _pieces(cl[:, 2 * j + 1:2 * j + 2])
        kf = _place(lane, {0: a1, 1: a2, 2: a3, 3: one, 4: one, 5: one, 6: b1, 7: b2, 8: b3})
        kf_ref[0, j] = kf.astype(BF16)
        qa = _place(lane, {0: -one, 1: -one, 2: -one, 3: a1, 4: a2, 5: a3})
        qb = _place(lane, {3: b1, 4: b2, 5: b3, 6: -one, 7: -one, 8: -one})
        qf_ref[0, j, 0] = qa.T[:QF_ROWS].astype(BF16)
        qf_ref[0, j, 1] = qb.T[:QF_ROWS].astype(BF16)


def _fgate(x, g, wf, b_f, *, tile):
    B, S, D = x.shape
    npair = FOX_HEADS // 2
    return pl.pallas_call(
        _fgate_kernel,
        grid=(B, S // tile),
        in_specs=[
            pl.BlockSpec((1, tile, D), lambda b, s: (b, s, 0)),
            pl.BlockSpec((1, D), lambda b, s: (0, 0)),
            pl.BlockSpec((D, FOX_HEADS), lambda b, s: (0, 0)),
            pl.BlockSpec((1, FOX_HEADS), lambda b, s: (0, 0)),
        ],
        out_specs=[
            pl.BlockSpec((1, npair, tile, LANES), lambda b, s: (b, 0, s, 0)),
            pl.BlockSpec((1, npair, 2, QF_ROWS, tile), lambda b, s: (b, 0, 0, 0, s)),
        ],
        out_shape=[
            jax.ShapeDtypeStruct((B, npair, S, LANES), BF16),
            jax.ShapeDtypeStruct((B, npair, 2, QF_ROWS, S), BF16),
        ],
        scratch_shapes=[pltpu.VMEM((1, FOX_HEADS), F32)],
        compiler_params=_cp(("parallel", "arbitrary")),
        name="fgate",
    )(x, g.reshape(1, D), wf, b_f.reshape(1, FOX_HEADS))


def _proj_kernel(*refs, split_tile, add_tiles, n_t):
    it = iter(refs)
    x_ref, g_ref, w_ref = next(it), next(it), next(it)
    add_ref = next(it) if add_tiles else None
    wt_refs = [next(it) for _ in range(n_t)]
    u_ref = next(it)
    e_refs = [next(it), next(it)] if split_tile is not None else None
    ut_refs = [next(it) for _ in range(n_t)]
    h_sc = next(it)
    j = pl.program_id(1)

    @pl.when(j == 0)
    def _():
        h = _rms(x_ref[...], g_ref[...]).astype(BF16)
        h_sc[...] = h
        for wt_ref, ut_ref in zip(wt_refs, ut_refs):
            ut_ref[0] = _dot_nt(wt_ref[...], h).astype(ut_ref.dtype)

    acc = _dot(h_sc[...], w_ref[...])
    if add_tiles:
        is_add = functools.reduce(jnp.logical_or, [j == t for t in add_tiles])

        @pl.when(is_add)
        def _():
            u_ref[...] = (acc + add_ref[...].astype(F32)).astype(u_ref.dtype)

        @pl.when(jnp.logical_not(is_add))
        def _():
            u_ref[...] = acc.astype(u_ref.dtype)
    else:
        u_ref[...] = acc.astype(u_ref.dtype)
    if split_tile is not None:
        @pl.when(j == split_tile)
        def _():
            half = acc.shape[1] // 2
            e_refs[0][...] = acc[:, :half].astype(e_refs[0].dtype)
            e_refs[1][...] = acc[:, half:].astype(e_refs[1].dtype)


def _proj(x2, g, w, *, seq, split_tile=None, addend=None, add_tiles=(), w_t=(), t_dtypes=(),
          tm=PROJ_TM, tn=PROJ_TN):
    N, D = x2.shape
    W = w.shape[1]
    nbs = seq // tm
    in_specs = [
        pl.BlockSpec((tm, D), lambda i, j: (i, 0)),
        pl.BlockSpec((1, D), lambda i, j: (0, 0)),
        pl.BlockSpec((D, tn), lambda i, j: (0, j)),
    ]
    args = [x2, g.reshape(1, D), w]
    if add_tiles:
        in_specs.append(pl.BlockSpec((tm, tn), lambda i, j: (i % nbs, 0)))
        args.append(addend)
    out_shape = [jax.ShapeDtypeStruct((N, W), BF16)]
    out_specs = [pl.BlockSpec((tm, tn), lambda i, j: (i, j))]
    if split_tile is not None:
        out_shape += [jax.ShapeDtypeStruct((N, tn // 2), BF16)] * 2
        out_specs += [pl.BlockSpec((tm, tn // 2), lambda i, j: (i, 0))] * 2
    for wt, dt in zip(w_t, t_dtypes):
        rows = wt.shape[0]
        in_specs.append(pl.BlockSpec((rows, D), lambda i, j: (0, 0)))
        args.append(wt)
        out_shape.append(jax.ShapeDtypeStruct((N // seq, rows, seq), dt))
        out_specs.append(pl.BlockSpec((1, rows, tm), lambda i, j: (i // nbs, 0, i % nbs)))
    return pl.pallas_call(
        functools.partial(_proj_kernel, split_tile=split_tile, add_tiles=tuple(add_tiles),
                          n_t=len(w_t)),
        grid=(N // tm, W // tn),
        in_specs=in_specs,
        out_specs=out_specs,
        out_shape=out_shape,
        scratch_shapes=[pltpu.VMEM((tm, D), BF16)],
        compiler_params=_cp(("parallel", "arbitrary")),
        name="proj",
    )(*args)


def _flash_ops(bm_sc, s_bufs, mx_sc, acc_sc, nh, tq, tk, k_tile, v_rows):
    acc_sc[...] = jnp.zeros_like(acc_sc)
    ones = jnp.ones((ACC_ROWS - HEAD_DIM, tk), BF16)

    def qk_head(tile, slot, h):
        cols = slice(h * tq, (h + 1) * tq)
        s = _dot(k_tile(pl.multiple_of(tile * tk, tk)), bm_sc[:, cols])
        s_bufs[slot][:, cols] = s
        if mx_sc is not None:
            mx_sc[slot, :, cols] = jnp.max(s, axis=0, keepdims=True)

    def soft_head(tile, slot, m_old, h, valid, bias):
        cols = slice(h * tq, (h + 1) * tq)
        buf = s_bufs[slot]
        k0 = pl.multiple_of(tile * tk, tk)
        if valid is not None or bias is not None:
            s = buf[:, cols] + bias if valid is None else jnp.where(valid, buf[:, cols], NEG)
            m_new = jnp.maximum(m_old, jnp.max(s, axis=0, keepdims=True))
            p = jnp.exp2(s - m_new).astype(BF16)
        else:
            m_new = jnp.maximum(m_old, mx_sc[slot, :, cols])
            p = jnp.exp2(buf[:, cols] - m_new).astype(BF16)
        alpha = jnp.exp2(m_old - m_new)
        lhs = jnp.concatenate([v_rows(h, k0), ones], axis=0)
        acc_sc[h] = alpha * acc_sc[h] + _dot(lhs, p)
        return m_new

    def qk_all(tile, slot):
        for h in range(nh):
            qk_head(tile, slot, h)

    def step(cur, slot, ms, nxt=None, valid=None, bias=None, before=None):
        out = []
        for h in range(nh):
            if nxt is not None:
                qk_head(nxt, 1 - slot, h)
            if before is not None:
                before(h)
            out.append(soft_head(cur, slot, ms[h], h, valid, bias))
        return tuple(out)

    def finish():
        outs = []
        for h in range(nh):
            a = acc_sc[h]
            outs.append(a[:HEAD_DIM] * (1.0 / a[HEAD_DIM:HEAD_DIM + 1]))
        return outs

    return qk_head, qk_all, step, finish


def _flash_t(ops, nh, tq, tk, n_un, tile_of, diag, valid_fn, first_done=False):
    _, qk_all, step, finish = ops
    key_iota = lax.broadcasted_iota(jnp.int32, (tk, tq), 0)
    diag_valid = lambda: valid_fn(pl.multiple_of(diag * tk, tk) + key_iota)

    def pair(i, ms):
        t_a, t_b, t_c = tile_of(2 * i), tile_of(2 * i + 1), tile_of(2 * i + 2)
        return step(t_b, 1, step(t_a, 0, ms, nxt=t_b), nxt=t_c)

    def odd_tail(ms):
        return step(diag, 1, step(tile_of(n_un - 1), 0, ms, nxt=diag), valid=diag_valid())

    def even_tail(ms):
        return step(diag, 0, ms, valid=diag_valid())

    if not first_done:
        qk_all(tile_of(0), 0)
    ms = (jnp.full((1, tq), NEG, F32),) * nh
    ms = lax.fori_loop(0, n_un // 2, pair, ms)
    lax.cond(n_un % 2 == 1, odd_tail, even_tail, ms)
    return finish()


def _flash_scratch(nh, tq, tk, kdim, col_max=True):
    bufs = [pltpu.VMEM((kdim, nh * tq), BF16), pltpu.VMEM((tk, nh * tq), F32),
            pltpu.VMEM((tk, nh * tq), F32)]
    if col_max:
        bufs.append(pltpu.VMEM((2, 1, nh * tq), F32))
    return bufs + [pltpu.VMEM((nh, ACC_ROWS, tq), F32)]


def _fox_kernel(qt_ref, qf_ref, k_ref, kf_ref, vt_ref, o_ref, bm_sc, s0_sc, s1_sc, mx_sc, acc_sc):
    qi = pl.program_id(2)
    tq, tk = FOX_TQ, FOX_TK
    bm_sc[...] = jnp.zeros_like(bm_sc)
    for h in range(2):
        rows = slice(h * HEAD_DIM, (h + 1) * HEAD_DIM)
        bm_sc[rows, h * tq:(h + 1) * tq] = qt_ref[0, rows, :]
        bm_sc[LANES:LANES + QF_ROWS, h * tq:(h + 1) * tq] = qf_ref[0, 0, h]
    qpos = qi * tq + lax.broadcasted_iota(jnp.int32, (tk, tq), 1)

    def k_tile(k0):
        return jnp.concatenate([k_ref[0, pl.ds(k0, tk), :], kf_ref[0, 0, pl.ds(k0, tk), :]], axis=1)

    def v_rows(h, k0):
        return vt_ref[0, h * HEAD_DIM:(h + 1) * HEAD_DIM, pl.ds(k0, tk)]

    diag = (qi * tq) // tk
    ops = _flash_ops(bm_sc, (s0_sc, s1_sc), mx_sc, acc_sc, 2, tq, tk, k_tile, v_rows)
    outs = _flash_t(ops, 2, tq, tk, diag, lambda i: i, diag, lambda key: key <= qpos)
    o_ref[0] = jnp.concatenate(outs, axis=0).T.astype(o_ref.dtype)


def _fox(u3, ut, kfeat, qfeat, *, q_row, k_col, v_row):
    B, S, _ = u3.shape
    npair = FOX_HEADS // 2
    return pl.pallas_call(
        _fox_kernel,
        grid=(B, npair, S // FOX_TQ),
        in_specs=[
            pl.BlockSpec((1, LANES, FOX_TQ), lambda b, j, i: (b, q_row + j, i)),
            pl.BlockSpec((1, 1, 2, QF_ROWS, FOX_TQ), lambda b, j, i: (b, j, 0, 0, i)),
            pl.BlockSpec((1, S, LANES), lambda b, j, i: (b, 0, k_col + j)),
            pl.BlockSpec((1, 1, S, LANES), lambda b, j, i: (b, j, 0, 0)),
            pl.BlockSpec((1, LANES, S), lambda b, j, i: (b, v_row + j, 0)),
        ],
        out_specs=pl.BlockSpec((1, FOX_TQ, LANES), lambda b, j, i: (b, i, j)),
        out_shape=jax.ShapeDtypeStruct((B, S, FOX_WIDTH), BF16),
        scratch_shapes=_flash_scratch(2, FOX_TQ, FOX_TK, 2 * LANES),
        compiler_params=_cp(("parallel", "parallel", "arbitrary")),
        name="fox",
    )(ut, qfeat, u3, kfeat, ut)


def _ret_kernel(q_ref, k_ref, v_ref, inner_ref, cross_ref, kdec_ref, cd_ref, bd_ref, gn_ref,
                o_ref, state_sc):
    @pl.when(pl.program_id(1) == 0)
    def _():
        state_sc[...] = jnp.zeros_like(state_sc)

    low = _low_half((RET_CHUNK, LANES), 1)
    inv = 1.0 / HEAD_DIM
    for c, j in [(c, j) for c in range(RET_STEP) for j in range(RET_HEADS // 2)]:
        rows = slice(c * RET_CHUNK, (c + 1) * RET_CHUNK)
        cols = slice(j * LANES, (j + 1) * LANES)
        q, k, v = q_ref[0, rows, cols], k_ref[0, rows, cols], v_ref[0, rows, cols]
        zero = jnp.zeros_like(q)
        qa, qb = jnp.where(low, q, zero), jnp.where(low, zero, q)
        pa = (_dot_nt(qa, k) * inner_ref[j, 0]).astype(BF16)
        pb = (_dot_nt(qb, k) * inner_ref[j, 1]).astype(BF16)
        o_in = jnp.where(low, _dot(pa, v), _dot(pb, v))
        state = state_sc[j]
        o = o_in + _dot(q, state.astype(BF16)) * cross_ref[j]
        kd = (k.astype(F32) * kdec_ref[j]).astype(BF16)
        state_sc[j] = state * cd_ref[j] + _dot_tn(kd, v) * bd_ref[...]
        sa = jnp.sum(jnp.where(low, o, 0.0), axis=-1, keepdims=True)
        st = jnp.sum(o, axis=-1, keepdims=True)
        mu = jnp.where(low, sa, st - sa) * inv
        d = o - mu
        d2 = d * d
        va = jnp.sum(jnp.where(low, d2, 0.0), axis=-1, keepdims=True)
        vt = jnp.sum(d2, axis=-1, keepdims=True)
        var = jnp.where(low, va, vt - va) * inv
        o_ref[0, rows, cols] = (d * lax.rsqrt(var + GN_EPS) * gn_ref[:, cols]).astype(o_ref.dtype)


def _ret_constants():
    lg = np.log(1.0 - 2.0 ** (-5.0 - np.arange(RET_HEADS)))
    i = np.arange(RET_CHUNK)
    diff = i[:, None] - i[None, :]
    inner = np.where(diff[None] >= 0, np.exp(lg[:, None, None] * np.maximum(diff, 0)[None]), 0.0)
    cross = np.exp(lg[:, None] * (i[None, :] + 1))
    kdec = np.exp(lg[:, None] * (RET_CHUNK - 1 - i)[None, :])
    cdec = np.exp(lg * RET_CHUNK)
    npair = RET_HEADS // 2
    inner = inner.reshape(npair, 2, RET_CHUNK, RET_CHUNK)

    def lanes(a):
        a = a.reshape(npair, 2, RET_CHUNK)
        return np.repeat(a.transpose(0, 2, 1), HEAD_DIM, axis=2)

    bd = np.kron(np.eye(2), np.ones((HEAD_DIM, HEAD_DIM)))
    cd = np.repeat(cdec.reshape(npair, 2), HEAD_DIM, axis=1)[:, :, None] * bd[None]
    f = lambda a: jnp.asarray(a, F32)
    return f(inner), f(lanes(cross)), f(lanes(kdec)), f(cd), f(bd)


def _retention(u3, gn_g, *, q_col, k_col, v_col):
    B, S, _ = u3.shape
    C = RET_CHUNK
    rows = RET_STEP * C
    npair = RET_HEADS // 2
    inner, cross, kdec, cd, bd = _ret_constants()
    full = lambda shape: pl.BlockSpec(shape, lambda b, i: (0,) * len(shape))
    return pl.pallas_call(
        _ret_kernel,
        grid=(B, S // rows),
        in_specs=[
            pl.BlockSpec((1, rows, RET_WIDTH), lambda b, i: (b, i, q_col)),
            pl.BlockSpec((1, rows, RET_WIDTH), lambda b, i: (b, i, k_col)),
            pl.BlockSpec((1, rows, RET_WIDTH), lambda b, i: (b, i, v_col)),
            full((npair, 2, C, C)), full((npair, C, LANES)), full((npair, C, LANES)),
            full((npair, LANES, LANES)), full((LANES, LANES)), full((1, RET_WIDTH)),
        ],
        out_specs=pl.BlockSpec((1, rows, RET_WIDTH), lambda b, i: (b, i, 0)),
        out_shape=jax.ShapeDtypeStruct((B, S, RET_WIDTH), BF16),
        scratch_shapes=[pltpu.VMEM((npair, LANES, LANES), F32)],
        compiler_params=_cp(("parallel", "arbitrary")),
        name="retention",
    )(u3, u3, u3, inner, cross, kdec, cd, bd, gn_g.reshape(1, RET_WIDTH))


def _out0_kernel(of_ref, or_ref, z_ref, x_ref, w_ref, o_ref):
    z = _silu(z_ref[...].astype(F32))
    ya = (of_ref[...].astype(F32) * z[:, :FOX_WIDTH]).astype(BF16)
    yb = (or_ref[...].astype(F32) * z[:, FOX_WIDTH:]).astype(BF16)
    o_ref[...] = x_ref[...] + _dot(ya, w_ref[:FOX_WIDTH, :]) + _dot(yb, w_ref[FOX_WIDTH:, :])


def _out0(o_f, o_r, u, x2, w_out, *, tm=OUT_TM):
    N, D = x2.shape
    return pl.pallas_call(
        _out0_kernel,
        grid=(N // tm,),
        in_specs=[
            pl.BlockSpec((tm, FOX_WIDTH), lambda i: (i, 0)),
            pl.BlockSpec((tm, RET_WIDTH), lambda i: (i, 0)),
            pl.BlockSpec((tm, D), lambda i: (i, 0)),
            pl.BlockSpec((tm, D), lambda i: (i, 0)),
            pl.BlockSpec((D, D), lambda i: (0, 0)),
        ],
        out_specs=pl.BlockSpec((tm, D), lambda i: (i, 0)),
        out_shape=jax.ShapeDtypeStruct((N, D), F32),
        compiler_params=_cp(("parallel",)),
        name="out0",
    )(o_f, o_r, u, x2, w_out)


def _out1_kernel(oc_ref, os_ref, z_ref, x_ref, w_ref, g_ref, o_ref):
    z = _silu(z_ref[...].astype(F32))
    y = ((oc_ref[...].astype(F32) + os_ref[...].astype(F32)) * z).astype(BF16)
    o_ref[...] = _rms(x_ref[...] + _dot(y, w_ref[...]), g_ref[...])


def _out1(o_c, o_s, u, x2, w_out, final_g, *, tm=OUT_TM):
    N, D = x2.shape
    row = pl.BlockSpec((tm, D), lambda i: (i, 0))
    return pl.pallas_call(
        _out1_kernel,
        grid=(N // tm,),
        in_specs=[row, row, row, row,
                  pl.BlockSpec((D, D), lambda i: (0, 0)),
                  pl.BlockSpec((1, D), lambda i: (0, 0))],
        out_specs=row,
        out_shape=jax.ShapeDtypeStruct((N, D), F32),
        compiler_params=_cp(("parallel",)),
        name="out1",
    )(o_c, o_s, u, x2, w_out, final_g.reshape(1, D))


def _compress_kernel(x_ref, pea_ref, peb_ref, wa_ref, wb_ref, w2_ref, *rest, transposed):
    x = x_ref[0].astype(F32)
    a = _dot((x + pea_ref[...]).astype(BF16), wa_ref[0])
    b = _dot((x + peb_ref[...]).astype(BF16), wb_ref[0])
    nseg = x.shape[0]
    pre = a + pltpu.roll(b, nseg - 1, 0)
    hid = _silu(pre).astype(BF16)
    if transposed:
        o_ref, = rest
        o_ref[0, 0] = _dot_nt(w2_ref[...], hid).astype(o_ref.dtype)
    else:
        feat_ref, o_ref = rest
        o_ref[0, 0] = (_dot(hid, w2_ref[...]) + feat_ref[...].astype(F32)).astype(o_ref.dtype)


def _compress(a3, pe, w1, w2, *, transposed):
    B, nseg, wid = a3.shape
    half = CMP_STRIDE * HEAD_DIM
    eye = jnp.eye(NSA_GROUPS, dtype=w1.dtype)

    def big(wh):
        w4 = wh.reshape(CMP_STRIDE, 1, HEAD_DIM, CMP_HIDDEN)
        sel = eye[:, None, :, None, None]
        return (sel * w4[None]).reshape(NSA_GROUPS, wid, CMP_HIDDEN).astype(BF16)

    def pe_big(p):
        return jnp.broadcast_to(p[:, None, :], (CMP_STRIDE, NSA_GROUPS, HEAD_DIM)).reshape(1, wid)

    args = [a3, pe_big(pe[:CMP_STRIDE]), pe_big(pe[CMP_STRIDE:]), big(w1[:half]), big(w1[half:])]
    in_specs = [
        pl.BlockSpec((1, nseg, wid), lambda b, g: (b, 0, 0)),
        pl.BlockSpec((1, wid), lambda b, g: (0, 0)),
        pl.BlockSpec((1, wid), lambda b, g: (0, 0)),
        pl.BlockSpec((1, wid, CMP_HIDDEN), lambda b, g: (g, 0, 0)),
        pl.BlockSpec((1, wid, CMP_HIDDEN), lambda b, g: (g, 0, 0)),
    ]
    if transposed:
        w2d = w2.T.astype(BF16)
        oshape, oblock = (B, NSA_GROUPS, HEAD_DIM, nseg), (1, 1, HEAD_DIM, nseg)
        args.append(w2d)
        in_specs.append(pl.BlockSpec(w2d.shape, lambda b, g: (0, 0)))
    else:
        w2d = jnp.pad(w2, ((0, 0), (0, LANES - HEAD_DIM))).astype(BF16)
        oshape, oblock = (B, NSA_GROUPS, nseg, LANES), (1, 1, nseg, LANES)
        feat = _pos_features(jnp.arange(nseg, dtype=jnp.int32) * CMP_STRIDE + (CMP_BLOCK - 1), LANES)
        args += [w2d, feat]
        in_specs += [pl.BlockSpec(w2d.shape, lambda b, g: (0, 0)),
                     pl.BlockSpec((nseg, LANES), lambda b, g: (0, 0))]
    return pl.pallas_call(
        functools.partial(_compress_kernel, transposed=transposed),
        grid=(B, NSA_GROUPS),
        in_specs=in_specs,
        out_specs=pl.BlockSpec(oblock, lambda b, g: (b, g, 0, 0)),
        out_shape=jax.ShapeDtypeStruct(oshape, BF16),
        compiler_params=_cp(("parallel", "parallel")),
        name="compress",
    )(*args)


def _slope_table():
    s = np.asarray(2.0 ** (-8.0 * (np.arange(NSA_HEADS) + 1) / NSA_HEADS), np.float32)
    sl = np.asarray(s.astype(np.float64) * LOG2E, np.float32)
    p1, p2, p3 = _np_pieces(sl)
    tab = np.zeros((NSA_HEADS, QF_ROWS), np.float32)
    for k, p in enumerate((p1, p1, p2, p2, p3, p3)):
        tab[:, k] = p
    tab[:, 6] = sl
    tab = np.broadcast_to(tab.reshape(NSA_GROUPS, NSA_HPG * QF_ROWS, 1),
                          (NSA_GROUPS, NSA_HPG * QF_ROWS, NSA_TQ))
    return jnp.asarray(tab)


def _pos_features(pos, width):
    pos = pos[:, None]
    lane = jnp.arange(width, dtype=jnp.int32)[None, :] % LANES
    hi = ((pos // SLC_BLOCK) * SLC_BLOCK).astype(F32)
    lo = (pos % SLC_BLOCK).astype(F32)
    k = lane - FEAT0
    f = jnp.where((k >= 0) & (k < 6), jnp.where(k % 2 == 0, hi, lo), 0.0)
    f = jnp.where((k >= 6) & (k < 9), 1.0, f)
    return f.astype(BF16)


def _nsa_queries(qt_ref, tab_ref, t0):
    tq = qt_ref.shape[2]
    r = lax.broadcasted_iota(jnp.int32, (QF_ROWS, tq), 0)
    t = (t0 + lax.broadcasted_iota(jnp.int32, (1, tq), 1)).astype(F32)
    zeros = jnp.zeros((LANES - HEAD_DIM - QF_ROWS, tq), BF16)
    out = []
    for i in range(NSA_HPG):
        tile = tab_ref[0, i * QF_ROWS:(i + 1) * QF_ROWS, :]
        a1, a2, a3 = _pieces(-(tile[6:7, :] * t))
        feat = jnp.where(r == 6, a1, jnp.where(r == 7, a2, jnp.where(r == 8, a3,
                                                                     jnp.where(r < 6, tile, 0.0))))
        out.append(jnp.concatenate([qt_ref[0, i * HEAD_DIM:(i + 1) * HEAD_DIM, :],
                                    feat.astype(BF16), zeros], axis=0))
    return out


def _gates_t(gt_ref, bg_ref, branch):
    gl = gt_ref[0] + bg_ref[...]
    return [_sigmoid(gl[N_BRANCH * i + branch:N_BRANCH * i + branch + 1, :]) for i in range(NSA_HPG)]


def _gated(outs_t, gates):
    return [o * gt for o, gt in zip(outs_t, gates)]


def _store_heads(o_ref, g):
    o_ref[0, :, :LANES] = jnp.concatenate(g[:2], axis=0).T.astype(o_ref.dtype)
    o_ref[0, :, LANES:] = jnp.concatenate(g[2:], axis=0).T.astype(o_ref.dtype)


def _nsa_specs(q_row):
    return dict(
        q=pl.BlockSpec((1, NSA_HPG * HEAD_DIM, NSA_TQ), lambda b, g, i: (b, q_row + g, i)),
        tab=pl.BlockSpec((1, NSA_HPG * QF_ROWS, NSA_TQ), lambda b, g, i: (g, 0, 0)),
        gt=pl.BlockSpec((1, GATE_ROWS, NSA_TQ), lambda b, g, i: (b, g, i)),
        bg=pl.BlockSpec((GATE_ROWS, 1), lambda b, g, i: (g, 0)),
        out=pl.BlockSpec((1, NSA_TQ, NSA_HPG * HEAD_DIM), lambda b, g, i: (b, i, g)),
    )


def _argmax_first(v, idx):
    n = v.shape[0]
    slabs = [(v[r:r + 8], idx[r:r + 8]) for r in range(0, n, 8)]
    while len(slabs) > 1:
        nxt = []
        for (va, ia), (vb, ib) in zip(slabs[0::2], slabs[1::2]):
            right = vb > va
            nxt.append((jnp.where(right, vb, va), jnp.where(right, ib, ia)))
        if len(slabs) % 2:
            nxt.append(slabs[-1])
        slabs = nxt
    v, idx = slabs[0]
    mx = jnp.max(v, axis=0, keepdims=True)
    first = jnp.min(jnp.where(v == mx, idx, float(n)), axis=0, keepdims=True)
    return mx, first


def _cmp_body(nc, t0, qh, kc_ref, vct_ref, mt_ref, grp_ref, gates, o_ref, sel_ref, flag_ref):
    tq = NSA_TQ
    rows = nc * CMP_CHUNK
    full = max(rows - CMP_CHUNK - 8, 0)
    nseg = kc_ref.shape[2]
    kc = kc_ref[0, 0, :rows, :]
    vct = vct_ref[0, 0, :, :rows]
    t = t0 + lax.broadcasted_iota(jnp.int32, (1, tq), 1)
    cidx = full + lax.broadcasted_iota(jnp.int32, (rows - full, 1), 0)
    valid = (cidx * CMP_STRIDE + (CMP_BLOCK - 1) <= t) & (cidx < nseg - 1)
    psum = jnp.zeros((rows, tq), F32)
    ps = []
    s_all = _dot(kc, jnp.concatenate(qh, axis=1))
    for i in range(NSA_HPG):
        s = s_all[:, i * tq:(i + 1) * tq]
        s_last = jnp.where(valid, s[full:], NEG)
        m = jnp.max(s_last, axis=0, keepdims=True)
        if nc > 1:
            m = jnp.maximum(m, jnp.max(s[:full], axis=0, keepdims=True))
        e = jnp.where(valid, jnp.exp2(s_last - m), 0.0)
        if nc > 1:
            e = jnp.concatenate([jnp.exp2(s[:full] - m), e], axis=0)
        l = jnp.sum(e, axis=0, keepdims=True)
        p = e * jnp.where(l > 0.0, 1.0 / l, 0.0)
        psum = psum + p
        ps.append(p.astype(BF16))
    o_all = _dot(vct, jnp.concatenate(ps, axis=1))
    _store_heads(o_ref, _gated([o_all[:, i * tq:(i + 1) * tq] for i in range(NSA_HPG)], gates))
    ns = rows * CMP_STRIDE // SLC_BLOCK
    mt = mt_ref[:ns, :rows]
    imp = sum(_dot(mt, p.astype(BF16)) for p in _pieces(psum))
    blk = lax.broadcasted_iota(jnp.int32, (ns, 1), 0)
    cur = t // SLC_BLOCK
    bvalid = blk * SLC_BLOCK <= t
    forced = (blk == 0) | (blk == cur) | (blk == cur - 1)
    score = jnp.where(forced, -jnp.inf, jnp.where(bvalid, imp, NEG))
    blk_f = jnp.broadcast_to(blk.astype(F32), (ns, tq))
    work = score
    for _ in range(SLC_TOPK - N_FORCED):
        mx, first = _argmax_first(work, blk_f)
        work = jnp.where(blk_f == first, -jnp.inf, work)
    picked = (score > mx) | ((score == mx) & (blk_f <= first))
    selneg = jnp.where(bvalid & (forced | picked), 0.0, -MASK_BIG)
    if ns < NS_PAD:
        selneg = jnp.concatenate([selneg, jnp.full((NS_PAD - ns, tq), -MASK_BIG, F32)], axis=0)
    sel_ref[0, 0] = selneg.astype(sel_ref.dtype)
    picked = jnp.where(selneg == 0.0, 1.0, 0.0).astype(BF16)
    used = _dot_nt(jnp.ones((8, tq), BF16), picked)
    used = jnp.where(used > 0.0, 1.0, 0.0).astype(BF16)
    flag_ref[0] = (_dot(used, grp_ref[...])[0:1] > 0.0).astype(jnp.int32)


def _cmp_kernel(q_ref, kc_ref, vct_ref, mt_ref, grp_ref, tab_ref, gt_ref, bg_ref,
                o_ref, sel_ref, flag_ref):
    t0 = pl.program_id(2) * NSA_TQ
    qh = _nsa_queries(q_ref, tab_ref, t0)
    gates = _gates_t(gt_ref, bg_ref, 0)
    nchunk = kc_ref.shape[2] // CMP_CHUNK
    last = t0 // (CMP_CHUNK * CMP_STRIDE)
    for nc in range(1, nchunk + 1):
        pl.when(last == nc - 1)(functools.partial(
            _cmp_body, nc, t0, qh, kc_ref, vct_ref, mt_ref, grp_ref, gates, o_ref, sel_ref, flag_ref))


def _cmp_to_slc_t(nseg, ns):
    c0 = np.arange(nseg)[:, None] * CMP_STRIDE
    s0 = np.arange(ns)[None, :] * SLC_BLOCK
    overlap = np.clip(np.minimum(c0 + CMP_BLOCK, s0 + SLC_BLOCK) - np.maximum(c0, s0), 0, None)
    m = overlap / CMP_STRIDE
    m[nseg - 1] = 0.0
    mt = np.zeros((NS_PAD, nseg))
    mt[:ns] = m.T
    return jnp.asarray(mt, BF16)


def _tile_groups():
    per = SLC_TK // SLC_BLOCK
    g = (np.arange(NS_PAD)[:, None] // per) == np.arange(NS_PAD)[None, :]
    return jnp.asarray(g, BF16)


def _cmp_attention(ut, kcmp, vcmp_t, gt, bg, *, q_row):
    B, _, S = ut.shape
    nseg = kcmp.shape[2]
    nq = S // NSA_TQ
    sp = _nsa_specs(q_row)
    return pl.pallas_call(
        _cmp_kernel,
        grid=(B, NSA_GROUPS, S // NSA_TQ),
        in_specs=[
            sp["q"],
            pl.BlockSpec((1, 1, nseg, LANES), lambda b, g, i: (b, g, 0, 0)),
            pl.BlockSpec((1, 1, HEAD_DIM, nseg), lambda b, g, i: (b, g, 0, 0)),
            pl.BlockSpec((NS_PAD, nseg), lambda b, g, i: (0, 0)),
            pl.BlockSpec((NS_PAD, NS_PAD), lambda b, g, i: (0, 0)),
            sp["tab"], sp["gt"], sp["bg"],
        ],
        out_specs=[sp["out"], pl.BlockSpec((1, 1, NS_PAD, NSA_TQ), lambda b, g, i: (b, g, 0, i)),
                   pl.BlockSpec((1, 1, NS_PAD), lambda b, g, i: ((b * NSA_GROUPS + g) * nq + i, 0, 0))],
        out_shape=[jax.ShapeDtypeStruct((B, S, NSA_WIDTH), BF16),
                   jax.ShapeDtypeStruct((B, NSA_GROUPS, NS_PAD, S), BF16),
                   jax.ShapeDtypeStruct((B * NSA_GROUPS * nq, 1, NS_PAD), jnp.int32)],
        compiler_params=_cp(("parallel", "parallel", "arbitrary")),
        name="cmp_attention",
    )(ut, kcmp, vcmp_t, _cmp_to_slc_t(nseg, S // SLC_BLOCK), _tile_groups(), _slope_table(), gt, bg)


def _window_branch(qh, t0, k_ref, vt_ref, bias_ref, bm_sc, s_bufs, acc_sc, last_before):
    tq, tk = NSA_TQ, WIN_TK
    bm_sc[...] = jnp.concatenate(qh, axis=1)
    _, qk_all, step, finish = _flash_ops(
        bm_sc, s_bufs, None, acc_sc, NSA_HPG, tq, tk,
        lambda k0: k_ref[0, pl.ds(k0, tk), :], lambda h, k0: vt_ref[0, :, pl.ds(k0, tk)])
    first = jnp.maximum(t0 - WINDOW, 0) // tk
    ntile = (WINDOW + tq) // tk
    qk_all(first, 0)
    ms = (jnp.full((1, tq), NEG, F32),) * NSA_HPG
    for j in range(ntile):
        last = j + 1 == ntile
        ms = step(first + j, j % 2, ms, nxt=None if last else first + j + 1,
                  bias=bias_ref[0, j * tk:(j + 1) * tk, :], before=last_before if last else None)
    return finish()


def _slc_win_kernel(fl_ref, q_ref, k_ref, vt_ref, sel_ref, e_ref, kw_ref, vwt_ref, bias_ref,
                    tab_ref, gt_ref, bg_ref, o_ref, bm_sc, s0_sc, s1_sc, mx_sc, acc_sc,
                    wbm_sc, ws0_sc, ws1_sc, wacc_sc, tiles_sm):
    qi = pl.program_id(2)
    tq, tk = NSA_TQ, SLC_TK
    t0 = qi * tq
    diag = t0 // tk
    row = (pl.program_id(0) * NSA_GROUPS + pl.program_id(1)) * pl.num_programs(2) + qi
    n_un = jnp.int32(0)
    for j in range(k_ref.shape[1] // tk):
        tiles_sm[n_un] = jnp.int32(j)
        n_un = n_un + ((fl_ref[row, j] > 0) & (j < diag)).astype(jnp.int32)
    tiles_sm[n_un] = diag
    qh = _nsa_queries(q_ref, tab_ref, t0)
    selneg = sel_ref[0, 0]
    bm_sc[...] = jnp.concatenate([jnp.concatenate([q, selneg], axis=0) for q in qh], axis=1)
    qpos = t0 + lax.broadcasted_iota(jnp.int32, (tk, tq), 1)

    def k_tile(k0):
        return jnp.concatenate([k_ref[0, pl.ds(k0, tk), :], e_ref[pl.ds(k0, tk), :]], axis=1)

    def v_rows(h, k0):
        return vt_ref[0, :, pl.ds(k0, tk)]

    ops = _flash_ops(bm_sc, (s0_sc, s1_sc), mx_sc, acc_sc, NSA_HPG, tq, tk, k_tile, v_rows)
    o_win = _gated(_window_branch(qh, t0, kw_ref, vwt_ref, bias_ref, wbm_sc, (ws0_sc, ws1_sc),
                                  wacc_sc, lambda h: ops[0](tiles_sm[0], 0, h)),
                   _gates_t(gt_ref, bg_ref, 2))
    outs = _flash_t(ops, NSA_HPG, tq, tk, n_un, lambda i: tiles_sm[i], diag,
                    lambda key: key <= qpos, first_done=True)
    o_slc = _gated(outs, _gates_t(gt_ref, bg_ref, 1))
    _store_heads(o_ref, [a + b for a, b in zip(o_slc, o_win)])


def _win_bias():
    tq, nk = NSA_TQ, WINDOW + NSA_TQ
    out = []
    for p in range(WINDOW // tq + 1):
        t0 = p * tq
        key = max(t0 - WINDOW, 0) + np.arange(nk)[:, None]
        qpos = t0 + np.arange(tq)[None, :]
        out.append(np.where((key <= qpos) & (key > qpos - WINDOW), 0.0, NEG))
    return jnp.asarray(np.stack(out), F32)


def _block_onehot(S):
    e = (np.arange(S)[:, None] // SLC_BLOCK) == np.arange(NS_PAD)[None, :]
    return jnp.asarray(e, BF16)


def _slc_win_attention(u3, ut, selneg, flags, gt, bg, *, q_row, ks_col, vs_row, kw_col, vw_row):
    B, S, _ = u3.shape
    sp = {k: pl.BlockSpec(v.block_shape, lambda b, g, i, fl, f=v.index_map: f(b, g, i))
          for k, v in _nsa_specs(q_row).items()}
    bias = _win_bias()
    npat = bias.shape[0]
    kspec = lambda col: pl.BlockSpec((1, S, LANES), lambda b, g, i, fl: (b, 0, col + g))
    vspec = lambda row: pl.BlockSpec((1, HEAD_DIM, S), lambda b, g, i, fl: (b, row + g, 0))
    grid_spec = pltpu.PrefetchScalarGridSpec(
        num_scalar_prefetch=1,
        grid=(B, NSA_GROUPS, S // NSA_TQ),
        in_specs=[
            sp["q"], kspec(ks_col), vspec(vs_row),
            pl.BlockSpec((1, 1, NS_PAD, NSA_TQ), lambda b, g, i, fl: (b, g, 0, i)),
            pl.BlockSpec((S, NS_PAD), lambda b, g, i, fl: (0, 0)),
            kspec(kw_col), vspec(vw_row),
            pl.BlockSpec((1,) + bias.shape[1:],
                         lambda b, g, i, fl: (jnp.minimum(i, npat - 1), 0, 0)),
            sp["tab"], sp["gt"], sp["bg"],
        ],
        out_specs=sp["out"],
        scratch_shapes=_flash_scratch(NSA_HPG, NSA_TQ, SLC_TK, 2 * LANES)
        + _flash_scratch(NSA_HPG, NSA_TQ, WIN_TK, LANES, col_max=False)
        + [pltpu.SMEM((S // SLC_TK + 1,), jnp.int32)],
    )
    return pl.pallas_call(
        _slc_win_kernel,
        grid_spec=grid_spec,
        out_shape=jax.ShapeDtypeStruct((B, S, NSA_WIDTH), BF16),
        compiler_params=_cp(("parallel", "parallel", "arbitrary")),
        name="slc_win_attention",
    )(flags, ut, u3, ut, selneg, _block_onehot(S), u3, ut, bias, _slope_table(), gt, bg)


def _aug_groups(w):
    d = w.shape[0]
    w = w.reshape(d, NSA_GROUPS, HEAD_DIM)
    return jnp.pad(w, ((0, 0), (0, 0), (0, LANES - HEAD_DIM))).reshape(d, NSA_GROUPS * LANES)


def _even_layer(x, norm_g, w_in, b_f, gn_g, w_out):
    B, S, D = x.shape
    qscale = HEAD_DIM ** -0.5 * LOG2E
    q_f, k_f, v_f, w_fl, q_r, k_r, v_r, z = jnp.split(
        w_in, np.cumsum([FOX_WIDTH] * 3 + [FOX_HEADS] + [RET_WIDTH] * 3).tolist(), axis=1)
    w = jnp.concatenate([z, k_f, q_r, k_r * HEAD_DIM ** -0.5, v_r], axis=1).astype(BF16)
    w_t = jnp.concatenate([q_f * qscale, v_f], axis=1).T.astype(BF16)
    x2 = x.reshape(B * S, D)
    u, ut = _proj(x2, norm_g, w, seq=S, w_t=[w_t], t_dtypes=[BF16])
    u3 = u.reshape(B, S, -1)
    kfeat, qfeat = _fgate(x, norm_g, w_fl, b_f, tile=min(512, S))
    o_f = _fox(u3, ut, kfeat, qfeat, q_row=0, k_col=D // LANES, v_row=FOX_WIDTH // LANES)
    rb = (D + FOX_WIDTH) // RET_WIDTH
    o_r = _retention(u3, gn_g, q_col=rb, k_col=rb + 1, v_col=rb + 2)
    out = _out0(o_f.reshape(B * S, -1), o_r.reshape(B * S, -1), u, x2, w_out.astype(BF16))
    return out.reshape(B, S, D)


def _odd_layer(x, norm_g, w_in, b_gate, pe_k, pe_v, wk1, wk2, wv1, wv2, w_out, final_g):
    B, S, D = x.shape
    assert S // SLC_BLOCK <= NS_PAD
    qscale = HEAD_DIM ** -0.5 * LOG2E
    sizes = [NSA_WIDTH] + [NSA_KV_WIDTH] * 6 + [NSA_HEADS * N_BRANCH]
    q, kc, vc, ks, vs, kw, vw, gl, z = jnp.split(w_in, np.cumsum(sizes).tolist(), axis=1)
    w = jnp.concatenate([z, kc, vc, _aug_groups(ks), _aug_groups(kw)], axis=1).astype(BF16)
    per_group = NSA_HPG * N_BRANCH
    glt = jnp.pad(gl.T.reshape(NSA_GROUPS, per_group, D), ((0, 0), (0, GATE_ROWS - per_group), (0, 0)))
    glt = glt.reshape(NSA_GROUPS * GATE_ROWS, D).astype(BF16)
    bg = jnp.pad(b_gate.reshape(NSA_GROUPS, per_group), ((0, 0), (0, GATE_ROWS - per_group)))
    bg = bg.reshape(NSA_GROUPS * GATE_ROWS, 1)
    w_vt = jnp.concatenate([q * qscale, vs, vw], axis=1).T.astype(BF16)
    x2 = x.reshape(B * S, D)
    first_k = (D + 2 * NSA_KV_WIDTH) // PROJ_TN
    u, kc_a, vc_a, ut, gt = _proj(
        x2, norm_g, w, seq=S, split_tile=D // PROJ_TN,
        addend=_pos_features(jnp.arange(S, dtype=jnp.int32), PROJ_TN), add_tiles=(first_k, first_k + 1),
        w_t=[w_vt, glt], t_dtypes=[BF16, F32])
    u3 = u.reshape(B, S, -1)
    nseg = S // CMP_STRIDE
    kcmp = _compress(kc_a.reshape(B, nseg, -1), pe_k, wk1, wk2, transposed=False)
    vcmp_t = _compress(vc_a.reshape(B, nseg, -1), pe_v, wv1, wv2, transposed=True)
    o_c, selneg, flags = _cmp_attention(ut, kcmp, vcmp_t, gt, bg, q_row=0)
    kb = (D + 2 * NSA_KV_WIDTH) // LANES
    vb = NSA_WIDTH // HEAD_DIM
    o_s = _slc_win_attention(u3, ut, selneg, flags[:, 0, :S // SLC_TK], gt, bg, q_row=0, ks_col=kb,
                             vs_row=vb, kw_col=kb + NSA_GROUPS, vw_row=vb + NSA_GROUPS)
    r = lambda a: a.reshape(B * S, -1)
    out = _out1(r(o_c), r(o_s), u, x2, w_out.astype(BF16), final_g)
    return out.reshape(B, S, D)


def kernel(x, even_norm_g, even_w_in, even_b_f, even_gn_g, even_w_out, odd_norm_g, odd_w_in,
           odd_b_gate, odd_pe_k, odd_pe_v, odd_wk1, odd_wk2, odd_wv1, odd_wv2, odd_w_out, final_g):
    x = _even_layer(x, even_norm_g[0], even_w_in[0], even_b_f[0], even_gn_g[0], even_w_out[0])
    return _odd_layer(x, odd_norm_g[0], odd_w_in[0], odd_b_gate[0], odd_pe_k[0], odd_pe_v[0],
                      odd_wk1[0], odd_wk2[0], odd_wv1[0], odd_wv2[0], odd_w_out[0], final_g)
```

```python
import functools
import math

import jax
import jax.numpy as jnp
import numpy as np
from jax import lax
from jax.experimental import pallas as pl
from jax.experimental.pallas import tpu as pltpu

D_MODEL = 1024
HEAD_DIM = 64
LANES = 128
FOX_HEADS = 8
RET_HEADS = 8
FOX_WIDTH = FOX_HEADS * HEAD_DIM
RET_WIDTH = RET_HEADS * HEAD_DIM
RET_CHUNK = 128
RET_STEP = 4
NSA_HEADS = 16
NSA_GROUPS = 4
NSA_HPG = NSA_HEADS // NSA_GROUPS
NSA_WIDTH = NSA_HEADS * HEAD_DIM
NSA_KV_WIDTH = NSA_GROUPS * HEAD_DIM
N_BRANCH = 3
GATE_ROWS = 16
CMP_BLOCK = 32
CMP_STRIDE = 16
CMP_HIDDEN = 256
CMP_CHUNK = 128
SLC_BLOCK = 64
SLC_TOPK = 16
N_FORCED = 3
NS_PAD = LANES
WINDOW = 512
RMS_EPS = 1e-6
GN_EPS = 1e-5
NEG = -1e30
FORCE_BONUS = 1e6
MASK_BIG = 2.0 ** 100
LOG2E = math.log2(math.e)
FEAT0 = HEAD_DIM
QF_ROWS = 16
ACC_ROWS = HEAD_DIM + 16

PROJ_TM = 1024
PROJ_TN = 512
FOX_TQ = 512
FOX_TK = 512
NSA_TQ = 256
SLC_TK = 512
WIN_TK = 256
OUT_TM = 1024
VMEM_LIMIT = 48 * 1024 * 1024

F32 = jnp.float32
BF16 = jnp.bfloat16


def _cp(sem, vmem=VMEM_LIMIT):
    return pltpu.CompilerParams(dimension_semantics=sem, vmem_limit_bytes=vmem)


def _dot(a, b):
    return jnp.dot(a, b, preferred_element_type=F32)


def _dot_nt(a, b):
    return lax.dot_general(a, b, (((1,), (1,)), ((), ())), preferred_element_type=F32)


def _dot_tn(a, b):
    return lax.dot_general(a, b, (((0,), (0,)), ((), ())), preferred_element_type=F32)


def _rms(x, g):
    return x * lax.rsqrt(jnp.mean(x * x, axis=-1, keepdims=True) + RMS_EPS) * g


def _silu(x):
    return x * (1.0 / (1.0 + jnp.exp(-x)))


def _sigmoid(x):
    return 1.0 / (1.0 + jnp.exp(-x))


def _low_half(shape, axis):
    return lax.broadcasted_iota(jnp.int32, shape, axis) < HEAD_DIM


def _pieces(v):
    p1 = v.astype(BF16).astype(F32)
    r = v - p1
    p2 = r.astype(BF16).astype(F32)
    p3 = (r - p2).astype(BF16).astype(F32)
    return p1, p2, p3


def _np_pieces(v):
    v = np.asarray(v, np.float64)
    bf = lambda a: np.asarray(a, np.float32).astype(BF16).astype(np.float64)
    p1 = bf(v)
    p2 = bf(v - p1)
    p3 = bf(v - p1 - p2)
    return p1, p2, p3


def _place(lane, cols):
    out = jnp.zeros(lane.shape, F32)
    for i, c in cols.items():
        out = jnp.where(lane == i, c, out)
    return out


def _fgate_kernel(x_ref, g_ref, wf_ref, b_ref, kf_ref, qf_ref, carry):
    @pl.when(pl.program_id(1) == 0)
    def _():
        carry[...] = jnp.zeros_like(carry)

    h = _rms(x_ref[0], g_ref[...])
    t = h.shape[0]
    h1 = h.astype(BF16)
    h2 = (h - h1.astype(F32)).astype(BF16)
    w = wf_ref[...]
    w1 = w.astype(BF16)
    w2 = (w - w1.astype(F32)).astype(BF16)
    nf = w.shape[1]
    r = _dot(jnp.concatenate([h1, h2], axis=0), jnp.concatenate([w1, w2], axis=1))
    f = r[:t, :nf] + r[:t, nf:] + r[t:, :nf] + b_ref[...]
    ls = jnp.minimum(f, 0.0) - jnp.log(1.0 + jnp.exp(-jnp.abs(f)))
    r = lax.broadcasted_iota(jnp.int32, (t, t), 0)
    c = lax.broadcasted_iota(jnp.int32, (t, t), 1)
    lower = jnp.where(c <= r, 1.0, 0.0).astype(BF16)
    r = _dot(lower, jnp.concatenate([p.astype(BF16) for p in _pieces(ls)], axis=1))
    cs = r[:, :nf] + r[:, nf:2 * nf] + r[:, 2 * nf:] + carry[...]
    carry[...] = cs[t - 1:t, :]
    cl = cs * LOG2E
    lane = lax.broadcasted_iota(jnp.int32, (t, LANES), 1)
    one = jnp.ones((t, 1), F32)
    for j in range(FOX_HEADS // 2):
        a1, a2, a3 = _pieces(cl[:, 2 * j:2 * j + 1])
        b1, b2, b3 = _pieces(cl[:, 2 * j + 1:2 * j + 2])
        kf = _place(lane, {0: a1, 1: a2, 2: a3, 3: one, 4: one, 5: one, 6: b1, 7: b2, 8: b3})
        kf_ref[0, j] = kf.astype(BF16)
        qa = _place(lane, {0: -one, 1: -one, 2: -one, 3: a1, 4: a2, 5: a3})
        qb = _place(lane, {3: b1, 4: b2, 5: b3, 6: -one, 7: -one, 8: -one})
        qf_ref[0, j, 0] = qa.T[:QF_ROWS].astype(BF16)
        qf_ref[0, j, 1] = qb.T[:QF_ROWS].astype(BF16)


def _fgate(x, g, wf, b_f, *, tile):
    B, S, D = x.shape
    npair = FOX_HEADS // 2
    return pl.pallas_call(
        _fgate_kernel,
        grid=(B, S // tile),
        in_specs=[
            pl.BlockSpec((1, tile, D), lambda b, s: (b, s, 0)),
            pl.BlockSpec((1, D), lambda b, s: (0, 0)),
            pl.BlockSpec((D, FOX_HEADS), lambda b, s: (0, 0)),
            pl.BlockSpec((1, FOX_HEADS), lambda b, s: (0, 0)),
        ],
        out_specs=[
            pl.BlockSpec((1, npair, tile, LANES), lambda b, s: (b, 0, s, 0)),
            pl.BlockSpec((1, npair, 2, QF_ROWS, tile), lambda b, s: (b, 0, 0, 0, s)),
        ],
        out_shape=[
            jax.ShapeDtypeStruct((B, npair, S, LANES), BF16),
            jax.ShapeDtypeStruct((B, npair, 2, QF_ROWS, S), BF16),
        ],
        scratch_shapes=[pltpu.VMEM((1, FOX_HEADS), F32)],
        compiler_params=_cp(("parallel", "arbitrary")),
        name="fgate",
    )(x, g.reshape(1, D), wf, b_f.reshape(1, FOX_HEADS))


def _proj_kernel(*refs, split_tile, add_tiles, n_t):
    it = iter(refs)
    x_ref, g_ref, w_ref = next(it), next(it), next(it)
    add_ref = next(it) if add_tiles else None
    wt_refs = [next(it) for _ in range(n_t)]
    u_ref = next(it)
    e_refs = [next(it), next(it)] if split_tile is not None else None
    ut_refs = [next(it) for _ in range(n_t)]
    h_sc = next(it)
    j = pl.program_id(1)

    @pl.when(j == 0)
    def _():
        h = _rms(x_ref[...], g_ref[...]).astype(BF16)
        h_sc[...] = h
        for wt_ref, ut_ref in zip(wt_refs, ut_refs):
            ut_ref[0] = _dot_nt(wt_ref[...], h).astype(ut_ref.dtype)

    acc = _dot(h_sc[...], w_ref[...])
    if add_tiles:
        is_add = functools.reduce(jnp.logical_or, [j == t for t in add_tiles])

        @pl.when(is_add)
        def _():
            u_ref[...] = (acc + add_ref[...].astype(F32)).astype(u_ref.dtype)

        @pl.when(jnp.logical_not(is_add))
        def _():
            u_ref[...] = acc.astype(u_ref.dtype)
    else:
        u_ref[...] = acc.astype(u_ref.dtype)
    if split_tile is not None:
        @pl.when(j == split_tile)
        def _():
            half = acc.shape[1] // 2
            e_refs[0][...] = acc[:, :half].astype(e_refs[0].dtype)
            e_refs[1][...] = acc[:, half:].astype(e_refs[1].dtype)


def _proj(x2, g, w, *, seq, split_tile=None, addend=None, add_tiles=(), w_t=(), t_dtypes=(),
          tm=PROJ_TM, tn=PROJ_TN):
    N, D = x2.shape
    W = w.shape[1]
    nbs = seq // tm
    in_specs = [
        pl.BlockSpec((tm, D), lambda i, j: (i, 0)),
        pl.BlockSpec((1, D), lambda i, j: (0, 0)),
        pl.BlockSpec((D, tn), lambda i, j: (0, j)),
    ]
    args = [x2, g.reshape(1, D), w]
    if add_tiles:
        in_specs.append(pl.BlockSpec((tm, tn), lambda i, j: (i % nbs, 0)))
        args.append(addend)
    out_shape = [jax.ShapeDtypeStruct((N, W), BF16)]
    out_specs = [pl.BlockSpec((tm, tn), lambda i, j: (i, j))]
    if split_tile is not None:
        out_shape += [jax.ShapeDtypeStruct((N, tn // 2), BF16)] * 2
        out_specs += [pl.BlockSpec((tm, tn // 2), lambda i, j: (i, 0))] * 2
    for wt, dt in zip(w_t, t_dtypes):
        rows = wt.shape[0]
        in_specs.append(pl.BlockSpec((rows, D), lambda i, j: (0, 0)))
        args.append(wt)
        out_shape.append(jax.ShapeDtypeStruct((N // seq, rows, seq), dt))
        out_specs.append(pl.BlockSpec((1, rows, tm), lambda i, j: (i // nbs, 0, i % nbs)))
    return pl.pallas_call(
        functools.partial(_proj_kernel, split_tile=split_tile, add_tiles=tuple(add_tiles),
                          n_t=len(w_t)),
        grid=(N // tm, W // tn),
        in_specs=in_specs,
        out_specs=out_specs,
        out_shape=out_shape,
        scratch_shapes=[pltpu.VMEM((tm, D), BF16)],
        compiler_params=_cp(("parallel", "arbitrary")),
        name="proj",
    )(*args)


def _flash_ops(bm_sc, s_bufs, mx_sc, acc_sc, nh, tq, tk, k_tile, v_rows):
    acc_sc[...] = jnp.zeros_like(acc_sc)
    ones = jnp.ones((ACC_ROWS - HEAD_DIM, tk), BF16)

    def qk_head(tile, slot, h):
        cols = slice(h * tq, (h + 1) * tq)
        s = _dot(k_tile(pl.multiple_of(tile * tk, tk)), bm_sc[:, cols])
        s_bufs[slot][:, cols] = s
        if mx_sc is not None:
            mx_sc[slot, :, cols] = jnp.max(s, axis=0, keepdims=True)

    def soft_head(tile, slot, m_old, h, valid, bias):
        cols = slice(h * tq, (h + 1) * tq)
        buf = s_bufs[slot]
        k0 = pl.multiple_of(tile * tk, tk)
        if valid is not None or bias is not None:
            s = buf[:, cols] + bias if valid is None else jnp.where(valid, buf[:, cols], NEG)
            m_new = jnp.maximum(m_old, jnp.max(s, axis=0, keepdims=True))
            p = jnp.exp2(s - m_new).astype(BF16)
        else:
            m_new = jnp.maximum(m_old, mx_sc[slot, :, cols])
            p = jnp.exp2(buf[:, cols] - m_new).astype(BF16)
        alpha = jnp.exp2(m_old - m_new)
        lhs = jnp.concatenate([v_rows(h, k0), ones], axis=0)
        acc_sc[h] = alpha * acc_sc[h] + _dot(lhs, p)
        return m_new

    def qk_all(tile, slot):
        for h in range(nh):
            qk_head(tile, slot, h)

    def step(cur, slot, ms, nxt=None, valid=None, bias=None, before=None):
        out = []
        for h in range(nh):
            if nxt is not None:
                qk_head(nxt, 1 - slot, h)
            if before is not None:
                before(h)
            out.append(soft_head(cur, slot, ms[h], h, valid, bias))
        return tuple(out)

    def finish():
        outs = []
        for h in range(nh):
            a = acc_sc[h]
            outs.append(a[:HEAD_DIM] * (1.0 / a[HEAD_DIM:HEAD_DIM + 1]))
        return outs

    return qk_head, qk_all, step, finish


def _flash_t(ops, nh, tq, tk, n_un, tile_of, diag, valid_fn, first_done=False):
    _, qk_all, step, finish = ops
    key_iota = lax.broadcasted_iota(jnp.int32, (tk, tq), 0)
    diag_valid = lambda: valid_fn(pl.multiple_of(diag * tk, tk) + key_iota)

    def pair(i, ms):
        t_a, t_b, t_c = tile_of(2 * i), tile_of(2 * i + 1), tile_of(2 * i + 2)
        return step(t_b, 1, step(t_a, 0, ms, nxt=t_b), nxt=t_c)

    def odd_tail(ms):
        return step(diag, 1, step(tile_of(n_un - 1), 0, ms, nxt=diag), valid=diag_valid())

    def even_tail(ms):
        return step(diag, 0, ms, valid=diag_valid())

    if not first_done:
        qk_all(tile_of(0), 0)
    ms = (jnp.full((1, tq), NEG, F32),) * nh
    ms = lax.fori_loop(0, n_un // 2, pair, ms)
    lax.cond(n_un % 2 == 1, odd_tail, even_tail, ms)
    return finish()


def _flash_scratch(nh, tq, tk, kdim, col_max=True):
    bufs = [pltpu.VMEM((kdim, nh * tq), BF16), pltpu.VMEM((tk, nh * tq), F32),
            pltpu.VMEM((tk, nh * tq), F32)]
    if col_max:
        bufs.append(pltpu.VMEM((2, 1, nh * tq), F32))
    return bufs + [pltpu.VMEM((nh, ACC_ROWS, tq), F32)]


def _fox_kernel(qt_ref, qf_ref, k_ref, kf_ref, vt_ref, o_ref, bm_sc, s0_sc, s1_sc, mx_sc, acc_sc):
    qi = pl.program_id(2)
    tq, tk = FOX_TQ, FOX_TK
    bm_sc[...] = jnp.zeros_like(bm_sc)
    for h in range(2):
        rows = slice(h * HEAD_DIM, (h + 1) * HEAD_DIM)
        bm_sc[rows, h * tq:(h + 1) * tq] = qt_ref[0, rows, :]
        bm_sc[LANES:LANES + QF_ROWS, h * tq:(h + 1) * tq] = qf_ref[0, 0, h]
    qpos = qi * tq + lax.broadcasted_iota(jnp.int32, (tk, tq), 1)

    def k_tile(k0):
        return jnp.concatenate([k_ref[0, pl.ds(k0, tk), :], kf_ref[0, 0, pl.ds(k0, tk), :]], axis=1)

    def v_rows(h, k0):
        return vt_ref[0, h * HEAD_DIM:(h + 1) * HEAD_DIM, pl.ds(k0, tk)]

    diag = (qi * tq) // tk
    ops = _flash_ops(bm_sc, (s0_sc, s1_sc), mx_sc, acc_sc, 2, tq, tk, k_tile, v_rows)
    outs = _flash_t(ops, 2, tq, tk, diag, lambda i: i, diag, lambda key: key <= qpos)
    o_ref[0] = jnp.concatenate(outs, axis=0).T.astype(o_ref.dtype)


def _fox(u3, ut, kfeat, qfeat, *, q_row, k_col, v_row):
    B, S, _ = u3.shape
    npair = FOX_HEADS // 2
    return pl.pallas_call(
        _fox_kernel,
        grid=(B, npair, S // FOX_TQ),
        in_specs=[
            pl.BlockSpec((1, LANES, FOX_TQ), lambda b, j, i: (b, q_row + j, i)),
            pl.BlockSpec((1, 1, 2, QF_ROWS, FOX_TQ), lambda b, j, i: (b, j, 0, 0, i)),
            pl.BlockSpec((1, S, LANES), lambda b, j, i: (b, 0, k_col + j)),
            pl.BlockSpec((1, 1, S, LANES), lambda b, j, i: (b, j, 0, 0)),
            pl.BlockSpec((1, LANES, S), lambda b, j, i: (b, v_row + j, 0)),
        ],
        out_specs=pl.BlockSpec((1, FOX_TQ, LANES), lambda b, j, i: (b, i, j)),
        out_shape=jax.ShapeDtypeStruct((B, S, FOX_WIDTH), BF16),
        scratch_shapes=_flash_scratch(2, FOX_TQ, FOX_TK, 2 * LANES),
        compiler_params=_cp(("parallel", "parallel", "arbitrary")),
        name="fox",
    )(ut, qfeat, u3, kfeat, ut)


def _ret_kernel(q_ref, k_ref, v_ref, inner_ref, cross_ref, kdec_ref, cd_ref, bd_ref, gn_ref,
                o_ref, state_sc):
    @pl.when(pl.program_id(1) == 0)
    def _():
        state_sc[...] = jnp.zeros_like(state_sc)

    low = _low_half((RET_CHUNK, LANES), 1)
    inv = 1.0 / HEAD_DIM
    for c, j in [(c, j) for c in range(RET_STEP) for j in range(RET_HEADS // 2)]:
        rows = slice(c * RET_CHUNK, (c + 1) * RET_CHUNK)
        cols = slice(j * LANES, (j + 1) * LANES)
        q, k, v = q_ref[0, rows, cols], k_ref[0, rows, cols], v_ref[0, rows, cols]
        zero = jnp.zeros_like(q)
        qa, qb = jnp.where(low, q, zero), jnp.where(low, zero, q)
        pa = (_dot_nt(qa, k) * inner_ref[j, 0]).astype(BF16)
        pb = (_dot_nt(qb, k) * inner_ref[j, 1]).astype(BF16)
        o_in = jnp.where(low, _dot(pa, v), _dot(pb, v))
        state = state_sc[j]
        o = o_in + _dot(q, state.astype(BF16)) * cross_ref[j]
        kd = (k.astype(F32) * kdec_ref[j]).astype(BF16)
        state_sc[j] = state * cd_ref[j] + _dot_tn(kd, v) * bd_ref[...]
        sa = jnp.sum(jnp.where(low, o, 0.0), axis=-1, keepdims=True)
        st = jnp.sum(o, axis=-1, keepdims=True)
        mu = jnp.where(low, sa, st - sa) * inv
        d = o - mu
        d2 = d * d
        va = jnp.sum(jnp.where(low, d2, 0.0), axis=-1, keepdims=True)
        vt = jnp.sum(d2, axis=-1, keepdims=True)
        var = jnp.where(low, va, vt - va) * inv
        o_ref[0, rows, cols] = (d * lax.rsqrt(var + GN_EPS) * gn_ref[:, cols]).astype(o_ref.dtype)


def _ret_constants():
    lg = np.log(1.0 - 2.0 ** (-5.0 - np.arange(RET_HEADS)))
    i = np.arange(RET_CHUNK)
    diff = i[:, None] - i[None, :]
    inner = np.where(diff[None] >= 0, np.exp(lg[:, None, None] * np.maximum(diff, 0)[None]), 0.0)
    cross = np.exp(lg[:, None] * (i[None, :] + 1))
    kdec = np.exp(lg[:, None] * (RET_CHUNK - 1 - i)[None, :])
    cdec = np.exp(lg * RET_CHUNK)
    npair = RET_HEADS // 2
    inner = inner.reshape(npair, 2, RET_CHUNK, RET_CHUNK)

    def lanes(a):
        a = a.reshape(npair, 2, RET_CHUNK)
        return np.repeat(a.transpose(0, 2, 1), HEAD_DIM, axis=2)

    bd = np.kron(np.eye(2), np.ones((HEAD_DIM, HEAD_DIM)))
    cd = np.repeat(cdec.reshape(npair, 2), HEAD_DIM, axis=1)[:, :, None] * bd[None]
    f = lambda a: jnp.asarray(a, F32)
    return f(inner), f(lanes(cross)), f(lanes(kdec)), f(cd), f(bd)


def _retention(u3, gn_g, *, q_col, k_col, v_col):
    B, S, _ = u3.shape
    C = RET_CHUNK
    rows = RET_STEP * C
    npair = RET_HEADS // 2
    inner, cross, kdec, cd, bd = _ret_constants()
    full = lambda shape: pl.BlockSpec(shape, lambda b, i: (0,) * len(shape))
    return pl.pallas_call(
        _ret_kernel,
        grid=(B, S // rows),
        in_specs=[
            pl.BlockSpec((1, rows, RET_WIDTH), lambda b, i: (b, i, q_col)),
            pl.BlockSpec((1, rows, RET_WIDTH), lambda b, i: (b, i, k_col)),
            pl.BlockSpec((1, rows, RET_WIDTH), lambda b, i: (b, i, v_col)),
            full((npair, 2, C, C)), full((npair, C, LANES)), full((npair, C, LANES)),
            full((npair, LANES, LANES)), full((LANES, LANES)), full((1, RET_WIDTH)),
        ],
        out_specs=pl.BlockSpec((1, rows, RET_WIDTH), lambda b, i: (b, i, 0)),
        out_shape=jax.ShapeDtypeStruct((B, S, RET_WIDTH), BF16),
        scratch_shapes=[pltpu.VMEM((npair, LANES, LANES), F32)],
        compiler_params=_cp(("parallel", "arbitrary")),
        name="retention",
    )(u3, u3, u3, inner, cross, kdec, cd, bd, gn_g.reshape(1, RET_WIDTH))


def _out0_kernel(of_ref, or_ref, z_ref, x_ref, w_ref, o_ref):
    z = _silu(z_ref[...].astype(F32))
    ya = (of_ref[...].astype(F32) * z[:, :FOX_WIDTH]).astype(BF16)
    yb = (or_ref[...].astype(F32) * z[:, FOX_WIDTH:]).astype(BF16)
    o_ref[...] = x_ref[...] + _dot(ya, w_ref[:FOX_WIDTH, :]) + _dot(yb, w_ref[FOX_WIDTH:, :])


def _out0(o_f, o_r, u, x2, w_out, *, tm=OUT_TM):
    N, D = x2.shape
    return pl.pallas_call(
        _out0_kernel,
        grid=(N // tm,),
        in_specs=[
            pl.BlockSpec((tm, FOX_WIDTH), lambda i: (i, 0)),
            pl.BlockSpec((tm, RET_WIDTH), lambda i: (i, 0)),
            pl.BlockSpec((tm, D), lambda i: (i, 0)),
            pl.BlockSpec((tm, D), lambda i: (i, 0)),
            pl.BlockSpec((D, D), lambda i: (0, 0)),
        ],
        out_specs=pl.BlockSpec((tm, D), lambda i: (i, 0)),
        out_shape=jax.ShapeDtypeStruct((N, D), F32),
        compiler_params=_cp(("parallel",)),
        name="out0",
    )(o_f, o_r, u, x2, w_out)


def _out1_kernel(a_ref, z_ref, x_ref, w_ref, g_ref, o_ref):
    z = _silu(z_ref[...].astype(F32))
    y = (a_ref[...].astype(F32) * z).astype(BF16)
    o_ref[...] = _rms(x_ref[...] + _dot(y, w_ref[...]), g_ref[...])


def _out1(o_a, u, x2, w_out, final_g, *, tm=OUT_TM):
    N, D = x2.shape
    row = pl.BlockSpec((tm, D), lambda i: (i, 0))
    return pl.pallas_call(
        _out1_kernel,
        grid=(N // tm,),
        in_specs=[row, row, row,
                  pl.BlockSpec((D, D), lambda i: (0, 0)),
                  pl.BlockSpec((1, D), lambda i: (0, 0))],
        out_specs=row,
        out_shape=jax.ShapeDtypeStruct((N, D), F32),
        compiler_params=_cp(("parallel",)),
        name="out1",
    )(o_a, u, x2, w_out, final_g.reshape(1, D))


def _compress_kernel(x_ref, pea_ref, peb_ref, wa_ref, wb_ref, w2_ref, *rest, transposed):
    x = x_ref[0].astype(F32)
    a = _dot((x + pea_ref[...]).astype(BF16), wa_ref[0])
    b = _dot((x + peb_ref[...]).astype(BF16), wb_ref[0])
    nseg = x.shape[0]
    pre = a + pltpu.roll(b, nseg - 1, 0)
    hid = _silu(pre).astype(BF16)
    if transposed:
        o_ref, = rest
        o_ref[0, 0] = _dot_nt(w2_ref[...], hid).astype(o_ref.dtype)
    else:
        feat_ref, o_ref = rest
        o_ref[0, 0] = (_dot(hid, w2_ref[...]) + feat_ref[...].astype(F32)).astype(o_ref.dtype)


def _compress(a3, pe, w1, w2, *, transposed):
    B, nseg, wid = a3.shape
    half = CMP_STRIDE * HEAD_DIM
    eye = jnp.eye(NSA_GROUPS, dtype=w1.dtype)

    def big(wh):
        w4 = wh.reshape(CMP_STRIDE, 1, HEAD_DIM, CMP_HIDDEN)
        sel = eye[:, None, :, None, None]
        return (sel * w4[None]).reshape(NSA_GROUPS, wid, CMP_HIDDEN).astype(BF16)

    def pe_big(p):
        return jnp.broadcast_to(p[:, None, :], (CMP_STRIDE, NSA_GROUPS, HEAD_DIM)).reshape(1, wid)

    args = [a3, pe_big(pe[:CMP_STRIDE]), pe_big(pe[CMP_STRIDE:]), big(w1[:half]), big(w1[half:])]
    in_specs = [
        pl.BlockSpec((1, nseg, wid), lambda b, g: (b, 0, 0)),
        pl.BlockSpec((1, wid), lambda b, g: (0, 0)),
        pl.BlockSpec((1, wid), lambda b, g: (0, 0)),
        pl.BlockSpec((1, wid, CMP_HIDDEN), lambda b, g: (g, 0, 0)),
        pl.BlockSpec((1, wid, CMP_HIDDEN), lambda b, g: (g, 0, 0)),
    ]
    if transposed:
        w2d = w2.T.astype(BF16)
        oshape, oblock = (B, NSA_GROUPS, HEAD_DIM, nseg), (1, 1, HEAD_DIM, nseg)
        args.append(w2d)
        in_specs.append(pl.BlockSpec(w2d.shape, lambda b, g: (0, 0)))
    else:
        w2d = jnp.pad(w2, ((0, 0), (0, LANES - HEAD_DIM))).astype(BF16)
        oshape, oblock = (B, NSA_GROUPS, nseg, LANES), (1, 1, nseg, LANES)
        feat = _pos_features(jnp.arange(nseg, dtype=jnp.int32) * CMP_STRIDE + (CMP_BLOCK - 1), LANES)
        args += [w2d, feat]
        in_specs += [pl.BlockSpec(w2d.shape, lambda b, g: (0, 0)),
                     pl.BlockSpec((nseg, LANES), lambda b, g: (0, 0))]
    return pl.pallas_call(
        functools.partial(_compress_kernel, transposed=transposed),
        grid=(B, NSA_GROUPS),
        in_specs=in_specs,
        out_specs=pl.BlockSpec(oblock, lambda b, g: (b, g, 0, 0)),
        out_shape=jax.ShapeDtypeStruct(oshape, BF16),
        compiler_params=_cp(("parallel", "parallel")),
        name="compress",
    )(*args)


def _slope_table():
    s = np.asarray(2.0 ** (-8.0 * (np.arange(NSA_HEADS) + 1) / NSA_HEADS), np.float32)
    sl = np.asarray(s.astype(np.float64) * LOG2E, np.float32)
    p1, p2, p3 = _np_pieces(sl)
    tab = np.zeros((NSA_HEADS, QF_ROWS), np.float32)
    for k, p in enumerate((p1, p1, p2, p2, p3, p3)):
        tab[:, k] = p
    tab[:, 6] = sl
    tab = np.broadcast_to(tab.reshape(NSA_GROUPS, NSA_HPG * QF_ROWS, 1),
                          (NSA_GROUPS, NSA_HPG * QF_ROWS, NSA_TQ))
    return jnp.asarray(tab)


def _pos_features(pos, width):
    pos = pos[:, None]
    lane = jnp.arange(width, dtype=jnp.int32)[None, :] % LANES
    hi = ((pos // SLC_BLOCK) * SLC_BLOCK).astype(F32)
    lo = (pos % SLC_BLOCK).astype(F32)
    k = lane - FEAT0
    f = jnp.where((k >= 0) & (k < 6), jnp.where(k % 2 == 0, hi, lo), 0.0)
    f = jnp.where((k >= 6) & (k < 9), 1.0, f)
    return f.astype(BF16)


def _nsa_queries(qt_ref, tab_ref, t0):
    tq = qt_ref.shape[2]
    r = lax.broadcasted_iota(jnp.int32, (QF_ROWS, tq), 0)
    t = (t0 + lax.broadcasted_iota(jnp.int32, (1, tq), 1)).astype(F32)
    zeros = jnp.zeros((LANES - HEAD_DIM - QF_ROWS, tq), BF16)
    out = []
    for i in range(NSA_HPG):
        tile = tab_ref[0, i * QF_ROWS:(i + 1) * QF_ROWS, :]
        a1, a2, a3 = _pieces(-(tile[6:7, :] * t))
        feat = jnp.where(r == 6, a1, jnp.where(r == 7, a2, jnp.where(r == 8, a3,
                                                                     jnp.where(r < 6, tile, 0.0))))
        out.append(jnp.concatenate([qt_ref[0, i * HEAD_DIM:(i + 1) * HEAD_DIM, :],
                                    feat.astype(BF16), zeros], axis=0))
    return out


def _gates_t(gt_ref, bg_ref, branch):
    gl = gt_ref[0] + bg_ref[...]
    return [_sigmoid(gl[N_BRANCH * i + branch:N_BRANCH * i + branch + 1, :]) for i in range(NSA_HPG)]


def _gated(outs_t, gates):
    return [o * gt for o, gt in zip(outs_t, gates)]


def _store_heads(o_ref, g):
    o_ref[0, :, :LANES] = jnp.concatenate(g[:2], axis=0).T.astype(o_ref.dtype)
    o_ref[0, :, LANES:] = jnp.concatenate(g[2:], axis=0).T.astype(o_ref.dtype)


def _nsa_specs(q_row):
    return dict(
        q=pl.BlockSpec((1, NSA_HPG * HEAD_DIM, NSA_TQ), lambda b, g, i: (b, q_row + g, i)),
        tab=pl.BlockSpec((1, NSA_HPG * QF_ROWS, NSA_TQ), lambda b, g, i: (g, 0, 0)),
        gt=pl.BlockSpec((1, GATE_ROWS, NSA_TQ), lambda b, g, i: (b, g, i)),
        bg=pl.BlockSpec((GATE_ROWS, 1), lambda b, g, i: (g, 0)),
        out=pl.BlockSpec((1, NSA_TQ, NSA_HPG * HEAD_DIM), lambda b, g, i: (b, i, g)),
    )


def _argmax_first(v, idx):
    n = v.shape[0]
    slabs = [(v[r:r + 8], idx[r:r + 8]) for r in range(0, n, 8)]
    while len(slabs) > 1:
        nxt = []
        for (va, ia), (vb, ib) in zip(slabs[0::2], slabs[1::2]):
            right = vb > va
            nxt.append((jnp.where(right, vb, va), jnp.where(right, ib, ia)))
        if len(slabs) % 2:
            nxt.append(slabs[-1])
        slabs = nxt
    v, idx = slabs[0]
    mx = jnp.max(v, axis=0, keepdims=True)
    first = jnp.min(jnp.where(v == mx, idx, float(n)), axis=0, keepdims=True)
    return mx, first


def _cmp_body(nc, t0, qh, kc_ref, vct_ref, mt_ref, grp_ref, gates, ocmp_sc, sel_sc, flag_sc):
    tq = NSA_TQ
    rows = nc * CMP_CHUNK
    full = max(rows - CMP_CHUNK - 8, 0)
    nseg = kc_ref.shape[2]
    kc = kc_ref[0, 0, :rows, :]
    vct = vct_ref[0, 0, :, :rows]
    t = t0 + lax.broadcasted_iota(jnp.int32, (1, tq), 1)
    cidx = full + lax.broadcasted_iota(jnp.int32, (rows - full, 1), 0)
    valid = (cidx * CMP_STRIDE + (CMP_BLOCK - 1) <= t) & (cidx < nseg - 1)
    psum = jnp.zeros((rows, tq), F32)
    ps = []
    s_all = _dot(kc, jnp.concatenate(qh, axis=1))
    for i in range(NSA_HPG):
        s = s_all[:, i * tq:(i + 1) * tq]
        s_last = jnp.where(valid, s[full:], NEG)
        m = jnp.max(s_last, axis=0, keepdims=True)
        if nc > 1:
            m = jnp.maximum(m, jnp.max(s[:full], axis=0, keepdims=True))
        e = jnp.where(valid, jnp.exp2(s_last - m), 0.0)
        if nc > 1:
            e = jnp.concatenate([jnp.exp2(s[:full] - m), e], axis=0)
        l = jnp.sum(e, axis=0, keepdims=True)
        p = e * jnp.where(l > 0.0, 1.0 / l, 0.0)
        psum = psum + p
        ps.append(p.astype(BF16))
    o_all = _dot(vct, jnp.concatenate(ps, axis=1))
    for i in range(NSA_HPG):
        ocmp_sc[i] = o_all[:, i * tq:(i + 1) * tq] * gates[i]
    ns = rows * CMP_STRIDE // SLC_BLOCK
    mt = mt_ref[:ns, :rows]
    imp = sum(_dot(mt, p.astype(BF16)) for p in _pieces(psum))
    blk = lax.broadcasted_iota(jnp.int32, (ns, 1), 0)
    cur = t // SLC_BLOCK
    bvalid = blk * SLC_BLOCK <= t
    forced = (blk == 0) | (blk == cur) | (blk == cur - 1)
    score = jnp.where(forced, -jnp.inf, jnp.where(bvalid, imp, NEG))
    blk_f = jnp.broadcast_to(blk.astype(F32), (ns, tq))
    work = score
    for _ in range(SLC_TOPK - N_FORCED):
        mx, first = _argmax_first(work, blk_f)
        work = jnp.where(blk_f == first, -jnp.inf, work)
    picked = (score > mx) | ((score == mx) & (blk_f <= first))
    selneg = jnp.where(bvalid & (forced | picked), 0.0, -MASK_BIG)
    if ns < NS_PAD:
        selneg = jnp.concatenate([selneg, jnp.full((NS_PAD - ns, tq), -MASK_BIG, F32)], axis=0)
    sel_sc[...] = selneg.astype(sel_sc.dtype)
    picked = jnp.where(selneg == 0.0, 1.0, 0.0).astype(BF16)
    used = _dot_nt(jnp.ones((8, tq), BF16), picked)
    used = jnp.where(used > 0.0, 1.0, 0.0).astype(BF16)
    flag_sc[...] = (_dot(used, grp_ref[...])[0:1] > 0.0).astype(jnp.int32)


def _cmp_to_slc_t(nseg, ns):
    c0 = np.arange(nseg)[:, None] * CMP_STRIDE
    s0 = np.arange(ns)[None, :] * SLC_BLOCK
    overlap = np.clip(np.minimum(c0 + CMP_BLOCK, s0 + SLC_BLOCK) - np.maximum(c0, s0), 0, None)
    m = overlap / CMP_STRIDE
    m[nseg - 1] = 0.0
    mt = np.zeros((NS_PAD, nseg))
    mt[:ns] = m.T
    return jnp.asarray(mt, BF16)


def _tile_groups():
    per = SLC_TK // SLC_BLOCK
    g = (np.arange(NS_PAD)[:, None] // per) == np.arange(NS_PAD)[None, :]
    return jnp.asarray(g, BF16)


def _window_branch(qh, t0, k_ref, vt_ref, bias_ref, bm_sc, s_bufs, acc_sc, last_before):
    tq, tk = NSA_TQ, WIN_TK
    bm_sc[...] = jnp.concatenate(qh, axis=1)
    _, qk_all, step, finish = _flash_ops(
        bm_sc, s_bufs, None, acc_sc, NSA_HPG, tq, tk,
        lambda k0: k_ref[0, pl.ds(k0, tk), :], lambda h, k0: vt_ref[0, :, pl.ds(k0, tk)])
    first = jnp.maximum(t0 - WINDOW, 0) // tk
    ntile = (WINDOW + tq) // tk
    qk_all(first, 0)
    ms = (jnp.full((1, tq), NEG, F32),) * NSA_HPG
    for j in range(ntile):
        last = j + 1 == ntile
        ms = step(first + j, j % 2, ms, nxt=None if last else first + j + 1,
                  bias=bias_ref[0, j * tk:(j + 1) * tk, :], before=last_before if last else None)
    return finish()


def _nsa_kernel(q_ref, kc_ref, vct_ref, mt_ref, grp_ref, k_ref, vt_ref, e_ref, kw_ref, vwt_ref,
                bias_ref, tab_ref, gt_ref, bg_ref, o_ref, bm_sc, s0_sc, s1_sc, mx_sc, acc_sc,
                wbm_sc, ws0_sc, ws1_sc, wacc_sc, ocmp_sc, sel_sc, flag_sc, tiles_sm):
    qi = pl.program_id(2)
    tq, tk = NSA_TQ, SLC_TK
    t0 = qi * tq
    diag = t0 // tk
    qh = _nsa_queries(q_ref, tab_ref, t0)
    gates = _gates_t(gt_ref, bg_ref, 0)
    last = t0 // (CMP_CHUNK * CMP_STRIDE)
    for nc in range(1, kc_ref.shape[2] // CMP_CHUNK + 1):
        pl.when(last == nc - 1)(functools.partial(
            _cmp_body, nc, t0, qh, kc_ref, vct_ref, mt_ref, grp_ref, gates, ocmp_sc, sel_sc, flag_sc))
    flags = flag_sc[...]
    n_un = jnp.int32(0)
    for j in range(k_ref.shape[1] // tk):
        tiles_sm[n_un] = jnp.int32(j)
        n_un = n_un + ((flags[0, j] > 0) & (j < diag)).astype(jnp.int32)
    tiles_sm[n_un] = diag
    selneg = sel_sc[...]
    bm_sc[...] = jnp.concatenate([jnp.concatenate([q, selneg], axis=0) for q in qh], axis=1)
    qpos = t0 + lax.broadcasted_iota(jnp.int32, (tk, tq), 1)

    def k_tile(k0):
        return jnp.concatenate([k_ref[0, pl.ds(k0, tk), :], e_ref[pl.ds(k0, tk), :]], axis=1)

    def v_rows(h, k0):
        return vt_ref[0, :, pl.ds(k0, tk)]

    ops = _flash_ops(bm_sc, (s0_sc, s1_sc), mx_sc, acc_sc, NSA_HPG, tq, tk, k_tile, v_rows)
    o_win = _gated(_window_branch(qh, t0, kw_ref, vwt_ref, bias_ref, wbm_sc, (ws0_sc, ws1_sc),
                                  wacc_sc, lambda h: ops[0](tiles_sm[0], 0, h)),
                   _gates_t(gt_ref, bg_ref, 2))
    outs = _flash_t(ops, NSA_HPG, tq, tk, n_un, lambda i: tiles_sm[i], diag,
                    lambda key: key <= qpos, first_done=True)
    o_slc = _gated(outs, _gates_t(gt_ref, bg_ref, 1))
    _store_heads(o_ref, [a + b + ocmp_sc[i] for i, (a, b) in enumerate(zip(o_slc, o_win))])


def _win_bias():
    tq, nk = NSA_TQ, WINDOW + NSA_TQ
    out = []
    for p in range(WINDOW // tq + 1):
        t0 = p * tq
        key = max(t0 - WINDOW, 0) + np.arange(nk)[:, None]
        qpos = t0 + np.arange(tq)[None, :]
        out.append(np.where((key <= qpos) & (key > qpos - WINDOW), 0.0, NEG))
    return jnp.asarray(np.stack(out), F32)


def _block_onehot(S):
    e = (np.arange(S)[:, None] // SLC_BLOCK) == np.arange(NS_PAD)[None, :]
    return jnp.asarray(e, BF16)


def _nsa_attention(u3, ut, kcmp, vcmp_t, gt, bg, *, q_row, ks_col, vs_row, kw_col, vw_row):
    B, S, _ = u3.shape
    nseg = kcmp.shape[2]
    sp = _nsa_specs(q_row)
    bias = _win_bias()
    npat = bias.shape[0]
    kspec = lambda col: pl.BlockSpec((1, S, LANES), lambda b, g, i: (b, 0, col + g))
    vspec = lambda row: pl.BlockSpec((1, HEAD_DIM, S), lambda b, g, i: (b, row + g, 0))
    const = lambda shape: pl.BlockSpec(shape, lambda b, g, i: (0,) * len(shape))
    return pl.pallas_call(
        _nsa_kernel,
        grid=(B, NSA_GROUPS, S // NSA_TQ),
        in_specs=[
            sp["q"],
            pl.BlockSpec((1, 1, nseg, LANES), lambda b, g, i: (b, g, 0, 0)),
            pl.BlockSpec((1, 1, HEAD_DIM, nseg), lambda b, g, i: (b, g, 0, 0)),
            const((NS_PAD, nseg)), const((NS_PAD, NS_PAD)),
            kspec(ks_col), vspec(vs_row), const((S, NS_PAD)),
            kspec(kw_col), vspec(vw_row),
            pl.BlockSpec((1,) + bias.shape[1:], lambda b, g, i: (jnp.minimum(i, npat - 1), 0, 0)),
            sp["tab"], sp["gt"], sp["bg"],
        ],
        out_specs=sp["out"],
        out_shape=jax.ShapeDtypeStruct((B, S, NSA_WIDTH), BF16),
        scratch_shapes=_flash_scratch(NSA_HPG, NSA_TQ, SLC_TK, 2 * LANES)
        + _flash_scratch(NSA_HPG, NSA_TQ, WIN_TK, LANES, col_max=False)
        + [pltpu.VMEM((NSA_HPG, HEAD_DIM, NSA_TQ), F32), pltpu.VMEM((NS_PAD, NSA_TQ), BF16),
           pltpu.VMEM((1, NS_PAD), jnp.int32), pltpu.SMEM((S // SLC_TK + 1,), jnp.int32)],
        compiler_params=_cp(("parallel", "parallel", "arbitrary")),
        name="nsa_attention",
    )(ut, kcmp, vcmp_t, _cmp_to_slc_t(nseg, S // SLC_BLOCK), _tile_groups(), u3, ut,
      _block_onehot(S), u3, ut, bias, _slope_table(), gt, bg)


def _aug_groups(w):
    d = w.shape[0]
    w = w.reshape(d, NSA_GROUPS, HEAD_DIM)
    return jnp.pad(w, ((0, 0), (0, 0), (0, LANES - HEAD_DIM))).reshape(d, NSA_GROUPS * LANES)


def _even_layer(x, norm_g, w_in, b_f, gn_g, w_out):
    B, S, D = x.shape
    qscale = HEAD_DIM ** -0.5 * LOG2E
    q_f, k_f, v_f, w_fl, q_r, k_r, v_r, z = jnp.split(
        w_in, np.cumsum([FOX_WIDTH] * 3 + [FOX_HEADS] + [RET_WIDTH] * 3).tolist(), axis=1)
    w = jnp.concatenate([z, k_f, q_r, k_r * HEAD_DIM ** -0.5, v_r], axis=1).astype(BF16)
    w_t = jnp.concatenate([q_f * qscale, v_f], axis=1).T.astype(BF16)
    x2 = x.reshape(B * S, D)
    u, ut = _proj(x2, norm_g, w, seq=S, w_t=[w_t], t_dtypes=[BF16])
    u3 = u.reshape(B, S, -1)
    kfeat, qfeat = _fgate(x, norm_g, w_fl, b_f, tile=min(512, S))
    o_f = _fox(u3, ut, kfeat, qfeat, q_row=0, k_col=D // LANES, v_row=FOX_WIDTH // LANES)
    rb = (D + FOX_WIDTH) // RET_WIDTH
    o_r = _retention(u3, gn_g, q_col=rb, k_col=rb + 1, v_col=rb + 2)
    out = _out0(o_f.reshape(B * S, -1), o_r.reshape(B * S, -1), u, x2, w_out.astype(BF16))
    return out.reshape(B, S, D)


def _odd_layer(x, norm_g, w_in, b_gate, pe_k, pe_v, wk1, wk2, wv1, wv2, w_out, final_g):
    B, S, D = x.shape
    assert S // SLC_BLOCK <= NS_PAD
    qscale = HEAD_DIM ** -0.5 * LOG2E
    sizes = [NSA_WIDTH] + [NSA_KV_WIDTH] * 6 + [NSA_HEADS * N_BRANCH]
    q, kc, vc, ks, vs, kw, vw, gl, z = jnp.split(w_in, np.cumsum(sizes).tolist(), axis=1)
    w = jnp.concatenate([z, kc, vc, _aug_groups(ks), _aug_groups(kw)], axis=1).astype(BF16)
    per_group = NSA_HPG * N_BRANCH
    glt = jnp.pad(gl.T.reshape(NSA_GROUPS, per_group, D), ((0, 0), (0, GATE_ROWS - per_group), (0, 0)))
    glt = glt.reshape(NSA_GROUPS * GATE_ROWS, D).astype(BF16)
    bg = jnp.pad(b_gate.reshape(NSA_GROUPS, per_group), ((0, 0), (0, GATE_ROWS - per_group)))
    bg = bg.reshape(NSA_GROUPS * GATE_ROWS, 1)
    w_vt = jnp.concatenate([q * qscale, vs, vw], axis=1).T.astype(BF16)
    x2 = x.reshape(B * S, D)
    first_k = (D + 2 * NSA_KV_WIDTH) // PROJ_TN
    u, kc_a, vc_a, ut, gt = _proj(
        x2, norm_g, w, seq=S, split_tile=D // PROJ_TN,
        addend=_pos_features(jnp.arange(S, dtype=jnp.int32), PROJ_TN), add_tiles=(first_k, first_k + 1),
        w_t=[w_vt, glt], t_dtypes=[BF16, F32])
    u3 = u.reshape(B, S, -1)
    nseg = S // CMP_STRIDE
    kcmp = _compress(kc_a.reshape(B, nseg, -1), pe_k, wk1, wk2, transposed=False)
    vcmp_t = _compress(vc_a.reshape(B, nseg, -1), pe_v, wv1, wv2, transposed=True)
    kb = (D + 2 * NSA_KV_WIDTH) // LANES
    vb = NSA_WIDTH // HEAD_DIM
    o_a = _nsa_attention(u3, ut, kcmp, vcmp_t, gt, bg, q_row=0, ks_col=kb, vs_row=vb,
                         kw_col=kb + NSA_GROUPS, vw_row=vb + NSA_GROUPS)
    out = _out1(o_a.reshape(B * S, -1), u, x2, w_out.astype(BF16), final_g)
    return out.reshape(B, S, D)


def kernel(x, even_norm_g, even_w_in, even_b_f, even_gn_g, even_w_out, odd_norm_g, odd_w_in,
           odd_b_gate, odd_pe_k, odd_pe_v, odd_wk1, odd_wk2, odd_wv1, odd_wv2, odd_w_out, final_g):
    x = _even_layer(x, even_norm_g[0], even_w_in[0], even_b_f[0], even_gn_g[0], even_w_out[0])
    return _odd_layer(x, odd_norm_g[0], odd_w_in[0], odd_b_gate[0], odd_pe_k[0], odd_pe_v[0],
                      odd_wk1[0], odd_wk2[0], odd_wv1[0], odd_wv2[0], odd_w_out[0], final_g)
```

```python
import functools
import math

import jax
import jax.numpy as jnp
import numpy as np
from jax import lax
from jax.experimental import pallas as pl
from jax.experimental.pallas import tpu as pltpu

D_MODEL = 1024
HEAD_DIM = 64
LANES = 128
FOX_HEADS = 8
RET_HEADS = 8
FOX_WIDTH = FOX_HEADS * HEAD_DIM
RET_WIDTH = RET_HEADS * HEAD_DIM
RET_CHUNK = 128
RET_STEP = 4
NSA_HEADS = 16
NSA_GROUPS = 4
NSA_HPG = NSA_HEADS // NSA_GROUPS
NSA_WIDTH = NSA_HEADS * HEAD_DIM
NSA_KV_WIDTH = NSA_GROUPS * HEAD_DIM
N_BRANCH = 3
GATE_ROWS = 16
CMP_BLOCK = 32
CMP_STRIDE = 16
CMP_HIDDEN = 256
CMP_CHUNK = 128
SLC_BLOCK = 64
SLC_TOPK = 16
N_FORCED = 3
NS_PAD = LANES
WINDOW = 512
RMS_EPS = 1e-6
GN_EPS = 1e-5
NEG = -1e30
FORCE_BONUS = 1e6
MASK_BIG = 2.0 ** 100
LOG2E = math.log2(math.e)
FEAT0 = HEAD_DIM
QF_ROWS = 16
ACC_ROWS = HEAD_DIM + 16

PROJ_TM = 1024
PROJ_TN = 512
FOX_TQ = 512
FOX_TK = 512
NSA_TQ = 512
SLC_TK = 512
WIN_TK = 256
OUT_TM = 1024
VMEM_LIMIT = 48 * 1024 * 1024

F32 = jnp.float32
BF16 = jnp.bfloat16


def _cp(sem, vmem=VMEM_LIMIT):
    return pltpu.CompilerParams(dimension_semantics=sem, vmem_limit_bytes=vmem)


def _dot(a, b):
    return jnp.dot(a, b, preferred_element_type=F32)


def _dot_nt(a, b):
    return lax.dot_general(a, b, (((1,), (1,)), ((), ())), preferred_element_type=F32)


def _dot_tn(a, b):
    return lax.dot_general(a, b, (((0,), (0,)), ((), ())), preferred_element_type=F32)


def _rms(x, g):
    return x * lax.rsqrt(jnp.mean(x * x, axis=-1, keepdims=True) + RMS_EPS) * g


def _silu(x):
    return x * (1.0 / (1.0 + jnp.exp(-x)))


def _sigmoid(x):
    return 1.0 / (1.0 + jnp.exp(-x))


def _low_half(shape, axis):
    return lax.broadcasted_iota(jnp.int32, shape, axis) < HEAD_DIM


def _pieces(v):
    p1 = v.astype(BF16).astype(F32)
    r = v - p1
    p2 = r.astype(BF16).astype(F32)
    p3 = (r - p2).astype(BF16).astype(F32)
    return p1, p2, p3


def _np_pieces(v):
    v = np.asarray(v, np.float64)
    bf = lambda a: np.asarray(a, np.float32).astype(BF16).astype(np.float64)
    p1 = bf(v)
    p2 = bf(v - p1)
    p3 = bf(v - p1 - p2)
    return p1, p2, p3


def _place(lane, cols):
    out = jnp.zeros(lane.shape, F32)
    for i, c in cols.items():
        out = jnp.where(lane == i, c, out)
    return out


def _fgate_kernel(x_ref, g_ref, wf_ref, b_ref, kf_ref, qf_ref, carry):
    @pl.when(pl.program_id(1) == 0)
    def _():
        carry[...] = jnp.zeros_like(carry)

    h = _rms(x_ref[0], g_ref[...])
    t = h.shape[0]
    h1 = h.astype(BF16)
    h2 = (h - h1.astype(F32)).astype(BF16)
    w = wf_ref[...]
    w1 = w.astype(BF16)
    w2 = (w - w1.astype(F32)).astype(BF16)
    nf = w.shape[1]
    r = _dot(jnp.concatenate([h1, h2], axis=0), jnp.concatenate([w1, w2], axis=1))
    f = r[:t, :nf] + r[:t, nf:] + r[t:, :nf] + b_ref[...]
    ls = jnp.minimum(f, 0.0) - jnp.log(1.0 + jnp.exp(-jnp.abs(f)))
    r = lax.broadcasted_iota(jnp.int32, (t, t), 0)
    c = lax.broadcasted_iota(jnp.int32, (t, t), 1)
    lower = jnp.where(c <= r, 1.0, 0.0).astype(BF16)
    r = _dot(lower, jnp.concatenate([p.astype(BF16) for p in _pieces(ls)], axis=1))
    cs = r[:, :nf] + r[:, nf:2 * nf] + r[:, 2 * nf:] + carry[...]
    carry[...] = cs[t - 1:t, :]
    cl = cs * LOG2E
    lane = lax.broadcasted_iota(jnp.int32, (t, LANES), 1)
    one = jnp.ones((t, 1), F32)
    for j in range(FOX_HEADS // 2):
        a1, a2, a3 = _pieces(cl[:, 2 * j:2 * j + 1])
        b1, b2, b3 = _pieces(cl[:, 2 * j + 1:2 * j + 2])
        kf = _place(lane, {0: a1, 1: a2, 2: a3, 3: one, 4: one, 5: one, 6: b1, 7: b2, 8: b3})
        kf_ref[0, j] = kf.astype(BF16)
        qa = _place(lane, {0: -one, 1: -one, 2: -one, 3: a1, 4: a2, 5: a3})
        qb = _place(lane, {3: b1, 4: b2, 5: b3, 6: -one, 7: -one, 8: -one})
        qf_ref[0, j, 0] = qa.T[:QF_ROWS].astype(BF16)
        qf_ref[0, j, 1] = qb.T[:QF_ROWS].astype(BF16)


def _fgate(x, g, wf, b_f, *, tile):
    B, S, D = x.shape
    npair = FOX_HEADS // 2
    return pl.pallas_call(
        _fgate_kernel,
        grid=(B, S // tile),
        in_specs=[
            pl.BlockSpec((1, tile, D), lambda b, s: (b, s, 0)),
            pl.BlockSpec((1, D), lambda b, s: (0, 0)),
            pl.BlockSpec((D, FOX_HEADS), lambda b, s: (0, 0)),
            pl.BlockSpec((1, FOX_HEADS), lambda b, s: (0, 0)),
        ],
        out_specs=[
            pl.BlockSpec((1, npair, tile, LANES), lambda b, s: (b, 0, s, 0)),
            pl.BlockSpec((1, npair, 2, QF_ROWS, tile), lambda b, s: (b, 0, 0, 0, s)),
        ],
        out_shape=[
            jax.ShapeDtypeStruct((B, npair, S, LANES), BF16),
            jax.ShapeDtypeStruct((B, npair, 2, QF_ROWS, S), BF16),
        ],
        scratch_shapes=[pltpu.VMEM((1, FOX_HEADS), F32)],
        compiler_params=_cp(("parallel", "arbitrary")),
        name="fgate",
    )(x, g.reshape(1, D), wf, b_f.reshape(1, FOX_HEADS))


def _proj_kernel(*refs, split_tile, add_tiles, n_t):
    it = iter(refs)
    x_ref, g_ref, w_ref = next(it), next(it), next(it)
    add_ref = next(it) if add_tiles else None
    wt_refs = [next(it) for _ in range(n_t)]
    u_ref = next(it)
    e_refs = [next(it), next(it)] if split_tile is not None else None
    ut_refs = [next(it) for _ in range(n_t)]
    h_sc = next(it)
    j = pl.program_id(1)

    @pl.when(j == 0)
    def _():
        h = _rms(x_ref[...], g_ref[...]).astype(BF16)
        h_sc[...] = h
        for wt_ref, ut_ref in zip(wt_refs, ut_refs):
            ut_ref[0] = _dot_nt(wt_ref[...], h).astype(ut_ref.dtype)

    acc = _dot(h_sc[...], w_ref[...])
    if add_tiles:
        is_add = functools.reduce(jnp.logical_or, [j == t for t in add_tiles])

        @pl.when(is_add)
        def _():
            u_ref[...] = (acc + add_ref[...].astype(F32)).astype(u_ref.dtype)

        @pl.when(jnp.logical_not(is_add))
        def _():
            u_ref[...] = acc.astype(u_ref.dtype)
    else:
        u_ref[...] = acc.astype(u_ref.dtype)
    if split_tile is not None:
        @pl.when(j == split_tile)
        def _():
            half = acc.shape[1] // 2
            e_refs[0][...] = acc[:, :half].astype(e_refs[0].dtype)
            e_refs[1][...] = acc[:, half:].astype(e_refs[1].dtype)


def _proj(x2, g, w, *, seq, split_tile=None, addend=None, add_tiles=(), w_t=(), t_dtypes=(),
          tm=PROJ_TM, tn=PROJ_TN):
    N, D = x2.shape
    W = w.shape[1]
    nbs = seq // tm
    in_specs = [
        pl.BlockSpec((tm, D), lambda i, j: (i, 0)),
        pl.BlockSpec((1, D), lambda i, j: (0, 0)),
        pl.BlockSpec((D, tn), lambda i, j: (0, j)),
    ]
    args = [x2, g.reshape(1, D), w]
    if add_tiles:
        in_specs.append(pl.BlockSpec((tm, tn), lambda i, j: (i % nbs, 0)))
        args.append(addend)
    out_shape = [jax.ShapeDtypeStruct((N, W), BF16)]
    out_specs = [pl.BlockSpec((tm, tn), lambda i, j: (i, j))]
    if split_tile is not None:
        out_shape += [jax.ShapeDtypeStruct((N, tn // 2), BF16)] * 2
        out_specs += [pl.BlockSpec((tm, tn // 2), lambda i, j: (i, 0))] * 2
    for wt, dt in zip(w_t, t_dtypes):
        rows = wt.shape[0]
        in_specs.append(pl.BlockSpec((rows, D), lambda i, j: (0, 0)))
        args.append(wt)
        out_shape.append(jax.ShapeDtypeStruct((N // seq, rows, seq), dt))
        out_specs.append(pl.BlockSpec((1, rows, tm), lambda i, j: (i // nbs, 0, i % nbs)))
    return pl.pallas_call(
        functools.partial(_proj_kernel, split_tile=split_tile, add_tiles=tuple(add_tiles),
                          n_t=len(w_t)),
        grid=(N // tm, W // tn),
        in_specs=in_specs,
        out_specs=out_specs,
        out_shape=out_shape,
        scratch_shapes=[pltpu.VMEM((tm, D), BF16)],
        compiler_params=_cp(("parallel", "arbitrary")),
        name="proj",
    )(*args)


def _flash_ops(bm_sc, s_bufs, mx_sc, acc_sc, nh, tq, tk, k_tile, v_rows):
    acc_sc[...] = jnp.zeros_like(acc_sc)
    ones = jnp.ones((ACC_ROWS - HEAD_DIM, tk), BF16)

    def qk_head(tile, slot, h):
        cols = slice(h * tq, (h + 1) * tq)
        s = _dot(k_tile(pl.multiple_of(tile * tk, tk)), bm_sc[:, cols])
        s_bufs[slot][:, cols] = s
        if mx_sc is not None:
            mx_sc[slot, :, cols] = jnp.max(s, axis=0, keepdims=True)

    def soft_head(tile, slot, m_old, h, valid, bias):
        cols = slice(h * tq, (h + 1) * tq)
        buf = s_bufs[slot]
        k0 = pl.multiple_of(tile * tk, tk)
        if valid is not None or bias is not None:
            s = buf[:, cols] + bias if valid is None else jnp.where(valid, buf[:, cols], NEG)
            m_new = jnp.maximum(m_old, jnp.max(s, axis=0, keepdims=True))
            p = jnp.exp2(s - m_new).astype(BF16)
        else:
            m_new = jnp.maximum(m_old, mx_sc[slot, :, cols])
            p = jnp.exp2(buf[:, cols] - m_new).astype(BF16)
        alpha = jnp.exp2(m_old - m_new)
        lhs = jnp.concatenate([v_rows(h, k0), ones], axis=0)
        acc_sc[h] = alpha * acc_sc[h] + _dot(lhs, p)
        return m_new

    def qk_all(tile, slot):
        for h in range(nh):
            qk_head(tile, slot, h)

    def step(cur, slot, ms, nxt=None, valid=None, bias=None, before=None):
        out = []
        for h in range(nh):
            if nxt is not None:
                qk_head(nxt, 1 - slot, h)
            if before is not None:
                before(h)
            out.append(soft_head(cur, slot, ms[h], h, valid, bias))
        return tuple(out)

    def finish():
        outs = []
        for h in range(nh):
            a = acc_sc[h]
            outs.append(a[:HEAD_DIM] * (1.0 / a[HEAD_DIM:HEAD_DIM + 1]))
        return outs

    return qk_head, qk_all, step, finish


def _flash_t(ops, nh, tq, tk, n_un, tile_of, diag, valid_fn, first_done=False):
    _, qk_all, step, finish = ops
    key_iota = lax.broadcasted_iota(jnp.int32, (tk, tq), 0)
    diag_valid = lambda: valid_fn(pl.multiple_of(diag * tk, tk) + key_iota)

    def pair(i, ms):
        t_a, t_b, t_c = tile_of(2 * i), tile_of(2 * i + 1), tile_of(2 * i + 2)
        return step(t_b, 1, step(t_a, 0, ms, nxt=t_b), nxt=t_c)

    def odd_tail(ms):
        return step(diag, 1, step(tile_of(n_un - 1), 0, ms, nxt=diag), valid=diag_valid())

    def even_tail(ms):
        return step(diag, 0, ms, valid=diag_valid())

    if not first_done:
        qk_all(tile_of(0), 0)
    ms = (jnp.full((1, tq), NEG, F32),) * nh
    ms = lax.fori_loop(0, n_un // 2, pair, ms)
    lax.cond(n_un % 2 == 1, odd_tail, even_tail, ms)
    return finish()


def _flash_scratch(nh, tq, tk, kdim, col_max=True):
    bufs = [pltpu.VMEM((kdim, nh * tq), BF16), pltpu.VMEM((tk, nh * tq), F32),
            pltpu.VMEM((tk, nh * tq), F32)]
    if col_max:
        bufs.append(pltpu.VMEM((2, 1, nh * tq), F32))
    return bufs + [pltpu.VMEM((nh, ACC_ROWS, tq), F32)]


def _fox_kernel(qt_ref, qf_ref, k_ref, kf_ref, vt_ref, o_ref, bm_sc, s0_sc, s1_sc, mx_sc, acc_sc):
    qi = pl.program_id(2)
    tq, tk = FOX_TQ, FOX_TK
    bm_sc[...] = jnp.zeros_like(bm_sc)
    for h in range(2):
        rows = slice(h * HEAD_DIM, (h + 1) * HEAD_DIM)
        bm_sc[rows, h * tq:(h + 1) * tq] = qt_ref[0, rows, :]
        bm_sc[LANES:LANES + QF_ROWS, h * tq:(h + 1) * tq] = qf_ref[0, 0, h]
    qpos = qi * tq + lax.broadcasted_iota(jnp.int32, (tk, tq), 1)

    def k_tile(k0):
        return jnp.concatenate([k_ref[0, pl.ds(k0, tk), :], kf_ref[0, 0, pl.ds(k0, tk), :]], axis=1)

    def v_rows(h, k0):
        return vt_ref[0, h * HEAD_DIM:(h + 1) * HEAD_DIM, pl.ds(k0, tk)]

    diag = (qi * tq) // tk
    ops = _flash_ops(bm_sc, (s0_sc, s1_sc), mx_sc, acc_sc, 2, tq, tk, k_tile, v_rows)
    outs = _flash_t(ops, 2, tq, tk, diag, lambda i: i, diag, lambda key: key <= qpos)
    o_ref[0] = jnp.concatenate(outs, axis=0).T.astype(o_ref.dtype)


def _fox(u3, ut, kfeat, qfeat, *, q_row, k_col, v_row):
    B, S, _ = u3.shape
    npair = FOX_HEADS // 2
    return pl.pallas_call(
        _fox_kernel,
        grid=(B, npair, S // FOX_TQ),
        in_specs=[
            pl.BlockSpec((1, LANES, FOX_TQ), lambda b, j, i: (b, q_row + j, i)),
            pl.BlockSpec((1, 1, 2, QF_ROWS, FOX_TQ), lambda b, j, i: (b, j, 0, 0, i)),
            pl.BlockSpec((1, S, LANES), lambda b, j, i: (b, 0, k_col + j)),
            pl.BlockSpec((1, 1, S, LANES), lambda b, j, i: (b, j, 0, 0)),
            pl.BlockSpec((1, LANES, S), lambda b, j, i: (b, v_row + j, 0)),
        ],
        out_specs=pl.BlockSpec((1, FOX_TQ, LANES), lambda b, j, i: (b, i, j)),
        out_shape=jax.ShapeDtypeStruct((B, S, FOX_WIDTH), BF16),
        scratch_shapes=_flash_scratch(2, FOX_TQ, FOX_TK, 2 * LANES),
        compiler_params=_cp(("parallel", "parallel", "arbitrary")),
        name="fox",
    )(ut, qfeat, u3, kfeat, ut)


def _ret_kernel(q_ref, k_ref, v_ref, inner_ref, cross_ref, kdec_ref, cd_ref, bd_ref, gn_ref,
                o_ref, state_sc):
    @pl.when(pl.program_id(1) == 0)
    def _():
        state_sc[...] = jnp.zeros_like(state_sc)

    low = _low_half((RET_CHUNK, LANES), 1)
    inv = 1.0 / HEAD_DIM
    for c, j in [(c, j) for c in range(RET_STEP) for j in range(RET_HEADS // 2)]:
        rows = slice(c * RET_CHUNK, (c + 1) * RET_CHUNK)
        cols = slice(j * LANES, (j + 1) * LANES)
        q, k, v = q_ref[0, rows, cols], k_ref[0, rows, cols], v_ref[0, rows, cols]
        zero = jnp.zeros_like(q)
        qa, qb = jnp.where(low, q, zero), jnp.where(low, zero, q)
        pa = (_dot_nt(qa, k) * inner_ref[j, 0]).astype(BF16)
        pb = (_dot_nt(qb, k) * inner_ref[j, 1]).astype(BF16)
        o_in = jnp.where(low, _dot(pa, v), _dot(pb, v))
        state = state_sc[j]
        o = o_in + _dot(q, state.astype(BF16)) * cross_ref[j]
        kd = (k.astype(F32) * kdec_ref[j]).astype(BF16)
        state_sc[j] = state * cd_ref[j] + _dot_tn(kd, v) * bd_ref[...]
        sa = jnp.sum(jnp.where(low, o, 0.0), axis=-1, keepdims=True)
        st = jnp.sum(o, axis=-1, keepdims=True)
        mu = jnp.where(low, sa, st - sa) * inv
        d = o - mu
        d2 = d * d
        va = jnp.sum(jnp.where(low, d2, 0.0), axis=-1, keepdims=True)
        vt = jnp.sum(d2, axis=-1, keepdims=True)
        var = jnp.where(low, va, vt - va) * inv
        o_ref[0, rows, cols] = (d * lax.rsqrt(var + GN_EPS) * gn_ref[:, cols]).astype(o_ref.dtype)


def _ret_constants():
    lg = np.log(1.0 - 2.0 ** (-5.0 - np.arange(RET_HEADS)))
    i = np.arange(RET_CHUNK)
    diff = i[:, None] - i[None, :]
    inner = np.where(diff[None] >= 0, np.exp(lg[:, None, None] * np.maximum(diff, 0)[None]), 0.0)
    cross = np.exp(lg[:, None] * (i[None, :] + 1))
    kdec = np.exp(lg[:, None] * (RET_CHUNK - 1 - i)[None, :])
    cdec = np.exp(lg * RET_CHUNK)
    npair = RET_HEADS // 2
    inner = inner.reshape(npair, 2, RET_CHUNK, RET_CHUNK)

    def lanes(a):
        a = a.reshape(npair, 2, RET_CHUNK)
        return np.repeat(a.transpose(0, 2, 1), HEAD_DIM, axis=2)

    bd = np.kron(np.eye(2), np.ones((HEAD_DIM, HEAD_DIM)))
    cd = np.repeat(cdec.reshape(npair, 2), HEAD_DIM, axis=1)[:, :, None] * bd[None]
    f = lambda a: jnp.asarray(a, F32)
    return f(inner), f(lanes(cross)), f(lanes(kdec)), f(cd), f(bd)


def _retention(u3, gn_g, *, q_col, k_col, v_col):
    B, S, _ = u3.shape
    C = RET_CHUNK
    rows = RET_STEP * C
    npair = RET_HEADS // 2
    inner, cross, kdec, cd, bd = _ret_constants()
    full = lambda shape: pl.BlockSpec(shape, lambda b, i: (0,) * len(shape))
    return pl.pallas_call(
        _ret_kernel,
        grid=(B, S // rows),
        in_specs=[
            pl.BlockSpec((1, rows, RET_WIDTH), lambda b, i: (b, i, q_col)),
            pl.BlockSpec((1, rows, RET_WIDTH), lambda b, i: (b, i, k_col)),
            pl.BlockSpec((1, rows, RET_WIDTH), lambda b, i: (b, i, v_col)),
            full((npair, 2, C, C)), full((npair, C, LANES)), full((npair, C, LANES)),
            full((npair, LANES, LANES)), full((LANES, LANES)), full((1, RET_WIDTH)),
        ],
        out_specs=pl.BlockSpec((1, rows, RET_WIDTH), lambda b, i: (b, i, 0)),
        out_shape=jax.ShapeDtypeStruct((B, S, RET_WIDTH), BF16),
        scratch_shapes=[pltpu.VMEM((npair, LANES, LANES), F32)],
        compiler_params=_cp(("parallel", "arbitrary")),
        name="retention",
    )(u3, u3, u3, inner, cross, kdec, cd, bd, gn_g.reshape(1, RET_WIDTH))


def _out0_kernel(of_ref, or_ref, z_ref, x_ref, w_ref, o_ref):
    z = _silu(z_ref[...].astype(F32))
    ya = (of_ref[...].astype(F32) * z[:, :FOX_WIDTH]).astype(BF16)
    yb = (or_ref[...].astype(F32) * z[:, FOX_WIDTH:]).astype(BF16)
    o_ref[...] = x_ref[...] + _dot(ya, w_ref[:FOX_WIDTH, :]) + _dot(yb, w_ref[FOX_WIDTH:, :])


def _out0(o_f, o_r, u, x2, w_out, *, tm=OUT_TM):
    N, D = x2.shape
    return pl.pallas_call(
        _out0_kernel,
        grid=(N // tm,),
        in_specs=[
            pl.BlockSpec((tm, FOX_WIDTH), lambda i: (i, 0)),
            pl.BlockSpec((tm, RET_WIDTH), lambda i: (i, 0)),
            pl.BlockSpec((tm, D), lambda i: (i, 0)),
            pl.BlockSpec((tm, D), lambda i: (i, 0)),
            pl.BlockSpec((D, D), lambda i: (0, 0)),
        ],
        out_specs=pl.BlockSpec((tm, D), lambda i: (i, 0)),
        out_shape=jax.ShapeDtypeStruct((N, D), F32),
        compiler_params=_cp(("parallel",)),
        name="out0",
    )(o_f, o_r, u, x2, w_out)


def _out1_kernel(oc_ref, os_ref, z_ref, x_ref, w_ref, g_ref, o_ref):
    z = _silu(z_ref[...].astype(F32))
    y = ((oc_ref[...].astype(F32) + os_ref[...].astype(F32)) * z).astype(BF16)
    o_ref[...] = _rms(x_ref[...] + _dot(y, w_ref[...]), g_ref[...])


def _out1(o_c, o_s, u, x2, w_out, final_g, *, tm=OUT_TM):
    N, D = x2.shape
    row = pl.BlockSpec((tm, D), lambda i: (i, 0))
    return pl.pallas_call(
        _out1_kernel,
        grid=(N // tm,),
        in_specs=[row, row, row, row,
                  pl.BlockSpec((D, D), lambda i: (0, 0)),
                  pl.BlockSpec((1, D), lambda i: (0, 0))],
        out_specs=row,
        out_shape=jax.ShapeDtypeStruct((N, D), F32),
        compiler_params=_cp(("parallel",)),
        name="out1",
    )(o_c, o_s, u, x2, w_out, final_g.reshape(1, D))


def _compress_kernel(x_ref, pea_ref, peb_ref, wa_ref, wb_ref, w2_ref, *rest, transposed):
    x = x_ref[0].astype(F32)
    a = _dot((x + pea_ref[...]).astype(BF16), wa_ref[0])
    b = _dot((x + peb_ref[...]).astype(BF16), wb_ref[0])
    nseg = x.shape[0]
    pre = a + pltpu.roll(b, nseg - 1, 0)
    hid = _silu(pre).astype(BF16)
    if transposed:
        o_ref, = rest
        o_ref[0, 0] = _dot_nt(w2_ref[...], hid).astype(o_ref.dtype)
    else:
        feat_ref, o_ref = rest
        o_ref[0, 0] = (_dot(hid, w2_ref[...]) + feat_ref[...].astype(F32)).astype(o_ref.dtype)


def _compress(a3, pe, w1, w2, *, transposed):
    B, nseg, wid = a3.shape
    half = CMP_STRIDE * HEAD_DIM
    eye = jnp.eye(NSA_GROUPS, dtype=w1.dtype)

    def big(wh):
        w4 = wh.reshape(CMP_STRIDE, 1, HEAD_DIM, CMP_HIDDEN)
        sel = eye[:, None, :, None, None]
        return (sel * w4[None]).reshape(NSA_GROUPS, wid, CMP_HIDDEN).astype(BF16)

    def pe_big(p):
        return jnp.broadcast_to(p[:, None, :], (CMP_STRIDE, NSA_GROUPS, HEAD_DIM)).reshape(1, wid)

    args = [a3, pe_big(pe[:CMP_STRIDE]), pe_big(pe[CMP_STRIDE:]), big(w1[:half]), big(w1[half:])]
    in_specs = [
        pl.BlockSpec((1, nseg, wid), lambda b, g: (b, 0, 0)),
        pl.BlockSpec((1, wid), lambda b, g: (0, 0)),
        pl.BlockSpec((1, wid), lambda b, g: (0, 0)),
        pl.BlockSpec((1, wid, CMP_HIDDEN), lambda b, g: (g, 0, 0)),
        pl.BlockSpec((1, wid, CMP_HIDDEN), lambda b, g: (g, 0, 0)),
    ]
    if transposed:
        w2d = w2.T.astype(BF16)
        oshape, oblock = (B, NSA_GROUPS, HEAD_DIM, nseg), (1, 1, HEAD_DIM, nseg)
        args.append(w2d)
        in_specs.append(pl.BlockSpec(w2d.shape, lambda b, g: (0, 0)))
    else:
        w2d = jnp.pad(w2, ((0, 0), (0, LANES - HEAD_DIM))).astype(BF16)
        oshape, oblock = (B, NSA_GROUPS, nseg, LANES), (1, 1, nseg, LANES)
        feat = _pos_features(jnp.arange(nseg, dtype=jnp.int32) * CMP_STRIDE + (CMP_BLOCK - 1), LANES)
        args += [w2d, feat]
        in_specs += [pl.BlockSpec(w2d.shape, lambda b, g: (0, 0)),
                     pl.BlockSpec((nseg, LANES), lambda b, g: (0, 0))]
    return pl.pallas_call(
        functools.partial(_compress_kernel, transposed=transposed),
        grid=(B, NSA_GROUPS),
        in_specs=in_specs,
        out_specs=pl.BlockSpec(oblock, lambda b, g: (b, g, 0, 0)),
        out_shape=jax.ShapeDtypeStruct(oshape, BF16),
        compiler_params=_cp(("parallel", "parallel")),
        name="compress",
    )(*args)


def _slope_table():
    s = np.asarray(2.0 ** (-8.0 * (np.arange(NSA_HEADS) + 1) / NSA_HEADS), np.float32)
    sl = np.asarray(s.astype(np.float64) * LOG2E, np.float32)
    p1, p2, p3 = _np_pieces(sl)
    tab = np.zeros((NSA_HEADS, QF_ROWS), np.float32)
    for k, p in enumerate((p1, p1, p2, p2, p3, p3)):
        tab[:, k] = p
    tab[:, 6] = sl
    tab = np.broadcast_to(tab.reshape(NSA_GROUPS, NSA_HPG * QF_ROWS, 1),
                          (NSA_GROUPS, NSA_HPG * QF_ROWS, NSA_TQ))
    return jnp.asarray(tab)


def _pos_features(pos, width):
    pos = pos[:, None]
    lane = jnp.arange(width, dtype=jnp.int32)[None, :] % LANES
    hi = ((pos // SLC_BLOCK) * SLC_BLOCK).astype(F32)
    lo = (pos % SLC_BLOCK).astype(F32)
    k = lane - FEAT0
    f = jnp.where((k >= 0) & (k < 6), jnp.where(k % 2 == 0, hi, lo), 0.0)
    f = jnp.where((k >= 6) & (k < 9), 1.0, f)
    return f.astype(BF16)


def _nsa_queries(qt_ref, tab_ref, t0):
    tq = qt_ref.shape[2]
    r = lax.broadcasted_iota(jnp.int32, (QF_ROWS, tq), 0)
    t = (t0 + lax.broadcasted_iota(jnp.int32, (1, tq), 1)).astype(F32)
    zeros = jnp.zeros((LANES - HEAD_DIM - QF_ROWS, tq), BF16)
    out = []
    for i in range(NSA_HPG):
        tile = tab_ref[0, i * QF_ROWS:(i + 1) * QF_ROWS, :]
        a1, a2, a3 = _pieces(-(tile[6:7, :] * t))
        feat = jnp.where(r == 6, a1, jnp.where(r == 7, a2, jnp.where(r == 8, a3,
                                                                     jnp.where(r < 6, tile, 0.0))))
        out.append(jnp.concatenate([qt_ref[0, i * HEAD_DIM:(i + 1) * HEAD_DIM, :],
                                    feat.astype(BF16), zeros], axis=0))
    return out


def _gates_t(gt_ref, bg_ref, branch):
    gl = gt_ref[0] + bg_ref[...]
    return [_sigmoid(gl[N_BRANCH * i + branch:N_BRANCH * i + branch + 1, :]) for i in range(NSA_HPG)]


def _gated(outs_t, gates):
    return [o * gt for o, gt in zip(outs_t, gates)]


def _store_heads(o_ref, g):
    o_ref[0, :, :LANES] = jnp.concatenate(g[:2], axis=0).T.astype(o_ref.dtype)
    o_ref[0, :, LANES:] = jnp.concatenate(g[2:], axis=0).T.astype(o_ref.dtype)


def _nsa_specs(q_row):
    return dict(
        q=pl.BlockSpec((1, NSA_HPG * HEAD_DIM, NSA_TQ), lambda b, g, i: (b, q_row + g, i)),
        tab=pl.BlockSpec((1, NSA_HPG * QF_ROWS, NSA_TQ), lambda b, g, i: (g, 0, 0)),
        gt=pl.BlockSpec((1, GATE_ROWS, NSA_TQ), lambda b, g, i: (b, g, i)),
        bg=pl.BlockSpec((GATE_ROWS, 1), lambda b, g, i: (g, 0)),
        out=pl.BlockSpec((1, NSA_TQ, NSA_HPG * HEAD_DIM), lambda b, g, i: (b, i, g)),
    )


def _argmax_first(v, idx):
    n = v.shape[0]
    slabs = [(v[r:r + 8], idx[r:r + 8]) for r in range(0, n, 8)]
    while len(slabs) > 1:
        nxt = []
        for (va, ia), (vb, ib) in zip(slabs[0::2], slabs[1::2]):
            right = vb > va
            nxt.append((jnp.where(right, vb, va), jnp.where(right, ib, ia)))
        if len(slabs) % 2:
            nxt.append(slabs[-1])
        slabs = nxt
    v, idx = slabs[0]
    mx = jnp.max(v, axis=0, keepdims=True)
    first = jnp.min(jnp.where(v == mx, idx, float(n)), axis=0, keepdims=True)
    return mx, first


def _cmp_body(nc, t0, qh, kc_ref, vct_ref, mt_ref, grp_ref, gates, o_ref, sel_ref, flag_ref):
    tq = NSA_TQ
    rows = nc * CMP_CHUNK
    full = max(rows - CMP_CHUNK - 8, 0)
    nseg = kc_ref.shape[2]
    kc = kc_ref[0, 0, :rows, :]
    vct = vct_ref[0, 0, :, :rows]
    t = t0 + lax.broadcasted_iota(jnp.int32, (1, tq), 1)
    cidx = full + lax.broadcasted_iota(jnp.int32, (rows - full, 1), 0)
    valid = (cidx * CMP_STRIDE + (CMP_BLOCK - 1) <= t) & (cidx < nseg - 1)
    psum = jnp.zeros((rows, tq), F32)
    ps = []
    s_all = _dot(kc, jnp.concatenate(qh, axis=1))
    for i in range(NSA_HPG):
        s = s_all[:, i * tq:(i + 1) * tq]
        s_last = jnp.where(valid, s[full:], NEG)
        m = jnp.max(s_last, axis=0, keepdims=True)
        if nc > 1:
            m = jnp.maximum(m, jnp.max(s[:full], axis=0, keepdims=True))
        e = jnp.where(valid, jnp.exp2(s_last - m), 0.0)
        if nc > 1:
            e = jnp.concatenate([jnp.exp2(s[:full] - m), e], axis=0)
        l = jnp.sum(e, axis=0, keepdims=True)
        p = e * jnp.where(l > 0.0, 1.0 / l, 0.0)
        psum = psum + p
        ps.append(p.astype(BF16))
    o_all = _dot(vct, jnp.concatenate(ps, axis=1))
    _store_heads(o_ref, _gated([o_all[:, i * tq:(i + 1) * tq] for i in range(NSA_HPG)], gates))
    ns = rows * CMP_STRIDE // SLC_BLOCK
    mt = mt_ref[:ns, :rows]
    imp = sum(_dot(mt, p.astype(BF16)) for p in _pieces(psum))
    blk = lax.broadcasted_iota(jnp.int32, (ns, 1), 0)
    cur = t // SLC_BLOCK
    bvalid = blk * SLC_BLOCK <= t
    forced = (blk == 0) | (blk == cur) | (blk == cur - 1)
    score = jnp.where(forced, -jnp.inf, jnp.where(bvalid, imp, NEG))
    blk_f = jnp.broadcast_to(blk.astype(F32), (ns, tq))
    work = score
    for _ in range(SLC_TOPK - N_FORCED):
        mx, first = _argmax_first(work, blk_f)
        work = jnp.where(blk_f == first, -jnp.inf, work)
    picked = (score > mx) | ((score == mx) & (blk_f <= first))
    selneg = jnp.where(bvalid & (forced | picked), 0.0, -MASK_BIG)
    if ns < NS_PAD:
        selneg = jnp.concatenate([selneg, jnp.full((NS_PAD - ns, tq), -MASK_BIG, F32)], axis=0)
    sel_ref[0, 0] = selneg.astype(sel_ref.dtype)
    picked = jnp.where(selneg == 0.0, 1.0, 0.0).astype(BF16)
    used = _dot_nt(jnp.ones((8, tq), BF16), picked)
    used = jnp.where(used > 0.0, 1.0, 0.0).astype(BF16)
    flag_ref[0] = (_dot(used, grp_ref[...])[0:1] > 0.0).astype(jnp.int32)


def _cmp_kernel(q_ref, kc_ref, vct_ref, mt_ref, grp_ref, tab_ref, gt_ref, bg_ref,
                o_ref, sel_ref, flag_ref):
    t0 = pl.program_id(2) * NSA_TQ
    qh = _nsa_queries(q_ref, tab_ref, t0)
    gates = _gates_t(gt_ref, bg_ref, 0)
    nchunk = kc_ref.shape[2] // CMP_CHUNK
    last = t0 // (CMP_CHUNK * CMP_STRIDE)
    for nc in range(1, nchunk + 1):
        pl.when(last == nc - 1)(functools.partial(
            _cmp_body, nc, t0, qh, kc_ref, vct_ref, mt_ref, grp_ref, gates, o_ref, sel_ref, flag_ref))


def _cmp_to_slc_t(nseg, ns):
    c0 = np.arange(nseg)[:, None] * CMP_STRIDE
    s0 = np.arange(ns)[None, :] * SLC_BLOCK
    overlap = np.clip(np.minimum(c0 + CMP_BLOCK, s0 + SLC_BLOCK) - np.maximum(c0, s0), 0, None)
    m = overlap / CMP_STRIDE
    m[nseg - 1] = 0.0
    mt = np.zeros((NS_PAD, nseg))
    mt[:ns] = m.T
    return jnp.asarray(mt, BF16)


def _tile_groups():
    per = SLC_TK // SLC_BLOCK
    g = (np.arange(NS_PAD)[:, None] // per) == np.arange(NS_PAD)[None, :]
    return jnp.asarray(g, BF16)


def _cmp_attention(ut, kcmp, vcmp_t, gt, bg, *, q_row):
    B, _, S = ut.shape
    nseg = kcmp.shape[2]
    nq = S // NSA_TQ
    sp = _nsa_specs(q_row)
    return pl.pallas_call(
        _cmp_kernel,
        grid=(B, NSA_GROUPS, S // NSA_TQ),
        in_specs=[
            sp["q"],
            pl.BlockSpec((1, 1, nseg, LANES), lambda b, g, i: (b, g, 0, 0)),
            pl.BlockSpec((1, 1, HEAD_DIM, nseg), lambda b, g, i: (b, g, 0, 0)),
            pl.BlockSpec((NS_PAD, nseg), lambda b, g, i: (0, 0)),
            pl.BlockSpec((NS_PAD, NS_PAD), lambda b, g, i: (0, 0)),
            sp["tab"], sp["gt"], sp["bg"],
        ],
        out_specs=[sp["out"], pl.BlockSpec((1, 1, NS_PAD, NSA_TQ), lambda b, g, i: (b, g, 0, i)),
                   pl.BlockSpec((1, 1, NS_PAD), lambda b, g, i: ((b * NSA_GROUPS + g) * nq + i, 0, 0))],
        out_shape=[jax.ShapeDtypeStruct((B, S, NSA_WIDTH), BF16),
                   jax.ShapeDtypeStruct((B, NSA_GROUPS, NS_PAD, S), BF16),
                   jax.ShapeDtypeStruct((B * NSA_GROUPS * nq, 1, NS_PAD), jnp.int32)],
        compiler_params=_cp(("parallel", "parallel", "arbitrary")),
        name="cmp_attention",
    )(ut, kcmp, vcmp_t, _cmp_to_slc_t(nseg, S // SLC_BLOCK), _tile_groups(), _slope_table(), gt, bg)


def _window_branch(qh, t0, k_ref, vt_ref, bias_ref, bm_sc, s_bufs, acc_sc, last_before):
    tq, tk = NSA_TQ, WIN_TK
    bm_sc[...] = jnp.concatenate(qh, axis=1)
    _, qk_all, step, finish = _flash_ops(
        bm_sc, s_bufs, None, acc_sc, NSA_HPG, tq, tk,
        lambda k0: k_ref[0, pl.ds(k0, tk), :], lambda h, k0: vt_ref[0, :, pl.ds(k0, tk)])
    first = jnp.maximum(t0 - WINDOW, 0) // tk
    ntile = (WINDOW + tq) // tk
    qk_all(first, 0)
    ms = (jnp.full((1, tq), NEG, F32),) * NSA_HPG
    for j in range(ntile):
        last = j + 1 == ntile
        ms = step(first + j, j % 2, ms, nxt=None if last else first + j + 1,
                  bias=bias_ref[0, j * tk:(j + 1) * tk, :], before=last_before if last else None)
    return finish()


def _slc_win_kernel(fl_ref, q_ref, k_ref, vt_ref, sel_ref, e_ref, kw_ref, vwt_ref, bias_ref,
                    tab_ref, gt_ref, bg_ref, o_ref, bm_sc, s0_sc, s1_sc, mx_sc, acc_sc,
                    wbm_sc, ws0_sc, ws1_sc, wacc_sc, tiles_sm):
    qi = pl.program_id(2)
    tq, tk = NSA_TQ, SLC_TK
    t0 = qi * tq
    diag = t0 // tk
    row = (pl.program_id(0) * NSA_GROUPS + pl.program_id(1)) * pl.num_programs(2) + qi
    n_un = jnp.int32(0)
    for j in range(k_ref.shape[1] // tk):
        tiles_sm[n_un] = jnp.int32(j)
        n_un = n_un + ((fl_ref[row, j] > 0) & (j < diag)).astype(jnp.int32)
    tiles_sm[n_un] = diag
    qh = _nsa_queries(q_ref, tab_ref, t0)
    selneg = sel_ref[0, 0]
    bm_sc[...] = jnp.concatenate([jnp.concatenate([q, selneg], axis=0) for q in qh], axis=1)
    qpos = t0 + lax.broadcasted_iota(jnp.int32, (tk, tq), 1)

    def k_tile(k0):
        return jnp.concatenate([k_ref[0, pl.ds(k0, tk), :], e_ref[pl.ds(k0, tk), :]], axis=1)

    def v_rows(h, k0):
        return vt_ref[0, :, pl.ds(k0, tk)]

    ops = _flash_ops(bm_sc, (s0_sc, s1_sc), mx_sc, acc_sc, NSA_HPG, tq, tk, k_tile, v_rows)
    o_win = _gated(_window_branch(qh, t0, kw_ref, vwt_ref, bias_ref, wbm_sc, (ws0_sc, ws1_sc),
                                  wacc_sc, lambda h: ops[0](tiles_sm[0], 0, h)),
                   _gates_t(gt_ref, bg_ref, 2))
    outs = _flash_t(ops, NSA_HPG, tq, tk, n_un, lambda i: tiles_sm[i], diag,
                    lambda key: key <= qpos, first_done=True)
    o_slc = _gated(outs, _gates_t(gt_ref, bg_ref, 1))
    _store_heads(o_ref, [a + b for a, b in zip(o_slc, o_win)])


def _win_bias():
    tq, nk = NSA_TQ, WINDOW + NSA_TQ
    out = []
    for p in range(WINDOW // tq + 1):
        t0 = p * tq
        key = max(t0 - WINDOW, 0) + np.arange(nk)[:, None]
        qpos = t0 + np.arange(tq)[None, :]
        out.append(np.where((key <= qpos) & (key > qpos - WINDOW), 0.0, NEG))
    return jnp.asarray(np.stack(out), F32)


def _block_onehot(S):
    e = (np.arange(S)[:, None] // SLC_BLOCK) == np.arange(NS_PAD)[None, :]
    return jnp.asarray(e, BF16)


def _slc_win_attention(u3, ut, selneg, flags, gt, bg, *, q_row, ks_col, vs_row, kw_col, vw_row):
    B, S, _ = u3.shape
    sp = {k: pl.BlockSpec(v.block_shape, lambda b, g, i, fl, f=v.index_map: f(b, g, i))
          for k, v in _nsa_specs(q_row).items()}
    bias = _win_bias()
    npat = bias.shape[0]
    kspec = lambda col: pl.BlockSpec((1, S, LANES), lambda b, g, i, fl: (b, 0, col + g))
    vspec = lambda row: pl.BlockSpec((1, HEAD_DIM, S), lambda b, g, i, fl: (b, row + g, 0))
    grid_spec = pltpu.PrefetchScalarGridSpec(
        num_scalar_prefetch=1,
        grid=(B, NSA_GROUPS, S // NSA_TQ),
        in_specs=[
            sp["q"], kspec(ks_col), vspec(vs_row),
            pl.BlockSpec((1, 1, NS_PAD, NSA_TQ), lambda b, g, i, fl: (b, g, 0, i)),
            pl.BlockSpec((S, NS_PAD), lambda b, g, i, fl: (0, 0)),
            kspec(kw_col), vspec(vw_row),
            pl.BlockSpec((1,) + bias.shape[1:],
                         lambda b, g, i, fl: (jnp.minimum(i, npat - 1), 0, 0)),
            sp["tab"], sp["gt"], sp["bg"],
        ],
        out_specs=sp["out"],
        scratch_shapes=_flash_scratch(NSA_HPG, NSA_TQ, SLC_TK, 2 * LANES)
        + _flash_scratch(NSA_HPG, NSA_TQ, WIN_TK, LANES, col_max=False)
        + [pltpu.SMEM((S // SLC_TK + 1,), jnp.int32)],
    )
    return pl.pallas_call(
        _slc_win_kernel,
        grid_spec=grid_spec,
        out_shape=jax.ShapeDtypeStruct((B, S, NSA_WIDTH), BF16),
        compiler_params=_cp(("parallel", "parallel", "arbitrary")),
        name="slc_win_attention",
    )(flags, ut, u3, ut, selneg, _block_onehot(S), u3, ut, bias, _slope_table(), gt, bg)


def _aug_groups(w):
    d = w.shape[0]
    w = w.reshape(d, NSA_GROUPS, HEAD_DIM)
    return jnp.pad(w, ((0, 0), (0, 0), (0, LANES - HEAD_DIM))).reshape(d, NSA_GROUPS * LANES)


def _even_layer(x, norm_g, w_in, b_f, gn_g, w_out):
    B, S, D = x.shape
    qscale = HEAD_DIM ** -0.5 * LOG2E
    q_f, k_f, v_f, w_fl, q_r, k_r, v_r, z = jnp.split(
        w_in, np.cumsum([FOX_WIDTH] * 3 + [FOX_HEADS] + [RET_WIDTH] * 3).tolist(), axis=1)
    w = jnp.concatenate([z, k_f, q_r, k_r * HEAD_DIM ** -0.5, v_r], axis=1).astype(BF16)
    w_t = jnp.concatenate([q_f * qscale, v_f], axis=1).T.astype(BF16)
    x2 = x.reshape(B * S, D)
    u, ut = _proj(x2, norm_g, w, seq=S, w_t=[w_t], t_dtypes=[BF16])
    u3 = u.reshape(B, S, -1)
    kfeat, qfeat = _fgate(x, norm_g, w_fl, b_f, tile=min(512, S))
    o_f = _fox(u3, ut, kfeat, qfeat, q_row=0, k_col=D // LANES, v_row=FOX_WIDTH // LANES)
    rb = (D + FOX_WIDTH) // RET_WIDTH
    o_r = _retention(u3, gn_g, q_col=rb, k_col=rb + 1, v_col=rb + 2)
    out = _out0(o_f.reshape(B * S, -1), o_r.reshape(B * S, -1), u, x2, w_out.astype(BF16))
    return out.reshape(B, S, D)


def _odd_layer(x, norm_g, w_in, b_gate, pe_k, pe_v, wk1, wk2, wv1, wv2, w_out, final_g):
    B, S, D = x.shape
    assert S // SLC_BLOCK <= NS_PAD
    qscale = HEAD_DIM ** -0.5 * LOG2E
    sizes = [NSA_WIDTH] + [NSA_KV_WIDTH] * 6 + [NSA_HEADS * N_BRANCH]
    q, kc, vc, ks, vs, kw, vw, gl, z = jnp.split(w_in, np.cumsum(sizes).tolist(), axis=1)
    w = jnp.concatenate([z, kc, vc, _aug_groups(ks), _aug_groups(kw)], axis=1).astype(BF16)
    per_group = NSA_HPG * N_BRANCH
    glt = jnp.pad(gl.T.reshape(NSA_GROUPS, per_group, D), ((0, 0), (0, GATE_ROWS - per_group), (0, 0)))
    glt = glt.reshape(NSA_GROUPS * GATE_ROWS, D).astype(BF16)
    bg = jnp.pad(b_gate.reshape(NSA_GROUPS, per_group), ((0, 0), (0, GATE_ROWS - per_group)))
    bg = bg.reshape(NSA_GROUPS * GATE_ROWS, 1)
    w_vt = jnp.concatenate([q * qscale, vs, vw], axis=1).T.astype(BF16)
    x2 = x.reshape(B * S, D)
    first_k = (D + 2 * NSA_KV_WIDTH) // PROJ_TN
    u, kc_a, vc_a, ut, gt = _proj(
        x2, norm_g, w, seq=S, split_tile=D // PROJ_TN,
        addend=_pos_features(jnp.arange(S, dtype=jnp.int32), PROJ_TN), add_tiles=(first_k, first_k + 1),
        w_t=[w_vt, glt], t_dtypes=[BF16, F32])
    u3 = u.reshape(B, S, -1)
    nseg = S // CMP_STRIDE
    kcmp = _compress(kc_a.reshape(B, nseg, -1), pe_k, wk1, wk2, transposed=False)
    vcmp_t = _compress(vc_a.reshape(B, nseg, -1), pe_v, wv1, wv2, transposed=True)
    o_c, selneg, flags = _cmp_attention(ut, kcmp, vcmp_t, gt, bg, q_row=0)
    kb = (D + 2 * NSA_KV_WIDTH) // LANES
    vb = NSA_WIDTH // HEAD_DIM
    o_s = _slc_win_attention(u3, ut, selneg, flags[:, 0, :S // SLC_TK], gt, bg, q_row=0, ks_col=kb,
                             vs_row=vb, kw_col=kb + NSA_GROUPS, vw_row=vb + NSA_GROUPS)
    r = lambda a: a.reshape(B * S, -1)
    out = _out1(r(o_c), r(o_s), u, x2, w_out.astype(BF16), final_g)
    return out.reshape(B, S, D)


def kernel(x, even_norm_g, even_w_in, even_b_f, even_gn_g, even_w_out, odd_norm_g, odd_w_in,
           odd_b_gate, odd_pe_k, odd_pe_v, odd_wk1, odd_wk2, odd_wv1, odd_wv2, odd_w_out, final_g):
    x = _even_layer(x, even_norm_g[0], even_w_in[0], even_b_f[0], even_gn_g[0], even_w_out[0])
    return _odd_layer(x, odd_norm_g[0], odd_w_in[0], odd_b_gate[0], odd_pe_k[0], odd_pe_v[0],
                      odd_wk1[0], odd_wk2[0], odd_wv1[0], odd_wv2[0], odd_w_out[0], final_g)
```

```python
import functools
import math

import jax
import jax.numpy as jnp
import numpy as np
from jax import lax
from jax.experimental import pallas as pl
from jax.experimental.pallas import tpu as pltpu

D_MODEL = 1024
HEAD_DIM = 64
LANES = 128
FOX_HEADS = 8
RET_HEADS = 8
FOX_WIDTH = FOX_HEADS * HEAD_DIM
RET_WIDTH = RET_HEADS * HEAD_DIM
RET_CHUNK = 128
RET_STEP = 8
NSA_HEADS = 16
NSA_GROUPS = 4
NSA_HPG = NSA_HEADS // NSA_GROUPS
NSA_WIDTH = NSA_HEADS * HEAD_DIM
NSA_KV_WIDTH = NSA_GROUPS * HEAD_DIM
N_BRANCH = 3
GATE_ROWS = 16
CMP_BLOCK = 32
CMP_STRIDE = 16
CMP_HIDDEN = 256
CMP_CHUNK = 128
SLC_BLOCK = 64
SLC_TOPK = 16
N_FORCED = 3
NS_PAD = LANES
WINDOW = 512
RMS_EPS = 1e-6
GN_EPS = 1e-5
NEG = -1e30
FORCE_BONUS = 1e6
MASK_BIG = 2.0 ** 100
LOG2E = math.log2(math.e)
FEAT0 = HEAD_DIM
QF_ROWS = 16
ACC_ROWS = HEAD_DIM + 16

PROJ_TM = 1024
PROJ_TN = 512
FOX_TQ = 512
FOX_TK = 512
NSA_TQ = 512
SLC_TK = 512
WIN_TK = 256
OUT_TM = 1024
VMEM_LIMIT = 48 * 1024 * 1024

F32 = jnp.float32
BF16 = jnp.bfloat16


def _cp(sem, vmem=VMEM_LIMIT):
    return pltpu.CompilerParams(dimension_semantics=sem, vmem_limit_bytes=vmem)


def _dot(a, b):
    return jnp.dot(a, b, preferred_element_type=F32)


def _dot_nt(a, b):
    return lax.dot_general(a, b, (((1,), (1,)), ((), ())), preferred_element_type=F32)


def _dot_tn(a, b):
    return lax.dot_general(a, b, (((0,), (0,)), ((), ())), preferred_element_type=F32)


def _rms(x, g):
    return x * lax.rsqrt(jnp.mean(x * x, axis=-1, keepdims=True) + RMS_EPS) * g


def _silu(x):
    return x * (1.0 / (1.0 + jnp.exp(-x)))


def _sigmoid(x):
    return 1.0 / (1.0 + jnp.exp(-x))


def _low_half(shape, axis):
    return lax.broadcasted_iota(jnp.int32, shape, axis) < HEAD_DIM


def _pieces(v):
    p1 = v.astype(BF16).astype(F32)
    r = v - p1
    p2 = r.astype(BF16).astype(F32)
    p3 = (r - p2).astype(BF16).astype(F32)
    return p1, p2, p3


def _np_pieces(v):
    v = np.asarray(v, np.float64)
    bf = lambda a: np.asarray(a, np.float32).astype(BF16).astype(np.float64)
    p1 = bf(v)
    p2 = bf(v - p1)
    p3 = bf(v - p1 - p2)
    return p1, p2, p3


def _place(lane, cols):
    out = jnp.zeros(lane.shape, F32)
    for i, c in cols.items():
        out = jnp.where(lane == i, c, out)
    return out


def _fgate_kernel(x_ref, g_ref, wf_ref, b_ref, kf_ref, qf_ref, carry):
    @pl.when(pl.program_id(1) == 0)
    def _():
        carry[...] = jnp.zeros_like(carry)

    h = _rms(x_ref[0], g_ref[...])
    t = h.shape[0]
    h1 = h.astype(BF16)
    h2 = (h - h1.astype(F32)).astype(BF16)
    w = wf_ref[...]
    w1 = w.astype(BF16)
    w2 = (w - w1.astype(F32)).astype(BF16)
    nf = w.shape[1]
    r = _dot(jnp.concatenate([h1, h2], axis=0), jnp.concatenate([w1, w2], axis=1))
    f = r[:t, :nf] + r[:t, nf:] + r[t:, :nf] + b_ref[...]
    ls = jnp.minimum(f, 0.0) - jnp.log(1.0 + jnp.exp(-jnp.abs(f)))
    r = lax.broadcasted_iota(jnp.int32, (t, t), 0)
    c = lax.broadcasted_iota(jnp.int32, (t, t), 1)
    lower = jnp.where(c <= r, 1.0, 0.0).astype(BF16)
    r = _dot(lower, jnp.concatenate([p.astype(BF16) for p in _pieces(ls)], axis=1))
    cs = r[:, :nf] + r[:, nf:2 * nf] + r[:, 2 * nf:] + carry[...]
    carry[...] = cs[t - 1:t, :]
    cl = cs * LOG2E
    lane = lax.broadcasted_iota(jnp.int32, (t, LANES), 1)
    one = jnp.ones((t, 1), F32)
    for j in range(FOX_HEADS // 2):
        a1, a2, a3 = _pieces(cl[:, 2 * j:2 * j + 1])
        b1, b2, b3 = _pieces(cl[:, 2 * j + 1:2 * j + 2])
        kf = _place(lane, {0: a1, 1: a2, 2: a3, 3: one, 4: one, 5: one, 6: b1, 7: b2, 8: b3})
        kf_ref[0, j] = kf.astype(BF16)
        qa = _place(lane, {0: -one, 1: -one, 2: -one, 3: a1, 4: a2, 5: a3})
        qb = _place(lane, {3: b1, 4: b2, 5: b3, 6: -one, 7: -one, 8: -one})
        qf_ref[0, j, 0] = qa.T[:QF_ROWS].astype(BF16)
        qf_ref[0, j, 1] = qb.T[:QF_ROWS].astype(BF16)


def _fgate(x, g, wf, b_f, *, tile):
    B, S, D = x.shape
    npair = FOX_HEADS // 2
    return pl.pallas_call(
        _fgate_kernel,
        grid=(B, S // tile),
        in_specs=[
            pl.BlockSpec((1, tile, D), lambda b, s: (b, s, 0)),
            pl.BlockSpec((1, D), lambda b, s: (0, 0)),
            pl.BlockSpec((D, FOX_HEADS), lambda b, s: (0, 0)),
            pl.BlockSpec((1, FOX_HEADS), lambda b, s: (0, 0)),
        ],
        out_specs=[
            pl.BlockSpec((1, npair, tile, LANES), lambda b, s: (b, 0, s, 0)),
            pl.BlockSpec((1, npair, 2, QF_ROWS, tile), lambda b, s: (b, 0, 0, 0, s)),
        ],
        out_shape=[
            jax.ShapeDtypeStruct((B, npair, S, LANES), BF16),
            jax.ShapeDtypeStruct((B, npair, 2, QF_ROWS, S), BF16),
        ],
        scratch_shapes=[pltpu.VMEM((1, FOX_HEADS), F32)],
        compiler_params=_cp(("parallel", "arbitrary")),
        name="fgate",
    )(x, g.reshape(1, D), wf, b_f.reshape(1, FOX_HEADS))


def _proj_kernel(*refs, split_tile, add_tiles, n_t):
    it = iter(refs)
    x_ref, g_ref, w_ref = next(it), next(it), next(it)
    add_ref = next(it) if add_tiles else None
    wt_refs = [next(it) for _ in range(n_t)]
    u_ref = next(it)
    e_refs = [next(it), next(it)] if split_tile is not None else None
    ut_refs = [next(it) for _ in range(n_t)]
    h_sc = next(it)
    j = pl.program_id(1)

    @pl.when(j == 0)
    def _():
        h = _rms(x_ref[...], g_ref[...]).astype(BF16)
        h_sc[...] = h
        for wt_ref, ut_ref in zip(wt_refs, ut_refs):
            ut_ref[0] = _dot_nt(wt_ref[...], h).astype(ut_ref.dtype)

    acc = _dot(h_sc[...], w_ref[...])
    if add_tiles:
        is_add = functools.reduce(jnp.logical_or, [j == t for t in add_tiles])

        @pl.when(is_add)
        def _():
            u_ref[...] = (acc + add_ref[...].astype(F32)).astype(u_ref.dtype)

        @pl.when(jnp.logical_not(is_add))
        def _():
            u_ref[...] = acc.astype(u_ref.dtype)
    else:
        u_ref[...] = acc.astype(u_ref.dtype)
    if split_tile is not None:
        @pl.when(j == split_tile)
        def _():
            half = acc.shape[1] // 2
            e_refs[0][...] = acc[:, :half].astype(e_refs[0].dtype)
            e_refs[1][...] = acc[:, half:].astype(e_refs[1].dtype)


def _proj(x2, g, w, *, seq, split_tile=None, addend=None, add_tiles=(), w_t=(), t_dtypes=(),
          tm=PROJ_TM, tn=PROJ_TN):
    N, D = x2.shape
    W = w.shape[1]
    nbs = seq // tm
    in_specs = [
        pl.BlockSpec((tm, D), lambda i, j: (i, 0)),
        pl.BlockSpec((1, D), lambda i, j: (0, 0)),
        pl.BlockSpec((D, tn), lambda i, j: (0, j)),
    ]
    args = [x2, g.reshape(1, D), w]
    if add_tiles:
        in_specs.append(pl.BlockSpec((tm, tn), lambda i, j: (i % nbs, 0)))
        args.append(addend)
    out_shape = [jax.ShapeDtypeStruct((N, W), BF16)]
    out_specs = [pl.BlockSpec((tm, tn), lambda i, j: (i, j))]
    if split_tile is not None:
        out_shape += [jax.ShapeDtypeStruct((N, tn // 2), BF16)] * 2
        out_specs += [pl.BlockSpec((tm, tn // 2), lambda i, j: (i, 0))] * 2
    for wt, dt in zip(w_t, t_dtypes):
        rows = wt.shape[0]
        in_specs.append(pl.BlockSpec((rows, D), lambda i, j: (0, 0)))
        args.append(wt)
        out_shape.append(jax.ShapeDtypeStruct((N // seq, rows, seq), dt))
        out_specs.append(pl.BlockSpec((1, rows, tm), lambda i, j: (i // nbs, 0, i % nbs)))
    return pl.pallas_call(
        functools.partial(_proj_kernel, split_tile=split_tile, add_tiles=tuple(add_tiles),
                          n_t=len(w_t)),
        grid=(N // tm, W // tn),
        in_specs=in_specs,
        out_specs=out_specs,
        out_shape=out_shape,
        scratch_shapes=[pltpu.VMEM((tm, D), BF16)],
        compiler_params=_cp(("parallel", "arbitrary")),
        name="proj",
    )(*args)


def _flash_ops(bm_sc, s_bufs, mx_sc, acc_sc, nh, tq, tk, k_tile, v_rows):
    acc_sc[...] = jnp.zeros_like(acc_sc)
    ones = jnp.ones((ACC_ROWS - HEAD_DIM, tk), BF16)

    def qk_head(tile, slot, h):
        cols = slice(h * tq, (h + 1) * tq)
        s = _dot(k_tile(pl.multiple_of(tile * tk, tk)), bm_sc[:, cols])
        s_bufs[slot][:, cols] = s
        if mx_sc is not None:
            mx_sc[slot, :, cols] = jnp.max(s, axis=0, keepdims=True)

    def soft_head(tile, slot, m_old, h, valid, bias):
        cols = slice(h * tq, (h + 1) * tq)
        buf = s_bufs[slot]
        k0 = pl.multiple_of(tile * tk, tk)
        if valid is not None or bias is not None:
            s = buf[:, cols] + bias if valid is None else jnp.where(valid, buf[:, cols], NEG)
            m_new = jnp.maximum(m_old, jnp.max(s, axis=0, keepdims=True))
            p = jnp.exp2(s - m_new).astype(BF16)
        else:
            m_new = jnp.maximum(m_old, mx_sc[slot, :, cols])
            p = jnp.exp2(buf[:, cols] - m_new).astype(BF16)
        alpha = jnp.exp2(m_old - m_new)
        lhs = jnp.concatenate([v_rows(h, k0), ones], axis=0)
        acc_sc[h] = alpha * acc_sc[h] + _dot(lhs, p)
        return m_new

    def qk_all(tile, slot):
        for h in range(nh):
            qk_head(tile, slot, h)

    def step(cur, slot, ms, nxt=None, valid=None, bias=None, before=None):
        out = []
        for h in range(nh):
            if nxt is not None:
                qk_head(nxt, 1 - slot, h)
            if before is not None:
                before(h)
            out.append(soft_head(cur, slot, ms[h], h, valid, bias))
        return tuple(out)

    def finish():
        outs = []
        for h in range(nh):
            a = acc_sc[h]
            outs.append(a[:HEAD_DIM] * (1.0 / a[HEAD_DIM:HEAD_DIM + 1]))
        return outs

    return qk_head, qk_all, step, finish


def _flash_t(ops, nh, tq, tk, n_un, tile_of, diag, valid_fn, first_done=False):
    _, qk_all, step, finish = ops
    key_iota = lax.broadcasted_iota(jnp.int32, (tk, tq), 0)
    diag_valid = lambda: valid_fn(pl.multiple_of(diag * tk, tk) + key_iota)

    def pair(i, ms):
        t_a, t_b, t_c = tile_of(2 * i), tile_of(2 * i + 1), tile_of(2 * i + 2)
        return step(t_b, 1, step(t_a, 0, ms, nxt=t_b), nxt=t_c)

    def odd_tail(ms):
        return step(diag, 1, step(tile_of(n_un - 1), 0, ms, nxt=diag), valid=diag_valid())

    def even_tail(ms):
        return step(diag, 0, ms, valid=diag_valid())

    if not first_done:
        qk_all(tile_of(0), 0)
    ms = (jnp.full((1, tq), NEG, F32),) * nh
    ms = lax.fori_loop(0, n_un // 2, pair, ms)
    lax.cond(n_un % 2 == 1, odd_tail, even_tail, ms)
    return finish()


def _flash_scratch(nh, tq, tk, kdim, col_max=True):
    bufs = [pltpu.VMEM((kdim, nh * tq), BF16), pltpu.VMEM((tk, nh * tq), F32),
            pltpu.VMEM((tk, nh * tq), F32)]
    if col_max:
        bufs.append(pltpu.VMEM((2, 1, nh * tq), F32))
    return bufs + [pltpu.VMEM((nh, ACC_ROWS, tq), F32)]


def _fox_kernel(qt_ref, qf_ref, k_ref, kf_ref, vt_ref, o_ref, bm_sc, s0_sc, s1_sc, mx_sc, acc_sc):
    qi = pl.program_id(2)
    tq, tk = FOX_TQ, FOX_TK
    bm_sc[...] = jnp.zeros_like(bm_sc)
    for h in range(2):
        rows = slice(h * HEAD_DIM, (h + 1) * HEAD_DIM)
        bm_sc[rows, h * tq:(h + 1) * tq] = qt_ref[0, rows, :]
        bm_sc[LANES:LANES + QF_ROWS, h * tq:(h + 1) * tq] = qf_ref[0, 0, h]
    qpos = qi * tq + lax.broadcasted_iota(jnp.int32, (tk, tq), 1)

    def k_tile(k0):
        return jnp.concatenate([k_ref[0, pl.ds(k0, tk), :], kf_ref[0, 0, pl.ds(k0, tk), :]], axis=1)

    def v_rows(h, k0):
        return vt_ref[0, h * HEAD_DIM:(h + 1) * HEAD_DIM, pl.ds(k0, tk)]

    diag = (qi * tq) // tk
    ops = _flash_ops(bm_sc, (s0_sc, s1_sc), mx_sc, acc_sc, 2, tq, tk, k_tile, v_rows)
    outs = _flash_t(ops, 2, tq, tk, diag, lambda i: i, diag, lambda key: key <= qpos)
    o_ref[0] = jnp.concatenate(outs, axis=0).T.astype(o_ref.dtype)


def _fox(u3, ut, kfeat, qfeat, *, q_row, k_col, v_row):
    B, S, _ = u3.shape
    npair = FOX_HEADS // 2
    return pl.pallas_call(
        _fox_kernel,
        grid=(B, npair, S // FOX_TQ),
        in_specs=[
            pl.BlockSpec((1, LANES, FOX_TQ), lambda b, j, i: (b, q_row + j, i)),
            pl.BlockSpec((1, 1, 2, QF_ROWS, FOX_TQ), lambda b, j, i: (b, j, 0, 0, i)),
            pl.BlockSpec((1, S, LANES), lambda b, j, i: (b, 0, k_col + j)),
            pl.BlockSpec((1, 1, S, LANES), lambda b, j, i: (b, j, 0, 0)),
            pl.BlockSpec((1, LANES, S), lambda b, j, i: (b, v_row + j, 0)),
        ],
        out_specs=pl.BlockSpec((1, FOX_TQ, LANES), lambda b, j, i: (b, i, j)),
        out_shape=jax.ShapeDtypeStruct((B, S, FOX_WIDTH), BF16),
        scratch_shapes=_flash_scratch(2, FOX_TQ, FOX_TK, 2 * LANES),
        compiler_params=_cp(("parallel", "parallel", "arbitrary")),
        name="fox",
    )(ut, qfeat, u3, kfeat, ut)


def _ret_kernel(q_ref, k_ref, v_ref, inner_ref, cross_ref, kdec_ref, cd_ref, bd_ref, gn_ref,
                o_ref, state_sc):
    @pl.when(pl.program_id(1) == 0)
    def _():
        state_sc[...] = jnp.zeros_like(state_sc)

    low = _low_half((RET_CHUNK, LANES), 1)
    inv = 1.0 / HEAD_DIM
    for c, j in [(c, j) for c in range(RET_STEP) for j in range(RET_HEADS // 2)]:
        rows = slice(c * RET_CHUNK, (c + 1) * RET_CHUNK)
        cols = slice(j * LANES, (j + 1) * LANES)
        q, k, v = q_ref[0, rows, cols], k_ref[0, rows, cols], v_ref[0, rows, cols]
        zero = jnp.zeros_like(q)
        qa, qb = jnp.where(low, q, zero), jnp.where(low, zero, q)
        pa = (_dot_nt(qa, k) * inner_ref[j, 0]).astype(BF16)
        pb = (_dot_nt(qb, k) * inner_ref[j, 1]).astype(BF16)
        o_in = jnp.where(low, _dot(pa, v), _dot(pb, v))
        state = state_sc[j]
        o = o_in + _dot(q, state.astype(BF16)) * cross_ref[j]
        kd = (k.astype(F32) * kdec_ref[j]).astype(BF16)
        state_sc[j] = state * cd_ref[j] + _dot_tn(kd, v) * bd_ref[...]
        sa = jnp.sum(jnp.where(low, o, 0.0), axis=-1, keepdims=True)
        st = jnp.sum(o, axis=-1, keepdims=True)
        mu = jnp.where(low, sa, st - sa) * inv
        d = o - mu
        d2 = d * d
        va = jnp.sum(jnp.where(low, d2, 0.0), axis=-1, keepdims=True)
        vt = jnp.sum(d2, axis=-1, keepdims=True)
        var = jnp.where(low, va, vt - va) * inv
        o_ref[0, rows, cols] = (d * lax.rsqrt(var + GN_EPS) * gn_ref[:, cols]).astype(o_ref.dtype)


def _ret_constants():
    lg = np.log(1.0 - 2.0 ** (-5.0 - np.arange(RET_HEADS)))
    i = np.arange(RET_CHUNK)
    diff = i[:, None] - i[None, :]
    inner = np.where(diff[None] >= 0, np.exp(lg[:, None, None] * np.maximum(diff, 0)[None]), 0.0)
    cross = np.exp(lg[:, None] * (i[None, :] + 1))
    kdec = np.exp(lg[:, None] * (RET_CHUNK - 1 - i)[None, :])
    cdec = np.exp(lg * RET_CHUNK)
    npair = RET_HEADS // 2
    inner = inner.reshape(npair, 2, RET_CHUNK, RET_CHUNK)

    def lanes(a):
        a = a.reshape(npair, 2, RET_CHUNK)
        return np.repeat(a.transpose(0, 2, 1), HEAD_DIM, axis=2)

    bd = np.kron(np.eye(2), np.ones((HEAD_DIM, HEAD_DIM)))
    cd = np.repeat(cdec.reshape(npair, 2), HEAD_DIM, axis=1)[:, :, None] * bd[None]
    f = lambda a: jnp.asarray(a, F32)
    return f(inner), f(lanes(cross)), f(lanes(kdec)), f(cd), f(bd)


def _retention(u3, gn_g, *, q_col, k_col, v_col):
    B, S, _ = u3.shape
    C = RET_CHUNK
    rows = RET_STEP * C
    npair = RET_HEADS // 2
    inner, cross, kdec, cd, bd = _ret_constants()
    full = lambda shape: pl.BlockSpec(shape, lambda b, i: (0,) * len(shape))
    return pl.pallas_call(
        _ret_kernel,
        grid=(B, S // rows),
        in_specs=[
            pl.BlockSpec((1, rows, RET_WIDTH), lambda b, i: (b, i, q_col)),
            pl.BlockSpec((1, rows, RET_WIDTH), lambda b, i: (b, i, k_col)),
            pl.BlockSpec((1, rows, RET_WIDTH), lambda b, i: (b, i, v_col)),
            full((npair, 2, C, C)), full((npair, C, LANES)), full((npair, C, LANES)),
            full((npair, LANES, LANES)), full((LANES, LANES)), full((1, RET_WIDTH)),
        ],
        out_specs=pl.BlockSpec((1, rows, RET_WIDTH), lambda b, i: (b, i, 0)),
        out_shape=jax.ShapeDtypeStruct((B, S, RET_WIDTH), BF16),
        scratch_shapes=[pltpu.VMEM((npair, LANES, LANES), F32)],
        compiler_params=_cp(("parallel", "arbitrary")),
        name="retention",
    )(u3, u3, u3, inner, cross, kdec, cd, bd, gn_g.reshape(1, RET_WIDTH))


def _out0_kernel(of_ref, or_ref, z_ref, x_ref, w_ref, o_ref):
    z = _silu(z_ref[...].astype(F32))
    ya = (of_ref[...].astype(F32) * z[:, :FOX_WIDTH]).astype(BF16)
    yb = (or_ref[...].astype(F32) * z[:, FOX_WIDTH:]).astype(BF16)
    o_ref[...] = x_ref[...] + _dot(ya, w_ref[:FOX_WIDTH, :]) + _dot(yb, w_ref[FOX_WIDTH:, :])


def _out0(o_f, o_r, u, x2, w_out, *, tm=OUT_TM):
    N, D = x2.shape
    return pl.pallas_call(
        _out0_kernel,
        grid=(N // tm,),
        in_specs=[
            pl.BlockSpec((tm, FOX_WIDTH), lambda i: (i, 0)),
            pl.BlockSpec((tm, RET_WIDTH), lambda i: (i, 0)),
            pl.BlockSpec((tm, D), lambda i: (i, 0)),
            pl.BlockSpec((tm, D), lambda i: (i, 0)),
            pl.BlockSpec((D, D), lambda i: (0, 0)),
        ],
        out_specs=pl.BlockSpec((tm, D), lambda i: (i, 0)),
        out_shape=jax.ShapeDtypeStruct((N, D), F32),
        compiler_params=_cp(("parallel",)),
        name="out0",
    )(o_f, o_r, u, x2, w_out)


def _out1_kernel(oc_ref, os_ref, z_ref, x_ref, w_ref, g_ref, o_ref):
    z = _silu(z_ref[...].astype(F32))
    y = ((oc_ref[...].astype(F32) + os_ref[...].astype(F32)) * z).astype(BF16)
    o_ref[...] = _rms(x_ref[...] + _dot(y, w_ref[...]), g_ref[...])


def _out1(o_c, o_s, u, x2, w_out, final_g, *, tm=OUT_TM):
    N, D = x2.shape
    row = pl.BlockSpec((tm, D), lambda i: (i, 0))
    return pl.pallas_call(
        _out1_kernel,
        grid=(N // tm,),
        in_specs=[row, row, row, row,
                  pl.BlockSpec((D, D), lambda i: (0, 0)),
                  pl.BlockSpec((1, D), lambda i: (0, 0))],
        out_specs=row,
        out_shape=jax.ShapeDtypeStruct((N, D), F32),
        compiler_params=_cp(("parallel",)),
        name="out1",
    )(o_c, o_s, u, x2, w_out, final_g.reshape(1, D))


def _compress_kernel(x_ref, pea_ref, peb_ref, wa_ref, wb_ref, w2_ref, *rest, transposed):
    x = x_ref[0].astype(F32)
    a = _dot((x + pea_ref[...]).astype(BF16), wa_ref[0])
    b = _dot((x + peb_ref[...]).astype(BF16), wb_ref[0])
    nseg = x.shape[0]
    pre = a + pltpu.roll(b, nseg - 1, 0)
    hid = _silu(pre).astype(BF16)
    if transposed:
        o_ref, = rest
        o_ref[0, 0] = _dot_nt(w2_ref[...], hid).astype(o_ref.dtype)
    else:
        feat_ref, o_ref = rest
        o_ref[0, 0] = (_dot(hid, w2_ref[...]) + feat_ref[...].astype(F32)).astype(o_ref.dtype)


def _compress(a3, pe, w1, w2, *, transposed):
    B, nseg, wid = a3.shape
    half = CMP_STRIDE * HEAD_DIM
    eye = jnp.eye(NSA_GROUPS, dtype=w1.dtype)

    def big(wh):
        w4 = wh.reshape(CMP_STRIDE, 1, HEAD_DIM, CMP_HIDDEN)
        sel = eye[:, None, :, None, None]
        return (sel * w4[None]).reshape(NSA_GROUPS, wid, CMP_HIDDEN).astype(BF16)

    def pe_big(p):
        return jnp.broadcast_to(p[:, None, :], (CMP_STRIDE, NSA_GROUPS, HEAD_DIM)).reshape(1, wid)

    args = [a3, pe_big(pe[:CMP_STRIDE]), pe_big(pe[CMP_STRIDE:]), big(w1[:half]), big(w1[half:])]
    in_specs = [
        pl.BlockSpec((1, nseg, wid), lambda b, g: (b, 0, 0)),
        pl.BlockSpec((1, wid), lambda b, g: (0, 0)),
        pl.BlockSpec((1, wid), lambda b, g: (0, 0)),
        pl.BlockSpec((1, wid, CMP_HIDDEN), lambda b, g: (g, 0, 0)),
        pl.BlockSpec((1, wid, CMP_HIDDEN), lambda b, g: (g, 0, 0)),
    ]
    if transposed:
        w2d = w2.T.astype(BF16)
        oshape, oblock = (B, NSA_GROUPS, HEAD_DIM, nseg), (1, 1, HEAD_DIM, nseg)
        args.append(w2d)
        in_specs.append(pl.BlockSpec(w2d.shape, lambda b, g: (0, 0)))
    else:
        w2d = jnp.pad(w2, ((0, 0), (0, LANES - HEAD_DIM))).astype(BF16)
        oshape, oblock = (B, NSA_GROUPS, nseg, LANES), (1, 1, nseg, LANES)
        feat = _pos_features(jnp.arange(nseg, dtype=jnp.int32) * CMP_STRIDE + (CMP_BLOCK - 1), LANES)
        args += [w2d, feat]
        in_specs += [pl.BlockSpec(w2d.shape, lambda b, g: (0, 0)),
                     pl.BlockSpec((nseg, LANES), lambda b, g: (0, 0))]
    return pl.pallas_call(
        functools.partial(_compress_kernel, transposed=transposed),
        grid=(B, NSA_GROUPS),
        in_specs=in_specs,
        out_specs=pl.BlockSpec(oblock, lambda b, g: (b, g, 0, 0)),
        out_shape=jax.ShapeDtypeStruct(oshape, BF16),
        compiler_params=_cp(("parallel", "parallel")),
        name="compress",
    )(*args)


def _slope_table():
    s = np.asarray(2.0 ** (-8.0 * (np.arange(NSA_HEADS) + 1) / NSA_HEADS), np.float32)
    sl = np.asarray(s.astype(np.float64) * LOG2E, np.float32)
    p1, p2, p3 = _np_pieces(sl)
    tab = np.zeros((NSA_HEADS, QF_ROWS), np.float32)
    for k, p in enumerate((p1, p1, p2, p2, p3, p3)):
        tab[:, k] = p
    tab[:, 6] = sl
    tab = np.broadcast_to(tab.reshape(NSA_GROUPS, NSA_HPG * QF_ROWS, 1),
                          (NSA_GROUPS, NSA_HPG * QF_ROWS, NSA_TQ))
    return jnp.asarray(tab)


def _pos_features(pos, width):
    pos = pos[:, None]
    lane = jnp.arange(width, dtype=jnp.int32)[None, :] % LANES
    hi = ((pos // SLC_BLOCK) * SLC_BLOCK).astype(F32)
    lo = (pos % SLC_BLOCK).astype(F32)
    k = lane - FEAT0
    f = jnp.where((k >= 0) & (k < 6), jnp.where(k % 2 == 0, hi, lo), 0.0)
    f = jnp.where((k >= 6) & (k < 9), 1.0, f)
    return f.astype(BF16)


def _nsa_queries(qt_ref, tab_ref, t0):
    tq = qt_ref.shape[2]
    r = lax.broadcasted_iota(jnp.int32, (QF_ROWS, tq), 0)
    t = (t0 + lax.broadcasted_iota(jnp.int32, (1, tq), 1)).astype(F32)
    zeros = jnp.zeros((LANES - HEAD_DIM - QF_ROWS, tq), BF16)
    out = []
    for i in range(NSA_HPG):
        tile = tab_ref[0, i * QF_ROWS:(i + 1) * QF_ROWS, :]
        a1, a2, a3 = _pieces(-(tile[6:7, :] * t))
        feat = jnp.where(r == 6, a1, jnp.where(r == 7, a2, jnp.where(r == 8, a3,
                                                                     jnp.where(r < 6, tile, 0.0))))
        out.append(jnp.concatenate([qt_ref[0, i * HEAD_DIM:(i + 1) * HEAD_DIM, :],
                                    feat.astype(BF16), zeros], axis=0))
    return out


def _gates_t(gt_ref, bg_ref, branch):
    gl = gt_ref[0] + bg_ref[...]
    return [_sigmoid(gl[N_BRANCH * i + branch:N_BRANCH * i + branch + 1, :]) for i in range(NSA_HPG)]


def _gated(outs_t, gates):
    return [o * gt for o, gt in zip(outs_t, gates)]


def _store_heads(o_ref, g):
    o_ref[0, :, :LANES] = jnp.concatenate(g[:2], axis=0).T.astype(o_ref.dtype)
    o_ref[0, :, LANES:] = jnp.concatenate(g[2:], axis=0).T.astype(o_ref.dtype)


def _nsa_specs(q_row):
    return dict(
        q=pl.BlockSpec((1, NSA_HPG * HEAD_DIM, NSA_TQ), lambda b, g, i: (b, q_row + g, i)),
        tab=pl.BlockSpec((1, NSA_HPG * QF_ROWS, NSA_TQ), lambda b, g, i: (g, 0, 0)),
        gt=pl.BlockSpec((1, GATE_ROWS, NSA_TQ), lambda b, g, i: (b, g, i)),
        bg=pl.BlockSpec((GATE_ROWS, 1), lambda b, g, i: (g, 0)),
        out=pl.BlockSpec((1, NSA_TQ, NSA_HPG * HEAD_DIM), lambda b, g, i: (b, i, g)),
    )


def _argmax_first(v, idx):
    n = v.shape[0]
    slabs = [(v[r:r + 8], idx[r:r + 8]) for r in range(0, n, 8)]
    while len(slabs) > 1:
        nxt = []
        for (va, ia), (vb, ib) in zip(slabs[0::2], slabs[1::2]):
            right = vb > va
            nxt.append((jnp.where(right, vb, va), jnp.where(right, ib, ia)))
        if len(slabs) % 2:
            nxt.append(slabs[-1])
        slabs = nxt
    v, idx = slabs[0]
    mx = jnp.max(v, axis=0, keepdims=True)
    first = jnp.min(jnp.where(v == mx, idx, float(n)), axis=0, keepdims=True)
    return mx, first


def _cmp_body(nc, t0, qh, kc_ref, vct_ref, mt_ref, grp_ref, gates, o_ref, sel_ref, flag_ref):
    tq = NSA_TQ
    rows = nc * CMP_CHUNK
    full = max(rows - CMP_CHUNK - 8, 0)
    nseg = kc_ref.shape[2]
    kc = kc_ref[0, 0, :rows, :]
    vct = vct_ref[0, 0, :, :rows]
    t = t0 + lax.broadcasted_iota(jnp.int32, (1, tq), 1)
    cidx = full + lax.broadcasted_iota(jnp.int32, (rows - full, 1), 0)
    valid = (cidx * CMP_STRIDE + (CMP_BLOCK - 1) <= t) & (cidx < nseg - 1)
    psum = jnp.zeros((rows, tq), F32)
    ps = []
    s_all = _dot(kc, jnp.concatenate(qh, axis=1))
    for i in range(NSA_HPG):
        s = s_all[:, i * tq:(i + 1) * tq]
        s_last = jnp.where(valid, s[full:], NEG)
        m = jnp.max(s_last, axis=0, keepdims=True)
        if nc > 1:
            m = jnp.maximum(m, jnp.max(s[:full], axis=0, keepdims=True))
        e = jnp.where(valid, jnp.exp2(s_last - m), 0.0)
        if nc > 1:
            e = jnp.concatenate([jnp.exp2(s[:full] - m), e], axis=0)
        l = jnp.sum(e, axis=0, keepdims=True)
        p = e * jnp.where(l > 0.0, 1.0 / l, 0.0)
        psum = psum + p
        ps.append(p.astype(BF16))
    o_all = _dot(vct, jnp.concatenate(ps, axis=1))
    _store_heads(o_ref, _gated([o_all[:, i * tq:(i + 1) * tq] for i in range(NSA_HPG)], gates))
    ns = rows * CMP_STRIDE // SLC_BLOCK
    mt = mt_ref[:ns, :rows]
    imp = sum(_dot(mt, p.astype(BF16)) for p in _pieces(psum))
    blk = lax.broadcasted_iota(jnp.int32, (ns, 1), 0)
    cur = t // SLC_BLOCK
    bvalid = blk * SLC_BLOCK <= t
    forced = (blk == 0) | (blk == cur) | (blk == cur - 1)
    score = jnp.where(forced, -jnp.inf, jnp.where(bvalid, imp, NEG))
    blk_f = jnp.broadcast_to(blk.astype(F32), (ns, tq))
    work = score
    for _ in range(SLC_TOPK - N_FORCED):
        mx, first = _argmax_first(work, blk_f)
        work = jnp.where(blk_f == first, -jnp.inf, work)
    picked = (score > mx) | ((score == mx) & (blk_f <= first))
    selneg = jnp.where(bvalid & (forced | picked), 0.0, -MASK_BIG)
    if ns < NS_PAD:
        selneg = jnp.concatenate([selneg, jnp.full((NS_PAD - ns, tq), -MASK_BIG, F32)], axis=0)
    sel_ref[0, 0] = selneg.astype(sel_ref.dtype)
    picked = jnp.where(selneg == 0.0, 1.0, 0.0).astype(BF16)
    used = _dot_nt(jnp.ones((8, tq), BF16), picked)
    used = jnp.where(used > 0.0, 1.0, 0.0).astype(BF16)
    flag_ref[0] = (_dot(used, grp_ref[...])[0:1] > 0.0).astype(jnp.int32)


def _cmp_kernel(q_ref, kc_ref, vct_ref, mt_ref, grp_ref, tab_ref, gt_ref, bg_ref,
                o_ref, sel_ref, flag_ref):
    t0 = pl.program_id(2) * NSA_TQ
    qh = _nsa_queries(q_ref, tab_ref, t0)
    gates = _gates_t(gt_ref, bg_ref, 0)
    nchunk = kc_ref.shape[2] // CMP_CHUNK
    last = t0 // (CMP_CHUNK * CMP_STRIDE)
    for nc in range(1, nchunk + 1):
        pl.when(last == nc - 1)(functools.partial(
            _cmp_body, nc, t0, qh, kc_ref, vct_ref, mt_ref, grp_ref, gates, o_ref, sel_ref, flag_ref))


def _cmp_to_slc_t(nseg, ns):
    c0 = np.arange(nseg)[:, None] * CMP_STRIDE
    s0 = np.arange(ns)[None, :] * SLC_BLOCK
    overlap = np.clip(np.minimum(c0 + CMP_BLOCK, s0 + SLC_BLOCK) - np.maximum(c0, s0), 0, None)
    m = overlap / CMP_STRIDE
    m[nseg - 1] = 0.0
    mt = np.zeros((NS_PAD, nseg))
    mt[:ns] = m.T
    return jnp.asarray(mt, BF16)


def _tile_groups():
    per = SLC_TK // SLC_BLOCK
    g = (np.arange(NS_PAD)[:, None] // per) == np.arange(NS_PAD)[None, :]
    return jnp.asarray(g, BF16)


def _cmp_attention(ut, kcmp, vcmp_t, gt, bg, *, q_row):
    B, _, S = ut.shape
    nseg = kcmp.shape[2]
    nq = S // NSA_TQ
    sp = _nsa_specs(q_row)
    return pl.pallas_call(
        _cmp_kernel,
        grid=(B, NSA_GROUPS, S // NSA_TQ),
        in_specs=[
            sp["q"],
            pl.BlockSpec((1, 1, nseg, LANES), lambda b, g, i: (b, g, 0, 0)),
            pl.BlockSpec((1, 1, HEAD_DIM, nseg), lambda b, g, i: (b, g, 0, 0)),
            pl.BlockSpec((NS_PAD, nseg), lambda b, g, i: (0, 0)),
            pl.BlockSpec((NS_PAD, NS_PAD), lambda b, g, i: (0, 0)),
            sp["tab"], sp["gt"], sp["bg"],
        ],
        out_specs=[sp["out"], pl.BlockSpec((1, 1, NS_PAD, NSA_TQ), lambda b, g, i: (b, g, 0, i)),
                   pl.BlockSpec((1, 1, NS_PAD), lambda b, g, i: ((b * NSA_GROUPS + g) * nq + i, 0, 0))],
        out_shape=[jax.ShapeDtypeStruct((B, S, NSA_WIDTH), BF16),
                   jax.ShapeDtypeStruct((B, NSA_GROUPS, NS_PAD, S), BF16),
                   jax.ShapeDtypeStruct((B * NSA_GROUPS * nq, 1, NS_PAD), jnp.int32)],
        compiler_params=_cp(("parallel", "parallel", "arbitrary")),
        name="cmp_attention",
    )(ut, kcmp, vcmp_t, _cmp_to_slc_t(nseg, S // SLC_BLOCK), _tile_groups(), _slope_table(), gt, bg)


def _window_branch(qh, t0, k_ref, vt_ref, bias_ref, bm_sc, s_bufs, acc_sc, last_before):
    tq, tk = NSA_TQ, WIN_TK
    bm_sc[...] = jnp.concatenate(qh, axis=1)
    _, qk_all, step, finish = _flash_ops(
        bm_sc, s_bufs, None, acc_sc, NSA_HPG, tq, tk,
        lambda k0: k_ref[0, pl.ds(k0, tk), :], lambda h, k0: vt_ref[0, :, pl.ds(k0, tk)])
    first = jnp.maximum(t0 - WINDOW, 0) // tk
    ntile = (WINDOW + tq) // tk
    qk_all(first, 0)
    ms = (jnp.full((1, tq), NEG, F32),) * NSA_HPG
    for j in range(ntile):
        last = j + 1 == ntile
        ms = step(first + j, j % 2, ms, nxt=None if last else first + j + 1,
                  bias=bias_ref[0, j * tk:(j + 1) * tk, :], before=last_before if last else None)
    return finish()


def _slc_win_kernel(fl_ref, q_ref, k_ref, vt_ref, sel_ref, e_ref, kw_ref, vwt_ref, bias_ref,
                    tab_ref, gt_ref, bg_ref, o_ref, bm_sc, s0_sc, s1_sc, mx_sc, acc_sc,
                    wbm_sc, ws0_sc, ws1_sc, wacc_sc, tiles_sm):
    qi = pl.program_id(2)
    tq, tk = NSA_TQ, SLC_TK
    t0 = qi * tq
    diag = t0 // tk
    row = (pl.program_id(0) * NSA_GROUPS + pl.program_id(1)) * pl.num_programs(2) + qi
    n_un = jnp.int32(0)
    for j in range(k_ref.shape[1] // tk):
        tiles_sm[n_un] = jnp.int32(j)
        n_un = n_un + ((fl_ref[row, j] > 0) & (j < diag)).astype(jnp.int32)
    tiles_sm[n_un] = diag
    qh = _nsa_queries(q_ref, tab_ref, t0)
    selneg = sel_ref[0, 0]
    bm_sc[...] = jnp.concatenate([jnp.concatenate([q, selneg], axis=0) for q in qh], axis=1)
    qpos = t0 + lax.broadcasted_iota(jnp.int32, (tk, tq), 1)

    def k_tile(k0):
        return jnp.concatenate([k_ref[0, pl.ds(k0, tk), :], e_ref[pl.ds(k0, tk), :]], axis=1)

    def v_rows(h, k0):
        return vt_ref[0, :, pl.ds(k0, tk)]

    ops = _flash_ops(bm_sc, (s0_sc, s1_sc), mx_sc, acc_sc, NSA_HPG, tq, tk, k_tile, v_rows)
    o_win = _gated(_window_branch(qh, t0, kw_ref, vwt_ref, bias_ref, wbm_sc, (ws0_sc, ws1_sc),
                                  wacc_sc, lambda h: ops[0](tiles_sm[0], 0, h)),
                   _gates_t(gt_ref, bg_ref, 2))
    outs = _flash_t(ops, NSA_HPG, tq, tk, n_un, lambda i: tiles_sm[i], diag,
                    lambda key: key <= qpos, first_done=True)
    o_slc = _gated(outs, _gates_t(gt_ref, bg_ref, 1))
    _store_heads(o_ref, [a + b for a, b in zip(o_slc, o_win)])


def _win_bias():
    tq, nk = NSA_TQ, WINDOW + NSA_TQ
    out = []
    for p in range(WINDOW // tq + 1):
        t0 = p * tq
        key = max(t0 - WINDOW, 0) + np.arange(nk)[:, None]
        qpos = t0 + np.arange(tq)[None, :]
        out.append(np.where((key <= qpos) & (key > qpos - WINDOW), 0.0, NEG))
    return jnp.asarray(np.stack(out), F32)


def _block_onehot(S):
    e = (np.arange(S)[:, None] // SLC_BLOCK) == np.arange(NS_PAD)[None, :]
    return jnp.asarray(e, BF16)


def _slc_win_attention(u3, ut, selneg, flags, gt, bg, *, q_row, ks_col, vs_row, kw_col, vw_row):
    B, S, _ = u3.shape
    sp = {k: pl.BlockSpec(v.block_shape, lambda b, g, i, fl, f=v.index_map: f(b, g, i))
          for k, v in _nsa_specs(q_row).items()}
    bias = _win_bias()
    npat = bias.shape[0]
    kspec = lambda col: pl.BlockSpec((1, S, LANES), lambda b, g, i, fl: (b, 0, col + g))
    vspec = lambda row: pl.BlockSpec((1, HEAD_DIM, S), lambda b, g, i, fl: (b, row + g, 0))
    grid_spec = pltpu.PrefetchScalarGridSpec(
        num_scalar_prefetch=1,
        grid=(B, NSA_GROUPS, S // NSA_TQ),
        in_specs=[
            sp["q"], kspec(ks_col), vspec(vs_row),
            pl.BlockSpec((1, 1, NS_PAD, NSA_TQ), lambda b, g, i, fl: (b, g, 0, i)),
            pl.BlockSpec((S, NS_PAD), lambda b, g, i, fl: (0, 0)),
            kspec(kw_col), vspec(vw_row),
            pl.BlockSpec((1,) + bias.shape[1:],
                         lambda b, g, i, fl: (jnp.minimum(i, npat - 1), 0, 0)),
            sp["tab"], sp["gt"], sp["bg"],
        ],
        out_specs=sp["out"],
        scratch_shapes=_flash_scratch(NSA_HPG, NSA_TQ, SLC_TK, 2 * LANES)
        + _flash_scratch(NSA_HPG, NSA_TQ, WIN_TK, LANES, col_max=False)
        + [pltpu.SMEM((S // SLC_TK + 1,), jnp.int32)],
    )
    return pl.pallas_call(
        _slc_win_kernel,
        grid_spec=grid_spec,
        out_shape=jax.ShapeDtypeStruct((B, S, NSA_WIDTH), BF16),
        compiler_params=_cp(("parallel", "parallel", "arbitrary")),
        name="slc_win_attention",
    )(flags, ut, u3, ut, selneg, _block_onehot(S), u3, ut, bias, _slope_table(), gt, bg)


def _aug_groups(w):
    d = w.shape[0]
    w = w.reshape(d, NSA_GROUPS, HEAD_DIM)
    return jnp.pad(w, ((0, 0), (0, 0), (0, LANES - HEAD_DIM))).reshape(d, NSA_GROUPS * LANES)


def _even_layer(x, norm_g, w_in, b_f, gn_g, w_out):
    B, S, D = x.shape
    qscale = HEAD_DIM ** -0.5 * LOG2E
    q_f, k_f, v_f, w_fl, q_r, k_r, v_r, z = jnp.split(
        w_in, np.cumsum([FOX_WIDTH] * 3 + [FOX_HEADS] + [RET_WIDTH] * 3).tolist(), axis=1)
    w = jnp.concatenate([z, k_f, q_r, k_r * HEAD_DIM ** -0.5, v_r], axis=1).astype(BF16)
    w_t = jnp.concatenate([q_f * qscale, v_f], axis=1).T.astype(BF16)
    x2 = x.reshape(B * S, D)
    u, ut = _proj(x2, norm_g, w, seq=S, w_t=[w_t], t_dtypes=[BF16], tn=2 * PROJ_TN)
    u3 = u.reshape(B, S, -1)
    kfeat, qfeat = _fgate(x, norm_g, w_fl, b_f, tile=min(512, S))
    o_f = _fox(u3, ut, kfeat, qfeat, q_row=0, k_col=D // LANES, v_row=FOX_WIDTH // LANES)
    rb = (D + FOX_WIDTH) // RET_WIDTH
    o_r = _retention(u3, gn_g, q_col=rb, k_col=rb + 1, v_col=rb + 2)
    out = _out0(o_f.reshape(B * S, -1), o_r.reshape(B * S, -1), u, x2, w_out.astype(BF16))
    return out.reshape(B, S, D)


def _odd_layer(x, norm_g, w_in, b_gate, pe_k, pe_v, wk1, wk2, wv1, wv2, w_out, final_g):
    B, S, D = x.shape
    assert S // SLC_BLOCK <= NS_PAD
    qscale = HEAD_DIM ** -0.5 * LOG2E
    sizes = [NSA_WIDTH] + [NSA_KV_WIDTH] * 6 + [NSA_HEADS * N_BRANCH]
    q, kc, vc, ks, vs, kw, vw, gl, z = jnp.split(w_in, np.cumsum(sizes).tolist(), axis=1)
    w = jnp.concatenate([z, kc, vc, _aug_groups(ks), _aug_groups(kw)], axis=1).astype(BF16)
    per_group = NSA_HPG * N_BRANCH
    glt = jnp.pad(gl.T.reshape(NSA_GROUPS, per_group, D), ((0, 0), (0, GATE_ROWS - per_group), (0, 0)))
    glt = glt.reshape(NSA_GROUPS * GATE_ROWS, D).astype(BF16)
    bg = jnp.pad(b_gate.reshape(NSA_GROUPS, per_group), ((0, 0), (0, GATE_ROWS - per_group)))
    bg = bg.reshape(NSA_GROUPS * GATE_ROWS, 1)
    w_vt = jnp.concatenate([q * qscale, vs, vw], axis=1).T.astype(BF16)
    x2 = x.reshape(B * S, D)
    first_k = (D + 2 * NSA_KV_WIDTH) // PROJ_TN
    u, kc_a, vc_a, ut, gt = _proj(
        x2, norm_g, w, seq=S, split_tile=D // PROJ_TN,
        addend=_pos_features(jnp.arange(S, dtype=jnp.int32), PROJ_TN), add_tiles=(first_k, first_k + 1),
        w_t=[w_vt, glt], t_dtypes=[BF16, F32])
    u3 = u.reshape(B, S, -1)
    nseg = S // CMP_STRIDE
    kcmp = _compress(kc_a.reshape(B, nseg, -1), pe_k, wk1, wk2, transposed=False)
    vcmp_t = _compress(vc_a.reshape(B, nseg, -1), pe_v, wv1, wv2, transposed=True)
    o_c, selneg, flags = _cmp_attention(ut, kcmp, vcmp_t, gt, bg, q_row=0)
    kb = (D + 2 * NSA_KV_WIDTH) // LANES
    vb = NSA_WIDTH // HEAD_DIM
    o_s = _slc_win_attention(u3, ut, selneg, flags[:, 0, :S // SLC_TK], gt, bg, q_row=0, ks_col=kb,
                             vs_row=vb, kw_col=kb + NSA_GROUPS, vw_row=vb + NSA_GROUPS)
    r = lambda a: a.reshape(B * S, -1)
    out = _out1(r(o_c), r(o_s), u, x2, w_out.astype(BF16), final_g)
    return out.reshape(B, S, D)


def kernel(x, even_norm_g, even_w_in, even_b_f, even_gn_g, even_w_out, odd_norm_g, odd_w_in,
           odd_b_gate, odd_pe_k, odd_pe_v, odd_wk1, odd_wk2, odd_wv1, odd_wv2, odd_w_out, final_g):
    x = _even_layer(x, even_norm_g[0], even_w_in[0], even_b_f[0], even_gn_g[0], even_w_out[0])
    return _odd_layer(x, odd_norm_g[0], odd_w_in[0], odd_b_gate[0], odd_pe_k[0], odd_pe_v[0],
                      odd_wk1[0], odd_wk2[0], odd_wv1[0], odd_wv2[0], odd_w_out[0], final_g)
```

```python
import functools
import math

import jax
import jax.numpy as jnp
import numpy as np
from jax import lax
from jax.experimental import pallas as pl
from jax.experimental.pallas import tpu as pltpu

D_MODEL = 1024
HEAD_DIM = 64
LANES = 128
FOX_HEADS = 8
RET_HEADS = 8
FOX_WIDTH = FOX_HEADS * HEAD_DIM
RET_WIDTH = RET_HEADS * HEAD_DIM
RET_CHUNK = 128
RET_STEP = 8
NSA_HEADS = 16
NSA_GROUPS = 4
NSA_HPG = NSA_HEADS // NSA_GROUPS
NSA_WIDTH = NSA_HEADS * HEAD_DIM
NSA_KV_WIDTH = NSA_GROUPS * HEAD_DIM
N_BRANCH = 3
GATE_ROWS = 16
CMP_BLOCK = 32
CMP_STRIDE = 16
CMP_HIDDEN = 256
CMP_CHUNK = 128
SLC_BLOCK = 64
SLC_TOPK = 16
N_FORCED = 3
NS_PAD = LANES
WINDOW = 512
RMS_EPS = 1e-6
GN_EPS = 1e-5
NEG = -1e30
FORCE_BONUS = 1e6
MASK_BIG = 2.0 ** 100
LOG2E = math.log2(math.e)
FEAT0 = HEAD_DIM
QF_ROWS = 16
ACC_ROWS = HEAD_DIM + 16

PROJ_TM = 1024
PROJ_TN = 512
FOX_TQ = 512
FOX_TK = 512
NSA_TQ = 512
SLC_TK = 512
WIN_TK = 256
OUT_TM = 1024
VMEM_LIMIT = 48 * 1024 * 1024

F32 = jnp.float32
BF16 = jnp.bfloat16


def _cp(sem, vmem=VMEM_LIMIT):
    return pltpu.CompilerParams(dimension_semantics=sem, vmem_limit_bytes=vmem)


def _dot(a, b):
    return jnp.dot(a, b, preferred_element_type=F32)


def _dot_nt(a, b):
    return lax.dot_general(a, b, (((1,), (1,)), ((), ())), preferred_element_type=F32)


def _dot_tn(a, b):
    return lax.dot_general(a, b, (((0,), (0,)), ((), ())), preferred_element_type=F32)


def _rms(x, g):
    return x * lax.rsqrt(jnp.mean(x * x, axis=-1, keepdims=True) + RMS_EPS) * g


def _silu(x):
    return x * (1.0 / (1.0 + jnp.exp(-x)))


def _sigmoid(x):
    return 1.0 / (1.0 + jnp.exp(-x))


def _low_half(shape, axis):
    return lax.broadcasted_iota(jnp.int32, shape, axis) < HEAD_DIM


def _pieces(v):
    p1 = v.astype(BF16).astype(F32)
    r = v - p1
    p2 = r.astype(BF16).astype(F32)
    p3 = (r - p2).astype(BF16).astype(F32)
    return p1, p2, p3


def _np_pieces(v):
    v = np.asarray(v, np.float64)
    bf = lambda a: np.asarray(a, np.float32).astype(BF16).astype(np.float64)
    p1 = bf(v)
    p2 = bf(v - p1)
    p3 = bf(v - p1 - p2)
    return p1, p2, p3


def _place(lane, cols):
    out = jnp.zeros(lane.shape, F32)
    for i, c in cols.items():
        out = jnp.where(lane == i, c, out)
    return out


def _fgate_kernel(x_ref, g_ref, wf_ref, b_ref, kf_ref, qf_ref, carry):
    @pl.when(pl.program_id(1) == 0)
    def _():
        carry[...] = jnp.zeros_like(carry)

    h = _rms(x_ref[0], g_ref[...])
    t = h.shape[0]
    h1 = h.astype(BF16)
    h2 = (h - h1.astype(F32)).astype(BF16)
    w = wf_ref[...]
    w1 = w.astype(BF16)
    w2 = (w - w1.astype(F32)).astype(BF16)
    nf = w.shape[1]
    r = _dot(jnp.concatenate([h1, h2], axis=0), jnp.concatenate([w1, w2], axis=1))
    f = r[:t, :nf] + r[:t, nf:] + r[t:, :nf] + b_ref[...]
    ls = jnp.minimum(f, 0.0) - jnp.log(1.0 + jnp.exp(-jnp.abs(f)))
    r = lax.broadcasted_iota(jnp.int32, (t, t), 0)
    c = lax.broadcasted_iota(jnp.int32, (t, t), 1)
    lower = jnp.where(c <= r, 1.0, 0.0).astype(BF16)
    r = _dot(lower, jnp.concatenate([p.astype(BF16) for p in _pieces(ls)], axis=1))
    cs = r[:, :nf] + r[:, nf:2 * nf] + r[:, 2 * nf:] + carry[...]
    carry[...] = cs[t - 1:t, :]
    cl = cs * LOG2E
    lane = lax.broadcasted_iota(jnp.int32, (t, LANES), 1)
    one = jnp.ones((t, 1), F32)
    for j in range(FOX_HEADS // 2):
        a1, a2, a3 = _pieces(cl[:, 2 * j:2 * j + 1])
        b1, b2, b3 = _pieces(cl[:, 2 * j + 1:2 * j + 2])
        kf = _place(lane, {0: a1, 1: a2, 2: a3, 3: one, 4: one, 5: one, 6: b1, 7: b2, 8: b3})
        kf_ref[0, j] = kf.astype(BF16)
        qa = _place(lane, {0: -one, 1: -one, 2: -one, 3: a1, 4: a2, 5: a3})
        qb = _place(lane, {3: b1, 4: b2, 5: b3, 6: -one, 7: -one, 8: -one})
        qf_ref[0, j, 0] = qa.T[:QF_ROWS].astype(BF16)
        qf_ref[0, j, 1] = qb.T[:QF_ROWS].astype(BF16)


def _fgate(x, g, wf, b_f, *, tile):
    B, S, D = x.shape
    npair = FOX_HEADS // 2
    return pl.pallas_call(
        _fgate_kernel,
        grid=(B, S // tile),
        in_specs=[
            pl.BlockSpec((1, tile, D), lambda b, s: (b, s, 0)),
            pl.BlockSpec((1, D), lambda b, s: (0, 0)),
            pl.BlockSpec((D, FOX_HEADS), lambda b, s: (0, 0)),
            pl.BlockSpec((1, FOX_HEADS), lambda b, s: (0, 0)),
        ],
        out_specs=[
            pl.BlockSpec((1, npair, tile, LANES), lambda b, s: (b, 0, s, 0)),
            pl.BlockSpec((1, npair, 2, QF_ROWS, tile), lambda b, s: (b, 0, 0, 0, s)),
        ],
        out_shape=[
            jax.ShapeDtypeStruct((B, npair, S, LANES), BF16),
            jax.ShapeDtypeStruct((B, npair, 2, QF_ROWS, S), BF16),
        ],
        scratch_shapes=[pltpu.VMEM((1, FOX_HEADS), F32)],
        compiler_params=_cp(("parallel", "arbitrary")),
        name="fgate",
    )(x, g.reshape(1, D), wf, b_f.reshape(1, FOX_HEADS))


def _proj_kernel(*refs, tn, ntile, split_cols, add_cols, n_t):
    it = iter(refs)
    x_ref, g_ref, w_ref = next(it), next(it), next(it)
    add_ref = next(it) if add_cols else None
    wt_refs = [next(it) for _ in range(n_t)]
    u_ref = next(it)
    e_refs = [next(it) for _ in split_cols]
    ut_refs = [next(it) for _ in range(n_t)]
    h_sc = next(it)
    j = pl.program_id(1)

    @pl.when(j == 0)
    def _():
        h = _rms(x_ref[...], g_ref[...]).astype(BF16)
        h_sc[...] = h
        for wt_ref, ut_ref in zip(wt_refs, ut_refs):
            ut_ref[0] = _dot_nt(wt_ref[...], h).astype(ut_ref.dtype)

    acc = _dot(h_sc[...], w_ref[...])
    u_ref[...] = acc.astype(u_ref.dtype)

    def extras(t):
        lo = t * tn
        adds = [c - lo for c in add_cols if lo <= c < lo + tn]
        splits = [(e, c - lo) for e, c in zip(e_refs, split_cols) if lo <= c < lo + tn]

        def body():
            for c in adds:
                cols = slice(c, c + add_ref.shape[1])
                u_ref[:, cols] = (acc[:, cols] + add_ref[...].astype(F32)).astype(u_ref.dtype)
            for e_ref, c in splits:
                e_ref[...] = acc[:, c:c + e_ref.shape[1]].astype(e_ref.dtype)
        return body if adds or splits else None

    for t in range(ntile):
        body = extras(t)
        if body is not None:
            pl.when(j == t)(body)


def _proj(x2, g, w, *, seq, split_cols=(), split_width=0, addend=None, add_cols=(), w_t=(),
          t_dtypes=(), tm=PROJ_TM, tn=PROJ_TN):
    N, D = x2.shape
    W = w.shape[1]
    nbs = seq // tm
    in_specs = [
        pl.BlockSpec((tm, D), lambda i, j: (i, 0)),
        pl.BlockSpec((1, D), lambda i, j: (0, 0)),
        pl.BlockSpec((D, tn), lambda i, j: (0, j)),
    ]
    args = [x2, g.reshape(1, D), w]
    if add_cols:
        in_specs.append(pl.BlockSpec((tm, addend.shape[1]), lambda i, j: (i % nbs, 0)))
        args.append(addend)
    out_shape = [jax.ShapeDtypeStruct((N, W), BF16)]
    out_specs = [pl.BlockSpec((tm, tn), lambda i, j: (i, j))]
    for _ in split_cols:
        out_shape.append(jax.ShapeDtypeStruct((N, split_width), BF16))
        out_specs.append(pl.BlockSpec((tm, split_width), lambda i, j: (i, 0)))
    for wt, dt in zip(w_t, t_dtypes):
        rows = wt.shape[0]
        in_specs.append(pl.BlockSpec((rows, D), lambda i, j: (0, 0)))
        args.append(wt)
        out_shape.append(jax.ShapeDtypeStruct((N // seq, rows, seq), dt))
        out_specs.append(pl.BlockSpec((1, rows, tm), lambda i, j: (i // nbs, 0, i % nbs)))
    return pl.pallas_call(
        functools.partial(_proj_kernel, tn=tn, ntile=W // tn, split_cols=tuple(split_cols),
                          add_cols=tuple(add_cols), n_t=len(w_t)),
        grid=(N // tm, W // tn),
        in_specs=in_specs,
        out_specs=out_specs,
        out_shape=out_shape,
        scratch_shapes=[pltpu.VMEM((tm, D), BF16)],
        compiler_params=_cp(("parallel", "arbitrary")),
        name="proj",
    )(*args)


def _flash_ops(bm_sc, s_bufs, mx_sc, acc_sc, nh, tq, tk, k_tile, v_rows):
    acc_sc[...] = jnp.zeros_like(acc_sc)
    ones = jnp.ones((ACC_ROWS - HEAD_DIM, tk), BF16)

    def qk_head(tile, slot, h):
        cols = slice(h * tq, (h + 1) * tq)
        s = _dot(k_tile(pl.multiple_of(tile * tk, tk)), bm_sc[:, cols])
        s_bufs[slot][:, cols] = s
        if mx_sc is not None:
            mx_sc[slot, :, cols] = jnp.max(s, axis=0, keepdims=True)

    def soft_head(tile, slot, m_old, h, valid, bias):
        cols = slice(h * tq, (h + 1) * tq)
        buf = s_bufs[slot]
        k0 = pl.multiple_of(tile * tk, tk)
        if valid is not None or bias is not None:
            s = buf[:, cols] + bias if valid is None else jnp.where(valid, buf[:, cols], NEG)
            m_new = jnp.maximum(m_old, jnp.max(s, axis=0, keepdims=True))
            p = jnp.exp2(s - m_new).astype(BF16)
        else:
            m_new = jnp.maximum(m_old, mx_sc[slot, :, cols])
            p = jnp.exp2(buf[:, cols] - m_new).astype(BF16)
        alpha = jnp.exp2(m_old - m_new)
        lhs = jnp.concatenate([v_rows(h, k0), ones], axis=0)
        acc_sc[h] = alpha * acc_sc[h] + _dot(lhs, p)
        return m_new

    def qk_all(tile, slot):
        for h in range(nh):
            qk_head(tile, slot, h)

    def step(cur, slot, ms, nxt=None, valid=None, bias=None, before=None):
        out = []
        for h in range(nh):
            if nxt is not None:
                qk_head(nxt, 1 - slot, h)
            if before is not None:
                before(h)
            out.append(soft_head(cur, slot, ms[h], h, valid, bias))
        return tuple(out)

    def finish():
        outs = []
        for h in range(nh):
            a = acc_sc[h]
            outs.append(a[:HEAD_DIM] * (1.0 / a[HEAD_DIM:HEAD_DIM + 1]))
        return outs

    return qk_head, qk_all, step, finish


def _flash_t(ops, nh, tq, tk, n_un, tile_of, diag, valid_fn, first_done=False):
    _, qk_all, step, finish = ops
    key_iota = lax.broadcasted_iota(jnp.int32, (tk, tq), 0)
    diag_valid = lambda: valid_fn(pl.multiple_of(diag * tk, tk) + key_iota)

    def pair(i, ms):
        t_a, t_b, t_c = tile_of(2 * i), tile_of(2 * i + 1), tile_of(2 * i + 2)
        return step(t_b, 1, step(t_a, 0, ms, nxt=t_b), nxt=t_c)

    def odd_tail(ms):
        return step(diag, 1, step(tile_of(n_un - 1), 0, ms, nxt=diag), valid=diag_valid())

    def even_tail(ms):
        return step(diag, 0, ms, valid=diag_valid())

    if not first_done:
        qk_all(tile_of(0), 0)
    ms = (jnp.full((1, tq), NEG, F32),) * nh
    ms = lax.fori_loop(0, n_un // 2, pair, ms)
    lax.cond(n_un % 2 == 1, odd_tail, even_tail, ms)
    return finish()


def _flash_scratch(nh, tq, tk, kdim, col_max=True):
    bufs = [pltpu.VMEM((kdim, nh * tq), BF16), pltpu.VMEM((tk, nh * tq), F32),
            pltpu.VMEM((tk, nh * tq), F32)]
    if col_max:
        bufs.append(pltpu.VMEM((2, 1, nh * tq), F32))
    return bufs + [pltpu.VMEM((nh, ACC_ROWS, tq), F32)]


def _fox_kernel(qt_ref, qf_ref, k_ref, kf_ref, vt_ref, o_ref, bm_sc, s0_sc, s1_sc, mx_sc, acc_sc):
    qi = pl.program_id(2)
    tq, tk = FOX_TQ, FOX_TK
    bm_sc[...] = jnp.zeros_like(bm_sc)
    for h in range(2):
        rows = slice(h * HEAD_DIM, (h + 1) * HEAD_DIM)
        bm_sc[rows, h * tq:(h + 1) * tq] = qt_ref[0, rows, :]
        bm_sc[LANES:LANES + QF_ROWS, h * tq:(h + 1) * tq] = qf_ref[0, 0, h]
    qpos = qi * tq + lax.broadcasted_iota(jnp.int32, (tk, tq), 1)

    def k_tile(k0):
        return jnp.concatenate([k_ref[0, pl.ds(k0, tk), :], kf_ref[0, 0, pl.ds(k0, tk), :]], axis=1)

    def v_rows(h, k0):
        return vt_ref[0, h * HEAD_DIM:(h + 1) * HEAD_DIM, pl.ds(k0, tk)]

    diag = (qi * tq) // tk
    ops = _flash_ops(bm_sc, (s0_sc, s1_sc), mx_sc, acc_sc, 2, tq, tk, k_tile, v_rows)
    outs = _flash_t(ops, 2, tq, tk, diag, lambda i: i, diag, lambda key: key <= qpos)
    o_ref[0] = jnp.concatenate(outs, axis=0).T.astype(o_ref.dtype)


def _fox(u3, ut, kfeat, qfeat, *, q_row, k_col, v_row):
    B, S, _ = u3.shape
    npair = FOX_HEADS // 2
    return pl.pallas_call(
        _fox_kernel,
        grid=(B, npair, S // FOX_TQ),
        in_specs=[
            pl.BlockSpec((1, LANES, FOX_TQ), lambda b, j, i: (b, q_row + j, i)),
            pl.BlockSpec((1, 1, 2, QF_ROWS, FOX_TQ), lambda b, j, i: (b, j, 0, 0, i)),
            pl.BlockSpec((1, S, LANES), lambda b, j, i: (b, 0, k_col + j)),
            pl.BlockSpec((1, 1, S, LANES), lambda b, j, i: (b, j, 0, 0)),
            pl.BlockSpec((1, LANES, S), lambda b, j, i: (b, v_row + j, 0)),
        ],
        out_specs=pl.BlockSpec((1, FOX_TQ, LANES), lambda b, j, i: (b, i, j)),
        out_shape=jax.ShapeDtypeStruct((B, S, FOX_WIDTH), BF16),
        scratch_shapes=_flash_scratch(2, FOX_TQ, FOX_TK, 2 * LANES),
        compiler_params=_cp(("parallel", "parallel", "arbitrary")),
        name="fox",
    )(ut, qfeat, u3, kfeat, ut)


def _ret_kernel(q_ref, k_ref, v_ref, inner_ref, cross_ref, kdec_ref, cd_ref, bd_ref, gn_ref,
                o_ref, state_sc):
    @pl.when(pl.program_id(1) == 0)
    def _():
        state_sc[...] = jnp.zeros_like(state_sc)

    low = _low_half((RET_CHUNK, LANES), 1)
    inv = 1.0 / HEAD_DIM
    for c, j in [(c, j) for c in range(RET_STEP) for j in range(RET_HEADS // 2)]:
        rows = slice(c * RET_CHUNK, (c + 1) * RET_CHUNK)
        cols = slice(j * LANES, (j + 1) * LANES)
        q, k, v = q_ref[0, rows, cols], k_ref[0, rows, cols], v_ref[0, rows, cols]
        zero = jnp.zeros_like(q)
        qa, qb = jnp.where(low, q, zero), jnp.where(low, zero, q)
        pa = (_dot_nt(qa, k) * inner_ref[j, 0]).astype(BF16)
        pb = (_dot_nt(qb, k) * inner_ref[j, 1]).astype(BF16)
        o_in = jnp.where(low, _dot(pa, v), _dot(pb, v))
        state = state_sc[j]
        o = o_in + _dot(q, state.astype(BF16)) * cross_ref[j]
        kd = (k.astype(F32) * kdec_ref[j]).astype(BF16)
        state_sc[j] = state * cd_ref[j] + _dot_tn(kd, v) * bd_ref[...]
        sa = jnp.sum(jnp.where(low, o, 0.0), axis=-1, keepdims=True)
        st = jnp.sum(o, axis=-1, keepdims=True)
        mu = jnp.where(low, sa, st - sa) * inv
        d = o - mu
        d2 = d * d
        va = jnp.sum(jnp.where(low, d2, 0.0), axis=-1, keepdims=True)
        vt = jnp.sum(d2, axis=-1, keepdims=True)
        var = jnp.where(low, va, vt - va) * inv
        o_ref[0, rows, cols] = (d * lax.rsqrt(var + GN_EPS) * gn_ref[:, cols]).astype(o_ref.dtype)


def _ret_constants():
    lg = np.log(1.0 - 2.0 ** (-5.0 - np.arange(RET_HEADS)))
    i = np.arange(RET_CHUNK)
    diff = i[:, None] - i[None, :]
    inner = np.where(diff[None] >= 0, np.exp(lg[:, None, None] * np.maximum(diff, 0)[None]), 0.0)
    cross = np.exp(lg[:, None] * (i[None, :] + 1))
    kdec = np.exp(lg[:, None] * (RET_CHUNK - 1 - i)[None, :])
    cdec = np.exp(lg * RET_CHUNK)
    npair = RET_HEADS // 2
    inner = inner.reshape(npair, 2, RET_CHUNK, RET_CHUNK)

    def lanes(a):
        a = a.reshape(npair, 2, RET_CHUNK)
        return np.repeat(a.transpose(0, 2, 1), HEAD_DIM, axis=2)

    bd = np.kron(np.eye(2), np.ones((HEAD_DIM, HEAD_DIM)))
    cd = np.repeat(cdec.reshape(npair, 2), HEAD_DIM, axis=1)[:, :, None] * bd[None]
    f = lambda a: jnp.asarray(a, F32)
    return f(inner), f(lanes(cross)), f(lanes(kdec)), f(cd), f(bd)


def _retention(u3, gn_g, *, q_col, k_col, v_col):
    B, S, _ = u3.shape
    C = RET_CHUNK
    rows = RET_STEP * C
    npair = RET_HEADS // 2
    inner, cross, kdec, cd, bd = _ret_constants()
    full = lambda shape: pl.BlockSpec(shape, lambda b, i: (0,) * len(shape))
    return pl.pallas_call(
        _ret_kernel,
        grid=(B, S // rows),
        in_specs=[
            pl.BlockSpec((1, rows, RET_WIDTH), lambda b, i: (b, i, q_col)),
            pl.BlockSpec((1, rows, RET_WIDTH), lambda b, i: (b, i, k_col)),
            pl.BlockSpec((1, rows, RET_WIDTH), lambda b, i: (b, i, v_col)),
            full((npair, 2, C, C)), full((npair, C, LANES)), full((npair, C, LANES)),
            full((npair, LANES, LANES)), full((LANES, LANES)), full((1, RET_WIDTH)),
        ],
        out_specs=pl.BlockSpec((1, rows, RET_WIDTH), lambda b, i: (b, i, 0)),
        out_shape=jax.ShapeDtypeStruct((B, S, RET_WIDTH), BF16),
        scratch_shapes=[pltpu.VMEM((npair, LANES, LANES), F32)],
        compiler_params=_cp(("parallel", "arbitrary")),
        name="retention",
    )(u3, u3, u3, inner, cross, kdec, cd, bd, gn_g.reshape(1, RET_WIDTH))


def _out0_kernel(of_ref, or_ref, z_ref, x_ref, w_ref, o_ref):
    z = _silu(z_ref[...].astype(F32))
    ya = (of_ref[...].astype(F32) * z[:, :FOX_WIDTH]).astype(BF16)
    yb = (or_ref[...].astype(F32) * z[:, FOX_WIDTH:]).astype(BF16)
    o_ref[...] = x_ref[...] + _dot(ya, w_ref[:FOX_WIDTH, :]) + _dot(yb, w_ref[FOX_WIDTH:, :])


def _out0(o_f, o_r, u, x2, w_out, *, tm=OUT_TM):
    N, D = x2.shape
    return pl.pallas_call(
        _out0_kernel,
        grid=(N // tm,),
        in_specs=[
            pl.BlockSpec((tm, FOX_WIDTH), lambda i: (i, 0)),
            pl.BlockSpec((tm, RET_WIDTH), lambda i: (i, 0)),
            pl.BlockSpec((tm, D), lambda i: (i, 0)),
            pl.BlockSpec((tm, D), lambda i: (i, 0)),
            pl.BlockSpec((D, D), lambda i: (0, 0)),
        ],
        out_specs=pl.BlockSpec((tm, D), lambda i: (i, 0)),
        out_shape=jax.ShapeDtypeStruct((N, D), F32),
        compiler_params=_cp(("parallel",)),
        name="out0",
    )(o_f, o_r, u, x2, w_out)


def _out1_kernel(oc_ref, os_ref, z_ref, x_ref, w_ref, g_ref, o_ref):
    z = _silu(z_ref[...].astype(F32))
    y = ((oc_ref[...].astype(F32) + os_ref[...].astype(F32)) * z).astype(BF16)
    o_ref[...] = _rms(x_ref[...] + _dot(y, w_ref[...]), g_ref[...])


def _out1(o_c, o_s, u, x2, w_out, final_g, *, tm=OUT_TM):
    N, D = x2.shape
    row = pl.BlockSpec((tm, D), lambda i: (i, 0))
    return pl.pallas_call(
        _out1_kernel,
        grid=(N // tm,),
        in_specs=[row, row, row, row,
                  pl.BlockSpec((D, D), lambda i: (0, 0)),
                  pl.BlockSpec((1, D), lambda i: (0, 0))],
        out_specs=row,
        out_shape=jax.ShapeDtypeStruct((N, D), F32),
        compiler_params=_cp(("parallel",)),
        name="out1",
    )(o_c, o_s, u, x2, w_out, final_g.reshape(1, D))


def _compress_kernel(x_ref, pea_ref, peb_ref, wa_ref, wb_ref, w2_ref, *rest, transposed):
    x = x_ref[0].astype(F32)
    a = _dot((x + pea_ref[...]).astype(BF16), wa_ref[0])
    b = _dot((x + peb_ref[...]).astype(BF16), wb_ref[0])
    nseg = x.shape[0]
    pre = a + pltpu.roll(b, nseg - 1, 0)
    hid = _silu(pre).astype(BF16)
    if transposed:
        o_ref, = rest
        o_ref[0, 0] = _dot_nt(w2_ref[...], hid).astype(o_ref.dtype)
    else:
        feat_ref, o_ref = rest
        o_ref[0, 0] = (_dot(hid, w2_ref[...]) + feat_ref[...].astype(F32)).astype(o_ref.dtype)


def _compress(a3, pe, w1, w2, *, transposed):
    B, nseg, wid = a3.shape
    half = CMP_STRIDE * HEAD_DIM
    eye = jnp.eye(NSA_GROUPS, dtype=w1.dtype)

    def big(wh):
        w4 = wh.reshape(CMP_STRIDE, 1, HEAD_DIM, CMP_HIDDEN)
        sel = eye[:, None, :, None, None]
        return (sel * w4[None]).reshape(NSA_GROUPS, wid, CMP_HIDDEN).astype(BF16)

    def pe_big(p):
        return jnp.broadcast_to(p[:, None, :], (CMP_STRIDE, NSA_GROUPS, HEAD_DIM)).reshape(1, wid)

    args = [a3, pe_big(pe[:CMP_STRIDE]), pe_big(pe[CMP_STRIDE:]), big(w1[:half]), big(w1[half:])]
    in_specs = [
        pl.BlockSpec((1, nseg, wid), lambda b, g: (b, 0, 0)),
        pl.BlockSpec((1, wid), lambda b, g: (0, 0)),
        pl.BlockSpec((1, wid), lambda b, g: (0, 0)),
        pl.BlockSpec((1, wid, CMP_HIDDEN), lambda b, g: (g, 0, 0)),
        pl.BlockSpec((1, wid, CMP_HIDDEN), lambda b, g: (g, 0, 0)),
    ]
    if transposed:
        w2d = w2.T.astype(BF16)
        oshape, oblock = (B, NSA_GROUPS, HEAD_DIM, nseg), (1, 1, HEAD_DIM, nseg)
        args.append(w2d)
        in_specs.append(pl.BlockSpec(w2d.shape, lambda b, g: (0, 0)))
    else:
        w2d = jnp.pad(w2, ((0, 0), (0, LANES - HEAD_DIM))).astype(BF16)
        oshape, oblock = (B, NSA_GROUPS, nseg, LANES), (1, 1, nseg, LANES)
        feat = _pos_features(jnp.arange(nseg, dtype=jnp.int32) * CMP_STRIDE + (CMP_BLOCK - 1), LANES)
        args += [w2d, feat]
        in_specs += [pl.BlockSpec(w2d.shape, lambda b, g: (0, 0)),
                     pl.BlockSpec((nseg, LANES), lambda b, g: (0, 0))]
    return pl.pallas_call(
        functools.partial(_compress_kernel, transposed=transposed),
        grid=(B, NSA_GROUPS),
        in_specs=in_specs,
        out_specs=pl.BlockSpec(oblock, lambda b, g: (b, g, 0, 0)),
        out_shape=jax.ShapeDtypeStruct(oshape, BF16),
        compiler_params=_cp(("parallel", "parallel")),
        name="compress",
    )(*args)


def _slope_table():
    s = np.asarray(2.0 ** (-8.0 * (np.arange(NSA_HEADS) + 1) / NSA_HEADS), np.float32)
    sl = np.asarray(s.astype(np.float64) * LOG2E, np.float32)
    p1, p2, p3 = _np_pieces(sl)
    tab = np.zeros((NSA_HEADS, QF_ROWS), np.float32)
    for k, p in enumerate((p1, p1, p2, p2, p3, p3)):
        tab[:, k] = p
    tab[:, 6] = sl
    tab = np.broadcast_to(tab.reshape(NSA_GROUPS, NSA_HPG * QF_ROWS, 1),
                          (NSA_GROUPS, NSA_HPG * QF_ROWS, NSA_TQ))
    return jnp.asarray(tab)


def _pos_features(pos, width):
    pos = pos[:, None]
    lane = jnp.arange(width, dtype=jnp.int32)[None, :] % LANES
    hi = ((pos // SLC_BLOCK) * SLC_BLOCK).astype(F32)
    lo = (pos % SLC_BLOCK).astype(F32)
    k = lane - FEAT0
    f = jnp.where((k >= 0) & (k < 6), jnp.where(k % 2 == 0, hi, lo), 0.0)
    f = jnp.where((k >= 6) & (k < 9), 1.0, f)
    return f.astype(BF16)


def _nsa_queries(qt_ref, tab_ref, t0):
    tq = qt_ref.shape[2]
    r = lax.broadcasted_iota(jnp.int32, (QF_ROWS, tq), 0)
    t = (t0 + lax.broadcasted_iota(jnp.int32, (1, tq), 1)).astype(F32)
    zeros = jnp.zeros((LANES - HEAD_DIM - QF_ROWS, tq), BF16)
    out = []
    for i in range(NSA_HPG):
        tile = tab_ref[0, i * QF_ROWS:(i + 1) * QF_ROWS, :]
        a1, a2, a3 = _pieces(-(tile[6:7, :] * t))
        feat = jnp.where(r == 6, a1, jnp.where(r == 7, a2, jnp.where(r == 8, a3,
                                                                     jnp.where(r < 6, tile, 0.0))))
        out.append(jnp.concatenate([qt_ref[0, i * HEAD_DIM:(i + 1) * HEAD_DIM, :],
                                    feat.astype(BF16), zeros], axis=0))
    return out


def _gates_t(gt_ref, bg_ref, branch):
    gl = gt_ref[0] + bg_ref[...]
    return [_sigmoid(gl[N_BRANCH * i + branch:N_BRANCH * i + branch + 1, :]) for i in range(NSA_HPG)]


def _gated(outs_t, gates):
    return [o * gt for o, gt in zip(outs_t, gates)]


def _store_heads(o_ref, g):
    o_ref[0, :, :LANES] = jnp.concatenate(g[:2], axis=0).T.astype(o_ref.dtype)
    o_ref[0, :, LANES:] = jnp.concatenate(g[2:], axis=0).T.astype(o_ref.dtype)


def _nsa_specs(q_row):
    return dict(
        q=pl.BlockSpec((1, NSA_HPG * HEAD_DIM, NSA_TQ), lambda b, g, i: (b, q_row + g, i)),
        tab=pl.BlockSpec((1, NSA_HPG * QF_ROWS, NSA_TQ), lambda b, g, i: (g, 0, 0)),
        gt=pl.BlockSpec((1, GATE_ROWS, NSA_TQ), lambda b, g, i: (b, g, i)),
        bg=pl.BlockSpec((GATE_ROWS, 1), lambda b, g, i: (g, 0)),
        out=pl.BlockSpec((1, NSA_TQ, NSA_HPG * HEAD_DIM), lambda b, g, i: (b, i, g)),
    )


def _argmax_first(v, idx):
    n = v.shape[0]
    slabs = [(v[r:r + 8], idx[r:r + 8]) for r in range(0, n, 8)]
    while len(slabs) > 1:
        nxt = []
        for (va, ia), (vb, ib) in zip(slabs[0::2], slabs[1::2]):
            right = vb > va
            nxt.append((jnp.where(right, vb, va), jnp.where(right, ib, ia)))
        if len(slabs) % 2:
            nxt.append(slabs[-1])
        slabs = nxt
    v, idx = slabs[0]
    mx = jnp.max(v, axis=0, keepdims=True)
    first = jnp.min(jnp.where(v == mx, idx, float(n)), axis=0, keepdims=True)
    return mx, first


def _cmp_body(nc, t0, qh, kc_ref, vct_ref, mt_ref, grp_ref, gates, o_ref, sel_ref, flag_ref):
    tq = NSA_TQ
    rows = nc * CMP_CHUNK
    full = max(rows - CMP_CHUNK - 8, 0)
    nseg = kc_ref.shape[2]
    kc = kc_ref[0, 0, :rows, :]
    vct = vct_ref[0, 0, :, :rows]
    t = t0 + lax.broadcasted_iota(jnp.int32, (1, tq), 1)
    cidx = full + lax.broadcasted_iota(jnp.int32, (rows - full, 1), 0)
    valid = (cidx * CMP_STRIDE + (CMP_BLOCK - 1) <= t) & (cidx < nseg - 1)
    psum = jnp.zeros((rows, tq), F32)
    ps = []
    s_all = _dot(kc, jnp.concatenate(qh, axis=1))
    for i in range(NSA_HPG):
        s = s_all[:, i * tq:(i + 1) * tq]
        s_last = jnp.where(valid, s[full:], NEG)
        m = jnp.max(s_last, axis=0, keepdims=True)
        if nc > 1:
            m = jnp.maximum(m, jnp.max(s[:full], axis=0, keepdims=True))
        e = jnp.where(valid, jnp.exp2(s_last - m), 0.0)
        if nc > 1:
            e = jnp.concatenate([jnp.exp2(s[:full] - m), e], axis=0)
        l = jnp.sum(e, axis=0, keepdims=True)
        p = e * jnp.where(l > 0.0, 1.0 / l, 0.0)
        psum = psum + p
        ps.append(p.astype(BF16))
    o_all = _dot(vct, jnp.concatenate(ps, axis=1))
    _store_heads(o_ref, _gated([o_all[:, i * tq:(i + 1) * tq] for i in range(NSA_HPG)], gates))
    ns = rows * CMP_STRIDE // SLC_BLOCK
    mt = mt_ref[:ns, :rows]
    imp = sum(_dot(mt, p.astype(BF16)) for p in _pieces(psum))
    blk = lax.broadcasted_iota(jnp.int32, (ns, 1), 0)
    cur = t // SLC_BLOCK
    bvalid = blk * SLC_BLOCK <= t
    forced = (blk == 0) | (blk == cur) | (blk == cur - 1)
    score = jnp.where(forced, -jnp.inf, jnp.where(bvalid, imp, NEG))
    blk_f = jnp.broadcast_to(blk.astype(F32), (ns, tq))
    work = score
    for _ in range(SLC_TOPK - N_FORCED):
        mx, first = _argmax_first(work, blk_f)
        work = jnp.where(blk_f == first, -jnp.inf, work)
    picked = (score > mx) | ((score == mx) & (blk_f <= first))
    selneg = jnp.where(bvalid & (forced | picked), 0.0, -MASK_BIG)
    if ns < NS_PAD:
        selneg = jnp.concatenate([selneg, jnp.full((NS_PAD - ns, tq), -MASK_BIG, F32)], axis=0)
    sel_ref[0, 0] = selneg.astype(sel_ref.dtype)
    picked = jnp.where(selneg == 0.0, 1.0, 0.0).astype(BF16)
    used = _dot_nt(jnp.ones((8, tq), BF16), picked)
    used = jnp.where(used > 0.0, 1.0, 0.0).astype(BF16)
    flag_ref[0] = (_dot(used, grp_ref[...])[0:1] > 0.0).astype(jnp.int32)


def _cmp_kernel(q_ref, kc_ref, vct_ref, mt_ref, grp_ref, tab_ref, gt_ref, bg_ref,
                o_ref, sel_ref, flag_ref):
    t0 = pl.program_id(2) * NSA_TQ
    qh = _nsa_queries(q_ref, tab_ref, t0)
    gates = _gates_t(gt_ref, bg_ref, 0)
    nchunk = kc_ref.shape[2] // CMP_CHUNK
    last = t0 // (CMP_CHUNK * CMP_STRIDE)
    for nc in range(1, nchunk + 1):
        pl.when(last == nc - 1)(functools.partial(
            _cmp_body, nc, t0, qh, kc_ref, vct_ref, mt_ref, grp_ref, gates, o_ref, sel_ref, flag_ref))


def _cmp_to_slc_t(nseg, ns):
    c0 = np.arange(nseg)[:, None] * CMP_STRIDE
    s0 = np.arange(ns)[None, :] * SLC_BLOCK
    overlap = np.clip(np.minimum(c0 + CMP_BLOCK, s0 + SLC_BLOCK) - np.maximum(c0, s0), 0, None)
    m = overlap / CMP_STRIDE
    m[nseg - 1] = 0.0
    mt = np.zeros((NS_PAD, nseg))
    mt[:ns] = m.T
    return jnp.asarray(mt, BF16)


def _tile_groups():
    per = SLC_TK // SLC_BLOCK
    g = (np.arange(NS_PAD)[:, None] // per) == np.arange(NS_PAD)[None, :]
    return jnp.asarray(g, BF16)


def _cmp_attention(ut, kcmp, vcmp_t, gt, bg, *, q_row):
    B, _, S = ut.shape
    nseg = kcmp.shape[2]
    nq = S // NSA_TQ
    sp = _nsa_specs(q_row)
    return pl.pallas_call(
        _cmp_kernel,
        grid=(B, NSA_GROUPS, S // NSA_TQ),
        in_specs=[
            sp["q"],
            pl.BlockSpec((1, 1, nseg, LANES), lambda b, g, i: (b, g, 0, 0)),
            pl.BlockSpec((1, 1, HEAD_DIM, nseg), lambda b, g, i: (b, g, 0, 0)),
            pl.BlockSpec((NS_PAD, nseg), lambda b, g, i: (0, 0)),
            pl.BlockSpec((NS_PAD, NS_PAD), lambda b, g, i: (0, 0)),
            sp["tab"], sp["gt"], sp["bg"],
        ],
        out_specs=[sp["out"], pl.BlockSpec((1, 1, NS_PAD, NSA_TQ), lambda b, g, i: (b, g, 0, i)),
                   pl.BlockSpec((1, 1, NS_PAD), lambda b, g, i: ((b * NSA_GROUPS + g) * nq + i, 0, 0))],
        out_shape=[jax.ShapeDtypeStruct((B, S, NSA_WIDTH), BF16),
                   jax.ShapeDtypeStruct((B, NSA_GROUPS, NS_PAD, S), BF16),
                   jax.ShapeDtypeStruct((B * NSA_GROUPS * nq, 1, NS_PAD), jnp.int32)],
        compiler_params=_cp(("parallel", "parallel", "arbitrary")),
        name="cmp_attention",
    )(ut, kcmp, vcmp_t, _cmp_to_slc_t(nseg, S // SLC_BLOCK), _tile_groups(), _slope_table(), gt, bg)


def _window_branch(qh, t0, k_ref, vt_ref, bias_ref, bm_sc, s_bufs, acc_sc, last_before):
    tq, tk = NSA_TQ, WIN_TK
    bm_sc[...] = jnp.concatenate(qh, axis=1)
    _, qk_all, step, finish = _flash_ops(
        bm_sc, s_bufs, None, acc_sc, NSA_HPG, tq, tk,
        lambda k0: k_ref[0, pl.ds(k0, tk), :], lambda h, k0: vt_ref[0, :, pl.ds(k0, tk)])
    first = jnp.maximum(t0 - WINDOW, 0) // tk
    ntile = (WINDOW + tq) // tk
    qk_all(first, 0)
    ms = (jnp.full((1, tq), NEG, F32),) * NSA_HPG
    for j in range(ntile):
        last = j + 1 == ntile
        ms = step(first + j, j % 2, ms, nxt=None if last else first + j + 1,
                  bias=bias_ref[0, j * tk:(j + 1) * tk, :], before=last_before if last else None)
    return finish()


def _slc_win_kernel(fl_ref, q_ref, k_ref, vt_ref, sel_ref, e_ref, kw_ref, vwt_ref, bias_ref,
                    tab_ref, gt_ref, bg_ref, o_ref, bm_sc, s0_sc, s1_sc, mx_sc, acc_sc,
                    wbm_sc, ws0_sc, ws1_sc, wacc_sc, tiles_sm):
    qi = pl.program_id(2)
    tq, tk = NSA_TQ, SLC_TK
    t0 = qi * tq
    diag = t0 // tk
    row = (pl.program_id(0) * NSA_GROUPS + pl.program_id(1)) * pl.num_programs(2) + qi
    n_un = jnp.int32(0)
    for j in range(k_ref.shape[1] // tk):
        tiles_sm[n_un] = jnp.int32(j)
        n_un = n_un + ((fl_ref[row, j] > 0) & (j < diag)).astype(jnp.int32)
    tiles_sm[n_un] = diag
    qh = _nsa_queries(q_ref, tab_ref, t0)
    selneg = sel_ref[0, 0]
    bm_sc[...] = jnp.concatenate([jnp.concatenate([q, selneg], axis=0) for q in qh], axis=1)
    qpos = t0 + lax.broadcasted_iota(jnp.int32, (tk, tq), 1)

    def k_tile(k0):
        return jnp.concatenate([k_ref[0, pl.ds(k0, tk), :], e_ref[pl.ds(k0, tk), :]], axis=1)

    def v_rows(h, k0):
        return vt_ref[0, :, pl.ds(k0, tk)]

    ops = _flash_ops(bm_sc, (s0_sc, s1_sc), mx_sc, acc_sc, NSA_HPG, tq, tk, k_tile, v_rows)
    o_win = _gated(_window_branch(qh, t0, kw_ref, vwt_ref, bias_ref, wbm_sc, (ws0_sc, ws1_sc),
                                  wacc_sc, lambda h: ops[0](tiles_sm[0], 0, h)),
                   _gates_t(gt_ref, bg_ref, 2))
    outs = _flash_t(ops, NSA_HPG, tq, tk, n_un, lambda i: tiles_sm[i], diag,
                    lambda key: key <= qpos, first_done=True)
    o_slc = _gated(outs, _gates_t(gt_ref, bg_ref, 1))
    _store_heads(o_ref, [a + b for a, b in zip(o_slc, o_win)])


def _win_bias():
    tq, nk = NSA_TQ, WINDOW + NSA_TQ
    out = []
    for p in range(WINDOW // tq + 1):
        t0 = p * tq
        key = max(t0 - WINDOW, 0) + np.arange(nk)[:, None]
        qpos = t0 + np.arange(tq)[None, :]
        out.append(np.where((key <= qpos) & (key > qpos - WINDOW), 0.0, NEG))
    return jnp.asarray(np.stack(out), F32)


def _block_onehot(S):
    e = (np.arange(S)[:, None] // SLC_BLOCK) == np.arange(NS_PAD)[None, :]
    return jnp.asarray(e, BF16)


def _slc_win_attention(u3, ut, selneg, flags, gt, bg, *, q_row, ks_col, vs_row, kw_col, vw_row):
    B, S, _ = u3.shape
    sp = {k: pl.BlockSpec(v.block_shape, lambda b, g, i, fl, f=v.index_map: f(b, g, i))
          for k, v in _nsa_specs(q_row).items()}
    bias = _win_bias()
    npat = bias.shape[0]
    kspec = lambda col: pl.BlockSpec((1, S, LANES), lambda b, g, i, fl: (b, 0, col + g))
    vspec = lambda row: pl.BlockSpec((1, HEAD_DIM, S), lambda b, g, i, fl: (b, row + g, 0))
    grid_spec = pltpu.PrefetchScalarGridSpec(
        num_scalar_prefetch=1,
        grid=(B, NSA_GROUPS, S // NSA_TQ),
        in_specs=[
            sp["q"], kspec(ks_col), vspec(vs_row),
            pl.BlockSpec((1, 1, NS_PAD, NSA_TQ), lambda b, g, i, fl: (b, g, 0, i)),
            pl.BlockSpec((S, NS_PAD), lambda b, g, i, fl: (0, 0)),
            kspec(kw_col), vspec(vw_row),
            pl.BlockSpec((1,) + bias.shape[1:],
                         lambda b, g, i, fl: (jnp.minimum(i, npat - 1), 0, 0)),
            sp["tab"], sp["gt"], sp["bg"],
        ],
        out_specs=sp["out"],
        scratch_shapes=_flash_scratch(NSA_HPG, NSA_TQ, SLC_TK, 2 * LANES)
        + _flash_scratch(NSA_HPG, NSA_TQ, WIN_TK, LANES, col_max=False)
        + [pltpu.SMEM((S // SLC_TK + 1,), jnp.int32)],
    )
    return pl.pallas_call(
        _slc_win_kernel,
        grid_spec=grid_spec,
        out_shape=jax.ShapeDtypeStruct((B, S, NSA_WIDTH), BF16),
        compiler_params=_cp(("parallel", "parallel", "arbitrary")),
        name="slc_win_attention",
    )(flags, ut, u3, ut, selneg, _block_onehot(S), u3, ut, bias, _slope_table(), gt, bg)


def _aug_groups(w):
    d = w.shape[0]
    w = w.reshape(d, NSA_GROUPS, HEAD_DIM)
    return jnp.pad(w, ((0, 0), (0, 0), (0, LANES - HEAD_DIM))).reshape(d, NSA_GROUPS * LANES)


def _even_layer(x, norm_g, w_in, b_f, gn_g, w_out):
    B, S, D = x.shape
    qscale = HEAD_DIM ** -0.5 * LOG2E
    q_f, k_f, v_f, w_fl, q_r, k_r, v_r, z = jnp.split(
        w_in, np.cumsum([FOX_WIDTH] * 3 + [FOX_HEADS] + [RET_WIDTH] * 3).tolist(), axis=1)
    w = jnp.concatenate([z, k_f, q_r, k_r * HEAD_DIM ** -0.5, v_r], axis=1).astype(BF16)
    w_t = jnp.concatenate([q_f * qscale, v_f], axis=1).T.astype(BF16)
    x2 = x.reshape(B * S, D)
    u, ut = _proj(x2, norm_g, w, seq=S, w_t=[w_t], t_dtypes=[BF16], tn=2 * PROJ_TN)
    u3 = u.reshape(B, S, -1)
    kfeat, qfeat = _fgate(x, norm_g, w_fl, b_f, tile=min(512, S))
    o_f = _fox(u3, ut, kfeat, qfeat, q_row=0, k_col=D // LANES, v_row=FOX_WIDTH // LANES)
    rb = (D + FOX_WIDTH) // RET_WIDTH
    o_r = _retention(u3, gn_g, q_col=rb, k_col=rb + 1, v_col=rb + 2)
    out = _out0(o_f.reshape(B * S, -1), o_r.reshape(B * S, -1), u, x2, w_out.astype(BF16))
    return out.reshape(B, S, D)


def _odd_layer(x, norm_g, w_in, b_gate, pe_k, pe_v, wk1, wk2, wv1, wv2, w_out, final_g):
    B, S, D = x.shape
    assert S // SLC_BLOCK <= NS_PAD
    qscale = HEAD_DIM ** -0.5 * LOG2E
    sizes = [NSA_WIDTH] + [NSA_KV_WIDTH] * 6 + [NSA_HEADS * N_BRANCH]
    q, kc, vc, ks, vs, kw, vw, gl, z = jnp.split(w_in, np.cumsum(sizes).tolist(), axis=1)
    w = jnp.concatenate([z, kc, vc, _aug_groups(ks), _aug_groups(kw)], axis=1).astype(BF16)
    per_group = NSA_HPG * N_BRANCH
    glt = jnp.pad(gl.T.reshape(NSA_GROUPS, per_group, D), ((0, 0), (0, GATE_ROWS - per_group), (0, 0)))
    glt = glt.reshape(NSA_GROUPS * GATE_ROWS, D).astype(BF16)
    bg = jnp.pad(b_gate.reshape(NSA_GROUPS, per_group), ((0, 0), (0, GATE_ROWS - per_group)))
    bg = bg.reshape(NSA_GROUPS * GATE_ROWS, 1)
    w_vt = jnp.concatenate([q * qscale, vs, vw], axis=1).T.astype(BF16)
    x2 = x.reshape(B * S, D)
    kcol = D + 2 * NSA_KV_WIDTH
    kwid = NSA_GROUPS * LANES
    u, kc_a, vc_a, ut, gt = _proj(
        x2, norm_g, w, seq=S, split_cols=(D, D + NSA_KV_WIDTH), split_width=NSA_KV_WIDTH,
        addend=_pos_features(jnp.arange(S, dtype=jnp.int32), kwid), add_cols=(kcol, kcol + kwid),
        w_t=[w_vt, glt], t_dtypes=[BF16, F32], tn=w.shape[1] // 2)
    u3 = u.reshape(B, S, -1)
    nseg = S // CMP_STRIDE
    kcmp = _compress(kc_a.reshape(B, nseg, -1), pe_k, wk1, wk2, transposed=False)
    vcmp_t = _compress(vc_a.reshape(B, nseg, -1), pe_v, wv1, wv2, transposed=True)
    o_c, selneg, flags = _cmp_attention(ut, kcmp, vcmp_t, gt, bg, q_row=0)
    kb = (D + 2 * NSA_KV_WIDTH) // LANES
    vb = NSA_WIDTH // HEAD_DIM
    o_s = _slc_win_attention(u3, ut, selneg, flags[:, 0, :S // SLC_TK], gt, bg, q_row=0, ks_col=kb,
                             vs_row=vb, kw_col=kb + NSA_GROUPS, vw_row=vb + NSA_GROUPS)
    r = lambda a: a.reshape(B * S, -1)
    out = _out1(r(o_c), r(o_s), u, x2, w_out.astype(BF16), final_g)
    return out.reshape(B, S, D)


def kernel(x, even_norm_g, even_w_in, even_b_f, even_gn_g, even_w_out, odd_norm_g, odd_w_in,
           odd_b_gate, odd_pe_k, odd_pe_v, odd_wk1, odd_wk2, odd_wv1, odd_wv2, odd_w_out, final_g):
    x = _even_layer(x, even_norm_g[0], even_w_in[0], even_b_f[0], even_gn_g[0], even_w_out[0])
    return _odd_layer(x, odd_norm_g[0], odd_w_in[0], odd_b_gate[0], odd_pe_k[0], odd_pe_v[0],
                      odd_wk1[0], odd_wk2[0], odd_wv1[0], odd_wv2[0], odd_w_out[0], final_g)
```

```python
import functools
import math

import jax
import jax.numpy as jnp
import numpy as np
from jax import lax
from jax.experimental import pallas as pl
from jax.experimental.pallas import tpu as pltpu

D_MODEL = 1024
HEAD_DIM = 64
LANES = 128
FOX_HEADS = 8
RET_HEADS = 8
FOX_WIDTH = FOX_HEADS * HEAD_DIM
RET_WIDTH = RET_HEADS * HEAD_DIM
RET_CHUNK = 128
RET_STEP = 8
NSA_HEADS = 16
NSA_GROUPS = 4
NSA_HPG = NSA_HEADS // NSA_GROUPS
NSA_WIDTH = NSA_HEADS * HEAD_DIM
NSA_KV_WIDTH = NSA_GROUPS * HEAD_DIM
N_BRANCH = 3
GATE_ROWS = 16
CMP_BLOCK = 32
CMP_STRIDE = 16
CMP_HIDDEN = 256
CMP_CHUNK = 128
SLC_BLOCK = 64
SLC_TOPK = 16
N_FORCED = 3
NS_PAD = LANES
WINDOW = 512
RMS_EPS = 1e-6
GN_EPS = 1e-5
NEG = -1e30
FORCE_BONUS = 1e6
MASK_BIG = 2.0 ** 100
LOG2E = math.log2(math.e)
FEAT0 = HEAD_DIM
QF_ROWS = 16
ACC_ROWS = HEAD_DIM + 16

PROJ_TM = 1024
PROJ_TN = 512
FOX_TQ = 512
FOX_TK = 512
NSA_TQ = 512
SLC_TK = 512
WIN_TK = 256
OUT_TM = 1024
VMEM_LIMIT = 48 * 1024 * 1024

F32 = jnp.float32
BF16 = jnp.bfloat16


def _cp(sem, vmem=VMEM_LIMIT):
    return pltpu.CompilerParams(dimension_semantics=sem, vmem_limit_bytes=vmem)


def _dot(a, b):
    return jnp.dot(a, b, preferred_element_type=F32)


def _dot_nt(a, b):
    return lax.dot_general(a, b, (((1,), (1,)), ((), ())), preferred_element_type=F32)


def _dot_tn(a, b):
    return lax.dot_general(a, b, (((0,), (0,)), ((), ())), preferred_element_type=F32)


def _rms(x, g):
    return x * lax.rsqrt(jnp.mean(x * x, axis=-1, keepdims=True) + RMS_EPS) * g


def _silu(x):
    return x * (1.0 / (1.0 + jnp.exp(-x)))


def _sigmoid(x):
    return 1.0 / (1.0 + jnp.exp(-x))


def _low_half(shape, axis):
    return lax.broadcasted_iota(jnp.int32, shape, axis) < HEAD_DIM


def _pieces(v):
    p1 = v.astype(BF16).astype(F32)
    r = v - p1
    p2 = r.astype(BF16).astype(F32)
    p3 = (r - p2).astype(BF16).astype(F32)
    return p1, p2, p3


def _np_pieces(v):
    v = np.asarray(v, np.float64)
    bf = lambda a: np.asarray(a, np.float32).astype(BF16).astype(np.float64)
    p1 = bf(v)
    p2 = bf(v - p1)
    p3 = bf(v - p1 - p2)
    return p1, p2, p3


def _fgate_kernel(x_ref, g_ref, wf_ref, b_ref, pk_ref, kc_ref, pq_ref, qc_ref, kf_ref, qf_ref,
                  carry):
    @pl.when(pl.program_id(1) == 0)
    def _():
        carry[...] = jnp.zeros_like(carry)

    h = _rms(x_ref[0], g_ref[...])
    t = h.shape[0]
    h1 = h.astype(BF16)
    h2 = (h - h1.astype(F32)).astype(BF16)
    w = wf_ref[...]
    w1 = w.astype(BF16)
    w2 = (w - w1.astype(F32)).astype(BF16)
    nf = w.shape[1]
    r = _dot(jnp.concatenate([h1, h2], axis=0), jnp.concatenate([w1, w2], axis=1))
    f = r[:t, :nf] + r[:t, nf:] + r[t:, :nf] + b_ref[...]
    ls = jnp.minimum(f, 0.0) - jnp.log(1.0 + jnp.exp(-jnp.abs(f)))
    r = lax.broadcasted_iota(jnp.int32, (t, t), 0)
    c = lax.broadcasted_iota(jnp.int32, (t, t), 1)
    lower = jnp.where(c <= r, 1.0, 0.0).astype(BF16)
    r = _dot(lower, jnp.concatenate([p.astype(BF16) for p in _pieces(ls)], axis=1))
    cs = r[:, :nf] + r[:, nf:2 * nf] + r[:, 2 * nf:] + carry[...]
    carry[...] = cs[t - 1:t, :]
    cl = cs * LOG2E
    p1, p2, p3 = (p.astype(BF16) for p in _pieces(cl))
    kf = _dot(p1, pk_ref[0]) + _dot(p2, pk_ref[1]) + _dot(p3, pk_ref[2]) + kc_ref[...]
    for j in range(FOX_HEADS // 2):
        kf_ref[0, j] = kf[:, j * LANES:(j + 1) * LANES].astype(BF16)
    qf = (_dot_nt(pq_ref[0], p1) + _dot_nt(pq_ref[1], p2) + _dot_nt(pq_ref[2], p3)
          + qc_ref[...])
    qf_ref[0] = qf.astype(BF16).reshape(FOX_HEADS // 2, 2, QF_ROWS, t)


def _fgate_tables():
    npair = FOX_HEADS // 2
    pk = np.zeros((3, FOX_HEADS, npair * LANES), np.float32)
    kc = np.zeros((1, npair * LANES), np.float32)
    pq = np.zeros((3, FOX_HEADS * QF_ROWS, FOX_HEADS), np.float32)
    qc = np.zeros((FOX_HEADS * QF_ROWS, 1), np.float32)
    for h in range(FOX_HEADS):
        j, b = divmod(h, 2)
        for i in range(3):
            pk[i, h, j * LANES + 6 * b + i] = 1.0
            pq[i, h * QF_ROWS + 3 + i, h] = 1.0
            qc[h * QF_ROWS + 6 * b + i, 0] = -1.0
    for j in range(npair):
        kc[0, j * LANES + 3:j * LANES + 6] = 1.0
    return (jnp.asarray(pk, BF16), jnp.asarray(kc, F32), jnp.asarray(pq, BF16), jnp.asarray(qc, F32))


def _fgate(x, g, wf, b_f, *, tile):
    B, S, D = x.shape
    npair = FOX_HEADS // 2
    tables = _fgate_tables()
    return pl.pallas_call(
        _fgate_kernel,
        grid=(B, S // tile),
        in_specs=[
            pl.BlockSpec((1, tile, D), lambda b, s: (b, s, 0)),
            pl.BlockSpec((1, D), lambda b, s: (0, 0)),
            pl.BlockSpec((D, FOX_HEADS), lambda b, s: (0, 0)),
            pl.BlockSpec((1, FOX_HEADS), lambda b, s: (0, 0)),
        ] + [pl.BlockSpec(t.shape, lambda b, s, n=t.ndim: (0,) * n) for t in tables],
        out_specs=[
            pl.BlockSpec((1, npair, tile, LANES), lambda b, s: (b, 0, s, 0)),
            pl.BlockSpec((1, npair, 2, QF_ROWS, tile), lambda b, s: (b, 0, 0, 0, s)),
        ],
        out_shape=[
            jax.ShapeDtypeStruct((B, npair, S, LANES), BF16),
            jax.ShapeDtypeStruct((B, npair, 2, QF_ROWS, S), BF16),
        ],
        scratch_shapes=[pltpu.VMEM((1, FOX_HEADS), F32)],
        compiler_params=_cp(("parallel", "arbitrary")),
        name="fgate",
    )(x, g.reshape(1, D), wf, b_f.reshape(1, FOX_HEADS), *tables)


def _proj_kernel(*refs, tn, ntile, split_cols, add_cols, n_t):
    it = iter(refs)
    x_ref, g_ref, w_ref = next(it), next(it), next(it)
    add_ref = next(it) if add_cols else None
    wt_refs = [next(it) for _ in range(n_t)]
    u_ref = next(it)
    e_refs = [next(it) for _ in split_cols]
    ut_refs = [next(it) for _ in range(n_t)]
    h_sc = next(it)
    j = pl.program_id(1)

    @pl.when(j == 0)
    def _():
        h = _rms(x_ref[...], g_ref[...]).astype(BF16)
        h_sc[...] = h
        for wt_ref, ut_ref in zip(wt_refs, ut_refs):
            ut_ref[0] = _dot_nt(wt_ref[...], h).astype(ut_ref.dtype)

    acc = _dot(h_sc[...], w_ref[...])
    u_ref[...] = acc.astype(u_ref.dtype)

    def extras(t):
        lo = t * tn
        adds = [c - lo for c in add_cols if lo <= c < lo + tn]
        splits = [(e, c - lo) for e, c in zip(e_refs, split_cols) if lo <= c < lo + tn]

        def body():
            for c in adds:
                cols = slice(c, c + add_ref.shape[1])
                u_ref[:, cols] = (acc[:, cols] + add_ref[...].astype(F32)).astype(u_ref.dtype)
            for e_ref, c in splits:
                e_ref[...] = acc[:, c:c + e_ref.shape[1]].astype(e_ref.dtype)
        return body if adds or splits else None

    for t in range(ntile):
        body = extras(t)
        if body is not None:
            pl.when(j == t)(body)


def _proj(x2, g, w, *, seq, split_cols=(), split_width=0, addend=None, add_cols=(), w_t=(),
          t_dtypes=(), tm=PROJ_TM, tn=PROJ_TN):
    N, D = x2.shape
    W = w.shape[1]
    nbs = seq // tm
    in_specs = [
        pl.BlockSpec((tm, D), lambda i, j: (i, 0)),
        pl.BlockSpec((1, D), lambda i, j: (0, 0)),
        pl.BlockSpec((D, tn), lambda i, j: (0, j)),
    ]
    args = [x2, g.reshape(1, D), w]
    if add_cols:
        in_specs.append(pl.BlockSpec((tm, addend.shape[1]), lambda i, j: (i % nbs, 0)))
        args.append(addend)
    out_shape = [jax.ShapeDtypeStruct((N, W), BF16)]
    out_specs = [pl.BlockSpec((tm, tn), lambda i, j: (i, j))]
    for _ in split_cols:
        out_shape.append(jax.ShapeDtypeStruct((N, split_width), BF16))
        out_specs.append(pl.BlockSpec((tm, split_width), lambda i, j: (i, 0)))
    for wt, dt in zip(w_t, t_dtypes):
        rows = wt.shape[0]
        in_specs.append(pl.BlockSpec((rows, D), lambda i, j: (0, 0)))
        args.append(wt)
        out_shape.append(jax.ShapeDtypeStruct((N // seq, rows, seq), dt))
        out_specs.append(pl.BlockSpec((1, rows, tm), lambda i, j: (i // nbs, 0, i % nbs)))
    return pl.pallas_call(
        functools.partial(_proj_kernel, tn=tn, ntile=W // tn, split_cols=tuple(split_cols),
                          add_cols=tuple(add_cols), n_t=len(w_t)),
        grid=(N // tm, W // tn),
        in_specs=in_specs,
        out_specs=out_specs,
        out_shape=out_shape,
        scratch_shapes=[pltpu.VMEM((tm, D), BF16)],
        compiler_params=_cp(("parallel", "arbitrary")),
        name="proj",
    )(*args)


def _flash_ops(bm_sc, s_bufs, mx_sc, acc_sc, nh, tq, tk, k_tile, v_rows):
    acc_sc[...] = jnp.zeros_like(acc_sc)
    ones = jnp.ones((ACC_ROWS - HEAD_DIM, tk), BF16)

    def qk_head(tile, slot, h):
        cols = slice(h * tq, (h + 1) * tq)
        s = _dot(k_tile(pl.multiple_of(tile * tk, tk)), bm_sc[:, cols])
        s_bufs[slot][:, cols] = s
        if mx_sc is not None:
            mx_sc[slot, :, cols] = jnp.max(s, axis=0, keepdims=True)

    def soft_head(tile, slot, m_old, h, valid, bias):
        cols = slice(h * tq, (h + 1) * tq)
        buf = s_bufs[slot]
        k0 = pl.multiple_of(tile * tk, tk)
        if valid is not None or bias is not None:
            s = buf[:, cols] + bias if valid is None else jnp.where(valid, buf[:, cols], NEG)
            m_new = jnp.maximum(m_old, jnp.max(s, axis=0, keepdims=True))
            p = jnp.exp2(s - m_new).astype(BF16)
        else:
            m_new = jnp.maximum(m_old, mx_sc[slot, :, cols])
            p = jnp.exp2(buf[:, cols] - m_new).astype(BF16)
        alpha = jnp.exp2(m_old - m_new)
        lhs = jnp.concatenate([v_rows(h, k0), ones], axis=0)
        acc_sc[h] = alpha * acc_sc[h] + _dot(lhs, p)
        return m_new

    def qk_all(tile, slot):
        for h in range(nh):
            qk_head(tile, slot, h)

    def step(cur, slot, ms, nxt=None, valid=None, bias=None, before=None):
        out = []
        for h in range(nh):
            if nxt is not None:
                qk_head(nxt, 1 - slot, h)
            if before is not None:
                before(h)
            out.append(soft_head(cur, slot, ms[h], h, valid, bias))
        return tuple(out)

    def finish():
        outs = []
        for h in range(nh):
            a = acc_sc[h]
            outs.append(a[:HEAD_DIM] * (1.0 / a[HEAD_DIM:HEAD_DIM + 1]))
        return outs

    return qk_head, qk_all, step, finish


def _flash_t(ops, nh, tq, tk, n_un, tile_of, diag, valid_fn, first_done=False):
    _, qk_all, step, finish = ops
    key_iota = lax.broadcasted_iota(jnp.int32, (tk, tq), 0)
    diag_valid = lambda: valid_fn(pl.multiple_of(diag * tk, tk) + key_iota)

    def pair(i, ms):
        t_a, t_b, t_c = tile_of(2 * i), tile_of(2 * i + 1), tile_of(2 * i + 2)
        return step(t_b, 1, step(t_a, 0, ms, nxt=t_b), nxt=t_c)

    def odd_tail(ms):
        return step(diag, 1, step(tile_of(n_un - 1), 0, ms, nxt=diag), valid=diag_valid())

    def even_tail(ms):
        return step(diag, 0, ms, valid=diag_valid())

    if not first_done:
        qk_all(tile_of(0), 0)
    ms = (jnp.full((1, tq), NEG, F32),) * nh
    ms = lax.fori_loop(0, n_un // 2, pair, ms)
    lax.cond(n_un % 2 == 1, odd_tail, even_tail, ms)
    return finish()


def _flash_scratch(nh, tq, tk, kdim, col_max=True):
    bufs = [pltpu.VMEM((kdim, nh * tq), BF16), pltpu.VMEM((tk, nh * tq), F32),
            pltpu.VMEM((tk, nh * tq), F32)]
    if col_max:
        bufs.append(pltpu.VMEM((2, 1, nh * tq), F32))
    return bufs + [pltpu.VMEM((nh, ACC_ROWS, tq), F32)]


def _fox_kernel(qt_ref, qf_ref, k_ref, kf_ref, vt_ref, o_ref, bm_sc, s0_sc, s1_sc, mx_sc, acc_sc):
    qi = pl.program_id(2)
    tq, tk = FOX_TQ, FOX_TK
    bm_sc[...] = jnp.zeros_like(bm_sc)
    for h in range(2):
        rows = slice(h * HEAD_DIM, (h + 1) * HEAD_DIM)
        bm_sc[rows, h * tq:(h + 1) * tq] = qt_ref[0, rows, :]
        bm_sc[LANES:LANES + QF_ROWS, h * tq:(h + 1) * tq] = qf_ref[0, 0, h]
    qpos = qi * tq + lax.broadcasted_iota(jnp.int32, (tk, tq), 1)

    def k_tile(k0):
        return jnp.concatenate([k_ref[0, pl.ds(k0, tk), :], kf_ref[0, 0, pl.ds(k0, tk), :]], axis=1)

    def v_rows(h, k0):
        return vt_ref[0, h * HEAD_DIM:(h + 1) * HEAD_DIM, pl.ds(k0, tk)]

    diag = (qi * tq) // tk
    ops = _flash_ops(bm_sc, (s0_sc, s1_sc), mx_sc, acc_sc, 2, tq, tk, k_tile, v_rows)
    outs = _flash_t(ops, 2, tq, tk, diag, lambda i: i, diag, lambda key: key <= qpos)
    o_ref[0] = jnp.concatenate(outs, axis=0).T.astype(o_ref.dtype)


def _fox(u3, ut, kfeat, qfeat, *, q_row, k_col, v_row):
    B, S, _ = u3.shape
    npair = FOX_HEADS // 2
    return pl.pallas_call(
        _fox_kernel,
        grid=(B, npair, S // FOX_TQ),
        in_specs=[
            pl.BlockSpec((1, LANES, FOX_TQ), lambda b, j, i: (b, q_row + j, i)),
            pl.BlockSpec((1, 1, 2, QF_ROWS, FOX_TQ), lambda b, j, i: (b, j, 0, 0, i)),
            pl.BlockSpec((1, S, LANES), lambda b, j, i: (b, 0, k_col + j)),
            pl.BlockSpec((1, 1, S, LANES), lambda b, j, i: (b, j, 0, 0)),
            pl.BlockSpec((1, LANES, S), lambda b, j, i: (b, v_row + j, 0)),
        ],
        out_specs=pl.BlockSpec((1, FOX_TQ, LANES), lambda b, j, i: (b, i, j)),
        out_shape=jax.ShapeDtypeStruct((B, S, FOX_WIDTH), BF16),
        scratch_shapes=_flash_scratch(2, FOX_TQ, FOX_TK, 2 * LANES),
        compiler_params=_cp(("parallel", "parallel", "arbitrary")),
        name="fox",
    )(ut, qfeat, u3, kfeat, ut)


def _ret_kernel(q_ref, k_ref, v_ref, inner_ref, cross_ref, kdec_ref, cd_ref, bd_ref, gn_ref,
                o_ref, state_sc):
    @pl.when(pl.program_id(1) == 0)
    def _():
        state_sc[...] = jnp.zeros_like(state_sc)

    low = _low_half((RET_CHUNK, LANES), 1)
    inv = 1.0 / HEAD_DIM
    for c, j in [(c, j) for c in range(RET_STEP) for j in range(RET_HEADS // 2)]:
        rows = slice(c * RET_CHUNK, (c + 1) * RET_CHUNK)
        cols = slice(j * LANES, (j + 1) * LANES)
        q, k, v = q_ref[0, rows, cols], k_ref[0, rows, cols], v_ref[0, rows, cols]
        zero = jnp.zeros_like(q)
        qa, qb = jnp.where(low, q, zero), jnp.where(low, zero, q)
        pa = (_dot_nt(qa, k) * inner_ref[j, 0]).astype(BF16)
        pb = (_dot_nt(qb, k) * inner_ref[j, 1]).astype(BF16)
        o_in = jnp.where(low, _dot(pa, v), _dot(pb, v))
        state = state_sc[j]
        o = o_in + _dot(q, state.astype(BF16)) * cross_ref[j]
        kd = (k.astype(F32) * kdec_ref[j]).astype(BF16)
        state_sc[j] = state * cd_ref[j] + _dot_tn(kd, v) * bd_ref[...]
        sa = jnp.sum(jnp.where(low, o, 0.0), axis=-1, keepdims=True)
        st = jnp.sum(o, axis=-1, keepdims=True)
        mu = jnp.where(low, sa, st - sa) * inv
        d = o - mu
        d2 = d * d
        va = jnp.sum(jnp.where(low, d2, 0.0), axis=-1, keepdims=True)
        vt = jnp.sum(d2, axis=-1, keepdims=True)
        var = jnp.where(low, va, vt - va) * inv
        o_ref[0, rows, cols] = (d * lax.rsqrt(var + GN_EPS) * gn_ref[:, cols]).astype(o_ref.dtype)


def _ret_constants():
    lg = np.log(1.0 - 2.0 ** (-5.0 - np.arange(RET_HEADS)))
    i = np.arange(RET_CHUNK)
    diff = i[:, None] - i[None, :]
    inner = np.where(diff[None] >= 0, np.exp(lg[:, None, None] * np.maximum(diff, 0)[None]), 0.0)
    cross = np.exp(lg[:, None] * (i[None, :] + 1))
    kdec = np.exp(lg[:, None] * (RET_CHUNK - 1 - i)[None, :])
    cdec = np.exp(lg * RET_CHUNK)
    npair = RET_HEADS // 2
    inner = inner.reshape(npair, 2, RET_CHUNK, RET_CHUNK)

    def lanes(a):
        a = a.reshape(npair, 2, RET_CHUNK)
        return np.repeat(a.transpose(0, 2, 1), HEAD_DIM, axis=2)

    bd = np.kron(np.eye(2), np.ones((HEAD_DIM, HEAD_DIM)))
    cd = np.repeat(cdec.reshape(npair, 2), HEAD_DIM, axis=1)[:, :, None] * bd[None]
    f = lambda a: jnp.asarray(a, F32)
    return f(inner), f(lanes(cross)), f(lanes(kdec)), f(cd), f(bd)


def _retention(u3, gn_g, *, q_col, k_col, v_col):
    B, S, _ = u3.shape
    C = RET_CHUNK
    rows = RET_STEP * C
    npair = RET_HEADS // 2
    inner, cross, kdec, cd, bd = _ret_constants()
    full = lambda shape: pl.BlockSpec(shape, lambda b, i: (0,) * len(shape))
    return pl.pallas_call(
        _ret_kernel,
        grid=(B, S // rows),
        in_specs=[
            pl.BlockSpec((1, rows, RET_WIDTH), lambda b, i: (b, i, q_col)),
            pl.BlockSpec((1, rows, RET_WIDTH), lambda b, i: (b, i, k_col)),
            pl.BlockSpec((1, rows, RET_WIDTH), lambda b, i: (b, i, v_col)),
            full((npair, 2, C, C)), full((npair, C, LANES)), full((npair, C, LANES)),
            full((npair, LANES, LANES)), full((LANES, LANES)), full((1, RET_WIDTH)),
        ],
        out_specs=pl.BlockSpec((1, rows, RET_WIDTH), lambda b, i: (b, i, 0)),
        out_shape=jax.ShapeDtypeStruct((B, S, RET_WIDTH), BF16),
        scratch_shapes=[pltpu.VMEM((npair, LANES, LANES), F32)],
        compiler_params=_cp(("parallel", "arbitrary")),
        name="retention",
    )(u3, u3, u3, inner, cross, kdec, cd, bd, gn_g.reshape(1, RET_WIDTH))


def _out0_kernel(of_ref, or_ref, z_ref, x_ref, w_ref, o_ref):
    z = _silu(z_ref[...].astype(F32))
    ya = (of_ref[...].astype(F32) * z[:, :FOX_WIDTH]).astype(BF16)
    yb = (or_ref[...].astype(F32) * z[:, FOX_WIDTH:]).astype(BF16)
    o_ref[...] = x_ref[...] + _dot(ya, w_ref[:FOX_WIDTH, :]) + _dot(yb, w_ref[FOX_WIDTH:, :])


def _out0(o_f, o_r, u, x2, w_out, *, tm=OUT_TM):
    N, D = x2.shape
    return pl.pallas_call(
        _out0_kernel,
        grid=(N // tm,),
        in_specs=[
            pl.BlockSpec((tm, FOX_WIDTH), lambda i: (i, 0)),
            pl.BlockSpec((tm, RET_WIDTH), lambda i: (i, 0)),
            pl.BlockSpec((tm, D), lambda i: (i, 0)),
            pl.BlockSpec((tm, D), lambda i: (i, 0)),
            pl.BlockSpec((D, D), lambda i: (0, 0)),
        ],
        out_specs=pl.BlockSpec((tm, D), lambda i: (i, 0)),
        out_shape=jax.ShapeDtypeStruct((N, D), F32),
        compiler_params=_cp(("parallel",)),
        name="out0",
    )(o_f, o_r, u, x2, w_out)


def _out1_kernel(oc_ref, os_ref, z_ref, x_ref, w_ref, g_ref, o_ref):
    z = _silu(z_ref[...].astype(F32))
    y = ((oc_ref[...].astype(F32) + os_ref[...].astype(F32)) * z).astype(BF16)
    o_ref[...] = _rms(x_ref[...] + _dot(y, w_ref[...]), g_ref[...])


def _out1(o_c, o_s, u, x2, w_out, final_g, *, tm=OUT_TM):
    N, D = x2.shape
    row = pl.BlockSpec((tm, D), lambda i: (i, 0))
    return pl.pallas_call(
        _out1_kernel,
        grid=(N // tm,),
        in_specs=[row, row, row, row,
                  pl.BlockSpec((D, D), lambda i: (0, 0)),
                  pl.BlockSpec((1, D), lambda i: (0, 0))],
        out_specs=row,
        out_shape=jax.ShapeDtypeStruct((N, D), F32),
        compiler_params=_cp(("parallel",)),
        name="out1",
    )(o_c, o_s, u, x2, w_out, final_g.reshape(1, D))


def _compress_kernel(x_ref, pea_ref, peb_ref, wa_ref, wb_ref, w2_ref, *rest, transposed):
    x = x_ref[0].astype(F32)
    a = _dot((x + pea_ref[...]).astype(BF16), wa_ref[0])
    b = _dot((x + peb_ref[...]).astype(BF16), wb_ref[0])
    nseg = x.shape[0]
    pre = a + pltpu.roll(b, nseg - 1, 0)
    hid = _silu(pre).astype(BF16)
    if transposed:
        o_ref, = rest
        o_ref[0, 0] = _dot_nt(w2_ref[...], hid).astype(o_ref.dtype)
    else:
        feat_ref, o_ref = rest
        o_ref[0, 0] = (_dot(hid, w2_ref[...]) + feat_ref[...].astype(F32)).astype(o_ref.dtype)


def _compress(a3, pe, w1, w2, *, transposed):
    B, nseg, wid = a3.shape
    half = CMP_STRIDE * HEAD_DIM
    eye = jnp.eye(NSA_GROUPS, dtype=w1.dtype)

    def big(wh):
        w4 = wh.reshape(CMP_STRIDE, 1, HEAD_DIM, CMP_HIDDEN)
        sel = eye[:, None, :, None, None]
        return (sel * w4[None]).reshape(NSA_GROUPS, wid, CMP_HIDDEN).astype(BF16)

    def pe_big(p):
        return jnp.broadcast_to(p[:, None, :], (CMP_STRIDE, NSA_GROUPS, HEAD_DIM)).reshape(1, wid)

    args = [a3, pe_big(pe[:CMP_STRIDE]), pe_big(pe[CMP_STRIDE:]), big(w1[:half]), big(w1[half:])]
    in_specs = [
        pl.BlockSpec((1, nseg, wid), lambda b, g: (b, 0, 0)),
        pl.BlockSpec((1, wid), lambda b, g: (0, 0)),
        pl.BlockSpec((1, wid), lambda b, g: (0, 0)),
        pl.BlockSpec((1, wid, CMP_HIDDEN), lambda b, g: (g, 0, 0)),
        pl.BlockSpec((1, wid, CMP_HIDDEN), lambda b, g: (g, 0, 0)),
    ]
    if transposed:
        w2d = w2.T.astype(BF16)
        oshape, oblock = (B, NSA_GROUPS, HEAD_DIM, nseg), (1, 1, HEAD_DIM, nseg)
        args.append(w2d)
        in_specs.append(pl.BlockSpec(w2d.shape, lambda b, g: (0, 0)))
    else:
        w2d = jnp.pad(w2, ((0, 0), (0, LANES - HEAD_DIM))).astype(BF16)
        oshape, oblock = (B, NSA_GROUPS, nseg, LANES), (1, 1, nseg, LANES)
        feat = _pos_features(jnp.arange(nseg, dtype=jnp.int32) * CMP_STRIDE + (CMP_BLOCK - 1), LANES)
        args += [w2d, feat]
        in_specs += [pl.BlockSpec(w2d.shape, lambda b, g: (0, 0)),
                     pl.BlockSpec((nseg, LANES), lambda b, g: (0, 0))]
    return pl.pallas_call(
        functools.partial(_compress_kernel, transposed=transposed),
        grid=(B, NSA_GROUPS),
        in_specs=in_specs,
        out_specs=pl.BlockSpec(oblock, lambda b, g: (b, g, 0, 0)),
        out_shape=jax.ShapeDtypeStruct(oshape, BF16),
        compiler_params=_cp(("parallel", "parallel")),
        name="compress",
    )(*args)


def _slope_table():
    s = np.asarray(2.0 ** (-8.0 * (np.arange(NSA_HEADS) + 1) / NSA_HEADS), np.float32)
    sl = np.asarray(s.astype(np.float64) * LOG2E, np.float32)
    p1, p2, p3 = _np_pieces(sl)
    tab = np.zeros((NSA_HEADS, QF_ROWS), np.float32)
    for k, p in enumerate((p1, p1, p2, p2, p3, p3)):
        tab[:, k] = p
    tab[:, 6] = sl
    tab = np.broadcast_to(tab.reshape(NSA_GROUPS, NSA_HPG * QF_ROWS, 1),
                          (NSA_GROUPS, NSA_HPG * QF_ROWS, NSA_TQ))
    return jnp.asarray(tab)


def _pos_features(pos, width):
    pos = pos[:, None]
    lane = jnp.arange(width, dtype=jnp.int32)[None, :] % LANES
    hi = ((pos // SLC_BLOCK) * SLC_BLOCK).astype(F32)
    lo = (pos % SLC_BLOCK).astype(F32)
    k = lane - FEAT0
    f = jnp.where((k >= 0) & (k < 6), jnp.where(k % 2 == 0, hi, lo), 0.0)
    f = jnp.where((k >= 6) & (k < 9), 1.0, f)
    return f.astype(BF16)


def _nsa_queries(qt_ref, tab_ref, t0):
    tq = qt_ref.shape[2]
    r = lax.broadcasted_iota(jnp.int32, (QF_ROWS, tq), 0)
    t = (t0 + lax.broadcasted_iota(jnp.int32, (1, tq), 1)).astype(F32)
    zeros = jnp.zeros((LANES - HEAD_DIM - QF_ROWS, tq), BF16)
    out = []
    for i in range(NSA_HPG):
        tile = tab_ref[0, i * QF_ROWS:(i + 1) * QF_ROWS, :]
        a1, a2, a3 = _pieces(-(tile[6:7, :] * t))
        feat = jnp.where(r == 6, a1, jnp.where(r == 7, a2, jnp.where(r == 8, a3,
                                                                     jnp.where(r < 6, tile, 0.0))))
        out.append(jnp.concatenate([qt_ref[0, i * HEAD_DIM:(i + 1) * HEAD_DIM, :],
                                    feat.astype(BF16), zeros], axis=0))
    return out


def _gates_t(gt_ref, bg_ref, branch):
    gl = gt_ref[0] + bg_ref[...]
    return [_sigmoid(gl[N_BRANCH * i + branch:N_BRANCH * i + branch + 1, :]) for i in range(NSA_HPG)]


def _gated(outs_t, gates):
    return [o * gt for o, gt in zip(outs_t, gates)]


def _store_heads(o_ref, g):
    o_ref[0, :, :LANES] = jnp.concatenate(g[:2], axis=0).T.astype(o_ref.dtype)
    o_ref[0, :, LANES:] = jnp.concatenate(g[2:], axis=0).T.astype(o_ref.dtype)


def _nsa_specs(q_row):
    return dict(
        q=pl.BlockSpec((1, NSA_HPG * HEAD_DIM, NSA_TQ), lambda b, g, i: (b, q_row + g, i)),
        tab=pl.BlockSpec((1, NSA_HPG * QF_ROWS, NSA_TQ), lambda b, g, i: (g, 0, 0)),
        gt=pl.BlockSpec((1, GATE_ROWS, NSA_TQ), lambda b, g, i: (b, g, i)),
        bg=pl.BlockSpec((GATE_ROWS, 1), lambda b, g, i: (g, 0)),
        out=pl.BlockSpec((1, NSA_TQ, NSA_HPG * HEAD_DIM), lambda b, g, i: (b, i, g)),
    )


def _argmax_first(v, idx):
    n = v.shape[0]
    slabs = [(v[r:r + 8], idx[r:r + 8]) for r in range(0, n, 8)]
    while len(slabs) > 1:
        nxt = []
        for (va, ia), (vb, ib) in zip(slabs[0::2], slabs[1::2]):
            right = vb > va
            nxt.append((jnp.where(right, vb, va), jnp.where(right, ib, ia)))
        if len(slabs) % 2:
            nxt.append(slabs[-1])
        slabs = nxt
    v, idx = slabs[0]
    mx = jnp.max(v, axis=0, keepdims=True)
    first = jnp.min(jnp.where(v == mx, idx, float(n)), axis=0, keepdims=True)
    return mx, first


def _cmp_body(nc, t0, qh, kc_ref, vct_ref, mt_ref, grp_ref, gates, o_ref, sel_ref, flag_ref):
    tq = NSA_TQ
    rows = nc * CMP_CHUNK
    full = max(rows - CMP_CHUNK - 8, 0)
    nseg = kc_ref.shape[2]
    kc = kc_ref[0, 0, :rows, :]
    vct = vct_ref[0, 0, :, :rows]
    t = t0 + lax.broadcasted_iota(jnp.int32, (1, tq), 1)
    cidx = full + lax.broadcasted_iota(jnp.int32, (rows - full, 1), 0)
    valid = (cidx * CMP_STRIDE + (CMP_BLOCK - 1) <= t) & (cidx < nseg - 1)
    psum = jnp.zeros((rows, tq), F32)
    ps = []
    s_all = _dot(kc, jnp.concatenate(qh, axis=1))
    for i in range(NSA_HPG):
        s = s_all[:, i * tq:(i + 1) * tq]
        s_last = jnp.where(valid, s[full:], NEG)
        m = jnp.max(s_last, axis=0, keepdims=True)
        if nc > 1:
            m = jnp.maximum(m, jnp.max(s[:full], axis=0, keepdims=True))
        e = jnp.where(valid, jnp.exp2(s_last - m), 0.0)
        if nc > 1:
            e = jnp.concatenate([jnp.exp2(s[:full] - m), e], axis=0)
        l = jnp.sum(e, axis=0, keepdims=True)
        p = e * jnp.where(l > 0.0, 1.0 / l, 0.0)
        psum = psum + p
        ps.append(p.astype(BF16))
    o_all = _dot(vct, jnp.concatenate(ps, axis=1))
    _store_heads(o_ref, _gated([o_all[:, i * tq:(i + 1) * tq] for i in range(NSA_HPG)], gates))
    ns = rows * CMP_STRIDE // SLC_BLOCK
    mt = mt_ref[:ns, :rows]
    imp = sum(_dot(mt, p.astype(BF16)) for p in _pieces(psum))
    blk = lax.broadcasted_iota(jnp.int32, (ns, 1), 0)
    cur = t // SLC_BLOCK
    bvalid = blk * SLC_BLOCK <= t
    forced = (blk == 0) | (blk == cur) | (blk == cur - 1)
    score = jnp.where(forced, -jnp.inf, jnp.where(bvalid, imp, NEG))
    blk_f = jnp.broadcast_to(blk.astype(F32), (ns, tq))
    work = score
    for _ in range(SLC_TOPK - N_FORCED):
        mx, first = _argmax_first(work, blk_f)
        work = jnp.where(blk_f == first, -jnp.inf, work)
    picked = (score > mx) | ((score == mx) & (blk_f <= first))
    selneg = jnp.where(bvalid & (forced | picked), 0.0, -MASK_BIG)
    if ns < NS_PAD:
        selneg = jnp.concatenate([selneg, jnp.full((NS_PAD - ns, tq), -MASK_BIG, F32)], axis=0)
    sel_ref[0, 0] = selneg.astype(sel_ref.dtype)
    picked = jnp.where(selneg == 0.0, 1.0, 0.0).astype(BF16)
    used = _dot_nt(jnp.ones((8, tq), BF16), picked)
    used = jnp.where(used > 0.0, 1.0, 0.0).astype(BF16)
    flag_ref[0] = (_dot(used, grp_ref[...])[0:1] > 0.0).astype(jnp.int32)


def _cmp_kernel(q_ref, kc_ref, vct_ref, mt_ref, grp_ref, tab_ref, gt_ref, bg_ref,
                o_ref, sel_ref, flag_ref):
    t0 = pl.program_id(2) * NSA_TQ
    qh = _nsa_queries(q_ref, tab_ref, t0)
    gates = _gates_t(gt_ref, bg_ref, 0)
    nchunk = kc_ref.shape[2] // CMP_CHUNK
    last = t0 // (CMP_CHUNK * CMP_STRIDE)
    for nc in range(1, nchunk + 1):
        pl.when(last == nc - 1)(functools.partial(
            _cmp_body, nc, t0, qh, kc_ref, vct_ref, mt_ref, grp_ref, gates, o_ref, sel_ref, flag_ref))


def _cmp_to_slc_t(nseg, ns):
    c0 = np.arange(nseg)[:, None] * CMP_STRIDE
    s0 = np.arange(ns)[None, :] * SLC_BLOCK
    overlap = np.clip(np.minimum(c0 + CMP_BLOCK, s0 + SLC_BLOCK) - np.maximum(c0, s0), 0, None)
    m = overlap / CMP_STRIDE
    m[nseg - 1] = 0.0
    mt = np.zeros((NS_PAD, nseg))
    mt[:ns] = m.T
    return jnp.asarray(mt, BF16)


def _tile_groups():
    per = SLC_TK // SLC_BLOCK
    g = (np.arange(NS_PAD)[:, None] // per) == np.arange(NS_PAD)[None, :]
    return jnp.asarray(g, BF16)


def _cmp_attention(ut, kcmp, vcmp_t, gt, bg, *, q_row):
    B, _, S = ut.shape
    nseg = kcmp.shape[2]
    nq = S // NSA_TQ
    sp = _nsa_specs(q_row)
    return pl.pallas_call(
        _cmp_kernel,
        grid=(B, NSA_GROUPS, S // NSA_TQ),
        in_specs=[
            sp["q"],
            pl.BlockSpec((1, 1, nseg, LANES), lambda b, g, i: (b, g, 0, 0)),
            pl.BlockSpec((1, 1, HEAD_DIM, nseg), lambda b, g, i: (b, g, 0, 0)),
            pl.BlockSpec((NS_PAD, nseg), lambda b, g, i: (0, 0)),
            pl.BlockSpec((NS_PAD, NS_PAD), lambda b, g, i: (0, 0)),
            sp["tab"], sp["gt"], sp["bg"],
        ],
        out_specs=[sp["out"], pl.BlockSpec((1, 1, NS_PAD, NSA_TQ), lambda b, g, i: (b, g, 0, i)),
                   pl.BlockSpec((1, 1, NS_PAD), lambda b, g, i: ((b * NSA_GROUPS + g) * nq + i, 0, 0))],
        out_shape=[jax.ShapeDtypeStruct((B, S, NSA_WIDTH), BF16),
                   jax.ShapeDtypeStruct((B, NSA_GROUPS, NS_PAD, S), BF16),
                   jax.ShapeDtypeStruct((B * NSA_GROUPS * nq, 1, NS_PAD), jnp.int32)],
        compiler_params=_cp(("parallel", "parallel", "arbitrary")),
        name="cmp_attention",
    )(ut, kcmp, vcmp_t, _cmp_to_slc_t(nseg, S // SLC_BLOCK), _tile_groups(), _slope_table(), gt, bg)


def _window_branch(qh, t0, k_ref, vt_ref, bias_ref, bm_sc, s_bufs, acc_sc, last_before):
    tq, tk = NSA_TQ, WIN_TK
    bm_sc[...] = jnp.concatenate(qh, axis=1)
    _, qk_all, step, finish = _flash_ops(
        bm_sc, s_bufs, None, acc_sc, NSA_HPG, tq, tk,
        lambda k0: k_ref[0, pl.ds(k0, tk), :], lambda h, k0: vt_ref[0, :, pl.ds(k0, tk)])
    first = jnp.maximum(t0 - WINDOW, 0) // tk
    ntile = (WINDOW + tq) // tk
    qk_all(first, 0)
    ms = (jnp.full((1, tq), NEG, F32),) * NSA_HPG
    for j in range(ntile):
        last = j + 1 == ntile
        ms = step(first + j, j % 2, ms, nxt=None if last else first + j + 1,
                  bias=bias_ref[0, j * tk:(j + 1) * tk, :], before=last_before if last else None)
    return finish()


def _slc_win_kernel(fl_ref, q_ref, k_ref, vt_ref, sel_ref, e_ref, kw_ref, vwt_ref, bias_ref,
                    tab_ref, gt_ref, bg_ref, o_ref, bm_sc, s0_sc, s1_sc, mx_sc, acc_sc,
                    wbm_sc, ws0_sc, ws1_sc, wacc_sc, tiles_sm):
    qi = pl.program_id(2)
    tq, tk = NSA_TQ, SLC_TK
    t0 = qi * tq
    diag = t0 // tk
    row = (pl.program_id(0) * NSA_GROUPS + pl.program_id(1)) * pl.num_programs(2) + qi
    n_un = jnp.int32(0)
    for j in range(k_ref.shape[1] // tk):
        tiles_sm[n_un] = jnp.int32(j)
        n_un = n_un + ((fl_ref[row, j] > 0) & (j < diag)).astype(jnp.int32)
    tiles_sm[n_un] = diag
    qh = _nsa_queries(q_ref, tab_ref, t0)
    selneg = sel_ref[0, 0]
    bm_sc[...] = jnp.concatenate([jnp.concatenate([q, selneg], axis=0) for q in qh], axis=1)
    qpos = t0 + lax.broadcasted_iota(jnp.int32, (tk, tq), 1)

    def k_tile(k0):
        return jnp.concatenate([k_ref[0, pl.ds(k0, tk), :], e_ref[pl.ds(k0, tk), :]], axis=1)

    def v_rows(h, k0):
        return vt_ref[0, :, pl.ds(k0, tk)]

    ops = _flash_ops(bm_sc, (s0_sc, s1_sc), mx_sc, acc_sc, NSA_HPG, tq, tk, k_tile, v_rows)
    o_win = _gated(_window_branch(qh, t0, kw_ref, vwt_ref, bias_ref, wbm_sc, (ws0_sc, ws1_sc),
                                  wacc_sc, lambda h: ops[0](tiles_sm[0], 0, h)),
                   _gates_t(gt_ref, bg_ref, 2))
    outs = _flash_t(ops, NSA_HPG, tq, tk, n_un, lambda i: tiles_sm[i], diag,
                    lambda key: key <= qpos, first_done=True)
    o_slc = _gated(outs, _gates_t(gt_ref, bg_ref, 1))
    _store_heads(o_ref, [a + b for a, b in zip(o_slc, o_win)])


def _win_bias():
    tq, nk = NSA_TQ, WINDOW + NSA_TQ
    out = []
    for p in range(WINDOW // tq + 1):
        t0 = p * tq
        key = max(t0 - WINDOW, 0) + np.arange(nk)[:, None]
        qpos = t0 + np.arange(tq)[None, :]
        out.append(np.where((key <= qpos) & (key > qpos - WINDOW), 0.0, NEG))
    return jnp.asarray(np.stack(out), F32)


def _block_onehot(S):
    e = (np.arange(S)[:, None] // SLC_BLOCK) == np.arange(NS_PAD)[None, :]
    return jnp.asarray(e, BF16)


def _slc_win_attention(u3, ut, selneg, flags, gt, bg, *, q_row, ks_col, vs_row, kw_col, vw_row):
    B, S, _ = u3.shape
    sp = {k: pl.BlockSpec(v.block_shape, lambda b, g, i, fl, f=v.index_map: f(b, g, i))
          for k, v in _nsa_specs(q_row).items()}
    bias = _win_bias()
    npat = bias.shape[0]
    kspec = lambda col: pl.BlockSpec((1, S, LANES), lambda b, g, i, fl: (b, 0, col + g))
    vspec = lambda row: pl.BlockSpec((1, HEAD_DIM, S), lambda b, g, i, fl: (b, row + g, 0))
    grid_spec = pltpu.PrefetchScalarGridSpec(
        num_scalar_prefetch=1,
        grid=(B, NSA_GROUPS, S // NSA_TQ),
        in_specs=[
            sp["q"], kspec(ks_col), vspec(vs_row),
            pl.BlockSpec((1, 1, NS_PAD, NSA_TQ), lambda b, g, i, fl: (b, g, 0, i)),
            pl.BlockSpec((S, NS_PAD), lambda b, g, i, fl: (0, 0)),
            kspec(kw_col), vspec(vw_row),
            pl.BlockSpec((1,) + bias.shape[1:],
                         lambda b, g, i, fl: (jnp.minimum(i, npat - 1), 0, 0)),
            sp["tab"], sp["gt"], sp["bg"],
        ],
        out_specs=sp["out"],
        scratch_shapes=_flash_scratch(NSA_HPG, NSA_TQ, SLC_TK, 2 * LANES)
        + _flash_scratch(NSA_HPG, NSA_TQ, WIN_TK, LANES, col_max=False)
        + [pltpu.SMEM((S // SLC_TK + 1,), jnp.int32)],
    )
    return pl.pallas_call(
        _slc_win_kernel,
        grid_spec=grid_spec,
        out_shape=jax.ShapeDtypeStruct((B, S, NSA_WIDTH), BF16),
        compiler_params=_cp(("parallel", "parallel", "arbitrary")),
        name="slc_win_attention",
    )(flags, ut, u3, ut, selneg, _block_onehot(S), u3, ut, bias, _slope_table(), gt, bg)


def _aug_groups(w):
    d = w.shape[0]
    w = w.reshape(d, NSA_GROUPS, HEAD_DIM)
    return jnp.pad(w, ((0, 0), (0, 0), (0, LANES - HEAD_DIM))).reshape(d, NSA_GROUPS * LANES)


def _even_layer(x, norm_g, w_in, b_f, gn_g, w_out):
    B, S, D = x.shape
    qscale = HEAD_DIM ** -0.5 * LOG2E
    q_f, k_f, v_f, w_fl, q_r, k_r, v_r, z = jnp.split(
        w_in, np.cumsum([FOX_WIDTH] * 3 + [FOX_HEADS] + [RET_WIDTH] * 3).tolist(), axis=1)
    w = jnp.concatenate([z, k_f, q_r, k_r * HEAD_DIM ** -0.5, v_r], axis=1).astype(BF16)
    w_t = jnp.concatenate([q_f * qscale, v_f], axis=1).T.astype(BF16)
    x2 = x.reshape(B * S, D)
    u, ut = _proj(x2, norm_g, w, seq=S, w_t=[w_t], t_dtypes=[BF16], tn=2 * PROJ_TN)
    u3 = u.reshape(B, S, -1)
    kfeat, qfeat = _fgate(x, norm_g, w_fl, b_f, tile=min(512, S))
    o_f = _fox(u3, ut, kfeat, qfeat, q_row=0, k_col=D // LANES, v_row=FOX_WIDTH // LANES)
    rb = (D + FOX_WIDTH) // RET_WIDTH
    o_r = _retention(u3, gn_g, q_col=rb, k_col=rb + 1, v_col=rb + 2)
    out = _out0(o_f.reshape(B * S, -1), o_r.reshape(B * S, -1), u, x2, w_out.astype(BF16))
    return out.reshape(B, S, D)


def _odd_layer(x, norm_g, w_in, b_gate, pe_k, pe_v, wk1, wk2, wv1, wv2, w_out, final_g):
    B, S, D = x.shape
    assert S // SLC_BLOCK <= NS_PAD
    qscale = HEAD_DIM ** -0.5 * LOG2E
    sizes = [NSA_WIDTH] + [NSA_KV_WIDTH] * 6 + [NSA_HEADS * N_BRANCH]
    q, kc, vc, ks, vs, kw, vw, gl, z = jnp.split(w_in, np.cumsum(sizes).tolist(), axis=1)
    w = jnp.concatenate([z, kc, vc, _aug_groups(ks), _aug_groups(kw)], axis=1).astype(BF16)
    per_group = NSA_HPG * N_BRANCH
    glt = jnp.pad(gl.T.reshape(NSA_GROUPS, per_group, D), ((0, 0), (0, GATE_ROWS - per_group), (0, 0)))
    glt = glt.reshape(NSA_GROUPS * GATE_ROWS, D).astype(BF16)
    bg = jnp.pad(b_gate.reshape(NSA_GROUPS, per_group), ((0, 0), (0, GATE_ROWS - per_group)))
    bg = bg.reshape(NSA_GROUPS * GATE_ROWS, 1)
    w_vt = jnp.concatenate([q * qscale, vs, vw], axis=1).T.astype(BF16)
    x2 = x.reshape(B * S, D)
    kcol = D + 2 * NSA_KV_WIDTH
    kwid = NSA_GROUPS * LANES
    u, kc_a, vc_a, ut, gt = _proj(
        x2, norm_g, w, seq=S, split_cols=(D, D + NSA_KV_WIDTH), split_width=NSA_KV_WIDTH,
        addend=_pos_features(jnp.arange(S, dtype=jnp.int32), kwid), add_cols=(kcol, kcol + kwid),
        w_t=[w_vt, glt], t_dtypes=[BF16, F32], tn=w.shape[1] // 2)
    u3 = u.reshape(B, S, -1)
    nseg = S // CMP_STRIDE
    kcmp = _compress(kc_a.reshape(B, nseg, -1), pe_k, wk1, wk2, transposed=False)
    vcmp_t = _compress(vc_a.reshape(B, nseg, -1), pe_v, wv1, wv2, transposed=True)
    o_c, selneg, flags = _cmp_attention(ut, kcmp, vcmp_t, gt, bg, q_row=0)
    kb = (D + 2 * NSA_KV_WIDTH) // LANES
    vb = NSA_WIDTH // HEAD_DIM
    o_s = _slc_win_attention(u3, ut, selneg, flags[:, 0, :S // SLC_TK], gt, bg, q_row=0, ks_col=kb,
                             vs_row=vb, kw_col=kb + NSA_GROUPS, vw_row=vb + NSA_GROUPS)
    r = lambda a: a.reshape(B * S, -1)
    out = _out1(r(o_c), r(o_s), u, x2, w_out.astype(BF16), final_g)
    return out.reshape(B, S, D)


def kernel(x, even_norm_g, even_w_in, even_b_f, even_gn_g, even_w_out, odd_norm_g, odd_w_in,
           odd_b_gate, odd_pe_k, odd_pe_v, odd_wk1, odd_wk2, odd_wv1, odd_wv2, odd_w_out, final_g):
    x = _even_layer(x, even_norm_g[0], even_w_in[0], even_b_f[0], even_gn_g[0], even_w_out[0])
    return _odd_layer(x, odd_norm_g[0], odd_w_in[0], odd_b_gate[0], odd_pe_k[0], odd_pe_v[0],
                      odd_wk1[0], odd_wk2[0], odd_wv1[0], odd_wv2[0], odd_w_out[0], final_g)
```

```python
import functools
import math

import jax
import jax.numpy as jnp
import numpy as np
from jax import lax
from jax.experimental import pallas as pl
from jax.experimental.pallas import tpu as pltpu

D_MODEL = 1024
HEAD_DIM = 64
LANES = 128
FOX_HEADS = 8
RET_HEADS = 8
FOX_WIDTH = FOX_HEADS * HEAD_DIM
RET_WIDTH = RET_HEADS * HEAD_DIM
RET_CHUNK = 128
RET_STEP = 8
NSA_HEADS = 16
NSA_GROUPS = 4
NSA_HPG = NSA_HEADS // NSA_GROUPS
NSA_WIDTH = NSA_HEADS * HEAD_DIM
NSA_KV_WIDTH = NSA_GROUPS * HEAD_DIM
N_BRANCH = 3
GATE_ROWS = 16
CMP_BLOCK = 32
CMP_STRIDE = 16
CMP_HIDDEN = 256
CMP_CHUNK = 128
SLC_BLOCK = 64
SLC_TOPK = 16
N_FORCED = 3
NS_PAD = LANES
WINDOW = 512
RMS_EPS = 1e-6
GN_EPS = 1e-5
NEG = -1e30
FORCE_BONUS = 1e6
MASK_BIG = 2.0 ** 100
LOG2E = math.log2(math.e)
FEAT0 = HEAD_DIM
QF_ROWS = 16
ACC_ROWS = HEAD_DIM + 16

PROJ_TM = 1024
PROJ_TN = 512
FOX_TQ = 512
FOX_TK = 512
NSA_TQ = 512
SLC_TK = 512
WIN_TK = 256
OUT_TM = 1024
VMEM_LIMIT = 48 * 1024 * 1024

F32 = jnp.float32
BF16 = jnp.bfloat16


def _cp(sem, vmem=VMEM_LIMIT):
    return pltpu.CompilerParams(dimension_semantics=sem, vmem_limit_bytes=vmem)


def _dot(a, b):
    return jnp.dot(a, b, preferred_element_type=F32)


def _dot_nt(a, b):
    return lax.dot_general(a, b, (((1,), (1,)), ((), ())), preferred_element_type=F32)


def _dot_tn(a, b):
    return lax.dot_general(a, b, (((0,), (0,)), ((), ())), preferred_element_type=F32)


def _rms(x, g):
    return x * lax.rsqrt(jnp.mean(x * x, axis=-1, keepdims=True) + RMS_EPS) * g


def _silu(x):
    return x * (1.0 / (1.0 + jnp.exp(-x)))


def _sigmoid(x):
    return 1.0 / (1.0 + jnp.exp(-x))


def _low_half(shape, axis):
    return lax.broadcasted_iota(jnp.int32, shape, axis) < HEAD_DIM


def _pieces(v):
    p1 = v.astype(BF16).astype(F32)
    r = v - p1
    p2 = r.astype(BF16).astype(F32)
    p3 = (r - p2).astype(BF16).astype(F32)
    return p1, p2, p3


def _np_pieces(v):
    v = np.asarray(v, np.float64)
    bf = lambda a: np.asarray(a, np.float32).astype(BF16).astype(np.float64)
    p1 = bf(v)
    p2 = bf(v - p1)
    p3 = bf(v - p1 - p2)
    return p1, p2, p3


def _fgate_kernel(x_ref, g_ref, wf_ref, b_ref, pk_ref, kc_ref, pq_ref, qc_ref, kf_ref, qf_ref,
                  carry):
    @pl.when(pl.program_id(1) == 0)
    def _():
        carry[...] = jnp.zeros_like(carry)

    h = _rms(x_ref[0], g_ref[...])
    t = h.shape[0]
    h1 = h.astype(BF16)
    h2 = (h - h1.astype(F32)).astype(BF16)
    w = wf_ref[...]
    w1 = w.astype(BF16)
    w2 = (w - w1.astype(F32)).astype(BF16)
    nf = w.shape[1]
    r = _dot(jnp.concatenate([h1, h2], axis=0), jnp.concatenate([w1, w2], axis=1))
    f = r[:t, :nf] + r[:t, nf:] + r[t:, :nf] + b_ref[...]
    ls = jnp.minimum(f, 0.0) - jnp.log(1.0 + jnp.exp(-jnp.abs(f)))
    r = lax.broadcasted_iota(jnp.int32, (t, t), 0)
    c = lax.broadcasted_iota(jnp.int32, (t, t), 1)
    lower = jnp.where(c <= r, 1.0, 0.0).astype(BF16)
    r = _dot(lower, jnp.concatenate([p.astype(BF16) for p in _pieces(ls)], axis=1))
    cs = r[:, :nf] + r[:, nf:2 * nf] + r[:, 2 * nf:] + carry[...]
    carry[...] = cs[t - 1:t, :]
    cl = cs * LOG2E
    p1, p2, p3 = (p.astype(BF16) for p in _pieces(cl))
    kf = _dot(p1, pk_ref[0]) + _dot(p2, pk_ref[1]) + _dot(p3, pk_ref[2]) + kc_ref[...]
    for j in range(FOX_HEADS // 2):
        kf_ref[0, j] = kf[:, j * LANES:(j + 1) * LANES].astype(BF16)
    qf = (_dot_nt(pq_ref[0], p1) + _dot_nt(pq_ref[1], p2) + _dot_nt(pq_ref[2], p3)
          + qc_ref[...])
    qf_ref[0] = qf.astype(BF16).reshape(FOX_HEADS // 2, 2, QF_ROWS, t)


def _fgate_tables():
    npair = FOX_HEADS // 2
    pk = np.zeros((3, FOX_HEADS, npair * LANES), np.float32)
    kc = np.zeros((1, npair * LANES), np.float32)
    pq = np.zeros((3, FOX_HEADS * QF_ROWS, FOX_HEADS), np.float32)
    qc = np.zeros((FOX_HEADS * QF_ROWS, 1), np.float32)
    for h in range(FOX_HEADS):
        j, b = divmod(h, 2)
        for i in range(3):
            pk[i, h, j * LANES + 6 * b + i] = 1.0
            pq[i, h * QF_ROWS + 3 + i, h] = 1.0
            qc[h * QF_ROWS + 6 * b + i, 0] = -1.0
    for j in range(npair):
        kc[0, j * LANES + 3:j * LANES + 6] = 1.0
    return (jnp.asarray(pk, BF16), jnp.asarray(kc, F32), jnp.asarray(pq, BF16), jnp.asarray(qc, F32))


def _fgate(x, g, wf, b_f, *, tile):
    B, S, D = x.shape
    npair = FOX_HEADS // 2
    tables = _fgate_tables()
    return pl.pallas_call(
        _fgate_kernel,
        grid=(B, S // tile),
        in_specs=[
            pl.BlockSpec((1, tile, D), lambda b, s: (b, s, 0)),
            pl.BlockSpec((1, D), lambda b, s: (0, 0)),
            pl.BlockSpec((D, FOX_HEADS), lambda b, s: (0, 0)),
            pl.BlockSpec((1, FOX_HEADS), lambda b, s: (0, 0)),
        ] + [pl.BlockSpec(t.shape, lambda b, s, n=t.ndim: (0,) * n) for t in tables],
        out_specs=[
            pl.BlockSpec((1, npair, tile, LANES), lambda b, s: (b, 0, s, 0)),
            pl.BlockSpec((1, npair, 2, QF_ROWS, tile), lambda b, s: (b, 0, 0, 0, s)),
        ],
        out_shape=[
            jax.ShapeDtypeStruct((B, npair, S, LANES), BF16),
            jax.ShapeDtypeStruct((B, npair, 2, QF_ROWS, S), BF16),
        ],
        scratch_shapes=[pltpu.VMEM((1, FOX_HEADS), F32)],
        compiler_params=_cp(("parallel", "arbitrary")),
        name="fgate",
    )(x, g.reshape(1, D), wf, b_f.reshape(1, FOX_HEADS), *tables)


def _proj_kernel(*refs, tn, ntile, split_cols, add_cols, n_t):
    it = iter(refs)
    x_ref, g_ref, w_ref = next(it), next(it), next(it)
    add_ref = next(it) if add_cols else None
    wt_refs = [next(it) for _ in range(n_t)]
    u_ref = next(it)
    e_refs = [next(it) for _ in split_cols]
    ut_refs = [next(it) for _ in range(n_t)]
    h_sc = next(it)
    j = pl.program_id(1)

    @pl.when(j == 0)
    def _():
        h = _rms(x_ref[...], g_ref[...]).astype(BF16)
        h_sc[...] = h
        for wt_ref, ut_ref in zip(wt_refs, ut_refs):
            ut_ref[0] = _dot_nt(wt_ref[...], h).astype(ut_ref.dtype)

    acc = _dot(h_sc[...], w_ref[...])
    u_ref[...] = acc.astype(u_ref.dtype)

    def extras(t):
        lo = t * tn
        adds = [c - lo for c in add_cols if lo <= c < lo + tn]
        splits = [(e, c - lo) for e, c in zip(e_refs, split_cols) if lo <= c < lo + tn]

        def body():
            for c in adds:
                cols = slice(c, c + add_ref.shape[1])
                u_ref[:, cols] = (acc[:, cols] + add_ref[...].astype(F32)).astype(u_ref.dtype)
            for e_ref, c in splits:
                e_ref[...] = acc[:, c:c + e_ref.shape[1]].astype(e_ref.dtype)
        return body if adds or splits else None

    for t in range(ntile):
        body = extras(t)
        if body is not None:
            pl.when(j == t)(body)


def _proj(x2, g, w, *, seq, split_cols=(), split_width=0, addend=None, add_cols=(), w_t=(),
          t_dtypes=(), tm=PROJ_TM, tn=PROJ_TN):
    N, D = x2.shape
    W = w.shape[1]
    nbs = seq // tm
    in_specs = [
        pl.BlockSpec((tm, D), lambda i, j: (i, 0)),
        pl.BlockSpec((1, D), lambda i, j: (0, 0)),
        pl.BlockSpec((D, tn), lambda i, j: (0, j)),
    ]
    args = [x2, g.reshape(1, D), w]
    if add_cols:
        in_specs.append(pl.BlockSpec((tm, addend.shape[1]), lambda i, j: (i % nbs, 0)))
        args.append(addend)
    out_shape = [jax.ShapeDtypeStruct((N, W), BF16)]
    out_specs = [pl.BlockSpec((tm, tn), lambda i, j: (i, j))]
    for _ in split_cols:
        out_shape.append(jax.ShapeDtypeStruct((N, split_width), BF16))
        out_specs.append(pl.BlockSpec((tm, split_width), lambda i, j: (i, 0)))
    for wt, dt in zip(w_t, t_dtypes):
        rows = wt.shape[0]
        in_specs.append(pl.BlockSpec((rows, D), lambda i, j: (0, 0)))
        args.append(wt)
        out_shape.append(jax.ShapeDtypeStruct((N // seq, rows, seq), dt))
        out_specs.append(pl.BlockSpec((1, rows, tm), lambda i, j: (i // nbs, 0, i % nbs)))
    return pl.pallas_call(
        functools.partial(_proj_kernel, tn=tn, ntile=W // tn, split_cols=tuple(split_cols),
                          add_cols=tuple(add_cols), n_t=len(w_t)),
        grid=(N // tm, W // tn),
        in_specs=in_specs,
        out_specs=out_specs,
        out_shape=out_shape,
        scratch_shapes=[pltpu.VMEM((tm, D), BF16)],
        compiler_params=_cp(("parallel", "arbitrary")),
        name="proj",
    )(*args)


def _flash_ops(bm_sc, s_bufs, mx_sc, acc_sc, nh, tq, tk, k_tile, v_rows):
    acc_sc[...] = jnp.zeros_like(acc_sc)
    ones = jnp.ones((ACC_ROWS - HEAD_DIM, tk), BF16)

    def qk_head(tile, slot, h, qc=None):
        c0, c1 = qc or (0, tq)
        cols = slice(h * tq + c0, h * tq + c1)
        s = _dot(k_tile(pl.multiple_of(tile * tk, tk)), bm_sc[:, cols])
        s_bufs[slot][:, cols] = s
        if mx_sc is not None:
            mx_sc[slot, :, cols] = jnp.max(s, axis=0, keepdims=True)

    def soft_head(tile, slot, m_old, h, valid, bias, qc):
        c0, c1 = qc or (0, tq)
        cols = slice(h * tq + c0, h * tq + c1)
        buf = s_bufs[slot]
        k0 = pl.multiple_of(tile * tk, tk)
        m_sub = m_old[:, c0:c1]
        if valid is not None or bias is not None:
            s = (buf[:, cols] + bias[:, c0:c1] if valid is None
                 else jnp.where(valid[:, c0:c1], buf[:, cols], NEG))
            m_new = jnp.maximum(m_sub, jnp.max(s, axis=0, keepdims=True))
            p = jnp.exp2(s - m_new).astype(BF16)
        else:
            m_new = jnp.maximum(m_sub, mx_sc[slot, :, cols])
            p = jnp.exp2(buf[:, cols] - m_new).astype(BF16)
        alpha = jnp.exp2(m_sub - m_new)
        lhs = jnp.concatenate([v_rows(h, k0), ones], axis=0)
        acc_sc[h, :, c0:c1] = alpha * acc_sc[h, :, c0:c1] + _dot(lhs, p)
        parts = [m_new]
        if c0 > 0:
            parts.insert(0, m_old[:, :c0])
        if c1 < tq:
            parts.append(m_old[:, c1:])
        return jnp.concatenate(parts, axis=1) if len(parts) > 1 else m_new

    def qk_all(tile, slot, qc=None):
        for h in range(nh):
            qk_head(tile, slot, h, qc)

    def step(cur, slot, ms, nxt=None, valid=None, bias=None, before=None, qc=None, nxt_qc=None):
        out = []
        for h in range(nh):
            if nxt is not None:
                qk_head(nxt, 1 - slot, h, nxt_qc)
            if before is not None:
                before(h)
            out.append(soft_head(cur, slot, ms[h], h, valid, bias, qc))
        return tuple(out)

    def finish():
        outs = []
        for h in range(nh):
            a = acc_sc[h]
            outs.append(a[:HEAD_DIM] * (1.0 / a[HEAD_DIM:HEAD_DIM + 1]))
        return outs

    return qk_head, qk_all, step, finish


def _flash_t(ops, nh, tq, tk, n_un, tile_of, diag, valid_fn, first_done=False):
    _, qk_all, step, finish = ops
    key_iota = lax.broadcasted_iota(jnp.int32, (tk, tq), 0)
    diag_valid = lambda: valid_fn(pl.multiple_of(diag * tk, tk) + key_iota)

    def pair(i, ms):
        t_a, t_b, t_c = tile_of(2 * i), tile_of(2 * i + 1), tile_of(2 * i + 2)
        return step(t_b, 1, step(t_a, 0, ms, nxt=t_b), nxt=t_c)

    def odd_tail(ms):
        return step(diag, 1, step(tile_of(n_un - 1), 0, ms, nxt=diag), valid=diag_valid())

    def even_tail(ms):
        return step(diag, 0, ms, valid=diag_valid())

    if not first_done:
        qk_all(tile_of(0), 0)
    ms = (jnp.full((1, tq), NEG, F32),) * nh
    ms = lax.fori_loop(0, n_un // 2, pair, ms)
    lax.cond(n_un % 2 == 1, odd_tail, even_tail, ms)
    return finish()


def _flash_scratch(nh, tq, tk, kdim, col_max=True):
    bufs = [pltpu.VMEM((kdim, nh * tq), BF16), pltpu.VMEM((tk, nh * tq), F32),
            pltpu.VMEM((tk, nh * tq), F32)]
    if col_max:
        bufs.append(pltpu.VMEM((2, 1, nh * tq), F32))
    return bufs + [pltpu.VMEM((nh, ACC_ROWS, tq), F32)]


def _fox_kernel(qt_ref, qf_ref, k_ref, kf_ref, vt_ref, o_ref, bm_sc, s0_sc, s1_sc, mx_sc, acc_sc):
    qi = pl.program_id(2)
    tq, tk = FOX_TQ, FOX_TK
    bm_sc[...] = jnp.zeros_like(bm_sc)
    for h in range(2):
        rows = slice(h * HEAD_DIM, (h + 1) * HEAD_DIM)
        bm_sc[rows, h * tq:(h + 1) * tq] = qt_ref[0, rows, :]
        bm_sc[LANES:LANES + QF_ROWS, h * tq:(h + 1) * tq] = qf_ref[0, 0, h]
    qpos = qi * tq + lax.broadcasted_iota(jnp.int32, (tk, tq), 1)

    def k_tile(k0):
        return jnp.concatenate([k_ref[0, pl.ds(k0, tk), :], kf_ref[0, 0, pl.ds(k0, tk), :]], axis=1)

    def v_rows(h, k0):
        return vt_ref[0, h * HEAD_DIM:(h + 1) * HEAD_DIM, pl.ds(k0, tk)]

    diag = (qi * tq) // tk
    ops = _flash_ops(bm_sc, (s0_sc, s1_sc), mx_sc, acc_sc, 2, tq, tk, k_tile, v_rows)
    outs = _flash_t(ops, 2, tq, tk, diag, lambda i: i, diag, lambda key: key <= qpos)
    o_ref[0] = jnp.concatenate(outs, axis=0).T.astype(o_ref.dtype)


def _fox(u3, ut, kfeat, qfeat, *, q_row, k_col, v_row):
    B, S, _ = u3.shape
    npair = FOX_HEADS // 2
    return pl.pallas_call(
        _fox_kernel,
        grid=(B, npair, S // FOX_TQ),
        in_specs=[
            pl.BlockSpec((1, LANES, FOX_TQ), lambda b, j, i: (b, q_row + j, i)),
            pl.BlockSpec((1, 1, 2, QF_ROWS, FOX_TQ), lambda b, j, i: (b, j, 0, 0, i)),
            pl.BlockSpec((1, S, LANES), lambda b, j, i: (b, 0, k_col + j)),
            pl.BlockSpec((1, 1, S, LANES), lambda b, j, i: (b, j, 0, 0)),
            pl.BlockSpec((1, LANES, S), lambda b, j, i: (b, v_row + j, 0)),
        ],
        out_specs=pl.BlockSpec((1, FOX_TQ, LANES), lambda b, j, i: (b, i, j)),
        out_shape=jax.ShapeDtypeStruct((B, S, FOX_WIDTH), BF16),
        scratch_shapes=_flash_scratch(2, FOX_TQ, FOX_TK, 2 * LANES),
        compiler_params=_cp(("parallel", "parallel", "arbitrary")),
        name="fox",
    )(ut, qfeat, u3, kfeat, ut)


def _ret_kernel(q_ref, k_ref, v_ref, inner_ref, cross_ref, kdec_ref, cd_ref, bd_ref, gn_ref,
                o_ref, state_sc):
    @pl.when(pl.program_id(1) == 0)
    def _():
        state_sc[...] = jnp.zeros_like(state_sc)

    low = _low_half((RET_CHUNK, LANES), 1)
    inv = 1.0 / HEAD_DIM
    for c, j in [(c, j) for c in range(RET_STEP) for j in range(RET_HEADS // 2)]:
        rows = slice(c * RET_CHUNK, (c + 1) * RET_CHUNK)
        cols = slice(j * LANES, (j + 1) * LANES)
        q, k, v = q_ref[0, rows, cols], k_ref[0, rows, cols], v_ref[0, rows, cols]
        zero = jnp.zeros_like(q)
        qa, qb = jnp.where(low, q, zero), jnp.where(low, zero, q)
        pa = (_dot_nt(qa, k) * inner_ref[j, 0]).astype(BF16)
        pb = (_dot_nt(qb, k) * inner_ref[j, 1]).astype(BF16)
        o_in = jnp.where(low, _dot(pa, v), _dot(pb, v))
        state = state_sc[j]
        o = o_in + _dot(q, state.astype(BF16)) * cross_ref[j]
        kd = (k.astype(F32) * kdec_ref[j]).astype(BF16)
        state_sc[j] = state * cd_ref[j] + _dot_tn(kd, v) * bd_ref[...]
        sa = jnp.sum(jnp.where(low, o, 0.0), axis=-1, keepdims=True)
        st = jnp.sum(o, axis=-1, keepdims=True)
        mu = jnp.where(low, sa, st - sa) * inv
        d = o - mu
        d2 = d * d
        va = jnp.sum(jnp.where(low, d2, 0.0), axis=-1, keepdims=True)
        vt = jnp.sum(d2, axis=-1, keepdims=True)
        var = jnp.where(low, va, vt - va) * inv
        o_ref[0, rows, cols] = (d * lax.rsqrt(var + GN_EPS) * gn_ref[:, cols]).astype(o_ref.dtype)


def _ret_constants():
    lg = np.log(1.0 - 2.0 ** (-5.0 - np.arange(RET_HEADS)))
    i = np.arange(RET_CHUNK)
    diff = i[:, None] - i[None, :]
    inner = np.where(diff[None] >= 0, np.exp(lg[:, None, None] * np.maximum(diff, 0)[None]), 0.0)
    cross = np.exp(lg[:, None] * (i[None, :] + 1))
    kdec = np.exp(lg[:, None] * (RET_CHUNK - 1 - i)[None, :])
    cdec = np.exp(lg * RET_CHUNK)
    npair = RET_HEADS // 2
    inner = inner.reshape(npair, 2, RET_CHUNK, RET_CHUNK)

    def lanes(a):
        a = a.reshape(npair, 2, RET_CHUNK)
        return np.repeat(a.transpose(0, 2, 1), HEAD_DIM, axis=2)

    bd = np.kron(np.eye(2), np.ones((HEAD_DIM, HEAD_DIM)))
    cd = np.repeat(cdec.reshape(npair, 2), HEAD_DIM, axis=1)[:, :, None] * bd[None]
    f = lambda a: jnp.asarray(a, F32)
    return f(inner), f(lanes(cross)), f(lanes(kdec)), f(cd), f(bd)


def _retention(u3, gn_g, *, q_col, k_col, v_col):
    B, S, _ = u3.shape
    C = RET_CHUNK
    rows = RET_STEP * C
    npair = RET_HEADS // 2
    inner, cross, kdec, cd, bd = _ret_constants()
    full = lambda shape: pl.BlockSpec(shape, lambda b, i: (0,) * len(shape))
    return pl.pallas_call(
        _ret_kernel,
        grid=(B, S // rows),
        in_specs=[
            pl.BlockSpec((1, rows, RET_WIDTH), lambda b, i: (b, i, q_col)),
            pl.BlockSpec((1, rows, RET_WIDTH), lambda b, i: (b, i, k_col)),
            pl.BlockSpec((1, rows, RET_WIDTH), lambda b, i: (b, i, v_col)),
            full((npair, 2, C, C)), full((npair, C, LANES)), full((npair, C, LANES)),
            full((npair, LANES, LANES)), full((LANES, LANES)), full((1, RET_WIDTH)),
        ],
        out_specs=pl.BlockSpec((1, rows, RET_WIDTH), lambda b, i: (b, i, 0)),
        out_shape=jax.ShapeDtypeStruct((B, S, RET_WIDTH), BF16),
        scratch_shapes=[pltpu.VMEM((npair, LANES, LANES), F32)],
        compiler_params=_cp(("parallel", "arbitrary")),
        name="retention",
    )(u3, u3, u3, inner, cross, kdec, cd, bd, gn_g.reshape(1, RET_WIDTH))


def _out0_kernel(of_ref, or_ref, z_ref, x_ref, w_ref, o_ref):
    z = _silu(z_ref[...].astype(F32))
    ya = (of_ref[...].astype(F32) * z[:, :FOX_WIDTH]).astype(BF16)
    yb = (or_ref[...].astype(F32) * z[:, FOX_WIDTH:]).astype(BF16)
    o_ref[...] = x_ref[...] + _dot(ya, w_ref[:FOX_WIDTH, :]) + _dot(yb, w_ref[FOX_WIDTH:, :])


def _out0(o_f, o_r, u, x2, w_out, *, tm=OUT_TM):
    N, D = x2.shape
    return pl.pallas_call(
        _out0_kernel,
        grid=(N // tm,),
        in_specs=[
            pl.BlockSpec((tm, FOX_WIDTH), lambda i: (i, 0)),
            pl.BlockSpec((tm, RET_WIDTH), lambda i: (i, 0)),
            pl.BlockSpec((tm, D), lambda i: (i, 0)),
            pl.BlockSpec((tm, D), lambda i: (i, 0)),
            pl.BlockSpec((D, D), lambda i: (0, 0)),
        ],
        out_specs=pl.BlockSpec((tm, D), lambda i: (i, 0)),
        out_shape=jax.ShapeDtypeStruct((N, D), F32),
        compiler_params=_cp(("parallel",)),
        name="out0",
    )(o_f, o_r, u, x2, w_out)


def _out1_kernel(oc_ref, os_ref, z_ref, x_ref, w_ref, g_ref, o_ref):
    z = _silu(z_ref[...].astype(F32))
    y = ((oc_ref[...].astype(F32) + os_ref[...].astype(F32)) * z).astype(BF16)
    o_ref[...] = _rms(x_ref[...] + _dot(y, w_ref[...]), g_ref[...])


def _out1(o_c, o_s, u, x2, w_out, final_g, *, tm=OUT_TM):
    N, D = x2.shape
    row = pl.BlockSpec((tm, D), lambda i: (i, 0))
    return pl.pallas_call(
        _out1_kernel,
        grid=(N // tm,),
        in_specs=[row, row, row, row,
                  pl.BlockSpec((D, D), lambda i: (0, 0)),
                  pl.BlockSpec((1, D), lambda i: (0, 0))],
        out_specs=row,
        out_shape=jax.ShapeDtypeStruct((N, D), F32),
        compiler_params=_cp(("parallel",)),
        name="out1",
    )(o_c, o_s, u, x2, w_out, final_g.reshape(1, D))


def _compress_kernel(x_ref, pea_ref, peb_ref, wa_ref, wb_ref, w2_ref, *rest, transposed):
    x = x_ref[0].astype(F32)
    a = _dot((x + pea_ref[...]).astype(BF16), wa_ref[0])
    b = _dot((x + peb_ref[...]).astype(BF16), wb_ref[0])
    nseg = x.shape[0]
    pre = a + pltpu.roll(b, nseg - 1, 0)
    hid = _silu(pre).astype(BF16)
    if transposed:
        o_ref, = rest
        o_ref[0, 0] = _dot_nt(w2_ref[...], hid).astype(o_ref.dtype)
    else:
        feat_ref, o_ref = rest
        o_ref[0, 0] = (_dot(hid, w2_ref[...]) + feat_ref[...].astype(F32)).astype(o_ref.dtype)


def _compress(a3, pe, w1, w2, *, transposed):
    B, nseg, wid = a3.shape
    half = CMP_STRIDE * HEAD_DIM
    eye = jnp.eye(NSA_GROUPS, dtype=w1.dtype)

    def big(wh):
        w4 = wh.reshape(CMP_STRIDE, 1, HEAD_DIM, CMP_HIDDEN)
        sel = eye[:, None, :, None, None]
        return (sel * w4[None]).reshape(NSA_GROUPS, wid, CMP_HIDDEN).astype(BF16)

    def pe_big(p):
        return jnp.broadcast_to(p[:, None, :], (CMP_STRIDE, NSA_GROUPS, HEAD_DIM)).reshape(1, wid)

    args = [a3, pe_big(pe[:CMP_STRIDE]), pe_big(pe[CMP_STRIDE:]), big(w1[:half]), big(w1[half:])]
    in_specs = [
        pl.BlockSpec((1, nseg, wid), lambda b, g: (b, 0, 0)),
        pl.BlockSpec((1, wid), lambda b, g: (0, 0)),
        pl.BlockSpec((1, wid), lambda b, g: (0, 0)),
        pl.BlockSpec((1, wid, CMP_HIDDEN), lambda b, g: (g, 0, 0)),
        pl.BlockSpec((1, wid, CMP_HIDDEN), lambda b, g: (g, 0, 0)),
    ]
    if transposed:
        w2d = w2.T.astype(BF16)
        oshape, oblock = (B, NSA_GROUPS, HEAD_DIM, nseg), (1, 1, HEAD_DIM, nseg)
        args.append(w2d)
        in_specs.append(pl.BlockSpec(w2d.shape, lambda b, g: (0, 0)))
    else:
        w2d = jnp.pad(w2, ((0, 0), (0, LANES - HEAD_DIM))).astype(BF16)
        oshape, oblock = (B, NSA_GROUPS, nseg, LANES), (1, 1, nseg, LANES)
        feat = _pos_features(jnp.arange(nseg, dtype=jnp.int32) * CMP_STRIDE + (CMP_BLOCK - 1), LANES)
        args += [w2d, feat]
        in_specs += [pl.BlockSpec(w2d.shape, lambda b, g: (0, 0)),
                     pl.BlockSpec((nseg, LANES), lambda b, g: (0, 0))]
    return pl.pallas_call(
        functools.partial(_compress_kernel, transposed=transposed),
        grid=(B, NSA_GROUPS),
        in_specs=in_specs,
        out_specs=pl.BlockSpec(oblock, lambda b, g: (b, g, 0, 0)),
        out_shape=jax.ShapeDtypeStruct(oshape, BF16),
        compiler_params=_cp(("parallel", "parallel")),
        name="compress",
    )(*args)


def _slope_table():
    s = np.asarray(2.0 ** (-8.0 * (np.arange(NSA_HEADS) + 1) / NSA_HEADS), np.float32)
    sl = np.asarray(s.astype(np.float64) * LOG2E, np.float32)
    p1, p2, p3 = _np_pieces(sl)
    tab = np.zeros((NSA_HEADS, QF_ROWS), np.float32)
    for k, p in enumerate((p1, p1, p2, p2, p3, p3)):
        tab[:, k] = p
    tab[:, 6] = sl
    tab = np.broadcast_to(tab.reshape(NSA_GROUPS, NSA_HPG * QF_ROWS, 1),
                          (NSA_GROUPS, NSA_HPG * QF_ROWS, NSA_TQ))
    return jnp.asarray(tab)


def _pos_features(pos, width):
    pos = pos[:, None]
    lane = jnp.arange(width, dtype=jnp.int32)[None, :] % LANES
    hi = ((pos // SLC_BLOCK) * SLC_BLOCK).astype(F32)
    lo = (pos % SLC_BLOCK).astype(F32)
    k = lane - FEAT0
    f = jnp.where((k >= 0) & (k < 6), jnp.where(k % 2 == 0, hi, lo), 0.0)
    f = jnp.where((k >= 6) & (k < 9), 1.0, f)
    return f.astype(BF16)


def _nsa_queries(qt_ref, tab_ref, t0):
    tq = qt_ref.shape[2]
    r = lax.broadcasted_iota(jnp.int32, (QF_ROWS, tq), 0)
    t = (t0 + lax.broadcasted_iota(jnp.int32, (1, tq), 1)).astype(F32)
    zeros = jnp.zeros((LANES - HEAD_DIM - QF_ROWS, tq), BF16)
    out = []
    for i in range(NSA_HPG):
        tile = tab_ref[0, i * QF_ROWS:(i + 1) * QF_ROWS, :]
        a1, a2, a3 = _pieces(-(tile[6:7, :] * t))
        feat = jnp.where(r == 6, a1, jnp.where(r == 7, a2, jnp.where(r == 8, a3,
                                                                     jnp.where(r < 6, tile, 0.0))))
        out.append(jnp.concatenate([qt_ref[0, i * HEAD_DIM:(i + 1) * HEAD_DIM, :],
                                    feat.astype(BF16), zeros], axis=0))
    return out


def _gates_t(gt_ref, bg_ref, branch):
    gl = gt_ref[0] + bg_ref[...]
    return [_sigmoid(gl[N_BRANCH * i + branch:N_BRANCH * i + branch + 1, :]) for i in range(NSA_HPG)]


def _gated(outs_t, gates):
    return [o * gt for o, gt in zip(outs_t, gates)]


def _store_heads(o_ref, g):
    o_ref[0, :, :LANES] = jnp.concatenate(g[:2], axis=0).T.astype(o_ref.dtype)
    o_ref[0, :, LANES:] = jnp.concatenate(g[2:], axis=0).T.astype(o_ref.dtype)


def _nsa_specs(q_row):
    return dict(
        q=pl.BlockSpec((1, NSA_HPG * HEAD_DIM, NSA_TQ), lambda b, g, i: (b, q_row + g, i)),
        tab=pl.BlockSpec((1, NSA_HPG * QF_ROWS, NSA_TQ), lambda b, g, i: (g, 0, 0)),
        gt=pl.BlockSpec((1, GATE_ROWS, NSA_TQ), lambda b, g, i: (b, g, i)),
        bg=pl.BlockSpec((GATE_ROWS, 1), lambda b, g, i: (g, 0)),
        out=pl.BlockSpec((1, NSA_TQ, NSA_HPG * HEAD_DIM), lambda b, g, i: (b, i, g)),
    )


def _argmax_first(v, idx):
    n = v.shape[0]
    slabs = [(v[r:r + 8], idx[r:r + 8]) for r in range(0, n, 8)]
    while len(slabs) > 1:
        nxt = []
        for (va, ia), (vb, ib) in zip(slabs[0::2], slabs[1::2]):
            right = vb > va
            nxt.append((jnp.where(right, vb, va), jnp.where(right, ib, ia)))
        if len(slabs) % 2:
            nxt.append(slabs[-1])
        slabs = nxt
    v, idx = slabs[0]
    mx = jnp.max(v, axis=0, keepdims=True)
    first = jnp.min(jnp.where(v == mx, idx, float(n)), axis=0, keepdims=True)
    return mx, first


def _cmp_body(nc, t0, qh, kc_ref, vct_ref, mt_ref, grp_ref, gates, o_ref, sel_ref, flag_ref):
    tq = NSA_TQ
    rows = nc * CMP_CHUNK
    full = max(rows - CMP_CHUNK - 8, 0)
    nseg = kc_ref.shape[2]
    kc = kc_ref[0, 0, :rows, :]
    vct = vct_ref[0, 0, :, :rows]
    t = t0 + lax.broadcasted_iota(jnp.int32, (1, tq), 1)
    cidx = full + lax.broadcasted_iota(jnp.int32, (rows - full, 1), 0)
    valid = (cidx * CMP_STRIDE + (CMP_BLOCK - 1) <= t) & (cidx < nseg - 1)
    psum = jnp.zeros((rows, tq), F32)
    ps = []
    s_all = _dot(kc, jnp.concatenate(qh, axis=1))
    for i in range(NSA_HPG):
        s = s_all[:, i * tq:(i + 1) * tq]
        s_last = jnp.where(valid, s[full:], NEG)
        m = jnp.max(s_last, axis=0, keepdims=True)
        if nc > 1:
            m = jnp.maximum(m, jnp.max(s[:full], axis=0, keepdims=True))
        e = jnp.where(valid, jnp.exp2(s_last - m), 0.0)
        if nc > 1:
            e = jnp.concatenate([jnp.exp2(s[:full] - m), e], axis=0)
        l = jnp.sum(e, axis=0, keepdims=True)
        p = e * jnp.where(l > 0.0, 1.0 / l, 0.0)
        psum = psum + p
        ps.append(p.astype(BF16))
    o_all = _dot(vct, jnp.concatenate(ps, axis=1))
    _store_heads(o_ref, _gated([o_all[:, i * tq:(i + 1) * tq] for i in range(NSA_HPG)], gates))
    ns = rows * CMP_STRIDE // SLC_BLOCK
    mt = mt_ref[:ns, :rows]
    imp = sum(_dot(mt, p.astype(BF16)) for p in _pieces(psum))
    blk = lax.broadcasted_iota(jnp.int32, (ns, 1), 0)
    cur = t // SLC_BLOCK
    bvalid = blk * SLC_BLOCK <= t
    forced = (blk == 0) | (blk == cur) | (blk == cur - 1)
    score = jnp.where(forced, -jnp.inf, jnp.where(bvalid, imp, NEG))
    blk_f = jnp.broadcast_to(blk.astype(F32), (ns, tq))
    work = score
    for _ in range(SLC_TOPK - N_FORCED):
        mx, first = _argmax_first(work, blk_f)
        work = jnp.where(blk_f == first, -jnp.inf, work)
    picked = (score > mx) | ((score == mx) & (blk_f <= first))
    selneg = jnp.where(bvalid & (forced | picked), 0.0, -MASK_BIG)
    if ns < NS_PAD:
        selneg = jnp.concatenate([selneg, jnp.full((NS_PAD - ns, tq), -MASK_BIG, F32)], axis=0)
    sel_ref[0, 0] = selneg.astype(sel_ref.dtype)
    picked = jnp.where(selneg == 0.0, 1.0, 0.0).astype(BF16)
    used = _dot_nt(jnp.ones((8, tq), BF16), picked)
    used = jnp.where(used > 0.0, 1.0, 0.0).astype(BF16)
    flag_ref[0] = (_dot(used, grp_ref[...])[0:1] > 0.0).astype(jnp.int32)


def _cmp_kernel(q_ref, kc_ref, vct_ref, mt_ref, grp_ref, tab_ref, gt_ref, bg_ref,
                o_ref, sel_ref, flag_ref):
    t0 = pl.program_id(2) * NSA_TQ
    qh = _nsa_queries(q_ref, tab_ref, t0)
    gates = _gates_t(gt_ref, bg_ref, 0)
    nchunk = kc_ref.shape[2] // CMP_CHUNK
    last = t0 // (CMP_CHUNK * CMP_STRIDE)
    for nc in range(1, nchunk + 1):
        pl.when(last == nc - 1)(functools.partial(
            _cmp_body, nc, t0, qh, kc_ref, vct_ref, mt_ref, grp_ref, gates, o_ref, sel_ref, flag_ref))


def _cmp_to_slc_t(nseg, ns):
    c0 = np.arange(nseg)[:, None] * CMP_STRIDE
    s0 = np.arange(ns)[None, :] * SLC_BLOCK
    overlap = np.clip(np.minimum(c0 + CMP_BLOCK, s0 + SLC_BLOCK) - np.maximum(c0, s0), 0, None)
    m = overlap / CMP_STRIDE
    m[nseg - 1] = 0.0
    mt = np.zeros((NS_PAD, nseg))
    mt[:ns] = m.T
    return jnp.asarray(mt, BF16)


def _tile_groups():
    per = SLC_TK // SLC_BLOCK
    g = (np.arange(NS_PAD)[:, None] // per) == np.arange(NS_PAD)[None, :]
    return jnp.asarray(g, BF16)


def _cmp_attention(ut, kcmp, vcmp_t, gt, bg, *, q_row):
    B, _, S = ut.shape
    nseg = kcmp.shape[2]
    nq = S // NSA_TQ
    sp = _nsa_specs(q_row)
    return pl.pallas_call(
        _cmp_kernel,
        grid=(B, NSA_GROUPS, S // NSA_TQ),
        in_specs=[
            sp["q"],
            pl.BlockSpec((1, 1, nseg, LANES), lambda b, g, i: (b, g, 0, 0)),
            pl.BlockSpec((1, 1, HEAD_DIM, nseg), lambda b, g, i: (b, g, 0, 0)),
            pl.BlockSpec((NS_PAD, nseg), lambda b, g, i: (0, 0)),
            pl.BlockSpec((NS_PAD, NS_PAD), lambda b, g, i: (0, 0)),
            sp["tab"], sp["gt"], sp["bg"],
        ],
        out_specs=[sp["out"], pl.BlockSpec((1, 1, NS_PAD, NSA_TQ), lambda b, g, i: (b, g, 0, i)),
                   pl.BlockSpec((1, 1, NS_PAD), lambda b, g, i: ((b * NSA_GROUPS + g) * nq + i, 0, 0))],
        out_shape=[jax.ShapeDtypeStruct((B, S, NSA_WIDTH), BF16),
                   jax.ShapeDtypeStruct((B, NSA_GROUPS, NS_PAD, S), BF16),
                   jax.ShapeDtypeStruct((B * NSA_GROUPS * nq, 1, NS_PAD), jnp.int32)],
        compiler_params=_cp(("parallel", "parallel", "arbitrary")),
        name="cmp_attention",
    )(ut, kcmp, vcmp_t, _cmp_to_slc_t(nseg, S // SLC_BLOCK), _tile_groups(), _slope_table(), gt, bg)


def _window_branch(qh, t0, k_ref, vt_ref, bias_ref, bm_sc, s_bufs, acc_sc, last_before):
    tq, tk = NSA_TQ, WIN_TK
    bm_sc[...] = jnp.concatenate(qh, axis=1)
    _, qk_all, step, finish = _flash_ops(
        bm_sc, s_bufs, None, acc_sc, NSA_HPG, tq, tk,
        lambda k0: k_ref[0, pl.ds(k0, tk), :], lambda h, k0: vt_ref[0, :, pl.ds(k0, tk)])
    first = jnp.maximum(t0 - WINDOW, 0) // tk
    ntile = (WINDOW + tq) // tk

    def run(qcs):
        qk_all(first, 0, qcs[0])
        ms = (jnp.full((1, tq), NEG, F32),) * NSA_HPG
        for j in range(ntile):
            last = j + 1 == ntile
            ms = step(first + j, j % 2, ms, nxt=None if last else first + j + 1,
                      bias=bias_ref[0, j * tk:(j + 1) * tk, :],
                      before=last_before if last else None, qc=qcs[j],
                      nxt_qc=None if last else qcs[j + 1])

    lane = lambda v: max(0, min(tq, v)) // LANES * LANES
    qcs = []
    for j in range(ntile):
        c0, c1 = lane(j * tk - WINDOW + 1), tq - lane(tq - (j + 1) * tk)
        qcs.append(None if (c0, c1) == (0, tq) else (c0, c1))
    if all(q is None for q in qcs):
        run(qcs)
    else:
        lax.cond(t0 >= WINDOW, lambda: run(qcs), lambda: run([None] * ntile))
    return finish()


def _slc_win_kernel(fl_ref, q_ref, k_ref, vt_ref, sel_ref, e_ref, kw_ref, vwt_ref, bias_ref,
                    tab_ref, gt_ref, bg_ref, o_ref, bm_sc, s0_sc, s1_sc, mx_sc, acc_sc,
                    wbm_sc, ws0_sc, ws1_sc, wacc_sc, tiles_sm):
    qi = pl.program_id(2)
    tq, tk = NSA_TQ, SLC_TK
    t0 = qi * tq
    diag = t0 // tk
    row = (pl.program_id(0) * NSA_GROUPS + pl.program_id(1)) * pl.num_programs(2) + qi
    n_un = jnp.int32(0)
    for j in range(k_ref.shape[1] // tk):
        tiles_sm[n_un] = jnp.int32(j)
        n_un = n_un + ((fl_ref[row, j] > 0) & (j < diag)).astype(jnp.int32)
    tiles_sm[n_un] = diag
    qh = _nsa_queries(q_ref, tab_ref, t0)
    selneg = sel_ref[0, 0]
    bm_sc[...] = jnp.concatenate([jnp.concatenate([q, selneg], axis=0) for q in qh], axis=1)
    qpos = t0 + lax.broadcasted_iota(jnp.int32, (tk, tq), 1)

    def k_tile(k0):
        return jnp.concatenate([k_ref[0, pl.ds(k0, tk), :], e_ref[pl.ds(k0, tk), :]], axis=1)

    def v_rows(h, k0):
        return vt_ref[0, :, pl.ds(k0, tk)]

    ops = _flash_ops(bm_sc, (s0_sc, s1_sc), mx_sc, acc_sc, NSA_HPG, tq, tk, k_tile, v_rows)
    o_win = _gated(_window_branch(qh, t0, kw_ref, vwt_ref, bias_ref, wbm_sc, (ws0_sc, ws1_sc),
                                  wacc_sc, lambda h: ops[0](tiles_sm[0], 0, h)),
                   _gates_t(gt_ref, bg_ref, 2))
    outs = _flash_t(ops, NSA_HPG, tq, tk, n_un, lambda i: tiles_sm[i], diag,
                    lambda key: key <= qpos, first_done=True)
    o_slc = _gated(outs, _gates_t(gt_ref, bg_ref, 1))
    _store_heads(o_ref, [a + b for a, b in zip(o_slc, o_win)])


def _win_bias():
    tq, nk = NSA_TQ, WINDOW + NSA_TQ
    out = []
    for p in range(WINDOW // tq + 1):
        t0 = p * tq
        key = max(t0 - WINDOW, 0) + np.arange(nk)[:, None]
        qpos = t0 + np.arange(tq)[None, :]
        out.append(np.where((key <= qpos) & (key > qpos - WINDOW), 0.0, NEG))
    return jnp.asarray(np.stack(out), F32)


def _block_onehot(S):
    e = (np.arange(S)[:, None] // SLC_BLOCK) == np.arange(NS_PAD)[None, :]
    return jnp.asarray(e, BF16)


def _slc_win_attention(u3, ut, selneg, flags, gt, bg, *, q_row, ks_col, vs_row, kw_col, vw_row):
    B, S, _ = u3.shape
    sp = {k: pl.BlockSpec(v.block_shape, lambda b, g, i, fl, f=v.index_map: f(b, g, i))
          for k, v in _nsa_specs(q_row).items()}
    bias = _win_bias()
    npat = bias.shape[0]
    kspec = lambda col: pl.BlockSpec((1, S, LANES), lambda b, g, i, fl: (b, 0, col + g))
    vspec = lambda row: pl.BlockSpec((1, HEAD_DIM, S), lambda b, g, i, fl: (b, row + g, 0))
    grid_spec = pltpu.PrefetchScalarGridSpec(
        num_scalar_prefetch=1,
        grid=(B, NSA_GROUPS, S // NSA_TQ),
        in_specs=[
            sp["q"], kspec(ks_col), vspec(vs_row),
            pl.BlockSpec((1, 1, NS_PAD, NSA_TQ), lambda b, g, i, fl: (b, g, 0, i)),
            pl.BlockSpec((S, NS_PAD), lambda b, g, i, fl: (0, 0)),
            kspec(kw_col), vspec(vw_row),
            pl.BlockSpec((1,) + bias.shape[1:],
                         lambda b, g, i, fl: (jnp.minimum(i, npat - 1), 0, 0)),
            sp["tab"], sp["gt"], sp["bg"],
        ],
        out_specs=sp["out"],
        scratch_shapes=_flash_scratch(NSA_HPG, NSA_TQ, SLC_TK, 2 * LANES)
        + _flash_scratch(NSA_HPG, NSA_TQ, WIN_TK, LANES, col_max=False)
        + [pltpu.SMEM((S // SLC_TK + 1,), jnp.int32)],
    )
    return pl.pallas_call(
        _slc_win_kernel,
        grid_spec=grid_spec,
        out_shape=jax.ShapeDtypeStruct((B, S, NSA_WIDTH), BF16),
        compiler_params=_cp(("parallel", "parallel", "arbitrary")),
        name="slc_win_attention",
    )(flags, ut, u3, ut, selneg, _block_onehot(S), u3, ut, bias, _slope_table(), gt, bg)


def _aug_groups(w):
    d = w.shape[0]
    w = w.reshape(d, NSA_GROUPS, HEAD_DIM)
    return jnp.pad(w, ((0, 0), (0, 0), (0, LANES - HEAD_DIM))).reshape(d, NSA_GROUPS * LANES)


def _even_layer(x, norm_g, w_in, b_f, gn_g, w_out):
    B, S, D = x.shape
    qscale = HEAD_DIM ** -0.5 * LOG2E
    q_f, k_f, v_f, w_fl, q_r, k_r, v_r, z = jnp.split(
        w_in, np.cumsum([FOX_WIDTH] * 3 + [FOX_HEADS] + [RET_WIDTH] * 3).tolist(), axis=1)
    w = jnp.concatenate([z, k_f, q_r, k_r * HEAD_DIM ** -0.5, v_r], axis=1).astype(BF16)
    w_t = jnp.concatenate([q_f * qscale, v_f], axis=1).T.astype(BF16)
    x2 = x.reshape(B * S, D)
    u, ut = _proj(x2, norm_g, w, seq=S, w_t=[w_t], t_dtypes=[BF16], tn=2 * PROJ_TN)
    u3 = u.reshape(B, S, -1)
    kfeat, qfeat = _fgate(x, norm_g, w_fl, b_f, tile=min(512, S))
    o_f = _fox(u3, ut, kfeat, qfeat, q_row=0, k_col=D // LANES, v_row=FOX_WIDTH // LANES)
    rb = (D + FOX_WIDTH) // RET_WIDTH
    o_r = _retention(u3, gn_g, q_col=rb, k_col=rb + 1, v_col=rb + 2)
    out = _out0(o_f.reshape(B * S, -1), o_r.reshape(B * S, -1), u, x2, w_out.astype(BF16))
    return out.reshape(B, S, D)


def _odd_layer(x, norm_g, w_in, b_gate, pe_k, pe_v, wk1, wk2, wv1, wv2, w_out, final_g):
    B, S, D = x.shape
    assert S // SLC_BLOCK <= NS_PAD
    qscale = HEAD_DIM ** -0.5 * LOG2E
    sizes = [NSA_WIDTH] + [NSA_KV_WIDTH] * 6 + [NSA_HEADS * N_BRANCH]
    q, kc, vc, ks, vs, kw, vw, gl, z = jnp.split(w_in, np.cumsum(sizes).tolist(), axis=1)
    w = jnp.concatenate([z, kc, vc, _aug_groups(ks), _aug_groups(kw)], axis=1).astype(BF16)
    per_group = NSA_HPG * N_BRANCH
    glt = jnp.pad(gl.T.reshape(NSA_GROUPS, per_group, D), ((0, 0), (0, GATE_ROWS - per_group), (0, 0)))
    glt = glt.reshape(NSA_GROUPS * GATE_ROWS, D).astype(BF16)
    bg = jnp.pad(b_gate.reshape(NSA_GROUPS, per_group), ((0, 0), (0, GATE_ROWS - per_group)))
    bg = bg.reshape(NSA_GROUPS * GATE_ROWS, 1)
    w_vt = jnp.concatenate([q * qscale, vs, vw], axis=1).T.astype(BF16)
    x2 = x.reshape(B * S, D)
    kcol = D + 2 * NSA_KV_WIDTH
    kwid = NSA_GROUPS * LANES
    u, kc_a, vc_a, ut, gt = _proj(
        x2, norm_g, w, seq=S, split_cols=(D, D + NSA_KV_WIDTH), split_width=NSA_KV_WIDTH,
        addend=_pos_features(jnp.arange(S, dtype=jnp.int32), kwid), add_cols=(kcol, kcol + kwid),
        w_t=[w_vt, glt], t_dtypes=[BF16, F32], tn=w.shape[1] // 2)
    u3 = u.reshape(B, S, -1)
    nseg = S // CMP_STRIDE
    kcmp = _compress(kc_a.reshape(B, nseg, -1), pe_k, wk1, wk2, transposed=False)
    vcmp_t = _compress(vc_a.reshape(B, nseg, -1), pe_v, wv1, wv2, transposed=True)
    o_c, selneg, flags = _cmp_attention(ut, kcmp, vcmp_t, gt, bg, q_row=0)
    kb = (D + 2 * NSA_KV_WIDTH) // LANES
    vb = NSA_WIDTH // HEAD_DIM
    o_s = _slc_win_attention(u3, ut, selneg, flags[:, 0, :S // SLC_TK], gt, bg, q_row=0, ks_col=kb,
                             vs_row=vb, kw_col=kb + NSA_GROUPS, vw_row=vb + NSA_GROUPS)
    r = lambda a: a.reshape(B * S, -1)
    out = _out1(r(o_c), r(o_s), u, x2, w_out.astype(BF16), final_g)
    return out.reshape(B, S, D)


def kernel(x, even_norm_g, even_w_in, even_b_f, even_gn_g, even_w_out, odd_norm_g, odd_w_in,
           odd_b_gate, odd_pe_k, odd_pe_v, odd_wk1, odd_wk2, odd_wv1, odd_wv2, odd_w_out, final_g):
    x = _even_layer(x, even_norm_g[0], even_w_in[0], even_b_f[0], even_gn_g[0], even_w_out[0])
    return _odd_layer(x, odd_norm_g[0], odd_w_in[0], odd_b_gate[0], odd_pe_k[0], odd_pe_v[0],
                      odd_wk1[0], odd_wk2[0], odd_wv1[0], odd_wv2[0], odd_w_out[0], final_g)
```

```python
import functools
import math

import jax
import jax.numpy as jnp
import numpy as np
from jax import lax
from jax.experimental import pallas as pl
from jax.experimental.pallas import tpu as pltpu

D_MODEL = 1024
HEAD_DIM = 64
LANES = 128
FOX_HEADS = 8
RET_HEADS = 8
FOX_WIDTH = FOX_HEADS * HEAD_DIM
RET_WIDTH = RET_HEADS * HEAD_DIM
RET_CHUNK = 128
RET_STEP = 8
NSA_HEADS = 16
NSA_GROUPS = 4
NSA_HPG = NSA_HEADS // NSA_GROUPS
NSA_WIDTH = NSA_HEADS * HEAD_DIM
NSA_KV_WIDTH = NSA_GROUPS * HEAD_DIM
N_BRANCH = 3
GATE_ROWS = 16
CMP_BLOCK = 32
CMP_STRIDE = 16
CMP_HIDDEN = 256
CMP_CHUNK = 128
SLC_BLOCK = 64
SLC_TOPK = 16
N_FORCED = 3
NS_PAD = LANES
WINDOW = 512
RMS_EPS = 1e-6
GN_EPS = 1e-5
NEG = -1e30
FORCE_BONUS = 1e6
MASK_BIG = 2.0 ** 100
LOG2E = math.log2(math.e)
FEAT0 = HEAD_DIM
QF_ROWS = 16
ACC_ROWS = HEAD_DIM + 16

PROJ_TM = 1024
PROJ_TN = 512
FOX_TQ = 512
FOX_TK = 512
NSA_TQ = 512
SLC_TK = 512
WIN_TK = 256
OUT_TM = 1024
VMEM_LIMIT = 48 * 1024 * 1024

F32 = jnp.float32
BF16 = jnp.bfloat16


def _cp(sem, vmem=VMEM_LIMIT):
    return pltpu.CompilerParams(dimension_semantics=sem, vmem_limit_bytes=vmem)


def _dot(a, b):
    return jnp.dot(a, b, preferred_element_type=F32)


def _dot_nt(a, b):
    return lax.dot_general(a, b, (((1,), (1,)), ((), ())), preferred_element_type=F32)


def _dot_tn(a, b):
    return lax.dot_general(a, b, (((0,), (0,)), ((), ())), preferred_element_type=F32)


def _rms(x, g):
    return x * lax.rsqrt(jnp.mean(x * x, axis=-1, keepdims=True) + RMS_EPS) * g


def _silu(x):
    return x * (1.0 / (1.0 + jnp.exp(-x)))


def _sigmoid(x):
    return 1.0 / (1.0 + jnp.exp(-x))


def _low_half(shape, axis):
    return lax.broadcasted_iota(jnp.int32, shape, axis) < HEAD_DIM


def _pieces(v):
    p1 = v.astype(BF16).astype(F32)
    r = v - p1
    p2 = r.astype(BF16).astype(F32)
    p3 = (r - p2).astype(BF16).astype(F32)
    return p1, p2, p3


def _np_pieces(v):
    v = np.asarray(v, np.float64)
    bf = lambda a: np.asarray(a, np.float32).astype(BF16).astype(np.float64)
    p1 = bf(v)
    p2 = bf(v - p1)
    p3 = bf(v - p1 - p2)
    return p1, p2, p3


def _fgate_kernel(x_ref, g_ref, wf_ref, b_ref, pk_ref, kc_ref, pq_ref, qc_ref, kf_ref, qf_ref,
                  carry):
    @pl.when(pl.program_id(1) == 0)
    def _():
        carry[...] = jnp.zeros_like(carry)

    h = _rms(x_ref[0], g_ref[...])
    t = h.shape[0]
    h1 = h.astype(BF16)
    h2 = (h - h1.astype(F32)).astype(BF16)
    w = wf_ref[...]
    w1 = w.astype(BF16)
    w2 = (w - w1.astype(F32)).astype(BF16)
    nf = w.shape[1]
    r = _dot(jnp.concatenate([h1, h2], axis=0), jnp.concatenate([w1, w2], axis=1))
    f = r[:t, :nf] + r[:t, nf:] + r[t:, :nf] + b_ref[...]
    ls = jnp.minimum(f, 0.0) - jnp.log(1.0 + jnp.exp(-jnp.abs(f)))
    r = lax.broadcasted_iota(jnp.int32, (t, t), 0)
    c = lax.broadcasted_iota(jnp.int32, (t, t), 1)
    lower = jnp.where(c <= r, 1.0, 0.0).astype(BF16)
    r = _dot(lower, jnp.concatenate([p.astype(BF16) for p in _pieces(ls)], axis=1))
    cs = r[:, :nf] + r[:, nf:2 * nf] + r[:, 2 * nf:] + carry[...]
    carry[...] = cs[t - 1:t, :]
    cl = cs * LOG2E
    p1, p2, p3 = (p.astype(BF16) for p in _pieces(cl))
    kf = _dot(p1, pk_ref[0]) + _dot(p2, pk_ref[1]) + _dot(p3, pk_ref[2]) + kc_ref[...]
    for j in range(FOX_HEADS // 2):
        kf_ref[0, j] = kf[:, j * LANES:(j + 1) * LANES].astype(BF16)
    qf = (_dot_nt(pq_ref[0], p1) + _dot_nt(pq_ref[1], p2) + _dot_nt(pq_ref[2], p3)
          + qc_ref[...])
    qf_ref[0] = qf.astype(BF16).reshape(FOX_HEADS // 2, 2, QF_ROWS, t)


def _fgate_tables():
    npair = FOX_HEADS // 2
    pk = np.zeros((3, FOX_HEADS, npair * LANES), np.float32)
    kc = np.zeros((1, npair * LANES), np.float32)
    pq = np.zeros((3, FOX_HEADS * QF_ROWS, FOX_HEADS), np.float32)
    qc = np.zeros((FOX_HEADS * QF_ROWS, 1), np.float32)
    for h in range(FOX_HEADS):
        j, b = divmod(h, 2)
        for i in range(3):
            pk[i, h, j * LANES + 6 * b + i] = 1.0
            pq[i, h * QF_ROWS + 3 + i, h] = 1.0
            qc[h * QF_ROWS + 6 * b + i, 0] = -1.0
    for j in range(npair):
        kc[0, j * LANES + 3:j * LANES + 6] = 1.0
    return (jnp.asarray(pk, BF16), jnp.asarray(kc, F32), jnp.asarray(pq, BF16), jnp.asarray(qc, F32))


def _fgate(x, g, wf, b_f, *, tile):
    B, S, D = x.shape
    npair = FOX_HEADS // 2
    tables = _fgate_tables()
    return pl.pallas_call(
        _fgate_kernel,
        grid=(B, S // tile),
        in_specs=[
            pl.BlockSpec((1, tile, D), lambda b, s: (b, s, 0)),
            pl.BlockSpec((1, D), lambda b, s: (0, 0)),
            pl.BlockSpec((D, FOX_HEADS), lambda b, s: (0, 0)),
            pl.BlockSpec((1, FOX_HEADS), lambda b, s: (0, 0)),
        ] + [pl.BlockSpec(t.shape, lambda b, s, n=t.ndim: (0,) * n) for t in tables],
        out_specs=[
            pl.BlockSpec((1, npair, tile, LANES), lambda b, s: (b, 0, s, 0)),
            pl.BlockSpec((1, npair, 2, QF_ROWS, tile), lambda b, s: (b, 0, 0, 0, s)),
        ],
        out_shape=[
            jax.ShapeDtypeStruct((B, npair, S, LANES), BF16),
            jax.ShapeDtypeStruct((B, npair, 2, QF_ROWS, S), BF16),
        ],
        scratch_shapes=[pltpu.VMEM((1, FOX_HEADS), F32)],
        compiler_params=_cp(("parallel", "arbitrary")),
        name="fgate",
    )(x, g.reshape(1, D), wf, b_f.reshape(1, FOX_HEADS), *tables)


def _proj_kernel(*refs, tn, ntile, split_cols, add_cols, n_t):
    it = iter(refs)
    x_ref, g_ref, w_ref = next(it), next(it), next(it)
    add_ref = next(it) if add_cols else None
    wt_refs = [next(it) for _ in range(n_t)]
    u_ref = next(it)
    e_refs = [next(it) for _ in split_cols]
    ut_refs = [next(it) for _ in range(n_t)]
    h_sc = next(it)
    j = pl.program_id(1)

    @pl.when(j == 0)
    def _():
        h = _rms(x_ref[...], g_ref[...]).astype(BF16)
        h_sc[...] = h
        for wt_ref, ut_ref in zip(wt_refs, ut_refs):
            ut_ref[0] = _dot_nt(wt_ref[...], h).astype(ut_ref.dtype)

    acc = _dot(h_sc[...], w_ref[...])
    u_ref[...] = acc.astype(u_ref.dtype)

    def extras(t):
        lo = t * tn
        adds = [c - lo for c in add_cols if lo <= c < lo + tn]
        splits = [(e, c - lo) for e, c in zip(e_refs, split_cols) if lo <= c < lo + tn]

        def body():
            for c in adds:
                cols = slice(c, c + add_ref.shape[1])
                u_ref[:, cols] = (acc[:, cols] + add_ref[...].astype(F32)).astype(u_ref.dtype)
            for e_ref, c in splits:
                e_ref[...] = acc[:, c:c + e_ref.shape[1]].astype(e_ref.dtype)
        return body if adds or splits else None

    for t in range(ntile):
        body = extras(t)
        if body is not None:
            pl.when(j == t)(body)


def _proj(x2, g, w, *, seq, split_cols=(), split_width=0, addend=None, add_cols=(), w_t=(),
          t_dtypes=(), tm=PROJ_TM, tn=PROJ_TN):
    N, D = x2.shape
    W = w.shape[1]
    nbs = seq // tm
    in_specs = [
        pl.BlockSpec((tm, D), lambda i, j: (i, 0)),
        pl.BlockSpec((1, D), lambda i, j: (0, 0)),
        pl.BlockSpec((D, tn), lambda i, j: (0, j)),
    ]
    args = [x2, g.reshape(1, D), w]
    if add_cols:
        in_specs.append(pl.BlockSpec((tm, addend.shape[1]), lambda i, j: (i % nbs, 0)))
        args.append(addend)
    out_shape = [jax.ShapeDtypeStruct((N, W), BF16)]
    out_specs = [pl.BlockSpec((tm, tn), lambda i, j: (i, j))]
    for _ in split_cols:
        out_shape.append(jax.ShapeDtypeStruct((N, split_width), BF16))
        out_specs.append(pl.BlockSpec((tm, split_width), lambda i, j: (i, 0)))
    for wt, dt in zip(w_t, t_dtypes):
        rows = wt.shape[0]
        in_specs.append(pl.BlockSpec((rows, D), lambda i, j: (0, 0)))
        args.append(wt)
        out_shape.append(jax.ShapeDtypeStruct((N // seq, rows, seq), dt))
        out_specs.append(pl.BlockSpec((1, rows, tm), lambda i, j: (i // nbs, 0, i % nbs)))
    return pl.pallas_call(
        functools.partial(_proj_kernel, tn=tn, ntile=W // tn, split_cols=tuple(split_cols),
                          add_cols=tuple(add_cols), n_t=len(w_t)),
        grid=(N // tm, W // tn),
        in_specs=in_specs,
        out_specs=out_specs,
        out_shape=out_shape,
        scratch_shapes=[pltpu.VMEM((tm, D), BF16)],
        compiler_params=_cp(("parallel", "arbitrary")),
        name="proj",
    )(*args)


def _flash_ops(bm_sc, s_bufs, mx_sc, acc_sc, nh, tq, tk, k_tile, v_rows):
    acc_sc[...] = jnp.zeros_like(acc_sc)
    ones = jnp.ones((ACC_ROWS - HEAD_DIM, tk), BF16)

    def qk_head(tile, slot, h, qc=None):
        c0, c1 = qc or (0, tq)
        cols = slice(h * tq + c0, h * tq + c1)
        s = _dot(k_tile(pl.multiple_of(tile * tk, tk)), bm_sc[:, cols])
        s_bufs[slot][:, cols] = s
        if mx_sc is not None:
            mx_sc[slot, :, cols] = jnp.max(s, axis=0, keepdims=True)

    def soft_head(tile, slot, m_old, h, valid, bias, qc):
        c0, c1 = qc or (0, tq)
        cols = slice(h * tq + c0, h * tq + c1)
        buf = s_bufs[slot]
        k0 = pl.multiple_of(tile * tk, tk)
        m_sub = m_old[:, c0:c1]
        if valid is not None or bias is not None:
            s = (buf[:, cols] + bias[:, c0:c1] if valid is None
                 else jnp.where(valid[:, c0:c1], buf[:, cols], NEG))
            m_new = jnp.maximum(m_sub, jnp.max(s, axis=0, keepdims=True))
            p = jnp.exp2(s - m_new).astype(BF16)
        else:
            m_new = jnp.maximum(m_sub, mx_sc[slot, :, cols])
            p = jnp.exp2(buf[:, cols] - m_new).astype(BF16)
        alpha = jnp.exp2(m_sub - m_new)
        lhs = jnp.concatenate([v_rows(h, k0), ones], axis=0)
        acc_sc[h, :, c0:c1] = alpha * acc_sc[h, :, c0:c1] + _dot(lhs, p)
        parts = [m_new]
        if c0 > 0:
            parts.insert(0, m_old[:, :c0])
        if c1 < tq:
            parts.append(m_old[:, c1:])
        return jnp.concatenate(parts, axis=1) if len(parts) > 1 else m_new

    def qk_all(tile, slot, qc=None):
        for h in range(nh):
            qk_head(tile, slot, h, qc)

    def step(cur, slot, ms, nxt=None, valid=None, bias=None, before=None, qc=None, nxt_qc=None):
        out = []
        for h in range(nh):
            if nxt is not None:
                qk_head(nxt, 1 - slot, h, nxt_qc)
            if before is not None:
                before(h)
            out.append(soft_head(cur, slot, ms[h], h, valid, bias, qc))
        return tuple(out)

    def finish():
        outs = []
        for h in range(nh):
            a = acc_sc[h]
            outs.append(a[:HEAD_DIM] * (1.0 / a[HEAD_DIM:HEAD_DIM + 1]))
        return outs

    return qk_head, qk_all, step, finish


def _flash_t(ops, nh, tq, tk, n_un, tile_of, diag, diag_bias, first_done=False):
    _, qk_all, step, finish = ops

    def pair(i, ms):
        t_a, t_b, t_c = tile_of(2 * i), tile_of(2 * i + 1), tile_of(2 * i + 2)
        return step(t_b, 1, step(t_a, 0, ms, nxt=t_b), nxt=t_c)

    def odd_tail(ms):
        return step(diag, 1, step(tile_of(n_un - 1), 0, ms, nxt=diag), bias=diag_bias[...])

    def even_tail(ms):
        return step(diag, 0, ms, bias=diag_bias[...])

    if not first_done:
        qk_all(tile_of(0), 0)
    ms = (jnp.full((1, tq), NEG, F32),) * nh
    ms = lax.fori_loop(0, n_un // 2, pair, ms)
    lax.cond(n_un % 2 == 1, odd_tail, even_tail, ms)
    return finish()


def _causal_bias(tk, tq):
    assert tk == tq
    return jnp.asarray(np.where(np.arange(tk)[:, None] <= np.arange(tq)[None, :], 0.0, NEG), F32)


def _flash_scratch(nh, tq, tk, kdim, col_max=True):
    bufs = [pltpu.VMEM((kdim, nh * tq), BF16), pltpu.VMEM((tk, nh * tq), F32),
            pltpu.VMEM((tk, nh * tq), F32)]
    if col_max:
        bufs.append(pltpu.VMEM((2, 1, nh * tq), F32))
    return bufs + [pltpu.VMEM((nh, ACC_ROWS, tq), F32)]


def _fox_kernel(qt_ref, qf_ref, k_ref, kf_ref, vt_ref, cb_ref, o_ref, bm_sc, s0_sc, s1_sc, mx_sc,
                acc_sc):
    qi = pl.program_id(2)
    tq, tk = FOX_TQ, FOX_TK
    bm_sc[...] = jnp.zeros_like(bm_sc)
    for h in range(2):
        rows = slice(h * HEAD_DIM, (h + 1) * HEAD_DIM)
        bm_sc[rows, h * tq:(h + 1) * tq] = qt_ref[0, rows, :]
        bm_sc[LANES:LANES + QF_ROWS, h * tq:(h + 1) * tq] = qf_ref[0, 0, h]

    def k_tile(k0):
        return jnp.concatenate([k_ref[0, pl.ds(k0, tk), :], kf_ref[0, 0, pl.ds(k0, tk), :]], axis=1)

    def v_rows(h, k0):
        return vt_ref[0, h * HEAD_DIM:(h + 1) * HEAD_DIM, pl.ds(k0, tk)]

    diag = (qi * tq) // tk
    ops = _flash_ops(bm_sc, (s0_sc, s1_sc), mx_sc, acc_sc, 2, tq, tk, k_tile, v_rows)
    outs = _flash_t(ops, 2, tq, tk, diag, lambda i: i, diag, cb_ref)
    o_ref[0] = jnp.concatenate(outs, axis=0).T.astype(o_ref.dtype)


def _fox(u3, ut, kfeat, qfeat, *, q_row, k_col, v_row):
    B, S, _ = u3.shape
    npair = FOX_HEADS // 2
    return pl.pallas_call(
        _fox_kernel,
        grid=(B, npair, S // FOX_TQ),
        in_specs=[
            pl.BlockSpec((1, LANES, FOX_TQ), lambda b, j, i: (b, q_row + j, i)),
            pl.BlockSpec((1, 1, 2, QF_ROWS, FOX_TQ), lambda b, j, i: (b, j, 0, 0, i)),
            pl.BlockSpec((1, S, LANES), lambda b, j, i: (b, 0, k_col + j)),
            pl.BlockSpec((1, 1, S, LANES), lambda b, j, i: (b, j, 0, 0)),
            pl.BlockSpec((1, LANES, S), lambda b, j, i: (b, v_row + j, 0)),
            pl.BlockSpec((FOX_TK, FOX_TQ), lambda b, j, i: (0, 0)),
        ],
        out_specs=pl.BlockSpec((1, FOX_TQ, LANES), lambda b, j, i: (b, i, j)),
        out_shape=jax.ShapeDtypeStruct((B, S, FOX_WIDTH), BF16),
        scratch_shapes=_flash_scratch(2, FOX_TQ, FOX_TK, 2 * LANES),
        compiler_params=_cp(("parallel", "parallel", "arbitrary")),
        name="fox",
    )(ut, qfeat, u3, kfeat, ut, _causal_bias(FOX_TK, FOX_TQ))


def _ret_kernel(q_ref, k_ref, v_ref, inner_ref, cross_ref, kdec_ref, cd_ref, bd_ref, gn_ref,
                o_ref, state_sc):
    @pl.when(pl.program_id(1) == 0)
    def _():
        state_sc[...] = jnp.zeros_like(state_sc)

    low = _low_half((RET_CHUNK, LANES), 1)
    inv = 1.0 / HEAD_DIM
    for c, j in [(c, j) for c in range(RET_STEP) for j in range(RET_HEADS // 2)]:
        rows = slice(c * RET_CHUNK, (c + 1) * RET_CHUNK)
        cols = slice(j * LANES, (j + 1) * LANES)
        q, k, v = q_ref[0, rows, cols], k_ref[0, rows, cols], v_ref[0, rows, cols]
        zero = jnp.zeros_like(q)
        qa, qb = jnp.where(low, q, zero), jnp.where(low, zero, q)
        pa = (_dot_nt(qa, k) * inner_ref[j, 0]).astype(BF16)
        pb = (_dot_nt(qb, k) * inner_ref[j, 1]).astype(BF16)
        o_in = jnp.where(low, _dot(pa, v), _dot(pb, v))
        state = state_sc[j]
        o = o_in + _dot(q, state.astype(BF16)) * cross_ref[j]
        kd = (k.astype(F32) * kdec_ref[j]).astype(BF16)
        state_sc[j] = state * cd_ref[j] + _dot_tn(kd, v) * bd_ref[...]
        sa = jnp.sum(jnp.where(low, o, 0.0), axis=-1, keepdims=True)
        st = jnp.sum(o, axis=-1, keepdims=True)
        mu = jnp.where(low, sa, st - sa) * inv
        d = o - mu
        d2 = d * d
        va = jnp.sum(jnp.where(low, d2, 0.0), axis=-1, keepdims=True)
        vt = jnp.sum(d2, axis=-1, keepdims=True)
        var = jnp.where(low, va, vt - va) * inv
        o_ref[0, rows, cols] = (d * lax.rsqrt(var + GN_EPS) * gn_ref[:, cols]).astype(o_ref.dtype)


def _ret_constants():
    lg = np.log(1.0 - 2.0 ** (-5.0 - np.arange(RET_HEADS)))
    i = np.arange(RET_CHUNK)
    diff = i[:, None] - i[None, :]
    inner = np.where(diff[None] >= 0, np.exp(lg[:, None, None] * np.maximum(diff, 0)[None]), 0.0)
    cross = np.exp(lg[:, None] * (i[None, :] + 1))
    kdec = np.exp(lg[:, None] * (RET_CHUNK - 1 - i)[None, :])
    cdec = np.exp(lg * RET_CHUNK)
    npair = RET_HEADS // 2
    inner = inner.reshape(npair, 2, RET_CHUNK, RET_CHUNK)

    def lanes(a):
        a = a.reshape(npair, 2, RET_CHUNK)
        return np.repeat(a.transpose(0, 2, 1), HEAD_DIM, axis=2)

    bd = np.kron(np.eye(2), np.ones((HEAD_DIM, HEAD_DIM)))
    cd = np.repeat(cdec.reshape(npair, 2), HEAD_DIM, axis=1)[:, :, None] * bd[None]
    f = lambda a: jnp.asarray(a, F32)
    return f(inner), f(lanes(cross)), f(lanes(kdec)), f(cd), f(bd)


def _retention(u3, gn_g, *, q_col, k_col, v_col):
    B, S, _ = u3.shape
    C = RET_CHUNK
    rows = RET_STEP * C
    npair = RET_HEADS // 2
    inner, cross, kdec, cd, bd = _ret_constants()
    full = lambda shape: pl.BlockSpec(shape, lambda b, i: (0,) * len(shape))
    return pl.pallas_call(
        _ret_kernel,
        grid=(B, S // rows),
        in_specs=[
            pl.BlockSpec((1, rows, RET_WIDTH), lambda b, i: (b, i, q_col)),
            pl.BlockSpec((1, rows, RET_WIDTH), lambda b, i: (b, i, k_col)),
            pl.BlockSpec((1, rows, RET_WIDTH), lambda b, i: (b, i, v_col)),
            full((npair, 2, C, C)), full((npair, C, LANES)), full((npair, C, LANES)),
            full((npair, LANES, LANES)), full((LANES, LANES)), full((1, RET_WIDTH)),
        ],
        out_specs=pl.BlockSpec((1, rows, RET_WIDTH), lambda b, i: (b, i, 0)),
        out_shape=jax.ShapeDtypeStruct((B, S, RET_WIDTH), BF16),
        scratch_shapes=[pltpu.VMEM((npair, LANES, LANES), F32)],
        compiler_params=_cp(("parallel", "arbitrary")),
        name="retention",
    )(u3, u3, u3, inner, cross, kdec, cd, bd, gn_g.reshape(1, RET_WIDTH))


def _out0_kernel(of_ref, or_ref, z_ref, x_ref, w_ref, o_ref):
    z = _silu(z_ref[...].astype(F32))
    ya = (of_ref[...].astype(F32) * z[:, :FOX_WIDTH]).astype(BF16)
    yb = (or_ref[...].astype(F32) * z[:, FOX_WIDTH:]).astype(BF16)
    o_ref[...] = x_ref[...] + _dot(ya, w_ref[:FOX_WIDTH, :]) + _dot(yb, w_ref[FOX_WIDTH:, :])


def _out0(o_f, o_r, u, x2, w_out, *, tm=OUT_TM):
    N, D = x2.shape
    return pl.pallas_call(
        _out0_kernel,
        grid=(N // tm,),
        in_specs=[
            pl.BlockSpec((tm, FOX_WIDTH), lambda i: (i, 0)),
            pl.BlockSpec((tm, RET_WIDTH), lambda i: (i, 0)),
            pl.BlockSpec((tm, D), lambda i: (i, 0)),
            pl.BlockSpec((tm, D), lambda i: (i, 0)),
            pl.BlockSpec((D, D), lambda i: (0, 0)),
        ],
        out_specs=pl.BlockSpec((tm, D), lambda i: (i, 0)),
        out_shape=jax.ShapeDtypeStruct((N, D), F32),
        compiler_params=_cp(("parallel",)),
        name="out0",
    )(o_f, o_r, u, x2, w_out)


def _out1_kernel(oc_ref, os_ref, z_ref, x_ref, w_ref, g_ref, o_ref):
    z = _silu(z_ref[...].astype(F32))
    y = ((oc_ref[...].astype(F32) + os_ref[...].astype(F32)) * z).astype(BF16)
    o_ref[...] = _rms(x_ref[...] + _dot(y, w_ref[...]), g_ref[...])


def _out1(o_c, o_s, u, x2, w_out, final_g, *, tm=OUT_TM):
    N, D = x2.shape
    row = pl.BlockSpec((tm, D), lambda i: (i, 0))
    return pl.pallas_call(
        _out1_kernel,
        grid=(N // tm,),
        in_specs=[row, row, row, row,
                  pl.BlockSpec((D, D), lambda i: (0, 0)),
                  pl.BlockSpec((1, D), lambda i: (0, 0))],
        out_specs=row,
        out_shape=jax.ShapeDtypeStruct((N, D), F32),
        compiler_params=_cp(("parallel",)),
        name="out1",
    )(o_c, o_s, u, x2, w_out, final_g.reshape(1, D))


def _compress_kernel(x_ref, pea_ref, peb_ref, wa_ref, wb_ref, w2_ref, *rest, transposed):
    x = x_ref[0].astype(F32)
    a = _dot((x + pea_ref[...]).astype(BF16), wa_ref[0])
    b = _dot((x + peb_ref[...]).astype(BF16), wb_ref[0])
    nseg = x.shape[0]
    pre = a + pltpu.roll(b, nseg - 1, 0)
    hid = _silu(pre).astype(BF16)
    if transposed:
        o_ref, = rest
        o_ref[0, 0] = _dot_nt(w2_ref[...], hid).astype(o_ref.dtype)
    else:
        feat_ref, o_ref = rest
        o_ref[0, 0] = (_dot(hid, w2_ref[...]) + feat_ref[...].astype(F32)).astype(o_ref.dtype)


def _compress(a3, pe, w1, w2, *, transposed):
    B, nseg, wid = a3.shape
    half = CMP_STRIDE * HEAD_DIM
    eye = jnp.eye(NSA_GROUPS, dtype=w1.dtype)

    def big(wh):
        w4 = wh.reshape(CMP_STRIDE, 1, HEAD_DIM, CMP_HIDDEN)
        sel = eye[:, None, :, None, None]
        return (sel * w4[None]).reshape(NSA_GROUPS, wid, CMP_HIDDEN).astype(BF16)

    def pe_big(p):
        return jnp.broadcast_to(p[:, None, :], (CMP_STRIDE, NSA_GROUPS, HEAD_DIM)).reshape(1, wid)

    args = [a3, pe_big(pe[:CMP_STRIDE]), pe_big(pe[CMP_STRIDE:]), big(w1[:half]), big(w1[half:])]
    in_specs = [
        pl.BlockSpec((1, nseg, wid), lambda b, g: (b, 0, 0)),
        pl.BlockSpec((1, wid), lambda b, g: (0, 0)),
        pl.BlockSpec((1, wid), lambda b, g: (0, 0)),
        pl.BlockSpec((1, wid, CMP_HIDDEN), lambda b, g: (g, 0, 0)),
        pl.BlockSpec((1, wid, CMP_HIDDEN), lambda b, g: (g, 0, 0)),
    ]
    if transposed:
        w2d = w2.T.astype(BF16)
        oshape, oblock = (B, NSA_GROUPS, HEAD_DIM, nseg), (1, 1, HEAD_DIM, nseg)
        args.append(w2d)
        in_specs.append(pl.BlockSpec(w2d.shape, lambda b, g: (0, 0)))
    else:
        w2d = jnp.pad(w2, ((0, 0), (0, LANES - HEAD_DIM))).astype(BF16)
        oshape, oblock = (B, NSA_GROUPS, nseg, LANES), (1, 1, nseg, LANES)
        feat = _pos_features(jnp.arange(nseg, dtype=jnp.int32) * CMP_STRIDE + (CMP_BLOCK - 1), LANES)
        args += [w2d, feat]
        in_specs += [pl.BlockSpec(w2d.shape, lambda b, g: (0, 0)),
                     pl.BlockSpec((nseg, LANES), lambda b, g: (0, 0))]
    return pl.pallas_call(
        functools.partial(_compress_kernel, transposed=transposed),
        grid=(B, NSA_GROUPS),
        in_specs=in_specs,
        out_specs=pl.BlockSpec(oblock, lambda b, g: (b, g, 0, 0)),
        out_shape=jax.ShapeDtypeStruct(oshape, BF16),
        compiler_params=_cp(("parallel", "parallel")),
        name="compress",
    )(*args)


def _slope_table():
    s = np.asarray(2.0 ** (-8.0 * (np.arange(NSA_HEADS) + 1) / NSA_HEADS), np.float32)
    sl = np.asarray(s.astype(np.float64) * LOG2E, np.float32)
    p1, p2, p3 = _np_pieces(sl)
    tab = np.zeros((NSA_HEADS, QF_ROWS), np.float32)
    for k, p in enumerate((p1, p1, p2, p2, p3, p3)):
        tab[:, k] = p
    tab[:, 6] = sl
    tab = np.broadcast_to(tab.reshape(NSA_GROUPS, NSA_HPG * QF_ROWS, 1),
                          (NSA_GROUPS, NSA_HPG * QF_ROWS, NSA_TQ))
    return jnp.asarray(tab)


def _pos_features(pos, width):
    pos = pos[:, None]
    lane = jnp.arange(width, dtype=jnp.int32)[None, :] % LANES
    hi = ((pos // SLC_BLOCK) * SLC_BLOCK).astype(F32)
    lo = (pos % SLC_BLOCK).astype(F32)
    k = lane - FEAT0
    f = jnp.where((k >= 0) & (k < 6), jnp.where(k % 2 == 0, hi, lo), 0.0)
    f = jnp.where((k >= 6) & (k < 9), 1.0, f)
    return f.astype(BF16)


def _nsa_queries(qt_ref, tab_ref, t0):
    tq = qt_ref.shape[2]
    r = lax.broadcasted_iota(jnp.int32, (QF_ROWS, tq), 0)
    t = (t0 + lax.broadcasted_iota(jnp.int32, (1, tq), 1)).astype(F32)
    zeros = jnp.zeros((LANES - HEAD_DIM - QF_ROWS, tq), BF16)
    out = []
    for i in range(NSA_HPG):
        tile = tab_ref[0, i * QF_ROWS:(i + 1) * QF_ROWS, :]
        a1, a2, a3 = _pieces(-(tile[6:7, :] * t))
        feat = jnp.where(r == 6, a1, jnp.where(r == 7, a2, jnp.where(r == 8, a3,
                                                                     jnp.where(r < 6, tile, 0.0))))
        out.append(jnp.concatenate([qt_ref[0, i * HEAD_DIM:(i + 1) * HEAD_DIM, :],
                                    feat.astype(BF16), zeros], axis=0))
    return out


def _gates_t(gt_ref, bg_ref, branch):
    gl = gt_ref[0] + bg_ref[...]
    return [_sigmoid(gl[N_BRANCH * i + branch:N_BRANCH * i + branch + 1, :]) for i in range(NSA_HPG)]


def _gated(outs_t, gates):
    return [o * gt for o, gt in zip(outs_t, gates)]


def _store_heads(o_ref, g):
    o_ref[0, :, :LANES] = jnp.concatenate(g[:2], axis=0).T.astype(o_ref.dtype)
    o_ref[0, :, LANES:] = jnp.concatenate(g[2:], axis=0).T.astype(o_ref.dtype)


def _nsa_specs(q_row):
    return dict(
        q=pl.BlockSpec((1, NSA_HPG * HEAD_DIM, NSA_TQ), lambda b, g, i: (b, q_row + g, i)),
        tab=pl.BlockSpec((1, NSA_HPG * QF_ROWS, NSA_TQ), lambda b, g, i: (g, 0, 0)),
        gt=pl.BlockSpec((1, GATE_ROWS, NSA_TQ), lambda b, g, i: (b, g, i)),
        bg=pl.BlockSpec((GATE_ROWS, 1), lambda b, g, i: (g, 0)),
        out=pl.BlockSpec((1, NSA_TQ, NSA_HPG * HEAD_DIM), lambda b, g, i: (b, i, g)),
    )


def _argmax_first(v, idx):
    n = v.shape[0]
    slabs = [(v[r:r + 8], idx[r:r + 8]) for r in range(0, n, 8)]
    while len(slabs) > 1:
        nxt = []
        for (va, ia), (vb, ib) in zip(slabs[0::2], slabs[1::2]):
            right = vb > va
            nxt.append((jnp.where(right, vb, va), jnp.where(right, ib, ia)))
        if len(slabs) % 2:
            nxt.append(slabs[-1])
        slabs = nxt
    v, idx = slabs[0]
    mx = jnp.max(v, axis=0, keepdims=True)
    first = jnp.min(jnp.where(v == mx, idx, float(n)), axis=0, keepdims=True)
    return mx, first


def _cmp_body(nc, t0, qh, kc_ref, vct_ref, mt_ref, grp_ref, gates, o_ref, sel_ref, flag_ref):
    tq = NSA_TQ
    rows = nc * CMP_CHUNK
    full = max(rows - CMP_CHUNK - 8, 0)
    nseg = kc_ref.shape[2]
    kc = kc_ref[0, 0, :rows, :]
    vct = vct_ref[0, 0, :, :rows]
    t = t0 + lax.broadcasted_iota(jnp.int32, (1, tq), 1)
    cidx = full + lax.broadcasted_iota(jnp.int32, (rows - full, 1), 0)
    valid = (cidx * CMP_STRIDE + (CMP_BLOCK - 1) <= t) & (cidx < nseg - 1)
    psum = jnp.zeros((rows, tq), F32)
    ps = []
    s_all = _dot(kc, jnp.concatenate(qh, axis=1))
    for i in range(NSA_HPG):
        s = s_all[:, i * tq:(i + 1) * tq]
        s_last = jnp.where(valid, s[full:], NEG)
        m = jnp.max(s_last, axis=0, keepdims=True)
        if nc > 1:
            m = jnp.maximum(m, jnp.max(s[:full], axis=0, keepdims=True))
        e = jnp.where(valid, jnp.exp2(s_last - m), 0.0)
        if nc > 1:
            e = jnp.concatenate([jnp.exp2(s[:full] - m), e], axis=0)
        l = jnp.sum(e, axis=0, keepdims=True)
        p = e * jnp.where(l > 0.0, 1.0 / l, 0.0)
        psum = psum + p
        ps.append(p.astype(BF16))
    o_all = _dot(vct, jnp.concatenate(ps, axis=1))
    _store_heads(o_ref, _gated([o_all[:, i * tq:(i + 1) * tq] for i in range(NSA_HPG)], gates))
    ns = rows * CMP_STRIDE // SLC_BLOCK
    mt = mt_ref[:ns, :rows]
    imp = sum(_dot(mt, p.astype(BF16)) for p in _pieces(psum))
    blk = lax.broadcasted_iota(jnp.int32, (ns, 1), 0)
    cur = t // SLC_BLOCK
    bvalid = blk * SLC_BLOCK <= t
    forced = (blk == 0) | (blk == cur) | (blk == cur - 1)
    score = jnp.where(forced, -jnp.inf, jnp.where(bvalid, imp, NEG))
    blk_f = jnp.broadcast_to(blk.astype(F32), (ns, tq))
    work = score
    for _ in range(SLC_TOPK - N_FORCED):
        mx, first = _argmax_first(work, blk_f)
        work = jnp.where(blk_f == first, -jnp.inf, work)
    picked = (score > mx) | ((score == mx) & (blk_f <= first))
    selneg = jnp.where(bvalid & (forced | picked), 0.0, -MASK_BIG)
    if ns < NS_PAD:
        selneg = jnp.concatenate([selneg, jnp.full((NS_PAD - ns, tq), -MASK_BIG, F32)], axis=0)
    sel_ref[0, 0] = selneg.astype(sel_ref.dtype)
    picked = jnp.where(selneg == 0.0, 1.0, 0.0).astype(BF16)
    used = _dot_nt(jnp.ones((8, tq), BF16), picked)
    used = jnp.where(used > 0.0, 1.0, 0.0).astype(BF16)
    flag_ref[0] = (_dot(used, grp_ref[...])[0:1] > 0.0).astype(jnp.int32)


def _cmp_kernel(q_ref, kc_ref, vct_ref, mt_ref, grp_ref, tab_ref, gt_ref, bg_ref,
                o_ref, sel_ref, flag_ref):
    t0 = pl.program_id(2) * NSA_TQ
    qh = _nsa_queries(q_ref, tab_ref, t0)
    gates = _gates_t(gt_ref, bg_ref, 0)
    nchunk = kc_ref.shape[2] // CMP_CHUNK
    last = t0 // (CMP_CHUNK * CMP_STRIDE)
    for nc in range(1, nchunk + 1):
        pl.when(last == nc - 1)(functools.partial(
            _cmp_body, nc, t0, qh, kc_ref, vct_ref, mt_ref, grp_ref, gates, o_ref, sel_ref, flag_ref))


def _cmp_to_slc_t(nseg, ns):
    c0 = np.arange(nseg)[:, None] * CMP_STRIDE
    s0 = np.arange(ns)[None, :] * SLC_BLOCK
    overlap = np.clip(np.minimum(c0 + CMP_BLOCK, s0 + SLC_BLOCK) - np.maximum(c0, s0), 0, None)
    m = overlap / CMP_STRIDE
    m[nseg - 1] = 0.0
    mt = np.zeros((NS_PAD, nseg))
    mt[:ns] = m.T
    return jnp.asarray(mt, BF16)


def _tile_groups():
    per = SLC_TK // SLC_BLOCK
    g = (np.arange(NS_PAD)[:, None] // per) == np.arange(NS_PAD)[None, :]
    return jnp.asarray(g, BF16)


def _cmp_attention(ut, kcmp, vcmp_t, gt, bg, *, q_row):
    B, _, S = ut.shape
    nseg = kcmp.shape[2]
    nq = S // NSA_TQ
    sp = _nsa_specs(q_row)
    return pl.pallas_call(
        _cmp_kernel,
        grid=(B, NSA_GROUPS, S // NSA_TQ),
        in_specs=[
            sp["q"],
            pl.BlockSpec((1, 1, nseg, LANES), lambda b, g, i: (b, g, 0, 0)),
            pl.BlockSpec((1, 1, HEAD_DIM, nseg), lambda b, g, i: (b, g, 0, 0)),
            pl.BlockSpec((NS_PAD, nseg), lambda b, g, i: (0, 0)),
            pl.BlockSpec((NS_PAD, NS_PAD), lambda b, g, i: (0, 0)),
            sp["tab"], sp["gt"], sp["bg"],
        ],
        out_specs=[sp["out"], pl.BlockSpec((1, 1, NS_PAD, NSA_TQ), lambda b, g, i: (b, g, 0, i)),
                   pl.BlockSpec((1, 1, NS_PAD), lambda b, g, i: ((b * NSA_GROUPS + g) * nq + i, 0, 0))],
        out_shape=[jax.ShapeDtypeStruct((B, S, NSA_WIDTH), BF16),
                   jax.ShapeDtypeStruct((B, NSA_GROUPS, NS_PAD, S), BF16),
                   jax.ShapeDtypeStruct((B * NSA_GROUPS * nq, 1, NS_PAD), jnp.int32)],
        compiler_params=_cp(("parallel", "parallel", "arbitrary")),
        name="cmp_attention",
    )(ut, kcmp, vcmp_t, _cmp_to_slc_t(nseg, S // SLC_BLOCK), _tile_groups(), _slope_table(), gt, bg)


def _window_branch(qh, t0, k_ref, vt_ref, bias_ref, bm_sc, s_bufs, acc_sc, last_before):
    tq, tk = NSA_TQ, WIN_TK
    bm_sc[...] = jnp.concatenate(qh, axis=1)
    _, qk_all, step, finish = _flash_ops(
        bm_sc, s_bufs, None, acc_sc, NSA_HPG, tq, tk,
        lambda k0: k_ref[0, pl.ds(k0, tk), :], lambda h, k0: vt_ref[0, :, pl.ds(k0, tk)])
    first = jnp.maximum(t0 - WINDOW, 0) // tk
    ntile = (WINDOW + tq) // tk

    def run(qcs):
        qk_all(first, 0, qcs[0])
        ms = (jnp.full((1, tq), NEG, F32),) * NSA_HPG
        for j in range(ntile):
            last = j + 1 == ntile
            ms = step(first + j, j % 2, ms, nxt=None if last else first + j + 1,
                      bias=bias_ref[0, j * tk:(j + 1) * tk, :],
                      before=last_before if last else None, qc=qcs[j],
                      nxt_qc=None if last else qcs[j + 1])

    lane = lambda v: max(0, min(tq, v)) // LANES * LANES
    qcs = []
    for j in range(ntile):
        c0, c1 = lane(j * tk - WINDOW + 1), tq - lane(tq - (j + 1) * tk)
        qcs.append(None if (c0, c1) == (0, tq) else (c0, c1))
    if all(q is None for q in qcs):
        run(qcs)
    else:
        lax.cond(t0 >= WINDOW, lambda: run(qcs), lambda: run([None] * ntile))
    return finish()


def _slc_win_kernel(fl_ref, q_ref, k_ref, vt_ref, sel_ref, e_ref, kw_ref, vwt_ref, bias_ref,
                    cb_ref, tab_ref, gt_ref, bg_ref, o_ref, bm_sc, s0_sc, s1_sc, mx_sc, acc_sc,
                    wbm_sc, ws0_sc, ws1_sc, wacc_sc, tiles_sm):
    qi = pl.program_id(2)
    tq, tk = NSA_TQ, SLC_TK
    t0 = qi * tq
    diag = t0 // tk
    row = (pl.program_id(0) * NSA_GROUPS + pl.program_id(1)) * pl.num_programs(2) + qi
    n_un = jnp.int32(0)
    for j in range(k_ref.shape[1] // tk):
        tiles_sm[n_un] = jnp.int32(j)
        n_un = n_un + ((fl_ref[row, j] > 0) & (j < diag)).astype(jnp.int32)
    tiles_sm[n_un] = diag
    qh = _nsa_queries(q_ref, tab_ref, t0)
    selneg = sel_ref[0, 0]
    bm_sc[...] = jnp.concatenate([jnp.concatenate([q, selneg], axis=0) for q in qh], axis=1)

    def k_tile(k0):
        return jnp.concatenate([k_ref[0, pl.ds(k0, tk), :], e_ref[pl.ds(k0, tk), :]], axis=1)

    def v_rows(h, k0):
        return vt_ref[0, :, pl.ds(k0, tk)]

    ops = _flash_ops(bm_sc, (s0_sc, s1_sc), mx_sc, acc_sc, NSA_HPG, tq, tk, k_tile, v_rows)
    o_win = _gated(_window_branch(qh, t0, kw_ref, vwt_ref, bias_ref, wbm_sc, (ws0_sc, ws1_sc),
                                  wacc_sc, lambda h: ops[0](tiles_sm[0], 0, h)),
                   _gates_t(gt_ref, bg_ref, 2))
    outs = _flash_t(ops, NSA_HPG, tq, tk, n_un, lambda i: tiles_sm[i], diag, cb_ref,
                    first_done=True)
    o_slc = _gated(outs, _gates_t(gt_ref, bg_ref, 1))
    _store_heads(o_ref, [a + b for a, b in zip(o_slc, o_win)])


def _win_bias():
    tq, nk = NSA_TQ, WINDOW + NSA_TQ
    out = []
    for p in range(WINDOW // tq + 1):
        t0 = p * tq
        key = max(t0 - WINDOW, 0) + np.arange(nk)[:, None]
        qpos = t0 + np.arange(tq)[None, :]
        out.append(np.where((key <= qpos) & (key > qpos - WINDOW), 0.0, NEG))
    return jnp.asarray(np.stack(out), F32)


def _block_onehot(S):
    e = (np.arange(S)[:, None] // SLC_BLOCK) == np.arange(NS_PAD)[None, :]
    return jnp.asarray(e, BF16)


def _slc_win_attention(u3, ut, selneg, flags, gt, bg, *, q_row, ks_col, vs_row, kw_col, vw_row):
    B, S, _ = u3.shape
    sp = {k: pl.BlockSpec(v.block_shape, lambda b, g, i, fl, f=v.index_map: f(b, g, i))
          for k, v in _nsa_specs(q_row).items()}
    bias = _win_bias()
    npat = bias.shape[0]
    kspec = lambda col: pl.BlockSpec((1, S, LANES), lambda b, g, i, fl: (b, 0, col + g))
    vspec = lambda row: pl.BlockSpec((1, HEAD_DIM, S), lambda b, g, i, fl: (b, row + g, 0))
    grid_spec = pltpu.PrefetchScalarGridSpec(
        num_scalar_prefetch=1,
        grid=(B, NSA_GROUPS, S // NSA_TQ),
        in_specs=[
            sp["q"], kspec(ks_col), vspec(vs_row),
            pl.BlockSpec((1, 1, NS_PAD, NSA_TQ), lambda b, g, i, fl: (b, g, 0, i)),
            pl.BlockSpec((S, NS_PAD), lambda b, g, i, fl: (0, 0)),
            kspec(kw_col), vspec(vw_row),
            pl.BlockSpec((1,) + bias.shape[1:],
                         lambda b, g, i, fl: (jnp.minimum(i, npat - 1), 0, 0)),
            pl.BlockSpec((SLC_TK, NSA_TQ), lambda b, g, i, fl: (0, 0)),
            sp["tab"], sp["gt"], sp["bg"],
        ],
        out_specs=sp["out"],
        scratch_shapes=_flash_scratch(NSA_HPG, NSA_TQ, SLC_TK, 2 * LANES)
        + _flash_scratch(NSA_HPG, NSA_TQ, WIN_TK, LANES, col_max=False)
        + [pltpu.SMEM((S // SLC_TK + 1,), jnp.int32)],
    )
    return pl.pallas_call(
        _slc_win_kernel,
        grid_spec=grid_spec,
        out_shape=jax.ShapeDtypeStruct((B, S, NSA_WIDTH), BF16),
        compiler_params=_cp(("parallel", "parallel", "arbitrary")),
        name="slc_win_attention",
    )(flags, ut, u3, ut, selneg, _block_onehot(S), u3, ut, bias, _causal_bias(SLC_TK, NSA_TQ),
      _slope_table(), gt, bg)


def _aug_groups(w):
    d = w.shape[0]
    w = w.reshape(d, NSA_GROUPS, HEAD_DIM)
    return jnp.pad(w, ((0, 0), (0, 0), (0, LANES - HEAD_DIM))).reshape(d, NSA_GROUPS * LANES)


def _even_layer(x, norm_g, w_in, b_f, gn_g, w_out):
    B, S, D = x.shape
    qscale = HEAD_DIM ** -0.5 * LOG2E
    q_f, k_f, v_f, w_fl, q_r, k_r, v_r, z = jnp.split(
        w_in, np.cumsum([FOX_WIDTH] * 3 + [FOX_HEADS] + [RET_WIDTH] * 3).tolist(), axis=1)
    w = jnp.concatenate([z, k_f, q_r, k_r * HEAD_DIM ** -0.5, v_r], axis=1).astype(BF16)
    w_t = jnp.concatenate([q_f * qscale, v_f], axis=1).T.astype(BF16)
    x2 = x.reshape(B * S, D)
    u, ut = _proj(x2, norm_g, w, seq=S, w_t=[w_t], t_dtypes=[BF16], tn=2 * PROJ_TN)
    u3 = u.reshape(B, S, -1)
    kfeat, qfeat = _fgate(x, norm_g, w_fl, b_f, tile=min(512, S))
    o_f = _fox(u3, ut, kfeat, qfeat, q_row=0, k_col=D // LANES, v_row=FOX_WIDTH // LANES)
    rb = (D + FOX_WIDTH) // RET_WIDTH
    o_r = _retention(u3, gn_g, q_col=rb, k_col=rb + 1, v_col=rb + 2)
    out = _out0(o_f.reshape(B * S, -1), o_r.reshape(B * S, -1), u, x2, w_out.astype(BF16))
    return out.reshape(B, S, D)


def _odd_layer(x, norm_g, w_in, b_gate, pe_k, pe_v, wk1, wk2, wv1, wv2, w_out, final_g):
    B, S, D = x.shape
    assert S // SLC_BLOCK <= NS_PAD
    qscale = HEAD_DIM ** -0.5 * LOG2E
    sizes = [NSA_WIDTH] + [NSA_KV_WIDTH] * 6 + [NSA_HEADS * N_BRANCH]
    q, kc, vc, ks, vs, kw, vw, gl, z = jnp.split(w_in, np.cumsum(sizes).tolist(), axis=1)
    w = jnp.concatenate([z, kc, vc, _aug_groups(ks), _aug_groups(kw)], axis=1).astype(BF16)
    per_group = NSA_HPG * N_BRANCH
    glt = jnp.pad(gl.T.reshape(NSA_GROUPS, per_group, D), ((0, 0), (0, GATE_ROWS - per_group), (0, 0)))
    glt = glt.reshape(NSA_GROUPS * GATE_ROWS, D).astype(BF16)
    bg = jnp.pad(b_gate.reshape(NSA_GROUPS, per_group), ((0, 0), (0, GATE_ROWS - per_group)))
    bg = bg.reshape(NSA_GROUPS * GATE_ROWS, 1)
    w_vt = jnp.concatenate([q * qscale, vs, vw], axis=1).T.astype(BF16)
    x2 = x.reshape(B * S, D)
    kcol = D + 2 * NSA_KV_WIDTH
    kwid = NSA_GROUPS * LANES
    u, kc_a, vc_a, ut, gt = _proj(
        x2, norm_g, w, seq=S, split_cols=(D, D + NSA_KV_WIDTH), split_width=NSA_KV_WIDTH,
        addend=_pos_features(jnp.arange(S, dtype=jnp.int32), kwid), add_cols=(kcol, kcol + kwid),
        w_t=[w_vt, glt], t_dtypes=[BF16, F32], tn=w.shape[1] // 2)
    u3 = u.reshape(B, S, -1)
    nseg = S // CMP_STRIDE
    kcmp = _compress(kc_a.reshape(B, nseg, -1), pe_k, wk1, wk2, transposed=False)
    vcmp_t = _compress(vc_a.reshape(B, nseg, -1), pe_v, wv1, wv2, transposed=True)
    o_c, selneg, flags = _cmp_attention(ut, kcmp, vcmp_t, gt, bg, q_row=0)
    kb = (D + 2 * NSA_KV_WIDTH) // LANES
    vb = NSA_WIDTH // HEAD_DIM
    o_s = _slc_win_attention(u3, ut, selneg, flags[:, 0, :S // SLC_TK], gt, bg, q_row=0, ks_col=kb,
                             vs_row=vb, kw_col=kb + NSA_GROUPS, vw_row=vb + NSA_GROUPS)
    r = lambda a: a.reshape(B * S, -1)
    out = _out1(r(o_c), r(o_s), u, x2, w_out.astype(BF16), final_g)
    return out.reshape(B, S, D)


def kernel(x, even_norm_g, even_w_in, even_b_f, even_gn_g, even_w_out, odd_norm_g, odd_w_in,
           odd_b_gate, odd_pe_k, odd_pe_v, odd_wk1, odd_wk2, odd_wv1, odd_wv2, odd_w_out, final_g):
    x = _even_layer(x, even_norm_g[0], even_w_in[0], even_b_f[0], even_gn_g[0], even_w_out[0])
    return _odd_layer(x, odd_norm_g[0], odd_w_in[0], odd_b_gate[0], odd_pe_k[0], odd_pe_v[0],
                      odd_wk1[0], odd_wk2[0], odd_wv1[0], odd_wv2[0], odd_w_out[0], final_g)
```

```python
import functools
import math

import jax
import jax.numpy as jnp
import numpy as np
from jax import lax
from jax.experimental import pallas as pl
from jax.experimental.pallas import tpu as pltpu

D_MODEL = 1024
HEAD_DIM = 64
LANES = 128
FOX_HEADS = 8
RET_HEADS = 8
FOX_WIDTH = FOX_HEADS * HEAD_DIM
RET_WIDTH = RET_HEADS * HEAD_DIM
RET_CHUNK = 128
RET_STEP = 8
NSA_HEADS = 16
NSA_GROUPS = 4
NSA_HPG = NSA_HEADS // NSA_GROUPS
NSA_WIDTH = NSA_HEADS * HEAD_DIM
NSA_KV_WIDTH = NSA_GROUPS * HEAD_DIM
N_BRANCH = 3
GATE_ROWS = 16
CMP_BLOCK = 32
CMP_STRIDE = 16
CMP_HIDDEN = 256
CMP_CHUNK = 128
SLC_BLOCK = 64
SLC_TOPK = 16
N_FORCED = 3
NS_PAD = LANES
WINDOW = 512
RMS_EPS = 1e-6
GN_EPS = 1e-5
NEG = -1e30
FORCE_BONUS = 1e6
MASK_BIG = 2.0 ** 100
LOG2E = math.log2(math.e)
FEAT0 = HEAD_DIM
QF_ROWS = 16
ACC_ROWS = HEAD_DIM + 16

PROJ_TM = 1024
PROJ_TN = 512
FOX_TQ = 512
FOX_TK = 512
NSA_TQ = 512
SLC_TK = 512
SLC_HEAD = 128
WIN_TK = 256
OUT_TM = 1024
VMEM_LIMIT = 48 * 1024 * 1024

F32 = jnp.float32
BF16 = jnp.bfloat16


def _cp(sem, vmem=VMEM_LIMIT):
    return pltpu.CompilerParams(dimension_semantics=sem, vmem_limit_bytes=vmem)


def _dot(a, b):
    return jnp.dot(a, b, preferred_element_type=F32)


def _dot_nt(a, b):
    return lax.dot_general(a, b, (((1,), (1,)), ((), ())), preferred_element_type=F32)


def _dot_tn(a, b):
    return lax.dot_general(a, b, (((0,), (0,)), ((), ())), preferred_element_type=F32)


def _rms(x, g):
    return x * lax.rsqrt(jnp.mean(x * x, axis=-1, keepdims=True) + RMS_EPS) * g


def _silu(x):
    return x * (1.0 / (1.0 + jnp.exp(-x)))


def _sigmoid(x):
    return 1.0 / (1.0 + jnp.exp(-x))


def _low_half(shape, axis):
    return lax.broadcasted_iota(jnp.int32, shape, axis) < HEAD_DIM


def _pieces(v):
    p1 = v.astype(BF16).astype(F32)
    r = v - p1
    p2 = r.astype(BF16).astype(F32)
    p3 = (r - p2).astype(BF16).astype(F32)
    return p1, p2, p3


def _np_pieces(v):
    v = np.asarray(v, np.float64)
    bf = lambda a: np.asarray(a, np.float32).astype(BF16).astype(np.float64)
    p1 = bf(v)
    p2 = bf(v - p1)
    p3 = bf(v - p1 - p2)
    return p1, p2, p3


def _fgate_kernel(x_ref, g_ref, wf_ref, b_ref, pk_ref, kc_ref, pq_ref, qc_ref, kf_ref, qf_ref,
                  carry):
    @pl.when(pl.program_id(1) == 0)
    def _():
        carry[...] = jnp.zeros_like(carry)

    h = _rms(x_ref[0], g_ref[...])
    t = h.shape[0]
    h1 = h.astype(BF16)
    h2 = (h - h1.astype(F32)).astype(BF16)
    w = wf_ref[...]
    w1 = w.astype(BF16)
    w2 = (w - w1.astype(F32)).astype(BF16)
    nf = w.shape[1]
    r = _dot(jnp.concatenate([h1, h2], axis=0), jnp.concatenate([w1, w2], axis=1))
    f = r[:t, :nf] + r[:t, nf:] + r[t:, :nf] + b_ref[...]
    ls = jnp.minimum(f, 0.0) - jnp.log(1.0 + jnp.exp(-jnp.abs(f)))
    r = lax.broadcasted_iota(jnp.int32, (t, t), 0)
    c = lax.broadcasted_iota(jnp.int32, (t, t), 1)
    lower = jnp.where(c <= r, 1.0, 0.0).astype(BF16)
    r = _dot(lower, jnp.concatenate([p.astype(BF16) for p in _pieces(ls)], axis=1))
    cs = r[:, :nf] + r[:, nf:2 * nf] + r[:, 2 * nf:] + carry[...]
    carry[...] = cs[t - 1:t, :]
    cl = cs * LOG2E
    p1, p2, p3 = (p.astype(BF16) for p in _pieces(cl))
    kf = _dot(p1, pk_ref[0]) + _dot(p2, pk_ref[1]) + _dot(p3, pk_ref[2]) + kc_ref[...]
    for j in range(FOX_HEADS // 2):
        kf_ref[0, j] = kf[:, j * LANES:(j + 1) * LANES].astype(BF16)
    qf = (_dot_nt(pq_ref[0], p1) + _dot_nt(pq_ref[1], p2) + _dot_nt(pq_ref[2], p3)
          + qc_ref[...])
    qf_ref[0] = qf.astype(BF16).reshape(FOX_HEADS // 2, 2, QF_ROWS, t)


def _fgate_tables():
    npair = FOX_HEADS // 2
    pk = np.zeros((3, FOX_HEADS, npair * LANES), np.float32)
    kc = np.zeros((1, npair * LANES), np.float32)
    pq = np.zeros((3, FOX_HEADS * QF_ROWS, FOX_HEADS), np.float32)
    qc = np.zeros((FOX_HEADS * QF_ROWS, 1), np.float32)
    for h in range(FOX_HEADS):
        j, b = divmod(h, 2)
        for i in range(3):
            pk[i, h, j * LANES + 6 * b + i] = 1.0
            pq[i, h * QF_ROWS + 3 + i, h] = 1.0
            qc[h * QF_ROWS + 6 * b + i, 0] = -1.0
    for j in range(npair):
        kc[0, j * LANES + 3:j * LANES + 6] = 1.0
    return (jnp.asarray(pk, BF16), jnp.asarray(kc, F32), jnp.asarray(pq, BF16), jnp.asarray(qc, F32))


def _fgate(x, g, wf, b_f, *, tile):
    B, S, D = x.shape
    npair = FOX_HEADS // 2
    tables = _fgate_tables()
    return pl.pallas_call(
        _fgate_kernel,
        grid=(B, S // tile),
        in_specs=[
            pl.BlockSpec((1, tile, D), lambda b, s: (b, s, 0)),
            pl.BlockSpec((1, D), lambda b, s: (0, 0)),
            pl.BlockSpec((D, FOX_HEADS), lambda b, s: (0, 0)),
            pl.BlockSpec((1, FOX_HEADS), lambda b, s: (0, 0)),
        ] + [pl.BlockSpec(t.shape, lambda b, s, n=t.ndim: (0,) * n) for t in tables],
        out_specs=[
            pl.BlockSpec((1, npair, tile, LANES), lambda b, s: (b, 0, s, 0)),
            pl.BlockSpec((1, npair, 2, QF_ROWS, tile), lambda b, s: (b, 0, 0, 0, s)),
        ],
        out_shape=[
            jax.ShapeDtypeStruct((B, npair, S, LANES), BF16),
            jax.ShapeDtypeStruct((B, npair, 2, QF_ROWS, S), BF16),
        ],
        scratch_shapes=[pltpu.VMEM((1, FOX_HEADS), F32)],
        compiler_params=_cp(("parallel", "arbitrary")),
        name="fgate",
    )(x, g.reshape(1, D), wf, b_f.reshape(1, FOX_HEADS), *tables)


def _proj_kernel(*refs, tn, ntile, split_cols, add_cols, n_t):
    it = iter(refs)
    x_ref, g_ref, w_ref = next(it), next(it), next(it)
    add_ref = next(it) if add_cols else None
    wt_refs = [next(it) for _ in range(n_t)]
    u_ref = next(it)
    e_refs = [next(it) for _ in split_cols]
    ut_refs = [next(it) for _ in range(n_t)]
    h_sc = next(it)
    j = pl.program_id(1)

    @pl.when(j == 0)
    def _():
        h = _rms(x_ref[...], g_ref[...]).astype(BF16)
        h_sc[...] = h
        for wt_ref, ut_ref in zip(wt_refs, ut_refs):
            ut_ref[0] = _dot_nt(wt_ref[...], h).astype(ut_ref.dtype)

    acc = _dot(h_sc[...], w_ref[...])
    u_ref[...] = acc.astype(u_ref.dtype)

    def extras(t):
        lo = t * tn
        adds = [c - lo for c in add_cols if lo <= c < lo + tn]
        splits = [(e, c - lo) for e, c in zip(e_refs, split_cols) if lo <= c < lo + tn]

        def body():
            for c in adds:
                cols = slice(c, c + add_ref.shape[1])
                u_ref[:, cols] = (acc[:, cols] + add_ref[...].astype(F32)).astype(u_ref.dtype)
            for e_ref, c in splits:
                e_ref[...] = acc[:, c:c + e_ref.shape[1]].astype(e_ref.dtype)
        return body if adds or splits else None

    for t in range(ntile):
        body = extras(t)
        if body is not None:
            pl.when(j == t)(body)


def _proj(x2, g, w, *, seq, split_cols=(), split_width=0, addend=None, add_cols=(), w_t=(),
          t_dtypes=(), tm=PROJ_TM, tn=PROJ_TN):
    N, D = x2.shape
    W = w.shape[1]
    nbs = seq // tm
    in_specs = [
        pl.BlockSpec((tm, D), lambda i, j: (i, 0)),
        pl.BlockSpec((1, D), lambda i, j: (0, 0)),
        pl.BlockSpec((D, tn), lambda i, j: (0, j)),
    ]
    args = [x2, g.reshape(1, D), w]
    if add_cols:
        in_specs.append(pl.BlockSpec((tm, addend.shape[1]), lambda i, j: (i % nbs, 0)))
        args.append(addend)
    out_shape = [jax.ShapeDtypeStruct((N, W), BF16)]
    out_specs = [pl.BlockSpec((tm, tn), lambda i, j: (i, j))]
    for _ in split_cols:
        out_shape.append(jax.ShapeDtypeStruct((N, split_width), BF16))
        out_specs.append(pl.BlockSpec((tm, split_width), lambda i, j: (i, 0)))
    for wt, dt in zip(w_t, t_dtypes):
        rows = wt.shape[0]
        in_specs.append(pl.BlockSpec((rows, D), lambda i, j: (0, 0)))
        args.append(wt)
        out_shape.append(jax.ShapeDtypeStruct((N // seq, rows, seq), dt))
        out_specs.append(pl.BlockSpec((1, rows, tm), lambda i, j: (i // nbs, 0, i % nbs)))
    return pl.pallas_call(
        functools.partial(_proj_kernel, tn=tn, ntile=W // tn, split_cols=tuple(split_cols),
                          add_cols=tuple(add_cols), n_t=len(w_t)),
        grid=(N // tm, W // tn),
        in_specs=in_specs,
        out_specs=out_specs,
        out_shape=out_shape,
        scratch_shapes=[pltpu.VMEM((tm, D), BF16)],
        compiler_params=_cp(("parallel", "arbitrary")),
        name="proj",
    )(*args)


def _flash_ops(bm_sc, s_bufs, mx_sc, acc_sc, nh, tq, tk, k_tile, v_rows):
    acc_sc[...] = jnp.zeros_like(acc_sc)
    ones = jnp.ones((ACC_ROWS - HEAD_DIM, tk), BF16)

    def qk_head(tile, slot, h, qc=None, nk=tk):
        c0, c1 = qc or (0, tq)
        cols = slice(h * tq + c0, h * tq + c1)
        s = _dot(k_tile(pl.multiple_of(tile * tk, tk))[:nk], bm_sc[:, cols])
        s_bufs[slot][:nk, cols] = s
        if mx_sc is not None:
            mx_sc[slot, :, cols] = jnp.max(s, axis=0, keepdims=True)

    def soft_head(tile, slot, m_old, h, valid, bias, qc, nk):
        c0, c1 = qc or (0, tq)
        cols = slice(h * tq + c0, h * tq + c1)
        buf = s_bufs[slot]
        k0 = pl.multiple_of(tile * tk, tk)
        m_sub = m_old[:, c0:c1]
        if valid is not None or bias is not None:
            s = (buf[:nk, cols] + bias[:nk, c0:c1] if valid is None
                 else jnp.where(valid[:nk, c0:c1], buf[:nk, cols], NEG))
            m_new = jnp.maximum(m_sub, jnp.max(s, axis=0, keepdims=True))
            p = jnp.exp2(s - m_new).astype(BF16)
        else:
            m_new = jnp.maximum(m_sub, mx_sc[slot, :, cols])
            p = jnp.exp2(buf[:nk, cols] - m_new).astype(BF16)
        alpha = jnp.exp2(m_sub - m_new)
        lhs = jnp.concatenate([v_rows(h, k0)[:, :nk], ones[:, :nk]], axis=0)
        acc_sc[h, :, c0:c1] = alpha * acc_sc[h, :, c0:c1] + _dot(lhs, p)
        parts = [m_new]
        if c0 > 0:
            parts.insert(0, m_old[:, :c0])
        if c1 < tq:
            parts.append(m_old[:, c1:])
        return jnp.concatenate(parts, axis=1) if len(parts) > 1 else m_new

    def qk_all(tile, slot, qc=None, nk=tk):
        for h in range(nh):
            qk_head(tile, slot, h, qc, nk)

    def step(cur, slot, ms, nxt=None, valid=None, bias=None, before=None, qc=None, nxt_qc=None,
             nk=tk):
        out = []
        for h in range(nh):
            if nxt is not None:
                qk_head(nxt, 1 - slot, h, nxt_qc)
            if before is not None:
                before(h)
            out.append(soft_head(cur, slot, ms[h], h, valid, bias, qc, nk))
        return tuple(out)

    def finish():
        outs = []
        for h in range(nh):
            a = acc_sc[h]
            outs.append(a[:HEAD_DIM] * (1.0 / a[HEAD_DIM:HEAD_DIM + 1]))
        return outs

    return qk_head, qk_all, step, finish


def _flash_t(ops, nh, tq, tk, n_un, tile_of, diag, diag_bias, first_done=False, prologue=None):
    _, qk_all, step, finish = ops

    def pair(i, ms):
        t_a, t_b, t_c = tile_of(2 * i), tile_of(2 * i + 1), tile_of(2 * i + 2)
        return step(t_b, 1, step(t_a, 0, ms, nxt=t_b), nxt=t_c)

    def odd_tail(ms):
        return step(diag, 1, step(tile_of(n_un - 1), 0, ms, nxt=diag), bias=diag_bias[...])

    def even_tail(ms):
        return step(diag, 0, ms, bias=diag_bias[...])

    if not first_done:
        qk_all(tile_of(0), 0)
    ms = (jnp.full((1, tq), NEG, F32),) * nh
    if prologue is not None:
        ms = prologue(ms)
    ms = lax.fori_loop(0, n_un // 2, pair, ms)
    lax.cond(n_un % 2 == 1, odd_tail, even_tail, ms)
    return finish()


def _causal_bias(tk, tq):
    assert tk == tq
    return jnp.asarray(np.where(np.arange(tk)[:, None] <= np.arange(tq)[None, :], 0.0, NEG), F32)


def _flash_scratch(nh, tq, tk, kdim, col_max=True):
    bufs = [pltpu.VMEM((kdim, nh * tq), BF16), pltpu.VMEM((tk, nh * tq), F32),
            pltpu.VMEM((tk, nh * tq), F32)]
    if col_max:
        bufs.append(pltpu.VMEM((2, 1, nh * tq), F32))
    return bufs + [pltpu.VMEM((nh, ACC_ROWS, tq), F32)]


def _fox_kernel(qt_ref, qf_ref, k_ref, kf_ref, vt_ref, cb_ref, o_ref, bm_sc, s0_sc, s1_sc, mx_sc,
                acc_sc):
    qi = pl.program_id(2)
    tq, tk = FOX_TQ, FOX_TK
    bm_sc[...] = jnp.zeros_like(bm_sc)
    for h in range(2):
        rows = slice(h * HEAD_DIM, (h + 1) * HEAD_DIM)
        bm_sc[rows, h * tq:(h + 1) * tq] = qt_ref[0, rows, :]
        bm_sc[LANES:LANES + QF_ROWS, h * tq:(h + 1) * tq] = qf_ref[0, 0, h]

    def k_tile(k0):
        return jnp.concatenate([k_ref[0, pl.ds(k0, tk), :], kf_ref[0, 0, pl.ds(k0, tk), :]], axis=1)

    def v_rows(h, k0):
        return vt_ref[0, h * HEAD_DIM:(h + 1) * HEAD_DIM, pl.ds(k0, tk)]

    diag = (qi * tq) // tk
    ops = _flash_ops(bm_sc, (s0_sc, s1_sc), mx_sc, acc_sc, 2, tq, tk, k_tile, v_rows)
    outs = _flash_t(ops, 2, tq, tk, diag, lambda i: i, diag, cb_ref)
    o_ref[0] = jnp.concatenate(outs, axis=0).T.astype(o_ref.dtype)


def _fox(u3, ut, kfeat, qfeat, *, q_row, k_col, v_row):
    B, S, _ = u3.shape
    npair = FOX_HEADS // 2
    return pl.pallas_call(
        _fox_kernel,
        grid=(B, npair, S // FOX_TQ),
        in_specs=[
            pl.BlockSpec((1, LANES, FOX_TQ), lambda b, j, i: (b, q_row + j, i)),
            pl.BlockSpec((1, 1, 2, QF_ROWS, FOX_TQ), lambda b, j, i: (b, j, 0, 0, i)),
            pl.BlockSpec((1, S, LANES), lambda b, j, i: (b, 0, k_col + j)),
            pl.BlockSpec((1, 1, S, LANES), lambda b, j, i: (b, j, 0, 0)),
            pl.BlockSpec((1, LANES, S), lambda b, j, i: (b, v_row + j, 0)),
            pl.BlockSpec((FOX_TK, FOX_TQ), lambda b, j, i: (0, 0)),
        ],
        out_specs=pl.BlockSpec((1, FOX_TQ, LANES), lambda b, j, i: (b, i, j)),
        out_shape=jax.ShapeDtypeStruct((B, S, FOX_WIDTH), BF16),
        scratch_shapes=_flash_scratch(2, FOX_TQ, FOX_TK, 2 * LANES),
        compiler_params=_cp(("parallel", "parallel", "arbitrary")),
        name="fox",
    )(ut, qfeat, u3, kfeat, ut, _causal_bias(FOX_TK, FOX_TQ))


def _ret_kernel(q_ref, k_ref, v_ref, inner_ref, cross_ref, kdec_ref, cd_ref, bd_ref, gn_ref,
                o_ref, state_sc):
    @pl.when(pl.program_id(1) == 0)
    def _():
        state_sc[...] = jnp.zeros_like(state_sc)

    low = _low_half((RET_CHUNK, LANES), 1)
    inv = 1.0 / HEAD_DIM
    for c, j in [(c, j) for c in range(RET_STEP) for j in range(RET_HEADS // 2)]:
        rows = slice(c * RET_CHUNK, (c + 1) * RET_CHUNK)
        cols = slice(j * LANES, (j + 1) * LANES)
        q, k, v = q_ref[0, rows, cols], k_ref[0, rows, cols], v_ref[0, rows, cols]
        zero = jnp.zeros_like(q)
        qa, qb = jnp.where(low, q, zero), jnp.where(low, zero, q)
        pa = (_dot_nt(qa, k) * inner_ref[j, 0]).astype(BF16)
        pb = (_dot_nt(qb, k) * inner_ref[j, 1]).astype(BF16)
        o_in = jnp.where(low, _dot(pa, v), _dot(pb, v))
        state = state_sc[j]
        o = o_in + _dot(q, state.astype(BF16)) * cross_ref[j]
        kd = (k.astype(F32) * kdec_ref[j]).astype(BF16)
        state_sc[j] = state * cd_ref[j] + _dot_tn(kd, v) * bd_ref[...]
        sa = jnp.sum(jnp.where(low, o, 0.0), axis=-1, keepdims=True)
        st = jnp.sum(o, axis=-1, keepdims=True)
        mu = jnp.where(low, sa, st - sa) * inv
        d = o - mu
        d2 = d * d
        va = jnp.sum(jnp.where(low, d2, 0.0), axis=-1, keepdims=True)
        vt = jnp.sum(d2, axis=-1, keepdims=True)
        var = jnp.where(low, va, vt - va) * inv
        o_ref[0, rows, cols] = (d * lax.rsqrt(var + GN_EPS) * gn_ref[:, cols]).astype(o_ref.dtype)


def _ret_constants():
    lg = np.log(1.0 - 2.0 ** (-5.0 - np.arange(RET_HEADS)))
    i = np.arange(RET_CHUNK)
    diff = i[:, None] - i[None, :]
    inner = np.where(diff[None] >= 0, np.exp(lg[:, None, None] * np.maximum(diff, 0)[None]), 0.0)
    cross = np.exp(lg[:, None] * (i[None, :] + 1))
    kdec = np.exp(lg[:, None] * (RET_CHUNK - 1 - i)[None, :])
    cdec = np.exp(lg * RET_CHUNK)
    npair = RET_HEADS // 2
    inner = inner.reshape(npair, 2, RET_CHUNK, RET_CHUNK)

    def lanes(a):
        a = a.reshape(npair, 2, RET_CHUNK)
        return np.repeat(a.transpose(0, 2, 1), HEAD_DIM, axis=2)

    bd = np.kron(np.eye(2), np.ones((HEAD_DIM, HEAD_DIM)))
    cd = np.repeat(cdec.reshape(npair, 2), HEAD_DIM, axis=1)[:, :, None] * bd[None]
    f = lambda a: jnp.asarray(a, F32)
    return f(inner), f(lanes(cross)), f(lanes(kdec)), f(cd), f(bd)


def _retention(u3, gn_g, *, q_col, k_col, v_col):
    B, S, _ = u3.shape
    C = RET_CHUNK
    rows = RET_STEP * C
    npair = RET_HEADS // 2
    inner, cross, kdec, cd, bd = _ret_constants()
    full = lambda shape: pl.BlockSpec(shape, lambda b, i: (0,) * len(shape))
    return pl.pallas_call(
        _ret_kernel,
        grid=(B, S // rows),
        in_specs=[
            pl.BlockSpec((1, rows, RET_WIDTH), lambda b, i: (b, i, q_col)),
            pl.BlockSpec((1, rows, RET_WIDTH), lambda b, i: (b, i, k_col)),
            pl.BlockSpec((1, rows, RET_WIDTH), lambda b, i: (b, i, v_col)),
            full((npair, 2, C, C)), full((npair, C, LANES)), full((npair, C, LANES)),
            full((npair, LANES, LANES)), full((LANES, LANES)), full((1, RET_WIDTH)),
        ],
        out_specs=pl.BlockSpec((1, rows, RET_WIDTH), lambda b, i: (b, i, 0)),
        out_shape=jax.ShapeDtypeStruct((B, S, RET_WIDTH), BF16),
        scratch_shapes=[pltpu.VMEM((npair, LANES, LANES), F32)],
        compiler_params=_cp(("parallel", "arbitrary")),
        name="retention",
    )(u3, u3, u3, inner, cross, kdec, cd, bd, gn_g.reshape(1, RET_WIDTH))


def _out0_kernel(of_ref, or_ref, z_ref, x_ref, w_ref, o_ref):
    z = _silu(z_ref[...].astype(F32))
    ya = (of_ref[...].astype(F32) * z[:, :FOX_WIDTH]).astype(BF16)
    yb = (or_ref[...].astype(F32) * z[:, FOX_WIDTH:]).astype(BF16)
    o_ref[...] = x_ref[...] + _dot(ya, w_ref[:FOX_WIDTH, :]) + _dot(yb, w_ref[FOX_WIDTH:, :])


def _out0(o_f, o_r, u, x2, w_out, *, tm=OUT_TM):
    N, D = x2.shape
    return pl.pallas_call(
        _out0_kernel,
        grid=(N // tm,),
        in_specs=[
            pl.BlockSpec((tm, FOX_WIDTH), lambda i: (i, 0)),
            pl.BlockSpec((tm, RET_WIDTH), lambda i: (i, 0)),
            pl.BlockSpec((tm, D), lambda i: (i, 0)),
            pl.BlockSpec((tm, D), lambda i: (i, 0)),
            pl.BlockSpec((D, D), lambda i: (0, 0)),
        ],
        out_specs=pl.BlockSpec((tm, D), lambda i: (i, 0)),
        out_shape=jax.ShapeDtypeStruct((N, D), F32),
        compiler_params=_cp(("parallel",)),
        name="out0",
    )(o_f, o_r, u, x2, w_out)


def _out1_kernel(oc_ref, os_ref, z_ref, x_ref, w_ref, g_ref, o_ref):
    z = _silu(z_ref[...].astype(F32))
    y = ((oc_ref[...].astype(F32) + os_ref[...].astype(F32)) * z).astype(BF16)
    o_ref[...] = _rms(x_ref[...] + _dot(y, w_ref[...]), g_ref[...])


def _out1(o_c, o_s, u, x2, w_out, final_g, *, tm=OUT_TM):
    N, D = x2.shape
    row = pl.BlockSpec((tm, D), lambda i: (i, 0))
    return pl.pallas_call(
        _out1_kernel,
        grid=(N // tm,),
        in_specs=[row, row, row, row,
                  pl.BlockSpec((D, D), lambda i: (0, 0)),
                  pl.BlockSpec((1, D), lambda i: (0, 0))],
        out_specs=row,
        out_shape=jax.ShapeDtypeStruct((N, D), F32),
        compiler_params=_cp(("parallel",)),
        name="out1",
    )(o_c, o_s, u, x2, w_out, final_g.reshape(1, D))


def _compress_kernel(x_ref, pea_ref, peb_ref, wa_ref, wb_ref, w2_ref, *rest, transposed):
    x = x_ref[0].astype(F32)
    a = _dot((x + pea_ref[...]).astype(BF16), wa_ref[0])
    b = _dot((x + peb_ref[...]).astype(BF16), wb_ref[0])
    nseg = x.shape[0]
    pre = a + pltpu.roll(b, nseg - 1, 0)
    hid = _silu(pre).astype(BF16)
    if transposed:
        o_ref, = rest
        o_ref[0, 0] = _dot_nt(w2_ref[...], hid).astype(o_ref.dtype)
    else:
        feat_ref, o_ref = rest
        o_ref[0, 0] = (_dot(hid, w2_ref[...]) + feat_ref[...].astype(F32)).astype(o_ref.dtype)


def _compress(a3, pe, w1, w2, *, transposed):
    B, nseg, wid = a3.shape
    half = CMP_STRIDE * HEAD_DIM
    eye = jnp.eye(NSA_GROUPS, dtype=w1.dtype)

    def big(wh):
        w4 = wh.reshape(CMP_STRIDE, 1, HEAD_DIM, CMP_HIDDEN)
        sel = eye[:, None, :, None, None]
        return (sel * w4[None]).reshape(NSA_GROUPS, wid, CMP_HIDDEN).astype(BF16)

    def pe_big(p):
        return jnp.broadcast_to(p[:, None, :], (CMP_STRIDE, NSA_GROUPS, HEAD_DIM)).reshape(1, wid)

    args = [a3, pe_big(pe[:CMP_STRIDE]), pe_big(pe[CMP_STRIDE:]), big(w1[:half]), big(w1[half:])]
    in_specs = [
        pl.BlockSpec((1, nseg, wid), lambda b, g: (b, 0, 0)),
        pl.BlockSpec((1, wid), lambda b, g: (0, 0)),
        pl.BlockSpec((1, wid), lambda b, g: (0, 0)),
        pl.BlockSpec((1, wid, CMP_HIDDEN), lambda b, g: (g, 0, 0)),
        pl.BlockSpec((1, wid, CMP_HIDDEN), lambda b, g: (g, 0, 0)),
    ]
    if transposed:
        w2d = w2.T.astype(BF16)
        oshape, oblock = (B, NSA_GROUPS, HEAD_DIM, nseg), (1, 1, HEAD_DIM, nseg)
        args.append(w2d)
        in_specs.append(pl.BlockSpec(w2d.shape, lambda b, g: (0, 0)))
    else:
        w2d = jnp.pad(w2, ((0, 0), (0, LANES - HEAD_DIM))).astype(BF16)
        oshape, oblock = (B, NSA_GROUPS, nseg, LANES), (1, 1, nseg, LANES)
        feat = _pos_features(jnp.arange(nseg, dtype=jnp.int32) * CMP_STRIDE + (CMP_BLOCK - 1), LANES)
        args += [w2d, feat]
        in_specs += [pl.BlockSpec(w2d.shape, lambda b, g: (0, 0)),
                     pl.BlockSpec((nseg, LANES), lambda b, g: (0, 0))]
    return pl.pallas_call(
        functools.partial(_compress_kernel, transposed=transposed),
        grid=(B, NSA_GROUPS),
        in_specs=in_specs,
        out_specs=pl.BlockSpec(oblock, lambda b, g: (b, g, 0, 0)),
        out_shape=jax.ShapeDtypeStruct(oshape, BF16),
        compiler_params=_cp(("parallel", "parallel")),
        name="compress",
    )(*args)


def _slope_table():
    s = np.asarray(2.0 ** (-8.0 * (np.arange(NSA_HEADS) + 1) / NSA_HEADS), np.float32)
    sl = np.asarray(s.astype(np.float64) * LOG2E, np.float32)
    p1, p2, p3 = _np_pieces(sl)
    tab = np.zeros((NSA_HEADS, QF_ROWS), np.float32)
    for k, p in enumerate((p1, p1, p2, p2, p3, p3)):
        tab[:, k] = p
    tab[:, 6] = sl
    tab = np.broadcast_to(tab.reshape(NSA_GROUPS, NSA_HPG * QF_ROWS, 1),
                          (NSA_GROUPS, NSA_HPG * QF_ROWS, NSA_TQ))
    return jnp.asarray(tab)


def _pos_features(pos, width):
    pos = pos[:, None]
    lane = jnp.arange(width, dtype=jnp.int32)[None, :] % LANES
    hi = ((pos // SLC_BLOCK) * SLC_BLOCK).astype(F32)
    lo = (pos % SLC_BLOCK).astype(F32)
    k = lane - FEAT0
    f = jnp.where((k >= 0) & (k < 6), jnp.where(k % 2 == 0, hi, lo), 0.0)
    f = jnp.where((k >= 6) & (k < 9), 1.0, f)
    return f.astype(BF16)


def _nsa_queries(qt_ref, tab_ref, t0):
    tq = qt_ref.shape[2]
    r = lax.broadcasted_iota(jnp.int32, (QF_ROWS, tq), 0)
    t = (t0 + lax.broadcasted_iota(jnp.int32, (1, tq), 1)).astype(F32)
    zeros = jnp.zeros((LANES - HEAD_DIM - QF_ROWS, tq), BF16)
    out = []
    for i in range(NSA_HPG):
        tile = tab_ref[0, i * QF_ROWS:(i + 1) * QF_ROWS, :]
        a1, a2, a3 = _pieces(-(tile[6:7, :] * t))
        feat = jnp.where(r == 6, a1, jnp.where(r == 7, a2, jnp.where(r == 8, a3,
                                                                     jnp.where(r < 6, tile, 0.0))))
        out.append(jnp.concatenate([qt_ref[0, i * HEAD_DIM:(i + 1) * HEAD_DIM, :],
                                    feat.astype(BF16), zeros], axis=0))
    return out


def _gates_t(gt_ref, bg_ref, branch):
    gl = gt_ref[0] + bg_ref[...]
    return [_sigmoid(gl[N_BRANCH * i + branch:N_BRANCH * i + branch + 1, :]) for i in range(NSA_HPG)]


def _gated(outs_t, gates):
    return [o * gt for o, gt in zip(outs_t, gates)]


def _store_heads(o_ref, g):
    o_ref[0, :, :LANES] = jnp.concatenate(g[:2], axis=0).T.astype(o_ref.dtype)
    o_ref[0, :, LANES:] = jnp.concatenate(g[2:], axis=0).T.astype(o_ref.dtype)


def _nsa_specs(q_row):
    return dict(
        q=pl.BlockSpec((1, NSA_HPG * HEAD_DIM, NSA_TQ), lambda b, g, i: (b, q_row + g, i)),
        tab=pl.BlockSpec((1, NSA_HPG * QF_ROWS, NSA_TQ), lambda b, g, i: (g, 0, 0)),
        gt=pl.BlockSpec((1, GATE_ROWS, NSA_TQ), lambda b, g, i: (b, g, i)),
        bg=pl.BlockSpec((GATE_ROWS, 1), lambda b, g, i: (g, 0)),
        out=pl.BlockSpec((1, NSA_TQ, NSA_HPG * HEAD_DIM), lambda b, g, i: (b, i, g)),
    )


def _argmax_first(v, idx):
    n = v.shape[0]
    slabs = [(v[r:r + 8], idx[r:r + 8]) for r in range(0, n, 8)]
    while len(slabs) > 1:
        nxt = []
        for (va, ia), (vb, ib) in zip(slabs[0::2], slabs[1::2]):
            right = vb > va
            nxt.append((jnp.where(right, vb, va), jnp.where(right, ib, ia)))
        if len(slabs) % 2:
            nxt.append(slabs[-1])
        slabs = nxt
    v, idx = slabs[0]
    mx = jnp.max(v, axis=0, keepdims=True)
    first = jnp.min(jnp.where(v == mx, idx, float(n)), axis=0, keepdims=True)
    return mx, first


def _cmp_body(nc, t0, qh, kc_ref, vct_ref, mt_ref, grp_ref, gates, o_ref, sel_ref, flag_ref):
    tq = NSA_TQ
    rows = nc * CMP_CHUNK
    full = max(rows - CMP_CHUNK - 8, 0)
    nseg = kc_ref.shape[2]
    kc = kc_ref[0, 0, :rows, :]
    vct = vct_ref[0, 0, :, :rows]
    t = t0 + lax.broadcasted_iota(jnp.int32, (1, tq), 1)
    cidx = full + lax.broadcasted_iota(jnp.int32, (rows - full, 1), 0)
    valid = (cidx * CMP_STRIDE + (CMP_BLOCK - 1) <= t) & (cidx < nseg - 1)
    psum = jnp.zeros((rows, tq), F32)
    ps = []
    s_all = _dot(kc, jnp.concatenate(qh, axis=1))
    for i in range(NSA_HPG):
        s = s_all[:, i * tq:(i + 1) * tq]
        s_last = jnp.where(valid, s[full:], NEG)
        m = jnp.max(s_last, axis=0, keepdims=True)
        if nc > 1:
            m = jnp.maximum(m, jnp.max(s[:full], axis=0, keepdims=True))
        e = jnp.where(valid, jnp.exp2(s_last - m), 0.0)
        if nc > 1:
            e = jnp.concatenate([jnp.exp2(s[:full] - m), e], axis=0)
        l = jnp.sum(e, axis=0, keepdims=True)
        p = e * jnp.where(l > 0.0, 1.0 / l, 0.0)
        psum = psum + p
        ps.append(p.astype(BF16))
    o_all = _dot(vct, jnp.concatenate(ps, axis=1))
    _store_heads(o_ref, _gated([o_all[:, i * tq:(i + 1) * tq] for i in range(NSA_HPG)], gates))
    ns = rows * CMP_STRIDE // SLC_BLOCK
    mt = mt_ref[:ns, :rows]
    imp = sum(_dot(mt, p.astype(BF16)) for p in _pieces(psum))
    blk = lax.broadcasted_iota(jnp.int32, (ns, 1), 0)
    cur = t // SLC_BLOCK
    bvalid = blk * SLC_BLOCK <= t
    forced = (blk == 0) | (blk == cur) | (blk == cur - 1)
    score = jnp.where(forced, -jnp.inf, jnp.where(bvalid, imp, NEG))
    blk_f = jnp.broadcast_to(blk.astype(F32), (ns, tq))
    work = score
    for _ in range(SLC_TOPK - N_FORCED):
        mx, first = _argmax_first(work, blk_f)
        work = jnp.where(blk_f == first, -jnp.inf, work)
    picked = (score > mx) | ((score == mx) & (blk_f <= first))
    selneg = jnp.where(bvalid & (forced | picked), 0.0, -MASK_BIG)
    if ns < NS_PAD:
        selneg = jnp.concatenate([selneg, jnp.full((NS_PAD - ns, tq), -MASK_BIG, F32)], axis=0)
    sel_ref[0, 0] = selneg.astype(sel_ref.dtype)
    picked = jnp.where(selneg == 0.0, 1.0, 0.0).astype(BF16)
    used = _dot_nt(jnp.ones((8, tq), BF16), picked)
    used = jnp.where(used > 0.0, 1.0, 0.0).astype(BF16)
    flag_ref[0] = (_dot(used, grp_ref[...])[0:1] > 0.0).astype(jnp.int32)


def _cmp_kernel(q_ref, kc_ref, vct_ref, mt_ref, grp_ref, tab_ref, gt_ref, bg_ref,
                o_ref, sel_ref, flag_ref):
    t0 = pl.program_id(2) * NSA_TQ
    qh = _nsa_queries(q_ref, tab_ref, t0)
    gates = _gates_t(gt_ref, bg_ref, 0)
    nchunk = kc_ref.shape[2] // CMP_CHUNK
    last = t0 // (CMP_CHUNK * CMP_STRIDE)
    for nc in range(1, nchunk + 1):
        pl.when(last == nc - 1)(functools.partial(
            _cmp_body, nc, t0, qh, kc_ref, vct_ref, mt_ref, grp_ref, gates, o_ref, sel_ref, flag_ref))


def _cmp_to_slc_t(nseg, ns):
    c0 = np.arange(nseg)[:, None] * CMP_STRIDE
    s0 = np.arange(ns)[None, :] * SLC_BLOCK
    overlap = np.clip(np.minimum(c0 + CMP_BLOCK, s0 + SLC_BLOCK) - np.maximum(c0, s0), 0, None)
    m = overlap / CMP_STRIDE
    m[nseg - 1] = 0.0
    mt = np.zeros((NS_PAD, nseg))
    mt[:ns] = m.T
    return jnp.asarray(mt, BF16)


def _tile_groups(ntile):
    per = SLC_TK // SLC_BLOCK
    blk = np.arange(NS_PAD)[:, None]
    g = (blk // per) == np.arange(NS_PAD)[None, :]
    g[:, ntile] = ((blk >= SLC_HEAD // SLC_BLOCK) & (blk < per))[:, 0]
    return jnp.asarray(g, BF16)


def _cmp_attention(ut, kcmp, vcmp_t, gt, bg, *, q_row):
    B, _, S = ut.shape
    nseg = kcmp.shape[2]
    nq = S // NSA_TQ
    sp = _nsa_specs(q_row)
    return pl.pallas_call(
        _cmp_kernel,
        grid=(B, NSA_GROUPS, S // NSA_TQ),
        in_specs=[
            sp["q"],
            pl.BlockSpec((1, 1, nseg, LANES), lambda b, g, i: (b, g, 0, 0)),
            pl.BlockSpec((1, 1, HEAD_DIM, nseg), lambda b, g, i: (b, g, 0, 0)),
            pl.BlockSpec((NS_PAD, nseg), lambda b, g, i: (0, 0)),
            pl.BlockSpec((NS_PAD, NS_PAD), lambda b, g, i: (0, 0)),
            sp["tab"], sp["gt"], sp["bg"],
        ],
        out_specs=[sp["out"], pl.BlockSpec((1, 1, NS_PAD, NSA_TQ), lambda b, g, i: (b, g, 0, i)),
                   pl.BlockSpec((1, 1, NS_PAD), lambda b, g, i: ((b * NSA_GROUPS + g) * nq + i, 0, 0))],
        out_shape=[jax.ShapeDtypeStruct((B, S, NSA_WIDTH), BF16),
                   jax.ShapeDtypeStruct((B, NSA_GROUPS, NS_PAD, S), BF16),
                   jax.ShapeDtypeStruct((B * NSA_GROUPS * nq, 1, NS_PAD), jnp.int32)],
        compiler_params=_cp(("parallel", "parallel", "arbitrary")),
        name="cmp_attention",
    )(ut, kcmp, vcmp_t, _cmp_to_slc_t(nseg, S // SLC_BLOCK), _tile_groups(S // SLC_TK), _slope_table(), gt, bg)


def _window_branch(qh, t0, k_ref, vt_ref, bias_ref, bm_sc, s_bufs, acc_sc, last_before):
    tq, tk = NSA_TQ, WIN_TK
    bm_sc[...] = jnp.concatenate(qh, axis=1)
    _, qk_all, step, finish = _flash_ops(
        bm_sc, s_bufs, None, acc_sc, NSA_HPG, tq, tk,
        lambda k0: k_ref[0, pl.ds(k0, tk), :], lambda h, k0: vt_ref[0, :, pl.ds(k0, tk)])
    first = jnp.maximum(t0 - WINDOW, 0) // tk
    ntile = (WINDOW + tq) // tk

    def run(qcs):
        qk_all(first, 0, qcs[0])
        ms = (jnp.full((1, tq), NEG, F32),) * NSA_HPG
        for j in range(ntile):
            last = j + 1 == ntile
            ms = step(first + j, j % 2, ms, nxt=None if last else first + j + 1,
                      bias=bias_ref[0, j * tk:(j + 1) * tk, :],
                      before=last_before if last else None, qc=qcs[j],
                      nxt_qc=None if last else qcs[j + 1])

    lane = lambda v: max(0, min(tq, v)) // LANES * LANES
    qcs = []
    for j in range(ntile):
        c0, c1 = lane(j * tk - WINDOW + 1), tq - lane(tq - (j + 1) * tk)
        qcs.append(None if (c0, c1) == (0, tq) else (c0, c1))
    if all(q is None for q in qcs):
        run(qcs)
    else:
        lax.cond(t0 >= WINDOW, lambda: run(qcs), lambda: run([None] * ntile))
    return finish()


def _slc_win_kernel(fl_ref, q_ref, k_ref, vt_ref, sel_ref, e_ref, kw_ref, vwt_ref, bias_ref,
                    cb_ref, tab_ref, gt_ref, bg_ref, o_ref, bm_sc, s0_sc, s1_sc, mx_sc, acc_sc,
                    wbm_sc, ws0_sc, ws1_sc, wacc_sc, tiles_sm):
    qi = pl.program_id(2)
    tq, tk = NSA_TQ, SLC_TK
    t0 = qi * tq
    diag = t0 // tk
    row = (pl.program_id(0) * NSA_GROUPS + pl.program_id(1)) * pl.num_programs(2) + qi
    ntile = k_ref.shape[1] // tk
    short0 = (fl_ref[row, ntile] == 0) & (diag > 0)
    n_un = jnp.int32(0)
    for j in range(ntile):
        tiles_sm[n_un] = jnp.int32(j)
        keep = (fl_ref[row, j] > 0) & (j < diag)
        if j == 0:
            keep = keep & jnp.logical_not(short0)
        n_un = n_un + keep.astype(jnp.int32)
    tiles_sm[n_un] = diag
    qh = _nsa_queries(q_ref, tab_ref, t0)
    selneg = sel_ref[0, 0]
    bm_sc[...] = jnp.concatenate([jnp.concatenate([q, selneg], axis=0) for q in qh], axis=1)

    def k_tile(k0):
        return jnp.concatenate([k_ref[0, pl.ds(k0, tk), :], e_ref[pl.ds(k0, tk), :]], axis=1)

    def v_rows(h, k0):
        return vt_ref[0, :, pl.ds(k0, tk)]

    ops = _flash_ops(bm_sc, (s0_sc, s1_sc), mx_sc, acc_sc, NSA_HPG, tq, tk, k_tile, v_rows)
    o_win = _gated(_window_branch(qh, t0, kw_ref, vwt_ref, bias_ref, wbm_sc, (ws0_sc, ws1_sc),
                                  wacc_sc, lambda h: ops[0](tiles_sm[0], 0, h)),
                   _gates_t(gt_ref, bg_ref, 2))

    def short_first(ms):
        def run(ms):
            ops[1](0, 1, nk=SLC_HEAD)
            return ops[2](0, 1, ms, nk=SLC_HEAD)
        return lax.cond(short0, run, lambda ms: ms, ms)

    outs = _flash_t(ops, NSA_HPG, tq, tk, n_un, lambda i: tiles_sm[i], diag, cb_ref,
                    first_done=True, prologue=short_first)
    o_slc = _gated(outs, _gates_t(gt_ref, bg_ref, 1))
    _store_heads(o_ref, [a + b for a, b in zip(o_slc, o_win)])


def _win_bias():
    tq, nk = NSA_TQ, WINDOW + NSA_TQ
    out = []
    for p in range(WINDOW // tq + 1):
        t0 = p * tq
        key = max(t0 - WINDOW, 0) + np.arange(nk)[:, None]
        qpos = t0 + np.arange(tq)[None, :]
        out.append(np.where((key <= qpos) & (key > qpos - WINDOW), 0.0, NEG))
    return jnp.asarray(np.stack(out), F32)


def _block_onehot(S):
    e = (np.arange(S)[:, None] // SLC_BLOCK) == np.arange(NS_PAD)[None, :]
    return jnp.asarray(e, BF16)


def _slc_win_attention(u3, ut, selneg, flags, gt, bg, *, q_row, ks_col, vs_row, kw_col, vw_row):
    B, S, _ = u3.shape
    sp = {k: pl.BlockSpec(v.block_shape, lambda b, g, i, fl, f=v.index_map: f(b, g, i))
          for k, v in _nsa_specs(q_row).items()}
    bias = _win_bias()
    npat = bias.shape[0]
    kspec = lambda col: pl.BlockSpec((1, S, LANES), lambda b, g, i, fl: (b, 0, col + g))
    vspec = lambda row: pl.BlockSpec((1, HEAD_DIM, S), lambda b, g, i, fl: (b, row + g, 0))
    grid_spec = pltpu.PrefetchScalarGridSpec(
        num_scalar_prefetch=1,
        grid=(B, NSA_GROUPS, S // NSA_TQ),
        in_specs=[
            sp["q"], kspec(ks_col), vspec(vs_row),
            pl.BlockSpec((1, 1, NS_PAD, NSA_TQ), lambda b, g, i, fl: (b, g, 0, i)),
            pl.BlockSpec((S, NS_PAD), lambda b, g, i, fl: (0, 0)),
            kspec(kw_col), vspec(vw_row),
            pl.BlockSpec((1,) + bias.shape[1:],
                         lambda b, g, i, fl: (jnp.minimum(i, npat - 1), 0, 0)),
            pl.BlockSpec((SLC_TK, NSA_TQ), lambda b, g, i, fl: (0, 0)),
            sp["tab"], sp["gt"], sp["bg"],
        ],
        out_specs=sp["out"],
        scratch_shapes=_flash_scratch(NSA_HPG, NSA_TQ, SLC_TK, 2 * LANES)
        + _flash_scratch(NSA_HPG, NSA_TQ, WIN_TK, LANES, col_max=False)
        + [pltpu.SMEM((S // SLC_TK + 1,), jnp.int32)],
    )
    return pl.pallas_call(
        _slc_win_kernel,
        grid_spec=grid_spec,
        out_shape=jax.ShapeDtypeStruct((B, S, NSA_WIDTH), BF16),
        compiler_params=_cp(("parallel", "parallel", "arbitrary")),
        name="slc_win_attention",
    )(flags, ut, u3, ut, selneg, _block_onehot(S), u3, ut, bias, _causal_bias(SLC_TK, NSA_TQ),
      _slope_table(), gt, bg)


def _aug_groups(w):
    d = w.shape[0]
    w = w.reshape(d, NSA_GROUPS, HEAD_DIM)
    return jnp.pad(w, ((0, 0), (0, 0), (0, LANES - HEAD_DIM))).reshape(d, NSA_GROUPS * LANES)


def _even_layer(x, norm_g, w_in, b_f, gn_g, w_out):
    B, S, D = x.shape
    qscale = HEAD_DIM ** -0.5 * LOG2E
    q_f, k_f, v_f, w_fl, q_r, k_r, v_r, z = jnp.split(
        w_in, np.cumsum([FOX_WIDTH] * 3 + [FOX_HEADS] + [RET_WIDTH] * 3).tolist(), axis=1)
    w = jnp.concatenate([z, k_f, q_r, k_r * HEAD_DIM ** -0.5, v_r], axis=1).astype(BF16)
    w_t = jnp.concatenate([q_f * qscale, v_f], axis=1).T.astype(BF16)
    x2 = x.reshape(B * S, D)
    u, ut = _proj(x2, norm_g, w, seq=S, w_t=[w_t], t_dtypes=[BF16], tn=2 * PROJ_TN)
    u3 = u.reshape(B, S, -1)
    kfeat, qfeat = _fgate(x, norm_g, w_fl, b_f, tile=min(512, S))
    o_f = _fox(u3, ut, kfeat, qfeat, q_row=0, k_col=D // LANES, v_row=FOX_WIDTH // LANES)
    rb = (D + FOX_WIDTH) // RET_WIDTH
    o_r = _retention(u3, gn_g, q_col=rb, k_col=rb + 1, v_col=rb + 2)
    out = _out0(o_f.reshape(B * S, -1), o_r.reshape(B * S, -1), u, x2, w_out.astype(BF16))
    return out.reshape(B, S, D)


def _odd_layer(x, norm_g, w_in, b_gate, pe_k, pe_v, wk1, wk2, wv1, wv2, w_out, final_g):
    B, S, D = x.shape
    assert S // SLC_BLOCK <= NS_PAD
    qscale = HEAD_DIM ** -0.5 * LOG2E
    sizes = [NSA_WIDTH] + [NSA_KV_WIDTH] * 6 + [NSA_HEADS * N_BRANCH]
    q, kc, vc, ks, vs, kw, vw, gl, z = jnp.split(w_in, np.cumsum(sizes).tolist(), axis=1)
    w = jnp.concatenate([z, kc, vc, _aug_groups(ks), _aug_groups(kw)], axis=1).astype(BF16)
    per_group = NSA_HPG * N_BRANCH
    glt = jnp.pad(gl.T.reshape(NSA_GROUPS, per_group, D), ((0, 0), (0, GATE_ROWS - per_group), (0, 0)))
    glt = glt.reshape(NSA_GROUPS * GATE_ROWS, D).astype(BF16)
    bg = jnp.pad(b_gate.reshape(NSA_GROUPS, per_group), ((0, 0), (0, GATE_ROWS - per_group)))
    bg = bg.reshape(NSA_GROUPS * GATE_ROWS, 1)
    w_vt = jnp.concatenate([q * qscale, vs, vw], axis=1).T.astype(BF16)
    x2 = x.reshape(B * S, D)
    kcol = D + 2 * NSA_KV_WIDTH
    kwid = NSA_GROUPS * LANES
    u, kc_a, vc_a, ut, gt = _proj(
        x2, norm_g, w, seq=S, split_cols=(D, D + NSA_KV_WIDTH), split_width=NSA_KV_WIDTH,
        addend=_pos_features(jnp.arange(S, dtype=jnp.int32), kwid), add_cols=(kcol, kcol + kwid),
        w_t=[w_vt, glt], t_dtypes=[BF16, F32], tn=w.shape[1] // 2)
    u3 = u.reshape(B, S, -1)
    nseg = S // CMP_STRIDE
    kcmp = _compress(kc_a.reshape(B, nseg, -1), pe_k, wk1, wk2, transposed=False)
    vcmp_t = _compress(vc_a.reshape(B, nseg, -1), pe_v, wv1, wv2, transposed=True)
    o_c, selneg, flags = _cmp_attention(ut, kcmp, vcmp_t, gt, bg, q_row=0)
    kb = (D + 2 * NSA_KV_WIDTH) // LANES
    vb = NSA_WIDTH // HEAD_DIM
    o_s = _slc_win_attention(u3, ut, selneg, flags[:, 0, :S // SLC_TK + 1], gt, bg, q_row=0, ks_col=kb,
                             vs_row=vb, kw_col=kb + NSA_GROUPS, vw_row=vb + NSA_GROUPS)
    r = lambda a: a.reshape(B * S, -1)
    out = _out1(r(o_c), r(o_s), u, x2, w_out.astype(BF16), final_g)
    return out.reshape(B, S, D)


def kernel(x, even_norm_g, even_w_in, even_b_f, even_gn_g, even_w_out, odd_norm_g, odd_w_in,
           odd_b_gate, odd_pe_k, odd_pe_v, odd_wk1, odd_wk2, odd_wv1, odd_wv2, odd_w_out, final_g):
    x = _even_layer(x, even_norm_g[0], even_w_in[0], even_b_f[0], even_gn_g[0], even_w_out[0])
    return _odd_layer(x, odd_norm_g[0], odd_w_in[0], odd_b_gate[0], odd_pe_k[0], odd_pe_v[0],
                      odd_wk1[0], odd_wk2[0], odd_wv1[0], odd_wv2[0], odd_w_out[0], final_g)
```

```python
import functools
import math

import jax
import jax.numpy as jnp
import numpy as np
from jax import lax
from jax.experimental import pallas as pl
from jax.experimental.pallas import tpu as pltpu

D_MODEL = 1024
HEAD_DIM = 64
LANES = 128
FOX_HEADS = 8
RET_HEADS = 8
FOX_WIDTH = FOX_HEADS * HEAD_DIM
RET_WIDTH = RET_HEADS * HEAD_DIM
RET_CHUNK = 128
RET_STEP = 8
NSA_HEADS = 16
NSA_GROUPS = 4
NSA_HPG = NSA_HEADS // NSA_GROUPS
NSA_WIDTH = NSA_HEADS * HEAD_DIM
NSA_KV_WIDTH = NSA_GROUPS * HEAD_DIM
N_BRANCH = 3
GATE_ROWS = 16
CMP_BLOCK = 32
CMP_STRIDE = 16
CMP_HIDDEN = 256
CMP_CHUNK = 128
SLC_BLOCK = 64
SLC_TOPK = 16
N_FORCED = 3
NS_PAD = LANES
WINDOW = 512
RMS_EPS = 1e-6
GN_EPS = 1e-5
NEG = -1e30
FORCE_BONUS = 1e6
MASK_BIG = 2.0 ** 100
LOG2E = math.log2(math.e)
FEAT0 = HEAD_DIM
QF_ROWS = 16
ACC_ROWS = HEAD_DIM + 16

PROJ_TM = 1024
PROJ_TN = 512
FOX_TQ = 512
FOX_TK = 512
NSA_TQ = 512
SLC_TK = 512
SLC_HEAD = 128
WIN_TK = 256
OUT_TM = 1024
VMEM_LIMIT = 48 * 1024 * 1024

F32 = jnp.float32
BF16 = jnp.bfloat16


def _cp(sem, vmem=VMEM_LIMIT):
    return pltpu.CompilerParams(dimension_semantics=sem, vmem_limit_bytes=vmem)


def _dot(a, b):
    return jnp.dot(a, b, preferred_element_type=F32)


def _dot_nt(a, b):
    return lax.dot_general(a, b, (((1,), (1,)), ((), ())), preferred_element_type=F32)


def _dot_tn(a, b):
    return lax.dot_general(a, b, (((0,), (0,)), ((), ())), preferred_element_type=F32)


def _rms(x, g):
    return x * lax.rsqrt(jnp.mean(x * x, axis=-1, keepdims=True) + RMS_EPS) * g


def _silu(x):
    return x * (1.0 / (1.0 + jnp.exp(-x)))


def _sigmoid(x):
    return 1.0 / (1.0 + jnp.exp(-x))


def _low_half(shape, axis):
    return lax.broadcasted_iota(jnp.int32, shape, axis) < HEAD_DIM


def _pieces(v):
    p1 = v.astype(BF16).astype(F32)
    r = v - p1
    p2 = r.astype(BF16).astype(F32)
    p3 = (r - p2).astype(BF16).astype(F32)
    return p1, p2, p3


def _np_pieces(v):
    v = np.asarray(v, np.float64)
    bf = lambda a: np.asarray(a, np.float32).astype(BF16).astype(np.float64)
    p1 = bf(v)
    p2 = bf(v - p1)
    p3 = bf(v - p1 - p2)
    return p1, p2, p3


def _fgate_kernel(x_ref, g_ref, wf_ref, b_ref, pk_ref, kc_ref, pq_ref, qc_ref, kf_ref, qf_ref,
                  carry):
    @pl.when(pl.program_id(1) == 0)
    def _():
        carry[...] = jnp.zeros_like(carry)

    h = _rms(x_ref[0], g_ref[...])
    t = h.shape[0]
    h1 = h.astype(BF16)
    h2 = (h - h1.astype(F32)).astype(BF16)
    w = wf_ref[...]
    w1 = w.astype(BF16)
    w2 = (w - w1.astype(F32)).astype(BF16)
    nf = w.shape[1]
    r = _dot(jnp.concatenate([h1, h2], axis=0), jnp.concatenate([w1, w2], axis=1))
    f = r[:t, :nf] + r[:t, nf:] + r[t:, :nf] + b_ref[...]
    ls = jnp.minimum(f, 0.0) - jnp.log(1.0 + jnp.exp(-jnp.abs(f)))
    r = lax.broadcasted_iota(jnp.int32, (t, t), 0)
    c = lax.broadcasted_iota(jnp.int32, (t, t), 1)
    lower = jnp.where(c <= r, 1.0, 0.0).astype(BF16)
    r = _dot(lower, jnp.concatenate([p.astype(BF16) for p in _pieces(ls)], axis=1))
    cs = r[:, :nf] + r[:, nf:2 * nf] + r[:, 2 * nf:] + carry[...]
    carry[...] = cs[t - 1:t, :]
    cl = cs * LOG2E
    p1, p2, p3 = (p.astype(BF16) for p in _pieces(cl))
    kf = _dot(p1, pk_ref[0]) + _dot(p2, pk_ref[1]) + _dot(p3, pk_ref[2]) + kc_ref[...]
    for j in range(FOX_HEADS // 2):
        kf_ref[0, j] = kf[:, j * LANES:(j + 1) * LANES].astype(BF16)
    qf = (_dot_nt(pq_ref[0], p1) + _dot_nt(pq_ref[1], p2) + _dot_nt(pq_ref[2], p3)
          + qc_ref[...])
    qf_ref[0] = qf.astype(BF16).reshape(FOX_HEADS // 2, 2, QF_ROWS, t)


def _fgate_tables():
    npair = FOX_HEADS // 2
    pk = np.zeros((3, FOX_HEADS, npair * LANES), np.float32)
    kc = np.zeros((1, npair * LANES), np.float32)
    pq = np.zeros((3, FOX_HEADS * QF_ROWS, FOX_HEADS), np.float32)
    qc = np.zeros((FOX_HEADS * QF_ROWS, 1), np.float32)
    for h in range(FOX_HEADS):
        j, b = divmod(h, 2)
        for i in range(3):
            pk[i, h, j * LANES + 6 * b + i] = 1.0
            pq[i, h * QF_ROWS + 3 + i, h] = 1.0
            qc[h * QF_ROWS + 6 * b + i, 0] = -1.0
    for j in range(npair):
        kc[0, j * LANES + 3:j * LANES + 6] = 1.0
    return (jnp.asarray(pk, BF16), jnp.asarray(kc, F32), jnp.asarray(pq, BF16), jnp.asarray(qc, F32))


def _fgate(x, g, wf, b_f, *, tile):
    B, S, D = x.shape
    npair = FOX_HEADS // 2
    tables = _fgate_tables()
    return pl.pallas_call(
        _fgate_kernel,
        grid=(B, S // tile),
        in_specs=[
            pl.BlockSpec((1, tile, D), lambda b, s: (b, s, 0)),
            pl.BlockSpec((1, D), lambda b, s: (0, 0)),
            pl.BlockSpec((D, FOX_HEADS), lambda b, s: (0, 0)),
            pl.BlockSpec((1, FOX_HEADS), lambda b, s: (0, 0)),
        ] + [pl.BlockSpec(t.shape, lambda b, s, n=t.ndim: (0,) * n) for t in tables],
        out_specs=[
            pl.BlockSpec((1, npair, tile, LANES), lambda b, s: (b, 0, s, 0)),
            pl.BlockSpec((1, npair, 2, QF_ROWS, tile), lambda b, s: (b, 0, 0, 0, s)),
        ],
        out_shape=[
            jax.ShapeDtypeStruct((B, npair, S, LANES), BF16),
            jax.ShapeDtypeStruct((B, npair, 2, QF_ROWS, S), BF16),
        ],
        scratch_shapes=[pltpu.VMEM((1, FOX_HEADS), F32)],
        compiler_params=_cp(("parallel", "arbitrary")),
        name="fgate",
    )(x, g.reshape(1, D), wf, b_f.reshape(1, FOX_HEADS), *tables)


def _proj_kernel(*refs, tn, ntile, split_cols, add_cols, n_t):
    it = iter(refs)
    x_ref, g_ref, w_ref = next(it), next(it), next(it)
    add_ref = next(it) if add_cols else None
    wt_refs = [next(it) for _ in range(n_t)]
    u_ref = next(it)
    e_refs = [next(it) for _ in split_cols]
    ut_refs = [next(it) for _ in range(n_t)]
    h_sc = next(it)
    j = pl.program_id(1)

    @pl.when(j == 0)
    def _():
        h = _rms(x_ref[...], g_ref[...]).astype(BF16)
        h_sc[...] = h
        for wt_ref, ut_ref in zip(wt_refs, ut_refs):
            ut_ref[0] = _dot_nt(wt_ref[...], h).astype(ut_ref.dtype)

    acc = _dot(h_sc[...], w_ref[...])
    u_ref[...] = acc.astype(u_ref.dtype)

    def extras(t):
        lo = t * tn
        adds = [c - lo for c in add_cols if lo <= c < lo + tn]
        splits = [(e, c - lo) for e, c in zip(e_refs, split_cols) if lo <= c < lo + tn]

        def body():
            for c in adds:
                cols = slice(c, c + add_ref.shape[1])
                u_ref[:, cols] = (acc[:, cols] + add_ref[...].astype(F32)).astype(u_ref.dtype)
            for e_ref, c in splits:
                e_ref[...] = acc[:, c:c + e_ref.shape[1]].astype(e_ref.dtype)
        return body if adds or splits else None

    for t in range(ntile):
        body = extras(t)
        if body is not None:
            pl.when(j == t)(body)


def _proj(x2, g, w, *, seq, split_cols=(), split_width=0, addend=None, add_cols=(), w_t=(),
          t_dtypes=(), tm=PROJ_TM, tn=PROJ_TN):
    N, D = x2.shape
    W = w.shape[1]
    nbs = seq // tm
    in_specs = [
        pl.BlockSpec((tm, D), lambda i, j: (i, 0)),
        pl.BlockSpec((1, D), lambda i, j: (0, 0)),
        pl.BlockSpec((D, tn), lambda i, j: (0, j)),
    ]
    args = [x2, g.reshape(1, D), w]
    if add_cols:
        in_specs.append(pl.BlockSpec((tm, addend.shape[1]), lambda i, j: (i % nbs, 0)))
        args.append(addend)
    out_shape = [jax.ShapeDtypeStruct((N, W), BF16)]
    out_specs = [pl.BlockSpec((tm, tn), lambda i, j: (i, j))]
    for _ in split_cols:
        out_shape.append(jax.ShapeDtypeStruct((N, split_width), BF16))
        out_specs.append(pl.BlockSpec((tm, split_width), lambda i, j: (i, 0)))
    for wt, dt in zip(w_t, t_dtypes):
        rows = wt.shape[0]
        in_specs.append(pl.BlockSpec((rows, D), lambda i, j: (0, 0)))
        args.append(wt)
        out_shape.append(jax.ShapeDtypeStruct((N // seq, rows, seq), dt))
        out_specs.append(pl.BlockSpec((1, rows, tm), lambda i, j: (i // nbs, 0, i % nbs)))
    return pl.pallas_call(
        functools.partial(_proj_kernel, tn=tn, ntile=W // tn, split_cols=tuple(split_cols),
                          add_cols=tuple(add_cols), n_t=len(w_t)),
        grid=(N // tm, W // tn),
        in_specs=in_specs,
        out_specs=out_specs,
        out_shape=out_shape,
        scratch_shapes=[pltpu.VMEM((tm, D), BF16)],
        compiler_params=_cp(("parallel", "arbitrary")),
        name="proj",
    )(*args)


def _flash_ops(bm_sc, s_bufs, mx_sc, acc_sc, nh, tq, tk, k_tile, v_rows):
    acc_sc[...] = jnp.zeros_like(acc_sc)
    ones = jnp.ones((ACC_ROWS - HEAD_DIM, tk), BF16)

    def qk_head(tile, slot, h, qc=None, nk=tk):
        c0, c1 = qc or (0, tq)
        cols = slice(h * tq + c0, h * tq + c1)
        s = _dot(k_tile(pl.multiple_of(tile * tk, tk))[:nk], bm_sc[:, cols])
        s_bufs[slot][:nk, cols] = s
        if mx_sc is not None:
            mx_sc[slot, :, cols] = jnp.max(s, axis=0, keepdims=True)

    def soft_head(tile, slot, m_old, h, valid, bias, qc, kr):
        c0, c1 = qc or (0, tq)
        r0, r1 = kr
        cols = slice(h * tq + c0, h * tq + c1)
        buf = s_bufs[slot]
        k0 = pl.multiple_of(tile * tk, tk)
        m_sub = m_old[:, c0:c1]
        if valid is not None or bias is not None:
            s = (buf[r0:r1, cols] + bias[r0:r1, c0:c1] if valid is None
                 else jnp.where(valid[r0:r1, c0:c1], buf[r0:r1, cols], NEG))
            m_new = jnp.maximum(m_sub, jnp.max(s, axis=0, keepdims=True))
            p = jnp.exp2(s - m_new).astype(BF16)
        else:
            m_new = jnp.maximum(m_sub, mx_sc[slot, :, cols])
            p = jnp.exp2(buf[r0:r1, cols] - m_new).astype(BF16)
        alpha = jnp.exp2(m_sub - m_new)
        lhs = jnp.concatenate([v_rows(h, k0)[:, r0:r1], ones[:, :r1 - r0]], axis=0)
        acc_sc[h, :, c0:c1] = alpha * acc_sc[h, :, c0:c1] + _dot(lhs, p)
        parts = [m_new]
        if c0 > 0:
            parts.insert(0, m_old[:, :c0])
        if c1 < tq:
            parts.append(m_old[:, c1:])
        return jnp.concatenate(parts, axis=1) if len(parts) > 1 else m_new

    def qk_all(tile, slot, qc=None, nk=tk):
        for h in range(nh):
            qk_head(tile, slot, h, qc, nk)

    def step(cur, slot, ms, nxt=None, valid=None, bias=None, before=None, qc=None, nxt_qc=None,
             nk=tk, parts=None):
        out = []
        for h in range(nh):
            if nxt is not None:
                qk_head(nxt, 1 - slot, h, nxt_qc)
            if before is not None:
                before(h)
            m = ms[h]
            for kr, qcp in parts or [((0, nk), qc)]:
                m = soft_head(cur, slot, m, h, valid, bias, qcp, kr)
            out.append(m)
        return tuple(out)

    def finish():
        outs = []
        for h in range(nh):
            a = acc_sc[h]
            outs.append(a[:HEAD_DIM] * (1.0 / a[HEAD_DIM:HEAD_DIM + 1]))
        return outs

    return qk_head, qk_all, step, finish


def _flash_t(ops, nh, tq, tk, n_un, tile_of, diag, diag_bias, first_done=False, prologue=None):
    _, qk_all, step, finish = ops
    assert tq == tk
    half = tk // 2
    diag_parts = [((0, half), None), ((half, tk), (half, tq))]

    def pair(i, ms):
        t_a, t_b, t_c = tile_of(2 * i), tile_of(2 * i + 1), tile_of(2 * i + 2)
        return step(t_b, 1, step(t_a, 0, ms, nxt=t_b), nxt=t_c)

    def odd_tail(ms):
        return step(diag, 1, step(tile_of(n_un - 1), 0, ms, nxt=diag), bias=diag_bias[...],
                    parts=diag_parts)

    def even_tail(ms):
        return step(diag, 0, ms, bias=diag_bias[...], parts=diag_parts)

    if not first_done:
        qk_all(tile_of(0), 0)
    ms = (jnp.full((1, tq), NEG, F32),) * nh
    if prologue is not None:
        ms = prologue(ms)
    ms = lax.fori_loop(0, n_un // 2, pair, ms)
    lax.cond(n_un % 2 == 1, odd_tail, even_tail, ms)
    return finish()


def _causal_bias(tk, tq):
    assert tk == tq
    return jnp.asarray(np.where(np.arange(tk)[:, None] <= np.arange(tq)[None, :], 0.0, NEG), F32)


def _flash_scratch(nh, tq, tk, kdim, col_max=True):
    bufs = [pltpu.VMEM((kdim, nh * tq), BF16), pltpu.VMEM((tk, nh * tq), F32),
            pltpu.VMEM((tk, nh * tq), F32)]
    if col_max:
        bufs.append(pltpu.VMEM((2, 1, nh * tq), F32))
    return bufs + [pltpu.VMEM((nh, ACC_ROWS, tq), F32)]


def _fox_kernel(qt_ref, qf_ref, k_ref, kf_ref, vt_ref, cb_ref, o_ref, bm_sc, s0_sc, s1_sc, mx_sc,
                acc_sc):
    qi = pl.program_id(2)
    tq, tk = FOX_TQ, FOX_TK
    bm_sc[...] = jnp.zeros_like(bm_sc)
    for h in range(2):
        rows = slice(h * HEAD_DIM, (h + 1) * HEAD_DIM)
        bm_sc[rows, h * tq:(h + 1) * tq] = qt_ref[0, rows, :]
        bm_sc[LANES:LANES + QF_ROWS, h * tq:(h + 1) * tq] = qf_ref[0, 0, h]

    def k_tile(k0):
        return jnp.concatenate([k_ref[0, pl.ds(k0, tk), :], kf_ref[0, 0, pl.ds(k0, tk), :]], axis=1)

    def v_rows(h, k0):
        return vt_ref[0, h * HEAD_DIM:(h + 1) * HEAD_DIM, pl.ds(k0, tk)]

    diag = (qi * tq) // tk
    ops = _flash_ops(bm_sc, (s0_sc, s1_sc), mx_sc, acc_sc, 2, tq, tk, k_tile, v_rows)
    outs = _flash_t(ops, 2, tq, tk, diag, lambda i: i, diag, cb_ref)
    o_ref[0] = jnp.concatenate(outs, axis=0).T.astype(o_ref.dtype)


def _fox(u3, ut, kfeat, qfeat, *, q_row, k_col, v_row):
    B, S, _ = u3.shape
    npair = FOX_HEADS // 2
    return pl.pallas_call(
        _fox_kernel,
        grid=(B, npair, S // FOX_TQ),
        in_specs=[
            pl.BlockSpec((1, LANES, FOX_TQ), lambda b, j, i: (b, q_row + j, i)),
            pl.BlockSpec((1, 1, 2, QF_ROWS, FOX_TQ), lambda b, j, i: (b, j, 0, 0, i)),
            pl.BlockSpec((1, S, LANES), lambda b, j, i: (b, 0, k_col + j)),
            pl.BlockSpec((1, 1, S, LANES), lambda b, j, i: (b, j, 0, 0)),
            pl.BlockSpec((1, LANES, S), lambda b, j, i: (b, v_row + j, 0)),
            pl.BlockSpec((FOX_TK, FOX_TQ), lambda b, j, i: (0, 0)),
        ],
        out_specs=pl.BlockSpec((1, FOX_TQ, LANES), lambda b, j, i: (b, i, j)),
        out_shape=jax.ShapeDtypeStruct((B, S, FOX_WIDTH), BF16),
        scratch_shapes=_flash_scratch(2, FOX_TQ, FOX_TK, 2 * LANES),
        compiler_params=_cp(("parallel", "parallel", "arbitrary")),
        name="fox",
    )(ut, qfeat, u3, kfeat, ut, _causal_bias(FOX_TK, FOX_TQ))


def _ret_kernel(q_ref, k_ref, v_ref, inner_ref, cross_ref, kdec_ref, cd_ref, bd_ref, gn_ref,
                o_ref, state_sc):
    @pl.when(pl.program_id(1) == 0)
    def _():
        state_sc[...] = jnp.zeros_like(state_sc)

    low = _low_half((RET_CHUNK, LANES), 1)
    inv = 1.0 / HEAD_DIM
    for c, j in [(c, j) for c in range(RET_STEP) for j in range(RET_HEADS // 2)]:
        rows = slice(c * RET_CHUNK, (c + 1) * RET_CHUNK)
        cols = slice(j * LANES, (j + 1) * LANES)
        q, k, v = q_ref[0, rows, cols], k_ref[0, rows, cols], v_ref[0, rows, cols]
        zero = jnp.zeros_like(q)
        qa, qb = jnp.where(low, q, zero), jnp.where(low, zero, q)
        pa = (_dot_nt(qa, k) * inner_ref[j, 0]).astype(BF16)
        pb = (_dot_nt(qb, k) * inner_ref[j, 1]).astype(BF16)
        o_in = jnp.where(low, _dot(pa, v), _dot(pb, v))
        state = state_sc[j]
        o = o_in + _dot(q, state.astype(BF16)) * cross_ref[j]
        kd = (k.astype(F32) * kdec_ref[j]).astype(BF16)
        state_sc[j] = state * cd_ref[j] + _dot_tn(kd, v) * bd_ref[...]
        sa = jnp.sum(jnp.where(low, o, 0.0), axis=-1, keepdims=True)
        st = jnp.sum(o, axis=-1, keepdims=True)
        mu = jnp.where(low, sa, st - sa) * inv
        d = o - mu
        d2 = d * d
        va = jnp.sum(jnp.where(low, d2, 0.0), axis=-1, keepdims=True)
        vt = jnp.sum(d2, axis=-1, keepdims=True)
        var = jnp.where(low, va, vt - va) * inv
        o_ref[0, rows, cols] = (d * lax.rsqrt(var + GN_EPS) * gn_ref[:, cols]).astype(o_ref.dtype)


def _ret_constants():
    lg = np.log(1.0 - 2.0 ** (-5.0 - np.arange(RET_HEADS)))
    i = np.arange(RET_CHUNK)
    diff = i[:, None] - i[None, :]
    inner = np.where(diff[None] >= 0, np.exp(lg[:, None, None] * np.maximum(diff, 0)[None]), 0.0)
    cross = np.exp(lg[:, None] * (i[None, :] + 1))
    kdec = np.exp(lg[:, None] * (RET_CHUNK - 1 - i)[None, :])
    cdec = np.exp(lg * RET_CHUNK)
    npair = RET_HEADS // 2
    inner = inner.reshape(npair, 2, RET_CHUNK, RET_CHUNK)

    def lanes(a):
        a = a.reshape(npair, 2, RET_CHUNK)
        return np.repeat(a.transpose(0, 2, 1), HEAD_DIM, axis=2)

    bd = np.kron(np.eye(2), np.ones((HEAD_DIM, HEAD_DIM)))
    cd = np.repeat(cdec.reshape(npair, 2), HEAD_DIM, axis=1)[:, :, None] * bd[None]
    f = lambda a: jnp.asarray(a, F32)
    return f(inner), f(lanes(cross)), f(lanes(kdec)), f(cd), f(bd)


def _retention(u3, gn_g, *, q_col, k_col, v_col):
    B, S, _ = u3.shape
    C = RET_CHUNK
    rows = RET_STEP * C
    npair = RET_HEADS // 2
    inner, cross, kdec, cd, bd = _ret_constants()
    full = lambda shape: pl.BlockSpec(shape, lambda b, i: (0,) * len(shape))
    return pl.pallas_call(
        _ret_kernel,
        grid=(B, S // rows),
        in_specs=[
            pl.BlockSpec((1, rows, RET_WIDTH), lambda b, i: (b, i, q_col)),
            pl.BlockSpec((1, rows, RET_WIDTH), lambda b, i: (b, i, k_col)),
            pl.BlockSpec((1, rows, RET_WIDTH), lambda b, i: (b, i, v_col)),
            full((npair, 2, C, C)), full((npair, C, LANES)), full((npair, C, LANES)),
            full((npair, LANES, LANES)), full((LANES, LANES)), full((1, RET_WIDTH)),
        ],
        out_specs=pl.BlockSpec((1, rows, RET_WIDTH), lambda b, i: (b, i, 0)),
        out_shape=jax.ShapeDtypeStruct((B, S, RET_WIDTH), BF16),
        scratch_shapes=[pltpu.VMEM((npair, LANES, LANES), F32)],
        compiler_params=_cp(("parallel", "arbitrary")),
        name="retention",
    )(u3, u3, u3, inner, cross, kdec, cd, bd, gn_g.reshape(1, RET_WIDTH))


def _out0_kernel(of_ref, or_ref, z_ref, x_ref, w_ref, o_ref):
    z = _silu(z_ref[...].astype(F32))
    ya = (of_ref[...].astype(F32) * z[:, :FOX_WIDTH]).astype(BF16)
    yb = (or_ref[...].astype(F32) * z[:, FOX_WIDTH:]).astype(BF16)
    o_ref[...] = x_ref[...] + _dot(ya, w_ref[:FOX_WIDTH, :]) + _dot(yb, w_ref[FOX_WIDTH:, :])


def _out0(o_f, o_r, u, x2, w_out, *, tm=OUT_TM):
    N, D = x2.shape
    return pl.pallas_call(
        _out0_kernel,
        grid=(N // tm,),
        in_specs=[
            pl.BlockSpec((tm, FOX_WIDTH), lambda i: (i, 0)),
            pl.BlockSpec((tm, RET_WIDTH), lambda i: (i, 0)),
            pl.BlockSpec((tm, D), lambda i: (i, 0)),
            pl.BlockSpec((tm, D), lambda i: (i, 0)),
            pl.BlockSpec((D, D), lambda i: (0, 0)),
        ],
        out_specs=pl.BlockSpec((tm, D), lambda i: (i, 0)),
        out_shape=jax.ShapeDtypeStruct((N, D), F32),
        compiler_params=_cp(("parallel",)),
        name="out0",
    )(o_f, o_r, u, x2, w_out)


def _out1_kernel(oc_ref, os_ref, z_ref, x_ref, w_ref, g_ref, o_ref):
    z = _silu(z_ref[...].astype(F32))
    y = ((oc_ref[...].astype(F32) + os_ref[...].astype(F32)) * z).astype(BF16)
    o_ref[...] = _rms(x_ref[...] + _dot(y, w_ref[...]), g_ref[...])


def _out1(o_c, o_s, u, x2, w_out, final_g, *, tm=OUT_TM):
    N, D = x2.shape
    row = pl.BlockSpec((tm, D), lambda i: (i, 0))
    return pl.pallas_call(
        _out1_kernel,
        grid=(N // tm,),
        in_specs=[row, row, row, row,
                  pl.BlockSpec((D, D), lambda i: (0, 0)),
                  pl.BlockSpec((1, D), lambda i: (0, 0))],
        out_specs=row,
        out_shape=jax.ShapeDtypeStruct((N, D), F32),
        compiler_params=_cp(("parallel",)),
        name="out1",
    )(o_c, o_s, u, x2, w_out, final_g.reshape(1, D))


def _compress_kernel(x_ref, pea_ref, peb_ref, wa_ref, wb_ref, w2_ref, *rest, transposed):
    x = x_ref[0].astype(F32)
    a = _dot((x + pea_ref[...]).astype(BF16), wa_ref[0])
    b = _dot((x + peb_ref[...]).astype(BF16), wb_ref[0])
    nseg = x.shape[0]
    pre = a + pltpu.roll(b, nseg - 1, 0)
    hid = _silu(pre).astype(BF16)
    if transposed:
        o_ref, = rest
        o_ref[0, 0] = _dot_nt(w2_ref[...], hid).astype(o_ref.dtype)
    else:
        feat_ref, o_ref = rest
        o_ref[0, 0] = (_dot(hid, w2_ref[...]) + feat_ref[...].astype(F32)).astype(o_ref.dtype)


def _compress(a3, pe, w1, w2, *, transposed):
    B, nseg, wid = a3.shape
    half = CMP_STRIDE * HEAD_DIM
    eye = jnp.eye(NSA_GROUPS, dtype=w1.dtype)

    def big(wh):
        w4 = wh.reshape(CMP_STRIDE, 1, HEAD_DIM, CMP_HIDDEN)
        sel = eye[:, None, :, None, None]
        return (sel * w4[None]).reshape(NSA_GROUPS, wid, CMP_HIDDEN).astype(BF16)

    def pe_big(p):
        return jnp.broadcast_to(p[:, None, :], (CMP_STRIDE, NSA_GROUPS, HEAD_DIM)).reshape(1, wid)

    args = [a3, pe_big(pe[:CMP_STRIDE]), pe_big(pe[CMP_STRIDE:]), big(w1[:half]), big(w1[half:])]
    in_specs = [
        pl.BlockSpec((1, nseg, wid), lambda b, g: (b, 0, 0)),
        pl.BlockSpec((1, wid), lambda b, g: (0, 0)),
        pl.BlockSpec((1, wid), lambda b, g: (0, 0)),
        pl.BlockSpec((1, wid, CMP_HIDDEN), lambda b, g: (g, 0, 0)),
        pl.BlockSpec((1, wid, CMP_HIDDEN), lambda b, g: (g, 0, 0)),
    ]
    if transposed:
        w2d = w2.T.astype(BF16)
        oshape, oblock = (B, NSA_GROUPS, HEAD_DIM, nseg), (1, 1, HEAD_DIM, nseg)
        args.append(w2d)
        in_specs.append(pl.BlockSpec(w2d.shape, lambda b, g: (0, 0)))
    else:
        w2d = jnp.pad(w2, ((0, 0), (0, LANES - HEAD_DIM))).astype(BF16)
        oshape, oblock = (B, NSA_GROUPS, nseg, LANES), (1, 1, nseg, LANES)
        feat = _pos_features(jnp.arange(nseg, dtype=jnp.int32) * CMP_STRIDE + (CMP_BLOCK - 1), LANES)
        args += [w2d, feat]
        in_specs += [pl.BlockSpec(w2d.shape, lambda b, g: (0, 0)),
                     pl.BlockSpec((nseg, LANES), lambda b, g: (0, 0))]
    return pl.pallas_call(
        functools.partial(_compress_kernel, transposed=transposed),
        grid=(B, NSA_GROUPS),
        in_specs=in_specs,
        out_specs=pl.BlockSpec(oblock, lambda b, g: (b, g, 0, 0)),
        out_shape=jax.ShapeDtypeStruct(oshape, BF16),
        compiler_params=_cp(("parallel", "parallel")),
        name="compress",
    )(*args)


def _slope_table():
    s = np.asarray(2.0 ** (-8.0 * (np.arange(NSA_HEADS) + 1) / NSA_HEADS), np.float32)
    sl = np.asarray(s.astype(np.float64) * LOG2E, np.float32)
    p1, p2, p3 = _np_pieces(sl)
    tab = np.zeros((NSA_HEADS, QF_ROWS), np.float32)
    for k, p in enumerate((p1, p1, p2, p2, p3, p3)):
        tab[:, k] = p
    tab[:, 6] = sl
    tab = np.broadcast_to(tab.reshape(NSA_GROUPS, NSA_HPG * QF_ROWS, 1),
                          (NSA_GROUPS, NSA_HPG * QF_ROWS, NSA_TQ))
    return jnp.asarray(tab)


def _pos_features(pos, width):
    pos = pos[:, None]
    lane = jnp.arange(width, dtype=jnp.int32)[None, :] % LANES
    hi = ((pos // SLC_BLOCK) * SLC_BLOCK).astype(F32)
    lo = (pos % SLC_BLOCK).astype(F32)
    k = lane - FEAT0
    f = jnp.where((k >= 0) & (k < 6), jnp.where(k % 2 == 0, hi, lo), 0.0)
    f = jnp.where((k >= 6) & (k < 9), 1.0, f)
    return f.astype(BF16)


def _nsa_queries(qt_ref, tab_ref, t0):
    tq = qt_ref.shape[2]
    r = lax.broadcasted_iota(jnp.int32, (QF_ROWS, tq), 0)
    t = (t0 + lax.broadcasted_iota(jnp.int32, (1, tq), 1)).astype(F32)
    zeros = jnp.zeros((LANES - HEAD_DIM - QF_ROWS, tq), BF16)
    out = []
    for i in range(NSA_HPG):
        tile = tab_ref[0, i * QF_ROWS:(i + 1) * QF_ROWS, :]
        a1, a2, a3 = _pieces(-(tile[6:7, :] * t))
        feat = jnp.where(r == 6, a1, jnp.where(r == 7, a2, jnp.where(r == 8, a3,
                                                                     jnp.where(r < 6, tile, 0.0))))
        out.append(jnp.concatenate([qt_ref[0, i * HEAD_DIM:(i + 1) * HEAD_DIM, :],
                                    feat.astype(BF16), zeros], axis=0))
    return out


def _gates_t(gt_ref, bg_ref, branch):
    gl = gt_ref[0] + bg_ref[...]
    return [_sigmoid(gl[N_BRANCH * i + branch:N_BRANCH * i + branch + 1, :]) for i in range(NSA_HPG)]


def _gated(outs_t, gates):
    return [o * gt for o, gt in zip(outs_t, gates)]


def _store_heads(o_ref, g):
    o_ref[0, :, :LANES] = jnp.concatenate(g[:2], axis=0).T.astype(o_ref.dtype)
    o_ref[0, :, LANES:] = jnp.concatenate(g[2:], axis=0).T.astype(o_ref.dtype)


def _nsa_specs(q_row):
    return dict(
        q=pl.BlockSpec((1, NSA_HPG * HEAD_DIM, NSA_TQ), lambda b, g, i: (b, q_row + g, i)),
        tab=pl.BlockSpec((1, NSA_HPG * QF_ROWS, NSA_TQ), lambda b, g, i: (g, 0, 0)),
        gt=pl.BlockSpec((1, GATE_ROWS, NSA_TQ), lambda b, g, i: (b, g, i)),
        bg=pl.BlockSpec((GATE_ROWS, 1), lambda b, g, i: (g, 0)),
        out=pl.BlockSpec((1, NSA_TQ, NSA_HPG * HEAD_DIM), lambda b, g, i: (b, i, g)),
    )


def _argmax_first(v, idx):
    n = v.shape[0]
    slabs = [(v[r:r + 8], idx[r:r + 8]) for r in range(0, n, 8)]
    while len(slabs) > 1:
        nxt = []
        for (va, ia), (vb, ib) in zip(slabs[0::2], slabs[1::2]):
            right = vb > va
            nxt.append((jnp.where(right, vb, va), jnp.where(right, ib, ia)))
        if len(slabs) % 2:
            nxt.append(slabs[-1])
        slabs = nxt
    v, idx = slabs[0]
    mx = jnp.max(v, axis=0, keepdims=True)
    first = jnp.min(jnp.where(v == mx, idx, float(n)), axis=0, keepdims=True)
    return mx, first


def _cmp_body(nc, t0, qh, kc_ref, vct_ref, mt_ref, grp_ref, gates, o_ref, sel_ref, flag_ref):
    tq = NSA_TQ
    rows = nc * CMP_CHUNK
    full = max(rows - CMP_CHUNK - 8, 0)
    nseg = kc_ref.shape[2]
    kc = kc_ref[0, 0, :rows, :]
    vct = vct_ref[0, 0, :, :rows]
    t = t0 + lax.broadcasted_iota(jnp.int32, (1, tq), 1)
    cidx = full + lax.broadcasted_iota(jnp.int32, (rows - full, 1), 0)
    valid = (cidx * CMP_STRIDE + (CMP_BLOCK - 1) <= t) & (cidx < nseg - 1)
    psum = jnp.zeros((rows, tq), F32)
    ps = []
    s_all = _dot(kc, jnp.concatenate(qh, axis=1))
    for i in range(NSA_HPG):
        s = s_all[:, i * tq:(i + 1) * tq]
        s_last = jnp.where(valid, s[full:], NEG)
        m = jnp.max(s_last, axis=0, keepdims=True)
        if nc > 1:
            m = jnp.maximum(m, jnp.max(s[:full], axis=0, keepdims=True))
        e = jnp.where(valid, jnp.exp2(s_last - m), 0.0)
        if nc > 1:
            e = jnp.concatenate([jnp.exp2(s[:full] - m), e], axis=0)
        l = jnp.sum(e, axis=0, keepdims=True)
        p = e * jnp.where(l > 0.0, 1.0 / l, 0.0)
        psum = psum + p
        ps.append(p.astype(BF16))
    o_all = _dot(vct, jnp.concatenate(ps, axis=1))
    _store_heads(o_ref, _gated([o_all[:, i * tq:(i + 1) * tq] for i in range(NSA_HPG)], gates))
    ns = rows * CMP_STRIDE // SLC_BLOCK
    mt = mt_ref[:ns, :rows]
    imp = sum(_dot(mt, p.astype(BF16)) for p in _pieces(psum))
    blk = lax.broadcasted_iota(jnp.int32, (ns, 1), 0)
    cur = t // SLC_BLOCK
    bvalid = blk * SLC_BLOCK <= t
    forced = (blk == 0) | (blk == cur) | (blk == cur - 1)
    score = jnp.where(forced, -jnp.inf, jnp.where(bvalid, imp, NEG))
    blk_f = jnp.broadcast_to(blk.astype(F32), (ns, tq))
    work = score
    for _ in range(SLC_TOPK - N_FORCED):
        mx, first = _argmax_first(work, blk_f)
        work = jnp.where(blk_f == first, -jnp.inf, work)
    picked = (score > mx) | ((score == mx) & (blk_f <= first))
    selneg = jnp.where(bvalid & (forced | picked), 0.0, -MASK_BIG)
    if ns < NS_PAD:
        selneg = jnp.concatenate([selneg, jnp.full((NS_PAD - ns, tq), -MASK_BIG, F32)], axis=0)
    sel_ref[0, 0] = selneg.astype(sel_ref.dtype)
    picked = jnp.where(selneg == 0.0, 1.0, 0.0).astype(BF16)
    used = _dot_nt(jnp.ones((8, tq), BF16), picked)
    used = jnp.where(used > 0.0, 1.0, 0.0).astype(BF16)
    flag_ref[0] = (_dot(used, grp_ref[...])[0:1] > 0.0).astype(jnp.int32)


def _cmp_kernel(q_ref, kc_ref, vct_ref, mt_ref, grp_ref, tab_ref, gt_ref, bg_ref,
                o_ref, sel_ref, flag_ref):
    t0 = pl.program_id(2) * NSA_TQ
    qh = _nsa_queries(q_ref, tab_ref, t0)
    gates = _gates_t(gt_ref, bg_ref, 0)
    nchunk = kc_ref.shape[2] // CMP_CHUNK
    last = t0 // (CMP_CHUNK * CMP_STRIDE)
    for nc in range(1, nchunk + 1):
        pl.when(last == nc - 1)(functools.partial(
            _cmp_body, nc, t0, qh, kc_ref, vct_ref, mt_ref, grp_ref, gates, o_ref, sel_ref, flag_ref))


def _cmp_to_slc_t(nseg, ns):
    c0 = np.arange(nseg)[:, None] * CMP_STRIDE
    s0 = np.arange(ns)[None, :] * SLC_BLOCK
    overlap = np.clip(np.minimum(c0 + CMP_BLOCK, s0 + SLC_BLOCK) - np.maximum(c0, s0), 0, None)
    m = overlap / CMP_STRIDE
    m[nseg - 1] = 0.0
    mt = np.zeros((NS_PAD, nseg))
    mt[:ns] = m.T
    return jnp.asarray(mt, BF16)


def _tile_groups(ntile):
    per = SLC_TK // SLC_BLOCK
    blk = np.arange(NS_PAD)[:, None]
    g = (blk // per) == np.arange(NS_PAD)[None, :]
    g[:, ntile] = ((blk >= SLC_HEAD // SLC_BLOCK) & (blk < per))[:, 0]
    return jnp.asarray(g, BF16)


def _cmp_attention(ut, kcmp, vcmp_t, gt, bg, *, q_row):
    B, _, S = ut.shape
    nseg = kcmp.shape[2]
    nq = S // NSA_TQ
    sp = _nsa_specs(q_row)
    return pl.pallas_call(
        _cmp_kernel,
        grid=(B, NSA_GROUPS, S // NSA_TQ),
        in_specs=[
            sp["q"],
            pl.BlockSpec((1, 1, nseg, LANES), lambda b, g, i: (b, g, 0, 0)),
            pl.BlockSpec((1, 1, HEAD_DIM, nseg), lambda b, g, i: (b, g, 0, 0)),
            pl.BlockSpec((NS_PAD, nseg), lambda b, g, i: (0, 0)),
            pl.BlockSpec((NS_PAD, NS_PAD), lambda b, g, i: (0, 0)),
            sp["tab"], sp["gt"], sp["bg"],
        ],
        out_specs=[sp["out"], pl.BlockSpec((1, 1, NS_PAD, NSA_TQ), lambda b, g, i: (b, g, 0, i)),
                   pl.BlockSpec((1, 1, NS_PAD), lambda b, g, i: ((b * NSA_GROUPS + g) * nq + i, 0, 0))],
        out_shape=[jax.ShapeDtypeStruct((B, S, NSA_WIDTH), BF16),
                   jax.ShapeDtypeStruct((B, NSA_GROUPS, NS_PAD, S), BF16),
                   jax.ShapeDtypeStruct((B * NSA_GROUPS * nq, 1, NS_PAD), jnp.int32)],
        compiler_params=_cp(("parallel", "parallel", "arbitrary")),
        name="cmp_attention",
    )(ut, kcmp, vcmp_t, _cmp_to_slc_t(nseg, S // SLC_BLOCK), _tile_groups(S // SLC_TK), _slope_table(), gt, bg)


def _window_branch(qh, t0, k_ref, vt_ref, bias_ref, bm_sc, s_bufs, acc_sc, last_before):
    tq, tk = NSA_TQ, WIN_TK
    bm_sc[...] = jnp.concatenate(qh, axis=1)
    _, qk_all, step, finish = _flash_ops(
        bm_sc, s_bufs, None, acc_sc, NSA_HPG, tq, tk,
        lambda k0: k_ref[0, pl.ds(k0, tk), :], lambda h, k0: vt_ref[0, :, pl.ds(k0, tk)])
    first = jnp.maximum(t0 - WINDOW, 0) // tk
    ntile = (WINDOW + tq) // tk

    def run(qcs):
        qk_all(first, 0, qcs[0])
        ms = (jnp.full((1, tq), NEG, F32),) * NSA_HPG
        for j in range(ntile):
            last = j + 1 == ntile
            ms = step(first + j, j % 2, ms, nxt=None if last else first + j + 1,
                      bias=bias_ref[0, j * tk:(j + 1) * tk, :],
                      before=last_before if last else None, qc=qcs[j],
                      nxt_qc=None if last else qcs[j + 1])

    lane = lambda v: max(0, min(tq, v)) // LANES * LANES
    qcs = []
    for j in range(ntile):
        c0, c1 = lane(j * tk - WINDOW + 1), tq - lane(tq - (j + 1) * tk)
        qcs.append(None if (c0, c1) == (0, tq) else (c0, c1))
    if all(q is None for q in qcs):
        run(qcs)
    else:
        lax.cond(t0 >= WINDOW, lambda: run(qcs), lambda: run([None] * ntile))
    return finish()


def _slc_win_kernel(fl_ref, q_ref, k_ref, vt_ref, sel_ref, e_ref, kw_ref, vwt_ref, bias_ref,
                    cb_ref, tab_ref, gt_ref, bg_ref, o_ref, bm_sc, s0_sc, s1_sc, mx_sc, acc_sc,
                    wbm_sc, ws0_sc, ws1_sc, wacc_sc, tiles_sm):
    qi = pl.program_id(2)
    tq, tk = NSA_TQ, SLC_TK
    t0 = qi * tq
    diag = t0 // tk
    row = (pl.program_id(0) * NSA_GROUPS + pl.program_id(1)) * pl.num_programs(2) + qi
    ntile = k_ref.shape[1] // tk
    short0 = (fl_ref[row, ntile] == 0) & (diag > 0)
    n_un = jnp.int32(0)
    for j in range(ntile):
        tiles_sm[n_un] = jnp.int32(j)
        keep = (fl_ref[row, j] > 0) & (j < diag)
        if j == 0:
            keep = keep & jnp.logical_not(short0)
        n_un = n_un + keep.astype(jnp.int32)
    tiles_sm[n_un] = diag
    qh = _nsa_queries(q_ref, tab_ref, t0)
    selneg = sel_ref[0, 0]
    bm_sc[...] = jnp.concatenate([jnp.concatenate([q, selneg], axis=0) for q in qh], axis=1)

    def k_tile(k0):
        return jnp.concatenate([k_ref[0, pl.ds(k0, tk), :], e_ref[pl.ds(k0, tk), :]], axis=1)

    def v_rows(h, k0):
        return vt_ref[0, :, pl.ds(k0, tk)]

    ops = _flash_ops(bm_sc, (s0_sc, s1_sc), mx_sc, acc_sc, NSA_HPG, tq, tk, k_tile, v_rows)
    o_win = _gated(_window_branch(qh, t0, kw_ref, vwt_ref, bias_ref, wbm_sc, (ws0_sc, ws1_sc),
                                  wacc_sc, lambda h: ops[0](tiles_sm[0], 0, h)),
                   _gates_t(gt_ref, bg_ref, 2))

    def short_first(ms):
        def run(ms):
            ops[1](0, 1, nk=SLC_HEAD)
            return ops[2](0, 1, ms, nk=SLC_HEAD)
        return lax.cond(short0, run, lambda ms: ms, ms)

    outs = _flash_t(ops, NSA_HPG, tq, tk, n_un, lambda i: tiles_sm[i], diag, cb_ref,
                    first_done=True, prologue=short_first)
    o_slc = _gated(outs, _gates_t(gt_ref, bg_ref, 1))
    _store_heads(o_ref, [a + b for a, b in zip(o_slc, o_win)])


def _win_bias():
    tq, nk = NSA_TQ, WINDOW + NSA_TQ
    out = []
    for p in range(WINDOW // tq + 1):
        t0 = p * tq
        key = max(t0 - WINDOW, 0) + np.arange(nk)[:, None]
        qpos = t0 + np.arange(tq)[None, :]
        out.append(np.where((key <= qpos) & (key > qpos - WINDOW), 0.0, NEG))
    return jnp.asarray(np.stack(out), F32)


def _block_onehot(S):
    e = (np.arange(S)[:, None] // SLC_BLOCK) == np.arange(NS_PAD)[None, :]
    return jnp.asarray(e, BF16)


def _slc_win_attention(u3, ut, selneg, flags, gt, bg, *, q_row, ks_col, vs_row, kw_col, vw_row):
    B, S, _ = u3.shape
    sp = {k: pl.BlockSpec(v.block_shape, lambda b, g, i, fl, f=v.index_map: f(b, g, i))
          for k, v in _nsa_specs(q_row).items()}
    bias = _win_bias()
    npat = bias.shape[0]
    kspec = lambda col: pl.BlockSpec((1, S, LANES), lambda b, g, i, fl: (b, 0, col + g))
    vspec = lambda row: pl.BlockSpec((1, HEAD_DIM, S), lambda b, g, i, fl: (b, row + g, 0))
    grid_spec = pltpu.PrefetchScalarGridSpec(
        num_scalar_prefetch=1,
        grid=(B, NSA_GROUPS, S // NSA_TQ),
        in_specs=[
            sp["q"], kspec(ks_col), vspec(vs_row),
            pl.BlockSpec((1, 1, NS_PAD, NSA_TQ), lambda b, g, i, fl: (b, g, 0, i)),
            pl.BlockSpec((S, NS_PAD), lambda b, g, i, fl: (0, 0)),
            kspec(kw_col), vspec(vw_row),
            pl.BlockSpec((1,) + bias.shape[1:],
                         lambda b, g, i, fl: (jnp.minimum(i, npat - 1), 0, 0)),
            pl.BlockSpec((SLC_TK, NSA_TQ), lambda b, g, i, fl: (0, 0)),
            sp["tab"], sp["gt"], sp["bg"],
        ],
        out_specs=sp["out"],
        scratch_shapes=_flash_scratch(NSA_HPG, NSA_TQ, SLC_TK, 2 * LANES)
        + _flash_scratch(NSA_HPG, NSA_TQ, WIN_TK, LANES, col_max=False)
        + [pltpu.SMEM((S // SLC_TK + 1,), jnp.int32)],
    )
    return pl.pallas_call(
        _slc_win_kernel,
        grid_spec=grid_spec,
        out_shape=jax.ShapeDtypeStruct((B, S, NSA_WIDTH), BF16),
        compiler_params=_cp(("parallel", "parallel", "arbitrary")),
        name="slc_win_attention",
    )(flags, ut, u3, ut, selneg, _block_onehot(S), u3, ut, bias, _causal_bias(SLC_TK, NSA_TQ),
      _slope_table(), gt, bg)


def _aug_groups(w):
    d = w.shape[0]
    w = w.reshape(d, NSA_GROUPS, HEAD_DIM)
    return jnp.pad(w, ((0, 0), (0, 0), (0, LANES - HEAD_DIM))).reshape(d, NSA_GROUPS * LANES)


def _even_layer(x, norm_g, w_in, b_f, gn_g, w_out):
    B, S, D = x.shape
    qscale = HEAD_DIM ** -0.5 * LOG2E
    q_f, k_f, v_f, w_fl, q_r, k_r, v_r, z = jnp.split(
        w_in, np.cumsum([FOX_WIDTH] * 3 + [FOX_HEADS] + [RET_WIDTH] * 3).tolist(), axis=1)
    w = jnp.concatenate([z, k_f, q_r, k_r * HEAD_DIM ** -0.5, v_r], axis=1).astype(BF16)
    w_t = jnp.concatenate([q_f * qscale, v_f], axis=1).T.astype(BF16)
    x2 = x.reshape(B * S, D)
    u, ut = _proj(x2, norm_g, w, seq=S, w_t=[w_t], t_dtypes=[BF16], tn=2 * PROJ_TN)
    u3 = u.reshape(B, S, -1)
    kfeat, qfeat = _fgate(x, norm_g, w_fl, b_f, tile=min(512, S))
    o_f = _fox(u3, ut, kfeat, qfeat, q_row=0, k_col=D // LANES, v_row=FOX_WIDTH // LANES)
    rb = (D + FOX_WIDTH) // RET_WIDTH
    o_r = _retention(u3, gn_g, q_col=rb, k_col=rb + 1, v_col=rb + 2)
    out = _out0(o_f.reshape(B * S, -1), o_r.reshape(B * S, -1), u, x2, w_out.astype(BF16))
    return out.reshape(B, S, D)


def _odd_layer(x, norm_g, w_in, b_gate, pe_k, pe_v, wk1, wk2, wv1, wv2, w_out, final_g):
    B, S, D = x.shape
    assert S // SLC_BLOCK <= NS_PAD
    qscale = HEAD_DIM ** -0.5 * LOG2E
    sizes = [NSA_WIDTH] + [NSA_KV_WIDTH] * 6 + [NSA_HEADS * N_BRANCH]
    q, kc, vc, ks, vs, kw, vw, gl, z = jnp.split(w_in, np.cumsum(sizes).tolist(), axis=1)
    w = jnp.concatenate([z, kc, vc, _aug_groups(ks), _aug_groups(kw)], axis=1).astype(BF16)
    per_group = NSA_HPG * N_BRANCH
    glt = jnp.pad(gl.T.reshape(NSA_GROUPS, per_group, D), ((0, 0), (0, GATE_ROWS - per_group), (0, 0)))
    glt = glt.reshape(NSA_GROUPS * GATE_ROWS, D).astype(BF16)
    bg = jnp.pad(b_gate.reshape(NSA_GROUPS, per_group), ((0, 0), (0, GATE_ROWS - per_group)))
    bg = bg.reshape(NSA_GROUPS * GATE_ROWS, 1)
    w_vt = jnp.concatenate([q * qscale, vs, vw], axis=1).T.astype(BF16)
    x2 = x.reshape(B * S, D)
    kcol = D + 2 * NSA_KV_WIDTH
    kwid = NSA_GROUPS * LANES
    u, kc_a, vc_a, ut, gt = _proj(
        x2, norm_g, w, seq=S, split_cols=(D, D + NSA_KV_WIDTH), split_width=NSA_KV_WIDTH,
        addend=_pos_features(jnp.arange(S, dtype=jnp.int32), kwid), add_cols=(kcol, kcol + kwid),
        w_t=[w_vt, glt], t_dtypes=[BF16, F32], tn=w.shape[1] // 2)
    u3 = u.reshape(B, S, -1)
    nseg = S // CMP_STRIDE
    kcmp = _compress(kc_a.reshape(B, nseg, -1), pe_k, wk1, wk2, transposed=False)
    vcmp_t = _compress(vc_a.reshape(B, nseg, -1), pe_v, wv1, wv2, transposed=True)
    o_c, selneg, flags = _cmp_attention(ut, kcmp, vcmp_t, gt, bg, q_row=0)
    kb = (D + 2 * NSA_KV_WIDTH) // LANES
    vb = NSA_WIDTH // HEAD_DIM
    o_s = _slc_win_attention(u3, ut, selneg, flags[:, 0, :S // SLC_TK + 1], gt, bg, q_row=0, ks_col=kb,
                             vs_row=vb, kw_col=kb + NSA_GROUPS, vw_row=vb + NSA_GROUPS)
    r = lambda a: a.reshape(B * S, -1)
    out = _out1(r(o_c), r(o_s), u, x2, w_out.astype(BF16), final_g)
    return out.reshape(B, S, D)


def kernel(x, even_norm_g, even_w_in, even_b_f, even_gn_g, even_w_out, odd_norm_g, odd_w_in,
           odd_b_gate, odd_pe_k, odd_pe_v, odd_wk1, odd_wk2, odd_wv1, odd_wv2, odd_w_out, final_g):
    x = _even_layer(x, even_norm_g[0], even_w_in[0], even_b_f[0], even_gn_g[0], even_w_out[0])
    return _odd_layer(x, odd_norm_g[0], odd_w_in[0], odd_b_gate[0], odd_pe_k[0], odd_pe_v[0],
                      odd_wk1[0], odd_wk2[0], odd_wv1[0], odd_wv2[0], odd_w_out[0], final_g)
```

```python
import functools
import math

import jax
import jax.numpy as jnp
import numpy as np
from jax import lax
from jax.experimental import pallas as pl
from jax.experimental.pallas import tpu as pltpu

D_MODEL = 1024
HEAD_DIM = 64
LANES = 128
FOX_HEADS = 8
RET_HEADS = 8
FOX_WIDTH = FOX_HEADS * HEAD_DIM
RET_WIDTH = RET_HEADS * HEAD_DIM
RET_CHUNK = 128
RET_STEP = 8
NSA_HEADS = 16
NSA_GROUPS = 4
NSA_HPG = NSA_HEADS // NSA_GROUPS
NSA_WIDTH = NSA_HEADS * HEAD_DIM
NSA_KV_WIDTH = NSA_GROUPS * HEAD_DIM
N_BRANCH = 3
GATE_ROWS = 16
CMP_BLOCK = 32
CMP_STRIDE = 16
CMP_HIDDEN = 256
CMP_CHUNK = 128
SLC_BLOCK = 64
SLC_TOPK = 16
N_FORCED = 3
NS_PAD = LANES
WINDOW = 512
RMS_EPS = 1e-6
GN_EPS = 1e-5
NEG = -1e30
FORCE_BONUS = 1e6
MASK_BIG = 2.0 ** 100
LOG2E = math.log2(math.e)
FEAT0 = HEAD_DIM
QF_ROWS = 16
ACC_ROWS = HEAD_DIM + 16

PROJ_TM = 1024
PROJ_TN = 512
FOX_TQ = 512
FOX_TK = 512
NSA_TQ = 512
SLC_TK = 512
SLC_HEAD = 128
WIN_TK = 256
OUT_TM = 1024
VMEM_LIMIT = 48 * 1024 * 1024

F32 = jnp.float32
BF16 = jnp.bfloat16


def _cp(sem, vmem=VMEM_LIMIT):
    return pltpu.CompilerParams(dimension_semantics=sem, vmem_limit_bytes=vmem)


def _dot(a, b):
    return jnp.dot(a, b, preferred_element_type=F32)


def _dot_nt(a, b):
    return lax.dot_general(a, b, (((1,), (1,)), ((), ())), preferred_element_type=F32)


def _dot_tn(a, b):
    return lax.dot_general(a, b, (((0,), (0,)), ((), ())), preferred_element_type=F32)


def _rms(x, g):
    return x * lax.rsqrt(jnp.mean(x * x, axis=-1, keepdims=True) + RMS_EPS) * g


def _silu(x):
    return x * (1.0 / (1.0 + jnp.exp(-x)))


def _sigmoid(x):
    return 1.0 / (1.0 + jnp.exp(-x))


def _low_half(shape, axis):
    return lax.broadcasted_iota(jnp.int32, shape, axis) < HEAD_DIM


def _pieces(v):
    p1 = v.astype(BF16).astype(F32)
    r = v - p1
    p2 = r.astype(BF16).astype(F32)
    p3 = (r - p2).astype(BF16).astype(F32)
    return p1, p2, p3


def _np_pieces(v):
    v = np.asarray(v, np.float64)
    bf = lambda a: np.asarray(a, np.float32).astype(BF16).astype(np.float64)
    p1 = bf(v)
    p2 = bf(v - p1)
    p3 = bf(v - p1 - p2)
    return p1, p2, p3


def _fgate_kernel(x_ref, g_ref, wf_ref, b_ref, pk_ref, kc_ref, pq_ref, qc_ref, kf_ref, qf_ref,
                  carry):
    @pl.when(pl.program_id(1) == 0)
    def _():
        carry[...] = jnp.zeros_like(carry)

    h = _rms(x_ref[0], g_ref[...])
    t = h.shape[0]
    h1 = h.astype(BF16)
    h2 = (h - h1.astype(F32)).astype(BF16)
    w = wf_ref[...]
    w1 = w.astype(BF16)
    w2 = (w - w1.astype(F32)).astype(BF16)
    nf = w.shape[1]
    r = _dot(jnp.concatenate([h1, h2], axis=0), jnp.concatenate([w1, w2], axis=1))
    f = r[:t, :nf] + r[:t, nf:] + r[t:, :nf] + b_ref[...]
    ls = jnp.minimum(f, 0.0) - jnp.log(1.0 + jnp.exp(-jnp.abs(f)))
    r = lax.broadcasted_iota(jnp.int32, (t, t), 0)
    c = lax.broadcasted_iota(jnp.int32, (t, t), 1)
    lower = jnp.where(c <= r, 1.0, 0.0).astype(BF16)
    r = _dot(lower, jnp.concatenate([p.astype(BF16) for p in _pieces(ls)], axis=1))
    cs = r[:, :nf] + r[:, nf:2 * nf] + r[:, 2 * nf:] + carry[...]
    carry[...] = cs[t - 1:t, :]
    cl = cs * LOG2E
    p1, p2, p3 = (p.astype(BF16) for p in _pieces(cl))
    kf = _dot(p1, pk_ref[0]) + _dot(p2, pk_ref[1]) + _dot(p3, pk_ref[2]) + kc_ref[...]
    for j in range(FOX_HEADS // 2):
        kf_ref[0, j] = kf[:, j * LANES:(j + 1) * LANES].astype(BF16)
    qf = (_dot_nt(pq_ref[0], p1) + _dot_nt(pq_ref[1], p2) + _dot_nt(pq_ref[2], p3)
          + qc_ref[...])
    qf_ref[0] = qf.astype(BF16).reshape(FOX_HEADS // 2, 2, QF_ROWS, t)


def _fgate_tables():
    npair = FOX_HEADS // 2
    pk = np.zeros((3, FOX_HEADS, npair * LANES), np.float32)
    kc = np.zeros((1, npair * LANES), np.float32)
    pq = np.zeros((3, FOX_HEADS * QF_ROWS, FOX_HEADS), np.float32)
    qc = np.zeros((FOX_HEADS * QF_ROWS, 1), np.float32)
    for h in range(FOX_HEADS):
        j, b = divmod(h, 2)
        for i in range(3):
            pk[i, h, j * LANES + 6 * b + i] = 1.0
            pq[i, h * QF_ROWS + 3 + i, h] = 1.0
            qc[h * QF_ROWS + 6 * b + i, 0] = -1.0
    for j in range(npair):
        kc[0, j * LANES + 3:j * LANES + 6] = 1.0
    return (jnp.asarray(pk, BF16), jnp.asarray(kc, F32), jnp.asarray(pq, BF16), jnp.asarray(qc, F32))


def _fgate(x, g, wf, b_f, *, tile):
    B, S, D = x.shape
    npair = FOX_HEADS // 2
    tables = _fgate_tables()
    return pl.pallas_call(
        _fgate_kernel,
        grid=(B, S // tile),
        in_specs=[
            pl.BlockSpec((1, tile, D), lambda b, s: (b, s, 0)),
            pl.BlockSpec((1, D), lambda b, s: (0, 0)),
            pl.BlockSpec((D, FOX_HEADS), lambda b, s: (0, 0)),
            pl.BlockSpec((1, FOX_HEADS), lambda b, s: (0, 0)),
        ] + [pl.BlockSpec(t.shape, lambda b, s, n=t.ndim: (0,) * n) for t in tables],
        out_specs=[
            pl.BlockSpec((1, npair, tile, LANES), lambda b, s: (b, 0, s, 0)),
            pl.BlockSpec((1, npair, 2, QF_ROWS, tile), lambda b, s: (b, 0, 0, 0, s)),
        ],
        out_shape=[
            jax.ShapeDtypeStruct((B, npair, S, LANES), BF16),
            jax.ShapeDtypeStruct((B, npair, 2, QF_ROWS, S), BF16),
        ],
        scratch_shapes=[pltpu.VMEM((1, FOX_HEADS), F32)],
        compiler_params=_cp(("parallel", "arbitrary")),
        name="fgate",
    )(x, g.reshape(1, D), wf, b_f.reshape(1, FOX_HEADS), *tables)


def _proj_kernel(*refs, tn, ntile, split_cols, add_cols, n_t):
    it = iter(refs)
    x_ref, g_ref, w_ref = next(it), next(it), next(it)
    add_ref = next(it) if add_cols else None
    wt_refs = [next(it) for _ in range(n_t)]
    u_ref = next(it)
    e_refs = [next(it) for _ in split_cols]
    ut_refs = [next(it) for _ in range(n_t)]
    h_sc = next(it)
    j = pl.program_id(1)

    @pl.when(j == 0)
    def _():
        h = _rms(x_ref[...], g_ref[...]).astype(BF16)
        h_sc[...] = h
        for wt_ref, ut_ref in zip(wt_refs, ut_refs):
            ut_ref[0] = _dot_nt(wt_ref[...], h).astype(ut_ref.dtype)

    acc = _dot(h_sc[...], w_ref[...])
    u_ref[...] = acc.astype(u_ref.dtype)

    def extras(t):
        lo = t * tn
        adds = [c - lo for c in add_cols if lo <= c < lo + tn]
        splits = [(e, c - lo) for e, c in zip(e_refs, split_cols) if lo <= c < lo + tn]

        def body():
            for c in adds:
                cols = slice(c, c + add_ref.shape[1])
                u_ref[:, cols] = (acc[:, cols] + add_ref[...].astype(F32)).astype(u_ref.dtype)
            for e_ref, c in splits:
                e_ref[...] = acc[:, c:c + e_ref.shape[1]].astype(e_ref.dtype)
        return body if adds or splits else None

    for t in range(ntile):
        body = extras(t)
        if body is not None:
            pl.when(j == t)(body)


def _proj(x2, g, w, *, seq, split_cols=(), split_width=0, addend=None, add_cols=(), w_t=(),
          t_dtypes=(), tm=PROJ_TM, tn=PROJ_TN):
    N, D = x2.shape
    W = w.shape[1]
    nbs = seq // tm
    in_specs = [
        pl.BlockSpec((tm, D), lambda i, j: (i, 0)),
        pl.BlockSpec((1, D), lambda i, j: (0, 0)),
        pl.BlockSpec((D, tn), lambda i, j: (0, j)),
    ]
    args = [x2, g.reshape(1, D), w]
    if add_cols:
        in_specs.append(pl.BlockSpec((tm, addend.shape[1]), lambda i, j: (i % nbs, 0)))
        args.append(addend)
    out_shape = [jax.ShapeDtypeStruct((N, W), BF16)]
    out_specs = [pl.BlockSpec((tm, tn), lambda i, j: (i, j))]
    for _ in split_cols:
        out_shape.append(jax.ShapeDtypeStruct((N, split_width), BF16))
        out_specs.append(pl.BlockSpec((tm, split_width), lambda i, j: (i, 0)))
    for wt, dt in zip(w_t, t_dtypes):
        rows = wt.shape[0]
        in_specs.append(pl.BlockSpec((rows, D), lambda i, j: (0, 0)))
        args.append(wt)
        out_shape.append(jax.ShapeDtypeStruct((N // seq, rows, seq), dt))
        out_specs.append(pl.BlockSpec((1, rows, tm), lambda i, j: (i // nbs, 0, i % nbs)))
    return pl.pallas_call(
        functools.partial(_proj_kernel, tn=tn, ntile=W // tn, split_cols=tuple(split_cols),
                          add_cols=tuple(add_cols), n_t=len(w_t)),
        grid=(N // tm, W // tn),
        in_specs=in_specs,
        out_specs=out_specs,
        out_shape=out_shape,
        scratch_shapes=[pltpu.VMEM((tm, D), BF16)],
        compiler_params=_cp(("parallel", "arbitrary")),
        name="proj",
    )(*args)


def _flash_ops(bm_sc, s_bufs, mx_sc, acc_sc, nh, tq, tk, k_tile, v_rows):
    acc_sc[...] = jnp.zeros_like(acc_sc)
    ones = jnp.ones((ACC_ROWS - HEAD_DIM, tk), BF16)

    def qk_head(tile, slot, h, qc=None, nk=tk, parts=None):
        for (r0, r1), qcp in parts or [((0, nk), qc)]:
            c0, c1 = qcp or (0, tq)
            cols = slice(h * tq + c0, h * tq + c1)
            s = _dot(k_tile(pl.multiple_of(tile * tk, tk))[r0:r1], bm_sc[:, cols])
            s_bufs[slot][r0:r1, cols] = s
            if mx_sc is not None and parts is None:
                mx_sc[slot, :, cols] = jnp.max(s, axis=0, keepdims=True)

    def soft_head(tile, slot, m_old, h, valid, bias, qc, kr):
        c0, c1 = qc or (0, tq)
        r0, r1 = kr
        cols = slice(h * tq + c0, h * tq + c1)
        buf = s_bufs[slot]
        k0 = pl.multiple_of(tile * tk, tk)
        m_sub = m_old[:, c0:c1]
        if valid is not None or bias is not None:
            s = (buf[r0:r1, cols] + bias[r0:r1, c0:c1] if valid is None
                 else jnp.where(valid[r0:r1, c0:c1], buf[r0:r1, cols], NEG))
            m_new = jnp.maximum(m_sub, jnp.max(s, axis=0, keepdims=True))
            p = jnp.exp2(s - m_new).astype(BF16)
        else:
            m_new = jnp.maximum(m_sub, mx_sc[slot, :, cols])
            p = jnp.exp2(buf[r0:r1, cols] - m_new).astype(BF16)
        alpha = jnp.exp2(m_sub - m_new)
        lhs = jnp.concatenate([v_rows(h, k0)[:, r0:r1], ones[:, :r1 - r0]], axis=0)
        acc_sc[h, :, c0:c1] = alpha * acc_sc[h, :, c0:c1] + _dot(lhs, p)
        parts = [m_new]
        if c0 > 0:
            parts.insert(0, m_old[:, :c0])
        if c1 < tq:
            parts.append(m_old[:, c1:])
        return jnp.concatenate(parts, axis=1) if len(parts) > 1 else m_new

    def qk_all(tile, slot, qc=None, nk=tk):
        for h in range(nh):
            qk_head(tile, slot, h, qc, nk)

    def step(cur, slot, ms, nxt=None, valid=None, bias=None, before=None, qc=None, nxt_qc=None,
             nk=tk, parts=None, nxt_parts=None):
        out = []
        for h in range(nh):
            if nxt is not None:
                qk_head(nxt, 1 - slot, h, nxt_qc, parts=nxt_parts)
            if before is not None:
                before(h)
            m = ms[h]
            for kr, qcp in parts or [((0, nk), qc)]:
                m = soft_head(cur, slot, m, h, valid, bias, qcp, kr)
            out.append(m)
        return tuple(out)

    def finish():
        outs = []
        for h in range(nh):
            a = acc_sc[h]
            outs.append(a[:HEAD_DIM] * (1.0 / a[HEAD_DIM:HEAD_DIM + 1]))
        return outs

    return qk_head, qk_all, step, finish


def _flash_t(ops, nh, tq, tk, n_un, tile_of, diag, diag_bias, first_done=False, prologue=None):
    _, qk_all, step, finish = ops
    assert tq == tk
    half = tk // 2
    diag_parts = [((0, half), None), ((half, tk), (half, tq))]

    def pair(i, ms):
        t_a, t_b, t_c = tile_of(2 * i), tile_of(2 * i + 1), tile_of(2 * i + 2)
        return step(t_b, 1, step(t_a, 0, ms, nxt=t_b), nxt=t_c)

    def odd_tail(ms):
        ms = step(tile_of(n_un - 1), 0, ms, nxt=diag, nxt_parts=diag_parts)
        return step(diag, 1, ms, bias=diag_bias[...], parts=diag_parts)

    def even_tail(ms):
        return step(diag, 0, ms, bias=diag_bias[...], parts=diag_parts)

    if not first_done:
        qk_all(tile_of(0), 0)
    ms = (jnp.full((1, tq), NEG, F32),) * nh
    if prologue is not None:
        ms = prologue(ms)
    ms = lax.fori_loop(0, n_un // 2, pair, ms)
    lax.cond(n_un % 2 == 1, odd_tail, even_tail, ms)
    return finish()


def _causal_bias(tk, tq):
    assert tk == tq
    return jnp.asarray(np.where(np.arange(tk)[:, None] <= np.arange(tq)[None, :], 0.0, NEG), F32)


def _flash_scratch(nh, tq, tk, kdim, col_max=True):
    bufs = [pltpu.VMEM((kdim, nh * tq), BF16), pltpu.VMEM((tk, nh * tq), F32),
            pltpu.VMEM((tk, nh * tq), F32)]
    if col_max:
        bufs.append(pltpu.VMEM((2, 1, nh * tq), F32))
    return bufs + [pltpu.VMEM((nh, ACC_ROWS, tq), F32)]


def _fox_kernel(qt_ref, qf_ref, k_ref, kf_ref, vt_ref, cb_ref, o_ref, bm_sc, s0_sc, s1_sc, mx_sc,
                acc_sc):
    qi = pl.program_id(2)
    tq, tk = FOX_TQ, FOX_TK
    bm_sc[...] = jnp.zeros_like(bm_sc)
    for h in range(2):
        rows = slice(h * HEAD_DIM, (h + 1) * HEAD_DIM)
        bm_sc[rows, h * tq:(h + 1) * tq] = qt_ref[0, rows, :]
        bm_sc[LANES:LANES + QF_ROWS, h * tq:(h + 1) * tq] = qf_ref[0, 0, h]

    def k_tile(k0):
        return jnp.concatenate([k_ref[0, pl.ds(k0, tk), :], kf_ref[0, 0, pl.ds(k0, tk), :]], axis=1)

    def v_rows(h, k0):
        return vt_ref[0, h * HEAD_DIM:(h + 1) * HEAD_DIM, pl.ds(k0, tk)]

    diag = (qi * tq) // tk
    ops = _flash_ops(bm_sc, (s0_sc, s1_sc), mx_sc, acc_sc, 2, tq, tk, k_tile, v_rows)
    outs = _flash_t(ops, 2, tq, tk, diag, lambda i: i, diag, cb_ref)
    o_ref[0] = jnp.concatenate(outs, axis=0).T.astype(o_ref.dtype)


def _fox(u3, ut, kfeat, qfeat, *, q_row, k_col, v_row):
    B, S, _ = u3.shape
    npair = FOX_HEADS // 2
    return pl.pallas_call(
        _fox_kernel,
        grid=(B, npair, S // FOX_TQ),
        in_specs=[
            pl.BlockSpec((1, LANES, FOX_TQ), lambda b, j, i: (b, q_row + j, i)),
            pl.BlockSpec((1, 1, 2, QF_ROWS, FOX_TQ), lambda b, j, i: (b, j, 0, 0, i)),
            pl.BlockSpec((1, S, LANES), lambda b, j, i: (b, 0, k_col + j)),
            pl.BlockSpec((1, 1, S, LANES), lambda b, j, i: (b, j, 0, 0)),
            pl.BlockSpec((1, LANES, S), lambda b, j, i: (b, v_row + j, 0)),
            pl.BlockSpec((FOX_TK, FOX_TQ), lambda b, j, i: (0, 0)),
        ],
        out_specs=pl.BlockSpec((1, FOX_TQ, LANES), lambda b, j, i: (b, i, j)),
        out_shape=jax.ShapeDtypeStruct((B, S, FOX_WIDTH), BF16),
        scratch_shapes=_flash_scratch(2, FOX_TQ, FOX_TK, 2 * LANES),
        compiler_params=_cp(("parallel", "parallel", "arbitrary")),
        name="fox",
    )(ut, qfeat, u3, kfeat, ut, _causal_bias(FOX_TK, FOX_TQ))


def _ret_kernel(q_ref, k_ref, v_ref, inner_ref, cross_ref, kdec_ref, cd_ref, bd_ref, gn_ref,
                o_ref, state_sc):
    @pl.when(pl.program_id(1) == 0)
    def _():
        state_sc[...] = jnp.zeros_like(state_sc)

    low = _low_half((RET_CHUNK, LANES), 1)
    inv = 1.0 / HEAD_DIM
    for c, j in [(c, j) for c in range(RET_STEP) for j in range(RET_HEADS // 2)]:
        rows = slice(c * RET_CHUNK, (c + 1) * RET_CHUNK)
        cols = slice(j * LANES, (j + 1) * LANES)
        q, k, v = q_ref[0, rows, cols], k_ref[0, rows, cols], v_ref[0, rows, cols]
        zero = jnp.zeros_like(q)
        qa, qb = jnp.where(low, q, zero), jnp.where(low, zero, q)
        pa = (_dot_nt(qa, k) * inner_ref[j, 0]).astype(BF16)
        pb = (_dot_nt(qb, k) * inner_ref[j, 1]).astype(BF16)
        o_in = jnp.where(low, _dot(pa, v), _dot(pb, v))
        state = state_sc[j]
        o = o_in + _dot(q, state.astype(BF16)) * cross_ref[j]
        kd = (k.astype(F32) * kdec_ref[j]).astype(BF16)
        state_sc[j] = state * cd_ref[j] + _dot_tn(kd, v) * bd_ref[...]
        sa = jnp.sum(jnp.where(low, o, 0.0), axis=-1, keepdims=True)
        st = jnp.sum(o, axis=-1, keepdims=True)
        mu = jnp.where(low, sa, st - sa) * inv
        d = o - mu
        d2 = d * d
        va = jnp.sum(jnp.where(low, d2, 0.0), axis=-1, keepdims=True)
        vt = jnp.sum(d2, axis=-1, keepdims=True)
        var = jnp.where(low, va, vt - va) * inv
        o_ref[0, rows, cols] = (d * lax.rsqrt(var + GN_EPS) * gn_ref[:, cols]).astype(o_ref.dtype)


def _ret_constants():
    lg = np.log(1.0 - 2.0 ** (-5.0 - np.arange(RET_HEADS)))
    i = np.arange(RET_CHUNK)
    diff = i[:, None] - i[None, :]
    inner = np.where(diff[None] >= 0, np.exp(lg[:, None, None] * np.maximum(diff, 0)[None]), 0.0)
    cross = np.exp(lg[:, None] * (i[None, :] + 1))
    kdec = np.exp(lg[:, None] * (RET_CHUNK - 1 - i)[None, :])
    cdec = np.exp(lg * RET_CHUNK)
    npair = RET_HEADS // 2
    inner = inner.reshape(npair, 2, RET_CHUNK, RET_CHUNK)

    def lanes(a):
        a = a.reshape(npair, 2, RET_CHUNK)
        return np.repeat(a.transpose(0, 2, 1), HEAD_DIM, axis=2)

    bd = np.kron(np.eye(2), np.ones((HEAD_DIM, HEAD_DIM)))
    cd = np.repeat(cdec.reshape(npair, 2), HEAD_DIM, axis=1)[:, :, None] * bd[None]
    f = lambda a: jnp.asarray(a, F32)
    return f(inner), f(lanes(cross)), f(lanes(kdec)), f(cd), f(bd)


def _retention(u3, gn_g, *, q_col, k_col, v_col):
    B, S, _ = u3.shape
    C = RET_CHUNK
    rows = RET_STEP * C
    npair = RET_HEADS // 2
    inner, cross, kdec, cd, bd = _ret_constants()
    full = lambda shape: pl.BlockSpec(shape, lambda b, i: (0,) * len(shape))
    return pl.pallas_call(
        _ret_kernel,
        grid=(B, S // rows),
        in_specs=[
            pl.BlockSpec((1, rows, RET_WIDTH), lambda b, i: (b, i, q_col)),
            pl.BlockSpec((1, rows, RET_WIDTH), lambda b, i: (b, i, k_col)),
            pl.BlockSpec((1, rows, RET_WIDTH), lambda b, i: (b, i, v_col)),
            full((npair, 2, C, C)), full((npair, C, LANES)), full((npair, C, LANES)),
            full((npair, LANES, LANES)), full((LANES, LANES)), full((1, RET_WIDTH)),
        ],
        out_specs=pl.BlockSpec((1, rows, RET_WIDTH), lambda b, i: (b, i, 0)),
        out_shape=jax.ShapeDtypeStruct((B, S, RET_WIDTH), BF16),
        scratch_shapes=[pltpu.VMEM((npair, LANES, LANES), F32)],
        compiler_params=_cp(("parallel", "arbitrary")),
        name="retention",
    )(u3, u3, u3, inner, cross, kdec, cd, bd, gn_g.reshape(1, RET_WIDTH))


def _out0_kernel(of_ref, or_ref, z_ref, x_ref, w_ref, o_ref):
    z = _silu(z_ref[...].astype(F32))
    ya = (of_ref[...].astype(F32) * z[:, :FOX_WIDTH]).astype(BF16)
    yb = (or_ref[...].astype(F32) * z[:, FOX_WIDTH:]).astype(BF16)
    o_ref[...] = x_ref[...] + _dot(ya, w_ref[:FOX_WIDTH, :]) + _dot(yb, w_ref[FOX_WIDTH:, :])


def _out0(o_f, o_r, u, x2, w_out, *, tm=OUT_TM):
    N, D = x2.shape
    return pl.pallas_call(
        _out0_kernel,
        grid=(N // tm,),
        in_specs=[
            pl.BlockSpec((tm, FOX_WIDTH), lambda i: (i, 0)),
            pl.BlockSpec((tm, RET_WIDTH), lambda i: (i, 0)),
            pl.BlockSpec((tm, D), lambda i: (i, 0)),
            pl.BlockSpec((tm, D), lambda i: (i, 0)),
            pl.BlockSpec((D, D), lambda i: (0, 0)),
        ],
        out_specs=pl.BlockSpec((tm, D), lambda i: (i, 0)),
        out_shape=jax.ShapeDtypeStruct((N, D), F32),
        compiler_params=_cp(("parallel",)),
        name="out0",
    )(o_f, o_r, u, x2, w_out)


def _out1_kernel(oc_ref, os_ref, z_ref, x_ref, w_ref, g_ref, o_ref):
    z = _silu(z_ref[...].astype(F32))
    y = ((oc_ref[...].astype(F32) + os_ref[...].astype(F32)) * z).astype(BF16)
    o_ref[...] = _rms(x_ref[...] + _dot(y, w_ref[...]), g_ref[...])


def _out1(o_c, o_s, u, x2, w_out, final_g, *, tm=OUT_TM):
    N, D = x2.shape
    row = pl.BlockSpec((tm, D), lambda i: (i, 0))
    return pl.pallas_call(
        _out1_kernel,
        grid=(N // tm,),
        in_specs=[row, row, row, row,
                  pl.BlockSpec((D, D), lambda i: (0, 0)),
                  pl.BlockSpec((1, D), lambda i: (0, 0))],
        out_specs=row,
        out_shape=jax.ShapeDtypeStruct((N, D), F32),
        compiler_params=_cp(("parallel",)),
        name="out1",
    )(o_c, o_s, u, x2, w_out, final_g.reshape(1, D))


def _compress_kernel(x_ref, pea_ref, peb_ref, wa_ref, wb_ref, w2_ref, *rest, transposed):
    x = x_ref[0].astype(F32)
    a = _dot((x + pea_ref[...]).astype(BF16), wa_ref[0])
    b = _dot((x + peb_ref[...]).astype(BF16), wb_ref[0])
    nseg = x.shape[0]
    pre = a + pltpu.roll(b, nseg - 1, 0)
    hid = _silu(pre).astype(BF16)
    if transposed:
        o_ref, = rest
        o_ref[0, 0] = _dot_nt(w2_ref[...], hid).astype(o_ref.dtype)
    else:
        feat_ref, o_ref = rest
        o_ref[0, 0] = (_dot(hid, w2_ref[...]) + feat_ref[...].astype(F32)).astype(o_ref.dtype)


def _compress(a3, pe, w1, w2, *, transposed):
    B, nseg, wid = a3.shape
    half = CMP_STRIDE * HEAD_DIM
    eye = jnp.eye(NSA_GROUPS, dtype=w1.dtype)

    def big(wh):
        w4 = wh.reshape(CMP_STRIDE, 1, HEAD_DIM, CMP_HIDDEN)
        sel = eye[:, None, :, None, None]
        return (sel * w4[None]).reshape(NSA_GROUPS, wid, CMP_HIDDEN).astype(BF16)

    def pe_big(p):
        return jnp.broadcast_to(p[:, None, :], (CMP_STRIDE, NSA_GROUPS, HEAD_DIM)).reshape(1, wid)

    args = [a3, pe_big(pe[:CMP_STRIDE]), pe_big(pe[CMP_STRIDE:]), big(w1[:half]), big(w1[half:])]
    in_specs = [
        pl.BlockSpec((1, nseg, wid), lambda b, g: (b, 0, 0)),
        pl.BlockSpec((1, wid), lambda b, g: (0, 0)),
        pl.BlockSpec((1, wid), lambda b, g: (0, 0)),
        pl.BlockSpec((1, wid, CMP_HIDDEN), lambda b, g: (g, 0, 0)),
        pl.BlockSpec((1, wid, CMP_HIDDEN), lambda b, g: (g, 0, 0)),
    ]
    if transposed:
        w2d = w2.T.astype(BF16)
        oshape, oblock = (B, NSA_GROUPS, HEAD_DIM, nseg), (1, 1, HEAD_DIM, nseg)
        args.append(w2d)
        in_specs.append(pl.BlockSpec(w2d.shape, lambda b, g: (0, 0)))
    else:
        w2d = jnp.pad(w2, ((0, 0), (0, LANES - HEAD_DIM))).astype(BF16)
        oshape, oblock = (B, NSA_GROUPS, nseg, LANES), (1, 1, nseg, LANES)
        feat = _pos_features(jnp.arange(nseg, dtype=jnp.int32) * CMP_STRIDE + (CMP_BLOCK - 1), LANES)
        args += [w2d, feat]
        in_specs += [pl.BlockSpec(w2d.shape, lambda b, g: (0, 0)),
                     pl.BlockSpec((nseg, LANES), lambda b, g: (0, 0))]
    return pl.pallas_call(
        functools.partial(_compress_kernel, transposed=transposed),
        grid=(B, NSA_GROUPS),
        in_specs=in_specs,
        out_specs=pl.BlockSpec(oblock, lambda b, g: (b, g, 0, 0)),
        out_shape=jax.ShapeDtypeStruct(oshape, BF16),
        compiler_params=_cp(("parallel", "parallel")),
        name="compress",
    )(*args)


def _slope_table():
    s = np.asarray(2.0 ** (-8.0 * (np.arange(NSA_HEADS) + 1) / NSA_HEADS), np.float32)
    sl = np.asarray(s.astype(np.float64) * LOG2E, np.float32)
    p1, p2, p3 = _np_pieces(sl)
    tab = np.zeros((NSA_HEADS, QF_ROWS), np.float32)
    for k, p in enumerate((p1, p1, p2, p2, p3, p3)):
        tab[:, k] = p
    tab[:, 6] = sl
    tab = np.broadcast_to(tab.reshape(NSA_GROUPS, NSA_HPG * QF_ROWS, 1),
                          (NSA_GROUPS, NSA_HPG * QF_ROWS, NSA_TQ))
    return jnp.asarray(tab)


def _pos_features(pos, width):
    pos = pos[:, None]
    lane = jnp.arange(width, dtype=jnp.int32)[None, :] % LANES
    hi = ((pos // SLC_BLOCK) * SLC_BLOCK).astype(F32)
    lo = (pos % SLC_BLOCK).astype(F32)
    k = lane - FEAT0
    f = jnp.where((k >= 0) & (k < 6), jnp.where(k % 2 == 0, hi, lo), 0.0)
    f = jnp.where((k >= 6) & (k < 9), 1.0, f)
    return f.astype(BF16)


def _nsa_queries(qt_ref, tab_ref, t0):
    tq = qt_ref.shape[2]
    r = lax.broadcasted_iota(jnp.int32, (QF_ROWS, tq), 0)
    t = (t0 + lax.broadcasted_iota(jnp.int32, (1, tq), 1)).astype(F32)
    zeros = jnp.zeros((LANES - HEAD_DIM - QF_ROWS, tq), BF16)
    out = []
    for i in range(NSA_HPG):
        tile = tab_ref[0, i * QF_ROWS:(i + 1) * QF_ROWS, :]
        a1, a2, a3 = _pieces(-(tile[6:7, :] * t))
        feat = jnp.where(r == 6, a1, jnp.where(r == 7, a2, jnp.where(r == 8, a3,
                                                                     jnp.where(r < 6, tile, 0.0))))
        out.append(jnp.concatenate([qt_ref[0, i * HEAD_DIM:(i + 1) * HEAD_DIM, :],
                                    feat.astype(BF16), zeros], axis=0))
    return out


def _gates_t(gt_ref, bg_ref, branch):
    gl = gt_ref[0] + bg_ref[...]
    return [_sigmoid(gl[N_BRANCH * i + branch:N_BRANCH * i + branch + 1, :]) for i in range(NSA_HPG)]


def _gated(outs_t, gates):
    return [o * gt for o, gt in zip(outs_t, gates)]


def _store_heads(o_ref, g):
    o_ref[0, :, :LANES] = jnp.concatenate(g[:2], axis=0).T.astype(o_ref.dtype)
    o_ref[0, :, LANES:] = jnp.concatenate(g[2:], axis=0).T.astype(o_ref.dtype)


def _nsa_specs(q_row):
    return dict(
        q=pl.BlockSpec((1, NSA_HPG * HEAD_DIM, NSA_TQ), lambda b, g, i: (b, q_row + g, i)),
        tab=pl.BlockSpec((1, NSA_HPG * QF_ROWS, NSA_TQ), lambda b, g, i: (g, 0, 0)),
        gt=pl.BlockSpec((1, GATE_ROWS, NSA_TQ), lambda b, g, i: (b, g, i)),
        bg=pl.BlockSpec((GATE_ROWS, 1), lambda b, g, i: (g, 0)),
        out=pl.BlockSpec((1, NSA_TQ, NSA_HPG * HEAD_DIM), lambda b, g, i: (b, i, g)),
    )


def _argmax_first(v, idx):
    n = v.shape[0]
    slabs = [(v[r:r + 8], idx[r:r + 8]) for r in range(0, n, 8)]
    while len(slabs) > 1:
        nxt = []
        for (va, ia), (vb, ib) in zip(slabs[0::2], slabs[1::2]):
            right = vb > va
            nxt.append((jnp.where(right, vb, va), jnp.where(right, ib, ia)))
        if len(slabs) % 2:
            nxt.append(slabs[-1])
        slabs = nxt
    v, idx = slabs[0]
    mx = jnp.max(v, axis=0, keepdims=True)
    first = jnp.min(jnp.where(v == mx, idx, float(n)), axis=0, keepdims=True)
    return mx, first


def _cmp_body(nc, t0, qh, kc_ref, vct_ref, mt_ref, grp_ref, gates, o_ref, sel_ref, flag_ref):
    tq = NSA_TQ
    rows = nc * CMP_CHUNK
    full = max(rows - CMP_CHUNK - 8, 0)
    nseg = kc_ref.shape[2]
    kc = kc_ref[0, 0, :rows, :]
    vct = vct_ref[0, 0, :, :rows]
    t = t0 + lax.broadcasted_iota(jnp.int32, (1, tq), 1)
    cidx = full + lax.broadcasted_iota(jnp.int32, (rows - full, 1), 0)
    valid = (cidx * CMP_STRIDE + (CMP_BLOCK - 1) <= t) & (cidx < nseg - 1)
    psum = jnp.zeros((rows, tq), F32)
    ps = []
    s_all = _dot(kc, jnp.concatenate(qh, axis=1))
    for i in range(NSA_HPG):
        s = s_all[:, i * tq:(i + 1) * tq]
        s_last = jnp.where(valid, s[full:], NEG)
        m = jnp.max(s_last, axis=0, keepdims=True)
        if nc > 1:
            m = jnp.maximum(m, jnp.max(s[:full], axis=0, keepdims=True))
        e = jnp.where(valid, jnp.exp2(s_last - m), 0.0)
        if nc > 1:
            e = jnp.concatenate([jnp.exp2(s[:full] - m), e], axis=0)
        l = jnp.sum(e, axis=0, keepdims=True)
        p = e * jnp.where(l > 0.0, 1.0 / l, 0.0)
        psum = psum + p
        ps.append(p.astype(BF16))
    o_all = _dot(vct, jnp.concatenate(ps, axis=1))
    _store_heads(o_ref, _gated([o_all[:, i * tq:(i + 1) * tq] for i in range(NSA_HPG)], gates))
    ns = rows * CMP_STRIDE // SLC_BLOCK
    mt = mt_ref[:ns, :rows]
    imp = sum(_dot(mt, p.astype(BF16)) for p in _pieces(psum))
    blk = lax.broadcasted_iota(jnp.int32, (ns, 1), 0)
    cur = t // SLC_BLOCK
    bvalid = blk * SLC_BLOCK <= t
    forced = (blk == 0) | (blk == cur) | (blk == cur - 1)
    score = jnp.where(forced, -jnp.inf, jnp.where(bvalid, imp, NEG))
    blk_f = jnp.broadcast_to(blk.astype(F32), (ns, tq))
    work = score
    for _ in range(SLC_TOPK - N_FORCED):
        mx, first = _argmax_first(work, blk_f)
        work = jnp.where(blk_f == first, -jnp.inf, work)
    picked = (score > mx) | ((score == mx) & (blk_f <= first))
    selneg = jnp.where(bvalid & (forced | picked), 0.0, -MASK_BIG)
    if ns < NS_PAD:
        selneg = jnp.concatenate([selneg, jnp.full((NS_PAD - ns, tq), -MASK_BIG, F32)], axis=0)
    sel_ref[0, 0] = selneg.astype(sel_ref.dtype)
    picked = jnp.where(selneg == 0.0, 1.0, 0.0).astype(BF16)
    used = _dot_nt(jnp.ones((8, tq), BF16), picked)
    used = jnp.where(used > 0.0, 1.0, 0.0).astype(BF16)
    flag_ref[0] = (_dot(used, grp_ref[...])[0:1] > 0.0).astype(jnp.int32)


def _cmp_kernel(q_ref, kc_ref, vct_ref, mt_ref, grp_ref, tab_ref, gt_ref, bg_ref,
                o_ref, sel_ref, flag_ref):
    t0 = pl.program_id(2) * NSA_TQ
    qh = _nsa_queries(q_ref, tab_ref, t0)
    gates = _gates_t(gt_ref, bg_ref, 0)
    nchunk = kc_ref.shape[2] // CMP_CHUNK
    last = t0 // (CMP_CHUNK * CMP_STRIDE)
    for nc in range(1, nchunk + 1):
        pl.when(last == nc - 1)(functools.partial(
            _cmp_body, nc, t0, qh, kc_ref, vct_ref, mt_ref, grp_ref, gates, o_ref, sel_ref, flag_ref))


def _cmp_to_slc_t(nseg, ns):
    c0 = np.arange(nseg)[:, None] * CMP_STRIDE
    s0 = np.arange(ns)[None, :] * SLC_BLOCK
    overlap = np.clip(np.minimum(c0 + CMP_BLOCK, s0 + SLC_BLOCK) - np.maximum(c0, s0), 0, None)
    m = overlap / CMP_STRIDE
    m[nseg - 1] = 0.0
    mt = np.zeros((NS_PAD, nseg))
    mt[:ns] = m.T
    return jnp.asarray(mt, BF16)


def _tile_groups(ntile):
    per = SLC_TK // SLC_BLOCK
    blk = np.arange(NS_PAD)[:, None]
    g = (blk // per) == np.arange(NS_PAD)[None, :]
    g[:, ntile] = ((blk >= SLC_HEAD // SLC_BLOCK) & (blk < per))[:, 0]
    return jnp.asarray(g, BF16)


def _cmp_attention(ut, kcmp, vcmp_t, gt, bg, *, q_row):
    B, _, S = ut.shape
    nseg = kcmp.shape[2]
    nq = S // NSA_TQ
    sp = _nsa_specs(q_row)
    return pl.pallas_call(
        _cmp_kernel,
        grid=(B, NSA_GROUPS, S // NSA_TQ),
        in_specs=[
            sp["q"],
            pl.BlockSpec((1, 1, nseg, LANES), lambda b, g, i: (b, g, 0, 0)),
            pl.BlockSpec((1, 1, HEAD_DIM, nseg), lambda b, g, i: (b, g, 0, 0)),
            pl.BlockSpec((NS_PAD, nseg), lambda b, g, i: (0, 0)),
            pl.BlockSpec((NS_PAD, NS_PAD), lambda b, g, i: (0, 0)),
            sp["tab"], sp["gt"], sp["bg"],
        ],
        out_specs=[sp["out"], pl.BlockSpec((1, 1, NS_PAD, NSA_TQ), lambda b, g, i: (b, g, 0, i)),
                   pl.BlockSpec((1, 1, NS_PAD), lambda b, g, i: ((b * NSA_GROUPS + g) * nq + i, 0, 0))],
        out_shape=[jax.ShapeDtypeStruct((B, S, NSA_WIDTH), BF16),
                   jax.ShapeDtypeStruct((B, NSA_GROUPS, NS_PAD, S), BF16),
                   jax.ShapeDtypeStruct((B * NSA_GROUPS * nq, 1, NS_PAD), jnp.int32)],
        compiler_params=_cp(("parallel", "parallel", "arbitrary")),
        name="cmp_attention",
    )(ut, kcmp, vcmp_t, _cmp_to_slc_t(nseg, S // SLC_BLOCK), _tile_groups(S // SLC_TK), _slope_table(), gt, bg)


def _window_branch(qh, t0, k_ref, vt_ref, bias_ref, bm_sc, s_bufs, acc_sc, last_before):
    tq, tk = NSA_TQ, WIN_TK
    bm_sc[...] = jnp.concatenate(qh, axis=1)
    _, qk_all, step, finish = _flash_ops(
        bm_sc, s_bufs, None, acc_sc, NSA_HPG, tq, tk,
        lambda k0: k_ref[0, pl.ds(k0, tk), :], lambda h, k0: vt_ref[0, :, pl.ds(k0, tk)])
    first = jnp.maximum(t0 - WINDOW, 0) // tk
    ntile = (WINDOW + tq) // tk

    def run(qcs):
        qk_all(first, 0, qcs[0])
        ms = (jnp.full((1, tq), NEG, F32),) * NSA_HPG
        for j in range(ntile):
            last = j + 1 == ntile
            ms = step(first + j, j % 2, ms, nxt=None if last else first + j + 1,
                      bias=bias_ref[0, j * tk:(j + 1) * tk, :],
                      before=last_before if last else None, qc=qcs[j],
                      nxt_qc=None if last else qcs[j + 1])

    lane = lambda v: max(0, min(tq, v)) // LANES * LANES
    qcs = []
    for j in range(ntile):
        c0, c1 = lane(j * tk - WINDOW + 1), tq - lane(tq - (j + 1) * tk)
        qcs.append(None if (c0, c1) == (0, tq) else (c0, c1))
    if all(q is None for q in qcs):
        run(qcs)
    else:
        lax.cond(t0 >= WINDOW, lambda: run(qcs), lambda: run([None] * ntile))
    return finish()


def _slc_win_kernel(fl_ref, q_ref, k_ref, vt_ref, sel_ref, e_ref, kw_ref, vwt_ref, bias_ref,
                    cb_ref, tab_ref, gt_ref, bg_ref, o_ref, bm_sc, s0_sc, s1_sc, mx_sc, acc_sc,
                    wbm_sc, ws0_sc, ws1_sc, wacc_sc, tiles_sm):
    qi = pl.program_id(2)
    tq, tk = NSA_TQ, SLC_TK
    t0 = qi * tq
    diag = t0 // tk
    row = (pl.program_id(0) * NSA_GROUPS + pl.program_id(1)) * pl.num_programs(2) + qi
    ntile = k_ref.shape[1] // tk
    short0 = (fl_ref[row, ntile] == 0) & (diag > 0)
    n_un = jnp.int32(0)
    for j in range(ntile):
        tiles_sm[n_un] = jnp.int32(j)
        keep = (fl_ref[row, j] > 0) & (j < diag)
        if j == 0:
            keep = keep & jnp.logical_not(short0)
        n_un = n_un + keep.astype(jnp.int32)
    tiles_sm[n_un] = diag
    qh = _nsa_queries(q_ref, tab_ref, t0)
    selneg = sel_ref[0, 0]
    bm_sc[...] = jnp.concatenate([jnp.concatenate([q, selneg], axis=0) for q in qh], axis=1)

    def k_tile(k0):
        return jnp.concatenate([k_ref[0, pl.ds(k0, tk), :], e_ref[pl.ds(k0, tk), :]], axis=1)

    def v_rows(h, k0):
        return vt_ref[0, :, pl.ds(k0, tk)]

    ops = _flash_ops(bm_sc, (s0_sc, s1_sc), mx_sc, acc_sc, NSA_HPG, tq, tk, k_tile, v_rows)
    o_win = _gated(_window_branch(qh, t0, kw_ref, vwt_ref, bias_ref, wbm_sc, (ws0_sc, ws1_sc),
                                  wacc_sc, lambda h: ops[0](tiles_sm[0], 0, h)),
                   _gates_t(gt_ref, bg_ref, 2))

    def short_first(ms):
        def run(ms):
            ops[1](0, 1, nk=SLC_HEAD)
            return ops[2](0, 1, ms, nk=SLC_HEAD)
        return lax.cond(short0, run, lambda ms: ms, ms)

    outs = _flash_t(ops, NSA_HPG, tq, tk, n_un, lambda i: tiles_sm[i], diag, cb_ref,
                    first_done=True, prologue=short_first)
    o_slc = _gated(outs, _gates_t(gt_ref, bg_ref, 1))
    _store_heads(o_ref, [a + b for a, b in zip(o_slc, o_win)])


def _win_bias():
    tq, nk = NSA_TQ, WINDOW + NSA_TQ
    out = []
    for p in range(WINDOW // tq + 1):
        t0 = p * tq
        key = max(t0 - WINDOW, 0) + np.arange(nk)[:, None]
        qpos = t0 + np.arange(tq)[None, :]
        out.append(np.where((key <= qpos) & (key > qpos - WINDOW), 0.0, NEG))
    return jnp.asarray(np.stack(out), F32)


def _block_onehot(S):
    e = (np.arange(S)[:, None] // SLC_BLOCK) == np.arange(NS_PAD)[None, :]
    return jnp.asarray(e, BF16)


def _slc_win_attention(u3, ut, selneg, flags, gt, bg, *, q_row, ks_col, vs_row, kw_col, vw_row):
    B, S, _ = u3.shape
    sp = {k: pl.BlockSpec(v.block_shape, lambda b, g, i, fl, f=v.index_map: f(b, g, i))
          for k, v in _nsa_specs(q_row).items()}
    bias = _win_bias()
    npat = bias.shape[0]
    kspec = lambda col: pl.BlockSpec((1, S, LANES), lambda b, g, i, fl: (b, 0, col + g))
    vspec = lambda row: pl.BlockSpec((1, HEAD_DIM, S), lambda b, g, i, fl: (b, row + g, 0))
    grid_spec = pltpu.PrefetchScalarGridSpec(
        num_scalar_prefetch=1,
        grid=(B, NSA_GROUPS, S // NSA_TQ),
        in_specs=[
            sp["q"], kspec(ks_col), vspec(vs_row),
            pl.BlockSpec((1, 1, NS_PAD, NSA_TQ), lambda b, g, i, fl: (b, g, 0, i)),
            pl.BlockSpec((S, NS_PAD), lambda b, g, i, fl: (0, 0)),
            kspec(kw_col), vspec(vw_row),
            pl.BlockSpec((1,) + bias.shape[1:],
                         lambda b, g, i, fl: (jnp.minimum(i, npat - 1), 0, 0)),
            pl.BlockSpec((SLC_TK, NSA_TQ), lambda b, g, i, fl: (0, 0)),
            sp["tab"], sp["gt"], sp["bg"],
        ],
        out_specs=sp["out"],
        scratch_shapes=_flash_scratch(NSA_HPG, NSA_TQ, SLC_TK, 2 * LANES)
        + _flash_scratch(NSA_HPG, NSA_TQ, WIN_TK, LANES, col_max=False)
        + [pltpu.SMEM((S // SLC_TK + 1,), jnp.int32)],
    )
    return pl.pallas_call(
        _slc_win_kernel,
        grid_spec=grid_spec,
        out_shape=jax.ShapeDtypeStruct((B, S, NSA_WIDTH), BF16),
        compiler_params=_cp(("parallel", "parallel", "arbitrary")),
        name="slc_win_attention",
    )(flags, ut, u3, ut, selneg, _block_onehot(S), u3, ut, bias, _causal_bias(SLC_TK, NSA_TQ),
      _slope_table(), gt, bg)


def _aug_groups(w):
    d = w.shape[0]
    w = w.reshape(d, NSA_GROUPS, HEAD_DIM)
    return jnp.pad(w, ((0, 0), (0, 0), (0, LANES - HEAD_DIM))).reshape(d, NSA_GROUPS * LANES)


def _even_layer(x, norm_g, w_in, b_f, gn_g, w_out):
    B, S, D = x.shape
    qscale = HEAD_DIM ** -0.5 * LOG2E
    q_f, k_f, v_f, w_fl, q_r, k_r, v_r, z = jnp.split(
        w_in, np.cumsum([FOX_WIDTH] * 3 + [FOX_HEADS] + [RET_WIDTH] * 3).tolist(), axis=1)
    w = jnp.concatenate([z, k_f, q_r, k_r * HEAD_DIM ** -0.5, v_r], axis=1).astype(BF16)
    w_t = jnp.concatenate([q_f * qscale, v_f], axis=1).T.astype(BF16)
    x2 = x.reshape(B * S, D)
    u, ut = _proj(x2, norm_g, w, seq=S, w_t=[w_t], t_dtypes=[BF16], tn=w.shape[1] // 2)
    u3 = u.reshape(B, S, -1)
    kfeat, qfeat = _fgate(x, norm_g, w_fl, b_f, tile=min(512, S))
    o_f = _fox(u3, ut, kfeat, qfeat, q_row=0, k_col=D // LANES, v_row=FOX_WIDTH // LANES)
    rb = (D + FOX_WIDTH) // RET_WIDTH
    o_r = _retention(u3, gn_g, q_col=rb, k_col=rb + 1, v_col=rb + 2)
    out = _out0(o_f.reshape(B * S, -1), o_r.reshape(B * S, -1), u, x2, w_out.astype(BF16))
    return out.reshape(B, S, D)


def _odd_layer(x, norm_g, w_in, b_gate, pe_k, pe_v, wk1, wk2, wv1, wv2, w_out, final_g):
    B, S, D = x.shape
    assert S // SLC_BLOCK <= NS_PAD
    qscale = HEAD_DIM ** -0.5 * LOG2E
    sizes = [NSA_WIDTH] + [NSA_KV_WIDTH] * 6 + [NSA_HEADS * N_BRANCH]
    q, kc, vc, ks, vs, kw, vw, gl, z = jnp.split(w_in, np.cumsum(sizes).tolist(), axis=1)
    w = jnp.concatenate([z, kc, vc, _aug_groups(ks), _aug_groups(kw)], axis=1).astype(BF16)
    per_group = NSA_HPG * N_BRANCH
    glt = jnp.pad(gl.T.reshape(NSA_GROUPS, per_group, D), ((0, 0), (0, GATE_ROWS - per_group), (0, 0)))
    glt = glt.reshape(NSA_GROUPS * GATE_ROWS, D).astype(BF16)
    bg = jnp.pad(b_gate.reshape(NSA_GROUPS, per_group), ((0, 0), (0, GATE_ROWS - per_group)))
    bg = bg.reshape(NSA_GROUPS * GATE_ROWS, 1)
    w_vt = jnp.concatenate([q * qscale, vs, vw], axis=1).T.astype(BF16)
    x2 = x.reshape(B * S, D)
    kcol = D + 2 * NSA_KV_WIDTH
    kwid = NSA_GROUPS * LANES
    u, kc_a, vc_a, ut, gt = _proj(
        x2, norm_g, w, seq=S, split_cols=(D, D + NSA_KV_WIDTH), split_width=NSA_KV_WIDTH,
        addend=_pos_features(jnp.arange(S, dtype=jnp.int32), kwid), add_cols=(kcol, kcol + kwid),
        w_t=[w_vt, glt], t_dtypes=[BF16, F32], tn=w.shape[1] // 2)
    u3 = u.reshape(B, S, -1)
    nseg = S // CMP_STRIDE
    kcmp = _compress(kc_a.reshape(B, nseg, -1), pe_k, wk1, wk2, transposed=False)
    vcmp_t = _compress(vc_a.reshape(B, nseg, -1), pe_v, wv1, wv2, transposed=True)
    o_c, selneg, flags = _cmp_attention(ut, kcmp, vcmp_t, gt, bg, q_row=0)
    kb = (D + 2 * NSA_KV_WIDTH) // LANES
    vb = NSA_WIDTH // HEAD_DIM
    o_s = _slc_win_attention(u3, ut, selneg, flags[:, 0, :S // SLC_TK + 1], gt, bg, q_row=0, ks_col=kb,
                             vs_row=vb, kw_col=kb + NSA_GROUPS, vw_row=vb + NSA_GROUPS)
    r = lambda a: a.reshape(B * S, -1)
    out = _out1(r(o_c), r(o_s), u, x2, w_out.astype(BF16), final_g)
    return out.reshape(B, S, D)


def kernel(x, even_norm_g, even_w_in, even_b_f, even_gn_g, even_w_out, odd_norm_g, odd_w_in,
           odd_b_gate, odd_pe_k, odd_pe_v, odd_wk1, odd_wk2, odd_wv1, odd_wv2, odd_w_out, final_g):
    x = _even_layer(x, even_norm_g[0], even_w_in[0], even_b_f[0], even_gn_g[0], even_w_out[0])
    return _odd_layer(x, odd_norm_g[0], odd_w_in[0], odd_b_gate[0], odd_pe_k[0], odd_pe_v[0],
                      odd_wk1[0], odd_wk2[0], odd_wv1[0], odd_wv2[0], odd_w_out[0], final_g)
```

```python
import functools
import math

import jax
import jax.numpy as jnp
import numpy as np
from jax import lax
from jax.experimental import pallas as pl
from jax.experimental.pallas import tpu as pltpu

HEAD_DIM = 64
LANES = 128
FOX_HEADS = 8
RET_HEADS = 8
FOX_WIDTH = FOX_HEADS * HEAD_DIM
RET_WIDTH = RET_HEADS * HEAD_DIM
RET_CHUNK = 128
RET_STEP = 8
NSA_HEADS = 16
NSA_GROUPS = 4
NSA_HPG = NSA_HEADS // NSA_GROUPS
NSA_WIDTH = NSA_HEADS * HEAD_DIM
NSA_KV_WIDTH = NSA_GROUPS * HEAD_DIM
N_BRANCH = 3
GATE_ROWS = 16
CMP_BLOCK = 32
CMP_STRIDE = 16
CMP_HIDDEN = 256
CMP_CHUNK = 128
SLC_BLOCK = 64
SLC_TOPK = 16
N_FORCED = 3
NS_PAD = LANES
WINDOW = 512
RMS_EPS = 1e-6
GN_EPS = 1e-5
NEG = -1e30
FORCE_BONUS = 1e6
MASK_BIG = 2.0 ** 100
LOG2E = math.log2(math.e)
FEAT0 = HEAD_DIM
QF_ROWS = 16
ACC_ROWS = HEAD_DIM + 16

PROJ_TM = 1024
PROJ_TN = 512
FOX_TQ = 512
FOX_TK = 512
NSA_TQ = 512
SLC_TK = 512
SLC_HEAD = 128
WIN_TK = 256
OUT_TM = 1024
VMEM_LIMIT = 48 * 1024 * 1024

F32 = jnp.float32
BF16 = jnp.bfloat16


def _cp(sem, vmem=VMEM_LIMIT):
    return pltpu.CompilerParams(dimension_semantics=sem, vmem_limit_bytes=vmem)


def _dot(a, b):
    return jnp.dot(a, b, preferred_element_type=F32)


def _dot_nt(a, b):
    return lax.dot_general(a, b, (((1,), (1,)), ((), ())), preferred_element_type=F32)


def _dot_tn(a, b):
    return lax.dot_general(a, b, (((0,), (0,)), ((), ())), preferred_element_type=F32)


def _rms(x, g):
    return x * lax.rsqrt(jnp.mean(x * x, axis=-1, keepdims=True) + RMS_EPS) * g


def _silu(x):
    return x * (1.0 / (1.0 + jnp.exp(-x)))


def _sigmoid(x):
    return 1.0 / (1.0 + jnp.exp(-x))


def _low_half(shape, axis):
    return lax.broadcasted_iota(jnp.int32, shape, axis) < HEAD_DIM


def _pieces(v):
    p1 = v.astype(BF16).astype(F32)
    r = v - p1
    p2 = r.astype(BF16).astype(F32)
    p3 = (r - p2).astype(BF16).astype(F32)
    return p1, p2, p3


def _np_pieces(v):
    v = np.asarray(v, np.float64)
    bf = lambda a: np.asarray(a, np.float32).astype(BF16).astype(np.float64)
    p1 = bf(v)
    p2 = bf(v - p1)
    p3 = bf(v - p1 - p2)
    return p1, p2, p3


def _fgate_kernel(x_ref, g_ref, wf_ref, b_ref, pk_ref, kc_ref, pq_ref, qc_ref, kf_ref, qf_ref,
                  carry):
    @pl.when(pl.program_id(1) == 0)
    def _():
        carry[...] = jnp.zeros_like(carry)

    h = _rms(x_ref[0], g_ref[...])
    t = h.shape[0]
    h1 = h.astype(BF16)
    h2 = (h - h1.astype(F32)).astype(BF16)
    w = wf_ref[...]
    w1 = w.astype(BF16)
    w2 = (w - w1.astype(F32)).astype(BF16)
    nf = w.shape[1]
    r = _dot(jnp.concatenate([h1, h2], axis=0), jnp.concatenate([w1, w2], axis=1))
    f = r[:t, :nf] + r[:t, nf:] + r[t:, :nf] + b_ref[...]
    ls = jnp.minimum(f, 0.0) - jnp.log(1.0 + jnp.exp(-jnp.abs(f)))
    r = lax.broadcasted_iota(jnp.int32, (t, t), 0)
    c = lax.broadcasted_iota(jnp.int32, (t, t), 1)
    lower = jnp.where(c <= r, 1.0, 0.0).astype(BF16)
    r = _dot(lower, jnp.concatenate([p.astype(BF16) for p in _pieces(ls)], axis=1))
    cs = r[:, :nf] + r[:, nf:2 * nf] + r[:, 2 * nf:] + carry[...]
    carry[...] = cs[t - 1:t, :]
    cl = cs * LOG2E
    p1, p2, p3 = (p.astype(BF16) for p in _pieces(cl))
    kf = _dot(p1, pk_ref[0]) + _dot(p2, pk_ref[1]) + _dot(p3, pk_ref[2]) + kc_ref[...]
    for j in range(FOX_HEADS // 2):
        kf_ref[0, j] = kf[:, j * LANES:(j + 1) * LANES].astype(BF16)
    qf = (_dot_nt(pq_ref[0], p1) + _dot_nt(pq_ref[1], p2) + _dot_nt(pq_ref[2], p3)
          + qc_ref[...])
    qf_ref[0] = qf.astype(BF16).reshape(FOX_HEADS // 2, 2, QF_ROWS, t)


def _fgate_tables():
    npair = FOX_HEADS // 2
    pk = np.zeros((3, FOX_HEADS, npair * LANES), np.float32)
    kc = np.zeros((1, npair * LANES), np.float32)
    pq = np.zeros((3, FOX_HEADS * QF_ROWS, FOX_HEADS), np.float32)
    qc = np.zeros((FOX_HEADS * QF_ROWS, 1), np.float32)
    for h in range(FOX_HEADS):
        j, b = divmod(h, 2)
        for i in range(3):
            pk[i, h, j * LANES + 6 * b + i] = 1.0
            pq[i, h * QF_ROWS + 3 + i, h] = 1.0
            qc[h * QF_ROWS + 6 * b + i, 0] = -1.0
    for j in range(npair):
        kc[0, j * LANES + 3:j * LANES + 6] = 1.0
    return (jnp.asarray(pk, BF16), jnp.asarray(kc, F32), jnp.asarray(pq, BF16), jnp.asarray(qc, F32))


def _fgate(x, g, wf, b_f, *, tile):
    B, S, D = x.shape
    npair = FOX_HEADS // 2
    tables = _fgate_tables()
    return pl.pallas_call(
        _fgate_kernel,
        grid=(B, S // tile),
        in_specs=[
            pl.BlockSpec((1, tile, D), lambda b, s: (b, s, 0)),
            pl.BlockSpec((1, D), lambda b, s: (0, 0)),
            pl.BlockSpec((D, FOX_HEADS), lambda b, s: (0, 0)),
            pl.BlockSpec((1, FOX_HEADS), lambda b, s: (0, 0)),
        ] + [pl.BlockSpec(t.shape, lambda b, s, n=t.ndim: (0,) * n) for t in tables],
        out_specs=[
            pl.BlockSpec((1, npair, tile, LANES), lambda b, s: (b, 0, s, 0)),
            pl.BlockSpec((1, npair, 2, QF_ROWS, tile), lambda b, s: (b, 0, 0, 0, s)),
        ],
        out_shape=[
            jax.ShapeDtypeStruct((B, npair, S, LANES), BF16),
            jax.ShapeDtypeStruct((B, npair, 2, QF_ROWS, S), BF16),
        ],
        scratch_shapes=[pltpu.VMEM((1, FOX_HEADS), F32)],
        compiler_params=_cp(("parallel", "arbitrary")),
        name="fgate",
    )(x, g.reshape(1, D), wf, b_f.reshape(1, FOX_HEADS), *tables)


def _proj_kernel(*refs, tn, ntile, split_cols, add_cols, n_t):
    it = iter(refs)
    x_ref, g_ref, w_ref = next(it), next(it), next(it)
    add_ref = next(it) if add_cols else None
    wt_refs = [next(it) for _ in range(n_t)]
    u_ref = next(it)
    e_refs = [next(it) for _ in split_cols]
    ut_refs = [next(it) for _ in range(n_t)]
    h_sc = next(it)
    j = pl.program_id(1)

    @pl.when(j == 0)
    def _():
        h = _rms(x_ref[...], g_ref[...]).astype(BF16)
        h_sc[...] = h
        for wt_ref, ut_ref in zip(wt_refs, ut_refs):
            ut_ref[0] = _dot_nt(wt_ref[...], h).astype(ut_ref.dtype)

    acc = _dot(h_sc[...], w_ref[...])
    u_ref[...] = acc.astype(u_ref.dtype)

    def extras(t):
        lo = t * tn
        adds = [c - lo for c in add_cols if lo <= c < lo + tn]
        splits = [(e, c - lo) for e, c in zip(e_refs, split_cols) if lo <= c < lo + tn]

        def body():
            for c in adds:
                cols = slice(c, c + add_ref.shape[1])
                u_ref[:, cols] = (acc[:, cols] + add_ref[...].astype(F32)).astype(u_ref.dtype)
            for e_ref, c in splits:
                e_ref[...] = acc[:, c:c + e_ref.shape[1]].astype(e_ref.dtype)
        return body if adds or splits else None

    for t in range(ntile):
        body = extras(t)
        if body is not None:
            pl.when(j == t)(body)


def _proj(x2, g, w, *, seq, split_cols=(), split_width=0, addend=None, add_cols=(), w_t=(),
          t_dtypes=(), tm=PROJ_TM, tn=PROJ_TN):
    N, D = x2.shape
    W = w.shape[1]
    nbs = seq // tm
    in_specs = [
        pl.BlockSpec((tm, D), lambda i, j: (i, 0)),
        pl.BlockSpec((1, D), lambda i, j: (0, 0)),
        pl.BlockSpec((D, tn), lambda i, j: (0, j)),
    ]
    args = [x2, g.reshape(1, D), w]
    if add_cols:
        in_specs.append(pl.BlockSpec((tm, addend.shape[1]), lambda i, j: (i % nbs, 0)))
        args.append(addend)
    out_shape = [jax.ShapeDtypeStruct((N, W), BF16)]
    out_specs = [pl.BlockSpec((tm, tn), lambda i, j: (i, j))]
    for _ in split_cols:
        out_shape.append(jax.ShapeDtypeStruct((N, split_width), BF16))
        out_specs.append(pl.BlockSpec((tm, split_width), lambda i, j: (i, 0)))
    for wt, dt in zip(w_t, t_dtypes):
        rows = wt.shape[0]
        in_specs.append(pl.BlockSpec((rows, D), lambda i, j: (0, 0)))
        args.append(wt)
        out_shape.append(jax.ShapeDtypeStruct((N // seq, rows, seq), dt))
        out_specs.append(pl.BlockSpec((1, rows, tm), lambda i, j: (i // nbs, 0, i % nbs)))
    return pl.pallas_call(
        functools.partial(_proj_kernel, tn=tn, ntile=W // tn, split_cols=tuple(split_cols),
                          add_cols=tuple(add_cols), n_t=len(w_t)),
        grid=(N // tm, W // tn),
        in_specs=in_specs,
        out_specs=out_specs,
        out_shape=out_shape,
        scratch_shapes=[pltpu.VMEM((tm, D), BF16)],
        compiler_params=_cp(("parallel", "arbitrary")),
        name="proj",
    )(*args)


def _flash_ops(bm_sc, s_bufs, mx_sc, acc_sc, nh, tq, tk, k_tile, v_rows):
    acc_sc[...] = jnp.zeros_like(acc_sc)
    ones = jnp.ones((ACC_ROWS - HEAD_DIM, tk), BF16)

    def qk_head(tile, slot, h, qc=None, nk=tk):
        c0, c1 = qc or (0, tq)
        cols = slice(h * tq + c0, h * tq + c1)
        s = _dot(k_tile(pl.multiple_of(tile * tk, tk))[:nk], bm_sc[:, cols])
        s_bufs[slot][:nk, cols] = s
        if mx_sc is not None:
            mx_sc[slot, :, cols] = jnp.max(s, axis=0, keepdims=True)

    def soft_head(tile, slot, m_old, h, valid, bias, qc, kr):
        c0, c1 = qc or (0, tq)
        r0, r1 = kr
        cols = slice(h * tq + c0, h * tq + c1)
        buf = s_bufs[slot]
        k0 = pl.multiple_of(tile * tk, tk)
        m_sub = m_old[:, c0:c1]
        if valid is not None or bias is not None:
            s = (buf[r0:r1, cols] + bias[r0:r1, c0:c1] if valid is None
                 else jnp.where(valid[r0:r1, c0:c1], buf[r0:r1, cols], NEG))
            m_new = jnp.maximum(m_sub, jnp.max(s, axis=0, keepdims=True))
            p = jnp.exp2(s - m_new).astype(BF16)
        else:
            m_new = jnp.maximum(m_sub, mx_sc[slot, :, cols])
            p = jnp.exp2(buf[r0:r1, cols] - m_new).astype(BF16)
        alpha = jnp.exp2(m_sub - m_new)
        lhs = jnp.concatenate([v_rows(h, k0)[:, r0:r1], ones[:, :r1 - r0]], axis=0)
        acc_sc[h, :, c0:c1] = alpha * acc_sc[h, :, c0:c1] + _dot(lhs, p)
        parts = [m_new]
        if c0 > 0:
            parts.insert(0, m_old[:, :c0])
        if c1 < tq:
            parts.append(m_old[:, c1:])
        return jnp.concatenate(parts, axis=1) if len(parts) > 1 else m_new

    def qk_all(tile, slot, qc=None, nk=tk):
        for h in range(nh):
            qk_head(tile, slot, h, qc, nk)

    def step(cur, slot, ms, nxt=None, valid=None, bias=None, before=None, qc=None, nxt_qc=None,
             nk=tk, parts=None):
        out = []
        for h in range(nh):
            if nxt is not None:
                qk_head(nxt, 1 - slot, h, nxt_qc)
            if before is not None:
                before(h)
            m = ms[h]
            for kr, qcp in parts or [((0, nk), qc)]:
                m = soft_head(cur, slot, m, h, valid, bias, qcp, kr)
            out.append(m)
        return tuple(out)

    def finish():
        outs = []
        for h in range(nh):
            a = acc_sc[h]
            outs.append(a[:HEAD_DIM] * (1.0 / a[HEAD_DIM:HEAD_DIM + 1]))
        return outs

    return qk_head, qk_all, step, finish


def _flash_t(ops, nh, tq, tk, n_un, tile_of, diag, diag_bias, first_done=False, prologue=None):
    _, qk_all, step, finish = ops
    assert tq == tk
    half = tk // 2
    diag_parts = [((0, half), None), ((half, tk), (half, tq))]
    col_halves = [((0, tk), (0, tq // 2)), ((0, tk), (tq // 2, tq))]

    def pair(i, ms):
        t_a, t_b, t_c = tile_of(2 * i), tile_of(2 * i + 1), tile_of(2 * i + 2)
        return step(t_b, 1, step(t_a, 0, ms, nxt=t_b, parts=col_halves), nxt=t_c,
                    parts=col_halves)

    def odd_tail(ms):
        return step(diag, 1, step(tile_of(n_un - 1), 0, ms, nxt=diag), bias=diag_bias[...],
                    parts=diag_parts)

    def even_tail(ms):
        return step(diag, 0, ms, bias=diag_bias[...], parts=diag_parts)

    if not first_done:
        qk_all(tile_of(0), 0)
    ms = (jnp.full((1, tq), NEG, F32),) * nh
    if prologue is not None:
        ms = prologue(ms)
    ms = lax.fori_loop(0, n_un // 2, pair, ms)
    lax.cond(n_un % 2 == 1, odd_tail, even_tail, ms)
    return finish()


def _causal_bias(tk, tq):
    assert tk == tq
    return jnp.asarray(np.where(np.arange(tk)[:, None] <= np.arange(tq)[None, :], 0.0, NEG), F32)


def _flash_scratch(nh, tq, tk, kdim, col_max=True):
    bufs = [pltpu.VMEM((kdim, nh * tq), BF16), pltpu.VMEM((tk, nh * tq), F32),
            pltpu.VMEM((tk, nh * tq), F32)]
    if col_max:
        bufs.append(pltpu.VMEM((2, 1, nh * tq), F32))
    return bufs + [pltpu.VMEM((nh, ACC_ROWS, tq), F32)]


def _fox_kernel(qt_ref, qf_ref, k_ref, kf_ref, vt_ref, cb_ref, o_ref, bm_sc, s0_sc, s1_sc, mx_sc,
                acc_sc):
    qi = pl.program_id(2)
    tq, tk = FOX_TQ, FOX_TK
    bm_sc[...] = jnp.zeros_like(bm_sc)
    for h in range(2):
        rows = slice(h * HEAD_DIM, (h + 1) * HEAD_DIM)
        bm_sc[rows, h * tq:(h + 1) * tq] = qt_ref[0, rows, :]
        bm_sc[LANES:LANES + QF_ROWS, h * tq:(h + 1) * tq] = qf_ref[0, 0, h]

    def k_tile(k0):
        return jnp.concatenate([k_ref[0, pl.ds(k0, tk), :], kf_ref[0, 0, pl.ds(k0, tk), :]], axis=1)

    def v_rows(h, k0):
        return vt_ref[0, h * HEAD_DIM:(h + 1) * HEAD_DIM, pl.ds(k0, tk)]

    diag = (qi * tq) // tk
    ops = _flash_ops(bm_sc, (s0_sc, s1_sc), mx_sc, acc_sc, 2, tq, tk, k_tile, v_rows)
    outs = _flash_t(ops, 2, tq, tk, diag, lambda i: i, diag, cb_ref)
    o_ref[0] = jnp.concatenate(outs, axis=0).T.astype(o_ref.dtype)


def _fox(u3, ut, kfeat, qfeat, *, q_row, k_col, v_row):
    B, S, _ = u3.shape
    npair = FOX_HEADS // 2
    return pl.pallas_call(
        _fox_kernel,
        grid=(B, npair, S // FOX_TQ),
        in_specs=[
            pl.BlockSpec((1, LANES, FOX_TQ), lambda b, j, i: (b, q_row + j, i)),
            pl.BlockSpec((1, 1, 2, QF_ROWS, FOX_TQ), lambda b, j, i: (b, j, 0, 0, i)),
            pl.BlockSpec((1, S, LANES), lambda b, j, i: (b, 0, k_col + j)),
            pl.BlockSpec((1, 1, S, LANES), lambda b, j, i: (b, j, 0, 0)),
            pl.BlockSpec((1, LANES, S), lambda b, j, i: (b, v_row + j, 0)),
            pl.BlockSpec((FOX_TK, FOX_TQ), lambda b, j, i: (0, 0)),
        ],
        out_specs=pl.BlockSpec((1, FOX_TQ, LANES), lambda b, j, i: (b, i, j)),
        out_shape=jax.ShapeDtypeStruct((B, S, FOX_WIDTH), BF16),
        scratch_shapes=_flash_scratch(2, FOX_TQ, FOX_TK, 2 * LANES),
        compiler_params=_cp(("parallel", "parallel", "arbitrary")),
        name="fox",
    )(ut, qfeat, u3, kfeat, ut, _causal_bias(FOX_TK, FOX_TQ))


def _ret_kernel(q_ref, k_ref, v_ref, inner_ref, cross_ref, kdec_ref, cd_ref, bd_ref, gn_ref,
                o_ref, state_sc):
    @pl.when(pl.program_id(1) == 0)
    def _():
        state_sc[...] = jnp.zeros_like(state_sc)

    low = _low_half((RET_CHUNK, LANES), 1)
    inv = 1.0 / HEAD_DIM
    for c, j in [(c, j) for c in range(RET_STEP) for j in range(RET_HEADS // 2)]:
        rows = slice(c * RET_CHUNK, (c + 1) * RET_CHUNK)
        cols = slice(j * LANES, (j + 1) * LANES)
        q, k, v = q_ref[0, rows, cols], k_ref[0, rows, cols], v_ref[0, rows, cols]
        zero = jnp.zeros_like(q)
        qa, qb = jnp.where(low, q, zero), jnp.where(low, zero, q)
        pa = (_dot_nt(qa, k) * inner_ref[j, 0]).astype(BF16)
        pb = (_dot_nt(qb, k) * inner_ref[j, 1]).astype(BF16)
        o_in = jnp.where(low, _dot(pa, v), _dot(pb, v))
        state = state_sc[j]
        o = o_in + _dot(q, state.astype(BF16)) * cross_ref[j]
        kd = (k.astype(F32) * kdec_ref[j]).astype(BF16)
        state_sc[j] = state * cd_ref[j] + _dot_tn(kd, v) * bd_ref[...]
        sa = jnp.sum(jnp.where(low, o, 0.0), axis=-1, keepdims=True)
        st = jnp.sum(o, axis=-1, keepdims=True)
        mu = jnp.where(low, sa, st - sa) * inv
        d = o - mu
        d2 = d * d
        va = jnp.sum(jnp.where(low, d2, 0.0), axis=-1, keepdims=True)
        vt = jnp.sum(d2, axis=-1, keepdims=True)
        var = jnp.where(low, va, vt - va) * inv
        o_ref[0, rows, cols] = (d * lax.rsqrt(var + GN_EPS) * gn_ref[:, cols]).astype(o_ref.dtype)


def _ret_constants():
    lg = np.log(1.0 - 2.0 ** (-5.0 - np.arange(RET_HEADS)))
    i = np.arange(RET_CHUNK)
    diff = i[:, None] - i[None, :]
    inner = np.where(diff[None] >= 0, np.exp(lg[:, None, None] * np.maximum(diff, 0)[None]), 0.0)
    cross = np.exp(lg[:, None] * (i[None, :] + 1))
    kdec = np.exp(lg[:, None] * (RET_CHUNK - 1 - i)[None, :])
    cdec = np.exp(lg * RET_CHUNK)
    npair = RET_HEADS // 2
    inner = inner.reshape(npair, 2, RET_CHUNK, RET_CHUNK)

    def lanes(a):
        a = a.reshape(npair, 2, RET_CHUNK)
        return np.repeat(a.transpose(0, 2, 1), HEAD_DIM, axis=2)

    bd = np.kron(np.eye(2), np.ones((HEAD_DIM, HEAD_DIM)))
    cd = np.repeat(cdec.reshape(npair, 2), HEAD_DIM, axis=1)[:, :, None] * bd[None]
    f = lambda a: jnp.asarray(a, F32)
    return f(inner), f(lanes(cross)), f(lanes(kdec)), f(cd), f(bd)


def _retention(u3, gn_g, *, q_col, k_col, v_col):
    B, S, _ = u3.shape
    C = RET_CHUNK
    rows = RET_STEP * C
    npair = RET_HEADS // 2
    inner, cross, kdec, cd, bd = _ret_constants()
    full = lambda shape: pl.BlockSpec(shape, lambda b, i: (0,) * len(shape))
    return pl.pallas_call(
        _ret_kernel,
        grid=(B, S // rows),
        in_specs=[
            pl.BlockSpec((1, rows, RET_WIDTH), lambda b, i: (b, i, q_col)),
            pl.BlockSpec((1, rows, RET_WIDTH), lambda b, i: (b, i, k_col)),
            pl.BlockSpec((1, rows, RET_WIDTH), lambda b, i: (b, i, v_col)),
            full((npair, 2, C, C)), full((npair, C, LANES)), full((npair, C, LANES)),
            full((npair, LANES, LANES)), full((LANES, LANES)), full((1, RET_WIDTH)),
        ],
        out_specs=pl.BlockSpec((1, rows, RET_WIDTH), lambda b, i: (b, i, 0)),
        out_shape=jax.ShapeDtypeStruct((B, S, RET_WIDTH), BF16),
        scratch_shapes=[pltpu.VMEM((npair, LANES, LANES), F32)],
        compiler_params=_cp(("parallel", "arbitrary")),
        name="retention",
    )(u3, u3, u3, inner, cross, kdec, cd, bd, gn_g.reshape(1, RET_WIDTH))


def _out0_kernel(of_ref, or_ref, z_ref, x_ref, w_ref, o_ref):
    z = _silu(z_ref[...].astype(F32))
    ya = (of_ref[...].astype(F32) * z[:, :FOX_WIDTH]).astype(BF16)
    yb = (or_ref[...].astype(F32) * z[:, FOX_WIDTH:]).astype(BF16)
    o_ref[...] = x_ref[...] + _dot(ya, w_ref[:FOX_WIDTH, :]) + _dot(yb, w_ref[FOX_WIDTH:, :])


def _out0(o_f, o_r, u, x2, w_out, *, tm=OUT_TM):
    N, D = x2.shape
    return pl.pallas_call(
        _out0_kernel,
        grid=(N // tm,),
        in_specs=[
            pl.BlockSpec((tm, FOX_WIDTH), lambda i: (i, 0)),
            pl.BlockSpec((tm, RET_WIDTH), lambda i: (i, 0)),
            pl.BlockSpec((tm, D), lambda i: (i, 0)),
            pl.BlockSpec((tm, D), lambda i: (i, 0)),
            pl.BlockSpec((D, D), lambda i: (0, 0)),
        ],
        out_specs=pl.BlockSpec((tm, D), lambda i: (i, 0)),
        out_shape=jax.ShapeDtypeStruct((N, D), F32),
        compiler_params=_cp(("parallel",)),
        name="out0",
    )(o_f, o_r, u, x2, w_out)


def _out1_kernel(oc_ref, os_ref, z_ref, x_ref, w_ref, g_ref, o_ref):
    z = _silu(z_ref[...].astype(F32))
    y = ((oc_ref[...].astype(F32) + os_ref[...].astype(F32)) * z).astype(BF16)
    o_ref[...] = _rms(x_ref[...] + _dot(y, w_ref[...]), g_ref[...])


def _out1(o_c, o_s, u, x2, w_out, final_g, *, tm=OUT_TM):
    N, D = x2.shape
    row = pl.BlockSpec((tm, D), lambda i: (i, 0))
    return pl.pallas_call(
        _out1_kernel,
        grid=(N // tm,),
        in_specs=[row, row, row, row,
                  pl.BlockSpec((D, D), lambda i: (0, 0)),
                  pl.BlockSpec((1, D), lambda i: (0, 0))],
        out_specs=row,
        out_shape=jax.ShapeDtypeStruct((N, D), F32),
        compiler_params=_cp(("parallel",)),
        name="out1",
    )(o_c, o_s, u, x2, w_out, final_g.reshape(1, D))


def _compress_kernel(x_ref, pea_ref, peb_ref, wa_ref, wb_ref, w2_ref, *rest, transposed):
    x = x_ref[0].astype(F32)
    a = _dot((x + pea_ref[...]).astype(BF16), wa_ref[0])
    b = _dot((x + peb_ref[...]).astype(BF16), wb_ref[0])
    nseg = x.shape[0]
    pre = a + pltpu.roll(b, nseg - 1, 0)
    hid = _silu(pre).astype(BF16)
    if transposed:
        o_ref, = rest
        o_ref[0, 0] = _dot_nt(w2_ref[...], hid).astype(o_ref.dtype)
    else:
        feat_ref, o_ref = rest
        o_ref[0, 0] = (_dot(hid, w2_ref[...]) + feat_ref[...].astype(F32)).astype(o_ref.dtype)


def _compress(a3, pe, w1, w2, *, transposed):
    B, nseg, wid = a3.shape
    half = CMP_STRIDE * HEAD_DIM
    eye = jnp.eye(NSA_GROUPS, dtype=w1.dtype)

    def big(wh):
        w4 = wh.reshape(CMP_STRIDE, 1, HEAD_DIM, CMP_HIDDEN)
        sel = eye[:, None, :, None, None]
        return (sel * w4[None]).reshape(NSA_GROUPS, wid, CMP_HIDDEN).astype(BF16)

    def pe_big(p):
        return jnp.broadcast_to(p[:, None, :], (CMP_STRIDE, NSA_GROUPS, HEAD_DIM)).reshape(1, wid)

    args = [a3, pe_big(pe[:CMP_STRIDE]), pe_big(pe[CMP_STRIDE:]), big(w1[:half]), big(w1[half:])]
    in_specs = [
        pl.BlockSpec((1, nseg, wid), lambda b, g: (b, 0, 0)),
        pl.BlockSpec((1, wid), lambda b, g: (0, 0)),
        pl.BlockSpec((1, wid), lambda b, g: (0, 0)),
        pl.BlockSpec((1, wid, CMP_HIDDEN), lambda b, g: (g, 0, 0)),
        pl.BlockSpec((1, wid, CMP_HIDDEN), lambda b, g: (g, 0, 0)),
    ]
    if transposed:
        w2d = w2.T.astype(BF16)
        oshape, oblock = (B, NSA_GROUPS, HEAD_DIM, nseg), (1, 1, HEAD_DIM, nseg)
        args.append(w2d)
        in_specs.append(pl.BlockSpec(w2d.shape, lambda b, g: (0, 0)))
    else:
        w2d = jnp.pad(w2, ((0, 0), (0, LANES - HEAD_DIM))).astype(BF16)
        oshape, oblock = (B, NSA_GROUPS, nseg, LANES), (1, 1, nseg, LANES)
        feat = _pos_features(jnp.arange(nseg, dtype=jnp.int32) * CMP_STRIDE + (CMP_BLOCK - 1), LANES)
        args += [w2d, feat]
        in_specs += [pl.BlockSpec(w2d.shape, lambda b, g: (0, 0)),
                     pl.BlockSpec((nseg, LANES), lambda b, g: (0, 0))]
    return pl.pallas_call(
        functools.partial(_compress_kernel, transposed=transposed),
        grid=(B, NSA_GROUPS),
        in_specs=in_specs,
        out_specs=pl.BlockSpec(oblock, lambda b, g: (b, g, 0, 0)),
        out_shape=jax.ShapeDtypeStruct(oshape, BF16),
        compiler_params=_cp(("parallel", "parallel")),
        name="compress",
    )(*args)


def _slope_table():
    s = np.asarray(2.0 ** (-8.0 * (np.arange(NSA_HEADS) + 1) / NSA_HEADS), np.float32)
    sl = np.asarray(s.astype(np.float64) * LOG2E, np.float32)
    p1, p2, p3 = _np_pieces(sl)
    tab = np.zeros((NSA_HEADS, QF_ROWS), np.float32)
    for k, p in enumerate((p1, p1, p2, p2, p3, p3)):
        tab[:, k] = p
    tab[:, 6] = sl
    tab = np.broadcast_to(tab.reshape(NSA_GROUPS, NSA_HPG * QF_ROWS, 1),
                          (NSA_GROUPS, NSA_HPG * QF_ROWS, NSA_TQ))
    return jnp.asarray(tab)


def _pos_features(pos, width):
    pos = pos[:, None]
    lane = jnp.arange(width, dtype=jnp.int32)[None, :] % LANES
    hi = ((pos // SLC_BLOCK) * SLC_BLOCK).astype(F32)
    lo = (pos % SLC_BLOCK).astype(F32)
    k = lane - FEAT0
    f = jnp.where((k >= 0) & (k < 6), jnp.where(k % 2 == 0, hi, lo), 0.0)
    f = jnp.where((k >= 6) & (k < 9), 1.0, f)
    return f.astype(BF16)


def _nsa_queries(qt_ref, tab_ref, t0):
    tq = qt_ref.shape[2]
    r = lax.broadcasted_iota(jnp.int32, (QF_ROWS, tq), 0)
    t = (t0 + lax.broadcasted_iota(jnp.int32, (1, tq), 1)).astype(F32)
    zeros = jnp.zeros((LANES - HEAD_DIM - QF_ROWS, tq), BF16)
    out = []
    for i in range(NSA_HPG):
        tile = tab_ref[0, i * QF_ROWS:(i + 1) * QF_ROWS, :]
        a1, a2, a3 = _pieces(-(tile[6:7, :] * t))
        feat = jnp.where(r == 6, a1, jnp.where(r == 7, a2, jnp.where(r == 8, a3,
                                                                     jnp.where(r < 6, tile, 0.0))))
        out.append(jnp.concatenate([qt_ref[0, i * HEAD_DIM:(i + 1) * HEAD_DIM, :],
                                    feat.astype(BF16), zeros], axis=0))
    return out


def _gates_t(gt_ref, bg_ref, branch):
    gl = gt_ref[0] + bg_ref[...]
    return [_sigmoid(gl[N_BRANCH * i + branch:N_BRANCH * i + branch + 1, :]) for i in range(NSA_HPG)]


def _gated(outs_t, gates):
    return [o * gt for o, gt in zip(outs_t, gates)]


def _store_heads(o_ref, g):
    o_ref[0, :, :LANES] = jnp.concatenate(g[:2], axis=0).T.astype(o_ref.dtype)
    o_ref[0, :, LANES:] = jnp.concatenate(g[2:], axis=0).T.astype(o_ref.dtype)


def _nsa_specs(q_row):
    return dict(
        q=pl.BlockSpec((1, NSA_HPG * HEAD_DIM, NSA_TQ), lambda b, g, i: (b, q_row + g, i)),
        tab=pl.BlockSpec((1, NSA_HPG * QF_ROWS, NSA_TQ), lambda b, g, i: (g, 0, 0)),
        gt=pl.BlockSpec((1, GATE_ROWS, NSA_TQ), lambda b, g, i: (b, g, i)),
        bg=pl.BlockSpec((GATE_ROWS, 1), lambda b, g, i: (g, 0)),
        out=pl.BlockSpec((1, NSA_TQ, NSA_HPG * HEAD_DIM), lambda b, g, i: (b, i, g)),
    )


def _argmax_first(v, idx):
    n = v.shape[0]
    slabs = [(v[r:r + 8], idx[r:r + 8]) for r in range(0, n, 8)]
    while len(slabs) > 1:
        nxt = []
        for (va, ia), (vb, ib) in zip(slabs[0::2], slabs[1::2]):
            right = vb > va
            nxt.append((jnp.where(right, vb, va), jnp.where(right, ib, ia)))
        if len(slabs) % 2:
            nxt.append(slabs[-1])
        slabs = nxt
    v, idx = slabs[0]
    mx = jnp.max(v, axis=0, keepdims=True)
    first = jnp.min(jnp.where(v == mx, idx, float(n)), axis=0, keepdims=True)
    return mx, first


def _cmp_body(nc, t0, qh, kc_ref, vct_ref, mt_ref, grp_ref, gates, o_ref, sel_ref, flag_ref):
    tq = NSA_TQ
    rows = nc * CMP_CHUNK
    full = max(rows - CMP_CHUNK - 8, 0)
    nseg = kc_ref.shape[2]
    kc = kc_ref[0, 0, :rows, :]
    vct = vct_ref[0, 0, :, :rows]
    t = t0 + lax.broadcasted_iota(jnp.int32, (1, tq), 1)
    cidx = full + lax.broadcasted_iota(jnp.int32, (rows - full, 1), 0)
    valid = (cidx * CMP_STRIDE + (CMP_BLOCK - 1) <= t) & (cidx < nseg - 1)
    psum = jnp.zeros((rows, tq), F32)
    ps = []
    s_all = _dot(kc, jnp.concatenate(qh, axis=1))
    for i in range(NSA_HPG):
        s = s_all[:, i * tq:(i + 1) * tq]
        s_last = jnp.where(valid, s[full:], NEG)
        m = jnp.max(s_last, axis=0, keepdims=True)
        if nc > 1:
            m = jnp.maximum(m, jnp.max(s[:full], axis=0, keepdims=True))
        e = jnp.where(valid, jnp.exp2(s_last - m), 0.0)
        if nc > 1:
            e = jnp.concatenate([jnp.exp2(s[:full] - m), e], axis=0)
        l = jnp.sum(e, axis=0, keepdims=True)
        p = e * jnp.where(l > 0.0, 1.0 / l, 0.0)
        psum = psum + p
        ps.append(p.astype(BF16))
    o_all = _dot(vct, jnp.concatenate(ps, axis=1))
    _store_heads(o_ref, _gated([o_all[:, i * tq:(i + 1) * tq] for i in range(NSA_HPG)], gates))
    ns = rows * CMP_STRIDE // SLC_BLOCK
    mt = mt_ref[:ns, :rows]
    imp = sum(_dot(mt, p.astype(BF16)) for p in _pieces(psum))
    blk = lax.broadcasted_iota(jnp.int32, (ns, 1), 0)
    cur = t // SLC_BLOCK
    bvalid = blk * SLC_BLOCK <= t
    forced = (blk == 0) | (blk == cur) | (blk == cur - 1)
    assert 2 * NSA_HPG < FORCE_BONUS
    score = jnp.where(forced, -jnp.inf, jnp.where(bvalid, imp, NEG))
    blk_f = jnp.broadcast_to(blk.astype(F32), (ns, tq))
    work = score
    for _ in range(SLC_TOPK - N_FORCED):
        mx, first = _argmax_first(work, blk_f)
        work = jnp.where(blk_f == first, -jnp.inf, work)
    picked = (score > mx) | ((score == mx) & (blk_f <= first))
    selneg = jnp.where(bvalid & (forced | picked), 0.0, -MASK_BIG)
    if ns < NS_PAD:
        selneg = jnp.concatenate([selneg, jnp.full((NS_PAD - ns, tq), -MASK_BIG, F32)], axis=0)
    sel_ref[0, 0] = selneg.astype(sel_ref.dtype)
    picked = jnp.where(selneg == 0.0, 1.0, 0.0).astype(BF16)
    used = _dot_nt(jnp.ones((8, tq), BF16), picked)
    used = jnp.where(used > 0.0, 1.0, 0.0).astype(BF16)
    flag_ref[0] = (_dot(used, grp_ref[...])[0:1] > 0.0).astype(jnp.int32)


def _cmp_kernel(q_ref, kc_ref, vct_ref, mt_ref, grp_ref, tab_ref, gt_ref, bg_ref,
                o_ref, sel_ref, flag_ref):
    t0 = pl.program_id(2) * NSA_TQ
    qh = _nsa_queries(q_ref, tab_ref, t0)
    gates = _gates_t(gt_ref, bg_ref, 0)
    nchunk = kc_ref.shape[2] // CMP_CHUNK
    last = t0 // (CMP_CHUNK * CMP_STRIDE)
    for nc in range(1, nchunk + 1):
        pl.when(last == nc - 1)(functools.partial(
            _cmp_body, nc, t0, qh, kc_ref, vct_ref, mt_ref, grp_ref, gates, o_ref, sel_ref, flag_ref))


def _cmp_to_slc_t(nseg, ns):
    c0 = np.arange(nseg)[:, None] * CMP_STRIDE
    s0 = np.arange(ns)[None, :] * SLC_BLOCK
    overlap = np.clip(np.minimum(c0 + CMP_BLOCK, s0 + SLC_BLOCK) - np.maximum(c0, s0), 0, None)
    m = overlap / CMP_STRIDE
    m[nseg - 1] = 0.0
    mt = np.zeros((NS_PAD, nseg))
    mt[:ns] = m.T
    return jnp.asarray(mt, BF16)


def _tile_groups(ntile):
    per = SLC_TK // SLC_BLOCK
    blk = np.arange(NS_PAD)[:, None]
    g = (blk // per) == np.arange(NS_PAD)[None, :]
    g[:, ntile] = ((blk >= SLC_HEAD // SLC_BLOCK) & (blk < per))[:, 0]
    return jnp.asarray(g, BF16)


def _cmp_attention(ut, kcmp, vcmp_t, gt, bg, *, q_row):
    B, _, S = ut.shape
    nseg = kcmp.shape[2]
    nq = S // NSA_TQ
    sp = _nsa_specs(q_row)
    return pl.pallas_call(
        _cmp_kernel,
        grid=(B, NSA_GROUPS, S // NSA_TQ),
        in_specs=[
            sp["q"],
            pl.BlockSpec((1, 1, nseg, LANES), lambda b, g, i: (b, g, 0, 0)),
            pl.BlockSpec((1, 1, HEAD_DIM, nseg), lambda b, g, i: (b, g, 0, 0)),
            pl.BlockSpec((NS_PAD, nseg), lambda b, g, i: (0, 0)),
            pl.BlockSpec((NS_PAD, NS_PAD), lambda b, g, i: (0, 0)),
            sp["tab"], sp["gt"], sp["bg"],
        ],
        out_specs=[sp["out"], pl.BlockSpec((1, 1, NS_PAD, NSA_TQ), lambda b, g, i: (b, g, 0, i)),
                   pl.BlockSpec((1, 1, NS_PAD), lambda b, g, i: ((b * NSA_GROUPS + g) * nq + i, 0, 0))],
        out_shape=[jax.ShapeDtypeStruct((B, S, NSA_WIDTH), BF16),
                   jax.ShapeDtypeStruct((B, NSA_GROUPS, NS_PAD, S), BF16),
                   jax.ShapeDtypeStruct((B * NSA_GROUPS * nq, 1, NS_PAD), jnp.int32)],
        compiler_params=_cp(("parallel", "parallel", "arbitrary")),
        name="cmp_attention",
    )(ut, kcmp, vcmp_t, _cmp_to_slc_t(nseg, S // SLC_BLOCK), _tile_groups(S // SLC_TK), _slope_table(), gt, bg)


def _window_branch(qh, t0, k_ref, vt_ref, bias_ref, bm_sc, s_bufs, acc_sc, last_before):
    tq, tk = NSA_TQ, WIN_TK
    bm_sc[...] = jnp.concatenate(qh, axis=1)
    _, qk_all, step, finish = _flash_ops(
        bm_sc, s_bufs, None, acc_sc, NSA_HPG, tq, tk,
        lambda k0: k_ref[0, pl.ds(k0, tk), :], lambda h, k0: vt_ref[0, :, pl.ds(k0, tk)])
    first = jnp.maximum(t0 - WINDOW, 0) // tk
    ntile = (WINDOW + tq) // tk

    def run(qcs):
        qk_all(first, 0, qcs[0])
        ms = (jnp.full((1, tq), NEG, F32),) * NSA_HPG
        for j in range(ntile):
            last = j + 1 == ntile
            ms = step(first + j, j % 2, ms, nxt=None if last else first + j + 1,
                      bias=bias_ref[0, j * tk:(j + 1) * tk, :],
                      before=last_before if last else None, qc=qcs[j],
                      nxt_qc=None if last else qcs[j + 1])

    lane = lambda v: max(0, min(tq, v)) // LANES * LANES
    qcs = []
    for j in range(ntile):
        c0, c1 = lane(j * tk - WINDOW + 1), tq - lane(tq - (j + 1) * tk)
        qcs.append(None if (c0, c1) == (0, tq) else (c0, c1))
    if all(q is None for q in qcs):
        run(qcs)
    else:
        lax.cond(t0 >= WINDOW, lambda: run(qcs), lambda: run([None] * ntile))
    return finish()


def _slc_win_kernel(fl_ref, q_ref, k_ref, vt_ref, sel_ref, e_ref, kw_ref, vwt_ref, bias_ref,
                    cb_ref, tab_ref, gt_ref, bg_ref, o_ref, bm_sc, s0_sc, s1_sc, mx_sc, acc_sc,
                    wbm_sc, ws0_sc, ws1_sc, wacc_sc, tiles_sm):
    qi = pl.program_id(2)
    tq, tk = NSA_TQ, SLC_TK
    t0 = qi * tq
    diag = t0 // tk
    row = (pl.program_id(0) * NSA_GROUPS + pl.program_id(1)) * pl.num_programs(2) + qi
    ntile = k_ref.shape[1] // tk
    short0 = (fl_ref[row, ntile] == 0) & (diag > 0)
    n_un = jnp.int32(0)
    for j in range(ntile):
        tiles_sm[n_un] = jnp.int32(j)
        keep = (fl_ref[row, j] > 0) & (j < diag)
        if j == 0:
            keep = keep & jnp.logical_not(short0)
        n_un = n_un + keep.astype(jnp.int32)
    tiles_sm[n_un] = diag
    qh = _nsa_queries(q_ref, tab_ref, t0)
    selneg = sel_ref[0, 0]
    bm_sc[...] = jnp.concatenate([jnp.concatenate([q, selneg], axis=0) for q in qh], axis=1)

    def k_tile(k0):
        return jnp.concatenate([k_ref[0, pl.ds(k0, tk), :], e_ref[pl.ds(k0, tk), :]], axis=1)

    def v_rows(h, k0):
        return vt_ref[0, :, pl.ds(k0, tk)]

    ops = _flash_ops(bm_sc, (s0_sc, s1_sc), mx_sc, acc_sc, NSA_HPG, tq, tk, k_tile, v_rows)
    o_win = _gated(_window_branch(qh, t0, kw_ref, vwt_ref, bias_ref, wbm_sc, (ws0_sc, ws1_sc),
                                  wacc_sc, lambda h: ops[0](tiles_sm[0], 0, h)),
                   _gates_t(gt_ref, bg_ref, 2))

    def short_first(ms):
        def run(ms):
            ops[1](0, 1, nk=SLC_HEAD)
            return ops[2](0, 1, ms, nk=SLC_HEAD)
        return lax.cond(short0, run, lambda ms: ms, ms)

    outs = _flash_t(ops, NSA_HPG, tq, tk, n_un, lambda i: tiles_sm[i], diag, cb_ref,
                    first_done=True, prologue=short_first)
    o_slc = _gated(outs, _gates_t(gt_ref, bg_ref, 1))
    _store_heads(o_ref, [a + b for a, b in zip(o_slc, o_win)])


def _win_bias():
    tq, nk = NSA_TQ, WINDOW + NSA_TQ
    out = []
    for p in range(WINDOW // tq + 1):
        t0 = p * tq
        key = max(t0 - WINDOW, 0) + np.arange(nk)[:, None]
        qpos = t0 + np.arange(tq)[None, :]
        out.append(np.where((key <= qpos) & (key > qpos - WINDOW), 0.0, NEG))
    return jnp.asarray(np.stack(out), F32)


def _block_onehot(S):
    e = (np.arange(S)[:, None] // SLC_BLOCK) == np.arange(NS_PAD)[None, :]
    return jnp.asarray(e, BF16)


def _slc_win_attention(u3, ut, selneg, flags, gt, bg, *, q_row, ks_col, vs_row, kw_col, vw_row):
    B, S, _ = u3.shape
    sp = {k: pl.BlockSpec(v.block_shape, lambda b, g, i, fl, f=v.index_map: f(b, g, i))
          for k, v in _nsa_specs(q_row).items()}
    bias = _win_bias()
    npat = bias.shape[0]
    kspec = lambda col: pl.BlockSpec((1, S, LANES), lambda b, g, i, fl: (b, 0, col + g))
    vspec = lambda row: pl.BlockSpec((1, HEAD_DIM, S), lambda b, g, i, fl: (b, row + g, 0))
    grid_spec = pltpu.PrefetchScalarGridSpec(
        num_scalar_prefetch=1,
        grid=(B, NSA_GROUPS, S // NSA_TQ),
        in_specs=[
            sp["q"], kspec(ks_col), vspec(vs_row),
            pl.BlockSpec((1, 1, NS_PAD, NSA_TQ), lambda b, g, i, fl: (b, g, 0, i)),
            pl.BlockSpec((S, NS_PAD), lambda b, g, i, fl: (0, 0)),
            kspec(kw_col), vspec(vw_row),
            pl.BlockSpec((1,) + bias.shape[1:],
                         lambda b, g, i, fl: (jnp.minimum(i, npat - 1), 0, 0)),
            pl.BlockSpec((SLC_TK, NSA_TQ), lambda b, g, i, fl: (0, 0)),
            sp["tab"], sp["gt"], sp["bg"],
        ],
        out_specs=sp["out"],
        scratch_shapes=_flash_scratch(NSA_HPG, NSA_TQ, SLC_TK, 2 * LANES)
        + _flash_scratch(NSA_HPG, NSA_TQ, WIN_TK, LANES, col_max=False)
        + [pltpu.SMEM((S // SLC_TK + 1,), jnp.int32)],
    )
    return pl.pallas_call(
        _slc_win_kernel,
        grid_spec=grid_spec,
        out_shape=jax.ShapeDtypeStruct((B, S, NSA_WIDTH), BF16),
        compiler_params=_cp(("parallel", "parallel", "arbitrary")),
        name="slc_win_attention",
    )(flags, ut, u3, ut, selneg, _block_onehot(S), u3, ut, bias, _causal_bias(SLC_TK, NSA_TQ),
      _slope_table(), gt, bg)


def _aug_groups(w):
    d = w.shape[0]
    w = w.reshape(d, NSA_GROUPS, HEAD_DIM)
    return jnp.pad(w, ((0, 0), (0, 0), (0, LANES - HEAD_DIM))).reshape(d, NSA_GROUPS * LANES)


def _even_layer(x, norm_g, w_in, b_f, gn_g, w_out):
    B, S, D = x.shape
    qscale = HEAD_DIM ** -0.5 * LOG2E
    q_f, k_f, v_f, w_fl, q_r, k_r, v_r, z = jnp.split(
        w_in, np.cumsum([FOX_WIDTH] * 3 + [FOX_HEADS] + [RET_WIDTH] * 3).tolist(), axis=1)
    w = jnp.concatenate([z, k_f, q_r, k_r * HEAD_DIM ** -0.5, v_r], axis=1).astype(BF16)
    w_t = jnp.concatenate([q_f * qscale, v_f], axis=1).T.astype(BF16)
    x2 = x.reshape(B * S, D)
    u, ut = _proj(x2, norm_g, w, seq=S, w_t=[w_t], t_dtypes=[BF16], tn=w.shape[1] // 2)
    u3 = u.reshape(B, S, -1)
    kfeat, qfeat = _fgate(x, norm_g, w_fl, b_f, tile=min(512, S))
    o_f = _fox(u3, ut, kfeat, qfeat, q_row=0, k_col=D // LANES, v_row=FOX_WIDTH // LANES)
    rb = (D + FOX_WIDTH) // RET_WIDTH
    o_r = _retention(u3, gn_g, q_col=rb, k_col=rb + 1, v_col=rb + 2)
    out = _out0(o_f.reshape(B * S, -1), o_r.reshape(B * S, -1), u, x2, w_out.astype(BF16))
    return out.reshape(B, S, D)


def _odd_layer(x, norm_g, w_in, b_gate, pe_k, pe_v, wk1, wk2, wv1, wv2, w_out, final_g):
    B, S, D = x.shape
    assert S // SLC_BLOCK <= NS_PAD
    qscale = HEAD_DIM ** -0.5 * LOG2E
    sizes = [NSA_WIDTH] + [NSA_KV_WIDTH] * 6 + [NSA_HEADS * N_BRANCH]
    q, kc, vc, ks, vs, kw, vw, gl, z = jnp.split(w_in, np.cumsum(sizes).tolist(), axis=1)
    w = jnp.concatenate([z, kc, vc, _aug_groups(ks), _aug_groups(kw)], axis=1).astype(BF16)
    per_group = NSA_HPG * N_BRANCH
    glt = jnp.pad(gl.T.reshape(NSA_GROUPS, per_group, D), ((0, 0), (0, GATE_ROWS - per_group), (0, 0)))
    glt = glt.reshape(NSA_GROUPS * GATE_ROWS, D).astype(BF16)
    bg = jnp.pad(b_gate.reshape(NSA_GROUPS, per_group), ((0, 0), (0, GATE_ROWS - per_group)))
    bg = bg.reshape(NSA_GROUPS * GATE_ROWS, 1)
    w_vt = jnp.concatenate([q * qscale, vs, vw], axis=1).T.astype(BF16)
    x2 = x.reshape(B * S, D)
    kcol = D + 2 * NSA_KV_WIDTH
    kwid = NSA_GROUPS * LANES
    u, kc_a, vc_a, ut, gt = _proj(
        x2, norm_g, w, seq=S, split_cols=(D, D + NSA_KV_WIDTH), split_width=NSA_KV_WIDTH,
        addend=_pos_features(jnp.arange(S, dtype=jnp.int32), kwid), add_cols=(kcol, kcol + kwid),
        w_t=[w_vt, glt], t_dtypes=[BF16, F32], tn=w.shape[1] // 2)
    u3 = u.reshape(B, S, -1)
    nseg = S // CMP_STRIDE
    kcmp = _compress(kc_a.reshape(B, nseg, -1), pe_k, wk1, wk2, transposed=False)
    vcmp_t = _compress(vc_a.reshape(B, nseg, -1), pe_v, wv1, wv2, transposed=True)
    o_c, selneg, flags = _cmp_attention(ut, kcmp, vcmp_t, gt, bg, q_row=0)
    kb = (D + 2 * NSA_KV_WIDTH) // LANES
    vb = NSA_WIDTH // HEAD_DIM
    o_s = _slc_win_attention(u3, ut, selneg, flags[:, 0, :S // SLC_TK + 1], gt, bg, q_row=0, ks_col=kb,
                             vs_row=vb, kw_col=kb + NSA_GROUPS, vw_row=vb + NSA_GROUPS)
    r = lambda a: a.reshape(B * S, -1)
    out = _out1(r(o_c), r(o_s), u, x2, w_out.astype(BF16), final_g)
    return out.reshape(B, S, D)


def kernel(x, even_norm_g, even_w_in, even_b_f, even_gn_g, even_w_out, odd_norm_g, odd_w_in,
           odd_b_gate, odd_pe_k, odd_pe_v, odd_wk1, odd_wk2, odd_wv1, odd_wv2, odd_w_out, final_g):
    x = _even_layer(x, even_norm_g[0], even_w_in[0], even_b_f[0], even_gn_g[0], even_w_out[0])
    return _odd_layer(x, odd_norm_g[0], odd_w_in[0], odd_b_gate[0], odd_pe_k[0], odd_pe_v[0],
                      odd_wk1[0], odd_wk2[0], odd_wv1[0], odd_wv2[0], odd_w_out[0], final_g)
```
